```python
import jax, jax.numpy as jnp
from jax import lax
import numpy as np

D_MODEL = 1024
BATCH = 8
SEQ = 2048
DEPTH = 4

CONF_WIDTH = 512
CONF_KERNEL = 31
SC_WIDTH = 512
SC_KERNEL = 3
MLA_HEADS = 8
QK_NOPE = 64
QK_ROPE = 32
V_HEAD = 64
Q_LORA = 384
KV_LORA = 256
MLA_WIDTH = MLA_HEADS * V_HEAD
ROPE_THETA = 10000.0
Q_BLOCK = 128
N_BRANCH = 3
LN_EPS = 1e-5
RMS_EPS = 1e-6
DEEPNORM_ALPHA = (2 * DEPTH) ** 0.25
DEEPNORM_BETA = (8 * DEPTH) ** -0.25
ADA_SCALE = 0.5

IN_SIZES = (2 * CONF_WIDTH, CONF_WIDTH, 3 * SC_WIDTH, SC_WIDTH, Q_LORA, KV_LORA, QK_ROPE, MLA_WIDTH, N_BRANCH * D_MODEL)
D_IN = 2 * CONF_WIDTH + CONF_WIDTH + 3 * SC_WIDTH + SC_WIDTH + Q_LORA + KV_LORA + QK_ROPE + MLA_WIDTH + N_BRANCH * D_MODEL

kernel_name = "hybrid_conformer_shortconv_mla_deepnorm_adaln"


def layer_norm(x, g, b):
    x32 = x.astype(jnp.float32)
    mu = jnp.mean(x32, axis=-1, keepdims=True)
    var = jnp.mean(jnp.square(x32 - mu), axis=-1, keepdims=True)
    y = (x32 - mu) * lax.rsqrt(var + LN_EPS)
    return (y * g.astype(jnp.float32) + b.astype(jnp.float32)).astype(x.dtype)


def rms_norm(x, g):
    x32 = x.astype(jnp.float32)
    y = x32 * lax.rsqrt(jnp.mean(jnp.square(x32), axis=-1, keepdims=True) + RMS_EPS)
    return (y * g.astype(jnp.float32)).astype(x.dtype)


def causal_depthwise_conv(x, w):
    k_width, ch = w.shape
    return lax.conv_general_dilated(
        x, w.astype(x.dtype)[:, None, :], window_strides=(1,), padding=((k_width - 1, 0),),
        dimension_numbers=("NWC", "WIO", "NWC"), feature_group_count=ch)


def rope_tables(positions):
    inv_freq = ROPE_THETA ** (-jnp.arange(0, QK_ROPE, 2, dtype=jnp.float32) / QK_ROPE)
    ang = positions.astype(jnp.float32)[..., None] * inv_freq
    return jnp.cos(ang), jnp.sin(ang)


def apply_rope(x, cos, sin):
    cos = cos.astype(x.dtype)
    sin = sin.astype(x.dtype)
    x1, x2 = jnp.split(x, 2, axis=-1)
    return jnp.concatenate([x1 * cos - x2 * sin, x2 * cos + x1 * sin], axis=-1)


def causal_block_attention(q, k, v):
    b, s, h, dv = v.shape
    scale = (QK_NOPE + QK_ROPE) ** -0.5
    key_idx = jnp.arange(s)

    def one_block(i):
        start = i * Q_BLOCK
        qb = lax.dynamic_slice_in_dim(q, start, Q_BLOCK, axis=1)
        sc = jnp.einsum("bqhd,bkhd->bhqk", qb, k, preferred_element_type=jnp.float32) * scale
        q_idx = start + jnp.arange(Q_BLOCK)
        mask = key_idx[None, :] <= q_idx[:, None]
        sc = jnp.where(mask[None, None], sc, -jnp.inf)
        p = jax.nn.softmax(sc, axis=-1).astype(v.dtype)
        return jnp.einsum("bhqk,bkhd->bqhd", p, v)

    out = lax.map(one_block, jnp.arange(s // Q_BLOCK))
    return out.transpose(1, 0, 2, 3, 4).reshape(b, s, h, dv)


def hybrid_layer(x, c_act, cos, sin, w_ada, b_ada, w_in, conv_a_w, conv_a_b, ln_a_g, ln_a_b, w_a_out,
                 conv_b_w, w_b_out, q_norm_g, kv_norm_g, w_uq, w_ukv, w_c_out, w_o, ln_g, ln_b):
    b, s, _ = x.shape
    ada = c_act @ w_ada + b_ada
    shift, scale, gate = jnp.split(ada, 3, axis=-1)
    u = x * (1.0 + scale[:, None, :]) + shift[:, None, :]

    proj = jnp.einsum("bsd,dp->bsp", u, w_in)
    split_pts = [int(v) for v in np.cumsum(IN_SIZES)[:-1]]
    a_in, a_gate, b_in, b_gate, q_lat, kv_lat, k_rope, c_gate, merge_logits = jnp.split(proj, split_pts, axis=-1)

    a1, a2 = jnp.split(a_in, 2, axis=-1)
    a = a1 * jax.nn.sigmoid(a2)
    a = causal_depthwise_conv(a, conv_a_w) + conv_a_b
    a = jax.nn.silu(layer_norm(a, ln_a_g, ln_a_b))
    y_a = jnp.einsum("bsc,cd->bsd", a * jax.nn.silu(a_gate), w_a_out)

    xb, gb, gc = jnp.split(b_in, 3, axis=-1)
    yb = gb * causal_depthwise_conv(gc * xb, conv_b_w)
    y_b = jnp.einsum("bsc,cd->bsd", yb * jax.nn.silu(b_gate), w_b_out)

    q = jnp.einsum("bsr,rk->bsk", rms_norm(q_lat, q_norm_g), w_uq).reshape(b, s, MLA_HEADS, QK_NOPE + QK_ROPE)
    q_nope, q_pe = jnp.split(q, [QK_NOPE], axis=-1)
    q_pe = apply_rope(q_pe, cos[:, :, None, :], sin[:, :, None, :])
    kv = jnp.einsum("bsr,rk->bsk", rms_norm(kv_lat, kv_norm_g), w_ukv).reshape(b, s, MLA_HEADS, QK_NOPE + V_HEAD)
    k_nope, v = jnp.split(kv, [QK_NOPE], axis=-1)
    k_pe = apply_rope(k_rope, cos, sin)[:, :, None, :]
    qf = jnp.concatenate([q_nope, q_pe], axis=-1)
    kf = jnp.concatenate([k_nope, jnp.broadcast_to(k_pe, (b, s, MLA_HEADS, QK_ROPE))], axis=-1)
    o = causal_block_attention(qf, kf, v).reshape(b, s, MLA_WIDTH)
    y_c = jnp.einsum("bsc,cd->bsd", o * jax.nn.silu(c_gate), w_c_out)

    g_a, g_b, g_c = jnp.split(jax.nn.sigmoid(merge_logits), N_BRANCH, axis=-1)
    m = g_a * y_a + g_b * y_b + g_c * y_c
    out = jnp.einsum("bsd,de->bse", m, w_o)

    return layer_norm(DEEPNORM_ALPHA * x + gate[:, None, :] * out, ln_g, ln_b)


def _fwd_setup_inputs(seed: int = 0) -> dict:
    key = jax.random.key(seed)
    ks = jax.random.split(key, 24)
    L, D = DEPTH, D_MODEL

    def nrm(k, shape, scale):
        return jax.random.normal(k, shape, dtype=jnp.float32) * scale

    x = nrm(ks[0], (BATCH, SEQ, D), 1.0)
    c = nrm(ks[1], (BATCH, D), 1.0)
    offsets = jax.random.randint(ks[2], (BATCH, 1), 0, 1024, dtype=jnp.int32)
    positions = (jnp.arange(SEQ, dtype=jnp.int32)[None, :] + offsets).astype(jnp.int32)
    return {
        "x": x,
        "c": c,
        "positions": positions,
        "w_ada": nrm(ks[3], (L, D, 3 * D), ADA_SCALE * D ** -0.5),
        "b_ada": nrm(ks[4], (L, 3 * D), 0.02),
        "w_in": nrm(ks[5], (L, D, D_IN), D ** -0.5),
        "conv_a_w": nrm(ks[6], (L, CONF_KERNEL, CONF_WIDTH), CONF_KERNEL ** -0.5),
        "conv_a_b": nrm(ks[7], (L, CONF_WIDTH), 0.02),
        "ln_a_g": 1.0 + nrm(ks[8], (L, CONF_WIDTH), 0.02),
        "ln_a_b": nrm(ks[9], (L, CONF_WIDTH), 0.02),
        "w_a_out": nrm(ks[10], (L, CONF_WIDTH, D), DEEPNORM_BETA * CONF_WIDTH ** -0.5),
        "conv_b_w": nrm(ks[11], (L, SC_KERNEL, SC_WIDTH), SC_KERNEL ** -0.5),
        "w_b_out": nrm(ks[12], (L, SC_WIDTH, D), DEEPNORM_BETA * SC_WIDTH ** -0.5),
        "q_norm_g": 1.0 + nrm(ks[13], (L, Q_LORA), 0.02),
        "kv_norm_g": 1.0 + nrm(ks[14], (L, KV_LORA), 0.02),
        "w_uq": nrm(ks[15], (L, Q_LORA, MLA_HEADS * (QK_NOPE + QK_ROPE)), Q_LORA ** -0.5),
        "w_ukv": nrm(ks[16], (L, KV_LORA, MLA_HEADS * (QK_NOPE + V_HEAD)), KV_LORA ** -0.5),
        "w_c_out": nrm(ks[17], (L, MLA_WIDTH, D), DEEPNORM_BETA * MLA_WIDTH ** -0.5),
        "w_o": nrm(ks[18], (L, D, D), DEEPNORM_BETA * D ** -0.5),
        "ln_g": 1.0 + nrm(ks[19], (L, D), 0.02),
        "ln_b": nrm(ks[20], (L, D), 0.02),
    }


def _fwd_reference(x, c, positions, w_ada, b_ada, w_in, conv_a_w, conv_a_b, ln_a_g, ln_a_b, w_a_out,
              conv_b_w, w_b_out, q_norm_g, kv_norm_g, w_uq, w_ukv, w_c_out, w_o, ln_g, ln_b):
    c_act = jax.nn.silu(c)
    cos, sin = rope_tables(positions)
    h = x
    for l in range(DEPTH):
        h = hybrid_layer(h, c_act, cos, sin, w_ada[l], b_ada[l], w_in[l], conv_a_w[l], conv_a_b[l],
                         ln_a_g[l], ln_a_b[l], w_a_out[l], conv_b_w[l], w_b_out[l], q_norm_g[l],
                         kv_norm_g[l], w_uq[l], w_ukv[l], w_c_out[l], w_o[l], ln_g[l], ln_b[l])
    return h


import jax as _jax
import jax.numpy as _jnp

TWIN_FORMAT = 'train_step'
FWD_PARAMS = ['x', 'c', 'positions', 'w_ada', 'b_ada', 'w_in', 'conv_a_w', 'conv_a_b', 'ln_a_g', 'ln_a_b', 'w_a_out', 'conv_b_w', 'w_b_out', 'q_norm_g', 'kv_norm_g', 'w_uq', 'w_ukv', 'w_c_out', 'w_o', 'ln_g', 'ln_b']
TWIN_WEIGHTS = ['w_ada', 'b_ada', 'w_in', 'conv_a_w', 'conv_a_b', 'ln_a_g', 'ln_a_b', 'w_a_out', 'conv_b_w', 'w_b_out', 'q_norm_g', 'kv_norm_g', 'w_uq', 'w_ukv', 'w_c_out', 'w_o', 'ln_g', 'ln_b']
TWIN_DIFF_INPUT = 'x'
TWIN_INPUTS = ['x', 'c', 'positions', 'w_ada', 'b_ada', 'w_in', 'conv_a_w', 'conv_a_b', 'ln_a_g', 'ln_a_b', 'w_a_out', 'conv_b_w', 'w_b_out', 'q_norm_g', 'kv_norm_g', 'w_uq', 'w_ukv', 'w_c_out', 'w_o', 'ln_g', 'ln_b', 'loss_target', 'm_w_ada', 'm_b_ada', 'm_w_in', 'm_conv_a_w', 'm_conv_a_b', 'm_ln_a_g', 'm_ln_a_b', 'm_w_a_out', 'm_conv_b_w', 'm_w_b_out', 'm_q_norm_g', 'm_kv_norm_g', 'm_w_uq', 'm_w_ukv', 'm_w_c_out', 'm_w_o', 'm_ln_g', 'm_ln_b', 'v_w_ada', 'v_b_ada', 'v_w_in', 'v_conv_a_w', 'v_conv_a_b', 'v_ln_a_g', 'v_ln_a_b', 'v_w_a_out', 'v_conv_b_w', 'v_w_b_out', 'v_q_norm_g', 'v_kv_norm_g', 'v_w_uq', 'v_w_ukv', 'v_w_c_out', 'v_w_o', 'v_ln_g', 'v_ln_b']
TWIN_OUTPUTS = ['loss', 'grad_x', 'grad_w_ada', 'grad_b_ada', 'grad_w_in', 'grad_conv_a_w', 'grad_conv_a_b', 'grad_ln_a_g', 'grad_ln_a_b', 'grad_w_a_out', 'grad_conv_b_w', 'grad_w_b_out', 'grad_q_norm_g', 'grad_kv_norm_g', 'grad_w_uq', 'grad_w_ukv', 'grad_w_c_out', 'grad_w_o', 'grad_ln_g', 'grad_ln_b', 'delta_w_ada', 'delta_b_ada', 'delta_w_in', 'delta_conv_a_w', 'delta_conv_a_b', 'delta_ln_a_g', 'delta_ln_a_b', 'delta_w_a_out', 'delta_conv_b_w', 'delta_w_b_out', 'delta_q_norm_g', 'delta_kv_norm_g', 'delta_w_uq', 'delta_w_ukv', 'delta_w_c_out', 'delta_w_o', 'delta_ln_g', 'delta_ln_b', 'new_m_w_ada', 'new_m_b_ada', 'new_m_w_in', 'new_m_conv_a_w', 'new_m_conv_a_b', 'new_m_ln_a_g', 'new_m_ln_a_b', 'new_m_w_a_out', 'new_m_conv_b_w', 'new_m_w_b_out', 'new_m_q_norm_g', 'new_m_kv_norm_g', 'new_m_w_uq', 'new_m_w_ukv', 'new_m_w_c_out', 'new_m_w_o', 'new_m_ln_g', 'new_m_ln_b', 'new_v_w_ada', 'new_v_b_ada', 'new_v_w_in', 'new_v_conv_a_w', 'new_v_conv_a_b', 'new_v_ln_a_g', 'new_v_ln_a_b', 'new_v_w_a_out', 'new_v_conv_b_w', 'new_v_w_b_out', 'new_v_q_norm_g', 'new_v_kv_norm_g', 'new_v_w_uq', 'new_v_w_ukv', 'new_v_w_c_out', 'new_v_w_o', 'new_v_ln_g', 'new_v_ln_b']
TWIN_LEAF_KINDS = {'loss': 'loss', 'grad_x': 'grad_x', 'grad_w_ada': 'grad_w', 'grad_b_ada': 'grad_w', 'grad_w_in': 'grad_w', 'grad_conv_a_w': 'grad_w', 'grad_conv_a_b': 'grad_w', 'grad_ln_a_g': 'grad_w', 'grad_ln_a_b': 'grad_w', 'grad_w_a_out': 'grad_w', 'grad_conv_b_w': 'grad_w', 'grad_w_b_out': 'grad_w', 'grad_q_norm_g': 'grad_w', 'grad_kv_norm_g': 'grad_w', 'grad_w_uq': 'grad_w', 'grad_w_ukv': 'grad_w', 'grad_w_c_out': 'grad_w', 'grad_w_o': 'grad_w', 'grad_ln_g': 'grad_w', 'grad_ln_b': 'grad_w', 'delta_w_ada': 'delta_w', 'delta_b_ada': 'delta_w', 'delta_w_in': 'delta_w', 'delta_conv_a_w': 'delta_w', 'delta_conv_a_b': 'delta_w', 'delta_ln_a_g': 'delta_w', 'delta_ln_a_b': 'delta_w', 'delta_w_a_out': 'delta_w', 'delta_conv_b_w': 'delta_w', 'delta_w_b_out': 'delta_w', 'delta_q_norm_g': 'delta_w', 'delta_kv_norm_g': 'delta_w', 'delta_w_uq': 'delta_w', 'delta_w_ukv': 'delta_w', 'delta_w_c_out': 'delta_w', 'delta_w_o': 'delta_w', 'delta_ln_g': 'delta_w', 'delta_ln_b': 'delta_w', 'new_m_w_ada': 'new_m', 'new_m_b_ada': 'new_m', 'new_m_w_in': 'new_m', 'new_m_conv_a_w': 'new_m', 'new_m_conv_a_b': 'new_m', 'new_m_ln_a_g': 'new_m', 'new_m_ln_a_b': 'new_m', 'new_m_w_a_out': 'new_m', 'new_m_conv_b_w': 'new_m', 'new_m_w_b_out': 'new_m', 'new_m_q_norm_g': 'new_m', 'new_m_kv_norm_g': 'new_m', 'new_m_w_uq': 'new_m', 'new_m_w_ukv': 'new_m', 'new_m_w_c_out': 'new_m', 'new_m_w_o': 'new_m', 'new_m_ln_g': 'new_m', 'new_m_ln_b': 'new_m', 'new_v_w_ada': 'new_v', 'new_v_b_ada': 'new_v', 'new_v_w_in': 'new_v', 'new_v_conv_a_w': 'new_v', 'new_v_conv_a_b': 'new_v', 'new_v_ln_a_g': 'new_v', 'new_v_ln_a_b': 'new_v', 'new_v_w_a_out': 'new_v', 'new_v_conv_b_w': 'new_v', 'new_v_w_b_out': 'new_v', 'new_v_q_norm_g': 'new_v', 'new_v_kv_norm_g': 'new_v', 'new_v_w_uq': 'new_v', 'new_v_w_ukv': 'new_v', 'new_v_w_c_out': 'new_v', 'new_v_w_o': 'new_v', 'new_v_ln_g': 'new_v', 'new_v_ln_b': 'new_v'}


def _forward(args):
    return _fwd_reference(*[args[k] for k in FWD_PARAMS])


def _output_shape():
    out = _jax.eval_shape(lambda: _forward(_fwd_setup_inputs(0)))
    return out.shape, out.dtype

N_MICROBATCH = 1
ADAM_LR = 0.001
ADAM_B1 = 0.9
ADAM_B2 = 0.999
ADAM_EPS = 1e-08
ADAM_WD = 0.01
ADAM_STEP = 10
PER_EXAMPLE_BATCH_AXIS = {'x': 0, 'c': 0, 'positions': 0, 'loss_target': 0}
SHARED_INPUTS = []
_WEIGHT_DTYPES = {'w_ada': _jnp.float32, 'b_ada': _jnp.float32, 'w_in': _jnp.float32, 'conv_a_w': _jnp.float32, 'conv_a_b': _jnp.float32, 'ln_a_g': _jnp.float32, 'ln_a_b': _jnp.float32, 'w_a_out': _jnp.float32, 'conv_b_w': _jnp.float32, 'w_b_out': _jnp.float32, 'q_norm_g': _jnp.float32, 'kv_norm_g': _jnp.float32, 'w_uq': _jnp.float32, 'w_ukv': _jnp.float32, 'w_c_out': _jnp.float32, 'w_o': _jnp.float32, 'ln_g': _jnp.float32, 'ln_b': _jnp.float32}
MOMENT_SCALE = {'w_ada': 3.266288e-03, 'b_ada': 5.285259e-03, 'w_in': 1.561423e-03, 'conv_a_w': 1.307574e-03, 'conv_a_b': 2.455266e-03, 'ln_a_g': 1.551373e-03, 'ln_a_b': 1.370318e-03, 'w_a_out': 2.120572e-03, 'conv_b_w': 2.739272e-03, 'w_b_out': 4.631108e-03, 'q_norm_g': 4.112046e-04, 'kv_norm_g': 1.120469e-03, 'w_uq': 2.951948e-04, 'w_ukv': 5.656858e-04, 'w_c_out': 1.279362e-03, 'w_o': 5.237261e-03, 'ln_g': 8.032661e+00, 'ln_b': 3.679681e-01}


def _to_microbatches(a, axis):
    t = _jnp.moveaxis(a, axis, 0)
    t = t.reshape((N_MICROBATCH, t.shape[0] // N_MICROBATCH) + t.shape[1:])
    return _jnp.moveaxis(t, 1, axis + 1)


def setup_inputs(seed: int = 0) -> dict:
    inp = _fwd_setup_inputs(seed)
    key = _jax.random.fold_in(_jax.random.key(seed), 7919)
    shape, _ = _output_shape()
    out = dict(inp)
    out["loss_target"] = _jax.random.normal(_jax.random.fold_in(key, 0), shape, _jnp.float32)
    for i, name in enumerate(TWIN_WEIGHTS):
        w = inp[name].astype(_jnp.float32)
        if MOMENT_SCALE is None:
            s = _jnp.sqrt(_jnp.mean(_jnp.square(w)) + 1e-30)
        else:
            s = MOMENT_SCALE[name]
        km, kv = _jax.random.split(_jax.random.fold_in(key, i + 1))
        out[name] = w
        out["m_" + name] = s * _jax.random.normal(km, w.shape, _jnp.float32)
        out["v_" + name] = (s * s) * _jax.random.uniform(kv, w.shape, _jnp.float32, 0.5, 1.5)
    if N_MICROBATCH > 1:
        for name, axis in PER_EXAMPLE_BATCH_AXIS.items():
            out[name] = _to_microbatches(out[name], axis)
    return {'x': out['x'], 'c': out['c'], 'positions': out['positions'], 'w_ada': out['w_ada'], 'b_ada': out['b_ada'], 'w_in': out['w_in'], 'conv_a_w': out['conv_a_w'], 'conv_a_b': out['conv_a_b'], 'ln_a_g': out['ln_a_g'], 'ln_a_b': out['ln_a_b'], 'w_a_out': out['w_a_out'], 'conv_b_w': out['conv_b_w'], 'w_b_out': out['w_b_out'], 'q_norm_g': out['q_norm_g'], 'kv_norm_g': out['kv_norm_g'], 'w_uq': out['w_uq'], 'w_ukv': out['w_ukv'], 'w_c_out': out['w_c_out'], 'w_o': out['w_o'], 'ln_g': out['ln_g'], 'ln_b': out['ln_b'], 'loss_target': out['loss_target'], 'm_w_ada': out['m_w_ada'], 'm_b_ada': out['m_b_ada'], 'm_w_in': out['m_w_in'], 'm_conv_a_w': out['m_conv_a_w'], 'm_conv_a_b': out['m_conv_a_b'], 'm_ln_a_g': out['m_ln_a_g'], 'm_ln_a_b': out['m_ln_a_b'], 'm_w_a_out': out['m_w_a_out'], 'm_conv_b_w': out['m_conv_b_w'], 'm_w_b_out': out['m_w_b_out'], 'm_q_norm_g': out['m_q_norm_g'], 'm_kv_norm_g': out['m_kv_norm_g'], 'm_w_uq': out['m_w_uq'], 'm_w_ukv': out['m_w_ukv'], 'm_w_c_out': out['m_w_c_out'], 'm_w_o': out['m_w_o'], 'm_ln_g': out['m_ln_g'], 'm_ln_b': out['m_ln_b'], 'v_w_ada': out['v_w_ada'], 'v_b_ada': out['v_b_ada'], 'v_w_in': out['v_w_in'], 'v_conv_a_w': out['v_conv_a_w'], 'v_conv_a_b': out['v_conv_a_b'], 'v_ln_a_g': out['v_ln_a_g'], 'v_ln_a_b': out['v_ln_a_b'], 'v_w_a_out': out['v_w_a_out'], 'v_conv_b_w': out['v_conv_b_w'], 'v_w_b_out': out['v_w_b_out'], 'v_q_norm_g': out['v_q_norm_g'], 'v_kv_norm_g': out['v_kv_norm_g'], 'v_w_uq': out['v_w_uq'], 'v_w_ukv': out['v_w_ukv'], 'v_w_c_out': out['v_w_c_out'], 'v_w_o': out['v_w_o'], 'v_ln_g': out['v_ln_g'], 'v_ln_b': out['v_ln_b']}


def _loss(weights, diff, rest, loss_target):
    with _jax.named_scope("forward"):
        args = {**rest, TWIN_DIFF_INPUT: diff, **{k: w.astype(_WEIGHT_DTYPES[k]) for k, w in weights.items()}}
        y = _forward(args)
    with _jax.named_scope("loss_head"):
        err = _jnp.square(y.astype(_jnp.float32) - loss_target)
        return 0.5 * _jnp.sum(_jnp.mean(err, axis=-1)) if err.ndim else 0.5 * err


def _adamw(w, g, m, v):
    m = ADAM_B1 * m + (1.0 - ADAM_B1) * g
    v = ADAM_B2 * v + (1.0 - ADAM_B2) * _jnp.square(g)
    m_hat = m / (1.0 - ADAM_B1 ** ADAM_STEP)
    v_hat = v / (1.0 - ADAM_B2 ** ADAM_STEP)
    delta = -ADAM_LR * (m_hat / (_jnp.sqrt(v_hat) + ADAM_EPS) + ADAM_WD * w)
    return delta, m, v


def reference(x, c, positions, w_ada, b_ada, w_in, conv_a_w, conv_a_b, ln_a_g, ln_a_b, w_a_out, conv_b_w, w_b_out, q_norm_g, kv_norm_g, w_uq, w_ukv, w_c_out, w_o, ln_g, ln_b, loss_target, m_w_ada, m_b_ada, m_w_in, m_conv_a_w, m_conv_a_b, m_ln_a_g, m_ln_a_b, m_w_a_out, m_conv_b_w, m_w_b_out, m_q_norm_g, m_kv_norm_g, m_w_uq, m_w_ukv, m_w_c_out, m_w_o, m_ln_g, m_ln_b, v_w_ada, v_b_ada, v_w_in, v_conv_a_w, v_conv_a_b, v_ln_a_g, v_ln_a_b, v_w_a_out, v_conv_b_w, v_w_b_out, v_q_norm_g, v_kv_norm_g, v_w_uq, v_w_ukv, v_w_c_out, v_w_o, v_ln_g, v_ln_b):
    given = dict(x=x, c=c, positions=positions, w_ada=w_ada, b_ada=b_ada, w_in=w_in, conv_a_w=conv_a_w, conv_a_b=conv_a_b, ln_a_g=ln_a_g, ln_a_b=ln_a_b, w_a_out=w_a_out, conv_b_w=conv_b_w, w_b_out=w_b_out, q_norm_g=q_norm_g, kv_norm_g=kv_norm_g, w_uq=w_uq, w_ukv=w_ukv, w_c_out=w_c_out, w_o=w_o, ln_g=ln_g, ln_b=ln_b, loss_target=loss_target, m_w_ada=m_w_ada, m_b_ada=m_b_ada, m_w_in=m_w_in, m_conv_a_w=m_conv_a_w, m_conv_a_b=m_conv_a_b, m_ln_a_g=m_ln_a_g, m_ln_a_b=m_ln_a_b, m_w_a_out=m_w_a_out, m_conv_b_w=m_conv_b_w, m_w_b_out=m_w_b_out, m_q_norm_g=m_q_norm_g, m_kv_norm_g=m_kv_norm_g, m_w_uq=m_w_uq, m_w_ukv=m_w_ukv, m_w_c_out=m_w_c_out, m_w_o=m_w_o, m_ln_g=m_ln_g, m_ln_b=m_ln_b, v_w_ada=v_w_ada, v_b_ada=v_b_ada, v_w_in=v_w_in, v_conv_a_w=v_conv_a_w, v_conv_a_b=v_conv_a_b, v_ln_a_g=v_ln_a_g, v_ln_a_b=v_ln_a_b, v_w_a_out=v_w_a_out, v_conv_b_w=v_conv_b_w, v_w_b_out=v_w_b_out, v_q_norm_g=v_q_norm_g, v_kv_norm_g=v_kv_norm_g, v_w_uq=v_w_uq, v_w_ukv=v_w_ukv, v_w_c_out=v_w_c_out, v_w_o=v_w_o, v_ln_g=v_ln_g, v_ln_b=v_ln_b)
    weights = {n: given[n] for n in TWIN_WEIGHTS}
    shared = {n: given[n] for n in SHARED_INPUTS}
    per_example = {n: given[n] for n in ['x', 'c', 'positions']}
    grad_fn = _jax.value_and_grad(_loss, argnums=(0, 1))

    def one_microbatch(ex, loss_target):
        ex = dict(ex)
        diff = ex.pop(TWIN_DIFF_INPUT)
        return grad_fn(weights, diff, {**shared, **ex}, loss_target)

    if N_MICROBATCH == 1:
        loss, (grad_w, grad_x) = one_microbatch(per_example, given["loss_target"])
    else:
        def body(carry, xs):
            loss_sum, grad_sum = carry
            l_k, (gw_k, gx_k) = one_microbatch(xs[0], xs[1])
            with _jax.named_scope("update"):
                return (loss_sum + l_k, _jax.tree.map(_jnp.add, grad_sum, gw_k)), gx_k

        init = (_jnp.zeros((), _jnp.float32), _jax.tree.map(_jnp.zeros_like, weights))
        (loss, grad_w), grad_x = _jax.lax.scan(body, init, (per_example, given["loss_target"]))
    with _jax.named_scope("update"):
        delta_w, new_m, new_v = {}, {}, {}
        for n in TWIN_WEIGHTS:
            delta_w[n], new_m[n], new_v[n] = _adamw(weights[n], grad_w[n], given["m_" + n], given["v_" + n])
    return (loss, grad_x, *[grad_w[n] for n in TWIN_WEIGHTS], *[delta_w[n] for n in TWIN_WEIGHTS],
            *[new_m[n] for n in TWIN_WEIGHTS], *[new_v[n] for n in TWIN_WEIGHTS])
```

```python
import functools
import math

import numpy as np
import jax
import jax.numpy as jnp
from jax import lax
from jax.experimental import pallas as pl
from jax.experimental.pallas import tpu as pltpu

BF = jnp.bfloat16
F32 = jnp.float32
MESH = pl.DeviceIdType.MESH
NDEV = 8

HEADS, NOPE, ROPE, VH = 8, 64, 32, 64
HP = 128
ROPE_THETA = 10000.0
LN_EPS = 1e-5
RMS_EPS = 1e-6
LR, B1, B2, EPS, WD, STEP = 0.001, 0.9, 0.999, 1e-08, 0.01, 10

LANE = 128
VMEM_LIMIT = 56 * 1024 * 1024

D_MODEL, CW, QL, KVL = 1024, 512, 384, 256
OFF_M, OFF_A1, OFF_A2, OFF_AG, OFF_XB, OFF_GB, OFF_GC, OFF_BG, OFF_CG = 0, 3072, 3584, 4096, 4608, 5120, 5632, 6144, 6656
OFF_KV, OFF_KR, OFF_Q, NP = 7168, 7424, 7680, 8192
D_IN = 7840


def _cparams(**kw):
    return pltpu.CompilerParams(vmem_limit_bytes=VMEM_LIMIT, **kw)


def _sigmoid(x):
    return jax.nn.sigmoid(x)


def _silu(x):
    return x * _sigmoid(x)


def _dsilu(x):
    s = _sigmoid(x)
    return s * (1.0 + x * (1.0 - s))


def _pick_tile(n, cap, mult):
    if n <= cap:
        return n
    for t in range(cap - cap % mult, 0, -mult):
        if n % t == 0:
            return t
    raise ValueError((n, cap, mult))


def mm(a, b, *, name, trans_a=False, trans_b=False, out_dtype=F32, bias=None, tm=1024, tn=1024, tk=2048):
    if trans_a:
        K, M = a.shape
    else:
        M, K = a.shape
    if trans_b:
        N, K2 = b.shape
    else:
        K2, N = b.shape
    assert K == K2 and not (trans_a and trans_b), (a.shape, b.shape)
    tm, tn = _pick_tile(M, tm, 16), _pick_tile(N, tn, LANE)
    tk = _pick_tile(K, tk, LANE if trans_b else 16)
    assert M % tm == 0 and N % tn == 0 and K % tk == 0, (M, N, K, tm, tn, tk)
    nk = K // tk
    dims = (((0 if trans_a else 1,), (1 if trans_b else 0,)), ((), ()))
    has_bias = bias is not None

    def body(*refs):
        a_ref, b_ref = refs[0], refs[1]
        bias_ref = refs[2] if has_bias else None
        o_ref = refs[3] if has_bias else refs[2]
        p = lax.dot_general(a_ref[...], b_ref[...], dims, preferred_element_type=F32)

        def finish(v):
            if has_bias:
                v = v + bias_ref[...]
            o_ref[...] = v.astype(o_ref.dtype)

        if nk == 1:
            finish(p)
        else:
            acc = refs[-1]
            k = pl.program_id(2)

            @pl.when(k == 0)
            def _():
                acc[...] = p

            @pl.when(k > 0)
            def _():
                acc[...] += p

            @pl.when(k == nk - 1)
            def _():
                finish(acc[...])

    if trans_a:
        a_spec = pl.BlockSpec((tk, tm), lambda i, j, k: (k, i))
    else:
        a_spec = pl.BlockSpec((tm, tk), lambda i, j, k: (i, k))
    if trans_b:
        b_spec = pl.BlockSpec((tn, tk), lambda i, j, k: (j, k))
    else:
        b_spec = pl.BlockSpec((tk, tn), lambda i, j, k: (k, j))
    in_specs = [a_spec, b_spec]
    args = [a, b]
    if has_bias:
        in_specs.append(pl.BlockSpec((1, tn), lambda i, j, k: (0, j)))
        args.append(bias)
    return pl.pallas_call(
        body, name=name, grid=(M // tm, N // tn, nk),
        in_specs=in_specs, out_specs=pl.BlockSpec((tm, tn), lambda i, j, k: (i, j)),
        out_shape=jax.ShapeDtypeStruct((M, N), out_dtype),
        scratch_shapes=[pltpu.VMEM((tm, tn), F32)] if nk > 1 else [],
        compiler_params=_cparams(),
    )(*args)


def rowwise(name, fn, S, T, row_ins, full_ins, row_outs, acc_outs=()):
    n_in = len(row_ins) + len(full_ins)
    n_ro, n_ao = len(row_outs), len(acc_outs)

    def body(*refs):
        vals = [r[...] for r in refs[:n_in]]
        outs = fn(*vals)
        if not isinstance(outs, (tuple, list)):
            outs = (outs,)
        assert len(outs) == n_ro + n_ao, (name, len(outs))
        for r, v in zip(refs[n_in:n_in + n_ro], outs[:n_ro]):
            r[...] = v.astype(r.dtype)
        first = pl.program_id(0) == 0
        for r, v in zip(refs[n_in + n_ro:], outs[n_ro:]):
            def init(r=r, v=v):
                r[...] = v

            def accum(r=r, v=v):
                r[...] += v

            pl.when(first)(init)
            pl.when(jnp.logical_not(first))(accum)

    in_specs, args = [], []
    for arr, W, off in row_ins:
        assert off % W == 0 and arr.shape[0] == S, (name, arr.shape, W, off)
        in_specs.append(pl.BlockSpec((T, W), functools.partial(lambda i, cb: (i, cb), cb=off // W)))
        args.append(arr)
    for arr in full_ins:
        in_specs.append(pl.BlockSpec(arr.shape, lambda i: (0, 0)))
        args.append(arr)
    out_specs = [pl.BlockSpec((T, W), lambda i: (i, 0)) for W, _ in row_outs]
    out_specs += [pl.BlockSpec((1, W), lambda i: (0, 0)) for W in acc_outs]
    out_shape = [jax.ShapeDtypeStruct((S, W), dt) for W, dt in row_outs]
    out_shape += [jax.ShapeDtypeStruct((1, W), F32) for W in acc_outs]
    return pl.pallas_call(
        body, name=name, grid=(S // T,), in_specs=in_specs, out_specs=out_specs, out_shape=out_shape,
        compiler_params=_cparams(),
    )(*args)


def _colsum(v):
    return jnp.sum(v, axis=0, keepdims=True)


def _ln_stats(r):
    mu = jnp.mean(r, axis=-1, keepdims=True)
    d = r - mu
    var = jnp.mean(d * d, axis=-1, keepdims=True)
    rstd = lax.rsqrt(var + LN_EPS)
    return d * rstd, rstd


def _ln_bwd(dn, n, rstd):
    return rstd * (dn - jnp.mean(dn, axis=-1, keepdims=True) - n * jnp.mean(dn * n, axis=-1, keepdims=True))


CPAD = 32
TC = 64


def _pre(mode, x1, x2):
    return x1 * _sigmoid(x2) if mode == "glu" else x1 * x2


def _shifted(ext, sft):
    n = TC + CPAD
    return pltpu.roll(ext, (n - sft) % n, 0)[0:TC]


def conv_fwd(name, src, off1, off2, w_pad, taps, mode, S, C):
    nchunk = S // TC

    def body(x1_ref, x2_ref, w_ref, o_ref, a_pad):
        a_pad[0:CPAD, :] = jnp.zeros((CPAD, LANE), F32)

        def fill(i, _):
            r = pl.multiple_of(i * 256, 256)
            a_pad[pl.ds(CPAD + r, 256), :] = _pre(mode, x1_ref[pl.ds(r, 256), :], x2_ref[pl.ds(r, 256), :])
            return 0

        lax.fori_loop(0, S // 256, fill, 0)

        def chunk(i, _):
            base = pl.multiple_of(i * TC, TC)
            ext = a_pad[pl.ds(base, TC + CPAD), :]
            acc = jnp.zeros((TC, LANE), F32)
            for k in range(taps):
                acc = acc + w_ref[pl.ds(k, 1), :] * _shifted(ext, CPAD - (taps - 1) + k)
            o_ref[pl.ds(base, TC), :] = acc
            return 0

        lax.fori_loop(0, nchunk, chunk, 0)

    kp = w_pad.shape[0]
    return pl.pallas_call(
        body, name=name, grid=(C // LANE,),
        in_specs=[pl.BlockSpec((S, LANE), functools.partial(lambda j, o: (0, o + j), o=off1 // LANE)),
                  pl.BlockSpec((S, LANE), functools.partial(lambda j, o: (0, o + j), o=off2 // LANE)),
                  pl.BlockSpec((kp, LANE), lambda j: (0, j))],
        out_specs=pl.BlockSpec((S, LANE), lambda j: (0, j)),
        out_shape=jax.ShapeDtypeStruct((S, C), F32),
        scratch_shapes=[pltpu.VMEM((S + CPAD, LANE), F32)],
        compiler_params=_cparams(),
    )(src, src, w_pad)


def conv_bwd(name, src, off1, off2, dc, w_pad, taps, mode, S, C):
    nchunk = S // TC
    kp = w_pad.shape[0]

    def body(x1_ref, x2_ref, dc_ref, w_ref, d1_ref, d2_ref, dw_ref, a_pad, dc_pad, dw_acc):
        a_pad[0:CPAD, :] = jnp.zeros((CPAD, LANE), F32)
        dc_pad[S:S + CPAD, :] = jnp.zeros((CPAD, LANE), F32)
        dw_acc[...] = jnp.zeros(dw_acc.shape, F32)

        def fill(i, _):
            r = pl.multiple_of(i * 256, 256)
            a_pad[pl.ds(CPAD + r, 256), :] = _pre(mode, x1_ref[pl.ds(r, 256), :], x2_ref[pl.ds(r, 256), :])
            dc_pad[pl.ds(r, 256), :] = dc_ref[pl.ds(r, 256), :]
            return 0

        lax.fori_loop(0, S // 256, fill, 0)

        def chunk(i, _):
            base = pl.multiple_of(i * TC, TC)
            ext_d = dc_pad[pl.ds(base, TC + CPAD), :]
            ext_a = a_pad[pl.ds(base, TC + CPAD), :]
            dcv = ext_d[0:TC]
            da = jnp.zeros((TC, LANE), F32)
            for k in range(taps):
                da = da + w_ref[pl.ds(k, 1), :] * _shifted(ext_d, taps - 1 - k)
                prod = dcv * _shifted(ext_a, CPAD - (taps - 1) + k)
                fold = prod[0:8]
                for g in range(1, TC // 8):
                    fold = fold + prod[8 * g:8 * g + 8]
                dw_acc[pl.ds(8 * k, 8), :] += fold
            x1 = x1_ref[pl.ds(base, TC), :]
            x2 = x2_ref[pl.ds(base, TC), :]
            if mode == "glu":
                s = _sigmoid(x2)
                d1, d2 = da * s, da * x1 * s * (1.0 - s)
            else:
                d1, d2 = da * x2, da * x1
            d1_ref[pl.ds(base, TC), :] = d1.astype(BF)
            d2_ref[pl.ds(base, TC), :] = d2.astype(BF)
            return 0

        lax.fori_loop(0, nchunk, chunk, 0)
        dw_ref[...] = jnp.zeros(dw_ref.shape, F32)
        for k in range(taps):
            dw_ref[pl.ds(k, 1), :] = jnp.sum(dw_acc[pl.ds(8 * k, 8), :], axis=0, keepdims=True)

    blk = pl.BlockSpec((S, LANE), lambda j: (0, j))
    return pl.pallas_call(
        body, name=name, grid=(C // LANE,),
        in_specs=[pl.BlockSpec((S, LANE), functools.partial(lambda j, o: (0, o + j), o=off1 // LANE)),
                  pl.BlockSpec((S, LANE), functools.partial(lambda j, o: (0, o + j), o=off2 // LANE)),
                  blk, pl.BlockSpec((kp, LANE), lambda j: (0, j))],
        out_specs=[blk, blk, pl.BlockSpec((kp, LANE), lambda j: (0, j))],
        out_shape=[jax.ShapeDtypeStruct((S, C), BF), jax.ShapeDtypeStruct((S, C), BF),
                   jax.ShapeDtypeStruct((kp, C), F32)],
        scratch_shapes=[pltpu.VMEM((S + CPAD, LANE), F32), pltpu.VMEM((S + CPAD, LANE), F32),
                        pltpu.VMEM((8 * kp, LANE), F32)],
        compiler_params=_cparams(),
    )(src, src, dc, w_pad)


TA = 256
SCALE = (NOPE + ROPE) ** -0.5
NT_DIMS = (((1,), (1,)), ((), ()))
TN_DIMS = (((0,), (0,)), ((), ()))


def _causal_mask():
    row = lax.broadcasted_iota(jnp.int32, (TA, TA), 0)
    col = lax.broadcasted_iota(jnp.int32, (TA, TA), 1)
    return col <= row


def attn_fwd(q, k, v, S):
    nq = S // TA

    def body(q_ref, k_ref, v_ref, o_ref, lse_ref):
        def q_block(qi, _):
            r0 = pl.multiple_of(qi * TA, TA)
            qb = q_ref[0, pl.ds(r0, TA), :]

            def step(kj, carry, masked):
                m, l, acc = carry
                c0 = pl.multiple_of(kj * TA, TA)
                kb = k_ref[0, pl.ds(c0, TA), :]
                vb = v_ref[0, pl.ds(c0, TA), :]
                s = lax.dot_general(qb, kb, NT_DIMS, preferred_element_type=F32) * SCALE
                if masked:
                    s = jnp.where(_causal_mask(), s, -jnp.inf)
                m_new = jnp.maximum(m, jnp.max(s, axis=-1, keepdims=True))
                p = jnp.exp(s - m_new)
                alpha = jnp.exp(m - m_new)
                l = alpha * l + jnp.sum(p, axis=-1, keepdims=True)
                acc = alpha * acc + jnp.dot(p.astype(BF), vb, preferred_element_type=F32)
                return m_new, l, acc

            init = (jnp.full((TA, 1), -jnp.inf, F32), jnp.zeros((TA, 1), F32), jnp.zeros((TA, HP), F32))
            carry = lax.fori_loop(0, qi, lambda kj, c: step(kj, c, False), init)
            m, l, acc = step(qi, carry, True)
            o_ref[0, pl.ds(r0, TA), :] = acc / l
            lse_ref[0, pl.ds(r0, TA), :] = jnp.broadcast_to(m + jnp.log(l), (TA, HP))
            return 0

        lax.fori_loop(0, nq, q_block, 0)

    blk = pl.BlockSpec((1, S, HP), lambda h: (h, 0, 0))
    return pl.pallas_call(
        body, name="attn_fwd", grid=(HEADS,), in_specs=[blk, blk, blk], out_specs=[blk, blk],
        out_shape=[jax.ShapeDtypeStruct((HEADS, S, HP), F32), jax.ShapeDtypeStruct((HEADS, S, HP), F32)],
        compiler_params=_cparams(),
    )(q, k, v)


def attn_bwd(q, k, v, o, lse, do, S):
    nq = S // TA

    def body(q_ref, k_ref, v_ref, o_ref, lse_ref, do_ref, dq_ref, dk_ref, dv_ref):
        dk_ref[...] = jnp.zeros(dk_ref.shape, F32)
        dv_ref[...] = jnp.zeros(dv_ref.shape, F32)

        def q_block(qi, _):
            r0 = pl.multiple_of(qi * TA, TA)
            qb = q_ref[0, pl.ds(r0, TA), :]
            dof = do_ref[0, pl.ds(r0, TA), :]
            dob = dof.astype(BF)
            lse_b = lse_ref[0, pl.ds(r0, TA), :][:, 0:1]
            delta = jnp.sum(dof * o_ref[0, pl.ds(r0, TA), :], axis=-1, keepdims=True)

            def step(kj, dq, masked):
                c0 = pl.multiple_of(kj * TA, TA)
                kb = k_ref[0, pl.ds(c0, TA), :]
                vb = v_ref[0, pl.ds(c0, TA), :]
                s = lax.dot_general(qb, kb, NT_DIMS, preferred_element_type=F32) * SCALE
                if masked:
                    s = jnp.where(_causal_mask(), s, -jnp.inf)
                p = jnp.exp(s - lse_b)
                dp = lax.dot_general(dob, vb, NT_DIMS, preferred_element_type=F32)
                ds = (p * (dp - delta) * SCALE).astype(BF)
                dv_ref[0, pl.ds(c0, TA), :] += lax.dot_general(p.astype(BF), dob, TN_DIMS, preferred_element_type=F32)
                dk_ref[0, pl.ds(c0, TA), :] += lax.dot_general(ds, qb, TN_DIMS, preferred_element_type=F32)
                return dq + jnp.dot(ds, kb, preferred_element_type=F32)

            dq = lax.fori_loop(0, qi, lambda kj, c: step(kj, c, False), jnp.zeros((TA, HP), F32))
            dq_ref[0, pl.ds(r0, TA), :] = step(qi, dq, True)
            return 0

        lax.fori_loop(0, nq, q_block, 0)

    blk = pl.BlockSpec((1, S, HP), lambda h: (h, 0, 0))
    shp = jax.ShapeDtypeStruct((HEADS, S, HP), F32)
    return pl.pallas_call(
        body, name="attn_bwd", grid=(HEADS,), in_specs=[blk] * 6, out_specs=[blk] * 3, out_shape=[shp] * 3,
        compiler_params=_cparams(),
    )(q, k, v, o, lse, do)


def exchange(name, gathers, a2as):
    n_g, n = len(gathers), len(gathers) + len(a2as)

    def body(*refs):
        ins, outs = refs[:n], refs[n:2 * n]
        send_sems, recv_sems, loc_sems = refs[2 * n:]
        x, y, c = lax.axis_index("x"), lax.axis_index("y"), lax.axis_index("c")
        me = 4 * x + 2 * y + c

        def peer(k):
            px = 1 - x if k & 4 else x
            py = 1 - y if k & 2 else y
            pc = 1 - c if k & 1 else c
            return (px, py, pc), 4 * px + 2 * py + pc

        def remote(a, k):
            pid, pflat = peer(k)
            src = ins[a] if a < n_g else ins[a].at[pflat]
            return pltpu.make_async_remote_copy(
                src_ref=src, dst_ref=outs[a].at[me], send_sem=send_sems.at[a, k - 1], recv_sem=recv_sems.at[a, k - 1],
                device_id=pid, device_id_type=MESH)

        def arrival(a, k):
            pid, pflat = peer(k)
            src = ins[a] if a < n_g else ins[a].at[pflat]
            return pltpu.make_async_remote_copy(
                src_ref=src, dst_ref=outs[a].at[pflat], send_sem=send_sems.at[a, k - 1], recv_sem=recv_sems.at[a, k - 1],
                device_id=pid, device_id_type=MESH)

        local = []
        for a in range(n):
            own = ins[a] if a < n_g else ins[a].at[me]
            cp = pltpu.make_async_copy(own, outs[a].at[me], loc_sems.at[a])
            cp.start()
            local.append(cp)
        sent = []
        for k in (1, 2, 4, 3, 5, 6, 7):
            for a in range(n):
                cp = remote(a, k)
                cp.start()
                sent.append(cp)
        for k in range(1, 8):
            for a in range(n):
                arrival(a, k).wait_recv()
        for cp in sent:
            cp.wait_send()
        for cp in local:
            cp.wait()

    out_shape = [jax.ShapeDtypeStruct((NDEV,) + g.shape, g.dtype) for g in gathers]
    out_shape += [jax.ShapeDtypeStruct(a.shape, a.dtype) for a in a2as]
    any_spec = pl.BlockSpec(memory_space=pl.ANY)
    return pl.pallas_call(
        body, name=name, in_specs=[any_spec] * n, out_specs=[any_spec] * n, out_shape=out_shape,
        scratch_shapes=[pltpu.SemaphoreType.DMA((n, NDEV - 1)), pltpu.SemaphoreType.DMA((n, NDEV - 1)),
                        pltpu.SemaphoreType.DMA((n,))],
    )(*gathers, *a2as)


def _pick_rows(R, mult, cap):
    best = None
    for n in range(1, R + 1):
        if R % n == 0 and (R // n) % mult == 0 and R // n <= cap:
            best = R // n
            break
    assert best is not None, (R, mult, cap)
    return best


def sum_slots(name, x):
    _, R, _ = x.shape
    tr = _pick_rows(R, 16, 2304)

    def body(x_ref, o_ref):
        acc = x_ref[0].astype(F32)
        for d in range(1, NDEV):
            acc = acc + x_ref[d].astype(F32)
        o_ref[...] = acc

    return pl.pallas_call(
        body, name=name, grid=(R // tr,),
        in_specs=[pl.BlockSpec((NDEV, tr, LANE), lambda i: (0, i, 0))],
        out_specs=pl.BlockSpec((tr, LANE), lambda i: (i, 0)),
        out_shape=jax.ShapeDtypeStruct((R, LANE), F32), compiler_params=_cparams(),
    )(x)


def adamw(name, w, g, m, v):
    L, R, C = w.shape
    tr = _pick_rows(R, 8, 256) if R % 8 == 0 else R

    def body(w_ref, g_ref, m_ref, v_ref, d_ref, nm_ref, nv_ref):
        gg = g_ref[...]
        nm = B1 * m_ref[...] + (1.0 - B1) * gg
        nv = B2 * v_ref[...] + (1.0 - B2) * jnp.square(gg)
        m_hat = nm / (1.0 - B1 ** STEP)
        v_hat = nv / (1.0 - B2 ** STEP)
        d_ref[...] = -LR * (m_hat / (jnp.sqrt(v_hat) + EPS) + WD * w_ref[...])
        nm_ref[...] = nm
        nv_ref[...] = nv

    blk = pl.BlockSpec((1, tr, C), lambda l, i: (l, i, 0))
    shp = jax.ShapeDtypeStruct(w.shape, F32)
    return pl.pallas_call(
        body, name=name, grid=(L, R // tr), in_specs=[blk] * 4, out_specs=[blk] * 3, out_shape=[shp] * 3,
        compiler_params=_cparams(),
    )(w, g, m, v)


IN_SHARD = D_IN // NDEV
UQ_SHARD = HEADS * (NOPE + ROPE) // NDEV
W_IN_PAD = 1024
ROW_A, ROW_B, ROW_C, ROW_UKV, ROW_UQ, MISC_ROWS = 0, 512, 1024, 1536, 1792, 2176


def _in_perm_index():
    ar = np.arange
    z = lambda n: np.full((n,), -1, np.int64)
    return np.concatenate([ar(4768, 7840), ar(0, 3584), ar(4256, 4768), ar(3968, 4224), ar(4224, 4256),
                           z(OFF_Q - OFF_KR - ROPE), ar(3584, 3968), z(NP - OFF_Q - QL)])


def _head_perm_index(a, b):
    h = np.arange(HEADS)[:, None] * (a + b)
    return np.concatenate([(h + np.arange(a)[None]).reshape(-1), (h + a + np.arange(b)[None]).reshape(-1)])


def _inverse(perm, n):
    inv = np.full((n,), -1, np.int64)
    inv[perm[perm >= 0]] = np.nonzero(perm >= 0)[0]
    return inv


IN_PERM = _in_perm_index()
UQ_PERM = _head_perm_index(NOPE, ROPE)
UKV_PERM = _head_perm_index(NOPE, VH)


def _to_gathered(perm, shard, pad):
    return np.where(perm >= 0, (perm // shard) * pad + perm % shard, -1)


def _from_full(inv, shard, pad):
    j, i = np.divmod(np.arange(NDEV * pad), pad)
    return np.where(i < shard, inv[np.minimum(j * shard + i, inv.shape[0] - 1)], -1)


def col_gather(name, srcs, out_shapes, jobs):
    ns, nj = len(srcs), len(jobs)
    tables = [jnp.asarray(np.asarray(job[5], np.int32)[None, :]) for job in jobs]

    def view(ref, col0, width, r0, rc):
        n = ref.shape[-1]
        if len(ref.shape) == 3:
            return ref.at[col0 // n, pl.ds(r0, rc), pl.ds(col0 % n, width)]
        return ref.at[pl.ds(r0, rc), pl.ds(col0, width)]

    def body(*refs):
        src_refs, tab_refs, out_refs = refs[:ns], refs[ns:ns + nj], refs[ns + nj:]
        for ji, (si, srow, oi, orow, nrows, tgt) in enumerate(jobs):
            sref, oref = src_refs[si], out_refs[oi]
            tgt = np.asarray(tgt)
            tw = 256 if oref.shape[-1] % 256 == 0 else LANE
            rc = 256 if nrows % 256 == 0 else LANE
            for t in range(tgt.shape[0] // tw):
                tt = tgt[t * tw:(t + 1) * tw]
                tiles = sorted(set((tt[tt >= 0] // LANE).tolist()))
                straight = bool(tiles) and np.array_equal(tt, np.arange(tiles[0] * LANE, tiles[0] * LANE + tw))
                onehots = []
                if tiles and not straight:
                    want = tab_refs[ji][:, t * tw:(t + 1) * tw]
                    row = lax.broadcasted_iota(jnp.int32, (LANE, tw), 0)
                    onehots = [jnp.where(want == row + s * LANE, 1.0, 0.0).astype(BF) for s in tiles]

                def chunk(ci, _, t=t, tiles=tiles, straight=straight, onehots=onehots):
                    r0 = ci * rc
                    dst = view(oref, t * tw, tw, pl.multiple_of(orow + r0, LANE), rc)
                    rs = pl.multiple_of(srow + r0, LANE)
                    if not tiles:
                        dst[...] = jnp.zeros((rc, tw), BF)
                    elif straight:
                        for k in range(tw // LANE):
                            view(oref, t * tw + k * LANE, LANE, pl.multiple_of(orow + r0, LANE), rc)[...] = (
                                view(sref, (tiles[0] + k) * LANE, LANE, rs, rc)[...])
                    else:
                        acc = None
                        for s, oh in zip(tiles, onehots):
                            p = jnp.dot(view(sref, s * LANE, LANE, rs, rc)[...], oh, preferred_element_type=F32)
                            acc = p if acc is None else acc + p
                        dst[...] = acc.astype(BF)
                    return 0

                lax.fori_loop(0, nrows // rc, chunk, 0)

    vmem = pl.BlockSpec(memory_space=pltpu.VMEM)
    return pl.pallas_call(
        body, name=name, in_specs=[vmem] * (ns + nj), out_specs=[vmem] * len(out_shapes),
        out_shape=[jax.ShapeDtypeStruct(s, BF) for s in out_shapes], compiler_params=_cparams(),
    )(*srcs, *tables)


def sum_adamw(name, recvs, w, m, v):
    L, R, C = w.shape
    CP = recvs[0].shape[-1]
    tr = _pick_rows(R, 16, 128)

    def body(*refs):
        r_refs = refs[:L]
        w_ref, m_ref, v_ref, g_ref, d_ref, nm_ref, nv_ref, gsum = refs[L:]
        layer = pl.program_id(0)
        for k in range(L):
            def total(k=k):
                acc = r_refs[k][0].astype(F32)
                for d in range(1, NDEV):
                    acc = acc + r_refs[k][d].astype(F32)
                gsum[...] = acc
            pl.when(layer == k)(total)
        gg = gsum[:, 0:C]
        nm = B1 * m_ref[...] + (1.0 - B1) * gg
        nv = B2 * v_ref[...] + (1.0 - B2) * jnp.square(gg)
        m_hat = nm / (1.0 - B1 ** STEP)
        v_hat = nv / (1.0 - B2 ** STEP)
        g_ref[...] = gg
        d_ref[...] = -LR * (m_hat / (jnp.sqrt(v_hat) + EPS) + WD * w_ref[...])
        nm_ref[...] = nm
        nv_ref[...] = nv

    r_specs = [pl.BlockSpec((NDEV, tr, CP), functools.partial(lambda l, i, k: (0, jnp.where(l == k, i, 0), 0), k=k))
               for k in range(L)]
    blk = pl.BlockSpec((None, tr, C), lambda l, i: (l, i, 0))
    shp = jax.ShapeDtypeStruct(w.shape, F32)
    return pl.pallas_call(
        body, name=name, grid=(L, R // tr), in_specs=r_specs + [blk] * 3, out_specs=[blk] * 4, out_shape=[shp] * 4,
        scratch_shapes=[pltpu.VMEM((tr, CP), F32)], compiler_params=_cparams(),
    )(*recvs, w, m, v)


def _heads(x, width):
    S = x.shape[0]
    x = x.reshape(S, HEADS, width).transpose(1, 0, 2)
    return jnp.pad(x, ((0, 0), (0, 0), (0, HP - width)))


ALPHA = 8.0 ** 0.25


def _rope_fn(sign):
    def fn(x, cos, sin):
        W = x.shape[-1]
        lane = lax.broadcasted_iota(jnp.int32, x.shape, 1)
        first_half = (lane % ROPE) < (ROPE // 2)
        rot = jnp.where(first_half, -pltpu.roll(x, W - ROPE // 2, 1), pltpu.roll(x, ROPE // 2, 1))
        return x * cos + sign * rot * sin
    return fn


def layer_fwd(x, ada3, W, tabs, S):
    cos, sin = tabs
    T = 256
    u = rowwise("modulate", lambda xv, a: xv * (1.0 + a[1:2, :]) + a[0:1, :], S, T,
                [(x, D_MODEL, 0)], [ada3], [(D_MODEL, BF)])[0]
    proj = mm(u, W["in"], name="mm_proj", tm=1024, tn=1024)

    ca = conv_fwd("conv_a_fwd", proj, OFF_A1, OFF_A2, W["conv_a"], 31, "glu", S, CW)

    def a_post(c, ag, vec):
        n, _ = _ln_stats(c + vec[0:1, :])
        return _silu(n * vec[1:2, :] + vec[2:3, :]) * _silu(ag)

    h_a = rowwise("mix_a_post", a_post, S, T, [(ca, CW, 0), (proj, CW, OFF_AG)], [W["vec_a"]], [(CW, BF)])[0]
    y_a = mm(h_a, W["a_out"], name="mm_branch_out")

    cb = conv_fwd("conv_b_fwd", proj, OFF_XB, OFF_GC, W["conv_b"], 3, "mul", S, CW)
    h_b = rowwise("mix_b_post", lambda c, gb, bg: gb * c * _silu(bg), S, T,
                  [(cb, CW, 0), (proj, CW, OFF_GB), (proj, CW, OFF_BG)], [], [(CW, BF)])[0]
    y_b = mm(h_b, W["b_out"], name="mm_branch_out")

    def rms2(ql, kvl, gq, gkv):
        rq = lax.rsqrt(jnp.mean(ql * ql, axis=-1, keepdims=True) + RMS_EPS)
        rk = lax.rsqrt(jnp.mean(kvl * kvl, axis=-1, keepdims=True) + RMS_EPS)
        return ql * rq * gq, kvl * rk * gkv

    qn, kvn = rowwise("rms_fwd", rms2, S, T, [(proj, QL, OFF_Q), (proj, KVL, OFF_KV)], [W["gq"], W["gkv"]],
                      [(QL, BF), (KVL, BF)])
    q = mm(qn, W["uq"], name="mm_q")
    kv = mm(kvn, W["ukv"], name="mm_kv")
    rope = _rope_fn(1.0)
    q_pe, k_pe = rowwise("rope_fwd", lambda a, b, c1, s1: (rope(a, c1, s1), rope(b, c1[:, :LANE], s1[:, :LANE])),
                         S, T, [(q, 256, 512), (proj, LANE, OFF_KR), (cos, 256, 0), (sin, 256, 0)], [],
                         [(256, BF), (LANE, BF)])
    q_h = _heads(jnp.concatenate([q[:, :512].astype(BF).reshape(S, HEADS, NOPE), q_pe.reshape(S, HEADS, ROPE)],
                                 axis=2).reshape(S, HEADS * 96), 96)
    k_pe_b = jnp.broadcast_to(k_pe[:, None, :ROPE], (S, HEADS, ROPE))
    k_h = _heads(jnp.concatenate([kv[:, :512].astype(BF).reshape(S, HEADS, NOPE), k_pe_b], axis=2)
                 .reshape(S, HEADS * 96), 96)
    v_h = _heads(kv[:, 512:].astype(BF), VH)
    o_h, lse = attn_fwd(q_h, k_h, v_h, S)
    o = o_h[:, :, :VH].transpose(1, 0, 2).reshape(S, HEADS * VH)
    h_c = rowwise("mix_c_post", lambda ov, cg: ov * _silu(cg), S, T, [(o, CW, 0), (proj, CW, OFF_CG)], [],
                  [(CW, BF)])[0]
    y_c = mm(h_c, W["c_out"], name="mm_branch_out")

    def merge(la, lb, lc, ya, yb, yc):
        return _sigmoid(la) * ya + _sigmoid(lb) * yb + _sigmoid(lc) * yc

    m = rowwise("merge_fwd", merge, S, 128,
                [(proj, D_MODEL, 0), (proj, D_MODEL, 1024), (proj, D_MODEL, 2048), (y_a, D_MODEL, 0),
                 (y_b, D_MODEL, 0), (y_c, D_MODEL, 0)], [], [(D_MODEL, BF)])[0]
    out = mm(m, W["o"], name="mm_out")

    def ln_fwd(xv, ov, a, lnv):
        n, _ = _ln_stats(ALPHA * xv + a[2:3, :] * ov)
        return n * lnv[0:1, :] + lnv[1:2, :]

    x_next = rowwise("ln_fwd", ln_fwd, S, 128, [(x, D_MODEL, 0), (out, D_MODEL, 0)], [ada3, W["lnv"]],
                     [(D_MODEL, F32)])[0]
    saved = dict(x=x, u=u, proj=proj, ca=ca, cb=cb, h_a=h_a, h_b=h_b, h_c=h_c, y_a=y_a, y_b=y_b, y_c=y_c, qn=qn,
                 kvn=kvn, q_h=q_h, k_h=k_h, v_h=v_h, o_h=o_h, lse=lse, o=o, m=m, out=out)
    return x_next, saved


def layer_bwd(dxn, sv, ada3, W, tabs, S):
    cos, sin = tabs
    T = 256
    x, proj = sv["x"], sv["proj"]
    G = {}

    def ln_bwd(xv, ov, dy, a, lnv):
        gate = a[2:3, :]
        n, rstd = _ln_stats(ALPHA * xv + gate * ov)
        dr = _ln_bwd(dy * lnv[0:1, :], n, rstd)
        return ALPHA * dr, gate * dr, _colsum(dy * n), _colsum(dy), _colsum(dr * ov)

    dres, d_out, G["ln_g"], G["ln_b"], d_gate = rowwise(
        "ln_bwd", ln_bwd, S, 128, [(x, D_MODEL, 0), (sv["out"], D_MODEL, 0), (dxn, D_MODEL, 0)], [ada3, W["lnv"]],
        [(D_MODEL, F32), (D_MODEL, BF)], [D_MODEL] * 3)
    dm = mm(d_out, W["o"], name="mm_dm", trans_b=True)
    G["w_o"] = mm(sv["m"], d_out, name="mm_gw_o", trans_a=True, out_dtype=BF)

    def merge_bwd(dmv, la, lb, lc, ya, yb, yc):
        outs, dls = [], []
        for lg, yv in ((la, ya), (lb, yb), (lc, yc)):
            s = _sigmoid(lg)
            outs.append(dmv * s)
            dls.append(dmv * yv * s * (1.0 - s))
        return tuple(outs) + tuple(dls)

    dy_a, dy_b, dy_c, dl_a, dl_b, dl_c = rowwise(
        "merge_bwd", merge_bwd, S, 128,
        [(dm, D_MODEL, 0), (proj, D_MODEL, 0), (proj, D_MODEL, 1024), (proj, D_MODEL, 2048), (sv["y_a"], D_MODEL, 0),
         (sv["y_b"], D_MODEL, 0), (sv["y_c"], D_MODEL, 0)], [], [(D_MODEL, BF)] * 6)

    dh = {}
    for br, dy in (("a", dy_a), ("b", dy_b), ("c", dy_c)):
        dh[br] = mm(dy, W[br + "_out"], name="mm_dh", trans_b=True)
        G["w_%s_out" % br] = mm(sv["h_" + br], dy, name="mm_gw_branch", trans_a=True, out_dtype=BF)

    def a_post_bwd(c, ag, dhv, vec):
        n, rstd = _ln_stats(c + vec[0:1, :])
        z = n * vec[1:2, :] + vec[2:3, :]
        d_ag = dhv * _silu(z) * _dsilu(ag)
        dz = dhv * _silu(ag) * _dsilu(z)
        dc = _ln_bwd(dz * vec[1:2, :], n, rstd)
        return dc, d_ag, _colsum(dc), _colsum(dz * n), _colsum(dz)

    dca, d_ag, G["conv_a_b"], G["ln_a_g"], G["ln_a_b"] = rowwise(
        "mix_a_post_bwd", a_post_bwd, S, T, [(sv["ca"], CW, 0), (proj, CW, OFF_AG), (dh["a"], CW, 0)], [W["vec_a"]],
        [(CW, F32), (CW, BF)], [CW] * 3)
    d_a1, d_a2, G["conv_a_w"] = conv_bwd("conv_a_bwd", proj, OFF_A1, OFF_A2, dca, W["conv_a"], 31, "glu", S, CW)

    def b_post_bwd(c, gb, bg, dhv):
        sg = _silu(bg)
        return dhv * sg * gb, dhv * sg * c, dhv * gb * c * _dsilu(bg)

    dcb, d_gb, d_bg = rowwise("mix_b_post_bwd", b_post_bwd, S, T,
                              [(sv["cb"], CW, 0), (proj, CW, OFF_GB), (proj, CW, OFF_BG), (dh["b"], CW, 0)], [],
                              [(CW, F32), (CW, BF), (CW, BF)])
    d_xb, d_gc, G["conv_b_w"] = conv_bwd("conv_b_bwd", proj, OFF_XB, OFF_GC, dcb, W["conv_b"], 3, "mul", S, CW)

    d_o, d_cg = rowwise("mix_c_post_bwd", lambda ov, cg, dhv: (dhv * _silu(cg), dhv * ov * _dsilu(cg)), S, T,
                        [(sv["o"], CW, 0), (proj, CW, OFF_CG), (dh["c"], CW, 0)], [], [(CW, F32), (CW, BF)])
    dq_h, dk_h, dv_h = attn_bwd(sv["q_h"], sv["k_h"], sv["v_h"], sv["o_h"], sv["lse"], _heads(d_o, VH), S)
    unheads = lambda t, lo, hi: t[:, :, lo:hi].transpose(1, 0, 2).reshape(S, HEADS * (hi - lo))
    ropeT = _rope_fn(-1.0)

    def rope_bwd(dqp, dkp, c1, s1):
        f = dkp[:, :LANE] + dkp[:, LANE:]
        f = f + pltpu.roll(f, 64, 1)
        f = f + pltpu.roll(f, 32, 1)
        return ropeT(dqp, c1, s1), ropeT(f, c1[:, :LANE], s1[:, :LANE])

    dq_pe, dk_pe = rowwise("rope_bwd", rope_bwd, S, T,
                           [(unheads(dq_h, NOPE, 96), 256, 0), (unheads(dk_h, NOPE, 96), 256, 0), (cos, 256, 0),
                            (sin, 256, 0)], [], [(256, BF), (LANE, BF)])
    d_q = jnp.concatenate([unheads(dq_h, 0, NOPE).astype(BF), dq_pe], axis=1)
    d_kv = jnp.concatenate([unheads(dk_h, 0, NOPE), unheads(dv_h, 0, VH)], axis=1).astype(BF)
    d_qn = mm(d_q, W["uq"], name="mm_dqn", trans_b=True)
    d_kvn = mm(d_kv, W["ukv"], name="mm_dkvn", trans_b=True)
    G["w_uq"] = mm(sv["qn"], d_q, name="mm_gw_uq", trans_a=True, out_dtype=BF)
    G["w_ukv"] = mm(sv["kvn"], d_kv, name="mm_gw_ukv", trans_a=True, out_dtype=BF)

    def rms_bwd(ql, kvl, dqn, dkn, gq, gkv):
        res = []
        for xv, dy, g in ((ql, dqn, gq), (kvl, dkn, gkv)):
            r = lax.rsqrt(jnp.mean(xv * xv, axis=-1, keepdims=True) + RMS_EPS)
            dxh = dy * g
            res.append((r * (dxh - xv * (r * r) * jnp.mean(dxh * xv, axis=-1, keepdims=True)), _colsum(dy * xv * r)))
        return res[0][0], res[1][0], res[0][1], res[1][1]

    d_ql, d_kvl, G["q_norm_g"], G["kv_norm_g"] = rowwise(
        "rms_bwd", rms_bwd, S, T, [(proj, QL, OFF_Q), (proj, KVL, OFF_KV), (d_qn, QL, 0), (d_kvn, KVL, 0)],
        [W["gq"], W["gkv"]], [(QL, BF), (KVL, BF)], [QL, KVL])

    zeros = lambda n: jnp.zeros((S, n), BF)
    d_proj = jnp.concatenate([dl_a, dl_b, dl_c, d_a1, d_a2, d_ag, d_xb, d_gb, d_gc, d_bg, d_cg, d_kvl,
                              dk_pe[:, :ROPE], zeros(OFF_Q - OFF_KR - ROPE), d_ql, zeros(NP - OFF_Q - QL)], axis=1)
    du = mm(d_proj, W["in"], name="mm_du", trans_b=True, tk=1024)
    G["w_in"] = mm(sv["u"], d_proj, name="mm_gw_in", trans_a=True, out_dtype=BF)

    def mod_bwd(duv, xv, dr, a):
        return duv * (1.0 + a[1:2, :]) + dr, _colsum(duv), _colsum(duv * xv)

    dx, d_shift, d_scale = rowwise("mod_bwd", mod_bwd, S, 128, [(du, D_MODEL, 0), (x, D_MODEL, 0), (dres, D_MODEL, 0)],
                                   [ada3], [(D_MODEL, F32)], [D_MODEL] * 2)
    d_ada = jnp.concatenate([d_shift, d_scale, d_gate], axis=1)
    return dx, G, d_ada


SMALL = ("conv_a_b", "ln_a_g", "ln_a_b", "q_norm_g", "kv_norm_g", "ln_g", "ln_b")


def _rows(v):
    n = v.shape[0]
    r = -(-n // (LANE * 16)) * 16
    return jnp.pad(v, (0, r * LANE - n)).reshape(r, LANE)


def kernel(x, c, positions, w_ada, b_ada, w_in, conv_a_w, conv_a_b, ln_a_g, ln_a_b, w_a_out, conv_b_w, w_b_out, q_norm_g, kv_norm_g, w_uq, w_ukv, w_c_out, w_o, ln_g, ln_b, loss_target, m_w_ada, m_b_ada, m_w_in, m_conv_a_w, m_conv_a_b, m_ln_a_g, m_ln_a_b, m_w_a_out, m_conv_b_w, m_w_b_out, m_q_norm_g, m_kv_norm_g, m_w_uq, m_w_ukv, m_w_c_out, m_w_o, m_ln_g, m_ln_b, v_w_ada, v_b_ada, v_w_in, v_conv_a_w, v_conv_a_b, v_ln_a_g, v_ln_a_b, v_w_a_out, v_conv_b_w, v_w_b_out, v_q_norm_g, v_kv_norm_g, v_w_uq, v_w_ukv, v_w_c_out, v_w_o, v_ln_g, v_ln_b):
    P = dict(w_ada=w_ada, b_ada=b_ada, w_in=w_in, conv_a_w=conv_a_w, conv_a_b=conv_a_b, ln_a_g=ln_a_g, ln_a_b=ln_a_b,
             w_a_out=w_a_out, conv_b_w=conv_b_w, w_b_out=w_b_out, q_norm_g=q_norm_g, kv_norm_g=kv_norm_g, w_uq=w_uq,
             w_ukv=w_ukv, w_c_out=w_c_out, w_o=w_o, ln_g=ln_g, ln_b=ln_b)
    Mo = dict(w_ada=m_w_ada, b_ada=m_b_ada, w_in=m_w_in, conv_a_w=m_conv_a_w, conv_a_b=m_conv_a_b, ln_a_g=m_ln_a_g,
              ln_a_b=m_ln_a_b, w_a_out=m_w_a_out, conv_b_w=m_conv_b_w, w_b_out=m_w_b_out, q_norm_g=m_q_norm_g,
              kv_norm_g=m_kv_norm_g, w_uq=m_w_uq, w_ukv=m_w_ukv, w_c_out=m_w_c_out, w_o=m_w_o, ln_g=m_ln_g, ln_b=m_ln_b)
    Vo = dict(w_ada=v_w_ada, b_ada=v_b_ada, w_in=v_w_in, conv_a_w=v_conv_a_w, conv_a_b=v_conv_a_b, ln_a_g=v_ln_a_g,
              ln_a_b=v_ln_a_b, w_a_out=v_w_a_out, conv_b_w=v_conv_b_w, w_b_out=v_w_b_out, q_norm_g=v_q_norm_g,
              kv_norm_g=v_kv_norm_g, w_uq=v_w_uq, w_ukv=v_w_ukv, w_c_out=v_w_c_out, w_o=v_w_o, ln_g=v_ln_g, ln_b=v_ln_b)
    ORDER = ("w_ada", "b_ada", "w_in", "conv_a_w", "conv_a_b", "ln_a_g", "ln_a_b", "w_a_out", "conv_b_w", "w_b_out",
             "q_norm_g", "kv_norm_g", "w_uq", "w_ukv", "w_c_out", "w_o", "ln_g", "ln_b")
    L = w_ada.shape[0]
    S = x.shape[1]
    me = 4 * lax.axis_index("x") + 2 * lax.axis_index("y") + lax.axis_index("c")
    x2 = x[0]
    tgt = loss_target[0]

    small_in = _rows(jnp.concatenate([c.reshape(-1), conv_a_w.reshape(-1), conv_b_w.reshape(-1)]))
    sg = exchange("gather_small", [small_in], [])[0]
    w_in_b = jnp.pad(w_in.astype(BF), ((0, 0), (0, 0), (0, W_IN_PAD - IN_SHARD)))
    misc_b = jnp.concatenate([w_a_out, w_b_out, w_c_out, w_ukv, jnp.pad(w_uq, ((0, 0), (0, 0), (0, LANE - UQ_SHARD)))],
                             axis=1).astype(BF)
    w_o_b = w_o.astype(BF)
    gathered = [exchange("gather_layer", [w_in_b[l], misc_b[l], w_o_b[l]], []) for l in range(L)]
    sgf = sg.reshape(NDEV, -1)
    c_all = sgf[:, :D_MODEL]
    o1 = D_MODEL + L * 31 * 64
    conv_a_full = sgf[:, D_MODEL:o1].reshape(NDEV, L, 31, 64).transpose(1, 2, 0, 3).reshape(L, 31, CW)
    conv_b_full = sgf[:, o1:o1 + L * 3 * 64].reshape(NDEV, L, 3, 64).transpose(1, 2, 0, 3).reshape(L, 3, CW)

    c_act = rowwise("silu_c", _silu, 16, 16, [(jnp.pad(c_all, ((0, 8), (0, 0))), D_MODEL, 0)], [], [(D_MODEL, BF)])[0]
    ncol = w_ada.shape[2]
    w_ada_b = w_ada.astype(BF).transpose(1, 0, 2).reshape(D_MODEL, L * ncol)
    b_mine = lax.dynamic_slice_in_dim(b_ada, me * ncol, ncol, axis=1).reshape(1, L * ncol)
    ada_part = mm(c_act, w_ada_b, name="mm_ada", bias=b_mine)
    ada_rows = -(-(L * ncol) // (LANE * 8)) * 8
    ada_send = jnp.pad(ada_part[:NDEV].reshape(NDEV, -1, LANE), ((0, 0), (0, ada_rows - L * ncol // LANE), (0, 0)))
    ada_recv = exchange("a2a_ada", [], [ada_send])[0]
    ada = ada_recv[:, :L * ncol // LANE].reshape(NDEV, L, ncol).transpose(1, 0, 2).reshape(L, 3, D_MODEL)

    inv_freq = ROPE_THETA ** (-jnp.arange(0, ROPE, 2, dtype=F32) / ROPE)
    ang = positions[0].astype(F32)[:, None] * inv_freq
    tabs = (jnp.tile(jnp.cos(ang), (1, 2 * HEADS)), jnp.tile(jnp.sin(ang), (1, 2 * HEADS)))

    straight = np.arange(D_MODEL)
    fwd_in = [(0, 0, 0, 0, D_MODEL, _to_gathered(IN_PERM, IN_SHARD, W_IN_PAD))]
    fwd_misc = [(0, ROW_A, 0, 0, CW, straight), (0, ROW_B, 1, 0, CW, straight), (0, ROW_C, 2, 0, CW, straight),
                (0, ROW_UKV, 3, 0, KVL, UKV_PERM), (0, ROW_UQ, 4, 0, QL, _to_gathered(UQ_PERM, UQ_SHARD, LANE))]
    rev_in = [(0, 0, 0, 0, D_MODEL, _from_full(_inverse(IN_PERM, D_IN), IN_SHARD, W_IN_PAD))]
    rev_misc = [(0, 0, 0, ROW_A, CW, straight), (1, 0, 0, ROW_B, CW, straight), (2, 0, 0, ROW_C, CW, straight),
                (3, 0, 0, ROW_UKV, KVL, _from_full(_inverse(UKV_PERM, HEADS * (NOPE + VH)), LANE, LANE)),
                (4, 0, 0, ROW_UQ, QL, _from_full(_inverse(UQ_PERM, HEADS * (NOPE + ROPE)), UQ_SHARD, LANE))]

    def layer_weights(l):
        g_in, g_misc, g_o = gathered[l]
        w_in_p = col_gather("relayout_w_in", [g_in], [(D_MODEL, NP)], fwd_in)[0]
        a_out, b_out, c_out, ukv, uq = col_gather(
            "relayout_misc", [g_misc], [(CW, D_MODEL)] * 3 + [(KVL, HEADS * (NOPE + VH)), (QL, HEADS * (NOPE + ROPE))],
            fwd_misc)
        return {
            "in": w_in_p, "a_out": a_out, "b_out": b_out, "c_out": c_out, "uq": uq, "ukv": ukv,
            "o": g_o.reshape(D_MODEL, D_MODEL),
            "conv_a": jnp.pad(conv_a_full[l], ((0, 1), (0, 0))), "conv_b": jnp.pad(conv_b_full[l], ((0, 5), (0, 0))),
            "vec_a": jnp.stack([conv_a_b[l], ln_a_g[l], ln_a_b[l]]), "gq": q_norm_g[l][None], "gkv": kv_norm_g[l][None],
            "lnv": jnp.stack([ln_g[l], ln_b[l]]),
        }

    h = x2
    saved, weights = [], []
    for l in range(L):
        Wl = layer_weights(l)
        h, sv = layer_fwd(h, ada[l], Wl, tabs, S)
        saved.append(sv)
        weights.append(Wl)

    def loss_fn(y, t):
        e = y - t
        return e * (1.0 / D_MODEL), _colsum(e * e)

    dy, sq = rowwise("loss", loss_fn, S, 256, [(h, D_MODEL, 0), (tgt, D_MODEL, 0)], [], [(D_MODEL, F32)], [D_MODEL])
    loss = lax.psum(0.5 * jnp.sum(sq) / D_MODEL, ("x", "y", "c"))

    grads, d_adas, recv = [None] * L, [None] * L, [None] * L
    for l in reversed(range(L)):
        dy, g, d_adas[l] = layer_bwd(dy, saved[l], ada[l], weights[l], tabs, S)
        grads[l] = g
        send_in = col_gather("unrelayout_w_in", [g["w_in"]], [(NDEV, D_MODEL, W_IN_PAD)], rev_in)[0]
        send_misc = col_gather("unrelayout_misc", [g["w_a_out"], g["w_b_out"], g["w_c_out"], g["w_ukv"], g["w_uq"]],
                               [(NDEV, MISC_ROWS, LANE)], rev_misc)[0]
        send_o = g["w_o"].reshape(NDEV, D_MODEL // NDEV, D_MODEL)
        recv[l] = exchange("scatter_layer", [], [send_in, send_misc, send_o])
    grad_x = dy[None]

    small_parts = [jnp.stack([grads[l][n].reshape(-1) for l in range(L)]).reshape(-1) for n in SMALL]
    small_parts.append(jnp.stack([grads[l]["conv_a_w"][:31].reshape(-1) for l in range(L)]).reshape(-1))
    small_parts.append(jnp.stack([grads[l]["conv_b_w"][:3].reshape(-1) for l in range(L)]).reshape(-1))
    small_parts.append(jnp.stack([d_adas[l].reshape(-1) for l in range(L)]).reshape(-1))
    small_sizes = [int(p.shape[0]) for p in small_parts]
    gsmall = exchange("gather_small_grads", [_rows(jnp.concatenate(small_parts))], [])[0]
    gsum = sum_slots("sum_small", gsmall).reshape(-1)
    Gr = {}
    offs = np.cumsum([0] + small_sizes)
    for i, n in enumerate(SMALL):
        Gr[n] = gsum[offs[i]:offs[i + 1]].reshape(L, -1)
    ca = gsum[offs[7]:offs[8]].reshape(L, 31, CW)
    cbw = gsum[offs[8]:offs[9]].reshape(L, 3, CW)
    Gr["conv_a_w"] = lax.dynamic_slice_in_dim(ca, me * 64, 64, axis=2)
    Gr["conv_b_w"] = lax.dynamic_slice_in_dim(cbw, me * 64, 64, axis=2)
    Gr["b_ada"] = gsum[offs[9]:offs[10]].reshape(L, 3 * D_MODEL)
    d_ada_all = gsmall.reshape(NDEV, -1)[:, offs[9]:offs[10]].reshape(NDEV, L, 3 * D_MODEL)
    d_mine = lax.dynamic_slice_in_dim(d_ada_all, me * ncol, ncol, axis=2).reshape(NDEV, L * ncol)
    g_ada = mm(c_act, jnp.pad(d_mine, ((0, 8), (0, 0))).astype(BF), name="mm_gw_ada", trans_a=True)
    Gr["w_ada"] = g_ada.reshape(D_MODEL, L, ncol).transpose(1, 0, 2)

    D, NM, NV = {}, {}, {}
    D["w_ada"], NM["w_ada"], NV["w_ada"] = adamw("adamw_w_ada", P["w_ada"], Gr["w_ada"], Mo["w_ada"], Vo["w_ada"])
    Gr["w_in"], D["w_in"], NM["w_in"], NV["w_in"] = sum_adamw(
        "sum_adamw_w_in", [recv[l][0] for l in range(L)], P["w_in"], Mo["w_in"], Vo["w_in"])
    Gr["w_o"], D["w_o"], NM["w_o"], NV["w_o"] = sum_adamw(
        "sum_adamw_w_o", [recv[l][2] for l in range(L)], P["w_o"], Mo["w_o"], Vo["w_o"])
    misc = lambda T_: jnp.concatenate([T_["w_a_out"], T_["w_b_out"], T_["w_c_out"], T_["w_ukv"],
                                       jnp.pad(T_["w_uq"], ((0, 0), (0, 0), (0, LANE - UQ_SHARD)))], axis=1)
    res = sum_adamw("sum_adamw_misc", [recv[l][1] for l in range(L)], misc(P), misc(Mo), misc(Vo))
    for T_, r in zip((Gr, D, NM, NV), res):
        T_["w_a_out"], T_["w_b_out"], T_["w_c_out"] = r[:, ROW_A:ROW_B], r[:, ROW_B:ROW_C], r[:, ROW_C:ROW_UKV]
        T_["w_ukv"], T_["w_uq"] = r[:, ROW_UKV:ROW_UQ], r[:, ROW_UQ:MISC_ROWS, :UQ_SHARD]
    packed =("b_ada", "conv_a_w", "conv_b_w") + SMALL
    pk = lambda T_: _rows(jnp.concatenate([T_[n].reshape(-1) for n in packed]))[None]
    dS, mS, vS = adamw("adamw_small", pk(P), pk(Gr), pk(Mo), pk(Vo))
    o = 0
    for n in packed:
        sz = int(np.prod(P[n].shape))
        D[n] = dS.reshape(-1)[o:o + sz].reshape(P[n].shape)
        NM[n] = mS.reshape(-1)[o:o + sz].reshape(P[n].shape)
        NV[n] = vS.reshape(-1)[o:o + sz].reshape(P[n].shape)
        o += sz
    return (loss, grad_x, *[Gr[n] for n in ORDER], *[D[n] for n in ORDER], *[NM[n] for n in ORDER],
            *[NV[n] for n in ORDER])
```

```python
import functools
import math

import numpy as np
import jax
import jax.numpy as jnp
from jax import lax
from jax.experimental import pallas as pl
from jax.experimental.pallas import tpu as pltpu

BF = jnp.bfloat16
F32 = jnp.float32
MESH = pl.DeviceIdType.MESH
NDEV = 8

HEADS, NOPE, ROPE, VH = 8, 64, 32, 64
HP = 128
ROPE_THETA = 10000.0
LN_EPS = 1e-5
RMS_EPS = 1e-6
LR, B1, B2, EPS, WD, STEP = 0.001, 0.9, 0.999, 1e-08, 0.01, 10

LANE = 128
VMEM_LIMIT = 56 * 1024 * 1024

D_MODEL, CW, QL, KVL = 1024, 512, 384, 256
OFF_M, OFF_A1, OFF_A2, OFF_AG, OFF_XB, OFF_GB, OFF_GC, OFF_BG, OFF_CG = 0, 3072, 3584, 4096, 4608, 5120, 5632, 6144, 6656
OFF_KV, OFF_KR, OFF_Q, NP = 7168, 7424, 7680, 8192
D_IN = 7840


def _cparams(**kw):
    return pltpu.CompilerParams(vmem_limit_bytes=VMEM_LIMIT, **kw)


def _sigmoid(x):
    return jax.nn.sigmoid(x)


def _silu(x):
    return x * _sigmoid(x)


def _dsilu(x):
    s = _sigmoid(x)
    return s * (1.0 + x * (1.0 - s))


def _pick_tile(n, cap, mult):
    if n <= cap:
        return n
    for t in range(cap - cap % mult, 0, -mult):
        if n % t == 0:
            return t
    raise ValueError((n, cap, mult))


def mm(a, b, *, name, trans_a=False, trans_b=False, out_dtype=F32, bias=None, tm=1024, tn=1024, tk=2048):
    if trans_a:
        K, M = a.shape
    else:
        M, K = a.shape
    if trans_b:
        N, K2 = b.shape
    else:
        K2, N = b.shape
    assert K == K2 and not (trans_a and trans_b), (a.shape, b.shape)
    tm, tn = _pick_tile(M, tm, 16), _pick_tile(N, tn, LANE)
    tk = _pick_tile(K, tk, LANE if trans_b else 16)
    assert M % tm == 0 and N % tn == 0 and K % tk == 0, (M, N, K, tm, tn, tk)
    nk = K // tk
    dims = (((0 if trans_a else 1,), (1 if trans_b else 0,)), ((), ()))
    has_bias = bias is not None

    def body(*refs):
        a_ref, b_ref = refs[0], refs[1]
        bias_ref = refs[2] if has_bias else None
        o_ref = refs[3] if has_bias else refs[2]
        p = lax.dot_general(a_ref[...], b_ref[...], dims, preferred_element_type=F32)

        def finish(v):
            if has_bias:
                v = v + bias_ref[...]
            o_ref[...] = v.astype(o_ref.dtype)

        if nk == 1:
            finish(p)
        else:
            acc = refs[-1]
            k = pl.program_id(2)

            @pl.when(k == 0)
            def _():
                acc[...] = p

            @pl.when(k > 0)
            def _():
                acc[...] += p

            @pl.when(k == nk - 1)
            def _():
                finish(acc[...])

    if trans_a:
        a_spec = pl.BlockSpec((tk, tm), lambda i, j, k: (k, i))
    else:
        a_spec = pl.BlockSpec((tm, tk), lambda i, j, k: (i, k))
    if trans_b:
        b_spec = pl.BlockSpec((tn, tk), lambda i, j, k: (j, k))
    else:
        b_spec = pl.BlockSpec((tk, tn), lambda i, j, k: (k, j))
    in_specs = [a_spec, b_spec]
    args = [a, b]
    if has_bias:
        in_specs.append(pl.BlockSpec((1, tn), lambda i, j, k: (0, j)))
        args.append(bias)
    return pl.pallas_call(
        body, name=name, grid=(M // tm, N // tn, nk),
        in_specs=in_specs, out_specs=pl.BlockSpec((tm, tn), lambda i, j, k: (i, j)),
        out_shape=jax.ShapeDtypeStruct((M, N), out_dtype),
        scratch_shapes=[pltpu.VMEM((tm, tn), F32)] if nk > 1 else [],
        compiler_params=_cparams(),
    )(*args)


def rowwise(name, fn, S, T, row_ins, full_ins, row_outs, acc_outs=()):
    n_in = len(row_ins) + len(full_ins)
    n_ro, n_ao = len(row_outs), len(acc_outs)

    def body(*refs):
        vals = [r[...] for r in refs[:n_in]]
        outs = fn(*vals)
        if not isinstance(outs, (tuple, list)):
            outs = (outs,)
        assert len(outs) == n_ro + n_ao, (name, len(outs))
        for r, v in zip(refs[n_in:n_in + n_ro], outs[:n_ro]):
            r[...] = v.astype(r.dtype)
        first = pl.program_id(0) == 0
        for r, v in zip(refs[n_in + n_ro:], outs[n_ro:]):
            def init(r=r, v=v):
                r[...] = v

            def accum(r=r, v=v):
                r[...] += v

            pl.when(first)(init)
            pl.when(jnp.logical_not(first))(accum)

    in_specs, args = [], []
    for arr, W, off in row_ins:
        assert off % W == 0 and arr.shape[0] == S, (name, arr.shape, W, off)
        in_specs.append(pl.BlockSpec((T, W), functools.partial(lambda i, cb: (i, cb), cb=off // W)))
        args.append(arr)
    for arr in full_ins:
        in_specs.append(pl.BlockSpec(arr.shape, lambda i: (0, 0)))
        args.append(arr)
    out_specs = [pl.BlockSpec((T, W), lambda i: (i, 0)) for W, _ in row_outs]
    out_specs += [pl.BlockSpec((1, W), lambda i: (0, 0)) for W in acc_outs]
    out_shape = [jax.ShapeDtypeStruct((S, W), dt) for W, dt in row_outs]
    out_shape += [jax.ShapeDtypeStruct((1, W), F32) for W in acc_outs]
    return pl.pallas_call(
        body, name=name, grid=(S // T,), in_specs=in_specs, out_specs=out_specs, out_shape=out_shape,
        compiler_params=_cparams(),
    )(*args)


def _colsum(v):
    return jnp.sum(v, axis=0, keepdims=True)


def _ln_stats(r):
    mu = jnp.mean(r, axis=-1, keepdims=True)
    d = r - mu
    var = jnp.mean(d * d, axis=-1, keepdims=True)
    rstd = lax.rsqrt(var + LN_EPS)
    return d * rstd, rstd


def _ln_bwd(dn, n, rstd):
    return rstd * (dn - jnp.mean(dn, axis=-1, keepdims=True) - n * jnp.mean(dn * n, axis=-1, keepdims=True))


CPAD = 32
TC = 64


def _pre(mode, x1, x2):
    return x1 * _sigmoid(x2) if mode == "glu" else x1 * x2


def _shifted(ext, sft):
    n = TC + CPAD
    return pltpu.roll(ext, (n - sft) % n, 0)[0:TC]


def conv_fwd(name, src, off1, off2, w_pad, taps, mode, S, C):
    nchunk = S // TC

    def body(x1_ref, x2_ref, w_ref, o_ref, a_pad):
        a_pad[0:CPAD, :] = jnp.zeros((CPAD, LANE), F32)

        def fill(i, _):
            r = pl.multiple_of(i * 256, 256)
            a_pad[pl.ds(CPAD + r, 256), :] = _pre(mode, x1_ref[pl.ds(r, 256), :], x2_ref[pl.ds(r, 256), :])
            return 0

        lax.fori_loop(0, S // 256, fill, 0)

        def chunk(i, _):
            base = pl.multiple_of(i * TC, TC)
            ext = a_pad[pl.ds(base, TC + CPAD), :]
            acc = jnp.zeros((TC, LANE), F32)
            for k in range(taps):
                acc = acc + w_ref[pl.ds(k, 1), :] * _shifted(ext, CPAD - (taps - 1) + k)
            o_ref[pl.ds(base, TC), :] = acc
            return 0

        lax.fori_loop(0, nchunk, chunk, 0)

    kp = w_pad.shape[0]
    return pl.pallas_call(
        body, name=name, grid=(C // LANE,),
        in_specs=[pl.BlockSpec((S, LANE), functools.partial(lambda j, o: (0, o + j), o=off1 // LANE)),
                  pl.BlockSpec((S, LANE), functools.partial(lambda j, o: (0, o + j), o=off2 // LANE)),
                  pl.BlockSpec((kp, LANE), lambda j: (0, j))],
        out_specs=pl.BlockSpec((S, LANE), lambda j: (0, j)),
        out_shape=jax.ShapeDtypeStruct((S, C), F32),
        scratch_shapes=[pltpu.VMEM((S + CPAD, LANE), F32)],
        compiler_params=_cparams(),
    )(src, src, w_pad)


def conv_bwd(name, src, off1, off2, dc, w_pad, taps, mode, S, C):
    nchunk = S // TC
    kp = w_pad.shape[0]

    def body(x1_ref, x2_ref, dc_ref, w_ref, d1_ref, d2_ref, dw_ref, a_pad, dc_pad, dw_acc):
        a_pad[0:CPAD, :] = jnp.zeros((CPAD, LANE), F32)
        dc_pad[S:S + CPAD, :] = jnp.zeros((CPAD, LANE), F32)
        dw_acc[...] = jnp.zeros(dw_acc.shape, F32)

        def fill(i, _):
            r = pl.multiple_of(i * 256, 256)
            a_pad[pl.ds(CPAD + r, 256), :] = _pre(mode, x1_ref[pl.ds(r, 256), :], x2_ref[pl.ds(r, 256), :])
            dc_pad[pl.ds(r, 256), :] = dc_ref[pl.ds(r, 256), :]
            return 0

        lax.fori_loop(0, S // 256, fill, 0)

        def chunk(i, _):
            base = pl.multiple_of(i * TC, TC)
            ext_d = dc_pad[pl.ds(base, TC + CPAD), :]
            ext_a = a_pad[pl.ds(base, TC + CPAD), :]
            dcv = ext_d[0:TC]
            da = jnp.zeros((TC, LANE), F32)
            for k in range(taps):
                da = da + w_ref[pl.ds(k, 1), :] * _shifted(ext_d, taps - 1 - k)
                prod = dcv * _shifted(ext_a, CPAD - (taps - 1) + k)
                fold = prod[0:8]
                for g in range(1, TC // 8):
                    fold = fold + prod[8 * g:8 * g + 8]
                dw_acc[pl.ds(8 * k, 8), :] += fold
            x1 = x1_ref[pl.ds(base, TC), :]
            x2 = x2_ref[pl.ds(base, TC), :]
            if mode == "glu":
                s = _sigmoid(x2)
                d1, d2 = da * s, da * x1 * s * (1.0 - s)
            else:
                d1, d2 = da * x2, da * x1
            d1_ref[pl.ds(base, TC), :] = d1.astype(BF)
            d2_ref[pl.ds(base, TC), :] = d2.astype(BF)
            return 0

        lax.fori_loop(0, nchunk, chunk, 0)
        dw_ref[...] = jnp.zeros(dw_ref.shape, F32)
        for k in range(taps):
            dw_ref[pl.ds(k, 1), :] = jnp.sum(dw_acc[pl.ds(8 * k, 8), :], axis=0, keepdims=True)

    blk = pl.BlockSpec((S, LANE), lambda j: (0, j))
    return pl.pallas_call(
        body, name=name, grid=(C // LANE,),
        in_specs=[pl.BlockSpec((S, LANE), functools.partial(lambda j, o: (0, o + j), o=off1 // LANE)),
                  pl.BlockSpec((S, LANE), functools.partial(lambda j, o: (0, o + j), o=off2 // LANE)),
                  blk, pl.BlockSpec((kp, LANE), lambda j: (0, j))],
        out_specs=[blk, blk, pl.BlockSpec((kp, LANE), lambda j: (0, j))],
        out_shape=[jax.ShapeDtypeStruct((S, C), BF), jax.ShapeDtypeStruct((S, C), BF),
                   jax.ShapeDtypeStruct((kp, C), F32)],
        scratch_shapes=[pltpu.VMEM((S + CPAD, LANE), F32), pltpu.VMEM((S + CPAD, LANE), F32),
                        pltpu.VMEM((8 * kp, LANE), F32)],
        compiler_params=_cparams(),
    )(src, src, dc, w_pad)


TA = 256
SCALE = (NOPE + ROPE) ** -0.5
NT_DIMS = (((1,), (1,)), ((), ()))
TN_DIMS = (((0,), (0,)), ((), ()))


def _causal_mask():
    row = lax.broadcasted_iota(jnp.int32, (TA, TA), 0)
    col = lax.broadcasted_iota(jnp.int32, (TA, TA), 1)
    return col <= row


def attn_fwd(q, k, v, S):
    nq = S // TA

    def body(q_ref, k_ref, v_ref, o_ref, lse_ref):
        def q_block(qi, _):
            r0 = pl.multiple_of(qi * TA, TA)
            qb = q_ref[0, pl.ds(r0, TA), :]

            def step(kj, carry, masked):
                m, l, acc = carry
                c0 = pl.multiple_of(kj * TA, TA)
                kb = k_ref[0, pl.ds(c0, TA), :]
                vb = v_ref[0, pl.ds(c0, TA), :]
                s = lax.dot_general(qb, kb, NT_DIMS, preferred_element_type=F32) * SCALE
                if masked:
                    s = jnp.where(_causal_mask(), s, -jnp.inf)
                m_new = jnp.maximum(m, jnp.max(s, axis=-1, keepdims=True))
                p = jnp.exp(s - m_new)
                alpha = jnp.exp(m - m_new)
                l = alpha * l + jnp.sum(p, axis=-1, keepdims=True)
                acc = alpha * acc + jnp.dot(p.astype(BF), vb, preferred_element_type=F32)
                return m_new, l, acc

            init = (jnp.full((TA, 1), -jnp.inf, F32), jnp.zeros((TA, 1), F32), jnp.zeros((TA, HP), F32))
            carry = lax.fori_loop(0, qi, lambda kj, c: step(kj, c, False), init)
            m, l, acc = step(qi, carry, True)
            o_ref[0, pl.ds(r0, TA), :] = acc / l
            lse_ref[0, pl.ds(r0, TA), :] = jnp.broadcast_to(m + jnp.log(l), (TA, HP))
            return 0

        lax.fori_loop(0, nq, q_block, 0)

    blk = pl.BlockSpec((1, S, HP), lambda h: (h, 0, 0))
    return pl.pallas_call(
        body, name="attn_fwd", grid=(HEADS,), in_specs=[blk, blk, blk], out_specs=[blk, blk],
        out_shape=[jax.ShapeDtypeStruct((HEADS, S, HP), F32), jax.ShapeDtypeStruct((HEADS, S, HP), F32)],
        compiler_params=_cparams(),
    )(q, k, v)


def attn_bwd(q, k, v, o, lse, do, S):
    nq = S // TA

    def body(q_ref, k_ref, v_ref, o_ref, lse_ref, do_ref, dq_ref, dk_ref, dv_ref):
        dk_ref[...] = jnp.zeros(dk_ref.shape, F32)
        dv_ref[...] = jnp.zeros(dv_ref.shape, F32)

        def q_block(qi, _):
            r0 = pl.multiple_of(qi * TA, TA)
            qb = q_ref[0, pl.ds(r0, TA), :]
            dof = do_ref[0, pl.ds(r0, TA), :]
            dob = dof.astype(BF)
            lse_b = lse_ref[0, pl.ds(r0, TA), :][:, 0:1]
            delta = jnp.sum(dof * o_ref[0, pl.ds(r0, TA), :], axis=-1, keepdims=True)

            def step(kj, dq, masked):
                c0 = pl.multiple_of(kj * TA, TA)
                kb = k_ref[0, pl.ds(c0, TA), :]
                vb = v_ref[0, pl.ds(c0, TA), :]
                s = lax.dot_general(qb, kb, NT_DIMS, preferred_element_type=F32) * SCALE
                if masked:
                    s = jnp.where(_causal_mask(), s, -jnp.inf)
                p = jnp.exp(s - lse_b)
                dp = lax.dot_general(dob, vb, NT_DIMS, preferred_element_type=F32)
                ds = (p * (dp - delta) * SCALE).astype(BF)
                dv_ref[0, pl.ds(c0, TA), :] += lax.dot_general(p.astype(BF), dob, TN_DIMS, preferred_element_type=F32)
                dk_ref[0, pl.ds(c0, TA), :] += lax.dot_general(ds, qb, TN_DIMS, preferred_element_type=F32)
                return dq + jnp.dot(ds, kb, preferred_element_type=F32)

            dq = lax.fori_loop(0, qi, lambda kj, c: step(kj, c, False), jnp.zeros((TA, HP), F32))
            dq_ref[0, pl.ds(r0, TA), :] = step(qi, dq, True)
            return 0

        lax.fori_loop(0, nq, q_block, 0)

    blk = pl.BlockSpec((1, S, HP), lambda h: (h, 0, 0))
    shp = jax.ShapeDtypeStruct((HEADS, S, HP), F32)
    return pl.pallas_call(
        body, name="attn_bwd", grid=(HEADS,), in_specs=[blk] * 6, out_specs=[blk] * 3, out_shape=[shp] * 3,
        compiler_params=_cparams(),
    )(q, k, v, o, lse, do)


def exchange(name, gathers, a2as):
    n_g, n = len(gathers), len(gathers) + len(a2as)

    def body(*refs):
        ins, outs = refs[:n], refs[n:2 * n]
        send_sems, recv_sems, loc_sems = refs[2 * n:]
        x, y, c = lax.axis_index("x"), lax.axis_index("y"), lax.axis_index("c")
        me = 4 * x + 2 * y + c

        def peer(k):
            px = 1 - x if k & 4 else x
            py = 1 - y if k & 2 else y
            pc = 1 - c if k & 1 else c
            return (px, py, pc), 4 * px + 2 * py + pc

        def remote(a, k):
            pid, pflat = peer(k)
            src = ins[a] if a < n_g else ins[a].at[pflat]
            return pltpu.make_async_remote_copy(
                src_ref=src, dst_ref=outs[a].at[me], send_sem=send_sems.at[a, k - 1], recv_sem=recv_sems.at[a, k - 1],
                device_id=pid, device_id_type=MESH)

        def arrival(a, k):
            pid, pflat = peer(k)
            src = ins[a] if a < n_g else ins[a].at[pflat]
            return pltpu.make_async_remote_copy(
                src_ref=src, dst_ref=outs[a].at[pflat], send_sem=send_sems.at[a, k - 1], recv_sem=recv_sems.at[a, k - 1],
                device_id=pid, device_id_type=MESH)

        local = []
        for a in range(n):
            own = ins[a] if a < n_g else ins[a].at[me]
            cp = pltpu.make_async_copy(own, outs[a].at[me], loc_sems.at[a])
            cp.start()
            local.append(cp)
        sent = []
        for k in (1, 2, 4, 3, 5, 6, 7):
            for a in range(n):
                cp = remote(a, k)
                cp.start()
                sent.append(cp)
        for k in range(1, 8):
            for a in range(n):
                arrival(a, k).wait_recv()
        for cp in sent:
            cp.wait_send()
        for cp in local:
            cp.wait()

    out_shape = [jax.ShapeDtypeStruct((NDEV,) + g.shape, g.dtype) for g in gathers]
    out_shape += [jax.ShapeDtypeStruct(a.shape, a.dtype) for a in a2as]
    any_spec = pl.BlockSpec(memory_space=pl.ANY)
    return pl.pallas_call(
        body, name=name, in_specs=[any_spec] * n, out_specs=[any_spec] * n, out_shape=out_shape,
        scratch_shapes=[pltpu.SemaphoreType.DMA((n, NDEV - 1)), pltpu.SemaphoreType.DMA((n, NDEV - 1)),
                        pltpu.SemaphoreType.DMA((n,))],
    )(*gathers, *a2as)


def _peer(k, x, y, c):
    px = 1 - x if k & 4 else x
    py = 1 - y if k & 2 else y
    pc = 1 - c if k & 1 else c
    return (px, py, pc), 4 * px + 2 * py + pc


PEER_ORDER = (1, 2, 4, 3, 5, 6, 7)
HBM_SPEC = pl.BlockSpec(memory_space=pltpu.HBM)
SEM_SPEC = pl.BlockSpec(memory_space=pltpu.SEMAPHORE)
ANY_SPEC = pl.BlockSpec(memory_space=pl.ANY)


def _split_copies(ins, lands, n_g, send_sems, recv_sems):
    x, y, c = lax.axis_index("x"), lax.axis_index("y"), lax.axis_index("c")
    me = 4 * x + 2 * y + c

    def outgoing(a, k):
        pid, pflat = _peer(k, x, y, c)
        src = ins[a] if a < n_g else ins[a].at[pflat]
        return pltpu.make_async_remote_copy(
            src_ref=src, dst_ref=lands[a].at[me], send_sem=send_sems.at[a * (NDEV - 1) + k - 1],
            recv_sem=recv_sems.at[a * (NDEV - 1) + k - 1],
            device_id=pid, device_id_type=MESH)

    def arrival(a, k):
        pid, pflat = _peer(k, x, y, c)
        src = ins[a] if a < n_g else ins[a].at[pflat]
        return pltpu.make_async_remote_copy(
            src_ref=src, dst_ref=lands[a].at[pflat], send_sem=send_sems.at[a * (NDEV - 1) + k - 1],
            recv_sem=recv_sems.at[a * (NDEV - 1) + k - 1],
            device_id=pid, device_id_type=MESH)

    return outgoing, arrival


def exchange_begin(name, srcs, n_g, dep):
    n = len(srcs)
    land_shapes = [((NDEV,) + s.shape) if a < n_g else s.shape for a, s in enumerate(srcs)]

    def own_body(*refs):
        ins, outs, sems = refs[:n], refs[n + 1:2 * n + 1], refs[-1]
        me = 4 * lax.axis_index("x") + 2 * lax.axis_index("y") + lax.axis_index("c")
        cps = []
        for a in range(n):
            cp = pltpu.make_async_copy(ins[a] if a < n_g else ins[a].at[me], outs[a].at[me], sems.at[a])
            cp.start()
            cps.append(cp)
        for cp in cps:
            cp.wait()

    lands = pl.pallas_call(
        own_body, name=name + "_own", in_specs=[ANY_SPEC] * (n + 1), out_specs=[ANY_SPEC] * n,
        out_shape=[jax.ShapeDtypeStruct(sh, s.dtype) for sh, s in zip(land_shapes, srcs)],
        scratch_shapes=[pltpu.SemaphoreType.DMA((n,))],
    )(*srcs, dep)

    def start_body(*refs):
        ins, lz = refs[:n], refs[n:2 * n]
        send_sems, recv_sems, token = refs[2 * n], refs[2 * n + 1], refs[-1]
        outgoing, _ = _split_copies(ins, lz, n_g, send_sems, recv_sems)
        for k in PEER_ORDER:
            for a in range(n):
                outgoing(a, k).start()
        token[...] = jnp.zeros(token.shape, F32)

    hbm = lambda t: pltpu.HBM(t.shape, t.dtype)
    res = pl.pallas_call(
        start_body, name=name + "_start",
        out_shape=(pltpu.SemaphoreType.DMA((n * (NDEV - 1),)), pltpu.SemaphoreType.DMA((n * (NDEV - 1),)),
                   *[hbm(s) for s in srcs], *[hbm(t) for t in lands], jax.ShapeDtypeStruct((8, LANE), F32)),
        in_specs=[HBM_SPEC] * (2 * n),
        out_specs=(SEM_SPEC, SEM_SPEC, *[HBM_SPEC] * (2 * n), pl.BlockSpec(memory_space=pltpu.VMEM)),
        input_output_aliases={i: 2 + i for i in range(2 * n)},
        compiler_params=pltpu.CompilerParams(has_side_effects=pltpu.SideEffectType.DATAFLOW_SIDE_EFFECTING),
    )(*[pltpu.with_memory_space_constraint(t, pltpu.HBM) for t in list(srcs) + list(lands)])
    return (name, n, n_g, res[:-1]), res[-1]


def exchange_end(handle, after):
    name, n, n_g, (send_sems, recv_sems, *bufs) = handle

    def wait_body(*refs):
        ins, lz = refs[:n], refs[n:2 * n]
        ss, rs = refs[2 * n], refs[2 * n + 1]
        outgoing, arrival = _split_copies(ins, lz, n_g, ss, rs)
        for k in range(1, NDEV):
            for a in range(n):
                arrival(a, k).wait_recv()
        for k in range(1, NDEV):
            for a in range(n):
                outgoing(a, k).wait_send()

    res = pl.pallas_call(
        wait_body, name=name + "_wait", out_shape=tuple(pltpu.HBM(t.shape, t.dtype) for t in bufs),
        in_specs=[HBM_SPEC] * (2 * n) + [SEM_SPEC, SEM_SPEC, ANY_SPEC], out_specs=[HBM_SPEC] * (2 * n),
        input_output_aliases={i: i for i in range(2 * n)},
        compiler_params=pltpu.CompilerParams(has_side_effects=pltpu.SideEffectType.DATAFLOW_SIDE_EFFECTING),
    )(*bufs, send_sems, recv_sems, after)
    return list(res[n:])


def _pick_rows(R, mult, cap):
    best = None
    for n in range(1, R + 1):
        if R % n == 0 and (R // n) % mult == 0 and R // n <= cap:
            best = R // n
            break
    assert best is not None, (R, mult, cap)
    return best


def sum_slots(name, x):
    _, R, _ = x.shape
    tr = _pick_rows(R, 16, 2304)

    def body(x_ref, o_ref):
        acc = x_ref[0].astype(F32)
        for d in range(1, NDEV):
            acc = acc + x_ref[d].astype(F32)
        o_ref[...] = acc

    return pl.pallas_call(
        body, name=name, grid=(R // tr,),
        in_specs=[pl.BlockSpec((NDEV, tr, LANE), lambda i: (0, i, 0))],
        out_specs=pl.BlockSpec((tr, LANE), lambda i: (i, 0)),
        out_shape=jax.ShapeDtypeStruct((R, LANE), F32), compiler_params=_cparams(),
    )(x)


def adamw(name, w, g, m, v):
    L, R, C = w.shape
    tr = _pick_rows(R, 8, 256) if R % 8 == 0 else R

    def body(w_ref, g_ref, m_ref, v_ref, d_ref, nm_ref, nv_ref):
        gg = g_ref[...]
        nm = B1 * m_ref[...] + (1.0 - B1) * gg
        nv = B2 * v_ref[...] + (1.0 - B2) * jnp.square(gg)
        m_hat = nm / (1.0 - B1 ** STEP)
        v_hat = nv / (1.0 - B2 ** STEP)
        d_ref[...] = -LR * (m_hat / (jnp.sqrt(v_hat) + EPS) + WD * w_ref[...])
        nm_ref[...] = nm
        nv_ref[...] = nv

    blk = pl.BlockSpec((1, tr, C), lambda l, i: (l, i, 0))
    shp = jax.ShapeDtypeStruct(w.shape, F32)
    return pl.pallas_call(
        body, name=name, grid=(L, R // tr), in_specs=[blk] * 4, out_specs=[blk] * 3, out_shape=[shp] * 3,
        compiler_params=_cparams(),
    )(w, g, m, v)


IN_SHARD = D_IN // NDEV
UQ_SHARD = HEADS * (NOPE + ROPE) // NDEV
W_IN_PAD = 1024
ROW_A, ROW_B, ROW_C, ROW_UKV, ROW_UQ, MISC_ROWS = 0, 512, 1024, 1536, 1792, 2176


def _in_perm_index():
    ar = np.arange
    z = lambda n: np.full((n,), -1, np.int64)
    return np.concatenate([ar(4768, 7840), ar(0, 3584), ar(4256, 4768), ar(3968, 4224), ar(4224, 4256),
                           z(OFF_Q - OFF_KR - ROPE), ar(3584, 3968), z(NP - OFF_Q - QL)])


def _head_perm_index(a, b):
    h = np.arange(HEADS)[:, None] * (a + b)
    return np.concatenate([(h + np.arange(a)[None]).reshape(-1), (h + a + np.arange(b)[None]).reshape(-1)])


def _inverse(perm, n):
    inv = np.full((n,), -1, np.int64)
    inv[perm[perm >= 0]] = np.nonzero(perm >= 0)[0]
    return inv


IN_PERM = _in_perm_index()
UQ_PERM = _head_perm_index(NOPE, ROPE)
UKV_PERM = _head_perm_index(NOPE, VH)


def _to_gathered(perm, shard, pad):
    return np.where(perm >= 0, (perm // shard) * pad + perm % shard, -1)


def _from_full(inv, shard, pad):
    j, i = np.divmod(np.arange(NDEV * pad), pad)
    return np.where(i < shard, inv[np.minimum(j * shard + i, inv.shape[0] - 1)], -1)


def col_gather(name, srcs, out_shapes, jobs, deps=()):
    ns, nj, nd = len(srcs), len(jobs), len(deps)
    tables = [jnp.asarray(np.asarray(job[5], np.int32)[None, :]) for job in jobs]

    def view(ref, col0, width, r0, rc):
        n = ref.shape[-1]
        if len(ref.shape) == 3:
            return ref.at[col0 // n, pl.ds(r0, rc), pl.ds(col0 % n, width)]
        return ref.at[pl.ds(r0, rc), pl.ds(col0, width)]

    def body(*refs):
        src_refs, tab_refs, out_refs = refs[:ns], refs[ns:ns + nj], refs[ns + nj + nd:]
        for ji, (si, srow, oi, orow, nrows, tgt) in enumerate(jobs):
            sref, oref = src_refs[si], out_refs[oi]
            tgt = np.asarray(tgt)
            tw = 256 if oref.shape[-1] % 256 == 0 else LANE
            rc = 256 if nrows % 256 == 0 else LANE
            for t in range(tgt.shape[0] // tw):
                tt = tgt[t * tw:(t + 1) * tw]
                tiles = sorted(set((tt[tt >= 0] // LANE).tolist()))
                straight = bool(tiles) and np.array_equal(tt, np.arange(tiles[0] * LANE, tiles[0] * LANE + tw))
                onehots = []
                if tiles and not straight:
                    want = tab_refs[ji][:, t * tw:(t + 1) * tw]
                    row = lax.broadcasted_iota(jnp.int32, (LANE, tw), 0)
                    onehots = [jnp.where(want == row + s * LANE, 1.0, 0.0).astype(BF) for s in tiles]

                def chunk(ci, _, t=t, tiles=tiles, straight=straight, onehots=onehots):
                    r0 = ci * rc
                    dst = view(oref, t * tw, tw, pl.multiple_of(orow + r0, LANE), rc)
                    rs = pl.multiple_of(srow + r0, LANE)
                    if not tiles:
                        dst[...] = jnp.zeros((rc, tw), BF)
                    elif straight:
                        for k in range(tw // LANE):
                            view(oref, t * tw + k * LANE, LANE, pl.multiple_of(orow + r0, LANE), rc)[...] = (
                                view(sref, (tiles[0] + k) * LANE, LANE, rs, rc)[...])
                    else:
                        acc = None
                        for s, oh in zip(tiles, onehots):
                            p = jnp.dot(view(sref, s * LANE, LANE, rs, rc)[...], oh, preferred_element_type=F32)
                            acc = p if acc is None else acc + p
                        dst[...] = acc.astype(BF)
                    return 0

                lax.fori_loop(0, nrows // rc, chunk, 0)

    vmem = pl.BlockSpec(memory_space=pltpu.VMEM)
    return pl.pallas_call(
        body, name=name, in_specs=[vmem] * (ns + nj) + [ANY_SPEC] * nd, out_specs=[vmem] * len(out_shapes),
        out_shape=[jax.ShapeDtypeStruct(s, BF) for s in out_shapes], compiler_params=_cparams(),
    )(*srcs, *tables, *deps)


def sum_adamw(name, recvs, w, m, v):
    L, R, C = w.shape
    CP = recvs[0].shape[-1]
    tr = _pick_rows(R, 16, 128)

    def body(*refs):
        r_refs = refs[:L]
        w_ref, m_ref, v_ref, g_ref, d_ref, nm_ref, nv_ref, gsum = refs[L:]
        layer = pl.program_id(0)
        for k in range(L):
            def total(k=k):
                acc = r_refs[k][0].astype(F32)
                for d in range(1, NDEV):
                    acc = acc + r_refs[k][d].astype(F32)
                gsum[...] = acc
            pl.when(layer == k)(total)
        gg = gsum[:, 0:C]
        nm = B1 * m_ref[...] + (1.0 - B1) * gg
        nv = B2 * v_ref[...] + (1.0 - B2) * jnp.square(gg)
        m_hat = nm / (1.0 - B1 ** STEP)
        v_hat = nv / (1.0 - B2 ** STEP)
        g_ref[...] = gg
        d_ref[...] = -LR * (m_hat / (jnp.sqrt(v_hat) + EPS) + WD * w_ref[...])
        nm_ref[...] = nm
        nv_ref[...] = nv

    r_specs = [pl.BlockSpec((NDEV, tr, CP), functools.partial(lambda l, i, k: (0, jnp.where(l == k, i, 0), 0), k=k))
               for k in range(L)]
    blk = pl.BlockSpec((None, tr, C), lambda l, i: (l, i, 0))
    shp = jax.ShapeDtypeStruct(w.shape, F32)
    return pl.pallas_call(
        body, name=name, grid=(L, R // tr), in_specs=r_specs + [blk] * 3, out_specs=[blk] * 4, out_shape=[shp] * 4,
        scratch_shapes=[pltpu.VMEM((tr, CP), F32)], compiler_params=_cparams(),
    )(*recvs, w, m, v)


def _heads(x, width):
    S = x.shape[0]
    x = x.reshape(S, HEADS, width).transpose(1, 0, 2)
    return jnp.pad(x, ((0, 0), (0, 0), (0, HP - width)))


ALPHA = 8.0 ** 0.25


def _rope_fn(sign):
    def fn(x, cos, sin):
        W = x.shape[-1]
        lane = lax.broadcasted_iota(jnp.int32, x.shape, 1)
        first_half = (lane % ROPE) < (ROPE // 2)
        rot = jnp.where(first_half, -pltpu.roll(x, W - ROPE // 2, 1), pltpu.roll(x, ROPE // 2, 1))
        return x * cos + sign * rot * sin
    return fn


def layer_fwd(x, ada3, W, tabs, S):
    cos, sin = tabs
    T = 256
    u = rowwise("modulate", lambda xv, a: xv * (1.0 + a[1:2, :]) + a[0:1, :], S, T,
                [(x, D_MODEL, 0)], [ada3], [(D_MODEL, BF)])[0]
    proj = mm(u, W["in"], name="mm_proj", tm=1024, tn=1024)

    ca = conv_fwd("conv_a_fwd", proj, OFF_A1, OFF_A2, W["conv_a"], 31, "glu", S, CW)

    def a_post(c, ag, vec):
        n, _ = _ln_stats(c + vec[0:1, :])
        return _silu(n * vec[1:2, :] + vec[2:3, :]) * _silu(ag)

    h_a = rowwise("mix_a_post", a_post, S, T, [(ca, CW, 0), (proj, CW, OFF_AG)], [W["vec_a"]], [(CW, BF)])[0]
    y_a = mm(h_a, W["a_out"], name="mm_branch_out")

    cb = conv_fwd("conv_b_fwd", proj, OFF_XB, OFF_GC, W["conv_b"], 3, "mul", S, CW)
    h_b = rowwise("mix_b_post", lambda c, gb, bg: gb * c * _silu(bg), S, T,
                  [(cb, CW, 0), (proj, CW, OFF_GB), (proj, CW, OFF_BG)], [], [(CW, BF)])[0]
    y_b = mm(h_b, W["b_out"], name="mm_branch_out")

    def rms2(ql, kvl, gq, gkv):
        rq = lax.rsqrt(jnp.mean(ql * ql, axis=-1, keepdims=True) + RMS_EPS)
        rk = lax.rsqrt(jnp.mean(kvl * kvl, axis=-1, keepdims=True) + RMS_EPS)
        return ql * rq * gq, kvl * rk * gkv

    qn, kvn = rowwise("rms_fwd", rms2, S, T, [(proj, QL, OFF_Q), (proj, KVL, OFF_KV)], [W["gq"], W["gkv"]],
                      [(QL, BF), (KVL, BF)])
    q = mm(qn, W["uq"], name="mm_q")
    kv = mm(kvn, W["ukv"], name="mm_kv")
    rope = _rope_fn(1.0)
    q_pe, k_pe = rowwise("rope_fwd", lambda a, b, c1, s1: (rope(a, c1, s1), rope(b, c1[:, :LANE], s1[:, :LANE])),
                         S, T, [(q, 256, 512), (proj, LANE, OFF_KR), (cos, 256, 0), (sin, 256, 0)], [],
                         [(256, BF), (LANE, BF)])
    q_h = _heads(jnp.concatenate([q[:, :512].astype(BF).reshape(S, HEADS, NOPE), q_pe.reshape(S, HEADS, ROPE)],
                                 axis=2).reshape(S, HEADS * 96), 96)
    k_pe_b = jnp.broadcast_to(k_pe[:, None, :ROPE], (S, HEADS, ROPE))
    k_h = _heads(jnp.concatenate([kv[:, :512].astype(BF).reshape(S, HEADS, NOPE), k_pe_b], axis=2)
                 .reshape(S, HEADS * 96), 96)
    v_h = _heads(kv[:, 512:].astype(BF), VH)
    o_h, lse = attn_fwd(q_h, k_h, v_h, S)
    o = o_h[:, :, :VH].transpose(1, 0, 2).reshape(S, HEADS * VH)
    h_c = rowwise("mix_c_post", lambda ov, cg: ov * _silu(cg), S, T, [(o, CW, 0), (proj, CW, OFF_CG)], [],
                  [(CW, BF)])[0]
    y_c = mm(h_c, W["c_out"], name="mm_branch_out")

    def merge(la, lb, lc, ya, yb, yc):
        return _sigmoid(la) * ya + _sigmoid(lb) * yb + _sigmoid(lc) * yc

    m = rowwise("merge_fwd", merge, S, 128,
                [(proj, D_MODEL, 0), (proj, D_MODEL, 1024), (proj, D_MODEL, 2048), (y_a, D_MODEL, 0),
                 (y_b, D_MODEL, 0), (y_c, D_MODEL, 0)], [], [(D_MODEL, BF)])[0]
    out = mm(m, W["o"], name="mm_out")

    def ln_fwd(xv, ov, a, lnv):
        n, _ = _ln_stats(ALPHA * xv + a[2:3, :] * ov)
        return n * lnv[0:1, :] + lnv[1:2, :]

    x_next = rowwise("ln_fwd", ln_fwd, S, 128, [(x, D_MODEL, 0), (out, D_MODEL, 0)], [ada3, W["lnv"]],
                     [(D_MODEL, F32)])[0]
    saved = dict(x=x, u=u, proj=proj, ca=ca, cb=cb, h_a=h_a, h_b=h_b, h_c=h_c, y_a=y_a, y_b=y_b, y_c=y_c, qn=qn,
                 kvn=kvn, q_h=q_h, k_h=k_h, v_h=v_h, o_h=o_h, lse=lse, o=o, m=m, out=out)
    return x_next, saved


def layer_bwd(dxn, sv, ada3, W, tabs, S):
    cos, sin = tabs
    T = 256
    x, proj = sv["x"], sv["proj"]
    G = {}

    def ln_bwd(xv, ov, dy, a, lnv):
        gate = a[2:3, :]
        n, rstd = _ln_stats(ALPHA * xv + gate * ov)
        dr = _ln_bwd(dy * lnv[0:1, :], n, rstd)
        return ALPHA * dr, gate * dr, _colsum(dy * n), _colsum(dy), _colsum(dr * ov)

    dres, d_out, G["ln_g"], G["ln_b"], d_gate = rowwise(
        "ln_bwd", ln_bwd, S, 128, [(x, D_MODEL, 0), (sv["out"], D_MODEL, 0), (dxn, D_MODEL, 0)], [ada3, W["lnv"]],
        [(D_MODEL, F32), (D_MODEL, BF)], [D_MODEL] * 3)
    dm = mm(d_out, W["o"], name="mm_dm", trans_b=True)
    G["w_o"] = mm(sv["m"], d_out, name="mm_gw_o", trans_a=True, out_dtype=BF)

    def merge_bwd(dmv, la, lb, lc, ya, yb, yc):
        outs, dls = [], []
        for lg, yv in ((la, ya), (lb, yb), (lc, yc)):
            s = _sigmoid(lg)
            outs.append(dmv * s)
            dls.append(dmv * yv * s * (1.0 - s))
        return tuple(outs) + tuple(dls)

    dy_a, dy_b, dy_c, dl_a, dl_b, dl_c = rowwise(
        "merge_bwd", merge_bwd, S, 128,
        [(dm, D_MODEL, 0), (proj, D_MODEL, 0), (proj, D_MODEL, 1024), (proj, D_MODEL, 2048), (sv["y_a"], D_MODEL, 0),
         (sv["y_b"], D_MODEL, 0), (sv["y_c"], D_MODEL, 0)], [], [(D_MODEL, BF)] * 6)

    dh = {}
    for br, dy in (("a", dy_a), ("b", dy_b), ("c", dy_c)):
        dh[br] = mm(dy, W[br + "_out"], name="mm_dh", trans_b=True)
        G["w_%s_out" % br] = mm(sv["h_" + br], dy, name="mm_gw_branch", trans_a=True, out_dtype=BF)

    def a_post_bwd(c, ag, dhv, vec):
        n, rstd = _ln_stats(c + vec[0:1, :])
        z = n * vec[1:2, :] + vec[2:3, :]
        d_ag = dhv * _silu(z) * _dsilu(ag)
        dz = dhv * _silu(ag) * _dsilu(z)
        dc = _ln_bwd(dz * vec[1:2, :], n, rstd)
        return dc, d_ag, _colsum(dc), _colsum(dz * n), _colsum(dz)

    dca, d_ag, G["conv_a_b"], G["ln_a_g"], G["ln_a_b"] = rowwise(
        "mix_a_post_bwd", a_post_bwd, S, T, [(sv["ca"], CW, 0), (proj, CW, OFF_AG), (dh["a"], CW, 0)], [W["vec_a"]],
        [(CW, F32), (CW, BF)], [CW] * 3)
    d_a1, d_a2, G["conv_a_w"] = conv_bwd("conv_a_bwd", proj, OFF_A1, OFF_A2, dca, W["conv_a"], 31, "glu", S, CW)

    def b_post_bwd(c, gb, bg, dhv):
        sg = _silu(bg)
        return dhv * sg * gb, dhv * sg * c, dhv * gb * c * _dsilu(bg)

    dcb, d_gb, d_bg = rowwise("mix_b_post_bwd", b_post_bwd, S, T,
                              [(sv["cb"], CW, 0), (proj, CW, OFF_GB), (proj, CW, OFF_BG), (dh["b"], CW, 0)], [],
                              [(CW, F32), (CW, BF), (CW, BF)])
    d_xb, d_gc, G["conv_b_w"] = conv_bwd("conv_b_bwd", proj, OFF_XB, OFF_GC, dcb, W["conv_b"], 3, "mul", S, CW)

    d_o, d_cg = rowwise("mix_c_post_bwd", lambda ov, cg, dhv: (dhv * _silu(cg), dhv * ov * _dsilu(cg)), S, T,
                        [(sv["o"], CW, 0), (proj, CW, OFF_CG), (dh["c"], CW, 0)], [], [(CW, F32), (CW, BF)])
    dq_h, dk_h, dv_h = attn_bwd(sv["q_h"], sv["k_h"], sv["v_h"], sv["o_h"], sv["lse"], _heads(d_o, VH), S)
    unheads = lambda t, lo, hi: t[:, :, lo:hi].transpose(1, 0, 2).reshape(S, HEADS * (hi - lo))
    ropeT = _rope_fn(-1.0)

    def rope_bwd(dqp, dkp, c1, s1):
        f = dkp[:, :LANE] + dkp[:, LANE:]
        f = f + pltpu.roll(f, 64, 1)
        f = f + pltpu.roll(f, 32, 1)
        return ropeT(dqp, c1, s1), ropeT(f, c1[:, :LANE], s1[:, :LANE])

    dq_pe, dk_pe = rowwise("rope_bwd", rope_bwd, S, T,
                           [(unheads(dq_h, NOPE, 96), 256, 0), (unheads(dk_h, NOPE, 96), 256, 0), (cos, 256, 0),
                            (sin, 256, 0)], [], [(256, BF), (LANE, BF)])
    d_q = jnp.concatenate([unheads(dq_h, 0, NOPE).astype(BF), dq_pe], axis=1)
    d_kv = jnp.concatenate([unheads(dk_h, 0, NOPE), unheads(dv_h, 0, VH)], axis=1).astype(BF)
    d_qn = mm(d_q, W["uq"], name="mm_dqn", trans_b=True)
    d_kvn = mm(d_kv, W["ukv"], name="mm_dkvn", trans_b=True)
    G["w_uq"] = mm(sv["qn"], d_q, name="mm_gw_uq", trans_a=True, out_dtype=BF)
    G["w_ukv"] = mm(sv["kvn"], d_kv, name="mm_gw_ukv", trans_a=True, out_dtype=BF)

    def rms_bwd(ql, kvl, dqn, dkn, gq, gkv):
        res = []
        for xv, dy, g in ((ql, dqn, gq), (kvl, dkn, gkv)):
            r = lax.rsqrt(jnp.mean(xv * xv, axis=-1, keepdims=True) + RMS_EPS)
            dxh = dy * g
            res.append((r * (dxh - xv * (r * r) * jnp.mean(dxh * xv, axis=-1, keepdims=True)), _colsum(dy * xv * r)))
        return res[0][0], res[1][0], res[0][1], res[1][1]

    d_ql, d_kvl, G["q_norm_g"], G["kv_norm_g"] = rowwise(
        "rms_bwd", rms_bwd, S, T, [(proj, QL, OFF_Q), (proj, KVL, OFF_KV), (d_qn, QL, 0), (d_kvn, KVL, 0)],
        [W["gq"], W["gkv"]], [(QL, BF), (KVL, BF)], [QL, KVL])

    zeros = lambda n: jnp.zeros((S, n), BF)
    d_proj = jnp.concatenate([dl_a, dl_b, dl_c, d_a1, d_a2, d_ag, d_xb, d_gb, d_gc, d_bg, d_cg, d_kvl,
                              dk_pe[:, :ROPE], zeros(OFF_Q - OFF_KR - ROPE), d_ql, zeros(NP - OFF_Q - QL)], axis=1)
    du = mm(d_proj, W["in"], name="mm_du", trans_b=True, tk=1024)
    G["w_in"] = mm(sv["u"], d_proj, name="mm_gw_in", trans_a=True, out_dtype=BF)

    def mod_bwd(duv, xv, dr, a):
        return duv * (1.0 + a[1:2, :]) + dr, _colsum(duv), _colsum(duv * xv)

    dx, d_shift, d_scale = rowwise("mod_bwd", mod_bwd, S, 128, [(du, D_MODEL, 0), (x, D_MODEL, 0), (dres, D_MODEL, 0)],
                                   [ada3], [(D_MODEL, F32)], [D_MODEL] * 2)
    d_ada = jnp.concatenate([d_shift, d_scale, d_gate], axis=1)
    return dx, G, d_ada


SMALL = ("conv_a_b", "ln_a_g", "ln_a_b", "q_norm_g", "kv_norm_g", "ln_g", "ln_b")


def _rows(v):
    n = v.shape[0]
    r = -(-n // (LANE * 16)) * 16
    return jnp.pad(v, (0, r * LANE - n)).reshape(r, LANE)


def kernel(x, c, positions, w_ada, b_ada, w_in, conv_a_w, conv_a_b, ln_a_g, ln_a_b, w_a_out, conv_b_w, w_b_out, q_norm_g, kv_norm_g, w_uq, w_ukv, w_c_out, w_o, ln_g, ln_b, loss_target, m_w_ada, m_b_ada, m_w_in, m_conv_a_w, m_conv_a_b, m_ln_a_g, m_ln_a_b, m_w_a_out, m_conv_b_w, m_w_b_out, m_q_norm_g, m_kv_norm_g, m_w_uq, m_w_ukv, m_w_c_out, m_w_o, m_ln_g, m_ln_b, v_w_ada, v_b_ada, v_w_in, v_conv_a_w, v_conv_a_b, v_ln_a_g, v_ln_a_b, v_w_a_out, v_conv_b_w, v_w_b_out, v_q_norm_g, v_kv_norm_g, v_w_uq, v_w_ukv, v_w_c_out, v_w_o, v_ln_g, v_ln_b):
    P = dict(w_ada=w_ada, b_ada=b_ada, w_in=w_in, conv_a_w=conv_a_w, conv_a_b=conv_a_b, ln_a_g=ln_a_g, ln_a_b=ln_a_b,
             w_a_out=w_a_out, conv_b_w=conv_b_w, w_b_out=w_b_out, q_norm_g=q_norm_g, kv_norm_g=kv_norm_g, w_uq=w_uq,
             w_ukv=w_ukv, w_c_out=w_c_out, w_o=w_o, ln_g=ln_g, ln_b=ln_b)
    Mo = dict(w_ada=m_w_ada, b_ada=m_b_ada, w_in=m_w_in, conv_a_w=m_conv_a_w, conv_a_b=m_conv_a_b, ln_a_g=m_ln_a_g,
              ln_a_b=m_ln_a_b, w_a_out=m_w_a_out, conv_b_w=m_conv_b_w, w_b_out=m_w_b_out, q_norm_g=m_q_norm_g,
              kv_norm_g=m_kv_norm_g, w_uq=m_w_uq, w_ukv=m_w_ukv, w_c_out=m_w_c_out, w_o=m_w_o, ln_g=m_ln_g, ln_b=m_ln_b)
    Vo = dict(w_ada=v_w_ada, b_ada=v_b_ada, w_in=v_w_in, conv_a_w=v_conv_a_w, conv_a_b=v_conv_a_b, ln_a_g=v_ln_a_g,
              ln_a_b=v_ln_a_b, w_a_out=v_w_a_out, conv_b_w=v_conv_b_w, w_b_out=v_w_b_out, q_norm_g=v_q_norm_g,
              kv_norm_g=v_kv_norm_g, w_uq=v_w_uq, w_ukv=v_w_ukv, w_c_out=v_w_c_out, w_o=v_w_o, ln_g=v_ln_g, ln_b=v_ln_b)
    ORDER = ("w_ada", "b_ada", "w_in", "conv_a_w", "conv_a_b", "ln_a_g", "ln_a_b", "w_a_out", "conv_b_w", "w_b_out",
             "q_norm_g", "kv_norm_g", "w_uq", "w_ukv", "w_c_out", "w_o", "ln_g", "ln_b")
    L = w_ada.shape[0]
    S = x.shape[1]
    me = 4 * lax.axis_index("x") + 2 * lax.axis_index("y") + lax.axis_index("c")
    x2 = x[0]
    tgt = loss_target[0]

    small_in = _rows(jnp.concatenate([c.reshape(-1), conv_a_w.reshape(-1), conv_b_w.reshape(-1)]))
    w_in_b = jnp.pad(w_in.astype(BF), ((0, 0), (0, 0), (0, W_IN_PAD - IN_SHARD)))
    misc_b = jnp.concatenate([w_a_out, w_b_out, w_c_out, w_ukv, jnp.pad(w_uq, ((0, 0), (0, 0), (0, LANE - UQ_SHARD)))],
                             axis=1).astype(BF)
    w_o_b = w_o.astype(BF)
    gathered = [None] * L
    pending, _ = exchange_begin("gather0", [w_in_b[0], misc_b[0], w_o_b[0]], 3, small_in)
    sg = exchange("gather_small", [small_in], [])[0]
    sgf = sg.reshape(NDEV, -1)
    c_all = sgf[:, :D_MODEL]
    o1 = D_MODEL + L * 31 * 64
    conv_a_full = sgf[:, D_MODEL:o1].reshape(NDEV, L, 31, 64).transpose(1, 2, 0, 3).reshape(L, 31, CW)
    conv_b_full = sgf[:, o1:o1 + L * 3 * 64].reshape(NDEV, L, 3, 64).transpose(1, 2, 0, 3).reshape(L, 3, CW)

    c_act = rowwise("silu_c", _silu, 16, 16, [(jnp.pad(c_all, ((0, 8), (0, 0))), D_MODEL, 0)], [], [(D_MODEL, BF)])[0]
    ncol = w_ada.shape[2]
    w_ada_b = w_ada.astype(BF).transpose(1, 0, 2).reshape(D_MODEL, L * ncol)
    b_mine = lax.dynamic_slice_in_dim(b_ada, me * ncol, ncol, axis=1).reshape(1, L * ncol)
    ada_part = mm(c_act, w_ada_b, name="mm_ada", bias=b_mine)
    ada_rows = -(-(L * ncol) // (LANE * 8)) * 8
    ada_send = jnp.pad(ada_part[:NDEV].reshape(NDEV, -1, LANE), ((0, 0), (0, ada_rows - L * ncol // LANE), (0, 0)))
    ada_recv = exchange("a2a_ada", [], [ada_send])[0]
    ada = ada_recv[:, :L * ncol // LANE].reshape(NDEV, L, ncol).transpose(1, 0, 2).reshape(L, 3, D_MODEL)

    inv_freq = ROPE_THETA ** (-jnp.arange(0, ROPE, 2, dtype=F32) / ROPE)
    ang = positions[0].astype(F32)[:, None] * inv_freq
    tabs = (jnp.tile(jnp.cos(ang), (1, 2 * HEADS)), jnp.tile(jnp.sin(ang), (1, 2 * HEADS)))

    straight = np.arange(D_MODEL)
    fwd_in = [(0, 0, 0, 0, D_MODEL, _to_gathered(IN_PERM, IN_SHARD, W_IN_PAD))]
    fwd_misc = [(0, ROW_A, 0, 0, CW, straight), (0, ROW_B, 1, 0, CW, straight), (0, ROW_C, 2, 0, CW, straight),
                (0, ROW_UKV, 3, 0, KVL, UKV_PERM), (0, ROW_UQ, 4, 0, QL, _to_gathered(UQ_PERM, UQ_SHARD, LANE))]
    rev_in = [(0, 0, 0, 0, D_MODEL, _from_full(_inverse(IN_PERM, D_IN), IN_SHARD, W_IN_PAD))]
    rev_misc = [(0, 0, 0, ROW_A, CW, straight), (1, 0, 0, ROW_B, CW, straight), (2, 0, 0, ROW_C, CW, straight),
                (3, 0, 0, ROW_UKV, KVL, _from_full(_inverse(UKV_PERM, HEADS * (NOPE + VH)), LANE, LANE)),
                (4, 0, 0, ROW_UQ, QL, _from_full(_inverse(UQ_PERM, HEADS * (NOPE + ROPE)), UQ_SHARD, LANE))]

    def layer_weights(l, deps):
        g_in, g_misc, g_o = gathered[l]
        w_in_p = col_gather("relayout_w_in", [g_in], [(D_MODEL, NP)], fwd_in, deps)[0]
        a_out, b_out, c_out, ukv, uq = col_gather(
            "relayout_misc", [g_misc], [(CW, D_MODEL)] * 3 + [(KVL, HEADS * (NOPE + VH)), (QL, HEADS * (NOPE + ROPE))],
            fwd_misc, deps)
        return {
            "in": w_in_p, "a_out": a_out, "b_out": b_out, "c_out": c_out, "uq": uq, "ukv": ukv,
            "o": g_o.reshape(D_MODEL, D_MODEL),
            "conv_a": jnp.pad(conv_a_full[l], ((0, 1), (0, 0))), "conv_b": jnp.pad(conv_b_full[l], ((0, 5), (0, 0))),
            "vec_a": jnp.stack([conv_a_b[l], ln_a_g[l], ln_a_b[l]]), "gq": q_norm_g[l][None], "gkv": kv_norm_g[l][None],
            "lnv": jnp.stack([ln_g[l], ln_b[l]]),
        }

    h = x2
    saved, weights = [], []
    gathered[0] = exchange_end(pending, ada)
    for l in range(L):
        ada_l, deps = ada[l], ()
        if l + 1 < L:
            pending, token = exchange_begin("gather%d" % (l + 1), [w_in_b[l + 1], misc_b[l + 1], w_o_b[l + 1]], 3,
                                            gathered[l][0])
            ada_l, deps = ada_l + token[0, 0], (token,)
        Wl = layer_weights(l, deps)
        h, sv = layer_fwd(h, ada_l, Wl, tabs, S)
        if l + 1 < L:
            gathered[l + 1] = exchange_end(pending, h)
        saved.append(sv)
        weights.append(Wl)

    def loss_fn(y, t):
        e = y - t
        return e * (1.0 / D_MODEL), _colsum(e * e)

    dy, sq = rowwise("loss", loss_fn, S, 256, [(h, D_MODEL, 0), (tgt, D_MODEL, 0)], [], [(D_MODEL, F32)], [D_MODEL])
    loss = lax.psum(0.5 * jnp.sum(sq) / D_MODEL, ("x", "y", "c"))

    grads, d_adas, recv = [None] * L, [None] * L, [None] * L
    pending, token = None, None
    for l in reversed(range(L)):
        ada_l = ada[l] if token is None else ada[l] + token[0, 0]
        dy, g, d_adas[l] = layer_bwd(dy, saved[l], ada_l, weights[l], tabs, S)
        grads[l] = g
        if pending is not None:
            recv[l + 1] = exchange_end(pending, dy)
        send_in = col_gather("unrelayout_w_in", [g["w_in"]], [(NDEV, D_MODEL, W_IN_PAD)], rev_in)[0]
        send_misc = col_gather("unrelayout_misc", [g["w_a_out"], g["w_b_out"], g["w_c_out"], g["w_ukv"], g["w_uq"]],
                               [(NDEV, MISC_ROWS, LANE)], rev_misc)[0]
        send_o = g["w_o"].reshape(NDEV, D_MODEL // NDEV, D_MODEL)
        pending, token = exchange_begin("scatter%d" % l, [send_in, send_misc, send_o], 0,
                                        dy if l + 1 == L else recv[l + 1][0])
    grad_x = dy[None]

    small_parts = [jnp.stack([grads[l][n].reshape(-1) for l in range(L)]).reshape(-1) for n in SMALL]
    small_parts.append(jnp.stack([grads[l]["conv_a_w"][:31].reshape(-1) for l in range(L)]).reshape(-1))
    small_parts.append(jnp.stack([grads[l]["conv_b_w"][:3].reshape(-1) for l in range(L)]).reshape(-1))
    small_parts.append(jnp.stack([d_adas[l].reshape(-1) for l in range(L)]).reshape(-1))
    small_sizes = [int(p.shape[0]) for p in small_parts]
    gsmall = exchange("gather_small_grads", [_rows(jnp.concatenate(small_parts))], [])[0]
    gsum = sum_slots("sum_small", gsmall).reshape(-1)
    recv[0] = exchange_end(pending, gsum)
    Gr = {}
    offs = np.cumsum([0] + small_sizes)
    for i, n in enumerate(SMALL):
        Gr[n] = gsum[offs[i]:offs[i + 1]].reshape(L, -1)
    ca = gsum[offs[7]:offs[8]].reshape(L, 31, CW)
    cbw = gsum[offs[8]:offs[9]].reshape(L, 3, CW)
    Gr["conv_a_w"] = lax.dynamic_slice_in_dim(ca, me * 64, 64, axis=2)
    Gr["conv_b_w"] = lax.dynamic_slice_in_dim(cbw, me * 64, 64, axis=2)
    Gr["b_ada"] = gsum[offs[9]:offs[10]].reshape(L, 3 * D_MODEL)
    d_ada_all = gsmall.reshape(NDEV, -1)[:, offs[9]:offs[10]].reshape(NDEV, L, 3 * D_MODEL)
    d_mine = lax.dynamic_slice_in_dim(d_ada_all, me * ncol, ncol, axis=2).reshape(NDEV, L * ncol)
    g_ada = mm(c_act, jnp.pad(d_mine, ((0, 8), (0, 0))).astype(BF), name="mm_gw_ada", trans_a=True)
    Gr["w_ada"] = g_ada.reshape(D_MODEL, L, ncol).transpose(1, 0, 2)

    D, NM, NV = {}, {}, {}
    D["w_ada"], NM["w_ada"], NV["w_ada"] = adamw("adamw_w_ada", P["w_ada"], Gr["w_ada"], Mo["w_ada"], Vo["w_ada"])
    Gr["w_in"], D["w_in"], NM["w_in"], NV["w_in"] = sum_adamw(
        "sum_adamw_w_in", [recv[l][0] for l in range(L)], P["w_in"], Mo["w_in"], Vo["w_in"])
    Gr["w_o"], D["w_o"], NM["w_o"], NV["w_o"] = sum_adamw(
        "sum_adamw_w_o", [recv[l][2] for l in range(L)], P["w_o"], Mo["w_o"], Vo["w_o"])
    misc = lambda T_: jnp.concatenate([T_["w_a_out"], T_["w_b_out"], T_["w_c_out"], T_["w_ukv"],
                                       jnp.pad(T_["w_uq"], ((0, 0), (0, 0), (0, LANE - UQ_SHARD)))], axis=1)
    res = sum_adamw("sum_adamw_misc", [recv[l][1] for l in range(L)], misc(P), misc(Mo), misc(Vo))
    for T_, r in zip((Gr, D, NM, NV), res):
        T_["w_a_out"], T_["w_b_out"], T_["w_c_out"] = r[:, ROW_A:ROW_B], r[:, ROW_B:ROW_C], r[:, ROW_C:ROW_UKV]
        T_["w_ukv"], T_["w_uq"] = r[:, ROW_UKV:ROW_UQ], r[:, ROW_UQ:MISC_ROWS, :UQ_SHARD]
    packed =("b_ada", "conv_a_w", "conv_b_w") + SMALL
    pk = lambda T_: _rows(jnp.concatenate([T_[n].reshape(-1) for n in packed]))[None]
    dS, mS, vS = adamw("adamw_small", pk(P), pk(Gr), pk(Mo), pk(Vo))
    o = 0
    for n in packed:
        sz = int(np.prod(P[n].shape))
        D[n] = dS.reshape(-1)[o:o + sz].reshape(P[n].shape)
        NM[n] = mS.reshape(-1)[o:o + sz].reshape(P[n].shape)
        NV[n] = vS.reshape(-1)[o:o + sz].reshape(P[n].shape)
        o += sz
    return (loss, grad_x, *[Gr[n] for n in ORDER], *[D[n] for n in ORDER], *[NM[n] for n in ORDER],
            *[NV[n] for n in ORDER])
```

```python
import functools
import math

import numpy as np
import jax
import jax.numpy as jnp
from jax import lax
from jax.experimental import pallas as pl
from jax.experimental.pallas import tpu as pltpu

BF = jnp.bfloat16
F32 = jnp.float32
MESH = pl.DeviceIdType.MESH
NDEV = 8

HEADS, NOPE, ROPE, VH = 8, 64, 32, 64
HP = 128
ROPE_THETA = 10000.0
LN_EPS = 1e-5
RMS_EPS = 1e-6
LR, B1, B2, EPS, WD, STEP = 0.001, 0.9, 0.999, 1e-08, 0.01, 10

LANE = 128
VMEM_LIMIT = 56 * 1024 * 1024

D_MODEL, CW, QL, KVL = 1024, 512, 384, 256
OFF_M, OFF_A, OFF_AG, OFF_B, OFF_CG, OFF_GB, OFF_BG = 0, 3072, 4096, 4608, 5632, 6144, 6656
OFF_KV, OFF_KR, OFF_Q, NP = 7168, 7424, 7680, 8192
D_IN = 7840


def _cparams(**kw):
    return pltpu.CompilerParams(vmem_limit_bytes=VMEM_LIMIT, **kw)


def _sigmoid(x):
    return jax.nn.sigmoid(x)


def _silu(x):
    return x * _sigmoid(x)


def _dsilu(x):
    s = _sigmoid(x)
    return s * (1.0 + x * (1.0 - s))


def _pick_tile(n, cap, mult):
    if n <= cap:
        return n
    for t in range(cap - cap % mult, 0, -mult):
        if n % t == 0:
            return t
    raise ValueError((n, cap, mult))


def mm(a, b, *, name, trans_a=False, trans_b=False, out_dtype=F32, bias=None, tm=1024, tn=1024, tk=2048):
    if trans_a:
        K, M = a.shape
    else:
        M, K = a.shape
    if trans_b:
        N, K2 = b.shape
    else:
        K2, N = b.shape
    assert K == K2 and not (trans_a and trans_b), (a.shape, b.shape)
    tm, tn = _pick_tile(M, tm, 16), _pick_tile(N, tn, LANE)
    tk = _pick_tile(K, tk, LANE if trans_b else 16)
    assert M % tm == 0 and N % tn == 0 and K % tk == 0, (M, N, K, tm, tn, tk)
    nk = K // tk
    dims = (((0 if trans_a else 1,), (1 if trans_b else 0,)), ((), ()))
    has_bias = bias is not None

    def body(*refs):
        a_ref, b_ref = refs[0], refs[1]
        bias_ref = refs[2] if has_bias else None
        o_ref = refs[3] if has_bias else refs[2]
        p = lax.dot_general(a_ref[...], b_ref[...], dims, preferred_element_type=F32)

        def finish(v):
            if has_bias:
                v = v + bias_ref[...]
            o_ref[...] = v.astype(o_ref.dtype)

        if nk == 1:
            finish(p)
        else:
            acc = refs[-1]
            k = pl.program_id(2)

            @pl.when(k == 0)
            def _():
                acc[...] = p

            @pl.when(k > 0)
            def _():
                acc[...] += p

            @pl.when(k == nk - 1)
            def _():
                finish(acc[...])

    if trans_a:
        a_spec = pl.BlockSpec((tk, tm), lambda i, j, k: (k, i))
    else:
        a_spec = pl.BlockSpec((tm, tk), lambda i, j, k: (i, k))
    if trans_b:
        b_spec = pl.BlockSpec((tn, tk), lambda i, j, k: (j, k))
    else:
        b_spec = pl.BlockSpec((tk, tn), lambda i, j, k: (k, j))
    in_specs = [a_spec, b_spec]
    args = [a, b]
    if has_bias:
        in_specs.append(pl.BlockSpec((1, tn), lambda i, j, k: (0, j)))
        args.append(bias)
    return pl.pallas_call(
        body, name=name, grid=(M // tm, N // tn, nk),
        in_specs=in_specs, out_specs=pl.BlockSpec((tm, tn), lambda i, j, k: (i, j)),
        out_shape=jax.ShapeDtypeStruct((M, N), out_dtype),
        scratch_shapes=[pltpu.VMEM((tm, tn), F32)] if nk > 1 else [],
        compiler_params=_cparams(),
    )(*args)


def rowwise(name, fn, S, T, row_ins, full_ins, row_outs, acc_outs=(), into=None):
    n_in = len(row_ins) + len(full_ins)
    n_ro, n_ao = len(row_outs), len(acc_outs)
    alias = into is not None and into[0] is not None

    def body(*refs):
        vals = [r[...] for r in refs[:n_in]]
        outs = fn(*vals)
        if not isinstance(outs, (tuple, list)):
            outs = (outs,)
        assert len(outs) == n_ro + n_ao, (name, len(outs))
        o0 = n_in + (1 if alias else 0)
        for r, v in zip(refs[o0:o0 + n_ro], outs[:n_ro]):
            r[...] = v.astype(r.dtype)
        first = pl.program_id(0) == 0
        for r, v in zip(refs[o0 + n_ro:], outs[n_ro:]):
            def init(r=r, v=v):
                r[...] = v

            def accum(r=r, v=v):
                r[...] += v

            pl.when(first)(init)
            pl.when(jnp.logical_not(first))(accum)

    in_specs, args = [], []
    for arr, W, off in row_ins:
        assert off % W == 0 and arr.shape[0] == S, (name, arr.shape, W, off)
        in_specs.append(pl.BlockSpec((T, W), functools.partial(lambda i, cb: (i, cb), cb=off // W)))
        args.append(arr)
    for arr in full_ins:
        in_specs.append(pl.BlockSpec(arr.shape, lambda i: (0, 0)))
        args.append(arr)
    out_specs = [pl.BlockSpec((T, W), lambda i: (i, 0)) for W, _ in row_outs]
    out_shape = [jax.ShapeDtypeStruct((S, W), dt) for W, dt in row_outs]
    aliases = {}
    if into is not None:
        buf, total, off = into
        W0, dt0 = row_outs[0]
        assert off % W0 == 0
        out_specs[0] = pl.BlockSpec((T, W0), functools.partial(lambda i, cb: (i, cb), cb=off // W0))
        out_shape[0] = jax.ShapeDtypeStruct((S, total), dt0)
        if alias:
            in_specs.append(ANY_SPEC)
            args.append(buf)
            aliases = {n_in: 0}
    out_specs += [pl.BlockSpec((1, W), lambda i: (0, 0)) for W in acc_outs]
    out_shape += [jax.ShapeDtypeStruct((1, W), F32) for W in acc_outs]
    return pl.pallas_call(
        body, name=name, grid=(S // T,), in_specs=in_specs, out_specs=out_specs, out_shape=out_shape,
        input_output_aliases=aliases, compiler_params=_cparams(),
    )(*args)


def _colsum(v):
    return jnp.sum(v, axis=0, keepdims=True)


def _ln_stats(r):
    mu = jnp.mean(r, axis=-1, keepdims=True)
    d = r - mu
    var = jnp.mean(d * d, axis=-1, keepdims=True)
    rstd = lax.rsqrt(var + LN_EPS)
    return d * rstd, rstd


def _ln_bwd(dn, n, rstd):
    return rstd * (dn - jnp.mean(dn, axis=-1, keepdims=True) - n * jnp.mean(dn * n, axis=-1, keepdims=True))


CPAD = 32
TC = 64


def _pre(mode, x1, x2):
    return x1 * _sigmoid(x2) if mode == "glu" else x1 * x2


def _shifted(ext, sft):
    n = TC + CPAD
    return pltpu.roll(ext, (n - sft) % n, 0)[0:TC]


def _interleaved_specs(S, off):
    return [pl.BlockSpec((S, LANE), functools.partial(lambda j, o: (0, o + 2 * j), o=off // LANE)),
            pl.BlockSpec((S, LANE), functools.partial(lambda j, o: (0, o + 2 * j + 1), o=off // LANE))]


def conv_fwd(name, src, off, w_pad, taps, mode, S, C):
    nchunk = S // TC

    def body(x1_ref, x2_ref, w_ref, o_ref, a_pad):
        a_pad[0:CPAD, :] = jnp.zeros((CPAD, LANE), F32)

        def fill(i, _):
            r = pl.multiple_of(i * 256, 256)
            a_pad[pl.ds(CPAD + r, 256), :] = _pre(mode, x1_ref[pl.ds(r, 256), :], x2_ref[pl.ds(r, 256), :])
            return 0

        lax.fori_loop(0, S // 256, fill, 0)

        def chunk(i, _):
            base = pl.multiple_of(i * TC, TC)
            ext = a_pad[pl.ds(base, TC + CPAD), :]
            acc = jnp.zeros((TC, LANE), F32)
            for k in range(taps):
                acc = acc + w_ref[pl.ds(k, 1), :] * _shifted(ext, CPAD - (taps - 1) + k)
            o_ref[pl.ds(base, TC), :] = acc
            return 0

        lax.fori_loop(0, nchunk, chunk, 0)

    kp = w_pad.shape[0]
    return pl.pallas_call(
        body, name=name, grid=(C // LANE,),
        in_specs=_interleaved_specs(S, off) + [pl.BlockSpec((kp, LANE), lambda j: (0, j))],
        out_specs=pl.BlockSpec((S, LANE), lambda j: (0, j)),
        out_shape=jax.ShapeDtypeStruct((S, C), F32),
        scratch_shapes=[pltpu.VMEM((S + CPAD, LANE), F32)],
        compiler_params=_cparams(),
    )(src, src, w_pad)


def conv_bwd(name, src, off, dc, w_pad, taps, mode, S, C, buf):
    nchunk = S // TC
    kp = w_pad.shape[0]

    def body(x1_ref, x2_ref, dc_ref, w_ref, _, d_ref, dw_ref, a_pad, dc_pad, dw_acc):
        a_pad[0:CPAD, :] = jnp.zeros((CPAD, LANE), F32)
        dc_pad[S:S + CPAD, :] = jnp.zeros((CPAD, LANE), F32)
        dw_acc[...] = jnp.zeros(dw_acc.shape, F32)

        def fill(i, _):
            r = pl.multiple_of(i * 256, 256)
            a_pad[pl.ds(CPAD + r, 256), :] = _pre(mode, x1_ref[pl.ds(r, 256), :], x2_ref[pl.ds(r, 256), :])
            dc_pad[pl.ds(r, 256), :] = dc_ref[pl.ds(r, 256), :]
            return 0

        lax.fori_loop(0, S // 256, fill, 0)

        def chunk(i, _):
            base = pl.multiple_of(i * TC, TC)
            ext_d = dc_pad[pl.ds(base, TC + CPAD), :]
            ext_a = a_pad[pl.ds(base, TC + CPAD), :]
            dcv = ext_d[0:TC]
            da = jnp.zeros((TC, LANE), F32)
            for k in range(taps):
                da = da + w_ref[pl.ds(k, 1), :] * _shifted(ext_d, taps - 1 - k)
                prod = dcv * _shifted(ext_a, CPAD - (taps - 1) + k)
                fold = prod[0:8]
                for g in range(1, TC // 8):
                    fold = fold + prod[8 * g:8 * g + 8]
                dw_acc[pl.ds(8 * k, 8), :] += fold
            x1 = x1_ref[pl.ds(base, TC), :]
            x2 = x2_ref[pl.ds(base, TC), :]
            if mode == "glu":
                s = _sigmoid(x2)
                d1, d2 = da * s, da * x1 * s * (1.0 - s)
            else:
                d1, d2 = da * x2, da * x1
            d_ref[pl.ds(base, TC), 0:LANE] = d1.astype(BF)
            d_ref[pl.ds(base, TC), LANE:2 * LANE] = d2.astype(BF)
            return 0

        lax.fori_loop(0, nchunk, chunk, 0)
        dw_ref[...] = jnp.zeros(dw_ref.shape, F32)
        for k in range(taps):
            dw_ref[pl.ds(k, 1), :] = jnp.sum(dw_acc[pl.ds(8 * k, 8), :], axis=0, keepdims=True)

    blk = pl.BlockSpec((S, LANE), lambda j: (0, j))
    return pl.pallas_call(
        body, name=name, grid=(C // LANE,),
        in_specs=_interleaved_specs(S, off) + [blk, pl.BlockSpec((kp, LANE), lambda j: (0, j)), ANY_SPEC],
        out_specs=[pl.BlockSpec((S, 2 * LANE), functools.partial(lambda j, o: (0, o + j), o=off // (2 * LANE))),
                   pl.BlockSpec((kp, LANE), lambda j: (0, j))],
        out_shape=[jax.ShapeDtypeStruct(buf.shape, BF), jax.ShapeDtypeStruct((kp, C), F32)],
        input_output_aliases={4: 0},
        scratch_shapes=[pltpu.VMEM((S + CPAD, LANE), F32), pltpu.VMEM((S + CPAD, LANE), F32),
                        pltpu.VMEM((8 * kp, LANE), F32)],
        compiler_params=_cparams(),
    )(src, src, dc, w_pad, buf)


TA = 256
SCALE = (NOPE + ROPE) ** -0.5
NT_DIMS = (((1,), (1,)), ((), ()))
TN_DIMS = (((0,), (0,)), ((), ()))


def _causal_mask():
    row = lax.broadcasted_iota(jnp.int32, (TA, TA), 0)
    col = lax.broadcasted_iota(jnp.int32, (TA, TA), 1)
    return col <= row


def attn_fwd(q, k, v, S):
    nq = S // TA

    def body(q_ref, k_ref, v_ref, o_ref, lse_ref):
        def q_block(qi, _):
            r0 = pl.multiple_of(qi * TA, TA)
            qb = q_ref[0, pl.ds(r0, TA), :]

            def step(kj, carry, masked):
                m, l, acc = carry
                c0 = pl.multiple_of(kj * TA, TA)
                kb = k_ref[0, pl.ds(c0, TA), :]
                vb = v_ref[0, pl.ds(c0, TA), :]
                s = lax.dot_general(qb, kb, NT_DIMS, preferred_element_type=F32) * SCALE
                if masked:
                    s = jnp.where(_causal_mask(), s, -jnp.inf)
                m_new = jnp.maximum(m, jnp.max(s, axis=-1, keepdims=True))
                p = jnp.exp(s - m_new)
                alpha = jnp.exp(m - m_new)
                l = alpha * l + jnp.sum(p, axis=-1, keepdims=True)
                acc = alpha * acc + jnp.dot(p.astype(BF), vb, preferred_element_type=F32)
                return m_new, l, acc

            init = (jnp.full((TA, 1), -jnp.inf, F32), jnp.zeros((TA, 1), F32), jnp.zeros((TA, HP), F32))
            carry = lax.fori_loop(0, qi, lambda kj, c: step(kj, c, False), init)
            m, l, acc = step(qi, carry, True)
            o_ref[0, pl.ds(r0, TA), :] = acc / l
            lse_ref[0, pl.ds(r0, TA), :] = jnp.broadcast_to(m + jnp.log(l), (TA, HP))
            return 0

        lax.fori_loop(0, nq, q_block, 0)

    blk = pl.BlockSpec((1, S, HP), lambda h: (h, 0, 0))
    return pl.pallas_call(
        body, name="attn_fwd", grid=(HEADS,), in_specs=[blk, blk, blk], out_specs=[blk, blk],
        out_shape=[jax.ShapeDtypeStruct((HEADS, S, HP), F32), jax.ShapeDtypeStruct((HEADS, S, HP), F32)],
        compiler_params=_cparams(),
    )(q, k, v)


def attn_bwd(q, k, v, o, lse, do, S):
    nq = S // TA

    def body(q_ref, k_ref, v_ref, o_ref, lse_ref, do_ref, dq_ref, dk_ref, dv_ref):
        dk_ref[...] = jnp.zeros(dk_ref.shape, F32)
        dv_ref[...] = jnp.zeros(dv_ref.shape, F32)

        def q_block(qi, _):
            r0 = pl.multiple_of(qi * TA, TA)
            qb = q_ref[0, pl.ds(r0, TA), :]
            dof = do_ref[0, pl.ds(r0, TA), :]
            dob = dof.astype(BF)
            lse_b = lse_ref[0, pl.ds(r0, TA), :][:, 0:1]
            delta = jnp.sum(dof * o_ref[0, pl.ds(r0, TA), :], axis=-1, keepdims=True)

            def step(kj, dq, masked):
                c0 = pl.multiple_of(kj * TA, TA)
                kb = k_ref[0, pl.ds(c0, TA), :]
                vb = v_ref[0, pl.ds(c0, TA), :]
                s = lax.dot_general(qb, kb, NT_DIMS, preferred_element_type=F32) * SCALE
                if masked:
                    s = jnp.where(_causal_mask(), s, -jnp.inf)
                p = jnp.exp(s - lse_b)
                dp = lax.dot_general(dob, vb, NT_DIMS, preferred_element_type=F32)
                ds = (p * (dp - delta) * SCALE).astype(BF)
                dv_ref[0, pl.ds(c0, TA), :] += lax.dot_general(p.astype(BF), dob, TN_DIMS, preferred_element_type=F32)
                dk_ref[0, pl.ds(c0, TA), :] += lax.dot_general(ds, qb, TN_DIMS, preferred_element_type=F32)
                return dq + jnp.dot(ds, kb, preferred_element_type=F32)

            dq = lax.fori_loop(0, qi, lambda kj, c: step(kj, c, False), jnp.zeros((TA, HP), F32))
            dq_ref[0, pl.ds(r0, TA), :] = step(qi, dq, True)
            return 0

        lax.fori_loop(0, nq, q_block, 0)

    blk = pl.BlockSpec((1, S, HP), lambda h: (h, 0, 0))
    shp = jax.ShapeDtypeStruct((HEADS, S, HP), F32)
    return pl.pallas_call(
        body, name="attn_bwd", grid=(HEADS,), in_specs=[blk] * 6, out_specs=[blk] * 3, out_shape=[shp] * 3,
        compiler_params=_cparams(),
    )(q, k, v, o, lse, do)


def exchange(name, gathers, a2as):
    n_g, n = len(gathers), len(gathers) + len(a2as)

    def body(*refs):
        ins, outs = refs[:n], refs[n:2 * n]
        send_sems, recv_sems, loc_sems = refs[2 * n:]
        x, y, c = lax.axis_index("x"), lax.axis_index("y"), lax.axis_index("c")
        me = 4 * x + 2 * y + c

        def peer(k):
            px = 1 - x if k & 4 else x
            py = 1 - y if k & 2 else y
            pc = 1 - c if k & 1 else c
            return (px, py, pc), 4 * px + 2 * py + pc

        def remote(a, k):
            pid, pflat = peer(k)
            src = ins[a] if a < n_g else ins[a].at[pflat]
            return pltpu.make_async_remote_copy(
                src_ref=src, dst_ref=outs[a].at[me], send_sem=send_sems.at[a, k - 1], recv_sem=recv_sems.at[a, k - 1],
                device_id=pid, device_id_type=MESH)

        def arrival(a, k):
            pid, pflat = peer(k)
            src = ins[a] if a < n_g else ins[a].at[pflat]
            return pltpu.make_async_remote_copy(
                src_ref=src, dst_ref=outs[a].at[pflat], send_sem=send_sems.at[a, k - 1], recv_sem=recv_sems.at[a, k - 1],
                device_id=pid, device_id_type=MESH)

        local = []
        for a in range(n):
            own = ins[a] if a < n_g else ins[a].at[me]
            cp = pltpu.make_async_copy(own, outs[a].at[me], loc_sems.at[a])
            cp.start()
            local.append(cp)
        sent = []
        for k in (1, 2, 4, 3, 5, 6, 7):
            for a in range(n):
                cp = remote(a, k)
                cp.start()
                sent.append(cp)
        for k in range(1, 8):
            for a in range(n):
                arrival(a, k).wait_recv()
        for cp in sent:
            cp.wait_send()
        for cp in local:
            cp.wait()

    out_shape = [jax.ShapeDtypeStruct((NDEV,) + g.shape, g.dtype) for g in gathers]
    out_shape += [jax.ShapeDtypeStruct(a.shape, a.dtype) for a in a2as]
    any_spec = pl.BlockSpec(memory_space=pl.ANY)
    return pl.pallas_call(
        body, name=name, in_specs=[any_spec] * n, out_specs=[any_spec] * n, out_shape=out_shape,
        scratch_shapes=[pltpu.SemaphoreType.DMA((n, NDEV - 1)), pltpu.SemaphoreType.DMA((n, NDEV - 1)),
                        pltpu.SemaphoreType.DMA((n,))],
    )(*gathers, *a2as)


def _peer(k, x, y, c):
    px = 1 - x if k & 4 else x
    py = 1 - y if k & 2 else y
    pc = 1 - c if k & 1 else c
    return (px, py, pc), 4 * px + 2 * py + pc


PEER_ORDER = (1, 2, 4, 3, 5, 6, 7)
HBM_SPEC = pl.BlockSpec(memory_space=pltpu.HBM)
SEM_SPEC = pl.BlockSpec(memory_space=pltpu.SEMAPHORE)
ANY_SPEC = pl.BlockSpec(memory_space=pl.ANY)


def _split_copies(ins, lands, n_g, send_sems, recv_sems):
    x, y, c = lax.axis_index("x"), lax.axis_index("y"), lax.axis_index("c")
    me = 4 * x + 2 * y + c

    def outgoing(a, k):
        pid, pflat = _peer(k, x, y, c)
        src = ins[a] if a < n_g else ins[a].at[pflat]
        return pltpu.make_async_remote_copy(
            src_ref=src, dst_ref=lands[a].at[me], send_sem=send_sems.at[a * (NDEV - 1) + k - 1],
            recv_sem=recv_sems.at[a * (NDEV - 1) + k - 1],
            device_id=pid, device_id_type=MESH)

    def arrival(a, k):
        pid, pflat = _peer(k, x, y, c)
        src = ins[a] if a < n_g else ins[a].at[pflat]
        return pltpu.make_async_remote_copy(
            src_ref=src, dst_ref=lands[a].at[pflat], send_sem=send_sems.at[a * (NDEV - 1) + k - 1],
            recv_sem=recv_sems.at[a * (NDEV - 1) + k - 1],
            device_id=pid, device_id_type=MESH)

    return outgoing, arrival


def exchange_begin(name, srcs, n_g, dep):
    n = len(srcs)
    land_shapes = [((NDEV,) + s.shape) if a < n_g else s.shape for a, s in enumerate(srcs)]

    def own_body(*refs):
        ins, outs = refs[:n], refs[n + 1:2 * n + 1]
        stage, sems = refs[2 * n + 1:3 * n + 1], refs[-1]
        me = 4 * lax.axis_index("x") + 2 * lax.axis_index("y") + lax.axis_index("c")
        cps = [pltpu.make_async_copy(ins[a] if a < n_g else ins[a].at[me], stage[a], sems.at[a]) for a in range(n)]
        for cp in cps:
            cp.start()
        for cp in cps:
            cp.wait()
        cps = [pltpu.make_async_copy(stage[a], outs[a].at[me], sems.at[a]) for a in range(n)]
        for cp in cps:
            cp.start()
        for cp in cps:
            cp.wait()

    lands = pl.pallas_call(
        own_body, name=name + "_own", in_specs=[ANY_SPEC] * (n + 1), out_specs=[ANY_SPEC] * n,
        out_shape=[jax.ShapeDtypeStruct(sh, s.dtype) for sh, s in zip(land_shapes, srcs)],
        scratch_shapes=[pltpu.VMEM(sh[1:], s.dtype) for sh, s in zip(land_shapes, srcs)] + [pltpu.SemaphoreType.DMA((n,))],
        compiler_params=_cparams(),
    )(*srcs, dep)

    def start_body(*refs):
        ins, lz = refs[:n], refs[n:2 * n]
        send_sems, recv_sems, token = refs[2 * n], refs[2 * n + 1], refs[-1]
        outgoing, _ = _split_copies(ins, lz, n_g, send_sems, recv_sems)
        for k in PEER_ORDER:
            for a in range(n):
                outgoing(a, k).start()
        token[...] = jnp.zeros(token.shape, F32)

    hbm = lambda t: pltpu.HBM(t.shape, t.dtype)
    res = pl.pallas_call(
        start_body, name=name + "_start",
        out_shape=(pltpu.SemaphoreType.DMA((n * (NDEV - 1),)), pltpu.SemaphoreType.DMA((n * (NDEV - 1),)),
                   *[hbm(s) for s in srcs], *[hbm(t) for t in lands], jax.ShapeDtypeStruct((8, LANE), F32)),
        in_specs=[HBM_SPEC] * (2 * n),
        out_specs=(SEM_SPEC, SEM_SPEC, *[HBM_SPEC] * (2 * n), pl.BlockSpec(memory_space=pltpu.VMEM)),
        input_output_aliases={i: 2 + i for i in range(2 * n)},
        compiler_params=pltpu.CompilerParams(has_side_effects=pltpu.SideEffectType.DATAFLOW_SIDE_EFFECTING),
    )(*[pltpu.with_memory_space_constraint(t, pltpu.HBM) for t in list(srcs) + list(lands)])
    return (name, n, n_g, res[:-1]), res[-1]


def exchange_end(handle, after):
    name, n, n_g, (send_sems, recv_sems, *bufs) = handle

    def wait_body(*refs):
        ins, lz = refs[:n], refs[n:2 * n]
        ss, rs = refs[2 * n], refs[2 * n + 1]
        outgoing, arrival = _split_copies(ins, lz, n_g, ss, rs)
        for k in range(1, NDEV):
            for a in range(n):
                arrival(a, k).wait_recv()
        for k in range(1, NDEV):
            for a in range(n):
                outgoing(a, k).wait_send()

    res = pl.pallas_call(
        wait_body, name=name + "_wait", out_shape=tuple(pltpu.HBM(t.shape, t.dtype) for t in bufs),
        in_specs=[HBM_SPEC] * (2 * n) + [SEM_SPEC, SEM_SPEC, ANY_SPEC], out_specs=[HBM_SPEC] * (2 * n),
        input_output_aliases={i: i for i in range(2 * n)},
        compiler_params=pltpu.CompilerParams(has_side_effects=pltpu.SideEffectType.DATAFLOW_SIDE_EFFECTING),
    )(*bufs, send_sems, recv_sems, after)
    return list(res[n:])


def _pick_rows(R, mult, cap):
    best = None
    for n in range(1, R + 1):
        if R % n == 0 and (R // n) % mult == 0 and R // n <= cap:
            best = R // n
            break
    assert best is not None, (R, mult, cap)
    return best


def sum_slots(name, x):
    _, R, _ = x.shape
    tr = _pick_rows(R, 16, 2304)

    def body(x_ref, o_ref):
        acc = x_ref[0].astype(F32)
        for d in range(1, NDEV):
            acc = acc + x_ref[d].astype(F32)
        o_ref[...] = acc

    return pl.pallas_call(
        body, name=name, grid=(R // tr,),
        in_specs=[pl.BlockSpec((NDEV, tr, LANE), lambda i: (0, i, 0))],
        out_specs=pl.BlockSpec((tr, LANE), lambda i: (i, 0)),
        out_shape=jax.ShapeDtypeStruct((R, LANE), F32), compiler_params=_cparams(),
    )(x)


def adamw(name, w, g, m, v):
    L, R, C = w.shape
    tr = _pick_rows(R, 8, 256) if R % 8 == 0 else R

    def body(w_ref, g_ref, m_ref, v_ref, d_ref, nm_ref, nv_ref):
        gg = g_ref[...]
        nm = B1 * m_ref[...] + (1.0 - B1) * gg
        nv = B2 * v_ref[...] + (1.0 - B2) * jnp.square(gg)
        m_hat = nm / (1.0 - B1 ** STEP)
        v_hat = nv / (1.0 - B2 ** STEP)
        d_ref[...] = -LR * (m_hat / (jnp.sqrt(v_hat) + EPS) + WD * w_ref[...])
        nm_ref[...] = nm
        nv_ref[...] = nv

    blk = pl.BlockSpec((1, tr, C), lambda l, i: (l, i, 0))
    shp = jax.ShapeDtypeStruct(w.shape, F32)
    return pl.pallas_call(
        body, name=name, grid=(L, R // tr), in_specs=[blk] * 4, out_specs=[blk] * 3, out_shape=[shp] * 3,
        compiler_params=_cparams(),
    )(w, g, m, v)


IN_SHARD = D_IN // NDEV
UQ_SHARD = HEADS * (NOPE + ROPE) // NDEV
W_IN_PAD = 1024
ROW_A, ROW_B, ROW_C, ROW_UKV, ROW_UQ, MISC_ROWS = 0, 512, 1024, 1536, 1792, 2176


def _in_perm_index():
    ar = np.arange
    z = lambda n: np.full((n,), -1, np.int64)
    mix = lambda lo1, lo2: np.concatenate([ar(lo + LANE * j, lo + LANE * (j + 1)) for j in range(CW // LANE)
                                           for lo in (lo1, lo2)])
    return np.concatenate([ar(4768, 7840), mix(0, 512), ar(1024, 1536), mix(1536, 2560), ar(4256, 4768), ar(2048, 2560),
                           ar(3072, 3584), ar(3968, 4224), ar(4224, 4256), z(OFF_Q - OFF_KR - ROPE), ar(3584, 3968),
                           z(NP - OFF_Q - QL)])


def _head_perm_index(a, b):
    h = np.arange(HEADS)[:, None] * (a + b)
    return np.concatenate([(h + np.arange(a)[None]).reshape(-1), (h + a + np.arange(b)[None]).reshape(-1)])


def _inverse(perm, n):
    inv = np.full((n,), -1, np.int64)
    inv[perm[perm >= 0]] = np.nonzero(perm >= 0)[0]
    return inv


IN_PERM = _in_perm_index()
UQ_PERM = _head_perm_index(NOPE, ROPE)
UKV_PERM = _head_perm_index(NOPE, VH)


def _to_gathered(perm, shard, pad):
    return np.where(perm >= 0, (perm // shard) * pad + perm % shard, -1)


def _from_full(inv, shard, pad):
    j, i = np.divmod(np.arange(NDEV * pad), pad)
    return np.where(i < shard, inv[np.minimum(j * shard + i, inv.shape[0] - 1)], -1)


def col_gather(name, srcs, out_shapes, jobs, deps=()):
    ns, nj, nd = len(srcs), len(jobs), len(deps)
    tables = [jnp.asarray(np.asarray(job[5], np.int32)[None, :]) for job in jobs]

    def view(ref, col0, width, r0, rc):
        n = ref.shape[-1]
        if len(ref.shape) == 3:
            return ref.at[col0 // n, pl.ds(r0, rc), pl.ds(col0 % n, width)]
        return ref.at[pl.ds(r0, rc), pl.ds(col0, width)]

    def body(*refs):
        src_refs, tab_refs, out_refs = refs[:ns], refs[ns:ns + nj], refs[ns + nj + nd:]
        for ji, (si, srow, oi, orow, nrows, tgt) in enumerate(jobs):
            sref, oref = src_refs[si], out_refs[oi]
            tgt = np.asarray(tgt)
            tw = 256 if oref.shape[-1] % 256 == 0 else LANE
            rc = 256 if nrows % 256 == 0 else LANE
            for t in range(tgt.shape[0] // tw):
                tt = tgt[t * tw:(t + 1) * tw]
                tiles = sorted(set((tt[tt >= 0] // LANE).tolist()))
                straight = bool(tiles) and np.array_equal(tt, np.arange(tiles[0] * LANE, tiles[0] * LANE + tw))
                onehots = []
                if tiles and not straight:
                    want = tab_refs[ji][:, t * tw:(t + 1) * tw]
                    row = lax.broadcasted_iota(jnp.int32, (LANE, tw), 0)
                    onehots = [jnp.where(want == row + s * LANE, 1.0, 0.0).astype(BF) for s in tiles]

                def chunk(ci, _, t=t, tiles=tiles, straight=straight, onehots=onehots):
                    r0 = ci * rc
                    dst = view(oref, t * tw, tw, pl.multiple_of(orow + r0, LANE), rc)
                    rs = pl.multiple_of(srow + r0, LANE)
                    if not tiles:
                        dst[...] = jnp.zeros((rc, tw), BF)
                    elif straight:
                        for k in range(tw // LANE):
                            view(oref, t * tw + k * LANE, LANE, pl.multiple_of(orow + r0, LANE), rc)[...] = (
                                view(sref, (tiles[0] + k) * LANE, LANE, rs, rc)[...])
                    else:
                        acc = None
                        for s, oh in zip(tiles, onehots):
                            p = jnp.dot(view(sref, s * LANE, LANE, rs, rc)[...], oh, preferred_element_type=F32)
                            acc = p if acc is None else acc + p
                        dst[...] = acc.astype(BF)
                    return 0

                lax.fori_loop(0, nrows // rc, chunk, 0)

    vmem = pl.BlockSpec(memory_space=pltpu.VMEM)
    return pl.pallas_call(
        body, name=name, in_specs=[vmem] * (ns + nj) + [ANY_SPEC] * nd, out_specs=[vmem] * len(out_shapes),
        out_shape=[jax.ShapeDtypeStruct(s, BF) for s in out_shapes], compiler_params=_cparams(),
    )(*srcs, *tables, *deps)


def sum_adamw(name, recvs, w, m, v):
    L, R, C = w.shape
    CP = recvs[0].shape[-1]
    tr = _pick_rows(R, 16, 128)

    def body(*refs):
        r_refs = refs[:L]
        w_ref, m_ref, v_ref, g_ref, d_ref, nm_ref, nv_ref, gsum = refs[L:]
        layer = pl.program_id(0)
        for k in range(L):
            def total(k=k):
                acc = r_refs[k][0].astype(F32)
                for d in range(1, NDEV):
                    acc = acc + r_refs[k][d].astype(F32)
                gsum[...] = acc
            pl.when(layer == k)(total)
        gg = gsum[:, 0:C]
        nm = B1 * m_ref[...] + (1.0 - B1) * gg
        nv = B2 * v_ref[...] + (1.0 - B2) * jnp.square(gg)
        m_hat = nm / (1.0 - B1 ** STEP)
        v_hat = nv / (1.0 - B2 ** STEP)
        g_ref[...] = gg
        d_ref[...] = -LR * (m_hat / (jnp.sqrt(v_hat) + EPS) + WD * w_ref[...])
        nm_ref[...] = nm
        nv_ref[...] = nv

    r_specs = [pl.BlockSpec((NDEV, tr, CP), functools.partial(lambda l, i, k: (0, jnp.where(l == k, i, 0), 0), k=k))
               for k in range(L)]
    blk = pl.BlockSpec((None, tr, C), lambda l, i: (l, i, 0))
    shp = jax.ShapeDtypeStruct(w.shape, F32)
    return pl.pallas_call(
        body, name=name, grid=(L, R // tr), in_specs=r_specs + [blk] * 3, out_specs=[blk] * 4, out_shape=[shp] * 4,
        scratch_shapes=[pltpu.VMEM((tr, CP), F32)], compiler_params=_cparams(),
    )(*recvs, w, m, v)


def _heads(x, width):
    S = x.shape[0]
    x = x.reshape(S, HEADS, width).transpose(1, 0, 2)
    return jnp.pad(x, ((0, 0), (0, 0), (0, HP - width)))


ALPHA = 8.0 ** 0.25


def _rope_fn(sign):
    def fn(x, cos, sin):
        W = x.shape[-1]
        lane = lax.broadcasted_iota(jnp.int32, x.shape, 1)
        first_half = (lane % ROPE) < (ROPE // 2)
        rot = jnp.where(first_half, -pltpu.roll(x, W - ROPE // 2, 1), pltpu.roll(x, ROPE // 2, 1))
        return x * cos + sign * rot * sin
    return fn


def layer_fwd(x, ada3, W, tabs, S):
    cos, sin = tabs
    T = 256
    u = rowwise("modulate", lambda xv, a: xv * (1.0 + a[1:2, :]) + a[0:1, :], S, T,
                [(x, D_MODEL, 0)], [ada3], [(D_MODEL, BF)])[0]
    proj = mm(u, W["in"], name="mm_proj", tm=1024, tn=1024)

    ca = conv_fwd("conv_a_fwd", proj, OFF_A, W["conv_a"], 31, "glu", S, CW)

    def a_post(c, ag, vec):
        n, _ = _ln_stats(c + vec[0:1, :])
        return _silu(n * vec[1:2, :] + vec[2:3, :]) * _silu(ag)

    h_a = rowwise("mix_a_post", a_post, S, T, [(ca, CW, 0), (proj, CW, OFF_AG)], [W["vec_a"]], [(CW, BF)])[0]
    y_a = mm(h_a, W["a_out"], name="mm_branch_out")

    cb = conv_fwd("conv_b_fwd", proj, OFF_B, W["conv_b"], 3, "mul", S, CW)
    h_b = rowwise("mix_b_post", lambda c, gb, bg: gb * c * _silu(bg), S, T,
                  [(cb, CW, 0), (proj, CW, OFF_GB), (proj, CW, OFF_BG)], [], [(CW, BF)])[0]
    y_b = mm(h_b, W["b_out"], name="mm_branch_out")

    def rms2(ql, kvl, gq, gkv):
        rq = lax.rsqrt(jnp.mean(ql * ql, axis=-1, keepdims=True) + RMS_EPS)
        rk = lax.rsqrt(jnp.mean(kvl * kvl, axis=-1, keepdims=True) + RMS_EPS)
        return ql * rq * gq, kvl * rk * gkv

    qn, kvn = rowwise("rms_fwd", rms2, S, T, [(proj, QL, OFF_Q), (proj, KVL, OFF_KV)], [W["gq"], W["gkv"]],
                      [(QL, BF), (KVL, BF)])
    q = mm(qn, W["uq"], name="mm_q")
    kv = mm(kvn, W["ukv"], name="mm_kv")
    rope = _rope_fn(1.0)
    q_pe, k_pe = rowwise("rope_fwd", lambda a, b, c1, s1: (rope(a, c1, s1), rope(b, c1[:, :LANE], s1[:, :LANE])),
                         S, T, [(q, 256, 512), (proj, LANE, OFF_KR), (cos, 256, 0), (sin, 256, 0)], [],
                         [(256, BF), (LANE, BF)])
    q_h = _heads(jnp.concatenate([q[:, :512].astype(BF).reshape(S, HEADS, NOPE), q_pe.reshape(S, HEADS, ROPE)],
                                 axis=2).reshape(S, HEADS * 96), 96)
    k_pe_b = jnp.broadcast_to(k_pe[:, None, :ROPE], (S, HEADS, ROPE))
    k_h = _heads(jnp.concatenate([kv[:, :512].astype(BF).reshape(S, HEADS, NOPE), k_pe_b], axis=2)
                 .reshape(S, HEADS * 96), 96)
    v_h = _heads(kv[:, 512:].astype(BF), VH)
    o_h, lse = attn_fwd(q_h, k_h, v_h, S)
    o = o_h[:, :, :VH].transpose(1, 0, 2).reshape(S, HEADS * VH)
    h_c = rowwise("mix_c_post", lambda ov, cg: ov * _silu(cg), S, T, [(o, CW, 0), (proj, CW, OFF_CG)], [],
                  [(CW, BF)])[0]
    y_c = mm(h_c, W["c_out"], name="mm_branch_out")

    def merge(la, lb, lc, ya, yb, yc):
        return _sigmoid(la) * ya + _sigmoid(lb) * yb + _sigmoid(lc) * yc

    m = rowwise("merge_fwd", merge, S, 128,
                [(proj, D_MODEL, 0), (proj, D_MODEL, 1024), (proj, D_MODEL, 2048), (y_a, D_MODEL, 0),
                 (y_b, D_MODEL, 0), (y_c, D_MODEL, 0)], [], [(D_MODEL, BF)])[0]
    out = mm(m, W["o"], name="mm_out")

    def ln_fwd(xv, ov, a, lnv):
        n, _ = _ln_stats(ALPHA * xv + a[2:3, :] * ov)
        return n * lnv[0:1, :] + lnv[1:2, :]

    x_next = rowwise("ln_fwd", ln_fwd, S, 128, [(x, D_MODEL, 0), (out, D_MODEL, 0)], [ada3, W["lnv"]],
                     [(D_MODEL, F32)])[0]
    saved = dict(x=x, u=u, proj=proj, ca=ca, cb=cb, h_a=h_a, h_b=h_b, h_c=h_c, y_a=y_a, y_b=y_b, y_c=y_c, qn=qn,
                 kvn=kvn, q_h=q_h, k_h=k_h, v_h=v_h, o_h=o_h, lse=lse, o=o, m=m, out=out)
    return x_next, saved


def layer_bwd(dxn, sv, ada3, W, tabs, S):
    cos, sin = tabs
    T = 256
    x, proj = sv["x"], sv["proj"]
    G = {}

    def ln_bwd(xv, ov, dy, a, lnv):
        gate = a[2:3, :]
        n, rstd = _ln_stats(ALPHA * xv + gate * ov)
        dr = _ln_bwd(dy * lnv[0:1, :], n, rstd)
        return ALPHA * dr, gate * dr, _colsum(dy * n), _colsum(dy), _colsum(dr * ov)

    dres, d_out, G["ln_g"], G["ln_b"], d_gate = rowwise(
        "ln_bwd", ln_bwd, S, 128, [(x, D_MODEL, 0), (sv["out"], D_MODEL, 0), (dxn, D_MODEL, 0)], [ada3, W["lnv"]],
        [(D_MODEL, F32), (D_MODEL, BF)], [D_MODEL] * 3)
    dm = mm(d_out, W["o"], name="mm_dm", trans_b=True)
    G["w_o"] = mm(sv["m"], d_out, name="mm_gw_o", trans_a=True, out_dtype=BF)

    def merge_bwd(dmv, la, lb, lc, ya, yb, yc):
        outs, dls = [], []
        for lg, yv in ((la, ya), (lb, yb), (lc, yc)):
            s = _sigmoid(lg)
            outs.append(dmv * s)
            dls.append((dmv * yv * s * (1.0 - s)).astype(BF))
        return (jnp.concatenate(dls, axis=1),) + tuple(outs)

    d_proj, dy_a, dy_b, dy_c = rowwise(
        "merge_bwd", merge_bwd, S, 128,
        [(dm, D_MODEL, 0), (proj, D_MODEL, 0), (proj, D_MODEL, 1024), (proj, D_MODEL, 2048), (sv["y_a"], D_MODEL, 0),
         (sv["y_b"], D_MODEL, 0), (sv["y_c"], D_MODEL, 0)], [], [(3 * D_MODEL, BF)] + [(D_MODEL, BF)] * 3,
        into=(None, NP, OFF_M))

    dh = {}
    for br, dy in (("a", dy_a), ("b", dy_b), ("c", dy_c)):
        dh[br] = mm(dy, W[br + "_out"], name="mm_dh", trans_b=True)
        G["w_%s_out" % br] = mm(sv["h_" + br], dy, name="mm_gw_branch", trans_a=True, out_dtype=BF)

    def a_post_bwd(c, ag, dhv, vec):
        n, rstd = _ln_stats(c + vec[0:1, :])
        z = n * vec[1:2, :] + vec[2:3, :]
        d_ag = dhv * _silu(z) * _dsilu(ag)
        dz = dhv * _silu(ag) * _dsilu(z)
        dc = _ln_bwd(dz * vec[1:2, :], n, rstd)
        return d_ag, dc, _colsum(dc), _colsum(dz * n), _colsum(dz)

    d_proj, dca, G["conv_a_b"], G["ln_a_g"], G["ln_a_b"] = rowwise(
        "mix_a_post_bwd", a_post_bwd, S, T, [(sv["ca"], CW, 0), (proj, CW, OFF_AG), (dh["a"], CW, 0)], [W["vec_a"]],
        [(CW, BF), (CW, F32)], [CW] * 3, into=(d_proj, NP, OFF_AG))
    d_proj, G["conv_a_w"] = conv_bwd("conv_a_bwd", proj, OFF_A, dca, W["conv_a"], 31, "glu", S, CW, d_proj)

    def b_post_bwd(c, gb, bg, dhv):
        sg = _silu(bg)
        d_gb_bg = jnp.concatenate([(dhv * sg * c).astype(BF), (dhv * gb * c * _dsilu(bg)).astype(BF)], axis=1)
        return d_gb_bg, dhv * sg * gb

    d_proj, dcb = rowwise("mix_b_post_bwd", b_post_bwd, S, T,
                          [(sv["cb"], CW, 0), (proj, CW, OFF_GB), (proj, CW, OFF_BG), (dh["b"], CW, 0)], [],
                          [(2 * CW, BF), (CW, F32)], into=(d_proj, NP, OFF_GB))
    d_proj, G["conv_b_w"] = conv_bwd("conv_b_bwd", proj, OFF_B, dcb, W["conv_b"], 3, "mul", S, CW, d_proj)

    d_proj, d_o = rowwise("mix_c_post_bwd", lambda ov, cg, dhv: (dhv * ov * _dsilu(cg), dhv * _silu(cg)), S, T,
                          [(sv["o"], CW, 0), (proj, CW, OFF_CG), (dh["c"], CW, 0)], [], [(CW, BF), (CW, F32)],
                          into=(d_proj, NP, OFF_CG))
    dq_h, dk_h, dv_h = attn_bwd(sv["q_h"], sv["k_h"], sv["v_h"], sv["o_h"], sv["lse"], _heads(d_o, VH), S)
    unheads = lambda t, lo, hi: t[:, :, lo:hi].transpose(1, 0, 2).reshape(S, HEADS * (hi - lo))
    ropeT = _rope_fn(-1.0)

    def rope_bwd(dqp, dkp, c1, s1):
        f = dkp[:, :LANE] + dkp[:, LANE:]
        f = f + pltpu.roll(f, 64, 1)
        f = f + pltpu.roll(f, 32, 1)
        lane = lax.broadcasted_iota(jnp.int32, f.shape, 1)
        return ropeT(dqp, c1, s1), jnp.where(lane < ROPE, ropeT(f, c1[:, :LANE], s1[:, :LANE]), 0.0)

    dq_pe, dk_pe = rowwise("rope_bwd", rope_bwd, S, T,
                           [(unheads(dq_h, NOPE, 96), 256, 0), (unheads(dk_h, NOPE, 96), 256, 0), (cos, 256, 0),
                            (sin, 256, 0)], [], [(256, BF), (LANE, BF)])
    d_q = jnp.concatenate([unheads(dq_h, 0, NOPE).astype(BF), dq_pe], axis=1)
    d_kv = jnp.concatenate([unheads(dk_h, 0, NOPE), unheads(dv_h, 0, VH)], axis=1).astype(BF)
    d_qn = mm(d_q, W["uq"], name="mm_dqn", trans_b=True)
    d_kvn = mm(d_kv, W["ukv"], name="mm_dkvn", trans_b=True)
    G["w_uq"] = mm(sv["qn"], d_q, name="mm_gw_uq", trans_a=True, out_dtype=BF)
    G["w_ukv"] = mm(sv["kvn"], d_kv, name="mm_gw_ukv", trans_a=True, out_dtype=BF)

    def rms_bwd(ql, kvl, dqn, dkn, dkp, gq, gkv):
        res = []
        for xv, dy, g in ((ql, dqn, gq), (kvl, dkn, gkv)):
            r = lax.rsqrt(jnp.mean(xv * xv, axis=-1, keepdims=True) + RMS_EPS)
            dxh = dy * g
            res.append(((r * (dxh - xv * (r * r) * jnp.mean(dxh * xv, axis=-1, keepdims=True))).astype(BF),
                        _colsum(dy * xv * r)))
        pad = jnp.zeros((ql.shape[0], LANE), BF)
        return jnp.concatenate([res[1][0], dkp, pad, res[0][0], pad], axis=1), res[0][1], res[1][1]

    d_proj, G["q_norm_g"], G["kv_norm_g"] = rowwise(
        "rms_bwd", rms_bwd, S, T,
        [(proj, QL, OFF_Q), (proj, KVL, OFF_KV), (d_qn, QL, 0), (d_kvn, KVL, 0), (dk_pe, LANE, 0)],
        [W["gq"], W["gkv"]], [(NP - OFF_KV, BF)], [QL, KVL], into=(d_proj, NP, OFF_KV))
    du = mm(d_proj, W["in"], name="mm_du", trans_b=True, tk=1024)
    G["w_in"] = mm(sv["u"], d_proj, name="mm_gw_in", trans_a=True, out_dtype=BF)

    def mod_bwd(duv, xv, dr, a):
        return duv * (1.0 + a[1:2, :]) + dr, _colsum(duv), _colsum(duv * xv)

    dx, d_shift, d_scale = rowwise("mod_bwd", mod_bwd, S, 128, [(du, D_MODEL, 0), (x, D_MODEL, 0), (dres, D_MODEL, 0)],
                                   [ada3], [(D_MODEL, F32)], [D_MODEL] * 2)
    d_ada = jnp.concatenate([d_shift, d_scale, d_gate], axis=1)
    return dx, G, d_ada


SMALL = ("conv_a_b", "ln_a_g", "ln_a_b", "q_norm_g", "kv_norm_g", "ln_g", "ln_b")


def _rows(v):
    n = v.shape[0]
    r = -(-n // (LANE * 16)) * 16
    return jnp.pad(v, (0, r * LANE - n)).reshape(r, LANE)


def kernel(x, c, positions, w_ada, b_ada, w_in, conv_a_w, conv_a_b, ln_a_g, ln_a_b, w_a_out, conv_b_w, w_b_out, q_norm_g, kv_norm_g, w_uq, w_ukv, w_c_out, w_o, ln_g, ln_b, loss_target, m_w_ada, m_b_ada, m_w_in, m_conv_a_w, m_conv_a_b, m_ln_a_g, m_ln_a_b, m_w_a_out, m_conv_b_w, m_w_b_out, m_q_norm_g, m_kv_norm_g, m_w_uq, m_w_ukv, m_w_c_out, m_w_o, m_ln_g, m_ln_b, v_w_ada, v_b_ada, v_w_in, v_conv_a_w, v_conv_a_b, v_ln_a_g, v_ln_a_b, v_w_a_out, v_conv_b_w, v_w_b_out, v_q_norm_g, v_kv_norm_g, v_w_uq, v_w_ukv, v_w_c_out, v_w_o, v_ln_g, v_ln_b):
    P = dict(w_ada=w_ada, b_ada=b_ada, w_in=w_in, conv_a_w=conv_a_w, conv_a_b=conv_a_b, ln_a_g=ln_a_g, ln_a_b=ln_a_b,
             w_a_out=w_a_out, conv_b_w=conv_b_w, w_b_out=w_b_out, q_norm_g=q_norm_g, kv_norm_g=kv_norm_g, w_uq=w_uq,
             w_ukv=w_ukv, w_c_out=w_c_out, w_o=w_o, ln_g=ln_g, ln_b=ln_b)
    Mo = dict(w_ada=m_w_ada, b_ada=m_b_ada, w_in=m_w_in, conv_a_w=m_conv_a_w, conv_a_b=m_conv_a_b, ln_a_g=m_ln_a_g,
              ln_a_b=m_ln_a_b, w_a_out=m_w_a_out, conv_b_w=m_conv_b_w, w_b_out=m_w_b_out, q_norm_g=m_q_norm_g,
              kv_norm_g=m_kv_norm_g, w_uq=m_w_uq, w_ukv=m_w_ukv, w_c_out=m_w_c_out, w_o=m_w_o, ln_g=m_ln_g, ln_b=m_ln_b)
    Vo = dict(w_ada=v_w_ada, b_ada=v_b_ada, w_in=v_w_in, conv_a_w=v_conv_a_w, conv_a_b=v_conv_a_b, ln_a_g=v_ln_a_g,
              ln_a_b=v_ln_a_b, w_a_out=v_w_a_out, conv_b_w=v_conv_b_w, w_b_out=v_w_b_out, q_norm_g=v_q_norm_g,
              kv_norm_g=v_kv_norm_g, w_uq=v_w_uq, w_ukv=v_w_ukv, w_c_out=v_w_c_out, w_o=v_w_o, ln_g=v_ln_g, ln_b=v_ln_b)
    ORDER = ("w_ada", "b_ada", "w_in", "conv_a_w", "conv_a_b", "ln_a_g", "ln_a_b", "w_a_out", "conv_b_w", "w_b_out",
             "q_norm_g", "kv_norm_g", "w_uq", "w_ukv", "w_c_out", "w_o", "ln_g", "ln_b")
    L = w_ada.shape[0]
    S = x.shape[1]
    me = 4 * lax.axis_index("x") + 2 * lax.axis_index("y") + lax.axis_index("c")
    x2 = x[0]
    tgt = loss_target[0]

    small_in = _rows(jnp.concatenate([c.reshape(-1), conv_a_w.reshape(-1), conv_b_w.reshape(-1)]))
    w_in_b = jnp.pad(w_in.astype(BF), ((0, 0), (0, 0), (0, W_IN_PAD - IN_SHARD)))
    misc_b = jnp.concatenate([w_a_out, w_b_out, w_c_out, w_ukv, jnp.pad(w_uq, ((0, 0), (0, 0), (0, LANE - UQ_SHARD)))],
                             axis=1).astype(BF)
    w_o_b = w_o.astype(BF)
    gathered = [None] * L
    pending, _ = exchange_begin("gather0", [w_in_b[0], misc_b[0], w_o_b[0]], 3, small_in)
    sg = exchange("gather_small", [small_in], [])[0]
    sgf = sg.reshape(NDEV, -1)
    c_all = sgf[:, :D_MODEL]
    o1 = D_MODEL + L * 31 * 64
    conv_a_full = sgf[:, D_MODEL:o1].reshape(NDEV, L, 31, 64).transpose(1, 2, 0, 3).reshape(L, 31, CW)
    conv_b_full = sgf[:, o1:o1 + L * 3 * 64].reshape(NDEV, L, 3, 64).transpose(1, 2, 0, 3).reshape(L, 3, CW)

    c_act = rowwise("silu_c", _silu, 16, 16, [(jnp.pad(c_all, ((0, 8), (0, 0))), D_MODEL, 0)], [], [(D_MODEL, BF)])[0]
    ncol = w_ada.shape[2]
    w_ada_b = w_ada.astype(BF).transpose(1, 0, 2).reshape(D_MODEL, L * ncol)
    b_mine = lax.dynamic_slice_in_dim(b_ada, me * ncol, ncol, axis=1).reshape(1, L * ncol)
    ada_part = mm(c_act, w_ada_b, name="mm_ada", bias=b_mine)
    ada_rows = -(-(L * ncol) // (LANE * 8)) * 8
    ada_send = jnp.pad(ada_part[:NDEV].reshape(NDEV, -1, LANE), ((0, 0), (0, ada_rows - L * ncol // LANE), (0, 0)))
    ada_recv = exchange("a2a_ada", [], [ada_send])[0]
    ada = ada_recv[:, :L * ncol // LANE].reshape(NDEV, L, ncol).transpose(1, 0, 2).reshape(L, 3, D_MODEL)

    inv_freq = ROPE_THETA ** (-jnp.arange(0, ROPE, 2, dtype=F32) / ROPE)
    ang = positions[0].astype(F32)[:, None] * inv_freq
    tabs = (jnp.tile(jnp.cos(ang), (1, 2 * HEADS)), jnp.tile(jnp.sin(ang), (1, 2 * HEADS)))

    straight = np.arange(D_MODEL)
    fwd_in = [(0, 0, 0, 0, D_MODEL, _to_gathered(IN_PERM, IN_SHARD, W_IN_PAD))]
    fwd_misc = [(0, ROW_A, 0, 0, CW, straight), (0, ROW_B, 1, 0, CW, straight), (0, ROW_C, 2, 0, CW, straight),
                (0, ROW_UKV, 3, 0, KVL, UKV_PERM), (0, ROW_UQ, 4, 0, QL, _to_gathered(UQ_PERM, UQ_SHARD, LANE))]
    rev_in = [(0, 0, 0, 0, D_MODEL, _from_full(_inverse(IN_PERM, D_IN), IN_SHARD, W_IN_PAD))]
    rev_misc = [(0, 0, 0, ROW_A, CW, straight), (1, 0, 0, ROW_B, CW, straight), (2, 0, 0, ROW_C, CW, straight),
                (3, 0, 0, ROW_UKV, KVL, _from_full(_inverse(UKV_PERM, HEADS * (NOPE + VH)), LANE, LANE)),
                (4, 0, 0, ROW_UQ, QL, _from_full(_inverse(UQ_PERM, HEADS * (NOPE + ROPE)), UQ_SHARD, LANE))]

    def layer_weights(l, deps):
        g_in, g_misc, g_o = gathered[l]
        w_in_p = col_gather("relayout_w_in", [g_in], [(D_MODEL, NP)], fwd_in, deps)[0]
        a_out, b_out, c_out, ukv, uq = col_gather(
            "relayout_misc", [g_misc], [(CW, D_MODEL)] * 3 + [(KVL, HEADS * (NOPE + VH)), (QL, HEADS * (NOPE + ROPE))],
            fwd_misc, deps)
        return {
            "in": w_in_p, "a_out": a_out, "b_out": b_out, "c_out": c_out, "uq": uq, "ukv": ukv,
            "o": g_o.reshape(D_MODEL, D_MODEL),
            "conv_a": jnp.pad(conv_a_full[l], ((0, 1), (0, 0))), "conv_b": jnp.pad(conv_b_full[l], ((0, 5), (0, 0))),
            "vec_a": jnp.stack([conv_a_b[l], ln_a_g[l], ln_a_b[l]]), "gq": q_norm_g[l][None], "gkv": kv_norm_g[l][None],
            "lnv": jnp.stack([ln_g[l], ln_b[l]]),
        }

    h = x2
    saved, weights = [], []
    gathered[0] = exchange_end(pending, ada)
    for l in range(L):
        ada_l, deps = ada[l], ()
        if l + 1 < L:
            pending, token = exchange_begin("gather%d" % (l + 1), [w_in_b[l + 1], misc_b[l + 1], w_o_b[l + 1]], 3,
                                            gathered[l][0])
            ada_l, deps = ada_l + token[0, 0], (token,)
        Wl = layer_weights(l, deps)
        h, sv = layer_fwd(h, ada_l, Wl, tabs, S)
        if l + 1 < L:
            gathered[l + 1] = exchange_end(pending, h)
        saved.append(sv)
        weights.append(Wl)

    def loss_fn(y, t):
        e = y - t
        return e * (1.0 / D_MODEL), _colsum(e * e)

    dy, sq = rowwise("loss", loss_fn, S, 256, [(h, D_MODEL, 0), (tgt, D_MODEL, 0)], [], [(D_MODEL, F32)], [D_MODEL])
    loss = lax.psum(0.5 * jnp.sum(sq) / D_MODEL, ("x", "y", "c"))

    grads, d_adas, recv = [None] * L, [None] * L, [None] * L
    pending, token = None, None
    for l in reversed(range(L)):
        ada_l = ada[l] if token is None else ada[l] + token[0, 0]
        dy, g, d_adas[l] = layer_bwd(dy, saved[l], ada_l, weights[l], tabs, S)
        grads[l] = g
        if pending is not None:
            recv[l + 1] = exchange_end(pending, dy)
        send_in = col_gather("unrelayout_w_in", [g["w_in"]], [(NDEV, D_MODEL, W_IN_PAD)], rev_in)[0]
        send_misc = col_gather("unrelayout_misc", [g["w_a_out"], g["w_b_out"], g["w_c_out"], g["w_ukv"], g["w_uq"]],
                               [(NDEV, MISC_ROWS, LANE)], rev_misc)[0]
        send_o = g["w_o"].reshape(NDEV, D_MODEL // NDEV, D_MODEL)
        pending, token = exchange_begin("scatter%d" % l, [send_in, send_misc, send_o], 0,
                                        dy if l + 1 == L else recv[l + 1][0])
    grad_x = dy[None]

    small_parts = [jnp.stack([grads[l][n].reshape(-1) for l in range(L)]).reshape(-1) for n in SMALL]
    small_parts.append(jnp.stack([grads[l]["conv_a_w"][:31].reshape(-1) for l in range(L)]).reshape(-1))
    small_parts.append(jnp.stack([grads[l]["conv_b_w"][:3].reshape(-1) for l in range(L)]).reshape(-1))
    small_parts.append(jnp.stack([d_adas[l].reshape(-1) for l in range(L)]).reshape(-1))
    small_sizes = [int(p.shape[0]) for p in small_parts]
    gsmall = exchange("gather_small_grads", [_rows(jnp.concatenate(small_parts))], [])[0]
    gsum = sum_slots("sum_small", gsmall).reshape(-1)
    recv[0] = exchange_end(pending, gsum)
    Gr = {}
    offs = np.cumsum([0] + small_sizes)
    for i, n in enumerate(SMALL):
        Gr[n] = gsum[offs[i]:offs[i + 1]].reshape(L, -1)
    ca = gsum[offs[7]:offs[8]].reshape(L, 31, CW)
    cbw = gsum[offs[8]:offs[9]].reshape(L, 3, CW)
    Gr["conv_a_w"] = lax.dynamic_slice_in_dim(ca, me * 64, 64, axis=2)
    Gr["conv_b_w"] = lax.dynamic_slice_in_dim(cbw, me * 64, 64, axis=2)
    Gr["b_ada"] = gsum[offs[9]:offs[10]].reshape(L, 3 * D_MODEL)
    d_ada_all = gsmall.reshape(NDEV, -1)[:, offs[9]:offs[10]].reshape(NDEV, L, 3 * D_MODEL)
    d_mine = lax.dynamic_slice_in_dim(d_ada_all, me * ncol, ncol, axis=2).reshape(NDEV, L * ncol)
    g_ada = mm(c_act, jnp.pad(d_mine, ((0, 8), (0, 0))).astype(BF), name="mm_gw_ada", trans_a=True)
    Gr["w_ada"] = g_ada.reshape(D_MODEL, L, ncol).transpose(1, 0, 2)

    D, NM, NV = {}, {}, {}
    D["w_ada"], NM["w_ada"], NV["w_ada"] = adamw("adamw_w_ada", P["w_ada"], Gr["w_ada"], Mo["w_ada"], Vo["w_ada"])
    Gr["w_in"], D["w_in"], NM["w_in"], NV["w_in"] = sum_adamw(
        "sum_adamw_w_in", [recv[l][0] for l in range(L)], P["w_in"], Mo["w_in"], Vo["w_in"])
    Gr["w_o"], D["w_o"], NM["w_o"], NV["w_o"] = sum_adamw(
        "sum_adamw_w_o", [recv[l][2] for l in range(L)], P["w_o"], Mo["w_o"], Vo["w_o"])
    misc = lambda T_: jnp.concatenate([T_["w_a_out"], T_["w_b_out"], T_["w_c_out"], T_["w_ukv"],
                                       jnp.pad(T_["w_uq"], ((0, 0), (0, 0), (0, LANE - UQ_SHARD)))], axis=1)
    res = sum_adamw("sum_adamw_misc", [recv[l][1] for l in range(L)], misc(P), misc(Mo), misc(Vo))
    for T_, r in zip((Gr, D, NM, NV), res):
        T_["w_a_out"], T_["w_b_out"], T_["w_c_out"] = r[:, ROW_A:ROW_B], r[:, ROW_B:ROW_C], r[:, ROW_C:ROW_UKV]
        T_["w_ukv"], T_["w_uq"] = r[:, ROW_UKV:ROW_UQ], r[:, ROW_UQ:MISC_ROWS, :UQ_SHARD]
    packed =("b_ada", "conv_a_w", "conv_b_w") + SMALL
    pk = lambda T_: _rows(jnp.concatenate([T_[n].reshape(-1) for n in packed]))[None]
    dS, mS, vS = adamw("adamw_small", pk(P), pk(Gr), pk(Mo), pk(Vo))
    o = 0
    for n in packed:
        sz = int(np.prod(P[n].shape))
        D[n] = dS.reshape(-1)[o:o + sz].reshape(P[n].shape)
        NM[n] = mS.reshape(-1)[o:o + sz].reshape(P[n].shape)
        NV[n] = vS.reshape(-1)[o:o + sz].reshape(P[n].shape)
        o += sz
    return (loss, grad_x, *[Gr[n] for n in ORDER], *[D[n] for n in ORDER], *[NM[n] for n in ORDER],
            *[NV[n] for n in ORDER])
```

```python
import functools
import math

import numpy as np
import jax
import jax.numpy as jnp
from jax import lax
from jax.experimental import pallas as pl
from jax.experimental.pallas import tpu as pltpu

BF = jnp.bfloat16
F32 = jnp.float32
MESH = pl.DeviceIdType.MESH
NDEV = 8

HEADS, NOPE, ROPE, VH = 8, 64, 32, 64
HP = 128
ROPE_THETA = 10000.0
LN_EPS = 1e-5
RMS_EPS = 1e-6
LR, B1, B2, EPS, WD, STEP = 0.001, 0.9, 0.999, 1e-08, 0.01, 10

LANE = 128
VMEM_LIMIT = 56 * 1024 * 1024

D_MODEL, CW, QL, KVL = 1024, 512, 384, 256
OFF_M, OFF_A, OFF_AG, OFF_B, OFF_CG, OFF_GB, OFF_BG = 0, 3072, 4096, 4608, 5632, 6144, 6656
OFF_KV, OFF_KR, OFF_Q, NP = 7168, 7424, 7680, 8192
D_IN = 7840


def _cparams(**kw):
    return pltpu.CompilerParams(vmem_limit_bytes=VMEM_LIMIT, **kw)


def _sigmoid(x):
    return jax.nn.sigmoid(x)


def _silu(x):
    return x * _sigmoid(x)


def _dsilu(x):
    s = _sigmoid(x)
    return s * (1.0 + x * (1.0 - s))


def _pick_tile(n, cap, mult):
    if n <= cap:
        return n
    for t in range(cap - cap % mult, 0, -mult):
        if n % t == 0:
            return t
    raise ValueError((n, cap, mult))


def mm(a, b, *, name, trans_a=False, trans_b=False, out_dtype=F32, bias=None, tm=1024, tn=1024, tk=2048):
    if trans_a:
        K, M = a.shape
    else:
        M, K = a.shape
    if trans_b:
        N, K2 = b.shape
    else:
        K2, N = b.shape
    assert K == K2 and not (trans_a and trans_b), (a.shape, b.shape)
    tm, tn = _pick_tile(M, tm, 16), _pick_tile(N, tn, LANE)
    tk = _pick_tile(K, tk, LANE if trans_b else 16)
    assert M % tm == 0 and N % tn == 0 and K % tk == 0, (M, N, K, tm, tn, tk)
    nk = K // tk
    dims = (((0 if trans_a else 1,), (1 if trans_b else 0,)), ((), ()))
    has_bias = bias is not None

    def body(*refs):
        a_ref, b_ref = refs[0], refs[1]
        bias_ref = refs[2] if has_bias else None
        o_ref = refs[3] if has_bias else refs[2]
        p = lax.dot_general(a_ref[...], b_ref[...], dims, preferred_element_type=F32)

        def finish(v):
            if has_bias:
                v = v + bias_ref[...]
            o_ref[...] = v.astype(o_ref.dtype)

        if nk == 1:
            finish(p)
        else:
            acc = refs[-1]
            k = pl.program_id(2)

            @pl.when(k == 0)
            def _():
                acc[...] = p

            @pl.when(k > 0)
            def _():
                acc[...] += p

            @pl.when(k == nk - 1)
            def _():
                finish(acc[...])

    if trans_a:
        a_spec = pl.BlockSpec((tk, tm), lambda i, j, k: (k, i))
    else:
        a_spec = pl.BlockSpec((tm, tk), lambda i, j, k: (i, k))
    if trans_b:
        b_spec = pl.BlockSpec((tn, tk), lambda i, j, k: (j, k))
    else:
        b_spec = pl.BlockSpec((tk, tn), lambda i, j, k: (k, j))
    in_specs = [a_spec, b_spec]
    args = [a, b]
    if has_bias:
        in_specs.append(pl.BlockSpec((1, tn), lambda i, j, k: (0, j)))
        args.append(bias)
    return pl.pallas_call(
        body, name=name, grid=(M // tm, N // tn, nk),
        in_specs=in_specs, out_specs=pl.BlockSpec((tm, tn), lambda i, j, k: (i, j)),
        out_shape=jax.ShapeDtypeStruct((M, N), out_dtype),
        scratch_shapes=[pltpu.VMEM((tm, tn), F32)] if nk > 1 else [],
        compiler_params=_cparams(),
    )(*args)


def rowwise(name, fn, S, T, row_ins, full_ins, row_outs, acc_outs=(), into=None):
    n_in = len(row_ins) + len(full_ins)
    n_ro, n_ao = len(row_outs), len(acc_outs)
    alias = into is not None and into[0] is not None

    def body(*refs):
        vals = [r[...] for r in refs[:n_in]]
        outs = fn(*vals)
        if not isinstance(outs, (tuple, list)):
            outs = (outs,)
        assert len(outs) == n_ro + n_ao, (name, len(outs))
        o0 = n_in + (1 if alias else 0)
        for r, v in zip(refs[o0:o0 + n_ro], outs[:n_ro]):
            r[...] = v.astype(r.dtype)
        first = pl.program_id(0) == 0
        for r, v in zip(refs[o0 + n_ro:], outs[n_ro:]):
            def init(r=r, v=v):
                r[...] = v

            def accum(r=r, v=v):
                r[...] += v

            pl.when(first)(init)
            pl.when(jnp.logical_not(first))(accum)

    in_specs, args = [], []
    for arr, W, off in row_ins:
        assert off % W == 0 and arr.shape[0] == S, (name, arr.shape, W, off)
        in_specs.append(pl.BlockSpec((T, W), functools.partial(lambda i, cb: (i, cb), cb=off // W)))
        args.append(arr)
    for arr in full_ins:
        in_specs.append(pl.BlockSpec(arr.shape, lambda i: (0, 0)))
        args.append(arr)
    out_specs = [pl.BlockSpec((T, W), lambda i: (i, 0)) for W, _ in row_outs]
    out_shape = [jax.ShapeDtypeStruct((S, W), dt) for W, dt in row_outs]
    aliases = {}
    if into is not None:
        buf, total, off = into
        W0, dt0 = row_outs[0]
        assert off % W0 == 0
        out_specs[0] = pl.BlockSpec((T, W0), functools.partial(lambda i, cb: (i, cb), cb=off // W0))
        out_shape[0] = jax.ShapeDtypeStruct((S, total), dt0)
        if alias:
            in_specs.append(ANY_SPEC)
            args.append(buf)
            aliases = {n_in: 0}
    out_specs += [pl.BlockSpec((1, W), lambda i: (0, 0)) for W in acc_outs]
    out_shape += [jax.ShapeDtypeStruct((1, W), F32) for W in acc_outs]
    return pl.pallas_call(
        body, name=name, grid=(S // T,), in_specs=in_specs, out_specs=out_specs, out_shape=out_shape,
        input_output_aliases=aliases, compiler_params=_cparams(),
    )(*args)


def _colsum(v):
    return jnp.sum(v, axis=0, keepdims=True)


def _ln_stats(r):
    mu = jnp.mean(r, axis=-1, keepdims=True)
    d = r - mu
    var = jnp.mean(d * d, axis=-1, keepdims=True)
    rstd = lax.rsqrt(var + LN_EPS)
    return d * rstd, rstd


def _ln_bwd(dn, n, rstd):
    return rstd * (dn - jnp.mean(dn, axis=-1, keepdims=True) - n * jnp.mean(dn * n, axis=-1, keepdims=True))


CPAD = 32
TC = 64


def _pre(mode, x1, x2):
    return x1 * _sigmoid(x2) if mode == "glu" else x1 * x2


def _shifted(ext, sft):
    n = TC + CPAD
    return pltpu.roll(ext, (n - sft) % n, 0)[0:TC]


def _interleaved_specs(S, off):
    return [pl.BlockSpec((S, LANE), functools.partial(lambda j, o: (0, o + 2 * j), o=off // LANE)),
            pl.BlockSpec((S, LANE), functools.partial(lambda j, o: (0, o + 2 * j + 1), o=off // LANE))]


def conv_fwd(name, src, off, w_pad, taps, mode, S, C):
    nchunk = S // TC

    def body(x1_ref, x2_ref, w_ref, o_ref, a_pad):
        a_pad[0:CPAD, :] = jnp.zeros((CPAD, LANE), F32)

        def fill(i, _):
            r = pl.multiple_of(i * 256, 256)
            a_pad[pl.ds(CPAD + r, 256), :] = _pre(mode, x1_ref[pl.ds(r, 256), :], x2_ref[pl.ds(r, 256), :])
            return 0

        lax.fori_loop(0, S // 256, fill, 0)

        def chunk(i, _):
            base = pl.multiple_of(i * TC, TC)
            ext = a_pad[pl.ds(base, TC + CPAD), :]
            acc = jnp.zeros((TC, LANE), F32)
            for k in range(taps):
                acc = acc + w_ref[pl.ds(k, 1), :] * _shifted(ext, CPAD - (taps - 1) + k)
            o_ref[pl.ds(base, TC), :] = acc
            return 0

        lax.fori_loop(0, nchunk, chunk, 0)

    kp = w_pad.shape[0]
    return pl.pallas_call(
        body, name=name, grid=(C // LANE,),
        in_specs=_interleaved_specs(S, off) + [pl.BlockSpec((kp, LANE), lambda j: (0, j))],
        out_specs=pl.BlockSpec((S, LANE), lambda j: (0, j)),
        out_shape=jax.ShapeDtypeStruct((S, C), F32),
        scratch_shapes=[pltpu.VMEM((S + CPAD, LANE), F32)],
        compiler_params=_cparams(),
    )(src, src, w_pad)


def conv_bwd(name, src, off, dc, w_pad, taps, mode, S, C, buf):
    nchunk = S // TC
    kp = w_pad.shape[0]

    def body(x1_ref, x2_ref, dc_ref, w_ref, _, d_ref, dw_ref, a_pad, dc_pad, dw_acc):
        a_pad[0:CPAD, :] = jnp.zeros((CPAD, LANE), F32)
        dc_pad[S:S + CPAD, :] = jnp.zeros((CPAD, LANE), F32)
        dw_acc[...] = jnp.zeros(dw_acc.shape, F32)

        def fill(i, _):
            r = pl.multiple_of(i * 256, 256)
            a_pad[pl.ds(CPAD + r, 256), :] = _pre(mode, x1_ref[pl.ds(r, 256), :], x2_ref[pl.ds(r, 256), :])
            dc_pad[pl.ds(r, 256), :] = dc_ref[pl.ds(r, 256), :]
            return 0

        lax.fori_loop(0, S // 256, fill, 0)

        def chunk(i, _):
            base = pl.multiple_of(i * TC, TC)
            ext_d = dc_pad[pl.ds(base, TC + CPAD), :]
            ext_a = a_pad[pl.ds(base, TC + CPAD), :]
            dcv = ext_d[0:TC]
            da = jnp.zeros((TC, LANE), F32)
            for k in range(taps):
                da = da + w_ref[pl.ds(k, 1), :] * _shifted(ext_d, taps - 1 - k)
                prod = dcv * _shifted(ext_a, CPAD - (taps - 1) + k)
                fold = prod[0:8]
                for g in range(1, TC // 8):
                    fold = fold + prod[8 * g:8 * g + 8]
                dw_acc[pl.ds(8 * k, 8), :] += fold
            x1 = x1_ref[pl.ds(base, TC), :]
            x2 = x2_ref[pl.ds(base, TC), :]
            if mode == "glu":
                s = _sigmoid(x2)
                d1, d2 = da * s, da * x1 * s * (1.0 - s)
            else:
                d1, d2 = da * x2, da * x1
            d_ref[pl.ds(base, TC), 0:LANE] = d1.astype(BF)
            d_ref[pl.ds(base, TC), LANE:2 * LANE] = d2.astype(BF)
            return 0

        lax.fori_loop(0, nchunk, chunk, 0)
        dw_ref[...] = jnp.zeros(dw_ref.shape, F32)
        for k in range(taps):
            dw_ref[pl.ds(k, 1), :] = jnp.sum(dw_acc[pl.ds(8 * k, 8), :], axis=0, keepdims=True)

    blk = pl.BlockSpec((S, LANE), lambda j: (0, j))
    return pl.pallas_call(
        body, name=name, grid=(C // LANE,),
        in_specs=_interleaved_specs(S, off) + [blk, pl.BlockSpec((kp, LANE), lambda j: (0, j)), ANY_SPEC],
        out_specs=[pl.BlockSpec((S, 2 * LANE), functools.partial(lambda j, o: (0, o + j), o=off // (2 * LANE))),
                   pl.BlockSpec((kp, LANE), lambda j: (0, j))],
        out_shape=[jax.ShapeDtypeStruct(buf.shape, BF), jax.ShapeDtypeStruct((kp, C), F32)],
        input_output_aliases={4: 0},
        scratch_shapes=[pltpu.VMEM((S + CPAD, LANE), F32), pltpu.VMEM((S + CPAD, LANE), F32),
                        pltpu.VMEM((8 * kp, LANE), F32)],
        compiler_params=_cparams(),
    )(src, src, dc, w_pad, buf)


TQ, TK = 128, 256
QUADS = HEADS // 4
QW, KVW = 4 * (NOPE + ROPE), 4 * (NOPE + VH)
SCALE = (NOPE + ROPE) ** -0.5
NT_DIMS = (((1,), (1,)), ((), ()))
TN_DIMS = (((0,), (0,)), ((), ()))


def _lane_mask(width, group, dtype):
    lane = lax.broadcasted_iota(jnp.int32, (1, LANE), 1)
    return jnp.where(lane // width == group, 1.0, 0.0).astype(dtype)


def _visible(qi, nfull):
    row = lax.broadcasted_iota(jnp.int32, (TQ, TK), 0)
    col = lax.broadcasted_iota(jnp.int32, (TQ, TK), 1)
    return col <= row + (qi * TQ - nfull * TK)


def attn_fwd(q, kv, kpe, S):
    nq = S // TQ

    def body(q_ref, kv_ref, kp_ref, o_ref, lse_ref):
        for t in range(2):
            def q_block(qi, _, t=t):
                r0 = pl.multiple_of(qi * TQ, TQ)
                qn = q_ref[pl.ds(r0, TQ), t * LANE:(t + 1) * LANE]
                qp = q_ref[pl.ds(r0, TQ), 2 * LANE:3 * LANE]
                qcat = [jnp.concatenate([qn * _lane_mask(NOPE, hh, BF), qp * _lane_mask(ROPE, 2 * t + hh, BF)], axis=1)
                        for hh in range(2)]
                nfull = (qi * TQ) // TK

                def step(kj, carry, masked):
                    c0 = pl.multiple_of(kj * TK, TK)
                    kc = jnp.concatenate([kv_ref[pl.ds(c0, TK), t * LANE:(t + 1) * LANE], kp_ref[pl.ds(c0, TK), :]],
                                         axis=1)
                    vt = kv_ref[pl.ds(c0, TK), (2 + t) * LANE:(3 + t) * LANE]
                    new = []
                    for hh in range(2):
                        m, l, acc = carry[hh]
                        s = lax.dot_general(qcat[hh], kc, NT_DIMS, preferred_element_type=F32) * SCALE
                        if masked:
                            s = jnp.where(_visible(qi, nfull), s, -jnp.inf)
                        m_new = jnp.maximum(m, jnp.max(s, axis=-1, keepdims=True))
                        p = jnp.exp(s - m_new)
                        alpha = jnp.exp(m - m_new)
                        l = alpha * l + jnp.sum(p, axis=-1, keepdims=True)
                        acc = alpha * acc + jnp.dot(p.astype(BF), vt, preferred_element_type=F32)
                        new.append((m_new, l, acc))
                    return tuple(new)

                one = (jnp.full((TQ, 1), -jnp.inf, F32), jnp.zeros((TQ, 1), F32), jnp.zeros((TQ, LANE), F32))
                carry = lax.fori_loop(0, nfull, lambda kj, c: step(kj, c, False), (one, one))
                (m0, l0, a0), (m1, l1, a1) = step(nfull, carry, True)
                first = lax.broadcasted_iota(jnp.int32, (TQ, LANE), 1) < NOPE
                o_ref[pl.ds(r0, TQ), t * LANE:(t + 1) * LANE] = jnp.where(first, a0 / l0, a1 / l1)
                lse_ref[pl.ds(r0, TQ), t * LANE:(t + 1) * LANE] = jnp.where(first, m0 + jnp.log(l0), m1 + jnp.log(l1))
                return 0

            lax.fori_loop(0, nq, q_block, 0)

    return pl.pallas_call(
        body, name="attn_fwd", grid=(QUADS,),
        in_specs=[pl.BlockSpec((S, QW), lambda g: (0, g)), pl.BlockSpec((S, KVW), lambda g: (0, g)),
                  pl.BlockSpec((S, LANE), lambda g: (0, 0))],
        out_specs=[pl.BlockSpec((S, 2 * LANE), lambda g: (0, g))] * 2,
        out_shape=[jax.ShapeDtypeStruct((S, HEADS * VH), F32)] * 2,
        compiler_params=_cparams(),
    )(q, kv, kpe)


def attn_bwd(q, kv, kpe, o, lse, do, S):
    nq = S // TQ

    def body(q_ref, kv_ref, kp_ref, o_ref, lse_ref, do_ref, dq_ref, dkv_ref, dkp_ref, dq_acc, dk_acc, dv_acc):
        first = lax.broadcasted_iota(jnp.int32, (TQ, LANE), 1) < NOPE
        for t in range(2):
            dk_acc[...] = jnp.zeros(dk_acc.shape, F32)
            dv_acc[...] = jnp.zeros(dv_acc.shape, F32)

            def q_block(qi, _, t=t):
                r0 = pl.multiple_of(qi * TQ, TQ)
                cols = slice(t * LANE, (t + 1) * LANE)
                qn = q_ref[pl.ds(r0, TQ), cols]
                qp = q_ref[pl.ds(r0, TQ), 2 * LANE:3 * LANE]
                dof = do_ref[pl.ds(r0, TQ), cols]
                prod = dof * o_ref[pl.ds(r0, TQ), cols]
                lse_t = lse_ref[pl.ds(r0, TQ), cols]
                qcat, dob, lse_h, delta = [], [], [], []
                for hh in range(2):
                    mn = _lane_mask(NOPE, hh, F32)
                    qcat.append(jnp.concatenate([qn * _lane_mask(NOPE, hh, BF), qp * _lane_mask(ROPE, 2 * t + hh, BF)],
                                                axis=1))
                    dob.append((dof * mn).astype(BF))
                    delta.append(jnp.sum(prod * mn, axis=-1, keepdims=True))
                    lse_h.append(lse_t[:, hh * NOPE:hh * NOPE + 1])
                nfull = (qi * TQ) // TK
                dq_acc[...] = jnp.zeros(dq_acc.shape, F32)

                def step(kj, _, masked):
                    c0 = pl.multiple_of(kj * TK, TK)
                    kc = jnp.concatenate([kv_ref[pl.ds(c0, TK), cols], kp_ref[pl.ds(c0, TK), :]], axis=1)
                    vt = kv_ref[pl.ds(c0, TK), (2 + t) * LANE:(3 + t) * LANE]
                    dkc, dvt = None, None
                    for hh in range(2):
                        s = lax.dot_general(qcat[hh], kc, NT_DIMS, preferred_element_type=F32) * SCALE
                        if masked:
                            s = jnp.where(_visible(qi, nfull), s, -jnp.inf)
                        p = jnp.exp(s - lse_h[hh])
                        dp = lax.dot_general(dob[hh], vt, NT_DIMS, preferred_element_type=F32)
                        ds = (p * (dp - delta[hh]) * SCALE).astype(BF)
                        dv1 = lax.dot_general(p.astype(BF), dob[hh], TN_DIMS, preferred_element_type=F32)
                        dk1 = lax.dot_general(ds, qcat[hh], TN_DIMS, preferred_element_type=F32)
                        dvt = dv1 if dvt is None else dvt + dv1
                        dkc = dk1 if dkc is None else dkc + dk1
                        dq_acc[hh] += jnp.dot(ds, kc, preferred_element_type=F32)
                    dv_acc[pl.ds(c0, TK), :] += dvt
                    dk_acc[pl.ds(c0, TK), :] += dkc
                    return 0

                lax.fori_loop(0, nfull, lambda kj, c: step(kj, c, False), 0)
                step(nfull, 0, True)
                d0, d1 = dq_acc[0], dq_acc[1]
                dq_ref[pl.ds(r0, TQ), cols] = jnp.where(first, d0[:, :LANE], d1[:, :LANE])
                pe = d0[:, LANE:] * _lane_mask(ROPE, 2 * t, F32) + d1[:, LANE:] * _lane_mask(ROPE, 2 * t + 1, F32)
                if t == 0:
                    dq_ref[pl.ds(r0, TQ), 2 * LANE:3 * LANE] = pe
                else:
                    dq_ref[pl.ds(r0, TQ), 2 * LANE:3 * LANE] += pe
                return 0

            lax.fori_loop(0, nq, q_block, 0)
            dkv_ref[:, t * LANE:(t + 1) * LANE] = dk_acc[:, :LANE].astype(BF)
            dkv_ref[:, (2 + t) * LANE:(3 + t) * LANE] = dv_acc[...].astype(BF)
            if t == 0:
                dkp_ref[...] = dk_acc[:, LANE:]
            else:
                dkp_ref[...] += dk_acc[:, LANE:]

    qspec = pl.BlockSpec((S, QW), lambda g: (0, g))
    kvspec = pl.BlockSpec((S, KVW), lambda g: (0, g))
    ospec = pl.BlockSpec((S, 2 * LANE), lambda g: (0, g))
    return pl.pallas_call(
        body, name="attn_bwd", grid=(QUADS,),
        in_specs=[qspec, kvspec, pl.BlockSpec((S, LANE), lambda g: (0, 0)), ospec, ospec, ospec],
        out_specs=[qspec, kvspec, pl.BlockSpec((S, LANE), lambda g: (0, g))],
        out_shape=[jax.ShapeDtypeStruct((S, HEADS * (NOPE + ROPE)), F32), jax.ShapeDtypeStruct((S, HEADS * (NOPE + VH)), BF),
                   jax.ShapeDtypeStruct((S, HEADS * ROPE), F32)],
        scratch_shapes=[pltpu.VMEM((2, TQ, 2 * LANE), F32), pltpu.VMEM((S, 2 * LANE), F32), pltpu.VMEM((S, LANE), F32)],
        compiler_params=_cparams(),
    )(q, kv, kpe, o, lse, do)


def exchange(name, gathers, a2as):
    n_g, n = len(gathers), len(gathers) + len(a2as)

    def body(*refs):
        ins, outs = refs[:n], refs[n:2 * n]
        send_sems, recv_sems, loc_sems = refs[2 * n:]
        x, y, c = lax.axis_index("x"), lax.axis_index("y"), lax.axis_index("c")
        me = 4 * x + 2 * y + c

        def peer(k):
            px = 1 - x if k & 4 else x
            py = 1 - y if k & 2 else y
            pc = 1 - c if k & 1 else c
            return (px, py, pc), 4 * px + 2 * py + pc

        def remote(a, k):
            pid, pflat = peer(k)
            src = ins[a] if a < n_g else ins[a].at[pflat]
            return pltpu.make_async_remote_copy(
                src_ref=src, dst_ref=outs[a].at[me], send_sem=send_sems.at[a, k - 1], recv_sem=recv_sems.at[a, k - 1],
                device_id=pid, device_id_type=MESH)

        def arrival(a, k):
            pid, pflat = peer(k)
            src = ins[a] if a < n_g else ins[a].at[pflat]
            return pltpu.make_async_remote_copy(
                src_ref=src, dst_ref=outs[a].at[pflat], send_sem=send_sems.at[a, k - 1], recv_sem=recv_sems.at[a, k - 1],
                device_id=pid, device_id_type=MESH)

        local = []
        for a in range(n):
            own = ins[a] if a < n_g else ins[a].at[me]
            cp = pltpu.make_async_copy(own, outs[a].at[me], loc_sems.at[a])
            cp.start()
            local.append(cp)
        sent = []
        for k in (1, 2, 4, 3, 5, 6, 7):
            for a in range(n):
                cp = remote(a, k)
                cp.start()
                sent.append(cp)
        for k in range(1, 8):
            for a in range(n):
                arrival(a, k).wait_recv()
        for cp in sent:
            cp.wait_send()
        for cp in local:
            cp.wait()

    out_shape = [jax.ShapeDtypeStruct((NDEV,) + g.shape, g.dtype) for g in gathers]
    out_shape += [jax.ShapeDtypeStruct(a.shape, a.dtype) for a in a2as]
    any_spec = pl.BlockSpec(memory_space=pl.ANY)
    return pl.pallas_call(
        body, name=name, in_specs=[any_spec] * n, out_specs=[any_spec] * n, out_shape=out_shape,
        scratch_shapes=[pltpu.SemaphoreType.DMA((n, NDEV - 1)), pltpu.SemaphoreType.DMA((n, NDEV - 1)),
                        pltpu.SemaphoreType.DMA((n,))],
    )(*gathers, *a2as)


def _peer(k, x, y, c):
    px = 1 - x if k & 4 else x
    py = 1 - y if k & 2 else y
    pc = 1 - c if k & 1 else c
    return (px, py, pc), 4 * px + 2 * py + pc


PEER_ORDER = (1, 2, 4, 3, 5, 6, 7)
HBM_SPEC = pl.BlockSpec(memory_space=pltpu.HBM)
SEM_SPEC = pl.BlockSpec(memory_space=pltpu.SEMAPHORE)
ANY_SPEC = pl.BlockSpec(memory_space=pl.ANY)


def _split_copies(ins, lands, n_g, send_sems, recv_sems):
    x, y, c = lax.axis_index("x"), lax.axis_index("y"), lax.axis_index("c")
    me = 4 * x + 2 * y + c

    def outgoing(a, k):
        pid, pflat = _peer(k, x, y, c)
        src = ins[a] if a < n_g else ins[a].at[pflat]
        return pltpu.make_async_remote_copy(
            src_ref=src, dst_ref=lands[a].at[me], send_sem=send_sems.at[a * (NDEV - 1) + k - 1],
            recv_sem=recv_sems.at[a * (NDEV - 1) + k - 1],
            device_id=pid, device_id_type=MESH)

    def arrival(a, k):
        pid, pflat = _peer(k, x, y, c)
        src = ins[a] if a < n_g else ins[a].at[pflat]
        return pltpu.make_async_remote_copy(
            src_ref=src, dst_ref=lands[a].at[pflat], send_sem=send_sems.at[a * (NDEV - 1) + k - 1],
            recv_sem=recv_sems.at[a * (NDEV - 1) + k - 1],
            device_id=pid, device_id_type=MESH)

    return outgoing, arrival


def exchange_begin(name, srcs, n_g, dep):
    n = len(srcs)
    land_shapes = [((NDEV,) + s.shape) if a < n_g else s.shape for a, s in enumerate(srcs)]

    def own_body(*refs):
        ins, outs = refs[:n], refs[n + 1:2 * n + 1]
        stage, sems = refs[2 * n + 1:3 * n + 1], refs[-1]
        me = 4 * lax.axis_index("x") + 2 * lax.axis_index("y") + lax.axis_index("c")
        cps = [pltpu.make_async_copy(ins[a] if a < n_g else ins[a].at[me], stage[a], sems.at[a]) for a in range(n)]
        for cp in cps:
            cp.start()
        for cp in cps:
            cp.wait()
        cps = [pltpu.make_async_copy(stage[a], outs[a].at[me], sems.at[a]) for a in range(n)]
        for cp in cps:
            cp.start()
        for cp in cps:
            cp.wait()

    lands = pl.pallas_call(
        own_body, name=name + "_own", in_specs=[ANY_SPEC] * (n + 1), out_specs=[ANY_SPEC] * n,
        out_shape=[jax.ShapeDtypeStruct(sh, s.dtype) for sh, s in zip(land_shapes, srcs)],
        scratch_shapes=[pltpu.VMEM(sh[1:], s.dtype) for sh, s in zip(land_shapes, srcs)] + [pltpu.SemaphoreType.DMA((n,))],
        compiler_params=_cparams(),
    )(*srcs, dep)

    def start_body(*refs):
        ins, lz = refs[:n], refs[n:2 * n]
        send_sems, recv_sems, token = refs[2 * n], refs[2 * n + 1], refs[-1]
        outgoing, _ = _split_copies(ins, lz, n_g, send_sems, recv_sems)
        for k in PEER_ORDER:
            for a in range(n):
                outgoing(a, k).start()
        token[...] = jnp.zeros(token.shape, F32)

    hbm = lambda t: pltpu.HBM(t.shape, t.dtype)
    res = pl.pallas_call(
        start_body, name=name + "_start",
        out_shape=(pltpu.SemaphoreType.DMA((n * (NDEV - 1),)), pltpu.SemaphoreType.DMA((n * (NDEV - 1),)),
                   *[hbm(s) for s in srcs], *[hbm(t) for t in lands], jax.ShapeDtypeStruct((8, LANE), F32)),
        in_specs=[HBM_SPEC] * (2 * n),
        out_specs=(SEM_SPEC, SEM_SPEC, *[HBM_SPEC] * (2 * n), pl.BlockSpec(memory_space=pltpu.VMEM)),
        input_output_aliases={i: 2 + i for i in range(2 * n)},
        compiler_params=pltpu.CompilerParams(has_side_effects=pltpu.SideEffectType.DATAFLOW_SIDE_EFFECTING),
    )(*[pltpu.with_memory_space_constraint(t, pltpu.HBM) for t in list(srcs) + list(lands)])
    return (name, n, n_g, res[:-1]), res[-1]


def exchange_end(handle, after):
    name, n, n_g, (send_sems, recv_sems, *bufs) = handle

    def wait_body(*refs):
        ins, lz = refs[:n], refs[n:2 * n]
        ss, rs = refs[2 * n], refs[2 * n + 1]
        outgoing, arrival = _split_copies(ins, lz, n_g, ss, rs)
        for k in range(1, NDEV):
            for a in range(n):
                arrival(a, k).wait_recv()
        for k in range(1, NDEV):
            for a in range(n):
                outgoing(a, k).wait_send()

    res = pl.pallas_call(
        wait_body, name=name + "_wait", out_shape=tuple(pltpu.HBM(t.shape, t.dtype) for t in bufs),
        in_specs=[HBM_SPEC] * (2 * n) + [SEM_SPEC, SEM_SPEC, ANY_SPEC], out_specs=[HBM_SPEC] * (2 * n),
        input_output_aliases={i: i for i in range(2 * n)},
        compiler_params=pltpu.CompilerParams(has_side_effects=pltpu.SideEffectType.DATAFLOW_SIDE_EFFECTING),
    )(*bufs, send_sems, recv_sems, after)
    return list(res[n:])


def _pick_rows(R, mult, cap):
    best = None
    for n in range(1, R + 1):
        if R % n == 0 and (R // n) % mult == 0 and R // n <= cap:
            best = R // n
            break
    assert best is not None, (R, mult, cap)
    return best


def sum_slots(name, x):
    _, R, _ = x.shape
    tr = _pick_rows(R, 16, 2304)

    def body(x_ref, o_ref):
        acc = x_ref[0].astype(F32)
        for d in range(1, NDEV):
            acc = acc + x_ref[d].astype(F32)
        o_ref[...] = acc

    return pl.pallas_call(
        body, name=name, grid=(R // tr,),
        in_specs=[pl.BlockSpec((NDEV, tr, LANE), lambda i: (0, i, 0))],
        out_specs=pl.BlockSpec((tr, LANE), lambda i: (i, 0)),
        out_shape=jax.ShapeDtypeStruct((R, LANE), F32), compiler_params=_cparams(),
    )(x)


def adamw(name, w, g, m, v):
    L, R, C = w.shape
    tr = _pick_rows(R, 8, 256) if R % 8 == 0 else R

    def body(w_ref, g_ref, m_ref, v_ref, d_ref, nm_ref, nv_ref):
        gg = g_ref[...]
        nm = B1 * m_ref[...] + (1.0 - B1) * gg
        nv = B2 * v_ref[...] + (1.0 - B2) * jnp.square(gg)
        m_hat = nm / (1.0 - B1 ** STEP)
        v_hat = nv / (1.0 - B2 ** STEP)
        d_ref[...] = -LR * (m_hat / (jnp.sqrt(v_hat) + EPS) + WD * w_ref[...])
        nm_ref[...] = nm
        nv_ref[...] = nv

    blk = pl.BlockSpec((1, tr, C), lambda l, i: (l, i, 0))
    shp = jax.ShapeDtypeStruct(w.shape, F32)
    return pl.pallas_call(
        body, name=name, grid=(L, R // tr), in_specs=[blk] * 4, out_specs=[blk] * 3, out_shape=[shp] * 3,
        compiler_params=_cparams(),
    )(w, g, m, v)


IN_SHARD = D_IN // NDEV
UQ_SHARD = HEADS * (NOPE + ROPE) // NDEV
W_IN_PAD = 1024
ROW_A, ROW_B, ROW_C, ROW_UKV, ROW_UQ, MISC_ROWS = 0, 512, 1024, 1536, 1792, 2176


def _in_perm_index():
    ar = np.arange
    z = lambda n: np.full((n,), -1, np.int64)
    mix = lambda lo1, lo2: np.concatenate([ar(lo + LANE * j, lo + LANE * (j + 1)) for j in range(CW // LANE)
                                           for lo in (lo1, lo2)])
    return np.concatenate([ar(4768, 7840), mix(0, 512), ar(1024, 1536), mix(1536, 2560), ar(4256, 4768), ar(2048, 2560),
                           ar(3072, 3584), ar(3968, 4224), ar(4224, 4256), z(OFF_Q - OFF_KR - ROPE), ar(3584, 3968),
                           z(NP - OFF_Q - QL)])


def _head_perm_index(a, b):
    parts = []
    for g in range(QUADS):
        h = np.arange(4 * g, 4 * g + 4)[:, None] * (a + b)
        parts += [(h + np.arange(a)[None]).reshape(-1), (h + a + np.arange(b)[None]).reshape(-1)]
    return np.concatenate(parts)


def _inverse(perm, n):
    inv = np.full((n,), -1, np.int64)
    inv[perm[perm >= 0]] = np.nonzero(perm >= 0)[0]
    return inv


IN_PERM = _in_perm_index()
UQ_PERM = _head_perm_index(NOPE, ROPE)
UKV_PERM = _head_perm_index(NOPE, VH)


def _to_gathered(perm, shard, pad):
    return np.where(perm >= 0, (perm // shard) * pad + perm % shard, -1)


def _from_full(inv, shard, pad):
    j, i = np.divmod(np.arange(NDEV * pad), pad)
    return np.where(i < shard, inv[np.minimum(j * shard + i, inv.shape[0] - 1)], -1)


def col_gather(name, srcs, out_shapes, jobs, deps=()):
    ns, nj, nd = len(srcs), len(jobs), len(deps)
    tables = [jnp.asarray(np.asarray(job[5], np.int32)[None, :]) for job in jobs]

    def view(ref, col0, width, r0, rc):
        n = ref.shape[-1]
        if len(ref.shape) == 3:
            return ref.at[col0 // n, pl.ds(r0, rc), pl.ds(col0 % n, width)]
        return ref.at[pl.ds(r0, rc), pl.ds(col0, width)]

    def body(*refs):
        src_refs, tab_refs, out_refs = refs[:ns], refs[ns:ns + nj], refs[ns + nj + nd:]
        for ji, (si, srow, oi, orow, nrows, tgt) in enumerate(jobs):
            sref, oref = src_refs[si], out_refs[oi]
            tgt = np.asarray(tgt)
            tw = 256 if oref.shape[-1] % 256 == 0 else LANE
            rc = 256 if nrows % 256 == 0 else LANE
            for t in range(tgt.shape[0] // tw):
                tt = tgt[t * tw:(t + 1) * tw]
                tiles = sorted(set((tt[tt >= 0] // LANE).tolist()))
                straight = bool(tiles) and np.array_equal(tt, np.arange(tiles[0] * LANE, tiles[0] * LANE + tw))
                onehots = []
                if tiles and not straight:
                    want = tab_refs[ji][:, t * tw:(t + 1) * tw]
                    row = lax.broadcasted_iota(jnp.int32, (LANE, tw), 0)
                    onehots = [jnp.where(want == row + s * LANE, 1.0, 0.0).astype(BF) for s in tiles]

                def chunk(ci, _, t=t, tiles=tiles, straight=straight, onehots=onehots):
                    r0 = ci * rc
                    dst = view(oref, t * tw, tw, pl.multiple_of(orow + r0, LANE), rc)
                    rs = pl.multiple_of(srow + r0, LANE)
                    if not tiles:
                        dst[...] = jnp.zeros((rc, tw), BF)
                    elif straight:
                        for k in range(tw // LANE):
                            view(oref, t * tw + k * LANE, LANE, pl.multiple_of(orow + r0, LANE), rc)[...] = (
                                view(sref, (tiles[0] + k) * LANE, LANE, rs, rc)[...])
                    else:
                        acc = None
                        for s, oh in zip(tiles, onehots):
                            p = jnp.dot(view(sref, s * LANE, LANE, rs, rc)[...], oh, preferred_element_type=F32)
                            acc = p if acc is None else acc + p
                        dst[...] = acc.astype(BF)
                    return 0

                lax.fori_loop(0, nrows // rc, chunk, 0)

    vmem = pl.BlockSpec(memory_space=pltpu.VMEM)
    return pl.pallas_call(
        body, name=name, in_specs=[vmem] * (ns + nj) + [ANY_SPEC] * nd, out_specs=[vmem] * len(out_shapes),
        out_shape=[jax.ShapeDtypeStruct(s, BF) for s in out_shapes], compiler_params=_cparams(),
    )(*srcs, *tables, *deps)


def sum_adamw(name, recvs, w, m, v):
    L, R, C = w.shape
    CP = recvs[0].shape[-1]
    tr = _pick_rows(R, 16, 128)

    def body(*refs):
        r_refs = refs[:L]
        w_ref, m_ref, v_ref, g_ref, d_ref, nm_ref, nv_ref, gsum = refs[L:]
        layer = pl.program_id(0)
        for k in range(L):
            def total(k=k):
                acc = r_refs[k][0].astype(F32)
                for d in range(1, NDEV):
                    acc = acc + r_refs[k][d].astype(F32)
                gsum[...] = acc
            pl.when(layer == k)(total)
        gg = gsum[:, 0:C]
        nm = B1 * m_ref[...] + (1.0 - B1) * gg
        nv = B2 * v_ref[...] + (1.0 - B2) * jnp.square(gg)
        m_hat = nm / (1.0 - B1 ** STEP)
        v_hat = nv / (1.0 - B2 ** STEP)
        g_ref[...] = gg
        d_ref[...] = -LR * (m_hat / (jnp.sqrt(v_hat) + EPS) + WD * w_ref[...])
        nm_ref[...] = nm
        nv_ref[...] = nv

    r_specs = [pl.BlockSpec((NDEV, tr, CP), functools.partial(lambda l, i, k: (0, jnp.where(l == k, i, 0), 0), k=k))
               for k in range(L)]
    blk = pl.BlockSpec((None, tr, C), lambda l, i: (l, i, 0))
    shp = jax.ShapeDtypeStruct(w.shape, F32)
    return pl.pallas_call(
        body, name=name, grid=(L, R // tr), in_specs=r_specs + [blk] * 3, out_specs=[blk] * 4, out_shape=[shp] * 4,
        scratch_shapes=[pltpu.VMEM((tr, CP), F32)], compiler_params=_cparams(),
    )(*recvs, w, m, v)


ALPHA = 8.0 ** 0.25


def _rope_fn(sign):
    def fn(x, cos, sin):
        W = x.shape[-1]
        lane = lax.broadcasted_iota(jnp.int32, x.shape, 1)
        first_half = (lane % ROPE) < (ROPE // 2)
        rot = jnp.where(first_half, -pltpu.roll(x, W - ROPE // 2, 1), pltpu.roll(x, ROPE // 2, 1))
        return x * cos + sign * rot * sin
    return fn


def layer_fwd(x, ada3, W, tabs, S):
    cos, sin = tabs
    T = 256
    u = rowwise("modulate", lambda xv, a: xv * (1.0 + a[1:2, :]) + a[0:1, :], S, T,
                [(x, D_MODEL, 0)], [ada3], [(D_MODEL, BF)])[0]
    proj = mm(u, W["in"], name="mm_proj", tm=1024, tn=1024)

    ca = conv_fwd("conv_a_fwd", proj, OFF_A, W["conv_a"], 31, "glu", S, CW)

    def a_post(c, ag, vec):
        n, _ = _ln_stats(c + vec[0:1, :])
        return _silu(n * vec[1:2, :] + vec[2:3, :]) * _silu(ag)

    h_a = rowwise("mix_a_post", a_post, S, T, [(ca, CW, 0), (proj, CW, OFF_AG)], [W["vec_a"]], [(CW, BF)])[0]
    y_a = mm(h_a, W["a_out"], name="mm_branch_out")

    cb = conv_fwd("conv_b_fwd", proj, OFF_B, W["conv_b"], 3, "mul", S, CW)
    h_b = rowwise("mix_b_post", lambda c, gb, bg: gb * c * _silu(bg), S, T,
                  [(cb, CW, 0), (proj, CW, OFF_GB), (proj, CW, OFF_BG)], [], [(CW, BF)])[0]
    y_b = mm(h_b, W["b_out"], name="mm_branch_out")

    def rms2(ql, kvl, gq, gkv):
        rq = lax.rsqrt(jnp.mean(ql * ql, axis=-1, keepdims=True) + RMS_EPS)
        rk = lax.rsqrt(jnp.mean(kvl * kvl, axis=-1, keepdims=True) + RMS_EPS)
        return ql * rq * gq, kvl * rk * gkv

    qn, kvn = rowwise("rms_fwd", rms2, S, T, [(proj, QL, OFF_Q), (proj, KVL, OFF_KV)], [W["gq"], W["gkv"]],
                      [(QL, BF), (KVL, BF)])
    q = mm(qn, W["uq"], name="mm_q")
    kv = mm(kvn, W["ukv"], name="mm_kv", out_dtype=BF)
    rope = _rope_fn(1.0)

    def rope_fwd(qv, kr, c1, s1):
        parts = []
        for g in range(QUADS):
            parts.append(qv[:, g * QW:g * QW + 2 * LANE].astype(BF))
            parts.append(rope(qv[:, g * QW + 2 * LANE:(g + 1) * QW], c1, s1).astype(BF))
        kp = rope(kr, c1, s1)
        kp = kp + pltpu.roll(kp, ROPE, 1) + pltpu.roll(kp, 2 * ROPE, 1) + pltpu.roll(kp, 3 * ROPE, 1)
        return jnp.concatenate(parts, axis=1), kp

    q_b, kpe = rowwise("rope_fwd", rope_fwd, S, T,
                       [(q, HEADS * (NOPE + ROPE), 0), (proj, LANE, OFF_KR), (cos, LANE, 0), (sin, LANE, 0)], [],
                       [(HEADS * (NOPE + ROPE), BF), (LANE, BF)])
    o, lse = attn_fwd(q_b, kv, kpe, S)
    h_c = rowwise("mix_c_post", lambda ov, cg: ov * _silu(cg), S, T, [(o, CW, 0), (proj, CW, OFF_CG)], [],
                  [(CW, BF)])[0]
    y_c = mm(h_c, W["c_out"], name="mm_branch_out")

    def merge(la, lb, lc, ya, yb, yc):
        return _sigmoid(la) * ya + _sigmoid(lb) * yb + _sigmoid(lc) * yc

    m = rowwise("merge_fwd", merge, S, 128,
                [(proj, D_MODEL, 0), (proj, D_MODEL, 1024), (proj, D_MODEL, 2048), (y_a, D_MODEL, 0),
                 (y_b, D_MODEL, 0), (y_c, D_MODEL, 0)], [], [(D_MODEL, BF)])[0]
    out = mm(m, W["o"], name="mm_out")

    def ln_fwd(xv, ov, a, lnv):
        n, _ = _ln_stats(ALPHA * xv + a[2:3, :] * ov)
        return n * lnv[0:1, :] + lnv[1:2, :]

    x_next = rowwise("ln_fwd", ln_fwd, S, 128, [(x, D_MODEL, 0), (out, D_MODEL, 0)], [ada3, W["lnv"]],
                     [(D_MODEL, F32)])[0]
    saved = dict(x=x, u=u, proj=proj, ca=ca, cb=cb, h_a=h_a, h_b=h_b, h_c=h_c, y_a=y_a, y_b=y_b, y_c=y_c, qn=qn,
                 kvn=kvn, q_b=q_b, kv=kv, kpe=kpe, lse=lse, o=o, m=m, out=out)
    return x_next, saved


def layer_bwd(dxn, sv, ada3, W, tabs, S):
    cos, sin = tabs
    T = 256
    x, proj = sv["x"], sv["proj"]
    G = {}

    def ln_bwd(xv, ov, dy, a, lnv):
        gate = a[2:3, :]
        n, rstd = _ln_stats(ALPHA * xv + gate * ov)
        dr = _ln_bwd(dy * lnv[0:1, :], n, rstd)
        return ALPHA * dr, gate * dr, _colsum(dy * n), _colsum(dy), _colsum(dr * ov)

    dres, d_out, G["ln_g"], G["ln_b"], d_gate = rowwise(
        "ln_bwd", ln_bwd, S, 128, [(x, D_MODEL, 0), (sv["out"], D_MODEL, 0), (dxn, D_MODEL, 0)], [ada3, W["lnv"]],
        [(D_MODEL, F32), (D_MODEL, BF)], [D_MODEL] * 3)
    dm = mm(d_out, W["o"], name="mm_dm", trans_b=True)
    G["w_o"] = mm(sv["m"], d_out, name="mm_gw_o", trans_a=True, out_dtype=BF)

    def merge_bwd(dmv, la, lb, lc, ya, yb, yc):
        outs, dls = [], []
        for lg, yv in ((la, ya), (lb, yb), (lc, yc)):
            s = _sigmoid(lg)
            outs.append(dmv * s)
            dls.append((dmv * yv * s * (1.0 - s)).astype(BF))
        return (jnp.concatenate(dls, axis=1),) + tuple(outs)

    d_proj, dy_a, dy_b, dy_c = rowwise(
        "merge_bwd", merge_bwd, S, 128,
        [(dm, D_MODEL, 0), (proj, D_MODEL, 0), (proj, D_MODEL, 1024), (proj, D_MODEL, 2048), (sv["y_a"], D_MODEL, 0),
         (sv["y_b"], D_MODEL, 0), (sv["y_c"], D_MODEL, 0)], [], [(3 * D_MODEL, BF)] + [(D_MODEL, BF)] * 3,
        into=(None, NP, OFF_M))

    dh = {}
    for br, dy in (("a", dy_a), ("b", dy_b), ("c", dy_c)):
        dh[br] = mm(dy, W[br + "_out"], name="mm_dh", trans_b=True)
        G["w_%s_out" % br] = mm(sv["h_" + br], dy, name="mm_gw_branch", trans_a=True, out_dtype=BF)

    def a_post_bwd(c, ag, dhv, vec):
        n, rstd = _ln_stats(c + vec[0:1, :])
        z = n * vec[1:2, :] + vec[2:3, :]
        d_ag = dhv * _silu(z) * _dsilu(ag)
        dz = dhv * _silu(ag) * _dsilu(z)
        dc = _ln_bwd(dz * vec[1:2, :], n, rstd)
        return d_ag, dc, _colsum(dc), _colsum(dz * n), _colsum(dz)

    d_proj, dca, G["conv_a_b"], G["ln_a_g"], G["ln_a_b"] = rowwise(
        "mix_a_post_bwd", a_post_bwd, S, T, [(sv["ca"], CW, 0), (proj, CW, OFF_AG), (dh["a"], CW, 0)], [W["vec_a"]],
        [(CW, BF), (CW, F32)], [CW] * 3, into=(d_proj, NP, OFF_AG))
    d_proj, G["conv_a_w"] = conv_bwd("conv_a_bwd", proj, OFF_A, dca, W["conv_a"], 31, "glu", S, CW, d_proj)

    def b_post_bwd(c, gb, bg, dhv):
        sg = _silu(bg)
        d_gb_bg = jnp.concatenate([(dhv * sg * c).astype(BF), (dhv * gb * c * _dsilu(bg)).astype(BF)], axis=1)
        return d_gb_bg, dhv * sg * gb

    d_proj, dcb = rowwise("mix_b_post_bwd", b_post_bwd, S, T,
                          [(sv["cb"], CW, 0), (proj, CW, OFF_GB), (proj, CW, OFF_BG), (dh["b"], CW, 0)], [],
                          [(2 * CW, BF), (CW, F32)], into=(d_proj, NP, OFF_GB))
    d_proj, G["conv_b_w"] = conv_bwd("conv_b_bwd", proj, OFF_B, dcb, W["conv_b"], 3, "mul", S, CW, d_proj)

    d_proj, d_o = rowwise("mix_c_post_bwd", lambda ov, cg, dhv: (dhv * ov * _dsilu(cg), dhv * _silu(cg)), S, T,
                          [(sv["o"], CW, 0), (proj, CW, OFF_CG), (dh["c"], CW, 0)], [], [(CW, BF), (CW, F32)],
                          into=(d_proj, NP, OFF_CG))
    dq, d_kv, dkp_heads = attn_bwd(sv["q_b"], sv["kv"], sv["kpe"], sv["o"], sv["lse"], d_o, S)
    ropeT = _rope_fn(-1.0)

    def rope_bwd(dqv, dkp, c1, s1):
        parts = []
        for g in range(QUADS):
            parts.append(dqv[:, g * QW:g * QW + 2 * LANE].astype(BF))
            parts.append(ropeT(dqv[:, g * QW + 2 * LANE:(g + 1) * QW], c1, s1).astype(BF))
        f = dkp[:, :LANE] + dkp[:, LANE:]
        f = f + pltpu.roll(f, 64, 1)
        f = f + pltpu.roll(f, 32, 1)
        lane = lax.broadcasted_iota(jnp.int32, f.shape, 1)
        return jnp.concatenate(parts, axis=1), jnp.where(lane < ROPE, ropeT(f, c1, s1), 0.0)

    d_q, dk_pe = rowwise("rope_bwd", rope_bwd, S, T,
                         [(dq, HEADS * (NOPE + ROPE), 0), (dkp_heads, HEADS * ROPE, 0), (cos, LANE, 0), (sin, LANE, 0)],
                         [], [(HEADS * (NOPE + ROPE), BF), (LANE, BF)])
    d_qn = mm(d_q, W["uq"], name="mm_dqn", trans_b=True)
    d_kvn = mm(d_kv, W["ukv"], name="mm_dkvn", trans_b=True)
    G["w_uq"] = mm(sv["qn"], d_q, name="mm_gw_uq", trans_a=True, out_dtype=BF)
    G["w_ukv"] = mm(sv["kvn"], d_kv, name="mm_gw_ukv", trans_a=True, out_dtype=BF)

    def rms_bwd(ql, kvl, dqn, dkn, dkp, gq, gkv):
        res = []
        for xv, dy, g in ((ql, dqn, gq), (kvl, dkn, gkv)):
            r = lax.rsqrt(jnp.mean(xv * xv, axis=-1, keepdims=True) + RMS_EPS)
            dxh = dy * g
            res.append(((r * (dxh - xv * (r * r) * jnp.mean(dxh * xv, axis=-1, keepdims=True))).astype(BF),
                        _colsum(dy * xv * r)))
        pad = jnp.zeros((ql.shape[0], LANE), BF)
        return jnp.concatenate([res[1][0], dkp, pad, res[0][0], pad], axis=1), res[0][1], res[1][1]

    d_proj, G["q_norm_g"], G["kv_norm_g"] = rowwise(
        "rms_bwd", rms_bwd, S, T,
        [(proj, QL, OFF_Q), (proj, KVL, OFF_KV), (d_qn, QL, 0), (d_kvn, KVL, 0), (dk_pe, LANE, 0)],
        [W["gq"], W["gkv"]], [(NP - OFF_KV, BF)], [QL, KVL], into=(d_proj, NP, OFF_KV))
    du = mm(d_proj, W["in"], name="mm_du", trans_b=True, tk=1024)
    G["w_in"] = mm(sv["u"], d_proj, name="mm_gw_in", trans_a=True, out_dtype=BF)

    def mod_bwd(duv, xv, dr, a):
        return duv * (1.0 + a[1:2, :]) + dr, _colsum(duv), _colsum(duv * xv)

    dx, d_shift, d_scale = rowwise("mod_bwd", mod_bwd, S, 128, [(du, D_MODEL, 0), (x, D_MODEL, 0), (dres, D_MODEL, 0)],
                                   [ada3], [(D_MODEL, F32)], [D_MODEL] * 2)
    d_ada = jnp.concatenate([d_shift, d_scale, d_gate], axis=1)
    return dx, G, d_ada


SMALL = ("conv_a_b", "ln_a_g", "ln_a_b", "q_norm_g", "kv_norm_g", "ln_g", "ln_b")


def _rows(v):
    n = v.shape[0]
    r = -(-n // (LANE * 16)) * 16
    return jnp.pad(v, (0, r * LANE - n)).reshape(r, LANE)


def kernel(x, c, positions, w_ada, b_ada, w_in, conv_a_w, conv_a_b, ln_a_g, ln_a_b, w_a_out, conv_b_w, w_b_out, q_norm_g, kv_norm_g, w_uq, w_ukv, w_c_out, w_o, ln_g, ln_b, loss_target, m_w_ada, m_b_ada, m_w_in, m_conv_a_w, m_conv_a_b, m_ln_a_g, m_ln_a_b, m_w_a_out, m_conv_b_w, m_w_b_out, m_q_norm_g, m_kv_norm_g, m_w_uq, m_w_ukv, m_w_c_out, m_w_o, m_ln_g, m_ln_b, v_w_ada, v_b_ada, v_w_in, v_conv_a_w, v_conv_a_b, v_ln_a_g, v_ln_a_b, v_w_a_out, v_conv_b_w, v_w_b_out, v_q_norm_g, v_kv_norm_g, v_w_uq, v_w_ukv, v_w_c_out, v_w_o, v_ln_g, v_ln_b):
    P = dict(w_ada=w_ada, b_ada=b_ada, w_in=w_in, conv_a_w=conv_a_w, conv_a_b=conv_a_b, ln_a_g=ln_a_g, ln_a_b=ln_a_b,
             w_a_out=w_a_out, conv_b_w=conv_b_w, w_b_out=w_b_out, q_norm_g=q_norm_g, kv_norm_g=kv_norm_g, w_uq=w_uq,
             w_ukv=w_ukv, w_c_out=w_c_out, w_o=w_o, ln_g=ln_g, ln_b=ln_b)
    Mo = dict(w_ada=m_w_ada, b_ada=m_b_ada, w_in=m_w_in, conv_a_w=m_conv_a_w, conv_a_b=m_conv_a_b, ln_a_g=m_ln_a_g,
              ln_a_b=m_ln_a_b, w_a_out=m_w_a_out, conv_b_w=m_conv_b_w, w_b_out=m_w_b_out, q_norm_g=m_q_norm_g,
              kv_norm_g=m_kv_norm_g, w_uq=m_w_uq, w_ukv=m_w_ukv, w_c_out=m_w_c_out, w_o=m_w_o, ln_g=m_ln_g, ln_b=m_ln_b)
    Vo = dict(w_ada=v_w_ada, b_ada=v_b_ada, w_in=v_w_in, conv_a_w=v_conv_a_w, conv_a_b=v_conv_a_b, ln_a_g=v_ln_a_g,
              ln_a_b=v_ln_a_b, w_a_out=v_w_a_out, conv_b_w=v_conv_b_w, w_b_out=v_w_b_out, q_norm_g=v_q_norm_g,
              kv_norm_g=v_kv_norm_g, w_uq=v_w_uq, w_ukv=v_w_ukv, w_c_out=v_w_c_out, w_o=v_w_o, ln_g=v_ln_g, ln_b=v_ln_b)
    ORDER = ("w_ada", "b_ada", "w_in", "conv_a_w", "conv_a_b", "ln_a_g", "ln_a_b", "w_a_out", "conv_b_w", "w_b_out",
             "q_norm_g", "kv_norm_g", "w_uq", "w_ukv", "w_c_out", "w_o", "ln_g", "ln_b")
    L = w_ada.shape[0]
    S = x.shape[1]
    me = 4 * lax.axis_index("x") + 2 * lax.axis_index("y") + lax.axis_index("c")
    x2 = x[0]
    tgt = loss_target[0]

    small_in = _rows(jnp.concatenate([c.reshape(-1), conv_a_w.reshape(-1), conv_b_w.reshape(-1)]))
    w_in_b = jnp.pad(w_in.astype(BF), ((0, 0), (0, 0), (0, W_IN_PAD - IN_SHARD)))
    misc_b = jnp.concatenate([w_a_out, w_b_out, w_c_out, w_ukv, jnp.pad(w_uq, ((0, 0), (0, 0), (0, LANE - UQ_SHARD)))],
                             axis=1).astype(BF)
    w_o_b = w_o.astype(BF)
    gathered = [None] * L
    pending, _ = exchange_begin("gather0", [w_in_b[0], misc_b[0], w_o_b[0]], 3, small_in)
    sg = exchange("gather_small", [small_in], [])[0]
    sgf = sg.reshape(NDEV, -1)
    c_all = sgf[:, :D_MODEL]
    o1 = D_MODEL + L * 31 * 64
    conv_a_full = sgf[:, D_MODEL:o1].reshape(NDEV, L, 31, 64).transpose(1, 2, 0, 3).reshape(L, 31, CW)
    conv_b_full = sgf[:, o1:o1 + L * 3 * 64].reshape(NDEV, L, 3, 64).transpose(1, 2, 0, 3).reshape(L, 3, CW)

    c_act = rowwise("silu_c", _silu, 16, 16, [(jnp.pad(c_all, ((0, 8), (0, 0))), D_MODEL, 0)], [], [(D_MODEL, BF)])[0]
    ncol = w_ada.shape[2]
    w_ada_b = w_ada.astype(BF).transpose(1, 0, 2).reshape(D_MODEL, L * ncol)
    b_mine = lax.dynamic_slice_in_dim(b_ada, me * ncol, ncol, axis=1).reshape(1, L * ncol)
    ada_part = mm(c_act, w_ada_b, name="mm_ada", bias=b_mine)
    ada_rows = -(-(L * ncol) // (LANE * 8)) * 8
    ada_send = jnp.pad(ada_part[:NDEV].reshape(NDEV, -1, LANE), ((0, 0), (0, ada_rows - L * ncol // LANE), (0, 0)))
    ada_recv = exchange("a2a_ada", [], [ada_send])[0]
    ada = ada_recv[:, :L * ncol // LANE].reshape(NDEV, L, ncol).transpose(1, 0, 2).reshape(L, 3, D_MODEL)

    inv_freq = ROPE_THETA ** (-jnp.arange(0, ROPE, 2, dtype=F32) / ROPE)
    ang = positions[0].astype(F32)[:, None] * inv_freq
    tabs = (jnp.tile(jnp.cos(ang), (1, 2 * LANE // ROPE)), jnp.tile(jnp.sin(ang), (1, 2 * LANE // ROPE)))

    straight = np.arange(D_MODEL)
    fwd_in = [(0, 0, 0, 0, D_MODEL, _to_gathered(IN_PERM, IN_SHARD, W_IN_PAD))]
    fwd_misc = [(0, ROW_A, 0, 0, CW, straight), (0, ROW_B, 1, 0, CW, straight), (0, ROW_C, 2, 0, CW, straight),
                (0, ROW_UKV, 3, 0, KVL, UKV_PERM), (0, ROW_UQ, 4, 0, QL, _to_gathered(UQ_PERM, UQ_SHARD, LANE))]
    rev_in = [(0, 0, 0, 0, D_MODEL, _from_full(_inverse(IN_PERM, D_IN), IN_SHARD, W_IN_PAD))]
    rev_misc = [(0, 0, 0, ROW_A, CW, straight), (1, 0, 0, ROW_B, CW, straight), (2, 0, 0, ROW_C, CW, straight),
                (3, 0, 0, ROW_UKV, KVL, _from_full(_inverse(UKV_PERM, HEADS * (NOPE + VH)), LANE, LANE)),
                (4, 0, 0, ROW_UQ, QL, _from_full(_inverse(UQ_PERM, HEADS * (NOPE + ROPE)), UQ_SHARD, LANE))]

    def layer_weights(l, deps):
        g_in, g_misc, g_o = gathered[l]
        w_in_p = col_gather("relayout_w_in", [g_in], [(D_MODEL, NP)], fwd_in, deps)[0]
        a_out, b_out, c_out, ukv, uq = col_gather(
            "relayout_misc", [g_misc], [(CW, D_MODEL)] * 3 + [(KVL, HEADS * (NOPE + VH)), (QL, HEADS * (NOPE + ROPE))],
            fwd_misc, deps)
        return {
            "in": w_in_p, "a_out": a_out, "b_out": b_out, "c_out": c_out, "uq": uq, "ukv": ukv,
            "o": g_o.reshape(D_MODEL, D_MODEL),
            "conv_a": jnp.pad(conv_a_full[l], ((0, 1), (0, 0))), "conv_b": jnp.pad(conv_b_full[l], ((0, 5), (0, 0))),
            "vec_a": jnp.stack([conv_a_b[l], ln_a_g[l], ln_a_b[l]]), "gq": q_norm_g[l][None], "gkv": kv_norm_g[l][None],
            "lnv": jnp.stack([ln_g[l], ln_b[l]]),
        }

    h = x2
    saved, weights = [], []
    gathered[0] = exchange_end(pending, ada)
    for l in range(L):
        ada_l, deps = ada[l], ()
        if l + 1 < L:
            pending, token = exchange_begin("gather%d" % (l + 1), [w_in_b[l + 1], misc_b[l + 1], w_o_b[l + 1]], 3,
                                            gathered[l][0])
            ada_l, deps = ada_l + token[0, 0], (token,)
        Wl = layer_weights(l, deps)
        h, sv = layer_fwd(h, ada_l, Wl, tabs, S)
        if l + 1 < L:
            gathered[l + 1] = exchange_end(pending, h)
        saved.append(sv)
        weights.append(Wl)

    def loss_fn(y, t):
        e = y - t
        return e * (1.0 / D_MODEL), _colsum(e * e)

    dy, sq = rowwise("loss", loss_fn, S, 256, [(h, D_MODEL, 0), (tgt, D_MODEL, 0)], [], [(D_MODEL, F32)], [D_MODEL])
    loss = lax.psum(0.5 * jnp.sum(sq) / D_MODEL, ("x", "y", "c"))

    grads, d_adas, recv = [None] * L, [None] * L, [None] * L
    pending, token = None, None
    for l in reversed(range(L)):
        ada_l = ada[l] if token is None else ada[l] + token[0, 0]
        dy, g, d_adas[l] = layer_bwd(dy, saved[l], ada_l, weights[l], tabs, S)
        grads[l] = g
        if pending is not None:
            recv[l + 1] = exchange_end(pending, dy)
        send_in = col_gather("unrelayout_w_in", [g["w_in"]], [(NDEV, D_MODEL, W_IN_PAD)], rev_in)[0]
        send_misc = col_gather("unrelayout_misc", [g["w_a_out"], g["w_b_out"], g["w_c_out"], g["w_ukv"], g["w_uq"]],
                               [(NDEV, MISC_ROWS, LANE)], rev_misc)[0]
        send_o = g["w_o"].reshape(NDEV, D_MODEL // NDEV, D_MODEL)
        pending, token = exchange_begin("scatter%d" % l, [send_in, send_misc, send_o], 0,
                                        dy if l + 1 == L else recv[l + 1][0])
    grad_x = dy[None]

    small_parts = [jnp.stack([grads[l][n].reshape(-1) for l in range(L)]).reshape(-1) for n in SMALL]
    small_parts.append(jnp.stack([grads[l]["conv_a_w"][:31].reshape(-1) for l in range(L)]).reshape(-1))
    small_parts.append(jnp.stack([grads[l]["conv_b_w"][:3].reshape(-1) for l in range(L)]).reshape(-1))
    small_parts.append(jnp.stack([d_adas[l].reshape(-1) for l in range(L)]).reshape(-1))
    small_sizes = [int(p.shape[0]) for p in small_parts]
    gsmall = exchange("gather_small_grads", [_rows(jnp.concatenate(small_parts))], [])[0]
    gsum = sum_slots("sum_small", gsmall).reshape(-1)
    recv[0] = exchange_end(pending, gsum)
    Gr = {}
    offs = np.cumsum([0] + small_sizes)
    for i, n in enumerate(SMALL):
        Gr[n] = gsum[offs[i]:offs[i + 1]].reshape(L, -1)
    ca = gsum[offs[7]:offs[8]].reshape(L, 31, CW)
    cbw = gsum[offs[8]:offs[9]].reshape(L, 3, CW)
    Gr["conv_a_w"] = lax.dynamic_slice_in_dim(ca, me * 64, 64, axis=2)
    Gr["conv_b_w"] = lax.dynamic_slice_in_dim(cbw, me * 64, 64, axis=2)
    Gr["b_ada"] = gsum[offs[9]:offs[10]].reshape(L, 3 * D_MODEL)
    d_ada_all = gsmall.reshape(NDEV, -1)[:, offs[9]:offs[10]].reshape(NDEV, L, 3 * D_MODEL)
    d_mine = lax.dynamic_slice_in_dim(d_ada_all, me * ncol, ncol, axis=2).reshape(NDEV, L * ncol)
    g_ada = mm(c_act, jnp.pad(d_mine, ((0, 8), (0, 0))).astype(BF), name="mm_gw_ada", trans_a=True)
    Gr["w_ada"] = g_ada.reshape(D_MODEL, L, ncol).transpose(1, 0, 2)

    D, NM, NV = {}, {}, {}
    D["w_ada"], NM["w_ada"], NV["w_ada"] = adamw("adamw_w_ada", P["w_ada"], Gr["w_ada"], Mo["w_ada"], Vo["w_ada"])
    Gr["w_in"], D["w_in"], NM["w_in"], NV["w_in"] = sum_adamw(
        "sum_adamw_w_in", [recv[l][0] for l in range(L)], P["w_in"], Mo["w_in"], Vo["w_in"])
    Gr["w_o"], D["w_o"], NM["w_o"], NV["w_o"] = sum_adamw(
        "sum_adamw_w_o", [recv[l][2] for l in range(L)], P["w_o"], Mo["w_o"], Vo["w_o"])
    misc = lambda T_: jnp.concatenate([T_["w_a_out"], T_["w_b_out"], T_["w_c_out"], T_["w_ukv"],
                                       jnp.pad(T_["w_uq"], ((0, 0), (0, 0), (0, LANE - UQ_SHARD)))], axis=1)
    res = sum_adamw("sum_adamw_misc", [recv[l][1] for l in range(L)], misc(P), misc(Mo), misc(Vo))
    for T_, r in zip((Gr, D, NM, NV), res):
        T_["w_a_out"], T_["w_b_out"], T_["w_c_out"] = r[:, ROW_A:ROW_B], r[:, ROW_B:ROW_C], r[:, ROW_C:ROW_UKV]
        T_["w_ukv"], T_["w_uq"] = r[:, ROW_UKV:ROW_UQ], r[:, ROW_UQ:MISC_ROWS, :UQ_SHARD]
    packed =("b_ada", "conv_a_w", "conv_b_w") + SMALL
    pk = lambda T_: _rows(jnp.concatenate([T_[n].reshape(-1) for n in packed]))[None]
    dS, mS, vS = adamw("adamw_small", pk(P), pk(Gr), pk(Mo), pk(Vo))
    o = 0
    for n in packed:
        sz = int(np.prod(P[n].shape))
        D[n] = dS.reshape(-1)[o:o + sz].reshape(P[n].shape)
        NM[n] = mS.reshape(-1)[o:o + sz].reshape(P[n].shape)
        NV[n] = vS.reshape(-1)[o:o + sz].reshape(P[n].shape)
        o += sz
    return (loss, grad_x, *[Gr[n] for n in ORDER], *[D[n] for n in ORDER], *[NM[n] for n in ORDER],
            *[NV[n] for n in ORDER])
```

```python
import functools
import math

import numpy as np
import jax
import jax.numpy as jnp
from jax import lax
from jax.experimental import pallas as pl
from jax.experimental.pallas import tpu as pltpu

BF = jnp.bfloat16
F32 = jnp.float32
MESH = pl.DeviceIdType.MESH
NDEV = 8

HEADS, NOPE, ROPE, VH = 8, 64, 32, 64
HP = 128
ROPE_THETA = 10000.0
LN_EPS = 1e-5
RMS_EPS = 1e-6
LR, B1, B2, EPS, WD, STEP = 0.001, 0.9, 0.999, 1e-08, 0.01, 10

LANE = 128
VMEM_LIMIT = 56 * 1024 * 1024

D_MODEL, CW, QL, KVL = 1024, 512, 384, 256
OFF_M, OFF_A, OFF_AG, OFF_B, OFF_CG, OFF_GB, OFF_BG = 0, 3072, 4096, 4608, 5632, 6144, 6656
OFF_KV, OFF_KR, OFF_Q, NP = 7168, 7424, 7680, 8192
D_IN = 7840


def _cparams(**kw):
    return pltpu.CompilerParams(vmem_limit_bytes=VMEM_LIMIT, **kw)


def _sigmoid(x):
    return jax.nn.sigmoid(x)


def _silu(x):
    return x * _sigmoid(x)


def _dsilu(x):
    s = _sigmoid(x)
    return s * (1.0 + x * (1.0 - s))


def _pick_tile(n, cap, mult):
    if n <= cap:
        return n
    for t in range(cap - cap % mult, 0, -mult):
        if n % t == 0:
            return t
    raise ValueError((n, cap, mult))


def mm(a, b, *, name, trans_a=False, trans_b=False, out_dtype=F32, bias=None, tm=1024, tn=1024, tk=2048):
    if trans_a:
        K, M = a.shape
    else:
        M, K = a.shape
    if trans_b:
        N, K2 = b.shape
    else:
        K2, N = b.shape
    assert K == K2 and not (trans_a and trans_b), (a.shape, b.shape)
    tm, tn = _pick_tile(M, tm, 16), _pick_tile(N, tn, LANE)
    tk = _pick_tile(K, tk, LANE if trans_b else 16)
    assert M % tm == 0 and N % tn == 0 and K % tk == 0, (M, N, K, tm, tn, tk)
    nk = K // tk
    dims = (((0 if trans_a else 1,), (1 if trans_b else 0,)), ((), ()))
    has_bias = bias is not None

    def body(*refs):
        a_ref, b_ref = refs[0], refs[1]
        bias_ref = refs[2] if has_bias else None
        o_ref = refs[3] if has_bias else refs[2]
        p = lax.dot_general(a_ref[...], b_ref[...], dims, preferred_element_type=F32)

        def finish(v):
            if has_bias:
                v = v + bias_ref[...]
            o_ref[...] = v.astype(o_ref.dtype)

        if nk == 1:
            finish(p)
        else:
            acc = refs[-1]
            k = pl.program_id(2)

            @pl.when(k == 0)
            def _():
                acc[...] = p

            @pl.when(k > 0)
            def _():
                acc[...] += p

            @pl.when(k == nk - 1)
            def _():
                finish(acc[...])

    if trans_a:
        a_spec = pl.BlockSpec((tk, tm), lambda i, j, k: (k, i))
    else:
        a_spec = pl.BlockSpec((tm, tk), lambda i, j, k: (i, k))
    if trans_b:
        b_spec = pl.BlockSpec((tn, tk), lambda i, j, k: (j, k))
    else:
        b_spec = pl.BlockSpec((tk, tn), lambda i, j, k: (k, j))
    in_specs = [a_spec, b_spec]
    args = [a, b]
    if has_bias:
        in_specs.append(pl.BlockSpec((1, tn), lambda i, j, k: (0, j)))
        args.append(bias)
    return pl.pallas_call(
        body, name=name, grid=(M // tm, N // tn, nk),
        in_specs=in_specs, out_specs=pl.BlockSpec((tm, tn), lambda i, j, k: (i, j)),
        out_shape=jax.ShapeDtypeStruct((M, N), out_dtype),
        scratch_shapes=[pltpu.VMEM((tm, tn), F32)] if nk > 1 else [],
        compiler_params=_cparams(),
    )(*args)


def rowwise(name, fn, S, T, row_ins, full_ins, row_outs, acc_outs=(), into=None):
    n_in = len(row_ins) + len(full_ins)
    n_ro, n_ao = len(row_outs), len(acc_outs)
    alias = into is not None and into[0] is not None

    def body(*refs):
        vals = [r[...] for r in refs[:n_in]]
        outs = fn(*vals)
        if not isinstance(outs, (tuple, list)):
            outs = (outs,)
        assert len(outs) == n_ro + n_ao, (name, len(outs))
        o0 = n_in + (1 if alias else 0)
        for r, v in zip(refs[o0:o0 + n_ro], outs[:n_ro]):
            r[...] = v.astype(r.dtype)
        first = pl.program_id(0) == 0
        for r, v in zip(refs[o0 + n_ro:], outs[n_ro:]):
            def init(r=r, v=v):
                r[...] = v

            def accum(r=r, v=v):
                r[...] += v

            pl.when(first)(init)
            pl.when(jnp.logical_not(first))(accum)

    in_specs, args = [], []
    for arr, W, off in row_ins:
        assert off % W == 0 and arr.shape[0] == S, (name, arr.shape, W, off)
        in_specs.append(pl.BlockSpec((T, W), functools.partial(lambda i, cb: (i, cb), cb=off // W)))
        args.append(arr)
    for arr in full_ins:
        in_specs.append(pl.BlockSpec(arr.shape, lambda i: (0, 0)))
        args.append(arr)
    out_specs = [pl.BlockSpec((T, W), lambda i: (i, 0)) for W, _ in row_outs]
    out_shape = [jax.ShapeDtypeStruct((S, W), dt) for W, dt in row_outs]
    aliases = {}
    if into is not None:
        buf, total, off = into
        W0, dt0 = row_outs[0]
        assert off % W0 == 0
        out_specs[0] = pl.BlockSpec((T, W0), functools.partial(lambda i, cb: (i, cb), cb=off // W0))
        out_shape[0] = jax.ShapeDtypeStruct((S, total), dt0)
        if alias:
            in_specs.append(ANY_SPEC)
            args.append(buf)
            aliases = {n_in: 0}
    out_specs += [pl.BlockSpec((1, W), lambda i: (0, 0)) for W in acc_outs]
    out_shape += [jax.ShapeDtypeStruct((1, W), F32) for W in acc_outs]
    return pl.pallas_call(
        body, name=name, grid=(S // T,), in_specs=in_specs, out_specs=out_specs, out_shape=out_shape,
        input_output_aliases=aliases, compiler_params=_cparams(),
    )(*args)


def _colsum(v):
    return jnp.sum(v, axis=0, keepdims=True)


def _ln_stats(r):
    mu = jnp.mean(r, axis=-1, keepdims=True)
    d = r - mu
    var = jnp.mean(d * d, axis=-1, keepdims=True)
    rstd = lax.rsqrt(var + LN_EPS)
    return d * rstd, rstd


def _ln_bwd(dn, n, rstd):
    return rstd * (dn - jnp.mean(dn, axis=-1, keepdims=True) - n * jnp.mean(dn * n, axis=-1, keepdims=True))


CPAD = 32
TC = 64


def _pre(mode, x1, x2):
    return x1 * _sigmoid(x2) if mode == "glu" else x1 * x2


def _shifted(ext, sft):
    n = TC + CPAD
    return pltpu.roll(ext, (n - sft) % n, 0)[0:TC]


def _interleaved_specs(S, off):
    return [pl.BlockSpec((S, LANE), functools.partial(lambda j, o: (0, o + 2 * j), o=off // LANE)),
            pl.BlockSpec((S, LANE), functools.partial(lambda j, o: (0, o + 2 * j + 1), o=off // LANE))]


def conv_fwd(name, src, off, w_pad, taps, mode, S, C):
    nchunk = S // TC

    def body(x1_ref, x2_ref, w_ref, o_ref, a_pad):
        a_pad[0:CPAD, :] = jnp.zeros((CPAD, LANE), F32)

        def fill(i, _):
            r = pl.multiple_of(i * 256, 256)
            a_pad[pl.ds(CPAD + r, 256), :] = _pre(mode, x1_ref[pl.ds(r, 256), :], x2_ref[pl.ds(r, 256), :])
            return 0

        lax.fori_loop(0, S // 256, fill, 0)

        def chunk(i, _):
            base = pl.multiple_of(i * TC, TC)
            ext = a_pad[pl.ds(base, TC + CPAD), :]
            acc = jnp.zeros((TC, LANE), F32)
            for k in range(taps):
                acc = acc + w_ref[pl.ds(k, 1), :] * _shifted(ext, CPAD - (taps - 1) + k)
            o_ref[pl.ds(base, TC), :] = acc
            return 0

        lax.fori_loop(0, nchunk, chunk, 0)

    kp = w_pad.shape[0]
    return pl.pallas_call(
        body, name=name, grid=(C // LANE,),
        in_specs=_interleaved_specs(S, off) + [pl.BlockSpec((kp, LANE), lambda j: (0, j))],
        out_specs=pl.BlockSpec((S, LANE), lambda j: (0, j)),
        out_shape=jax.ShapeDtypeStruct((S, C), F32),
        scratch_shapes=[pltpu.VMEM((S + CPAD, LANE), F32)],
        compiler_params=_cparams(),
    )(src, src, w_pad)


def conv_bwd(name, src, off, dc, w_pad, taps, mode, S, C, buf):
    nchunk = S // TC
    kp = w_pad.shape[0]

    def body(x1_ref, x2_ref, dc_ref, w_ref, _, d_ref, dw_ref, a_pad, dc_pad, dw_acc):
        a_pad[0:CPAD, :] = jnp.zeros((CPAD, LANE), F32)
        dc_pad[S:S + CPAD, :] = jnp.zeros((CPAD, LANE), F32)
        dw_acc[...] = jnp.zeros(dw_acc.shape, F32)

        def fill(i, _):
            r = pl.multiple_of(i * 256, 256)
            a_pad[pl.ds(CPAD + r, 256), :] = _pre(mode, x1_ref[pl.ds(r, 256), :], x2_ref[pl.ds(r, 256), :])
            dc_pad[pl.ds(r, 256), :] = dc_ref[pl.ds(r, 256), :]
            return 0

        lax.fori_loop(0, S // 256, fill, 0)

        def chunk(i, _):
            base = pl.multiple_of(i * TC, TC)
            ext_d = dc_pad[pl.ds(base, TC + CPAD), :]
            ext_a = a_pad[pl.ds(base, TC + CPAD), :]
            dcv = ext_d[0:TC]
            da = jnp.zeros((TC, LANE), F32)
            for k in range(taps):
                da = da + w_ref[pl.ds(k, 1), :] * _shifted(ext_d, taps - 1 - k)
                prod = dcv * _shifted(ext_a, CPAD - (taps - 1) + k)
                fold = prod[0:8]
                for g in range(1, TC // 8):
                    fold = fold + prod[8 * g:8 * g + 8]
                dw_acc[pl.ds(8 * k, 8), :] += fold
            x1 = x1_ref[pl.ds(base, TC), :]
            x2 = x2_ref[pl.ds(base, TC), :]
            if mode == "glu":
                s = _sigmoid(x2)
                d1, d2 = da * s, da * x1 * s * (1.0 - s)
            else:
                d1, d2 = da * x2, da * x1
            d_ref[pl.ds(base, TC), 0:LANE] = d1.astype(BF)
            d_ref[pl.ds(base, TC), LANE:2 * LANE] = d2.astype(BF)
            return 0

        lax.fori_loop(0, nchunk, chunk, 0)
        dw_ref[...] = jnp.zeros(dw_ref.shape, F32)
        for k in range(taps):
            dw_ref[pl.ds(k, 1), :] = jnp.sum(dw_acc[pl.ds(8 * k, 8), :], axis=0, keepdims=True)

    blk = pl.BlockSpec((S, LANE), lambda j: (0, j))
    return pl.pallas_call(
        body, name=name, grid=(C // LANE,),
        in_specs=_interleaved_specs(S, off) + [blk, pl.BlockSpec((kp, LANE), lambda j: (0, j)), ANY_SPEC],
        out_specs=[pl.BlockSpec((S, 2 * LANE), functools.partial(lambda j, o: (0, o + j), o=off // (2 * LANE))),
                   pl.BlockSpec((kp, LANE), lambda j: (0, j))],
        out_shape=[jax.ShapeDtypeStruct(buf.shape, BF), jax.ShapeDtypeStruct((kp, C), F32)],
        input_output_aliases={4: 0},
        scratch_shapes=[pltpu.VMEM((S + CPAD, LANE), F32), pltpu.VMEM((S + CPAD, LANE), F32),
                        pltpu.VMEM((8 * kp, LANE), F32)],
        compiler_params=_cparams(),
    )(src, src, dc, w_pad, buf)


FWD_TILES = (512, 512)
BWD_TILES = (256, 512)
QUADS = HEADS // 4
QW, KVW = 4 * (NOPE + ROPE), 4 * (NOPE + VH)
SCALE = (NOPE + ROPE) ** -0.5
NT_DIMS = (((1,), (1,)), ((), ()))
TN_DIMS = (((0,), (0,)), ((), ()))


def _lane_mask(width, group, dtype):
    lane = lax.broadcasted_iota(jnp.int32, (1, LANE), 1)
    return jnp.where(lane // width == group, 1.0, 0.0).astype(dtype)


def _visible(tq, tk, off):
    row = lax.broadcasted_iota(jnp.int32, (tq, tk), 0)
    col = lax.broadcasted_iota(jnp.int32, (tq, tk), 1)
    return col <= row + off


def _attn_tiles(S, tq, tk):
    tk = tk if S % tk == 0 else 256
    return min(tq, tk), tk


def attn_fwd(q, kv, kpe, S):
    tq, tk = _attn_tiles(S, *FWD_TILES)
    nq = S // tq

    def body(q_ref, kv_ref, kp_ref, o_ref, lse_ref):
        for t in range(2):
            cols = slice(t * LANE, (t + 1) * LANE)
            for hh in range(2):
                def q_block(qi, _, t=t, hh=hh, cols=cols):
                    r0 = pl.multiple_of(qi * tq, tq)
                    qcat = jnp.concatenate([q_ref[pl.ds(r0, tq), cols] * _lane_mask(NOPE, hh, BF),
                                            q_ref[pl.ds(r0, tq), 2 * LANE:3 * LANE] * _lane_mask(ROPE, 2 * t + hh, BF)],
                                           axis=1)
                    nfull = (qi * tq) // tk

                    def step(kj, carry, masked):
                        m, l, acc = carry
                        c0 = pl.multiple_of(kj * tk, tk)
                        kc = jnp.concatenate([kv_ref[pl.ds(c0, tk), cols], kp_ref[pl.ds(c0, tk), :]], axis=1)
                        vt = kv_ref[pl.ds(c0, tk), (2 + t) * LANE:(3 + t) * LANE]
                        s = lax.dot_general(qcat, kc, NT_DIMS, preferred_element_type=F32) * SCALE
                        if masked:
                            s = jnp.where(_visible(tq, tk, qi * tq - nfull * tk), s, -jnp.inf)
                        m_new = jnp.maximum(m, jnp.max(s, axis=-1, keepdims=True))
                        p = jnp.exp(s - m_new)
                        alpha = jnp.exp(m - m_new)
                        l = alpha * l + jnp.sum(p, axis=-1, keepdims=True)
                        acc = alpha * acc + jnp.dot(p.astype(BF), vt, preferred_element_type=F32)
                        return m_new, l, acc

                    init = (jnp.full((tq, 1), -jnp.inf, F32), jnp.zeros((tq, 1), F32), jnp.zeros((tq, LANE), F32))
                    carry = lax.fori_loop(0, nfull, lambda kj, c: step(kj, c, False), init)
                    m, l, acc = step(nfull, carry, True)
                    mine = _lane_mask(NOPE, hh, F32)
                    if hh == 0:
                        o_ref[pl.ds(r0, tq), cols] = (acc / l) * mine
                        lse_ref[pl.ds(r0, tq), cols] = (m + jnp.log(l)) * mine
                    else:
                        o_ref[pl.ds(r0, tq), cols] += (acc / l) * mine
                        lse_ref[pl.ds(r0, tq), cols] += (m + jnp.log(l)) * mine
                    return 0

                lax.fori_loop(0, nq, q_block, 0)

    return pl.pallas_call(
        body, name="attn_fwd", grid=(QUADS,),
        in_specs=[pl.BlockSpec((S, QW), lambda g: (0, g)), pl.BlockSpec((S, KVW), lambda g: (0, g)),
                  pl.BlockSpec((S, LANE), lambda g: (0, 0))],
        out_specs=[pl.BlockSpec((S, 2 * LANE), lambda g: (0, g))] * 2,
        out_shape=[jax.ShapeDtypeStruct((S, HEADS * VH), F32)] * 2,
        compiler_params=_cparams(),
    )(q, kv, kpe)


def attn_bwd(q, kv, kpe, o, lse, do, S):
    tq, tk = _attn_tiles(S, *BWD_TILES)
    nq = S // tq

    def body(q_ref, kv_ref, kp_ref, o_ref, lse_ref, do_ref, dq_ref, dkv_ref, dkp_ref, dq_acc, dk_acc, dv_acc):
        for t in range(2):
            cols = slice(t * LANE, (t + 1) * LANE)
            dk_acc[...] = jnp.zeros(dk_acc.shape, F32)
            dv_acc[...] = jnp.zeros(dv_acc.shape, F32)
            for hh in range(2):
                def q_block(qi, _, t=t, hh=hh, cols=cols):
                    r0 = pl.multiple_of(qi * tq, tq)
                    mine = _lane_mask(NOPE, hh, F32)
                    qcat = jnp.concatenate([q_ref[pl.ds(r0, tq), cols] * _lane_mask(NOPE, hh, BF),
                                            q_ref[pl.ds(r0, tq), 2 * LANE:3 * LANE] * _lane_mask(ROPE, 2 * t + hh, BF)],
                                           axis=1)
                    dof = do_ref[pl.ds(r0, tq), cols] * mine
                    dob = dof.astype(BF)
                    delta = jnp.sum(dof * o_ref[pl.ds(r0, tq), cols], axis=-1, keepdims=True)
                    lse_h = lse_ref[pl.ds(r0, tq), cols][:, hh * NOPE:hh * NOPE + 1]
                    nfull = (qi * tq) // tk
                    dq_acc[...] = jnp.zeros(dq_acc.shape, F32)

                    def step(kj, _, masked):
                        c0 = pl.multiple_of(kj * tk, tk)
                        kc = jnp.concatenate([kv_ref[pl.ds(c0, tk), cols], kp_ref[pl.ds(c0, tk), :]], axis=1)
                        vt = kv_ref[pl.ds(c0, tk), (2 + t) * LANE:(3 + t) * LANE]
                        s = lax.dot_general(qcat, kc, NT_DIMS, preferred_element_type=F32) * SCALE
                        if masked:
                            s = jnp.where(_visible(tq, tk, qi * tq - nfull * tk), s, -jnp.inf)
                        p = jnp.exp(s - lse_h)
                        dp = lax.dot_general(dob, vt, NT_DIMS, preferred_element_type=F32)
                        ds = (p * (dp - delta) * SCALE).astype(BF)
                        dv_acc[pl.ds(c0, tk), :] += lax.dot_general(p.astype(BF), dob, TN_DIMS,
                                                                    preferred_element_type=F32)
                        dk_acc[pl.ds(c0, tk), :] += lax.dot_general(ds, qcat, TN_DIMS, preferred_element_type=F32)
                        dq_acc[...] += jnp.dot(ds, kc, preferred_element_type=F32)
                        return 0

                    lax.fori_loop(0, nfull, lambda kj, c: step(kj, c, False), 0)
                    step(nfull, 0, True)
                    d = dq_acc[...]
                    pe = d[:, LANE:] * _lane_mask(ROPE, 2 * t + hh, F32)
                    if hh == 0:
                        dq_ref[pl.ds(r0, tq), cols] = d[:, :LANE] * mine
                    else:
                        dq_ref[pl.ds(r0, tq), cols] += d[:, :LANE] * mine
                    if t == 0 and hh == 0:
                        dq_ref[pl.ds(r0, tq), 2 * LANE:3 * LANE] = pe
                    else:
                        dq_ref[pl.ds(r0, tq), 2 * LANE:3 * LANE] += pe
                    return 0

                lax.fori_loop(0, nq, q_block, 0)
            dkv_ref[:, t * LANE:(t + 1) * LANE] = dk_acc[:, :LANE].astype(BF)
            dkv_ref[:, (2 + t) * LANE:(3 + t) * LANE] = dv_acc[...].astype(BF)
            if t == 0:
                dkp_ref[...] = dk_acc[:, LANE:]
            else:
                dkp_ref[...] += dk_acc[:, LANE:]

    qspec = pl.BlockSpec((S, QW), lambda g: (0, g))
    kvspec = pl.BlockSpec((S, KVW), lambda g: (0, g))
    ospec = pl.BlockSpec((S, 2 * LANE), lambda g: (0, g))
    return pl.pallas_call(
        body, name="attn_bwd", grid=(QUADS,),
        in_specs=[qspec, kvspec, pl.BlockSpec((S, LANE), lambda g: (0, 0)), ospec, ospec, ospec],
        out_specs=[qspec, kvspec, pl.BlockSpec((S, LANE), lambda g: (0, g))],
        out_shape=[jax.ShapeDtypeStruct((S, HEADS * (NOPE + ROPE)), F32), jax.ShapeDtypeStruct((S, HEADS * (NOPE + VH)), BF),
                   jax.ShapeDtypeStruct((S, HEADS * ROPE), F32)],
        scratch_shapes=[pltpu.VMEM((tq, 2 * LANE), F32), pltpu.VMEM((S, 2 * LANE), F32), pltpu.VMEM((S, LANE), F32)],
        compiler_params=_cparams(),
    )(q, kv, kpe, o, lse, do)


def exchange(name, gathers, a2as):
    n_g, n = len(gathers), len(gathers) + len(a2as)

    def body(*refs):
        ins, outs = refs[:n], refs[n:2 * n]
        send_sems, recv_sems, loc_sems = refs[2 * n:]
        x, y, c = lax.axis_index("x"), lax.axis_index("y"), lax.axis_index("c")
        me = 4 * x + 2 * y + c

        def peer(k):
            px = 1 - x if k & 4 else x
            py = 1 - y if k & 2 else y
            pc = 1 - c if k & 1 else c
            return (px, py, pc), 4 * px + 2 * py + pc

        def remote(a, k):
            pid, pflat = peer(k)
            src = ins[a] if a < n_g else ins[a].at[pflat]
            return pltpu.make_async_remote_copy(
                src_ref=src, dst_ref=outs[a].at[me], send_sem=send_sems.at[a, k - 1], recv_sem=recv_sems.at[a, k - 1],
                device_id=pid, device_id_type=MESH)

        def arrival(a, k):
            pid, pflat = peer(k)
            src = ins[a] if a < n_g else ins[a].at[pflat]
            return pltpu.make_async_remote_copy(
                src_ref=src, dst_ref=outs[a].at[pflat], send_sem=send_sems.at[a, k - 1], recv_sem=recv_sems.at[a, k - 1],
                device_id=pid, device_id_type=MESH)

        local = []
        for a in range(n):
            own = ins[a] if a < n_g else ins[a].at[me]
            cp = pltpu.make_async_copy(own, outs[a].at[me], loc_sems.at[a])
            cp.start()
            local.append(cp)
        sent = []
        for k in (1, 2, 4, 3, 5, 6, 7):
            for a in range(n):
                cp = remote(a, k)
                cp.start()
                sent.append(cp)
        for k in range(1, 8):
            for a in range(n):
                arrival(a, k).wait_recv()
        for cp in sent:
            cp.wait_send()
        for cp in local:
            cp.wait()

    out_shape = [jax.ShapeDtypeStruct((NDEV,) + g.shape, g.dtype) for g in gathers]
    out_shape += [jax.ShapeDtypeStruct(a.shape, a.dtype) for a in a2as]
    any_spec = pl.BlockSpec(memory_space=pl.ANY)
    return pl.pallas_call(
        body, name=name, in_specs=[any_spec] * n, out_specs=[any_spec] * n, out_shape=out_shape,
        scratch_shapes=[pltpu.SemaphoreType.DMA((n, NDEV - 1)), pltpu.SemaphoreType.DMA((n, NDEV - 1)),
                        pltpu.SemaphoreType.DMA((n,))],
    )(*gathers, *a2as)


def _peer(k, x, y, c):
    px = 1 - x if k & 4 else x
    py = 1 - y if k & 2 else y
    pc = 1 - c if k & 1 else c
    return (px, py, pc), 4 * px + 2 * py + pc


PEER_ORDER = (1, 2, 4, 3, 5, 6, 7)
HBM_SPEC = pl.BlockSpec(memory_space=pltpu.HBM)
SEM_SPEC = pl.BlockSpec(memory_space=pltpu.SEMAPHORE)
ANY_SPEC = pl.BlockSpec(memory_space=pl.ANY)


def _split_copies(ins, lands, n_g, send_sems, recv_sems):
    x, y, c = lax.axis_index("x"), lax.axis_index("y"), lax.axis_index("c")
    me = 4 * x + 2 * y + c

    def outgoing(a, k):
        pid, pflat = _peer(k, x, y, c)
        src = ins[a] if a < n_g else ins[a].at[pflat]
        return pltpu.make_async_remote_copy(
            src_ref=src, dst_ref=lands[a].at[me], send_sem=send_sems.at[a * (NDEV - 1) + k - 1],
            recv_sem=recv_sems.at[a * (NDEV - 1) + k - 1],
            device_id=pid, device_id_type=MESH)

    def arrival(a, k):
        pid, pflat = _peer(k, x, y, c)
        src = ins[a] if a < n_g else ins[a].at[pflat]
        return pltpu.make_async_remote_copy(
            src_ref=src, dst_ref=lands[a].at[pflat], send_sem=send_sems.at[a * (NDEV - 1) + k - 1],
            recv_sem=recv_sems.at[a * (NDEV - 1) + k - 1],
            device_id=pid, device_id_type=MESH)

    return outgoing, arrival


def exchange_begin(name, srcs, n_g, dep):
    n = len(srcs)
    land_shapes = [((NDEV,) + s.shape) if a < n_g else s.shape for a, s in enumerate(srcs)]

    def own_body(*refs):
        ins, outs = refs[:n], refs[n + 1:2 * n + 1]
        stage, sems = refs[2 * n + 1:3 * n + 1], refs[-1]
        me = 4 * lax.axis_index("x") + 2 * lax.axis_index("y") + lax.axis_index("c")
        cps = [pltpu.make_async_copy(ins[a] if a < n_g else ins[a].at[me], stage[a], sems.at[a]) for a in range(n)]
        for cp in cps:
            cp.start()
        for cp in cps:
            cp.wait()
        cps = [pltpu.make_async_copy(stage[a], outs[a].at[me], sems.at[a]) for a in range(n)]
        for cp in cps:
            cp.start()
        for cp in cps:
            cp.wait()

    lands = pl.pallas_call(
        own_body, name=name + "_own", in_specs=[ANY_SPEC] * (n + 1), out_specs=[ANY_SPEC] * n,
        out_shape=[jax.ShapeDtypeStruct(sh, s.dtype) for sh, s in zip(land_shapes, srcs)],
        scratch_shapes=[pltpu.VMEM(sh[1:], s.dtype) for sh, s in zip(land_shapes, srcs)] + [pltpu.SemaphoreType.DMA((n,))],
        compiler_params=_cparams(),
    )(*srcs, dep)

    def start_body(*refs):
        ins, lz = refs[:n], refs[n:2 * n]
        send_sems, recv_sems, token = refs[2 * n], refs[2 * n + 1], refs[-1]
        outgoing, _ = _split_copies(ins, lz, n_g, send_sems, recv_sems)
        for k in PEER_ORDER:
            for a in range(n):
                outgoing(a, k).start()
        token[...] = jnp.zeros(token.shape, F32)

    hbm = lambda t: pltpu.HBM(t.shape, t.dtype)
    res = pl.pallas_call(
        start_body, name=name + "_start",
        out_shape=(pltpu.SemaphoreType.DMA((n * (NDEV - 1),)), pltpu.SemaphoreType.DMA((n * (NDEV - 1),)),
                   *[hbm(s) for s in srcs], *[hbm(t) for t in lands], jax.ShapeDtypeStruct((8, LANE), F32)),
        in_specs=[HBM_SPEC] * (2 * n),
        out_specs=(SEM_SPEC, SEM_SPEC, *[HBM_SPEC] * (2 * n), pl.BlockSpec(memory_space=pltpu.VMEM)),
        input_output_aliases={i: 2 + i for i in range(2 * n)},
        compiler_params=pltpu.CompilerParams(has_side_effects=pltpu.SideEffectType.DATAFLOW_SIDE_EFFECTING),
    )(*[pltpu.with_memory_space_constraint(t, pltpu.HBM) for t in list(srcs) + list(lands)])
    return (name, n, n_g, res[:-1]), res[-1]


def exchange_end(handle, after):
    name, n, n_g, (send_sems, recv_sems, *bufs) = handle

    def wait_body(*refs):
        ins, lz = refs[:n], refs[n:2 * n]
        ss, rs = refs[2 * n], refs[2 * n + 1]
        outgoing, arrival = _split_copies(ins, lz, n_g, ss, rs)
        for k in range(1, NDEV):
            for a in range(n):
                arrival(a, k).wait_recv()
        for k in range(1, NDEV):
            for a in range(n):
                outgoing(a, k).wait_send()

    res = pl.pallas_call(
        wait_body, name=name + "_wait", out_shape=tuple(pltpu.HBM(t.shape, t.dtype) for t in bufs),
        in_specs=[HBM_SPEC] * (2 * n) + [SEM_SPEC, SEM_SPEC, ANY_SPEC], out_specs=[HBM_SPEC] * (2 * n),
        input_output_aliases={i: i for i in range(2 * n)},
        compiler_params=pltpu.CompilerParams(has_side_effects=pltpu.SideEffectType.DATAFLOW_SIDE_EFFECTING),
    )(*bufs, send_sems, recv_sems, after)
    return list(res[n:])


def _pick_rows(R, mult, cap):
    best = None
    for n in range(1, R + 1):
        if R % n == 0 and (R // n) % mult == 0 and R // n <= cap:
            best = R // n
            break
    assert best is not None, (R, mult, cap)
    return best


def sum_slots(name, x):
    _, R, _ = x.shape
    tr = _pick_rows(R, 16, 2304)

    def body(x_ref, o_ref):
        acc = x_ref[0].astype(F32)
        for d in range(1, NDEV):
            acc = acc + x_ref[d].astype(F32)
        o_ref[...] = acc

    return pl.pallas_call(
        body, name=name, grid=(R // tr,),
        in_specs=[pl.BlockSpec((NDEV, tr, LANE), lambda i: (0, i, 0))],
        out_specs=pl.BlockSpec((tr, LANE), lambda i: (i, 0)),
        out_shape=jax.ShapeDtypeStruct((R, LANE), F32), compiler_params=_cparams(),
    )(x)


def adamw(name, w, g, m, v):
    L, R, C = w.shape
    tr = _pick_rows(R, 8, 256) if R % 8 == 0 else R

    def body(w_ref, g_ref, m_ref, v_ref, d_ref, nm_ref, nv_ref):
        gg = g_ref[...]
        nm = B1 * m_ref[...] + (1.0 - B1) * gg
        nv = B2 * v_ref[...] + (1.0 - B2) * jnp.square(gg)
        m_hat = nm / (1.0 - B1 ** STEP)
        v_hat = nv / (1.0 - B2 ** STEP)
        d_ref[...] = -LR * (m_hat / (jnp.sqrt(v_hat) + EPS) + WD * w_ref[...])
        nm_ref[...] = nm
        nv_ref[...] = nv

    blk = pl.BlockSpec((1, tr, C), lambda l, i: (l, i, 0))
    shp = jax.ShapeDtypeStruct(w.shape, F32)
    return pl.pallas_call(
        body, name=name, grid=(L, R // tr), in_specs=[blk] * 4, out_specs=[blk] * 3, out_shape=[shp] * 3,
        compiler_params=_cparams(),
    )(w, g, m, v)


IN_SHARD = D_IN // NDEV
UQ_SHARD = HEADS * (NOPE + ROPE) // NDEV
W_IN_PAD = 1024
ROW_A, ROW_B, ROW_C, ROW_UKV, ROW_UQ, MISC_ROWS = 0, 512, 1024, 1536, 1792, 2176


def _in_perm_index():
    ar = np.arange
    z = lambda n: np.full((n,), -1, np.int64)
    mix = lambda lo1, lo2: np.concatenate([ar(lo + LANE * j, lo + LANE * (j + 1)) for j in range(CW // LANE)
                                           for lo in (lo1, lo2)])
    return np.concatenate([ar(4768, 7840), mix(0, 512), ar(1024, 1536), mix(1536, 2560), ar(4256, 4768), ar(2048, 2560),
                           ar(3072, 3584), ar(3968, 4224), ar(4224, 4256), z(OFF_Q - OFF_KR - ROPE), ar(3584, 3968),
                           z(NP - OFF_Q - QL)])


def _head_perm_index(a, b):
    parts = []
    for g in range(QUADS):
        h = np.arange(4 * g, 4 * g + 4)[:, None] * (a + b)
        parts += [(h + np.arange(a)[None]).reshape(-1), (h + a + np.arange(b)[None]).reshape(-1)]
    return np.concatenate(parts)


def _inverse(perm, n):
    inv = np.full((n,), -1, np.int64)
    inv[perm[perm >= 0]] = np.nonzero(perm >= 0)[0]
    return inv


IN_PERM = _in_perm_index()
UQ_PERM = _head_perm_index(NOPE, ROPE)
UKV_PERM = _head_perm_index(NOPE, VH)


def _to_gathered(perm, shard, pad):
    return np.where(perm >= 0, (perm // shard) * pad + perm % shard, -1)


def _from_full(inv, shard, pad):
    j, i = np.divmod(np.arange(NDEV * pad), pad)
    return np.where(i < shard, inv[np.minimum(j * shard + i, inv.shape[0] - 1)], -1)


def col_gather(name, srcs, out_shapes, jobs, deps=()):
    ns, nj, nd = len(srcs), len(jobs), len(deps)
    tables = [jnp.asarray(np.asarray(job[5], np.int32)[None, :]) for job in jobs]

    def view(ref, col0, width, r0, rc):
        n = ref.shape[-1]
        if len(ref.shape) == 3:
            return ref.at[col0 // n, pl.ds(r0, rc), pl.ds(col0 % n, width)]
        return ref.at[pl.ds(r0, rc), pl.ds(col0, width)]

    def body(*refs):
        src_refs, tab_refs, out_refs = refs[:ns], refs[ns:ns + nj], refs[ns + nj + nd:]
        for ji, (si, srow, oi, orow, nrows, tgt) in enumerate(jobs):
            sref, oref = src_refs[si], out_refs[oi]
            tgt = np.asarray(tgt)
            tw = 256 if oref.shape[-1] % 256 == 0 else LANE
            rc = 256 if nrows % 256 == 0 else LANE
            for t in range(tgt.shape[0] // tw):
                tt = tgt[t * tw:(t + 1) * tw]
                tiles = sorted(set((tt[tt >= 0] // LANE).tolist()))
                straight = bool(tiles) and np.array_equal(tt, np.arange(tiles[0] * LANE, tiles[0] * LANE + tw))
                onehots = []
                if tiles and not straight:
                    want = tab_refs[ji][:, t * tw:(t + 1) * tw]
                    row = lax.broadcasted_iota(jnp.int32, (LANE, tw), 0)
                    onehots = [jnp.where(want == row + s * LANE, 1.0, 0.0).astype(BF) for s in tiles]

                def chunk(ci, _, t=t, tiles=tiles, straight=straight, onehots=onehots):
                    r0 = ci * rc
                    dst = view(oref, t * tw, tw, pl.multiple_of(orow + r0, LANE), rc)
                    rs = pl.multiple_of(srow + r0, LANE)
                    if not tiles:
                        dst[...] = jnp.zeros((rc, tw), BF)
                    elif straight:
                        for k in range(tw // LANE):
                            view(oref, t * tw + k * LANE, LANE, pl.multiple_of(orow + r0, LANE), rc)[...] = (
                                view(sref, (tiles[0] + k) * LANE, LANE, rs, rc)[...])
                    else:
                        acc = None
                        for s, oh in zip(tiles, onehots):
                            p = jnp.dot(view(sref, s * LANE, LANE, rs, rc)[...], oh, preferred_element_type=F32)
                            acc = p if acc is None else acc + p
                        dst[...] = acc.astype(BF)
                    return 0

                lax.fori_loop(0, nrows // rc, chunk, 0)

    vmem = pl.BlockSpec(memory_space=pltpu.VMEM)
    return pl.pallas_call(
        body, name=name, in_specs=[vmem] * (ns + nj) + [ANY_SPEC] * nd, out_specs=[vmem] * len(out_shapes),
        out_shape=[jax.ShapeDtypeStruct(s, BF) for s in out_shapes], compiler_params=_cparams(),
    )(*srcs, *tables, *deps)


def sum_adamw(name, recvs, w, m, v):
    L, R, C = w.shape
    CP = recvs[0].shape[-1]
    tr = _pick_rows(R, 16, 128)

    def body(*refs):
        r_refs = refs[:L]
        w_ref, m_ref, v_ref, g_ref, d_ref, nm_ref, nv_ref, gsum = refs[L:]
        layer = pl.program_id(0)
        for k in range(L):
            def total(k=k):
                acc = r_refs[k][0].astype(F32)
                for d in range(1, NDEV):
                    acc = acc + r_refs[k][d].astype(F32)
                gsum[...] = acc
            pl.when(layer == k)(total)
        gg = gsum[:, 0:C]
        nm = B1 * m_ref[...] + (1.0 - B1) * gg
        nv = B2 * v_ref[...] + (1.0 - B2) * jnp.square(gg)
        m_hat = nm / (1.0 - B1 ** STEP)
        v_hat = nv / (1.0 - B2 ** STEP)
        g_ref[...] = gg
        d_ref[...] = -LR * (m_hat / (jnp.sqrt(v_hat) + EPS) + WD * w_ref[...])
        nm_ref[...] = nm
        nv_ref[...] = nv

    r_specs = [pl.BlockSpec((NDEV, tr, CP), functools.partial(lambda l, i, k: (0, jnp.where(l == k, i, 0), 0), k=k))
               for k in range(L)]
    blk = pl.BlockSpec((None, tr, C), lambda l, i: (l, i, 0))
    shp = jax.ShapeDtypeStruct(w.shape, F32)
    return pl.pallas_call(
        body, name=name, grid=(L, R // tr), in_specs=r_specs + [blk] * 3, out_specs=[blk] * 4, out_shape=[shp] * 4,
        scratch_shapes=[pltpu.VMEM((tr, CP), F32)], compiler_params=_cparams(),
    )(*recvs, w, m, v)


ALPHA = 8.0 ** 0.25


def _rope_fn(sign):
    def fn(x, cos, sin):
        W = x.shape[-1]
        lane = lax.broadcasted_iota(jnp.int32, x.shape, 1)
        first_half = (lane % ROPE) < (ROPE // 2)
        rot = jnp.where(first_half, -pltpu.roll(x, W - ROPE // 2, 1), pltpu.roll(x, ROPE // 2, 1))
        return x * cos + sign * rot * sin
    return fn


def layer_fwd(x, ada3, W, tabs, S):
    cos, sin = tabs
    T = 256
    u = rowwise("modulate", lambda xv, a: xv * (1.0 + a[1:2, :]) + a[0:1, :], S, T,
                [(x, D_MODEL, 0)], [ada3], [(D_MODEL, BF)])[0]
    proj = mm(u, W["in"], name="mm_proj", tm=1024, tn=1024)

    ca = conv_fwd("conv_a_fwd", proj, OFF_A, W["conv_a"], 31, "glu", S, CW)

    def a_post(c, ag, vec):
        n, _ = _ln_stats(c + vec[0:1, :])
        return _silu(n * vec[1:2, :] + vec[2:3, :]) * _silu(ag)

    h_a = rowwise("mix_a_post", a_post, S, T, [(ca, CW, 0), (proj, CW, OFF_AG)], [W["vec_a"]], [(CW, BF)])[0]
    y_a = mm(h_a, W["a_out"], name="mm_branch_out")

    cb = conv_fwd("conv_b_fwd", proj, OFF_B, W["conv_b"], 3, "mul", S, CW)
    h_b = rowwise("mix_b_post", lambda c, gb, bg: gb * c * _silu(bg), S, T,
                  [(cb, CW, 0), (proj, CW, OFF_GB), (proj, CW, OFF_BG)], [], [(CW, BF)])[0]
    y_b = mm(h_b, W["b_out"], name="mm_branch_out")

    def rms2(ql, kvl, gq, gkv):
        rq = lax.rsqrt(jnp.mean(ql * ql, axis=-1, keepdims=True) + RMS_EPS)
        rk = lax.rsqrt(jnp.mean(kvl * kvl, axis=-1, keepdims=True) + RMS_EPS)
        return ql * rq * gq, kvl * rk * gkv

    qn, kvn = rowwise("rms_fwd", rms2, S, T, [(proj, QL, OFF_Q), (proj, KVL, OFF_KV)], [W["gq"], W["gkv"]],
                      [(QL, BF), (KVL, BF)])
    q = mm(qn, W["uq"], name="mm_q")
    kv = mm(kvn, W["ukv"], name="mm_kv", out_dtype=BF)
    rope = _rope_fn(1.0)

    def rope_fwd(qv, kr, c1, s1):
        parts = []
        for g in range(QUADS):
            parts.append(qv[:, g * QW:g * QW + 2 * LANE].astype(BF))
            parts.append(rope(qv[:, g * QW + 2 * LANE:(g + 1) * QW], c1, s1).astype(BF))
        kp = rope(kr, c1, s1)
        kp = kp + pltpu.roll(kp, ROPE, 1) + pltpu.roll(kp, 2 * ROPE, 1) + pltpu.roll(kp, 3 * ROPE, 1)
        return jnp.concatenate(parts, axis=1), kp

    q_b, kpe = rowwise("rope_fwd", rope_fwd, S, T,
                       [(q, HEADS * (NOPE + ROPE), 0), (proj, LANE, OFF_KR), (cos, LANE, 0), (sin, LANE, 0)], [],
                       [(HEADS * (NOPE + ROPE), BF), (LANE, BF)])
    o, lse = attn_fwd(q_b, kv, kpe, S)
    h_c = rowwise("mix_c_post", lambda ov, cg: ov * _silu(cg), S, T, [(o, CW, 0), (proj, CW, OFF_CG)], [],
                  [(CW, BF)])[0]
    y_c = mm(h_c, W["c_out"], name="mm_branch_out")

    def merge(la, lb, lc, ya, yb, yc):
        return _sigmoid(la) * ya + _sigmoid(lb) * yb + _sigmoid(lc) * yc

    m = rowwise("merge_fwd", merge, S, 128,
                [(proj, D_MODEL, 0), (proj, D_MODEL, 1024), (proj, D_MODEL, 2048), (y_a, D_MODEL, 0),
                 (y_b, D_MODEL, 0), (y_c, D_MODEL, 0)], [], [(D_MODEL, BF)])[0]
    out = mm(m, W["o"], name="mm_out")

    def ln_fwd(xv, ov, a, lnv):
        n, _ = _ln_stats(ALPHA * xv + a[2:3, :] * ov)
        return n * lnv[0:1, :] + lnv[1:2, :]

    x_next = rowwise("ln_fwd", ln_fwd, S, 128, [(x, D_MODEL, 0), (out, D_MODEL, 0)], [ada3, W["lnv"]],
                     [(D_MODEL, F32)])[0]
    saved = dict(x=x, u=u, proj=proj, ca=ca, cb=cb, h_a=h_a, h_b=h_b, h_c=h_c, y_a=y_a, y_b=y_b, y_c=y_c, qn=qn,
                 kvn=kvn, q_b=q_b, kv=kv, kpe=kpe, lse=lse, o=o, m=m, out=out)
    return x_next, saved


def layer_bwd(dxn, sv, ada3, W, tabs, S):
    cos, sin = tabs
    T = 256
    x, proj = sv["x"], sv["proj"]
    G = {}

    def ln_bwd(xv, ov, dy, a, lnv):
        gate = a[2:3, :]
        n, rstd = _ln_stats(ALPHA * xv + gate * ov)
        dr = _ln_bwd(dy * lnv[0:1, :], n, rstd)
        return ALPHA * dr, gate * dr, _colsum(dy * n), _colsum(dy), _colsum(dr * ov)

    dres, d_out, G["ln_g"], G["ln_b"], d_gate = rowwise(
        "ln_bwd", ln_bwd, S, 128, [(x, D_MODEL, 0), (sv["out"], D_MODEL, 0), (dxn, D_MODEL, 0)], [ada3, W["lnv"]],
        [(D_MODEL, F32), (D_MODEL, BF)], [D_MODEL] * 3)
    dm = mm(d_out, W["o"], name="mm_dm", trans_b=True)
    G["w_o"] = mm(sv["m"], d_out, name="mm_gw_o", trans_a=True, out_dtype=BF)

    def merge_bwd(dmv, la, lb, lc, ya, yb, yc):
        outs, dls = [], []
        for lg, yv in ((la, ya), (lb, yb), (lc, yc)):
            s = _sigmoid(lg)
            outs.append(dmv * s)
            dls.append((dmv * yv * s * (1.0 - s)).astype(BF))
        return (jnp.concatenate(dls, axis=1),) + tuple(outs)

    d_proj, dy_a, dy_b, dy_c = rowwise(
        "merge_bwd", merge_bwd, S, 128,
        [(dm, D_MODEL, 0), (proj, D_MODEL, 0), (proj, D_MODEL, 1024), (proj, D_MODEL, 2048), (sv["y_a"], D_MODEL, 0),
         (sv["y_b"], D_MODEL, 0), (sv["y_c"], D_MODEL, 0)], [], [(3 * D_MODEL, BF)] + [(D_MODEL, BF)] * 3,
        into=(None, NP, OFF_M))

    dh = {}
    for br, dy in (("a", dy_a), ("b", dy_b), ("c", dy_c)):
        dh[br] = mm(dy, W[br + "_out"], name="mm_dh", trans_b=True)
        G["w_%s_out" % br] = mm(sv["h_" + br], dy, name="mm_gw_branch", trans_a=True, out_dtype=BF)

    def a_post_bwd(c, ag, dhv, vec):
        n, rstd = _ln_stats(c + vec[0:1, :])
        z = n * vec[1:2, :] + vec[2:3, :]
        d_ag = dhv * _silu(z) * _dsilu(ag)
        dz = dhv * _silu(ag) * _dsilu(z)
        dc = _ln_bwd(dz * vec[1:2, :], n, rstd)
        return d_ag, dc, _colsum(dc), _colsum(dz * n), _colsum(dz)

    d_proj, dca, G["conv_a_b"], G["ln_a_g"], G["ln_a_b"] = rowwise(
        "mix_a_post_bwd", a_post_bwd, S, T, [(sv["ca"], CW, 0), (proj, CW, OFF_AG), (dh["a"], CW, 0)], [W["vec_a"]],
        [(CW, BF), (CW, F32)], [CW] * 3, into=(d_proj, NP, OFF_AG))
    d_proj, G["conv_a_w"] = conv_bwd("conv_a_bwd", proj, OFF_A, dca, W["conv_a"], 31, "glu", S, CW, d_proj)

    def b_post_bwd(c, gb, bg, dhv):
        sg = _silu(bg)
        d_gb_bg = jnp.concatenate([(dhv * sg * c).astype(BF), (dhv * gb * c * _dsilu(bg)).astype(BF)], axis=1)
        return d_gb_bg, dhv * sg * gb

    d_proj, dcb = rowwise("mix_b_post_bwd", b_post_bwd, S, T,
                          [(sv["cb"], CW, 0), (proj, CW, OFF_GB), (proj, CW, OFF_BG), (dh["b"], CW, 0)], [],
                          [(2 * CW, BF), (CW, F32)], into=(d_proj, NP, OFF_GB))
    d_proj, G["conv_b_w"] = conv_bwd("conv_b_bwd", proj, OFF_B, dcb, W["conv_b"], 3, "mul", S, CW, d_proj)

    d_proj, d_o = rowwise("mix_c_post_bwd", lambda ov, cg, dhv: (dhv * ov * _dsilu(cg), dhv * _silu(cg)), S, T,
                          [(sv["o"], CW, 0), (proj, CW, OFF_CG), (dh["c"], CW, 0)], [], [(CW, BF), (CW, F32)],
                          into=(d_proj, NP, OFF_CG))
    dq, d_kv, dkp_heads = attn_bwd(sv["q_b"], sv["kv"], sv["kpe"], sv["o"], sv["lse"], d_o, S)
    ropeT = _rope_fn(-1.0)

    def rope_bwd(dqv, dkp, c1, s1):
        parts = []
        for g in range(QUADS):
            parts.append(dqv[:, g * QW:g * QW + 2 * LANE].astype(BF))
            parts.append(ropeT(dqv[:, g * QW + 2 * LANE:(g + 1) * QW], c1, s1).astype(BF))
        f = dkp[:, :LANE] + dkp[:, LANE:]
        f = f + pltpu.roll(f, 64, 1)
        f = f + pltpu.roll(f, 32, 1)
        lane = lax.broadcasted_iota(jnp.int32, f.shape, 1)
        return jnp.concatenate(parts, axis=1), jnp.where(lane < ROPE, ropeT(f, c1, s1), 0.0)

    d_q, dk_pe = rowwise("rope_bwd", rope_bwd, S, T,
                         [(dq, HEADS * (NOPE + ROPE), 0), (dkp_heads, HEADS * ROPE, 0), (cos, LANE, 0), (sin, LANE, 0)],
                         [], [(HEADS * (NOPE + ROPE), BF), (LANE, BF)])
    d_qn = mm(d_q, W["uq"], name="mm_dqn", trans_b=True)
    d_kvn = mm(d_kv, W["ukv"], name="mm_dkvn", trans_b=True)
    G["w_uq"] = mm(sv["qn"], d_q, name="mm_gw_uq", trans_a=True, out_dtype=BF)
    G["w_ukv"] = mm(sv["kvn"], d_kv, name="mm_gw_ukv", trans_a=True, out_dtype=BF)

    def rms_bwd(ql, kvl, dqn, dkn, dkp, gq, gkv):
        res = []
        for xv, dy, g in ((ql, dqn, gq), (kvl, dkn, gkv)):
            r = lax.rsqrt(jnp.mean(xv * xv, axis=-1, keepdims=True) + RMS_EPS)
            dxh = dy * g
            res.append(((r * (dxh - xv * (r * r) * jnp.mean(dxh * xv, axis=-1, keepdims=True))).astype(BF),
                        _colsum(dy * xv * r)))
        pad = jnp.zeros((ql.shape[0], LANE), BF)
        return jnp.concatenate([res[1][0], dkp, pad, res[0][0], pad], axis=1), res[0][1], res[1][1]

    d_proj, G["q_norm_g"], G["kv_norm_g"] = rowwise(
        "rms_bwd", rms_bwd, S, T,
        [(proj, QL, OFF_Q), (proj, KVL, OFF_KV), (d_qn, QL, 0), (d_kvn, KVL, 0), (dk_pe, LANE, 0)],
        [W["gq"], W["gkv"]], [(NP - OFF_KV, BF)], [QL, KVL], into=(d_proj, NP, OFF_KV))
    du = mm(d_proj, W["in"], name="mm_du", trans_b=True, tk=1024)
    G["w_in"] = mm(sv["u"], d_proj, name="mm_gw_in", trans_a=True, out_dtype=BF)

    def mod_bwd(duv, xv, dr, a):
        return duv * (1.0 + a[1:2, :]) + dr, _colsum(duv), _colsum(duv * xv)

    dx, d_shift, d_scale = rowwise("mod_bwd", mod_bwd, S, 128, [(du, D_MODEL, 0), (x, D_MODEL, 0), (dres, D_MODEL, 0)],
                                   [ada3], [(D_MODEL, F32)], [D_MODEL] * 2)
    d_ada = jnp.concatenate([d_shift, d_scale, d_gate], axis=1)
    return dx, G, d_ada


SMALL = ("conv_a_b", "ln_a_g", "ln_a_b", "q_norm_g", "kv_norm_g", "ln_g", "ln_b")


def _rows(v):
    n = v.shape[0]
    r = -(-n // (LANE * 16)) * 16
    return jnp.pad(v, (0, r * LANE - n)).reshape(r, LANE)


def kernel(x, c, positions, w_ada, b_ada, w_in, conv_a_w, conv_a_b, ln_a_g, ln_a_b, w_a_out, conv_b_w, w_b_out, q_norm_g, kv_norm_g, w_uq, w_ukv, w_c_out, w_o, ln_g, ln_b, loss_target, m_w_ada, m_b_ada, m_w_in, m_conv_a_w, m_conv_a_b, m_ln_a_g, m_ln_a_b, m_w_a_out, m_conv_b_w, m_w_b_out, m_q_norm_g, m_kv_norm_g, m_w_uq, m_w_ukv, m_w_c_out, m_w_o, m_ln_g, m_ln_b, v_w_ada, v_b_ada, v_w_in, v_conv_a_w, v_conv_a_b, v_ln_a_g, v_ln_a_b, v_w_a_out, v_conv_b_w, v_w_b_out, v_q_norm_g, v_kv_norm_g, v_w_uq, v_w_ukv, v_w_c_out, v_w_o, v_ln_g, v_ln_b):
    P = dict(w_ada=w_ada, b_ada=b_ada, w_in=w_in, conv_a_w=conv_a_w, conv_a_b=conv_a_b, ln_a_g=ln_a_g, ln_a_b=ln_a_b,
             w_a_out=w_a_out, conv_b_w=conv_b_w, w_b_out=w_b_out, q_norm_g=q_norm_g, kv_norm_g=kv_norm_g, w_uq=w_uq,
             w_ukv=w_ukv, w_c_out=w_c_out, w_o=w_o, ln_g=ln_g, ln_b=ln_b)
    Mo = dict(w_ada=m_w_ada, b_ada=m_b_ada, w_in=m_w_in, conv_a_w=m_conv_a_w, conv_a_b=m_conv_a_b, ln_a_g=m_ln_a_g,
              ln_a_b=m_ln_a_b, w_a_out=m_w_a_out, conv_b_w=m_conv_b_w, w_b_out=m_w_b_out, q_norm_g=m_q_norm_g,
              kv_norm_g=m_kv_norm_g, w_uq=m_w_uq, w_ukv=m_w_ukv, w_c_out=m_w_c_out, w_o=m_w_o, ln_g=m_ln_g, ln_b=m_ln_b)
    Vo = dict(w_ada=v_w_ada, b_ada=v_b_ada, w_in=v_w_in, conv_a_w=v_conv_a_w, conv_a_b=v_conv_a_b, ln_a_g=v_ln_a_g,
              ln_a_b=v_ln_a_b, w_a_out=v_w_a_out, conv_b_w=v_conv_b_w, w_b_out=v_w_b_out, q_norm_g=v_q_norm_g,
              kv_norm_g=v_kv_norm_g, w_uq=v_w_uq, w_ukv=v_w_ukv, w_c_out=v_w_c_out, w_o=v_w_o, ln_g=v_ln_g, ln_b=v_ln_b)
    ORDER = ("w_ada", "b_ada", "w_in", "conv_a_w", "conv_a_b", "ln_a_g", "ln_a_b", "w_a_out", "conv_b_w", "w_b_out",
             "q_norm_g", "kv_norm_g", "w_uq", "w_ukv", "w_c_out", "w_o", "ln_g", "ln_b")
    L = w_ada.shape[0]
    S = x.shape[1]
    me = 4 * lax.axis_index("x") + 2 * lax.axis_index("y") + lax.axis_index("c")
    x2 = x[0]
    tgt = loss_target[0]

    small_in = _rows(jnp.concatenate([c.reshape(-1), conv_a_w.reshape(-1), conv_b_w.reshape(-1)]))
    w_in_b = jnp.pad(w_in.astype(BF), ((0, 0), (0, 0), (0, W_IN_PAD - IN_SHARD)))
    misc_b = jnp.concatenate([w_a_out, w_b_out, w_c_out, w_ukv, jnp.pad(w_uq, ((0, 0), (0, 0), (0, LANE - UQ_SHARD)))],
                             axis=1).astype(BF)
    w_o_b = w_o.astype(BF)
    gathered = [None] * L
    pending, _ = exchange_begin("gather0", [w_in_b[0], misc_b[0], w_o_b[0]], 3, small_in)
    sg = exchange("gather_small", [small_in], [])[0]
    sgf = sg.reshape(NDEV, -1)
    c_all = sgf[:, :D_MODEL]
    o1 = D_MODEL + L * 31 * 64
    conv_a_full = sgf[:, D_MODEL:o1].reshape(NDEV, L, 31, 64).transpose(1, 2, 0, 3).reshape(L, 31, CW)
    conv_b_full = sgf[:, o1:o1 + L * 3 * 64].reshape(NDEV, L, 3, 64).transpose(1, 2, 0, 3).reshape(L, 3, CW)

    c_act = rowwise("silu_c", _silu, 16, 16, [(jnp.pad(c_all, ((0, 8), (0, 0))), D_MODEL, 0)], [], [(D_MODEL, BF)])[0]
    ncol = w_ada.shape[2]
    w_ada_b = w_ada.astype(BF).transpose(1, 0, 2).reshape(D_MODEL, L * ncol)
    b_mine = lax.dynamic_slice_in_dim(b_ada, me * ncol, ncol, axis=1).reshape(1, L * ncol)
    ada_part = mm(c_act, w_ada_b, name="mm_ada", bias=b_mine)
    ada_rows = -(-(L * ncol) // (LANE * 8)) * 8
    ada_send = jnp.pad(ada_part[:NDEV].reshape(NDEV, -1, LANE), ((0, 0), (0, ada_rows - L * ncol // LANE), (0, 0)))
    ada_recv = exchange("a2a_ada", [], [ada_send])[0]
    ada = ada_recv[:, :L * ncol // LANE].reshape(NDEV, L, ncol).transpose(1, 0, 2).reshape(L, 3, D_MODEL)

    inv_freq = ROPE_THETA ** (-jnp.arange(0, ROPE, 2, dtype=F32) / ROPE)
    ang = positions[0].astype(F32)[:, None] * inv_freq
    tabs = (jnp.tile(jnp.cos(ang), (1, 2 * LANE // ROPE)), jnp.tile(jnp.sin(ang), (1, 2 * LANE // ROPE)))

    straight = np.arange(D_MODEL)
    fwd_in = [(0, 0, 0, 0, D_MODEL, _to_gathered(IN_PERM, IN_SHARD, W_IN_PAD))]
    fwd_misc = [(0, ROW_A, 0, 0, CW, straight), (0, ROW_B, 1, 0, CW, straight), (0, ROW_C, 2, 0, CW, straight),
                (0, ROW_UKV, 3, 0, KVL, UKV_PERM), (0, ROW_UQ, 4, 0, QL, _to_gathered(UQ_PERM, UQ_SHARD, LANE))]
    rev_in = [(0, 0, 0, 0, D_MODEL, _from_full(_inverse(IN_PERM, D_IN), IN_SHARD, W_IN_PAD))]
    rev_misc = [(0, 0, 0, ROW_A, CW, straight), (1, 0, 0, ROW_B, CW, straight), (2, 0, 0, ROW_C, CW, straight),
                (3, 0, 0, ROW_UKV, KVL, _from_full(_inverse(UKV_PERM, HEADS * (NOPE + VH)), LANE, LANE)),
                (4, 0, 0, ROW_UQ, QL, _from_full(_inverse(UQ_PERM, HEADS * (NOPE + ROPE)), UQ_SHARD, LANE))]

    def layer_weights(l, deps):
        g_in, g_misc, g_o = gathered[l]
        w_in_p = col_gather("relayout_w_in", [g_in], [(D_MODEL, NP)], fwd_in, deps)[0]
        a_out, b_out, c_out, ukv, uq = col_gather(
            "relayout_misc", [g_misc], [(CW, D_MODEL)] * 3 + [(KVL, HEADS * (NOPE + VH)), (QL, HEADS * (NOPE + ROPE))],
            fwd_misc, deps)
        return {
            "in": w_in_p, "a_out": a_out, "b_out": b_out, "c_out": c_out, "uq": uq, "ukv": ukv,
            "o": g_o.reshape(D_MODEL, D_MODEL),
            "conv_a": jnp.pad(conv_a_full[l], ((0, 1), (0, 0))), "conv_b": jnp.pad(conv_b_full[l], ((0, 5), (0, 0))),
            "vec_a": jnp.stack([conv_a_b[l], ln_a_g[l], ln_a_b[l]]), "gq": q_norm_g[l][None], "gkv": kv_norm_g[l][None],
            "lnv": jnp.stack([ln_g[l], ln_b[l]]),
        }

    h = x2
    saved, weights = [], []
    gathered[0] = exchange_end(pending, ada)
    for l in range(L):
        ada_l, deps = ada[l], ()
        if l + 1 < L:
            pending, token = exchange_begin("gather%d" % (l + 1), [w_in_b[l + 1], misc_b[l + 1], w_o_b[l + 1]], 3,
                                            gathered[l][0])
            ada_l, deps = ada_l + token[0, 0], (token,)
        Wl = layer_weights(l, deps)
        h, sv = layer_fwd(h, ada_l, Wl, tabs, S)
        if l + 1 < L:
            gathered[l + 1] = exchange_end(pending, h)
        saved.append(sv)
        weights.append(Wl)

    def loss_fn(y, t):
        e = y - t
        return e * (1.0 / D_MODEL), _colsum(e * e)

    dy, sq = rowwise("loss", loss_fn, S, 256, [(h, D_MODEL, 0), (tgt, D_MODEL, 0)], [], [(D_MODEL, F32)], [D_MODEL])
    loss = lax.psum(0.5 * jnp.sum(sq) / D_MODEL, ("x", "y", "c"))

    grads, d_adas, recv = [None] * L, [None] * L, [None] * L
    pending, token = None, None
    for l in reversed(range(L)):
        ada_l = ada[l] if token is None else ada[l] + token[0, 0]
        dy, g, d_adas[l] = layer_bwd(dy, saved[l], ada_l, weights[l], tabs, S)
        grads[l] = g
        if pending is not None:
            recv[l + 1] = exchange_end(pending, dy)
        send_in = col_gather("unrelayout_w_in", [g["w_in"]], [(NDEV, D_MODEL, W_IN_PAD)], rev_in)[0]
        send_misc = col_gather("unrelayout_misc", [g["w_a_out"], g["w_b_out"], g["w_c_out"], g["w_ukv"], g["w_uq"]],
                               [(NDEV, MISC_ROWS, LANE)], rev_misc)[0]
        send_o = g["w_o"].reshape(NDEV, D_MODEL // NDEV, D_MODEL)
        pending, token = exchange_begin("scatter%d" % l, [send_in, send_misc, send_o], 0,
                                        dy if l + 1 == L else recv[l + 1][0])
    grad_x = dy[None]

    small_parts = [jnp.stack([grads[l][n].reshape(-1) for l in range(L)]).reshape(-1) for n in SMALL]
    small_parts.append(jnp.stack([grads[l]["conv_a_w"][:31].reshape(-1) for l in range(L)]).reshape(-1))
    small_parts.append(jnp.stack([grads[l]["conv_b_w"][:3].reshape(-1) for l in range(L)]).reshape(-1))
    small_parts.append(jnp.stack([d_adas[l].reshape(-1) for l in range(L)]).reshape(-1))
    small_sizes = [int(p.shape[0]) for p in small_parts]
    gsmall = exchange("gather_small_grads", [_rows(jnp.concatenate(small_parts))], [])[0]
    gsum = sum_slots("sum_small", gsmall).reshape(-1)
    recv[0] = exchange_end(pending, gsum)
    Gr = {}
    offs = np.cumsum([0] + small_sizes)
    for i, n in enumerate(SMALL):
        Gr[n] = gsum[offs[i]:offs[i + 1]].reshape(L, -1)
    ca = gsum[offs[7]:offs[8]].reshape(L, 31, CW)
    cbw = gsum[offs[8]:offs[9]].reshape(L, 3, CW)
    Gr["conv_a_w"] = lax.dynamic_slice_in_dim(ca, me * 64, 64, axis=2)
    Gr["conv_b_w"] = lax.dynamic_slice_in_dim(cbw, me * 64, 64, axis=2)
    Gr["b_ada"] = gsum[offs[9]:offs[10]].reshape(L, 3 * D_MODEL)
    d_ada_all = gsmall.reshape(NDEV, -1)[:, offs[9]:offs[10]].reshape(NDEV, L, 3 * D_MODEL)
    d_mine = lax.dynamic_slice_in_dim(d_ada_all, me * ncol, ncol, axis=2).reshape(NDEV, L * ncol)
    g_ada = mm(c_act, jnp.pad(d_mine, ((0, 8), (0, 0))).astype(BF), name="mm_gw_ada", trans_a=True)
    Gr["w_ada"] = g_ada.reshape(D_MODEL, L, ncol).transpose(1, 0, 2)

    D, NM, NV = {}, {}, {}
    D["w_ada"], NM["w_ada"], NV["w_ada"] = adamw("adamw_w_ada", P["w_ada"], Gr["w_ada"], Mo["w_ada"], Vo["w_ada"])
    Gr["w_in"], D["w_in"], NM["w_in"], NV["w_in"] = sum_adamw(
        "sum_adamw_w_in", [recv[l][0] for l in range(L)], P["w_in"], Mo["w_in"], Vo["w_in"])
    Gr["w_o"], D["w_o"], NM["w_o"], NV["w_o"] = sum_adamw(
        "sum_adamw_w_o", [recv[l][2] for l in range(L)], P["w_o"], Mo["w_o"], Vo["w_o"])
    misc = lambda T_: jnp.concatenate([T_["w_a_out"], T_["w_b_out"], T_["w_c_out"], T_["w_ukv"],
                                       jnp.pad(T_["w_uq"], ((0, 0), (0, 0), (0, LANE - UQ_SHARD)))], axis=1)
    res = sum_adamw("sum_adamw_misc", [recv[l][1] for l in range(L)], misc(P), misc(Mo), misc(Vo))
    for T_, r in zip((Gr, D, NM, NV), res):
        T_["w_a_out"], T_["w_b_out"], T_["w_c_out"] = r[:, ROW_A:ROW_B], r[:, ROW_B:ROW_C], r[:, ROW_C:ROW_UKV]
        T_["w_ukv"], T_["w_uq"] = r[:, ROW_UKV:ROW_UQ], r[:, ROW_UQ:MISC_ROWS, :UQ_SHARD]
    packed =("b_ada", "conv_a_w", "conv_b_w") + SMALL
    pk = lambda T_: _rows(jnp.concatenate([T_[n].reshape(-1) for n in packed]))[None]
    dS, mS, vS = adamw("adamw_small", pk(P), pk(Gr), pk(Mo), pk(Vo))
    o = 0
    for n in packed:
        sz = int(np.prod(P[n].shape))
        D[n] = dS.reshape(-1)[o:o + sz].reshape(P[n].shape)
        NM[n] = mS.reshape(-1)[o:o + sz].reshape(P[n].shape)
        NV[n] = vS.reshape(-1)[o:o + sz].reshape(P[n].shape)
        o += sz
    return (loss, grad_x, *[Gr[n] for n in ORDER], *[D[n] for n in ORDER], *[NM[n] for n in ORDER],
            *[NV[n] for n in ORDER])
```

```python
import functools
import math

import numpy as np
import jax
import jax.numpy as jnp
from jax import lax
from jax.experimental import pallas as pl
from jax.experimental.pallas import tpu as pltpu

BF = jnp.bfloat16
F32 = jnp.float32
MESH = pl.DeviceIdType.MESH
NDEV = 8

HEADS, NOPE, ROPE, VH = 8, 64, 32, 64
HP = 128
ROPE_THETA = 10000.0
LN_EPS = 1e-5
RMS_EPS = 1e-6
LR, B1, B2, EPS, WD, STEP = 0.001, 0.9, 0.999, 1e-08, 0.01, 10

LANE = 128
VMEM_LIMIT = 56 * 1024 * 1024

D_MODEL, CW, QL, KVL = 1024, 512, 384, 256
OFF_M, OFF_A, OFF_AG, OFF_B, OFF_CG, OFF_GB, OFF_BG = 0, 3072, 4096, 4608, 5632, 6144, 6656
OFF_KV, OFF_KR, OFF_Q, NP = 7168, 7424, 7680, 8192
D_IN = 7840


def _cparams(**kw):
    return pltpu.CompilerParams(vmem_limit_bytes=VMEM_LIMIT, **kw)


def _sigmoid(x):
    return jax.nn.sigmoid(x)


def _silu(x):
    return x * _sigmoid(x)


def _dsilu(x):
    s = _sigmoid(x)
    return s * (1.0 + x * (1.0 - s))


def _pick_tile(n, cap, mult):
    if n <= cap:
        return n
    for t in range(cap - cap % mult, 0, -mult):
        if n % t == 0:
            return t
    raise ValueError((n, cap, mult))


def mm(a, b, *, name, trans_a=False, trans_b=False, out_dtype=F32, bias=None, tm=1024, tn=1024, tk=2048, deps=()):
    if trans_a:
        K, M = a.shape
    else:
        M, K = a.shape
    if trans_b:
        N, K2 = b.shape
    else:
        K2, N = b.shape
    assert K == K2 and not (trans_a and trans_b), (a.shape, b.shape)
    tm, tn = _pick_tile(M, tm, 16), _pick_tile(N, tn, LANE)
    tk = _pick_tile(K, tk, LANE if trans_b else 16)
    assert M % tm == 0 and N % tn == 0 and K % tk == 0, (M, N, K, tm, tn, tk)
    nk = K // tk
    dims = (((0 if trans_a else 1,), (1 if trans_b else 0,)), ((), ()))
    has_bias = bias is not None

    def body(*refs):
        a_ref, b_ref = refs[0], refs[1]
        bias_ref = refs[2] if has_bias else None
        o_ref = refs[(3 if has_bias else 2) + len(deps)]
        p = lax.dot_general(a_ref[...], b_ref[...], dims, preferred_element_type=F32)

        def finish(v):
            if has_bias:
                v = v + bias_ref[...]
            o_ref[...] = v.astype(o_ref.dtype)

        if nk == 1:
            finish(p)
        else:
            acc = refs[-1]
            k = pl.program_id(2)

            @pl.when(k == 0)
            def _():
                acc[...] = p

            @pl.when(k > 0)
            def _():
                acc[...] += p

            @pl.when(k == nk - 1)
            def _():
                finish(acc[...])

    if trans_a:
        a_spec = pl.BlockSpec((tk, tm), lambda i, j, k: (k, i))
    else:
        a_spec = pl.BlockSpec((tm, tk), lambda i, j, k: (i, k))
    if trans_b:
        b_spec = pl.BlockSpec((tn, tk), lambda i, j, k: (j, k))
    else:
        b_spec = pl.BlockSpec((tk, tn), lambda i, j, k: (k, j))
    in_specs = [a_spec, b_spec]
    args = [a, b]
    if has_bias:
        in_specs.append(pl.BlockSpec((1, tn), lambda i, j, k: (0, j)))
        args.append(bias)
    in_specs += [ANY_SPEC] * len(deps)
    args += list(deps)
    return pl.pallas_call(
        body, name=name, grid=(M // tm, N // tn, nk),
        in_specs=in_specs, out_specs=pl.BlockSpec((tm, tn), lambda i, j, k: (i, j)),
        out_shape=jax.ShapeDtypeStruct((M, N), out_dtype),
        scratch_shapes=[pltpu.VMEM((tm, tn), F32)] if nk > 1 else [],
        compiler_params=_cparams(),
    )(*args)


def rowwise(name, fn, S, T, row_ins, full_ins, row_outs, acc_outs=(), into=None):
    n_in = len(row_ins) + len(full_ins)
    n_ro, n_ao = len(row_outs), len(acc_outs)
    alias = into is not None and into[0] is not None

    def body(*refs):
        vals = [r[...] for r in refs[:n_in]]
        outs = fn(*vals)
        if not isinstance(outs, (tuple, list)):
            outs = (outs,)
        assert len(outs) == n_ro + n_ao, (name, len(outs))
        o0 = n_in + (1 if alias else 0)
        for r, v in zip(refs[o0:o0 + n_ro], outs[:n_ro]):
            r[...] = v.astype(r.dtype)
        first = pl.program_id(0) == 0
        for r, v in zip(refs[o0 + n_ro:], outs[n_ro:]):
            def init(r=r, v=v):
                r[...] = v

            def accum(r=r, v=v):
                r[...] += v

            pl.when(first)(init)
            pl.when(jnp.logical_not(first))(accum)

    in_specs, args = [], []
    for arr, W, off in row_ins:
        assert off % W == 0 and arr.shape[0] == S, (name, arr.shape, W, off)
        in_specs.append(pl.BlockSpec((T, W), functools.partial(lambda i, cb: (i, cb), cb=off // W)))
        args.append(arr)
    for arr in full_ins:
        in_specs.append(pl.BlockSpec(arr.shape, lambda i: (0, 0)))
        args.append(arr)
    out_specs = [pl.BlockSpec((T, W), lambda i: (i, 0)) for W, _ in row_outs]
    out_shape = [jax.ShapeDtypeStruct((S, W), dt) for W, dt in row_outs]
    aliases = {}
    if into is not None:
        buf, total, off = into
        W0, dt0 = row_outs[0]
        assert off % W0 == 0
        out_specs[0] = pl.BlockSpec((T, W0), functools.partial(lambda i, cb: (i, cb), cb=off // W0))
        out_shape[0] = jax.ShapeDtypeStruct((S, total), dt0)
        if alias:
            in_specs.append(ANY_SPEC)
            args.append(buf)
            aliases = {n_in: 0}
    out_specs += [pl.BlockSpec((1, W), lambda i: (0, 0)) for W in acc_outs]
    out_shape += [jax.ShapeDtypeStruct((1, W), F32) for W in acc_outs]
    return pl.pallas_call(
        body, name=name, grid=(S // T,), in_specs=in_specs, out_specs=out_specs, out_shape=out_shape,
        input_output_aliases=aliases, compiler_params=_cparams(),
    )(*args)


def _colsum(v):
    return jnp.sum(v, axis=0, keepdims=True)


def _ln_stats(r):
    mu = jnp.mean(r, axis=-1, keepdims=True)
    d = r - mu
    var = jnp.mean(d * d, axis=-1, keepdims=True)
    rstd = lax.rsqrt(var + LN_EPS)
    return d * rstd, rstd


def _ln_bwd(dn, n, rstd):
    return rstd * (dn - jnp.mean(dn, axis=-1, keepdims=True) - n * jnp.mean(dn * n, axis=-1, keepdims=True))


CPAD = 32
TC = 64


def _pre(mode, x1, x2):
    return x1 * _sigmoid(x2) if mode == "glu" else x1 * x2


def _shifted(ext, sft):
    n = TC + CPAD
    return pltpu.roll(ext, (n - sft) % n, 0)[0:TC]


def _interleaved_specs(S, off):
    return [pl.BlockSpec((S, LANE), functools.partial(lambda j, o: (0, o + 2 * j), o=off // LANE)),
            pl.BlockSpec((S, LANE), functools.partial(lambda j, o: (0, o + 2 * j + 1), o=off // LANE))]


def conv_fwd(name, src, off, w_pad, taps, mode, S, C):
    nchunk = S // TC

    def body(x1_ref, x2_ref, w_ref, o_ref, a_pad):
        a_pad[0:CPAD, :] = jnp.zeros((CPAD, LANE), F32)

        def fill(i, _):
            r = pl.multiple_of(i * 256, 256)
            a_pad[pl.ds(CPAD + r, 256), :] = _pre(mode, x1_ref[pl.ds(r, 256), :], x2_ref[pl.ds(r, 256), :])
            return 0

        lax.fori_loop(0, S // 256, fill, 0)

        def chunk(i, _):
            base = pl.multiple_of(i * TC, TC)
            ext = a_pad[pl.ds(base, TC + CPAD), :]
            acc = jnp.zeros((TC, LANE), F32)
            for k in range(taps):
                acc = acc + w_ref[pl.ds(k, 1), :] * _shifted(ext, CPAD - (taps - 1) + k)
            o_ref[pl.ds(base, TC), :] = acc
            return 0

        lax.fori_loop(0, nchunk, chunk, 0)

    kp = w_pad.shape[0]
    return pl.pallas_call(
        body, name=name, grid=(C // LANE,),
        in_specs=_interleaved_specs(S, off) + [pl.BlockSpec((kp, LANE), lambda j: (0, j))],
        out_specs=pl.BlockSpec((S, LANE), lambda j: (0, j)),
        out_shape=jax.ShapeDtypeStruct((S, C), F32),
        scratch_shapes=[pltpu.VMEM((S + CPAD, LANE), F32)],
        compiler_params=_cparams(),
    )(src, src, w_pad)


def conv_bwd(name, src, off, dc, w_pad, taps, mode, S, C, buf):
    nchunk = S // TC
    kp = w_pad.shape[0]

    def body(x1_ref, x2_ref, dc_ref, w_ref, _, d_ref, dw_ref, a_pad, dc_pad, dw_acc):
        a_pad[0:CPAD, :] = jnp.zeros((CPAD, LANE), F32)
        dc_pad[S:S + CPAD, :] = jnp.zeros((CPAD, LANE), F32)
        dw_acc[...] = jnp.zeros(dw_acc.shape, F32)

        def fill(i, _):
            r = pl.multiple_of(i * 256, 256)
            a_pad[pl.ds(CPAD + r, 256), :] = _pre(mode, x1_ref[pl.ds(r, 256), :], x2_ref[pl.ds(r, 256), :])
            dc_pad[pl.ds(r, 256), :] = dc_ref[pl.ds(r, 256), :]
            return 0

        lax.fori_loop(0, S // 256, fill, 0)

        def chunk(i, _):
            base = pl.multiple_of(i * TC, TC)
            ext_d = dc_pad[pl.ds(base, TC + CPAD), :]
            ext_a = a_pad[pl.ds(base, TC + CPAD), :]
            dcv = ext_d[0:TC]
            da = jnp.zeros((TC, LANE), F32)
            for k in range(taps):
                da = da + w_ref[pl.ds(k, 1), :] * _shifted(ext_d, taps - 1 - k)
                prod = dcv * _shifted(ext_a, CPAD - (taps - 1) + k)
                fold = prod[0:8]
                for g in range(1, TC // 8):
                    fold = fold + prod[8 * g:8 * g + 8]
                dw_acc[pl.ds(8 * k, 8), :] += fold
            x1 = x1_ref[pl.ds(base, TC), :]
            x2 = x2_ref[pl.ds(base, TC), :]
            if mode == "glu":
                s = _sigmoid(x2)
                d1, d2 = da * s, da * x1 * s * (1.0 - s)
            else:
                d1, d2 = da * x2, da * x1
            d_ref[pl.ds(base, TC), 0:LANE] = d1.astype(BF)
            d_ref[pl.ds(base, TC), LANE:2 * LANE] = d2.astype(BF)
            return 0

        lax.fori_loop(0, nchunk, chunk, 0)
        dw_ref[...] = jnp.zeros(dw_ref.shape, F32)
        for k in range(taps):
            dw_ref[pl.ds(k, 1), :] = jnp.sum(dw_acc[pl.ds(8 * k, 8), :], axis=0, keepdims=True)

    blk = pl.BlockSpec((S, LANE), lambda j: (0, j))
    return pl.pallas_call(
        body, name=name, grid=(C // LANE,),
        in_specs=_interleaved_specs(S, off) + [blk, pl.BlockSpec((kp, LANE), lambda j: (0, j)), ANY_SPEC],
        out_specs=[pl.BlockSpec((S, 2 * LANE), functools.partial(lambda j, o: (0, o + j), o=off // (2 * LANE))),
                   pl.BlockSpec((kp, LANE), lambda j: (0, j))],
        out_shape=[jax.ShapeDtypeStruct(buf.shape, BF), jax.ShapeDtypeStruct((kp, C), F32)],
        input_output_aliases={4: 0},
        scratch_shapes=[pltpu.VMEM((S + CPAD, LANE), F32), pltpu.VMEM((S + CPAD, LANE), F32),
                        pltpu.VMEM((8 * kp, LANE), F32)],
        compiler_params=_cparams(),
    )(src, src, dc, w_pad, buf)


FWD_TILES = (512, 512)
BWD_TILES = (256, 512)
QUADS = HEADS // 4
QW, KVW = 4 * (NOPE + ROPE), 4 * (NOPE + VH)
SCALE = (NOPE + ROPE) ** -0.5
NT_DIMS = (((1,), (1,)), ((), ()))
TN_DIMS = (((0,), (0,)), ((), ()))


def _lane_mask(width, group, dtype):
    lane = lax.broadcasted_iota(jnp.int32, (1, LANE), 1)
    return jnp.where(lane // width == group, 1.0, 0.0).astype(dtype)


def _visible(tq, tk, off):
    row = lax.broadcasted_iota(jnp.int32, (tq, tk), 0)
    col = lax.broadcasted_iota(jnp.int32, (tq, tk), 1)
    return col <= row + off


def _attn_tiles(S, tq, tk):
    tk = tk if S % tk == 0 else 256
    return min(tq, tk), tk


def attn_fwd(q, kv, kpe, S):
    tq, tk = _attn_tiles(S, *FWD_TILES)
    nq = S // tq

    def body(q_ref, kv_ref, kp_ref, o_ref, lse_ref):
        for t in range(2):
            cols = slice(t * LANE, (t + 1) * LANE)
            for hh in range(2):
                def q_block(qi, _, t=t, hh=hh, cols=cols):
                    r0 = pl.multiple_of(qi * tq, tq)
                    qcat = jnp.concatenate([q_ref[pl.ds(r0, tq), cols] * _lane_mask(NOPE, hh, BF),
                                            q_ref[pl.ds(r0, tq), 2 * LANE:3 * LANE] * _lane_mask(ROPE, 2 * t + hh, BF)],
                                           axis=1)
                    nfull = (qi * tq) // tk

                    def step(kj, carry, masked):
                        m, l, acc = carry
                        c0 = pl.multiple_of(kj * tk, tk)
                        kc = jnp.concatenate([kv_ref[pl.ds(c0, tk), cols], kp_ref[pl.ds(c0, tk), :]], axis=1)
                        vt = kv_ref[pl.ds(c0, tk), (2 + t) * LANE:(3 + t) * LANE]
                        s = lax.dot_general(qcat, kc, NT_DIMS, preferred_element_type=F32) * SCALE
                        if masked:
                            s = jnp.where(_visible(tq, tk, qi * tq - nfull * tk), s, -jnp.inf)
                        m_new = jnp.maximum(m, jnp.max(s, axis=-1, keepdims=True))
                        p = jnp.exp(s - m_new)
                        alpha = jnp.exp(m - m_new)
                        l = alpha * l + jnp.sum(p, axis=-1, keepdims=True)
                        acc = alpha * acc + jnp.dot(p.astype(BF), vt, preferred_element_type=F32)
                        return m_new, l, acc

                    init = (jnp.full((tq, 1), -jnp.inf, F32), jnp.zeros((tq, 1), F32), jnp.zeros((tq, LANE), F32))
                    carry = lax.fori_loop(0, nfull, lambda kj, c: step(kj, c, False), init)
                    m, l, acc = step(nfull, carry, True)
                    mine = _lane_mask(NOPE, hh, F32)
                    if hh == 0:
                        o_ref[pl.ds(r0, tq), cols] = (acc / l) * mine
                        lse_ref[pl.ds(r0, tq), cols] = (m + jnp.log(l)) * mine
                    else:
                        o_ref[pl.ds(r0, tq), cols] += (acc / l) * mine
                        lse_ref[pl.ds(r0, tq), cols] += (m + jnp.log(l)) * mine
                    return 0

                lax.fori_loop(0, nq, q_block, 0)

    return pl.pallas_call(
        body, name="attn_fwd", grid=(QUADS,),
        in_specs=[pl.BlockSpec((S, QW), lambda g: (0, g)), pl.BlockSpec((S, KVW), lambda g: (0, g)),
                  pl.BlockSpec((S, LANE), lambda g: (0, 0))],
        out_specs=[pl.BlockSpec((S, 2 * LANE), lambda g: (0, g))] * 2,
        out_shape=[jax.ShapeDtypeStruct((S, HEADS * VH), F32)] * 2,
        compiler_params=_cparams(),
    )(q, kv, kpe)


def attn_bwd(q, kv, kpe, o, lse, do, S):
    tq, tk = _attn_tiles(S, *BWD_TILES)
    nq = S // tq

    def body(q_ref, kv_ref, kp_ref, o_ref, lse_ref, do_ref, dq_ref, dkv_ref, dkp_ref, dq_acc, dk_acc, dv_acc):
        for t in range(2):
            cols = slice(t * LANE, (t + 1) * LANE)
            dk_acc[...] = jnp.zeros(dk_acc.shape, F32)
            dv_acc[...] = jnp.zeros(dv_acc.shape, F32)
            for hh in range(2):
                def q_block(qi, _, t=t, hh=hh, cols=cols):
                    r0 = pl.multiple_of(qi * tq, tq)
                    mine = _lane_mask(NOPE, hh, F32)
                    qcat = jnp.concatenate([q_ref[pl.ds(r0, tq), cols] * _lane_mask(NOPE, hh, BF),
                                            q_ref[pl.ds(r0, tq), 2 * LANE:3 * LANE] * _lane_mask(ROPE, 2 * t + hh, BF)],
                                           axis=1)
                    dof = do_ref[pl.ds(r0, tq), cols] * mine
                    dob = dof.astype(BF)
                    delta = jnp.sum(dof * o_ref[pl.ds(r0, tq), cols], axis=-1, keepdims=True)
                    lse_h = lse_ref[pl.ds(r0, tq), cols][:, hh * NOPE:hh * NOPE + 1]
                    nfull = (qi * tq) // tk
                    dq_acc[...] = jnp.zeros(dq_acc.shape, F32)

                    def step(kj, _, masked):
                        c0 = pl.multiple_of(kj * tk, tk)
                        kc = jnp.concatenate([kv_ref[pl.ds(c0, tk), cols], kp_ref[pl.ds(c0, tk), :]], axis=1)
                        vt = kv_ref[pl.ds(c0, tk), (2 + t) * LANE:(3 + t) * LANE]
                        s = lax.dot_general(qcat, kc, NT_DIMS, preferred_element_type=F32) * SCALE
                        if masked:
                            s = jnp.where(_visible(tq, tk, qi * tq - nfull * tk), s, -jnp.inf)
                        p = jnp.exp(s - lse_h)
                        dp = lax.dot_general(dob, vt, NT_DIMS, preferred_element_type=F32)
                        ds = (p * (dp - delta) * SCALE).astype(BF)
                        dv_acc[pl.ds(c0, tk), :] += lax.dot_general(p.astype(BF), dob, TN_DIMS,
                                                                    preferred_element_type=F32)
                        dk_acc[pl.ds(c0, tk), :] += lax.dot_general(ds, qcat, TN_DIMS, preferred_element_type=F32)
                        dq_acc[...] += jnp.dot(ds, kc, preferred_element_type=F32)
                        return 0

                    lax.fori_loop(0, nfull, lambda kj, c: step(kj, c, False), 0)
                    step(nfull, 0, True)
                    d = dq_acc[...]
                    pe = d[:, LANE:] * _lane_mask(ROPE, 2 * t + hh, F32)
                    if hh == 0:
                        dq_ref[pl.ds(r0, tq), cols] = d[:, :LANE] * mine
                    else:
                        dq_ref[pl.ds(r0, tq), cols] += d[:, :LANE] * mine
                    if t == 0 and hh == 0:
                        dq_ref[pl.ds(r0, tq), 2 * LANE:3 * LANE] = pe
                    else:
                        dq_ref[pl.ds(r0, tq), 2 * LANE:3 * LANE] += pe
                    return 0

                lax.fori_loop(0, nq, q_block, 0)
            dkv_ref[:, t * LANE:(t + 1) * LANE] = dk_acc[:, :LANE].astype(BF)
            dkv_ref[:, (2 + t) * LANE:(3 + t) * LANE] = dv_acc[...].astype(BF)
            if t == 0:
                dkp_ref[...] = dk_acc[:, LANE:]
            else:
                dkp_ref[...] += dk_acc[:, LANE:]

    qspec = pl.BlockSpec((S, QW), lambda g: (0, g))
    kvspec = pl.BlockSpec((S, KVW), lambda g: (0, g))
    ospec = pl.BlockSpec((S, 2 * LANE), lambda g: (0, g))
    return pl.pallas_call(
        body, name="attn_bwd", grid=(QUADS,),
        in_specs=[qspec, kvspec, pl.BlockSpec((S, LANE), lambda g: (0, 0)), ospec, ospec, ospec],
        out_specs=[qspec, kvspec, pl.BlockSpec((S, LANE), lambda g: (0, g))],
        out_shape=[jax.ShapeDtypeStruct((S, HEADS * (NOPE + ROPE)), F32), jax.ShapeDtypeStruct((S, HEADS * (NOPE + VH)), BF),
                   jax.ShapeDtypeStruct((S, HEADS * ROPE), F32)],
        scratch_shapes=[pltpu.VMEM((tq, 2 * LANE), F32), pltpu.VMEM((S, 2 * LANE), F32), pltpu.VMEM((S, LANE), F32)],
        compiler_params=_cparams(),
    )(q, kv, kpe, o, lse, do)


def exchange(name, gathers, a2as):
    n_g, n = len(gathers), len(gathers) + len(a2as)

    def body(*refs):
        ins, outs = refs[:n], refs[n:2 * n]
        send_sems, recv_sems, loc_sems = refs[2 * n:]
        x, y, c = lax.axis_index("x"), lax.axis_index("y"), lax.axis_index("c")
        me = 4 * x + 2 * y + c

        def peer(k):
            px = 1 - x if k & 4 else x
            py = 1 - y if k & 2 else y
            pc = 1 - c if k & 1 else c
            return (px, py, pc), 4 * px + 2 * py + pc

        def remote(a, k):
            pid, pflat = peer(k)
            src = ins[a] if a < n_g else ins[a].at[pflat]
            return pltpu.make_async_remote_copy(
                src_ref=src, dst_ref=outs[a].at[me], send_sem=send_sems.at[a, k - 1], recv_sem=recv_sems.at[a, k - 1],
                device_id=pid, device_id_type=MESH)

        def arrival(a, k):
            pid, pflat = peer(k)
            src = ins[a] if a < n_g else ins[a].at[pflat]
            return pltpu.make_async_remote_copy(
                src_ref=src, dst_ref=outs[a].at[pflat], send_sem=send_sems.at[a, k - 1], recv_sem=recv_sems.at[a, k - 1],
                device_id=pid, device_id_type=MESH)

        local = []
        for a in range(n):
            own = ins[a] if a < n_g else ins[a].at[me]
            cp = pltpu.make_async_copy(own, outs[a].at[me], loc_sems.at[a])
            cp.start()
            local.append(cp)
        sent = []
        for k in (1, 2, 4, 3, 5, 6, 7):
            for a in range(n):
                cp = remote(a, k)
                cp.start()
                sent.append(cp)
        for k in range(1, 8):
            for a in range(n):
                arrival(a, k).wait_recv()
        for cp in sent:
            cp.wait_send()
        for cp in local:
            cp.wait()

    out_shape = [jax.ShapeDtypeStruct((NDEV,) + g.shape, g.dtype) for g in gathers]
    out_shape += [jax.ShapeDtypeStruct(a.shape, a.dtype) for a in a2as]
    any_spec = pl.BlockSpec(memory_space=pl.ANY)
    return pl.pallas_call(
        body, name=name, in_specs=[any_spec] * n, out_specs=[any_spec] * n, out_shape=out_shape,
        scratch_shapes=[pltpu.SemaphoreType.DMA((n, NDEV - 1)), pltpu.SemaphoreType.DMA((n, NDEV - 1)),
                        pltpu.SemaphoreType.DMA((n,))],
    )(*gathers, *a2as)


def _peer(k, x, y, c):
    px = 1 - x if k & 4 else x
    py = 1 - y if k & 2 else y
    pc = 1 - c if k & 1 else c
    return (px, py, pc), 4 * px + 2 * py + pc


PEER_ORDER = (1, 2, 4, 3, 5, 6, 7)
HBM_SPEC = pl.BlockSpec(memory_space=pltpu.HBM)
SEM_SPEC = pl.BlockSpec(memory_space=pltpu.SEMAPHORE)
ANY_SPEC = pl.BlockSpec(memory_space=pl.ANY)


def _split_copies(ins, lands, n_g, send_sems, recv_sems):
    x, y, c = lax.axis_index("x"), lax.axis_index("y"), lax.axis_index("c")
    me = 4 * x + 2 * y + c

    def outgoing(a, k):
        pid, pflat = _peer(k, x, y, c)
        src = ins[a] if a < n_g else ins[a].at[pflat]
        return pltpu.make_async_remote_copy(
            src_ref=src, dst_ref=lands[a].at[me], send_sem=send_sems.at[a * (NDEV - 1) + k - 1],
            recv_sem=recv_sems.at[a * (NDEV - 1) + k - 1],
            device_id=pid, device_id_type=MESH)

    def arrival(a, k):
        pid, pflat = _peer(k, x, y, c)
        src = ins[a] if a < n_g else ins[a].at[pflat]
        return pltpu.make_async_remote_copy(
            src_ref=src, dst_ref=lands[a].at[pflat], send_sem=send_sems.at[a * (NDEV - 1) + k - 1],
            recv_sem=recv_sems.at[a * (NDEV - 1) + k - 1],
            device_id=pid, device_id_type=MESH)

    return outgoing, arrival


def exchange_begin(name, srcs, n_g, dep):
    n = len(srcs)
    land_shapes = [((NDEV,) + s.shape) if a < n_g else s.shape for a, s in enumerate(srcs)]

    def own_body(*refs):
        ins, outs = refs[:n], refs[n + 1:2 * n + 1]
        stage, sems = refs[2 * n + 1:3 * n + 1], refs[-1]
        me = 4 * lax.axis_index("x") + 2 * lax.axis_index("y") + lax.axis_index("c")
        cps = [pltpu.make_async_copy(ins[a] if a < n_g else ins[a].at[me], stage[a], sems.at[a]) for a in range(n)]
        for cp in cps:
            cp.start()
        for cp in cps:
            cp.wait()
        cps = [pltpu.make_async_copy(stage[a], outs[a].at[me], sems.at[a]) for a in range(n)]
        for cp in cps:
            cp.start()
        for cp in cps:
            cp.wait()

    lands = pl.pallas_call(
        own_body, name=name + "_own", in_specs=[ANY_SPEC] * (n + 1), out_specs=[ANY_SPEC] * n,
        out_shape=[jax.ShapeDtypeStruct(sh, s.dtype) for sh, s in zip(land_shapes, srcs)],
        scratch_shapes=[pltpu.VMEM(sh[1:], s.dtype) for sh, s in zip(land_shapes, srcs)] + [pltpu.SemaphoreType.DMA((n,))],
        compiler_params=_cparams(),
    )(*srcs, dep)

    def start_body(*refs):
        ins, lz = refs[:n], refs[n:2 * n]
        send_sems, recv_sems, token = refs[2 * n], refs[2 * n + 1], refs[-1]
        outgoing, _ = _split_copies(ins, lz, n_g, send_sems, recv_sems)
        for k in PEER_ORDER:
            for a in range(n):
                outgoing(a, k).start()
        token[...] = jnp.zeros(token.shape, F32)

    hbm = lambda t: pltpu.HBM(t.shape, t.dtype)
    res = pl.pallas_call(
        start_body, name=name + "_start",
        out_shape=(pltpu.SemaphoreType.DMA((n * (NDEV - 1),)), pltpu.SemaphoreType.DMA((n * (NDEV - 1),)),
                   *[hbm(s) for s in srcs], *[hbm(t) for t in lands], jax.ShapeDtypeStruct((8, LANE), F32)),
        in_specs=[HBM_SPEC] * (2 * n),
        out_specs=(SEM_SPEC, SEM_SPEC, *[HBM_SPEC] * (2 * n), pl.BlockSpec(memory_space=pltpu.VMEM)),
        input_output_aliases={i: 2 + i for i in range(2 * n)},
        compiler_params=pltpu.CompilerParams(has_side_effects=pltpu.SideEffectType.DATAFLOW_SIDE_EFFECTING),
    )(*[pltpu.with_memory_space_constraint(t, pltpu.HBM) for t in list(srcs) + list(lands)])
    return (name, n, n_g, res[:-1]), res[-1]


def exchange_end(handle, after):
    name, n, n_g, (send_sems, recv_sems, *bufs) = handle

    def wait_body(*refs):
        ins, lz = refs[:n], refs[n:2 * n]
        ss, rs = refs[2 * n], refs[2 * n + 1]
        outgoing, arrival = _split_copies(ins, lz, n_g, ss, rs)
        for k in range(1, NDEV):
            for a in range(n):
                arrival(a, k).wait_recv()
        for k in range(1, NDEV):
            for a in range(n):
                outgoing(a, k).wait_send()

    res = pl.pallas_call(
        wait_body, name=name + "_wait", out_shape=tuple(pltpu.HBM(t.shape, t.dtype) for t in bufs),
        in_specs=[HBM_SPEC] * (2 * n) + [SEM_SPEC, SEM_SPEC, ANY_SPEC], out_specs=[HBM_SPEC] * (2 * n),
        input_output_aliases={i: i for i in range(2 * n)},
        compiler_params=pltpu.CompilerParams(has_side_effects=pltpu.SideEffectType.DATAFLOW_SIDE_EFFECTING),
    )(*bufs, send_sems, recv_sems, after)
    return list(res[n:])


def _pick_rows(R, mult, cap):
    best = None
    for n in range(1, R + 1):
        if R % n == 0 and (R // n) % mult == 0 and R // n <= cap:
            best = R // n
            break
    assert best is not None, (R, mult, cap)
    return best


def sum_slots(name, x):
    _, R, _ = x.shape
    tr = _pick_rows(R, 16, 2304)

    def body(x_ref, o_ref):
        acc = x_ref[0].astype(F32)
        for d in range(1, NDEV):
            acc = acc + x_ref[d].astype(F32)
        o_ref[...] = acc

    return pl.pallas_call(
        body, name=name, grid=(R // tr,),
        in_specs=[pl.BlockSpec((NDEV, tr, LANE), lambda i: (0, i, 0))],
        out_specs=pl.BlockSpec((tr, LANE), lambda i: (i, 0)),
        out_shape=jax.ShapeDtypeStruct((R, LANE), F32), compiler_params=_cparams(),
    )(x)


def adamw(name, w, g, m, v):
    L, R, C = w.shape
    tr = _pick_rows(R, 8, 256) if R % 8 == 0 else R

    def body(w_ref, g_ref, m_ref, v_ref, d_ref, nm_ref, nv_ref):
        gg = g_ref[...]
        nm = B1 * m_ref[...] + (1.0 - B1) * gg
        nv = B2 * v_ref[...] + (1.0 - B2) * jnp.square(gg)
        m_hat = nm / (1.0 - B1 ** STEP)
        v_hat = nv / (1.0 - B2 ** STEP)
        d_ref[...] = -LR * (m_hat / (jnp.sqrt(v_hat) + EPS) + WD * w_ref[...])
        nm_ref[...] = nm
        nv_ref[...] = nv

    blk = pl.BlockSpec((1, tr, C), lambda l, i: (l, i, 0))
    shp = jax.ShapeDtypeStruct(w.shape, F32)
    return pl.pallas_call(
        body, name=name, grid=(L, R // tr), in_specs=[blk] * 4, out_specs=[blk] * 3, out_shape=[shp] * 3,
        compiler_params=_cparams(),
    )(w, g, m, v)


IN_SHARD = D_IN // NDEV
UQ_SHARD = HEADS * (NOPE + ROPE) // NDEV
W_IN_PAD = 1024
ROW_A, ROW_B, ROW_C, ROW_UKV, ROW_UQ, MISC_ROWS = 0, 512, 1024, 1536, 1792, 2176


def _in_perm_index():
    ar = np.arange
    z = lambda n: np.full((n,), -1, np.int64)
    mix = lambda lo1, lo2: np.concatenate([ar(lo + LANE * j, lo + LANE * (j + 1)) for j in range(CW // LANE)
                                           for lo in (lo1, lo2)])
    return np.concatenate([ar(4768, 7840), mix(0, 512), ar(1024, 1536), mix(1536, 2560), ar(4256, 4768), ar(2048, 2560),
                           ar(3072, 3584), ar(3968, 4224), ar(4224, 4256), z(OFF_Q - OFF_KR - ROPE), ar(3584, 3968),
                           z(NP - OFF_Q - QL)])


def _head_perm_index(a, b):
    parts = []
    for g in range(QUADS):
        h = np.arange(4 * g, 4 * g + 4)[:, None] * (a + b)
        parts += [(h + np.arange(a)[None]).reshape(-1), (h + a + np.arange(b)[None]).reshape(-1)]
    return np.concatenate(parts)


def _inverse(perm, n):
    inv = np.full((n,), -1, np.int64)
    inv[perm[perm >= 0]] = np.nonzero(perm >= 0)[0]
    return inv


IN_PERM = _in_perm_index()
UQ_PERM = _head_perm_index(NOPE, ROPE)
UKV_PERM = _head_perm_index(NOPE, VH)


def _to_gathered(perm, shard, pad):
    return np.where(perm >= 0, (perm // shard) * pad + perm % shard, -1)


def _from_full(inv, shard, pad):
    j, i = np.divmod(np.arange(NDEV * pad), pad)
    return np.where(i < shard, inv[np.minimum(j * shard + i, inv.shape[0] - 1)], -1)


def col_gather(name, srcs, out_shapes, jobs, deps=()):
    ns, nj, nd = len(srcs), len(jobs), len(deps)
    tables = [jnp.asarray(np.asarray(job[5], np.int32)[None, :]) for job in jobs]

    def view(ref, col0, width, r0, rc):
        n = ref.shape[-1]
        if len(ref.shape) == 3:
            return ref.at[col0 // n, pl.ds(r0, rc), pl.ds(col0 % n, width)]
        return ref.at[pl.ds(r0, rc), pl.ds(col0, width)]

    def body(*refs):
        src_refs, tab_refs, out_refs = refs[:ns], refs[ns:ns + nj], refs[ns + nj + nd:]
        for ji, (si, srow, oi, orow, nrows, tgt) in enumerate(jobs):
            sref, oref = src_refs[si], out_refs[oi]
            tgt = np.asarray(tgt)
            tw = 256 if oref.shape[-1] % 256 == 0 else LANE
            rc = 256 if nrows % 256 == 0 else LANE
            for t in range(tgt.shape[0] // tw):
                tt = tgt[t * tw:(t + 1) * tw]
                tiles = sorted(set((tt[tt >= 0] // LANE).tolist()))
                straight = bool(tiles) and np.array_equal(tt, np.arange(tiles[0] * LANE, tiles[0] * LANE + tw))
                onehots = []
                if tiles and not straight:
                    want = tab_refs[ji][:, t * tw:(t + 1) * tw]
                    row = lax.broadcasted_iota(jnp.int32, (LANE, tw), 0)
                    onehots = [jnp.where(want == row + s * LANE, 1.0, 0.0).astype(BF) for s in tiles]

                def chunk(ci, _, t=t, tiles=tiles, straight=straight, onehots=onehots):
                    r0 = ci * rc
                    dst = view(oref, t * tw, tw, pl.multiple_of(orow + r0, LANE), rc)
                    rs = pl.multiple_of(srow + r0, LANE)
                    if not tiles:
                        dst[...] = jnp.zeros((rc, tw), BF)
                    elif straight:
                        for k in range(tw // LANE):
                            view(oref, t * tw + k * LANE, LANE, pl.multiple_of(orow + r0, LANE), rc)[...] = (
                                view(sref, (tiles[0] + k) * LANE, LANE, rs, rc)[...])
                    else:
                        acc = None
                        for s, oh in zip(tiles, onehots):
                            p = jnp.dot(view(sref, s * LANE, LANE, rs, rc)[...], oh, preferred_element_type=F32)
                            acc = p if acc is None else acc + p
                        dst[...] = acc.astype(BF)
                    return 0

                lax.fori_loop(0, nrows // rc, chunk, 0)

    vmem = pl.BlockSpec(memory_space=pltpu.VMEM)
    return pl.pallas_call(
        body, name=name, in_specs=[vmem] * (ns + nj) + [ANY_SPEC] * nd, out_specs=[vmem] * len(out_shapes),
        out_shape=[jax.ShapeDtypeStruct(s, BF) for s in out_shapes], compiler_params=_cparams(),
    )(*srcs, *tables, *deps)


def sum_adamw(name, recvs, w, m, v):
    L, R, C = w.shape
    CP = recvs[0].shape[-1]
    tr = _pick_rows(R, 16, 128)

    def body(*refs):
        r_refs = refs[:L]
        w_ref, m_ref, v_ref, g_ref, d_ref, nm_ref, nv_ref, gsum = refs[L:]
        layer = pl.program_id(0)
        for k in range(L):
            def total(k=k):
                acc = r_refs[k][0].astype(F32)
                for d in range(1, NDEV):
                    acc = acc + r_refs[k][d].astype(F32)
                gsum[...] = acc
            pl.when(layer == k)(total)
        gg = gsum[:, 0:C]
        nm = B1 * m_ref[...] + (1.0 - B1) * gg
        nv = B2 * v_ref[...] + (1.0 - B2) * jnp.square(gg)
        m_hat = nm / (1.0 - B1 ** STEP)
        v_hat = nv / (1.0 - B2 ** STEP)
        g_ref[...] = gg
        d_ref[...] = -LR * (m_hat / (jnp.sqrt(v_hat) + EPS) + WD * w_ref[...])
        nm_ref[...] = nm
        nv_ref[...] = nv

    r_specs = [pl.BlockSpec((NDEV, tr, CP), functools.partial(lambda l, i, k: (0, jnp.where(l == k, i, 0), 0), k=k))
               for k in range(L)]
    blk = pl.BlockSpec((None, tr, C), lambda l, i: (l, i, 0))
    shp = jax.ShapeDtypeStruct(w.shape, F32)
    return pl.pallas_call(
        body, name=name, grid=(L, R // tr), in_specs=r_specs + [blk] * 3, out_specs=[blk] * 4, out_shape=[shp] * 4,
        scratch_shapes=[pltpu.VMEM((tr, CP), F32)], compiler_params=_cparams(),
    )(*recvs, w, m, v)


ALPHA = 8.0 ** 0.25


def _rope_fn(sign):
    def fn(x, cos, sin):
        W = x.shape[-1]
        lane = lax.broadcasted_iota(jnp.int32, x.shape, 1)
        first_half = (lane % ROPE) < (ROPE // 2)
        rot = jnp.where(first_half, -pltpu.roll(x, W - ROPE // 2, 1), pltpu.roll(x, ROPE // 2, 1))
        return x * cos + sign * rot * sin
    return fn


def layer_fwd(x, ada3, W, tabs, S):
    cos, sin = tabs
    T = 256
    u = rowwise("modulate", lambda xv, a: xv * (1.0 + a[1:2, :]) + a[0:1, :], S, T,
                [(x, D_MODEL, 0)], [ada3], [(D_MODEL, BF)])[0]
    proj = mm(u, W["in"], name="mm_proj", tm=1024, tn=1024)
    W = {**W, **W["late"](proj)}

    ca = conv_fwd("conv_a_fwd", proj, OFF_A, W["conv_a"], 31, "glu", S, CW)

    def a_post(c, ag, vec):
        n, _ = _ln_stats(c + vec[0:1, :])
        return _silu(n * vec[1:2, :] + vec[2:3, :]) * _silu(ag)

    h_a = rowwise("mix_a_post", a_post, S, T, [(ca, CW, 0), (proj, CW, OFF_AG)], [W["vec_a"]], [(CW, BF)])[0]
    y_a = mm(h_a, W["a_out"], name="mm_branch_out")

    cb = conv_fwd("conv_b_fwd", proj, OFF_B, W["conv_b"], 3, "mul", S, CW)
    h_b = rowwise("mix_b_post", lambda c, gb, bg: gb * c * _silu(bg), S, T,
                  [(cb, CW, 0), (proj, CW, OFF_GB), (proj, CW, OFF_BG)], [], [(CW, BF)])[0]
    y_b = mm(h_b, W["b_out"], name="mm_branch_out")

    def rms2(ql, kvl, gq, gkv):
        rq = lax.rsqrt(jnp.mean(ql * ql, axis=-1, keepdims=True) + RMS_EPS)
        rk = lax.rsqrt(jnp.mean(kvl * kvl, axis=-1, keepdims=True) + RMS_EPS)
        return ql * rq * gq, kvl * rk * gkv

    qn, kvn = rowwise("rms_fwd", rms2, S, T, [(proj, QL, OFF_Q), (proj, KVL, OFF_KV)], [W["gq"], W["gkv"]],
                      [(QL, BF), (KVL, BF)])
    q = mm(qn, W["uq"], name="mm_q")
    kv = mm(kvn, W["ukv"], name="mm_kv", out_dtype=BF)
    rope = _rope_fn(1.0)

    def rope_fwd(qv, kr, c1, s1):
        parts = []
        for g in range(QUADS):
            parts.append(qv[:, g * QW:g * QW + 2 * LANE].astype(BF))
            parts.append(rope(qv[:, g * QW + 2 * LANE:(g + 1) * QW], c1, s1).astype(BF))
        kp = rope(kr, c1, s1)
        kp = kp + pltpu.roll(kp, ROPE, 1) + pltpu.roll(kp, 2 * ROPE, 1) + pltpu.roll(kp, 3 * ROPE, 1)
        return jnp.concatenate(parts, axis=1), kp

    q_b, kpe = rowwise("rope_fwd", rope_fwd, S, T,
                       [(q, HEADS * (NOPE + ROPE), 0), (proj, LANE, OFF_KR), (cos, LANE, 0), (sin, LANE, 0)], [],
                       [(HEADS * (NOPE + ROPE), BF), (LANE, BF)])
    o, lse = attn_fwd(q_b, kv, kpe, S)
    h_c = rowwise("mix_c_post", lambda ov, cg: ov * _silu(cg), S, T, [(o, CW, 0), (proj, CW, OFF_CG)], [],
                  [(CW, BF)])[0]
    y_c = mm(h_c, W["c_out"], name="mm_branch_out")

    def merge(la, lb, lc, ya, yb, yc):
        return _sigmoid(la) * ya + _sigmoid(lb) * yb + _sigmoid(lc) * yc

    m = rowwise("merge_fwd", merge, S, 128,
                [(proj, D_MODEL, 0), (proj, D_MODEL, 1024), (proj, D_MODEL, 2048), (y_a, D_MODEL, 0),
                 (y_b, D_MODEL, 0), (y_c, D_MODEL, 0)], [], [(D_MODEL, BF)])[0]
    out = mm(m, W["o"], name="mm_out")

    def ln_fwd(xv, ov, a, lnv):
        n, _ = _ln_stats(ALPHA * xv + a[2:3, :] * ov)
        return n * lnv[0:1, :] + lnv[1:2, :]

    x_next = rowwise("ln_fwd", ln_fwd, S, 128, [(x, D_MODEL, 0), (out, D_MODEL, 0)], [ada3, W["lnv"]],
                     [(D_MODEL, F32)])[0]
    saved = dict(x=x, u=u, proj=proj, ca=ca, cb=cb, h_a=h_a, h_b=h_b, h_c=h_c, y_a=y_a, y_b=y_b, y_c=y_c, qn=qn,
                 kvn=kvn, q_b=q_b, kv=kv, kpe=kpe, lse=lse, o=o, m=m, out=out)
    return x_next, saved, W


def layer_bwd(dxn, sv, ada3, W, tabs, S, before_in=None):
    cos, sin = tabs
    T = 256
    x, proj = sv["x"], sv["proj"]
    G = {}

    def ln_bwd(xv, ov, dy, a, lnv):
        gate = a[2:3, :]
        n, rstd = _ln_stats(ALPHA * xv + gate * ov)
        dr = _ln_bwd(dy * lnv[0:1, :], n, rstd)
        return ALPHA * dr, gate * dr, _colsum(dy * n), _colsum(dy), _colsum(dr * ov)

    dres, d_out, G["ln_g"], G["ln_b"], d_gate = rowwise(
        "ln_bwd", ln_bwd, S, 128, [(x, D_MODEL, 0), (sv["out"], D_MODEL, 0), (dxn, D_MODEL, 0)], [ada3, W["lnv"]],
        [(D_MODEL, F32), (D_MODEL, BF)], [D_MODEL] * 3)
    dm = mm(d_out, W["o"], name="mm_dm", trans_b=True)
    G["w_o"] = mm(sv["m"], d_out, name="mm_gw_o", trans_a=True, out_dtype=BF)

    def merge_bwd(dmv, la, lb, lc, ya, yb, yc):
        outs, dls = [], []
        for lg, yv in ((la, ya), (lb, yb), (lc, yc)):
            s = _sigmoid(lg)
            outs.append(dmv * s)
            dls.append((dmv * yv * s * (1.0 - s)).astype(BF))
        return (jnp.concatenate(dls, axis=1),) + tuple(outs)

    d_proj, dy_a, dy_b, dy_c = rowwise(
        "merge_bwd", merge_bwd, S, 128,
        [(dm, D_MODEL, 0), (proj, D_MODEL, 0), (proj, D_MODEL, 1024), (proj, D_MODEL, 2048), (sv["y_a"], D_MODEL, 0),
         (sv["y_b"], D_MODEL, 0), (sv["y_c"], D_MODEL, 0)], [], [(3 * D_MODEL, BF)] + [(D_MODEL, BF)] * 3,
        into=(None, NP, OFF_M))

    dh = {}
    for br, dy in (("a", dy_a), ("b", dy_b), ("c", dy_c)):
        dh[br] = mm(dy, W[br + "_out"], name="mm_dh", trans_b=True)
        G["w_%s_out" % br] = mm(sv["h_" + br], dy, name="mm_gw_branch", trans_a=True, out_dtype=BF)

    def a_post_bwd(c, ag, dhv, vec):
        n, rstd = _ln_stats(c + vec[0:1, :])
        z = n * vec[1:2, :] + vec[2:3, :]
        d_ag = dhv * _silu(z) * _dsilu(ag)
        dz = dhv * _silu(ag) * _dsilu(z)
        dc = _ln_bwd(dz * vec[1:2, :], n, rstd)
        return d_ag, dc, _colsum(dc), _colsum(dz * n), _colsum(dz)

    d_proj, dca, G["conv_a_b"], G["ln_a_g"], G["ln_a_b"] = rowwise(
        "mix_a_post_bwd", a_post_bwd, S, T, [(sv["ca"], CW, 0), (proj, CW, OFF_AG), (dh["a"], CW, 0)], [W["vec_a"]],
        [(CW, BF), (CW, F32)], [CW] * 3, into=(d_proj, NP, OFF_AG))
    d_proj, G["conv_a_w"] = conv_bwd("conv_a_bwd", proj, OFF_A, dca, W["conv_a"], 31, "glu", S, CW, d_proj)

    def b_post_bwd(c, gb, bg, dhv):
        sg = _silu(bg)
        d_gb_bg = jnp.concatenate([(dhv * sg * c).astype(BF), (dhv * gb * c * _dsilu(bg)).astype(BF)], axis=1)
        return d_gb_bg, dhv * sg * gb

    d_proj, dcb = rowwise("mix_b_post_bwd", b_post_bwd, S, T,
                          [(sv["cb"], CW, 0), (proj, CW, OFF_GB), (proj, CW, OFF_BG), (dh["b"], CW, 0)], [],
                          [(2 * CW, BF), (CW, F32)], into=(d_proj, NP, OFF_GB))
    d_proj, G["conv_b_w"] = conv_bwd("conv_b_bwd", proj, OFF_B, dcb, W["conv_b"], 3, "mul", S, CW, d_proj)

    d_proj, d_o = rowwise("mix_c_post_bwd", lambda ov, cg, dhv: (dhv * ov * _dsilu(cg), dhv * _silu(cg)), S, T,
                          [(sv["o"], CW, 0), (proj, CW, OFF_CG), (dh["c"], CW, 0)], [], [(CW, BF), (CW, F32)],
                          into=(d_proj, NP, OFF_CG))
    dq, d_kv, dkp_heads = attn_bwd(sv["q_b"], sv["kv"], sv["kpe"], sv["o"], sv["lse"], d_o, S)
    ropeT = _rope_fn(-1.0)

    def rope_bwd(dqv, dkp, c1, s1):
        parts = []
        for g in range(QUADS):
            parts.append(dqv[:, g * QW:g * QW + 2 * LANE].astype(BF))
            parts.append(ropeT(dqv[:, g * QW + 2 * LANE:(g + 1) * QW], c1, s1).astype(BF))
        f = dkp[:, :LANE] + dkp[:, LANE:]
        f = f + pltpu.roll(f, 64, 1)
        f = f + pltpu.roll(f, 32, 1)
        lane = lax.broadcasted_iota(jnp.int32, f.shape, 1)
        return jnp.concatenate(parts, axis=1), jnp.where(lane < ROPE, ropeT(f, c1, s1), 0.0)

    d_q, dk_pe = rowwise("rope_bwd", rope_bwd, S, T,
                         [(dq, HEADS * (NOPE + ROPE), 0), (dkp_heads, HEADS * ROPE, 0), (cos, LANE, 0), (sin, LANE, 0)],
                         [], [(HEADS * (NOPE + ROPE), BF), (LANE, BF)])
    d_qn = mm(d_q, W["uq"], name="mm_dqn", trans_b=True)
    d_kvn = mm(d_kv, W["ukv"], name="mm_dkvn", trans_b=True)
    G["w_uq"] = mm(sv["qn"], d_q, name="mm_gw_uq", trans_a=True, out_dtype=BF)
    G["w_ukv"] = mm(sv["kvn"], d_kv, name="mm_gw_ukv", trans_a=True, out_dtype=BF)

    def rms_bwd(ql, kvl, dqn, dkn, dkp, gq, gkv):
        res = []
        for xv, dy, g in ((ql, dqn, gq), (kvl, dkn, gkv)):
            r = lax.rsqrt(jnp.mean(xv * xv, axis=-1, keepdims=True) + RMS_EPS)
            dxh = dy * g
            res.append(((r * (dxh - xv * (r * r) * jnp.mean(dxh * xv, axis=-1, keepdims=True))).astype(BF),
                        _colsum(dy * xv * r)))
        pad = jnp.zeros((ql.shape[0], LANE), BF)
        return jnp.concatenate([res[1][0], dkp, pad, res[0][0], pad], axis=1), res[0][1], res[1][1]

    d_proj, G["q_norm_g"], G["kv_norm_g"] = rowwise(
        "rms_bwd", rms_bwd, S, T,
        [(proj, QL, OFF_Q), (proj, KVL, OFF_KV), (d_qn, QL, 0), (d_kvn, KVL, 0), (dk_pe, LANE, 0)],
        [W["gq"], W["gkv"]], [(NP - OFF_KV, BF)], [QL, KVL], into=(d_proj, NP, OFF_KV))
    deps = before_in(G) if before_in is not None else ()
    du = mm(d_proj, W["in"], name="mm_du", trans_b=True, tk=1024, deps=deps)
    G["w_in"] = mm(sv["u"], d_proj, name="mm_gw_in", trans_a=True, out_dtype=BF, deps=deps)

    def mod_bwd(duv, xv, dr, a):
        return duv * (1.0 + a[1:2, :]) + dr, _colsum(duv), _colsum(duv * xv)

    dx, d_shift, d_scale = rowwise("mod_bwd", mod_bwd, S, 128, [(du, D_MODEL, 0), (x, D_MODEL, 0), (dres, D_MODEL, 0)],
                                   [ada3], [(D_MODEL, F32)], [D_MODEL] * 2)
    d_ada = jnp.concatenate([d_shift, d_scale, d_gate], axis=1)
    return dx, G, d_ada


SMALL = ("conv_a_b", "ln_a_g", "ln_a_b", "q_norm_g", "kv_norm_g", "ln_g", "ln_b")


def _rows(v):
    n = v.shape[0]
    r = -(-n // (LANE * 16)) * 16
    return jnp.pad(v, (0, r * LANE - n)).reshape(r, LANE)


def kernel(x, c, positions, w_ada, b_ada, w_in, conv_a_w, conv_a_b, ln_a_g, ln_a_b, w_a_out, conv_b_w, w_b_out, q_norm_g, kv_norm_g, w_uq, w_ukv, w_c_out, w_o, ln_g, ln_b, loss_target, m_w_ada, m_b_ada, m_w_in, m_conv_a_w, m_conv_a_b, m_ln_a_g, m_ln_a_b, m_w_a_out, m_conv_b_w, m_w_b_out, m_q_norm_g, m_kv_norm_g, m_w_uq, m_w_ukv, m_w_c_out, m_w_o, m_ln_g, m_ln_b, v_w_ada, v_b_ada, v_w_in, v_conv_a_w, v_conv_a_b, v_ln_a_g, v_ln_a_b, v_w_a_out, v_conv_b_w, v_w_b_out, v_q_norm_g, v_kv_norm_g, v_w_uq, v_w_ukv, v_w_c_out, v_w_o, v_ln_g, v_ln_b):
    P = dict(w_ada=w_ada, b_ada=b_ada, w_in=w_in, conv_a_w=conv_a_w, conv_a_b=conv_a_b, ln_a_g=ln_a_g, ln_a_b=ln_a_b,
             w_a_out=w_a_out, conv_b_w=conv_b_w, w_b_out=w_b_out, q_norm_g=q_norm_g, kv_norm_g=kv_norm_g, w_uq=w_uq,
             w_ukv=w_ukv, w_c_out=w_c_out, w_o=w_o, ln_g=ln_g, ln_b=ln_b)
    Mo = dict(w_ada=m_w_ada, b_ada=m_b_ada, w_in=m_w_in, conv_a_w=m_conv_a_w, conv_a_b=m_conv_a_b, ln_a_g=m_ln_a_g,
              ln_a_b=m_ln_a_b, w_a_out=m_w_a_out, conv_b_w=m_conv_b_w, w_b_out=m_w_b_out, q_norm_g=m_q_norm_g,
              kv_norm_g=m_kv_norm_g, w_uq=m_w_uq, w_ukv=m_w_ukv, w_c_out=m_w_c_out, w_o=m_w_o, ln_g=m_ln_g, ln_b=m_ln_b)
    Vo = dict(w_ada=v_w_ada, b_ada=v_b_ada, w_in=v_w_in, conv_a_w=v_conv_a_w, conv_a_b=v_conv_a_b, ln_a_g=v_ln_a_g,
              ln_a_b=v_ln_a_b, w_a_out=v_w_a_out, conv_b_w=v_conv_b_w, w_b_out=v_w_b_out, q_norm_g=v_q_norm_g,
              kv_norm_g=v_kv_norm_g, w_uq=v_w_uq, w_ukv=v_w_ukv, w_c_out=v_w_c_out, w_o=v_w_o, ln_g=v_ln_g, ln_b=v_ln_b)
    ORDER = ("w_ada", "b_ada", "w_in", "conv_a_w", "conv_a_b", "ln_a_g", "ln_a_b", "w_a_out", "conv_b_w", "w_b_out",
             "q_norm_g", "kv_norm_g", "w_uq", "w_ukv", "w_c_out", "w_o", "ln_g", "ln_b")
    L = w_ada.shape[0]
    S = x.shape[1]
    me = 4 * lax.axis_index("x") + 2 * lax.axis_index("y") + lax.axis_index("c")
    x2 = x[0]
    tgt = loss_target[0]

    small_in = _rows(jnp.concatenate([c.reshape(-1), conv_a_w.reshape(-1), conv_b_w.reshape(-1)]))
    w_in_b = jnp.pad(w_in.astype(BF), ((0, 0), (0, 0), (0, W_IN_PAD - IN_SHARD)))
    misc_b = jnp.concatenate([w_a_out, w_b_out, w_c_out, w_ukv, jnp.pad(w_uq, ((0, 0), (0, 0), (0, LANE - UQ_SHARD)))],
                             axis=1).astype(BF)
    w_o_b = w_o.astype(BF)
    gathered = [None] * L
    pending, _ = exchange_begin("gather0", [w_in_b[0]], 1, small_in)
    pending_rest, _ = exchange_begin("gather0_rest", [misc_b[0], w_o_b[0]], 2, small_in)
    sg = exchange("gather_small", [small_in], [])[0]
    sgf = sg.reshape(NDEV, -1)
    c_all = sgf[:, :D_MODEL]
    o1 = D_MODEL + L * 31 * 64
    conv_a_full = sgf[:, D_MODEL:o1].reshape(NDEV, L, 31, 64).transpose(1, 2, 0, 3).reshape(L, 31, CW)
    conv_b_full = sgf[:, o1:o1 + L * 3 * 64].reshape(NDEV, L, 3, 64).transpose(1, 2, 0, 3).reshape(L, 3, CW)

    c_act = rowwise("silu_c", _silu, 16, 16, [(jnp.pad(c_all, ((0, 8), (0, 0))), D_MODEL, 0)], [], [(D_MODEL, BF)])[0]
    ncol = w_ada.shape[2]
    w_ada_b = w_ada.astype(BF).transpose(1, 0, 2).reshape(D_MODEL, L * ncol)
    b_mine = lax.dynamic_slice_in_dim(b_ada, me * ncol, ncol, axis=1).reshape(1, L * ncol)
    ada_part = mm(c_act, w_ada_b, name="mm_ada", bias=b_mine)
    ada_rows = -(-(L * ncol) // (LANE * 8)) * 8
    ada_send = jnp.pad(ada_part[:NDEV].reshape(NDEV, -1, LANE), ((0, 0), (0, ada_rows - L * ncol // LANE), (0, 0)))
    ada_recv = exchange("a2a_ada", [], [ada_send])[0]
    ada = ada_recv[:, :L * ncol // LANE].reshape(NDEV, L, ncol).transpose(1, 0, 2).reshape(L, 3, D_MODEL)

    inv_freq = ROPE_THETA ** (-jnp.arange(0, ROPE, 2, dtype=F32) / ROPE)
    ang = positions[0].astype(F32)[:, None] * inv_freq
    tabs = (jnp.tile(jnp.cos(ang), (1, 2 * LANE // ROPE)), jnp.tile(jnp.sin(ang), (1, 2 * LANE // ROPE)))

    straight = np.arange(D_MODEL)
    fwd_in = [(0, 0, 0, 0, D_MODEL, _to_gathered(IN_PERM, IN_SHARD, W_IN_PAD))]
    fwd_misc = [(0, ROW_A, 0, 0, CW, straight), (0, ROW_B, 1, 0, CW, straight), (0, ROW_C, 2, 0, CW, straight),
                (0, ROW_UKV, 3, 0, KVL, UKV_PERM), (0, ROW_UQ, 4, 0, QL, _to_gathered(UQ_PERM, UQ_SHARD, LANE))]
    rev_in = [(0, 0, 0, 0, D_MODEL, _from_full(_inverse(IN_PERM, D_IN), IN_SHARD, W_IN_PAD))]
    rev_misc = [(0, 0, 0, ROW_A, CW, straight), (1, 0, 0, ROW_B, CW, straight), (2, 0, 0, ROW_C, CW, straight),
                (3, 0, 0, ROW_UKV, KVL, _from_full(_inverse(UKV_PERM, HEADS * (NOPE + VH)), LANE, LANE)),
                (4, 0, 0, ROW_UQ, QL, _from_full(_inverse(UQ_PERM, HEADS * (NOPE + ROPE)), UQ_SHARD, LANE))]

    def layer_weights(l, deps):
        w_in_p = col_gather("relayout_w_in", [gathered[l][0]], [(D_MODEL, NP)], fwd_in, deps)[0]

        def late(after):
            if len(gathered[l]) == 1:
                gathered[l] += exchange_end(pending_rest, after)
            _, g_misc, g_o = gathered[l]
            a_out, b_out, c_out, ukv, uq = col_gather(
                "relayout_misc", [g_misc],
                [(CW, D_MODEL)] * 3 + [(KVL, HEADS * (NOPE + VH)), (QL, HEADS * (NOPE + ROPE))], fwd_misc, deps)
            return {"a_out": a_out, "b_out": b_out, "c_out": c_out, "uq": uq, "ukv": ukv,
                    "o": g_o.reshape(D_MODEL, D_MODEL)}

        return {
            "in": w_in_p, "late": late,
            "conv_a": jnp.pad(conv_a_full[l], ((0, 1), (0, 0))), "conv_b": jnp.pad(conv_b_full[l], ((0, 5), (0, 0))),
            "vec_a": jnp.stack([conv_a_b[l], ln_a_g[l], ln_a_b[l]]), "gq": q_norm_g[l][None], "gkv": kv_norm_g[l][None],
            "lnv": jnp.stack([ln_g[l], ln_b[l]]),
        }

    h = x2
    saved, weights = [], []
    gathered[0] = exchange_end(pending, ada)
    for l in range(L):
        ada_l, deps = ada[l], ()
        if l + 1 < L:
            pending, token = exchange_begin("gather%d" % (l + 1), [w_in_b[l + 1], misc_b[l + 1], w_o_b[l + 1]], 3,
                                            gathered[l][0])
            ada_l, deps = ada_l + token[0, 0], (token,)
        h, sv, Wl = layer_fwd(h, ada_l, layer_weights(l, deps), tabs, S)
        if l + 1 < L:
            gathered[l + 1] = exchange_end(pending, h)
        saved.append(sv)
        weights.append(Wl)

    def loss_fn(y, t):
        e = y - t
        return e * (1.0 / D_MODEL), _colsum(e * e)

    dy, sq = rowwise("loss", loss_fn, S, 256, [(h, D_MODEL, 0), (tgt, D_MODEL, 0)], [], [(D_MODEL, F32)], [D_MODEL])
    loss = lax.psum(0.5 * jnp.sum(sq) / D_MODEL, ("x", "y", "c"))

    grads, d_adas, recv = [None] * L, [None] * L, [None] * L
    pending, token = None, None

    def send_rest(g):
        send_misc = col_gather("unrelayout_misc", [g["w_a_out"], g["w_b_out"], g["w_c_out"], g["w_ukv"], g["w_uq"]],
                               [(NDEV, MISC_ROWS, LANE)], rev_misc)[0]
        return [send_misc, g["w_o"].reshape(NDEV, D_MODEL // NDEV, D_MODEL)]

    rest0 = []

    def early_rest(g):
        handle, tok = exchange_begin("scatter0_rest", send_rest(g), 0, g["w_o"])
        rest0.append(handle)
        return (tok,)

    for l in reversed(range(L)):
        ada_l = ada[l] if token is None else ada[l] + token[0, 0]
        dy, g, d_adas[l] = layer_bwd(dy, saved[l], ada_l, weights[l], tabs, S, early_rest if l == 0 else None)
        grads[l] = g
        if pending is not None:
            recv[l + 1] = exchange_end(pending, dy)
        send_in = col_gather("unrelayout_w_in", [g["w_in"]], [(NDEV, D_MODEL, W_IN_PAD)], rev_in)[0]
        if l == 0:
            pending, token = exchange_begin("scatter0", [send_in], 0, recv[1][0])
        else:
            pending, token = exchange_begin("scatter%d" % l, [send_in] + send_rest(g), 0,
                                            dy if l + 1 == L else recv[l + 1][0])
    grad_x = dy[None]

    small_parts = [jnp.stack([grads[l][n].reshape(-1) for l in range(L)]).reshape(-1) for n in SMALL]
    small_parts.append(jnp.stack([grads[l]["conv_a_w"][:31].reshape(-1) for l in range(L)]).reshape(-1))
    small_parts.append(jnp.stack([grads[l]["conv_b_w"][:3].reshape(-1) for l in range(L)]).reshape(-1))
    small_parts.append(jnp.stack([d_adas[l].reshape(-1) for l in range(L)]).reshape(-1))
    small_sizes = [int(p.shape[0]) for p in small_parts]
    gsmall = exchange("gather_small_grads", [_rows(jnp.concatenate(small_parts))], [])[0]
    gsum = sum_slots("sum_small", gsmall).reshape(-1)
    recv[0] = [None] + exchange_end(rest0[0], gsum)
    Gr = {}
    offs = np.cumsum([0] + small_sizes)
    for i, n in enumerate(SMALL):
        Gr[n] = gsum[offs[i]:offs[i + 1]].reshape(L, -1)
    ca = gsum[offs[7]:offs[8]].reshape(L, 31, CW)
    cbw = gsum[offs[8]:offs[9]].reshape(L, 3, CW)
    Gr["conv_a_w"] = lax.dynamic_slice_in_dim(ca, me * 64, 64, axis=2)
    Gr["conv_b_w"] = lax.dynamic_slice_in_dim(cbw, me * 64, 64, axis=2)
    Gr["b_ada"] = gsum[offs[9]:offs[10]].reshape(L, 3 * D_MODEL)
    d_ada_all = gsmall.reshape(NDEV, -1)[:, offs[9]:offs[10]].reshape(NDEV, L, 3 * D_MODEL)
    d_mine = lax.dynamic_slice_in_dim(d_ada_all, me * ncol, ncol, axis=2).reshape(NDEV, L * ncol)
    g_ada = mm(c_act, jnp.pad(d_mine, ((0, 8), (0, 0))).astype(BF), name="mm_gw_ada", trans_a=True)
    Gr["w_ada"] = g_ada.reshape(D_MODEL, L, ncol).transpose(1, 0, 2)

    D, NM, NV = {}, {}, {}
    D["w_ada"], NM["w_ada"], NV["w_ada"] = adamw("adamw_w_ada", P["w_ada"], Gr["w_ada"], Mo["w_ada"], Vo["w_ada"])
    Gr["w_o"], D["w_o"], NM["w_o"], NV["w_o"] = sum_adamw(
        "sum_adamw_w_o", [recv[l][2] for l in range(L)], P["w_o"], Mo["w_o"], Vo["w_o"])
    misc = lambda T_: jnp.concatenate([T_["w_a_out"], T_["w_b_out"], T_["w_c_out"], T_["w_ukv"],
                                       jnp.pad(T_["w_uq"], ((0, 0), (0, 0), (0, LANE - UQ_SHARD)))], axis=1)
    res = sum_adamw("sum_adamw_misc", [recv[l][1] for l in range(L)], misc(P), misc(Mo), misc(Vo))
    for T_, r in zip((Gr, D, NM, NV), res):
        T_["w_a_out"], T_["w_b_out"], T_["w_c_out"] = r[:, ROW_A:ROW_B], r[:, ROW_B:ROW_C], r[:, ROW_C:ROW_UKV]
        T_["w_ukv"], T_["w_uq"] = r[:, ROW_UKV:ROW_UQ], r[:, ROW_UQ:MISC_ROWS, :UQ_SHARD]
    recv[0][0] = exchange_end(pending, res[1])[0]
    Gr["w_in"], D["w_in"], NM["w_in"], NV["w_in"] = sum_adamw(
        "sum_adamw_w_in", [recv[l][0] for l in range(L)], P["w_in"], Mo["w_in"], Vo["w_in"])
    packed =("b_ada", "conv_a_w", "conv_b_w") + SMALL
    pk = lambda T_: _rows(jnp.concatenate([T_[n].reshape(-1) for n in packed]))[None]
    dS, mS, vS = adamw("adamw_small", pk(P), pk(Gr), pk(Mo), pk(Vo))
    o = 0
    for n in packed:
        sz = int(np.prod(P[n].shape))
        D[n] = dS.reshape(-1)[o:o + sz].reshape(P[n].shape)
        NM[n] = mS.reshape(-1)[o:o + sz].reshape(P[n].shape)
        NV[n] = vS.reshape(-1)[o:o + sz].reshape(P[n].shape)
        o += sz
    return (loss, grad_x, *[Gr[n] for n in ORDER], *[D[n] for n in ORDER], *[NM[n] for n in ORDER],
            *[NV[n] for n in ORDER])
```

```python
import functools
import math

import numpy as np
import jax
import jax.numpy as jnp
from jax import lax
from jax.experimental import pallas as pl
from jax.experimental.pallas import tpu as pltpu

BF = jnp.bfloat16
F32 = jnp.float32
MESH = pl.DeviceIdType.MESH
NDEV = 8

HEADS, NOPE, ROPE, VH = 8, 64, 32, 64
HP = 128
ROPE_THETA = 10000.0
LN_EPS = 1e-5
RMS_EPS = 1e-6
LR, B1, B2, EPS, WD, STEP = 0.001, 0.9, 0.999, 1e-08, 0.01, 10

LANE = 128
VMEM_LIMIT = 56 * 1024 * 1024

D_MODEL, CW, QL, KVL = 1024, 512, 384, 256
OFF_M, OFF_A, OFF_AG, OFF_B, OFF_CG, OFF_GB, OFF_BG = 0, 3072, 4096, 4608, 5632, 6144, 6656
OFF_KV, OFF_KR, OFF_Q, NP = 7168, 7424, 7680, 8192
D_IN = 7840


def _cparams(**kw):
    return pltpu.CompilerParams(vmem_limit_bytes=VMEM_LIMIT, **kw)


def _sigmoid(x):
    return jax.nn.sigmoid(x)


def _silu(x):
    return x * _sigmoid(x)


def _dsilu(x):
    s = _sigmoid(x)
    return s * (1.0 + x * (1.0 - s))


def _pick_tile(n, cap, mult):
    if n <= cap:
        return n
    for t in range(cap - cap % mult, 0, -mult):
        if n % t == 0:
            return t
    raise ValueError((n, cap, mult))


def mm(a, b, *, name, trans_a=False, trans_b=False, out_dtype=F32, bias=None, tm=1024, tn=1024, tk=2048, deps=()):
    if trans_a:
        K, M = a.shape
    else:
        M, K = a.shape
    if trans_b:
        N, K2 = b.shape
    else:
        K2, N = b.shape
    assert K == K2 and not (trans_a and trans_b), (a.shape, b.shape)
    tm, tn = _pick_tile(M, tm, 16), _pick_tile(N, tn, LANE)
    tk = _pick_tile(K, tk, LANE if trans_b else 16)
    assert M % tm == 0 and N % tn == 0 and K % tk == 0, (M, N, K, tm, tn, tk)
    nk = K // tk
    dims = (((0 if trans_a else 1,), (1 if trans_b else 0,)), ((), ()))
    has_bias = bias is not None

    def body(*refs):
        a_ref, b_ref = refs[0], refs[1]
        bias_ref = refs[2] if has_bias else None
        o_ref = refs[(3 if has_bias else 2) + len(deps)]
        p = lax.dot_general(a_ref[...], b_ref[...], dims, preferred_element_type=F32)

        def finish(v):
            if has_bias:
                v = v + bias_ref[...]
            o_ref[...] = v.astype(o_ref.dtype)

        if nk == 1:
            finish(p)
        else:
            acc = refs[-1]
            k = pl.program_id(2)

            @pl.when(k == 0)
            def _():
                acc[...] = p

            @pl.when(k > 0)
            def _():
                acc[...] += p

            @pl.when(k == nk - 1)
            def _():
                finish(acc[...])

    if trans_a:
        a_spec = pl.BlockSpec((tk, tm), lambda i, j, k: (k, i))
    else:
        a_spec = pl.BlockSpec((tm, tk), lambda i, j, k: (i, k))
    if trans_b:
        b_spec = pl.BlockSpec((tn, tk), lambda i, j, k: (j, k))
    else:
        b_spec = pl.BlockSpec((tk, tn), lambda i, j, k: (k, j))
    in_specs = [a_spec, b_spec]
    args = [a, b]
    if has_bias:
        in_specs.append(pl.BlockSpec((1, tn), lambda i, j, k: (0, j)))
        args.append(bias)
    in_specs += [ANY_SPEC] * len(deps)
    args += list(deps)
    return pl.pallas_call(
        body, name=name, grid=(M // tm, N // tn, nk),
        in_specs=in_specs, out_specs=pl.BlockSpec((tm, tn), lambda i, j, k: (i, j)),
        out_shape=jax.ShapeDtypeStruct((M, N), out_dtype),
        scratch_shapes=[pltpu.VMEM((tm, tn), F32)] if nk > 1 else [],
        compiler_params=_cparams(),
    )(*args)


def rowwise(name, fn, S, T, row_ins, full_ins, row_outs, acc_outs=(), into=None):
    n_in = len(row_ins) + len(full_ins)
    n_ro, n_ao = len(row_outs), len(acc_outs)
    alias = into is not None and into[0] is not None

    def body(*refs):
        vals = [r[...] for r in refs[:n_in]]
        vals = [v.astype(F32) if v.dtype == BF else v for v in vals]
        outs = fn(*vals)
        if not isinstance(outs, (tuple, list)):
            outs = (outs,)
        assert len(outs) == n_ro + n_ao, (name, len(outs))
        o0 = n_in + (1 if alias else 0)
        for r, v in zip(refs[o0:o0 + n_ro], outs[:n_ro]):
            r[...] = v.astype(r.dtype)
        first = pl.program_id(0) == 0
        for r, v in zip(refs[o0 + n_ro:], outs[n_ro:]):
            def init(r=r, v=v):
                r[...] = v

            def accum(r=r, v=v):
                r[...] += v

            pl.when(first)(init)
            pl.when(jnp.logical_not(first))(accum)

    in_specs, args = [], []
    for arr, W, off in row_ins:
        assert off % W == 0 and arr.shape[0] == S, (name, arr.shape, W, off)
        in_specs.append(pl.BlockSpec((T, W), functools.partial(lambda i, cb: (i, cb), cb=off // W)))
        args.append(arr)
    for arr in full_ins:
        in_specs.append(pl.BlockSpec(arr.shape, lambda i: (0, 0)))
        args.append(arr)
    out_specs = [pl.BlockSpec((T, W), lambda i: (i, 0)) for W, _ in row_outs]
    out_shape = [jax.ShapeDtypeStruct((S, W), dt) for W, dt in row_outs]
    aliases = {}
    if into is not None:
        buf, total, off = into
        W0, dt0 = row_outs[0]
        assert off % W0 == 0
        out_specs[0] = pl.BlockSpec((T, W0), functools.partial(lambda i, cb: (i, cb), cb=off // W0))
        out_shape[0] = jax.ShapeDtypeStruct((S, total), dt0)
        if alias:
            in_specs.append(ANY_SPEC)
            args.append(buf)
            aliases = {n_in: 0}
    out_specs += [pl.BlockSpec((1, W), lambda i: (0, 0)) for W in acc_outs]
    out_shape += [jax.ShapeDtypeStruct((1, W), F32) for W in acc_outs]
    return pl.pallas_call(
        body, name=name, grid=(S // T,), in_specs=in_specs, out_specs=out_specs, out_shape=out_shape,
        input_output_aliases=aliases, compiler_params=_cparams(),
    )(*args)


def _colsum(v):
    return jnp.sum(v, axis=0, keepdims=True)


def _ln_stats(r):
    mu = jnp.mean(r, axis=-1, keepdims=True)
    d = r - mu
    var = jnp.mean(d * d, axis=-1, keepdims=True)
    rstd = lax.rsqrt(var + LN_EPS)
    return d * rstd, rstd


def _ln_bwd(dn, n, rstd):
    return rstd * (dn - jnp.mean(dn, axis=-1, keepdims=True) - n * jnp.mean(dn * n, axis=-1, keepdims=True))


CPAD = 32
TC = 64


def _pre(mode, x1, x2):
    return x1 * _sigmoid(x2) if mode == "glu" else x1 * x2


def _shifted(ext, sft):
    n = TC + CPAD
    return pltpu.roll(ext, (n - sft) % n, 0)[0:TC]


def _interleaved_specs(S, off):
    return [pl.BlockSpec((S, LANE), functools.partial(lambda j, o: (0, o + 2 * j), o=off // LANE)),
            pl.BlockSpec((S, LANE), functools.partial(lambda j, o: (0, o + 2 * j + 1), o=off // LANE))]


def conv_fwd(name, src, off, w_pad, taps, mode, S, C):
    nchunk = S // TC

    def body(x1_ref, x2_ref, w_ref, o_ref, a_pad):
        a_pad[0:CPAD, :] = jnp.zeros((CPAD, LANE), F32)

        def fill(i, _):
            r = pl.multiple_of(i * 256, 256)
            a_pad[pl.ds(CPAD + r, 256), :] = _pre(mode, x1_ref[pl.ds(r, 256), :].astype(F32),
                                                  x2_ref[pl.ds(r, 256), :].astype(F32))
            return 0

        lax.fori_loop(0, S // 256, fill, 0)

        def chunk(i, _):
            base = pl.multiple_of(i * TC, TC)
            ext = a_pad[pl.ds(base, TC + CPAD), :]
            acc = jnp.zeros((TC, LANE), F32)
            for k in range(taps):
                acc = acc + w_ref[pl.ds(k, 1), :] * _shifted(ext, CPAD - (taps - 1) + k)
            o_ref[pl.ds(base, TC), :] = acc
            return 0

        lax.fori_loop(0, nchunk, chunk, 0)

    kp = w_pad.shape[0]
    return pl.pallas_call(
        body, name=name, grid=(C // LANE,),
        in_specs=_interleaved_specs(S, off) + [pl.BlockSpec((kp, LANE), lambda j: (0, j))],
        out_specs=pl.BlockSpec((S, LANE), lambda j: (0, j)),
        out_shape=jax.ShapeDtypeStruct((S, C), F32),
        scratch_shapes=[pltpu.VMEM((S + CPAD, LANE), F32)],
        compiler_params=_cparams(),
    )(src, src, w_pad)


def conv_bwd(name, src, off, dc, w_pad, taps, mode, S, C, buf):
    nchunk = S // TC
    kp = w_pad.shape[0]

    def body(x1_ref, x2_ref, dc_ref, w_ref, _, d_ref, dw_ref, a_pad, dc_pad, dw_acc):
        a_pad[0:CPAD, :] = jnp.zeros((CPAD, LANE), F32)
        dc_pad[S:S + CPAD, :] = jnp.zeros((CPAD, LANE), F32)
        dw_acc[...] = jnp.zeros(dw_acc.shape, F32)

        def fill(i, _):
            r = pl.multiple_of(i * 256, 256)
            a_pad[pl.ds(CPAD + r, 256), :] = _pre(mode, x1_ref[pl.ds(r, 256), :].astype(F32),
                                                  x2_ref[pl.ds(r, 256), :].astype(F32))
            dc_pad[pl.ds(r, 256), :] = dc_ref[pl.ds(r, 256), :]
            return 0

        lax.fori_loop(0, S // 256, fill, 0)

        def chunk(i, _):
            base = pl.multiple_of(i * TC, TC)
            ext_d = dc_pad[pl.ds(base, TC + CPAD), :]
            ext_a = a_pad[pl.ds(base, TC + CPAD), :]
            dcv = ext_d[0:TC]
            da = jnp.zeros((TC, LANE), F32)
            for k in range(taps):
                da = da + w_ref[pl.ds(k, 1), :] * _shifted(ext_d, taps - 1 - k)
                prod = dcv * _shifted(ext_a, CPAD - (taps - 1) + k)
                fold = prod[0:8]
                for g in range(1, TC // 8):
                    fold = fold + prod[8 * g:8 * g + 8]
                dw_acc[pl.ds(8 * k, 8), :] += fold
            x1 = x1_ref[pl.ds(base, TC), :].astype(F32)
            x2 = x2_ref[pl.ds(base, TC), :].astype(F32)
            if mode == "glu":
                s = _sigmoid(x2)
                d1, d2 = da * s, da * x1 * s * (1.0 - s)
            else:
                d1, d2 = da * x2, da * x1
            d_ref[pl.ds(base, TC), 0:LANE] = d1.astype(BF)
            d_ref[pl.ds(base, TC), LANE:2 * LANE] = d2.astype(BF)
            return 0

        lax.fori_loop(0, nchunk, chunk, 0)
        dw_ref[...] = jnp.zeros(dw_ref.shape, F32)
        for k in range(taps):
            dw_ref[pl.ds(k, 1), :] = jnp.sum(dw_acc[pl.ds(8 * k, 8), :], axis=0, keepdims=True)

    blk = pl.BlockSpec((S, LANE), lambda j: (0, j))
    return pl.pallas_call(
        body, name=name, grid=(C // LANE,),
        in_specs=_interleaved_specs(S, off) + [blk, pl.BlockSpec((kp, LANE), lambda j: (0, j)), ANY_SPEC],
        out_specs=[pl.BlockSpec((S, 2 * LANE), functools.partial(lambda j, o: (0, o + j), o=off // (2 * LANE))),
                   pl.BlockSpec((kp, LANE), lambda j: (0, j))],
        out_shape=[jax.ShapeDtypeStruct(buf.shape, BF), jax.ShapeDtypeStruct((kp, C), F32)],
        input_output_aliases={4: 0},
        scratch_shapes=[pltpu.VMEM((S + CPAD, LANE), F32), pltpu.VMEM((S + CPAD, LANE), F32),
                        pltpu.VMEM((8 * kp, LANE), F32)],
        compiler_params=_cparams(),
    )(src, src, dc, w_pad, buf)


FWD_TILES = (512, 512)
BWD_TILES = (512, 512)
QUADS = HEADS // 4
QW, KVW = 4 * (NOPE + ROPE), 4 * (NOPE + VH)
SCALE = (NOPE + ROPE) ** -0.5
NT_DIMS = (((1,), (1,)), ((), ()))
TN_DIMS = (((0,), (0,)), ((), ()))


def _lane_mask(width, group, dtype):
    lane = lax.broadcasted_iota(jnp.int32, (1, LANE), 1)
    return jnp.where(lane // width == group, 1.0, 0.0).astype(dtype)


def _visible(tq, tk, off):
    row = lax.broadcasted_iota(jnp.int32, (tq, tk), 0)
    col = lax.broadcasted_iota(jnp.int32, (tq, tk), 1)
    return col <= row + off


def _attn_tiles(S, tq, tk):
    tk = tk if S % tk == 0 else 256
    return min(tq, tk), tk


def attn_fwd(q, kv, kpe, S):
    tq, tk = _attn_tiles(S, *FWD_TILES)
    nq = S // tq

    def body(q_ref, kv_ref, kp_ref, o_ref, lse_ref):
        for t in range(2):
            cols = slice(t * LANE, (t + 1) * LANE)
            for hh in range(2):
                def q_block(qi, _, t=t, hh=hh, cols=cols):
                    r0 = pl.multiple_of(qi * tq, tq)
                    qcat = jnp.concatenate([q_ref[pl.ds(r0, tq), cols] * _lane_mask(NOPE, hh, BF),
                                            q_ref[pl.ds(r0, tq), 2 * LANE:3 * LANE] * _lane_mask(ROPE, 2 * t + hh, BF)],
                                           axis=1)
                    nfull = (qi * tq) // tk

                    def step(kj, carry, masked):
                        m, l, acc = carry
                        c0 = pl.multiple_of(kj * tk, tk)
                        kc = jnp.concatenate([kv_ref[pl.ds(c0, tk), cols], kp_ref[pl.ds(c0, tk), :]], axis=1)
                        vt = kv_ref[pl.ds(c0, tk), (2 + t) * LANE:(3 + t) * LANE]
                        s = lax.dot_general(qcat, kc, NT_DIMS, preferred_element_type=F32) * SCALE
                        if masked:
                            s = jnp.where(_visible(tq, tk, qi * tq - nfull * tk), s, -jnp.inf)
                        m_new = jnp.maximum(m, jnp.max(s, axis=-1, keepdims=True))
                        p = jnp.exp(s - m_new)
                        alpha = jnp.exp(m - m_new)
                        l = alpha * l + jnp.sum(p, axis=-1, keepdims=True)
                        acc = alpha * acc + jnp.dot(p.astype(BF), vt, preferred_element_type=F32)
                        return m_new, l, acc

                    init = (jnp.full((tq, 1), -jnp.inf, F32), jnp.zeros((tq, 1), F32), jnp.zeros((tq, LANE), F32))
                    carry = lax.fori_loop(0, nfull, lambda kj, c: step(kj, c, False), init)
                    m, l, acc = step(nfull, carry, True)
                    mine = _lane_mask(NOPE, hh, F32)
                    if hh == 0:
                        o_ref[pl.ds(r0, tq), cols] = (acc / l) * mine
                        lse_ref[pl.ds(r0, tq), cols] = (m + jnp.log(l)) * mine
                    else:
                        o_ref[pl.ds(r0, tq), cols] += (acc / l) * mine
                        lse_ref[pl.ds(r0, tq), cols] += (m + jnp.log(l)) * mine
                    return 0

                lax.fori_loop(0, nq, q_block, 0)

    return pl.pallas_call(
        body, name="attn_fwd", grid=(QUADS,),
        in_specs=[pl.BlockSpec((S, QW), lambda g: (0, g)), pl.BlockSpec((S, KVW), lambda g: (0, g)),
                  pl.BlockSpec((S, LANE), lambda g: (0, 0))],
        out_specs=[pl.BlockSpec((S, 2 * LANE), lambda g: (0, g))] * 2,
        out_shape=[jax.ShapeDtypeStruct((S, HEADS * VH), F32)] * 2,
        compiler_params=_cparams(),
    )(q, kv, kpe)


def attn_bwd(q, kv, kpe, o, lse, do, S):
    tq, tk = _attn_tiles(S, *BWD_TILES)
    nq = S // tq

    def body(q_ref, kv_ref, kp_ref, o_ref, lse_ref, do_ref, dq_ref, dkv_ref, dkp_ref, dq_acc, dk_acc, dv_acc):
        for t in range(2):
            cols = slice(t * LANE, (t + 1) * LANE)
            dk_acc[...] = jnp.zeros(dk_acc.shape, F32)
            dv_acc[...] = jnp.zeros(dv_acc.shape, F32)
            for hh in range(2):
                def q_block(qi, _, t=t, hh=hh, cols=cols):
                    r0 = pl.multiple_of(qi * tq, tq)
                    mine = _lane_mask(NOPE, hh, F32)
                    qcat = jnp.concatenate([q_ref[pl.ds(r0, tq), cols] * _lane_mask(NOPE, hh, BF),
                                            q_ref[pl.ds(r0, tq), 2 * LANE:3 * LANE] * _lane_mask(ROPE, 2 * t + hh, BF)],
                                           axis=1)
                    dof = do_ref[pl.ds(r0, tq), cols] * mine
                    dob = dof.astype(BF)
                    delta = jnp.sum(dof * o_ref[pl.ds(r0, tq), cols], axis=-1, keepdims=True)
                    lse_h = lse_ref[pl.ds(r0, tq), cols][:, hh * NOPE:hh * NOPE + 1]
                    nfull = (qi * tq) // tk
                    dq_acc[...] = jnp.zeros(dq_acc.shape, F32)

                    def step(kj, _, masked):
                        c0 = pl.multiple_of(kj * tk, tk)
                        kc = jnp.concatenate([kv_ref[pl.ds(c0, tk), cols], kp_ref[pl.ds(c0, tk), :]], axis=1)
                        vt = kv_ref[pl.ds(c0, tk), (2 + t) * LANE:(3 + t) * LANE]
                        s = lax.dot_general(qcat, kc, NT_DIMS, preferred_element_type=F32) * SCALE
                        if masked:
                            s = jnp.where(_visible(tq, tk, qi * tq - nfull * tk), s, -jnp.inf)
                        p = jnp.exp(s - lse_h)
                        dp = lax.dot_general(dob, vt, NT_DIMS, preferred_element_type=F32)
                        ds = (p * (dp - delta) * SCALE).astype(BF)
                        dv_acc[pl.ds(c0, tk), :] += lax.dot_general(p.astype(BF), dob, TN_DIMS,
                                                                    preferred_element_type=F32)
                        dk_acc[pl.ds(c0, tk), :] += lax.dot_general(ds, qcat, TN_DIMS, preferred_element_type=F32)
                        dq_acc[...] += jnp.dot(ds, kc, preferred_element_type=F32)
                        return 0

                    lax.fori_loop(0, nfull, lambda kj, c: step(kj, c, False), 0)
                    step(nfull, 0, True)
                    d = dq_acc[...]
                    pe = d[:, LANE:] * _lane_mask(ROPE, 2 * t + hh, F32)
                    if hh == 0:
                        dq_ref[pl.ds(r0, tq), cols] = d[:, :LANE] * mine
                    else:
                        dq_ref[pl.ds(r0, tq), cols] += d[:, :LANE] * mine
                    if t == 0 and hh == 0:
                        dq_ref[pl.ds(r0, tq), 2 * LANE:3 * LANE] = pe
                    else:
                        dq_ref[pl.ds(r0, tq), 2 * LANE:3 * LANE] += pe
                    return 0

                lax.fori_loop(0, nq, q_block, 0)
            dkv_ref[:, t * LANE:(t + 1) * LANE] = dk_acc[:, :LANE].astype(BF)
            dkv_ref[:, (2 + t) * LANE:(3 + t) * LANE] = dv_acc[...].astype(BF)
            if t == 0:
                dkp_ref[...] = dk_acc[:, LANE:]
            else:
                dkp_ref[...] += dk_acc[:, LANE:]

    qspec = pl.BlockSpec((S, QW), lambda g: (0, g))
    kvspec = pl.BlockSpec((S, KVW), lambda g: (0, g))
    ospec = pl.BlockSpec((S, 2 * LANE), lambda g: (0, g))
    return pl.pallas_call(
        body, name="attn_bwd", grid=(QUADS,),
        in_specs=[qspec, kvspec, pl.BlockSpec((S, LANE), lambda g: (0, 0)), ospec, ospec, ospec],
        out_specs=[qspec, kvspec, pl.BlockSpec((S, LANE), lambda g: (0, g))],
        out_shape=[jax.ShapeDtypeStruct((S, HEADS * (NOPE + ROPE)), F32), jax.ShapeDtypeStruct((S, HEADS * (NOPE + VH)), BF),
                   jax.ShapeDtypeStruct((S, HEADS * ROPE), F32)],
        scratch_shapes=[pltpu.VMEM((tq, 2 * LANE), F32), pltpu.VMEM((S, 2 * LANE), F32), pltpu.VMEM((S, LANE), F32)],
        compiler_params=_cparams(),
    )(q, kv, kpe, o, lse, do)


def exchange(name, gathers, a2as):
    n_g, n = len(gathers), len(gathers) + len(a2as)

    def body(*refs):
        ins, outs = refs[:n], refs[n:2 * n]
        send_sems, recv_sems, loc_sems = refs[2 * n:]
        x, y, c = lax.axis_index("x"), lax.axis_index("y"), lax.axis_index("c")
        me = 4 * x + 2 * y + c

        def peer(k):
            px = 1 - x if k & 4 else x
            py = 1 - y if k & 2 else y
            pc = 1 - c if k & 1 else c
            return (px, py, pc), 4 * px + 2 * py + pc

        def remote(a, k):
            pid, pflat = peer(k)
            src = ins[a] if a < n_g else ins[a].at[pflat]
            return pltpu.make_async_remote_copy(
                src_ref=src, dst_ref=outs[a].at[me], send_sem=send_sems.at[a, k - 1], recv_sem=recv_sems.at[a, k - 1],
                device_id=pid, device_id_type=MESH)

        def arrival(a, k):
            pid, pflat = peer(k)
            src = ins[a] if a < n_g else ins[a].at[pflat]
            return pltpu.make_async_remote_copy(
                src_ref=src, dst_ref=outs[a].at[pflat], send_sem=send_sems.at[a, k - 1], recv_sem=recv_sems.at[a, k - 1],
                device_id=pid, device_id_type=MESH)

        local = []
        for a in range(n):
            own = ins[a] if a < n_g else ins[a].at[me]
            cp = pltpu.make_async_copy(own, outs[a].at[me], loc_sems.at[a])
            cp.start()
            local.append(cp)
        sent = []
        for k in (1, 2, 4, 3, 5, 6, 7):
            for a in range(n):
                cp = remote(a, k)
                cp.start()
                sent.append(cp)
        for k in range(1, 8):
            for a in range(n):
                arrival(a, k).wait_recv()
        for cp in sent:
            cp.wait_send()
        for cp in local:
            cp.wait()

    out_shape = [jax.ShapeDtypeStruct((NDEV,) + g.shape, g.dtype) for g in gathers]
    out_shape += [jax.ShapeDtypeStruct(a.shape, a.dtype) for a in a2as]
    any_spec = pl.BlockSpec(memory_space=pl.ANY)
    return pl.pallas_call(
        body, name=name, in_specs=[any_spec] * n, out_specs=[any_spec] * n, out_shape=out_shape,
        scratch_shapes=[pltpu.SemaphoreType.DMA((n, NDEV - 1)), pltpu.SemaphoreType.DMA((n, NDEV - 1)),
                        pltpu.SemaphoreType.DMA((n,))],
    )(*gathers, *a2as)


def _peer(k, x, y, c):
    px = 1 - x if k & 4 else x
    py = 1 - y if k & 2 else y
    pc = 1 - c if k & 1 else c
    return (px, py, pc), 4 * px + 2 * py + pc


PEER_ORDER = (1, 2, 4, 3, 5, 6, 7)
HBM_SPEC = pl.BlockSpec(memory_space=pltpu.HBM)
SEM_SPEC = pl.BlockSpec(memory_space=pltpu.SEMAPHORE)
ANY_SPEC = pl.BlockSpec(memory_space=pl.ANY)


def _split_copies(ins, lands, n_g, send_sems, recv_sems):
    x, y, c = lax.axis_index("x"), lax.axis_index("y"), lax.axis_index("c")
    me = 4 * x + 2 * y + c

    def outgoing(a, k):
        pid, pflat = _peer(k, x, y, c)
        src = ins[a] if a < n_g else ins[a].at[pflat]
        return pltpu.make_async_remote_copy(
            src_ref=src, dst_ref=lands[a].at[me], send_sem=send_sems.at[a * (NDEV - 1) + k - 1],
            recv_sem=recv_sems.at[a * (NDEV - 1) + k - 1],
            device_id=pid, device_id_type=MESH)

    def arrival(a, k):
        pid, pflat = _peer(k, x, y, c)
        src = ins[a] if a < n_g else ins[a].at[pflat]
        return pltpu.make_async_remote_copy(
            src_ref=src, dst_ref=lands[a].at[pflat], send_sem=send_sems.at[a * (NDEV - 1) + k - 1],
            recv_sem=recv_sems.at[a * (NDEV - 1) + k - 1],
            device_id=pid, device_id_type=MESH)

    return outgoing, arrival


def exchange_begin(name, srcs, n_g, dep):
    n = len(srcs)
    land_shapes = [((NDEV,) + s.shape) if a < n_g else s.shape for a, s in enumerate(srcs)]

    def own_body(*refs):
        ins, outs = refs[:n], refs[n + 1:2 * n + 1]
        stage, sems = refs[2 * n + 1:3 * n + 1], refs[-1]
        me = 4 * lax.axis_index("x") + 2 * lax.axis_index("y") + lax.axis_index("c")
        cps = [pltpu.make_async_copy(ins[a] if a < n_g else ins[a].at[me], stage[a], sems.at[a]) for a in range(n)]
        for cp in cps:
            cp.start()
        for cp in cps:
            cp.wait()
        cps = [pltpu.make_async_copy(stage[a], outs[a].at[me], sems.at[a]) for a in range(n)]
        for cp in cps:
            cp.start()
        for cp in cps:
            cp.wait()

    lands = pl.pallas_call(
        own_body, name=name + "_own", in_specs=[ANY_SPEC] * (n + 1), out_specs=[ANY_SPEC] * n,
        out_shape=[jax.ShapeDtypeStruct(sh, s.dtype) for sh, s in zip(land_shapes, srcs)],
        scratch_shapes=[pltpu.VMEM(sh[1:], s.dtype) for sh, s in zip(land_shapes, srcs)] + [pltpu.SemaphoreType.DMA((n,))],
        compiler_params=_cparams(),
    )(*srcs, dep)

    def start_body(*refs):
        ins, lz = refs[:n], refs[n:2 * n]
        send_sems, recv_sems, token = refs[2 * n], refs[2 * n + 1], refs[-1]
        outgoing, _ = _split_copies(ins, lz, n_g, send_sems, recv_sems)
        for k in PEER_ORDER:
            for a in range(n):
                outgoing(a, k).start()
        token[...] = jnp.zeros(token.shape, F32)

    hbm = lambda t: pltpu.HBM(t.shape, t.dtype)
    res = pl.pallas_call(
        start_body, name=name + "_start",
        out_shape=(pltpu.SemaphoreType.DMA((n * (NDEV - 1),)), pltpu.SemaphoreType.DMA((n * (NDEV - 1),)),
                   *[hbm(s) for s in srcs], *[hbm(t) for t in lands], jax.ShapeDtypeStruct((8, LANE), F32)),
        in_specs=[HBM_SPEC] * (2 * n),
        out_specs=(SEM_SPEC, SEM_SPEC, *[HBM_SPEC] * (2 * n), pl.BlockSpec(memory_space=pltpu.VMEM)),
        input_output_aliases={i: 2 + i for i in range(2 * n)},
        compiler_params=pltpu.CompilerParams(has_side_effects=pltpu.SideEffectType.DATAFLOW_SIDE_EFFECTING),
    )(*[pltpu.with_memory_space_constraint(t, pltpu.HBM) for t in list(srcs) + list(lands)])
    return (name, n, n_g, res[:-1]), res[-1]


def exchange_end(handle, after):
    name, n, n_g, (send_sems, recv_sems, *bufs) = handle

    def wait_body(*refs):
        ins, lz = refs[:n], refs[n:2 * n]
        ss, rs = refs[2 * n], refs[2 * n + 1]
        outgoing, arrival = _split_copies(ins, lz, n_g, ss, rs)
        for k in range(1, NDEV):
            for a in range(n):
                arrival(a, k).wait_recv()
        for k in range(1, NDEV):
            for a in range(n):
                outgoing(a, k).wait_send()

    res = pl.pallas_call(
        wait_body, name=name + "_wait", out_shape=tuple(pltpu.HBM(t.shape, t.dtype) for t in bufs),
        in_specs=[HBM_SPEC] * (2 * n) + [SEM_SPEC, SEM_SPEC, ANY_SPEC], out_specs=[HBM_SPEC] * (2 * n),
        input_output_aliases={i: i for i in range(2 * n)},
        compiler_params=pltpu.CompilerParams(has_side_effects=pltpu.SideEffectType.DATAFLOW_SIDE_EFFECTING),
    )(*bufs, send_sems, recv_sems, after)
    return list(res[n:])


def _pick_rows(R, mult, cap):
    best = None
    for n in range(1, R + 1):
        if R % n == 0 and (R // n) % mult == 0 and R // n <= cap:
            best = R // n
            break
    assert best is not None, (R, mult, cap)
    return best


def sum_slots(name, x):
    _, R, _ = x.shape
    tr = _pick_rows(R, 16, 2304)

    def body(x_ref, o_ref):
        acc = x_ref[0].astype(F32)
        for d in range(1, NDEV):
            acc = acc + x_ref[d].astype(F32)
        o_ref[...] = acc

    return pl.pallas_call(
        body, name=name, grid=(R // tr,),
        in_specs=[pl.BlockSpec((NDEV, tr, LANE), lambda i: (0, i, 0))],
        out_specs=pl.BlockSpec((tr, LANE), lambda i: (i, 0)),
        out_shape=jax.ShapeDtypeStruct((R, LANE), F32), compiler_params=_cparams(),
    )(x)


def adamw(name, w, g, m, v):
    L, R, C = w.shape
    tr = _pick_rows(R, 8, 256) if R % 8 == 0 else R

    def body(w_ref, g_ref, m_ref, v_ref, d_ref, nm_ref, nv_ref):
        gg = g_ref[...]
        nm = B1 * m_ref[...] + (1.0 - B1) * gg
        nv = B2 * v_ref[...] + (1.0 - B2) * jnp.square(gg)
        m_hat = nm / (1.0 - B1 ** STEP)
        v_hat = nv / (1.0 - B2 ** STEP)
        d_ref[...] = -LR * (m_hat / (jnp.sqrt(v_hat) + EPS) + WD * w_ref[...])
        nm_ref[...] = nm
        nv_ref[...] = nv

    blk = pl.BlockSpec((1, tr, C), lambda l, i: (l, i, 0))
    shp = jax.ShapeDtypeStruct(w.shape, F32)
    return pl.pallas_call(
        body, name=name, grid=(L, R // tr), in_specs=[blk] * 4, out_specs=[blk] * 3, out_shape=[shp] * 3,
        compiler_params=_cparams(),
    )(w, g, m, v)


IN_SHARD = D_IN // NDEV
UQ_SHARD = HEADS * (NOPE + ROPE) // NDEV
W_IN_PAD = 1024
ROW_A, ROW_B, ROW_C, ROW_UKV, ROW_UQ, MISC_ROWS = 0, 512, 1024, 1536, 1792, 2176


def _in_perm_index():
    ar = np.arange
    z = lambda n: np.full((n,), -1, np.int64)
    mix = lambda lo1, lo2: np.concatenate([ar(lo + LANE * j, lo + LANE * (j + 1)) for j in range(CW // LANE)
                                           for lo in (lo1, lo2)])
    return np.concatenate([ar(4768, 7840), mix(0, 512), ar(1024, 1536), mix(1536, 2560), ar(4256, 4768), ar(2048, 2560),
                           ar(3072, 3584), ar(3968, 4224), ar(4224, 4256), z(OFF_Q - OFF_KR - ROPE), ar(3584, 3968),
                           z(NP - OFF_Q - QL)])


def _head_perm_index(a, b):
    parts = []
    for g in range(QUADS):
        h = np.arange(4 * g, 4 * g + 4)[:, None] * (a + b)
        parts += [(h + np.arange(a)[None]).reshape(-1), (h + a + np.arange(b)[None]).reshape(-1)]
    return np.concatenate(parts)


def _inverse(perm, n):
    inv = np.full((n,), -1, np.int64)
    inv[perm[perm >= 0]] = np.nonzero(perm >= 0)[0]
    return inv


IN_PERM = _in_perm_index()
UQ_PERM = _head_perm_index(NOPE, ROPE)
UKV_PERM = _head_perm_index(NOPE, VH)


def _to_gathered(perm, shard, pad):
    return np.where(perm >= 0, (perm // shard) * pad + perm % shard, -1)


def _from_full(inv, shard, pad):
    j, i = np.divmod(np.arange(NDEV * pad), pad)
    return np.where(i < shard, inv[np.minimum(j * shard + i, inv.shape[0] - 1)], -1)


def col_gather(name, srcs, out_shapes, jobs, deps=()):
    ns, nj, nd = len(srcs), len(jobs), len(deps)
    tables = [jnp.asarray(np.asarray(job[5], np.int32)[None, :]) for job in jobs]

    def view(ref, col0, width, r0, rc):
        n = ref.shape[-1]
        if len(ref.shape) == 3:
            return ref.at[col0 // n, pl.ds(r0, rc), pl.ds(col0 % n, width)]
        return ref.at[pl.ds(r0, rc), pl.ds(col0, width)]

    def body(*refs):
        src_refs, tab_refs, out_refs = refs[:ns], refs[ns:ns + nj], refs[ns + nj + nd:]
        for ji, (si, srow, oi, orow, nrows, tgt) in enumerate(jobs):
            sref, oref = src_refs[si], out_refs[oi]
            tgt = np.asarray(tgt)
            tw = 256 if oref.shape[-1] % 256 == 0 else LANE
            rc = 256 if nrows % 256 == 0 else LANE
            for t in range(tgt.shape[0] // tw):
                tt = tgt[t * tw:(t + 1) * tw]
                tiles = sorted(set((tt[tt >= 0] // LANE).tolist()))
                straight = bool(tiles) and np.array_equal(tt, np.arange(tiles[0] * LANE, tiles[0] * LANE + tw))
                onehots = []
                if tiles and not straight:
                    want = tab_refs[ji][:, t * tw:(t + 1) * tw]
                    row = lax.broadcasted_iota(jnp.int32, (LANE, tw), 0)
                    onehots = [jnp.where(want == row + s * LANE, 1.0, 0.0).astype(BF) for s in tiles]

                def chunk(ci, _, t=t, tiles=tiles, straight=straight, onehots=onehots):
                    r0 = ci * rc
                    dst = view(oref, t * tw, tw, pl.multiple_of(orow + r0, LANE), rc)
                    rs = pl.multiple_of(srow + r0, LANE)
                    if not tiles:
                        dst[...] = jnp.zeros((rc, tw), BF)
                    elif straight:
                        for k in range(tw // LANE):
                            view(oref, t * tw + k * LANE, LANE, pl.multiple_of(orow + r0, LANE), rc)[...] = (
                                view(sref, (tiles[0] + k) * LANE, LANE, rs, rc)[...])
                    else:
                        acc = None
                        for s, oh in zip(tiles, onehots):
                            p = jnp.dot(view(sref, s * LANE, LANE, rs, rc)[...], oh, preferred_element_type=F32)
                            acc = p if acc is None else acc + p
                        dst[...] = acc.astype(BF)
                    return 0

                lax.fori_loop(0, nrows // rc, chunk, 0)

    vmem = pl.BlockSpec(memory_space=pltpu.VMEM)
    return pl.pallas_call(
        body, name=name, in_specs=[vmem] * (ns + nj) + [ANY_SPEC] * nd, out_specs=[vmem] * len(out_shapes),
        out_shape=[jax.ShapeDtypeStruct(s, BF) for s in out_shapes], compiler_params=_cparams(),
    )(*srcs, *tables, *deps)


def sum_adamw(name, recvs, w, m, v):
    L, R, C = w.shape
    CP = recvs[0].shape[-1]
    tr = _pick_rows(R, 16, 128)

    def body(*refs):
        r_refs = refs[:L]
        w_ref, m_ref, v_ref, g_ref, d_ref, nm_ref, nv_ref, gsum = refs[L:]
        layer = pl.program_id(0)
        for k in range(L):
            def total(k=k):
                acc = r_refs[k][0].astype(F32)
                for d in range(1, NDEV):
                    acc = acc + r_refs[k][d].astype(F32)
                gsum[...] = acc
            pl.when(layer == k)(total)
        gg = gsum[:, 0:C]
        nm = B1 * m_ref[...] + (1.0 - B1) * gg
        nv = B2 * v_ref[...] + (1.0 - B2) * jnp.square(gg)
        m_hat = nm / (1.0 - B1 ** STEP)
        v_hat = nv / (1.0 - B2 ** STEP)
        g_ref[...] = gg
        d_ref[...] = -LR * (m_hat / (jnp.sqrt(v_hat) + EPS) + WD * w_ref[...])
        nm_ref[...] = nm
        nv_ref[...] = nv

    r_specs = [pl.BlockSpec((NDEV, tr, CP), functools.partial(lambda l, i, k: (0, jnp.where(l == k, i, 0), 0), k=k))
               for k in range(L)]
    blk = pl.BlockSpec((None, tr, C), lambda l, i: (l, i, 0))
    shp = jax.ShapeDtypeStruct(w.shape, F32)
    return pl.pallas_call(
        body, name=name, grid=(L, R // tr), in_specs=r_specs + [blk] * 3, out_specs=[blk] * 4, out_shape=[shp] * 4,
        scratch_shapes=[pltpu.VMEM((tr, CP), F32)], compiler_params=_cparams(),
    )(*recvs, w, m, v)


ALPHA = 8.0 ** 0.25


def _rope_fn(sign):
    def fn(x, cos, sin):
        W = x.shape[-1]
        lane = lax.broadcasted_iota(jnp.int32, x.shape, 1)
        first_half = (lane % ROPE) < (ROPE // 2)
        rot = jnp.where(first_half, -pltpu.roll(x, W - ROPE // 2, 1), pltpu.roll(x, ROPE // 2, 1))
        return x * cos + sign * rot * sin
    return fn


def layer_fwd(x, ada3, W, tabs, S):
    cos, sin = tabs
    T = 256
    u = rowwise("modulate", lambda xv, a: xv * (1.0 + a[1:2, :]) + a[0:1, :], S, T,
                [(x, D_MODEL, 0)], [ada3], [(D_MODEL, BF)])[0]
    proj = mm(u, W["in"], name="mm_proj", tm=1024, tn=1024, out_dtype=BF)
    W = {**W, **W["late"](proj)}

    ca = conv_fwd("conv_a_fwd", proj, OFF_A, W["conv_a"], 31, "glu", S, CW)

    def a_post(c, ag, vec):
        n, _ = _ln_stats(c + vec[0:1, :])
        return _silu(n * vec[1:2, :] + vec[2:3, :]) * _silu(ag)

    h_a = rowwise("mix_a_post", a_post, S, T, [(ca, CW, 0), (proj, CW, OFF_AG)], [W["vec_a"]], [(CW, BF)])[0]
    y_a = mm(h_a, W["a_out"], name="mm_branch_out", out_dtype=BF)

    cb = conv_fwd("conv_b_fwd", proj, OFF_B, W["conv_b"], 3, "mul", S, CW)
    h_b = rowwise("mix_b_post", lambda c, gb, bg: gb * c * _silu(bg), S, T,
                  [(cb, CW, 0), (proj, CW, OFF_GB), (proj, CW, OFF_BG)], [], [(CW, BF)])[0]
    y_b = mm(h_b, W["b_out"], name="mm_branch_out", out_dtype=BF)

    def rms2(ql, kvl, gq, gkv):
        rq = lax.rsqrt(jnp.mean(ql * ql, axis=-1, keepdims=True) + RMS_EPS)
        rk = lax.rsqrt(jnp.mean(kvl * kvl, axis=-1, keepdims=True) + RMS_EPS)
        return ql * rq * gq, kvl * rk * gkv

    qn, kvn = rowwise("rms_fwd", rms2, S, T, [(proj, QL, OFF_Q), (proj, KVL, OFF_KV)], [W["gq"], W["gkv"]],
                      [(QL, BF), (KVL, BF)])
    q = mm(qn, W["uq"], name="mm_q")
    kv = mm(kvn, W["ukv"], name="mm_kv", out_dtype=BF)
    rope = _rope_fn(1.0)

    def rope_fwd(qv, kr, c1, s1):
        parts = []
        for g in range(QUADS):
            parts.append(qv[:, g * QW:g * QW + 2 * LANE].astype(BF))
            parts.append(rope(qv[:, g * QW + 2 * LANE:(g + 1) * QW], c1, s1).astype(BF))
        kp = rope(kr, c1, s1)
        kp = kp + pltpu.roll(kp, ROPE, 1) + pltpu.roll(kp, 2 * ROPE, 1) + pltpu.roll(kp, 3 * ROPE, 1)
        return jnp.concatenate(parts, axis=1), kp

    q_b, kpe = rowwise("rope_fwd", rope_fwd, S, T,
                       [(q, HEADS * (NOPE + ROPE), 0), (proj, LANE, OFF_KR), (cos, LANE, 0), (sin, LANE, 0)], [],
                       [(HEADS * (NOPE + ROPE), BF), (LANE, BF)])
    o, lse = attn_fwd(q_b, kv, kpe, S)
    h_c = rowwise("mix_c_post", lambda ov, cg: ov * _silu(cg), S, T, [(o, CW, 0), (proj, CW, OFF_CG)], [],
                  [(CW, BF)])[0]
    y_c = mm(h_c, W["c_out"], name="mm_branch_out", out_dtype=BF)

    def merge(la, lb, lc, ya, yb, yc):
        return _sigmoid(la) * ya + _sigmoid(lb) * yb + _sigmoid(lc) * yc

    m = rowwise("merge_fwd", merge, S, 128,
                [(proj, D_MODEL, 0), (proj, D_MODEL, 1024), (proj, D_MODEL, 2048), (y_a, D_MODEL, 0),
                 (y_b, D_MODEL, 0), (y_c, D_MODEL, 0)], [], [(D_MODEL, BF)])[0]
    out = mm(m, W["o"], name="mm_out")

    def ln_fwd(xv, ov, a, lnv):
        n, _ = _ln_stats(ALPHA * xv + a[2:3, :] * ov)
        return n * lnv[0:1, :] + lnv[1:2, :]

    x_next = rowwise("ln_fwd", ln_fwd, S, 128, [(x, D_MODEL, 0), (out, D_MODEL, 0)], [ada3, W["lnv"]],
                     [(D_MODEL, F32)])[0]
    saved = dict(x=x, u=u, proj=proj, ca=ca, cb=cb, h_a=h_a, h_b=h_b, h_c=h_c, y_a=y_a, y_b=y_b, y_c=y_c, qn=qn,
                 kvn=kvn, q_b=q_b, kv=kv, kpe=kpe, lse=lse, o=o, m=m, out=out)
    return x_next, saved, W


def layer_bwd(dxn, sv, ada3, W, tabs, S, before_in=None):
    cos, sin = tabs
    T = 256
    x, proj = sv["x"], sv["proj"]
    G = {}

    def ln_bwd(xv, ov, dy, a, lnv):
        gate = a[2:3, :]
        n, rstd = _ln_stats(ALPHA * xv + gate * ov)
        dr = _ln_bwd(dy * lnv[0:1, :], n, rstd)
        return ALPHA * dr, gate * dr, _colsum(dy * n), _colsum(dy), _colsum(dr * ov)

    dres, d_out, G["ln_g"], G["ln_b"], d_gate = rowwise(
        "ln_bwd", ln_bwd, S, 128, [(x, D_MODEL, 0), (sv["out"], D_MODEL, 0), (dxn, D_MODEL, 0)], [ada3, W["lnv"]],
        [(D_MODEL, F32), (D_MODEL, BF)], [D_MODEL] * 3)
    dm = mm(d_out, W["o"], name="mm_dm", trans_b=True, out_dtype=BF)
    G["w_o"] = mm(sv["m"], d_out, name="mm_gw_o", trans_a=True, out_dtype=BF)

    def merge_bwd(dmv, la, lb, lc, ya, yb, yc):
        outs, dls = [], []
        for lg, yv in ((la, ya), (lb, yb), (lc, yc)):
            s = _sigmoid(lg)
            outs.append(dmv * s)
            dls.append((dmv * yv * s * (1.0 - s)).astype(BF))
        return (jnp.concatenate(dls, axis=1),) + tuple(outs)

    d_proj, dy_a, dy_b, dy_c = rowwise(
        "merge_bwd", merge_bwd, S, 128,
        [(dm, D_MODEL, 0), (proj, D_MODEL, 0), (proj, D_MODEL, 1024), (proj, D_MODEL, 2048), (sv["y_a"], D_MODEL, 0),
         (sv["y_b"], D_MODEL, 0), (sv["y_c"], D_MODEL, 0)], [], [(3 * D_MODEL, BF)] + [(D_MODEL, BF)] * 3,
        into=(None, NP, OFF_M))

    dh = {}
    for br, dy in (("a", dy_a), ("b", dy_b), ("c", dy_c)):
        dh[br] = mm(dy, W[br + "_out"], name="mm_dh", trans_b=True, out_dtype=BF)
        G["w_%s_out" % br] = mm(sv["h_" + br], dy, name="mm_gw_branch", trans_a=True, out_dtype=BF)

    def a_post_bwd(c, ag, dhv, vec):
        n, rstd = _ln_stats(c + vec[0:1, :])
        z = n * vec[1:2, :] + vec[2:3, :]
        d_ag = dhv * _silu(z) * _dsilu(ag)
        dz = dhv * _silu(ag) * _dsilu(z)
        dc = _ln_bwd(dz * vec[1:2, :], n, rstd)
        return d_ag, dc, _colsum(dc), _colsum(dz * n), _colsum(dz)

    d_proj, dca, G["conv_a_b"], G["ln_a_g"], G["ln_a_b"] = rowwise(
        "mix_a_post_bwd", a_post_bwd, S, T, [(sv["ca"], CW, 0), (proj, CW, OFF_AG), (dh["a"], CW, 0)], [W["vec_a"]],
        [(CW, BF), (CW, F32)], [CW] * 3, into=(d_proj, NP, OFF_AG))
    d_proj, G["conv_a_w"] = conv_bwd("conv_a_bwd", proj, OFF_A, dca, W["conv_a"], 31, "glu", S, CW, d_proj)

    def b_post_bwd(c, gb, bg, dhv):
        sg = _silu(bg)
        d_gb_bg = jnp.concatenate([(dhv * sg * c).astype(BF), (dhv * gb * c * _dsilu(bg)).astype(BF)], axis=1)
        return d_gb_bg, dhv * sg * gb

    d_proj, dcb = rowwise("mix_b_post_bwd", b_post_bwd, S, T,
                          [(sv["cb"], CW, 0), (proj, CW, OFF_GB), (proj, CW, OFF_BG), (dh["b"], CW, 0)], [],
                          [(2 * CW, BF), (CW, F32)], into=(d_proj, NP, OFF_GB))
    d_proj, G["conv_b_w"] = conv_bwd("conv_b_bwd", proj, OFF_B, dcb, W["conv_b"], 3, "mul", S, CW, d_proj)

    d_proj, d_o = rowwise("mix_c_post_bwd", lambda ov, cg, dhv: (dhv * ov * _dsilu(cg), dhv * _silu(cg)), S, T,
                          [(sv["o"], CW, 0), (proj, CW, OFF_CG), (dh["c"], CW, 0)], [], [(CW, BF), (CW, F32)],
                          into=(d_proj, NP, OFF_CG))
    dq, d_kv, dkp_heads = attn_bwd(sv["q_b"], sv["kv"], sv["kpe"], sv["o"], sv["lse"], d_o, S)
    ropeT = _rope_fn(-1.0)

    def rope_bwd(dqv, dkp, c1, s1):
        parts = []
        for g in range(QUADS):
            parts.append(dqv[:, g * QW:g * QW + 2 * LANE].astype(BF))
            parts.append(ropeT(dqv[:, g * QW + 2 * LANE:(g + 1) * QW], c1, s1).astype(BF))
        f = dkp[:, :LANE] + dkp[:, LANE:]
        f = f + pltpu.roll(f, 64, 1)
        f = f + pltpu.roll(f, 32, 1)
        lane = lax.broadcasted_iota(jnp.int32, f.shape, 1)
        return jnp.concatenate(parts, axis=1), jnp.where(lane < ROPE, ropeT(f, c1, s1), 0.0)

    d_q, dk_pe = rowwise("rope_bwd", rope_bwd, S, T,
                         [(dq, HEADS * (NOPE + ROPE), 0), (dkp_heads, HEADS * ROPE, 0), (cos, LANE, 0), (sin, LANE, 0)],
                         [], [(HEADS * (NOPE + ROPE), BF), (LANE, BF)])
    d_qn = mm(d_q, W["uq"], name="mm_dqn", trans_b=True, out_dtype=BF)
    d_kvn = mm(d_kv, W["ukv"], name="mm_dkvn", trans_b=True, out_dtype=BF)
    G["w_uq"] = mm(sv["qn"], d_q, name="mm_gw_uq", trans_a=True, out_dtype=BF)
    G["w_ukv"] = mm(sv["kvn"], d_kv, name="mm_gw_ukv", trans_a=True, out_dtype=BF)

    def rms_bwd(ql, kvl, dqn, dkn, dkp, gq, gkv):
        res = []
        for xv, dy, g in ((ql, dqn, gq), (kvl, dkn, gkv)):
            r = lax.rsqrt(jnp.mean(xv * xv, axis=-1, keepdims=True) + RMS_EPS)
            dxh = dy * g
            res.append(((r * (dxh - xv * (r * r) * jnp.mean(dxh * xv, axis=-1, keepdims=True))).astype(BF),
                        _colsum(dy * xv * r)))
        pad = jnp.zeros((ql.shape[0], LANE), BF)
        return jnp.concatenate([res[1][0], dkp, pad, res[0][0], pad], axis=1), res[0][1], res[1][1]

    d_proj, G["q_norm_g"], G["kv_norm_g"] = rowwise(
        "rms_bwd", rms_bwd, S, T,
        [(proj, QL, OFF_Q), (proj, KVL, OFF_KV), (d_qn, QL, 0), (d_kvn, KVL, 0), (dk_pe, LANE, 0)],
        [W["gq"], W["gkv"]], [(NP - OFF_KV, BF)], [QL, KVL], into=(d_proj, NP, OFF_KV))
    deps = before_in(G) if before_in is not None else ()
    du = mm(d_proj, W["in"], name="mm_du", trans_b=True, tk=1024, deps=deps)
    G["w_in"] = mm(sv["u"], d_proj, name="mm_gw_in", trans_a=True, out_dtype=BF, deps=deps)

    def mod_bwd(duv, xv, dr, a):
        return duv * (1.0 + a[1:2, :]) + dr, _colsum(duv), _colsum(duv * xv)

    dx, d_shift, d_scale = rowwise("mod_bwd", mod_bwd, S, 128, [(du, D_MODEL, 0), (x, D_MODEL, 0), (dres, D_MODEL, 0)],
                                   [ada3], [(D_MODEL, F32)], [D_MODEL] * 2)
    d_ada = jnp.concatenate([d_shift, d_scale, d_gate], axis=1)
    return dx, G, d_ada


SMALL = ("conv_a_b", "ln_a_g", "ln_a_b", "q_norm_g", "kv_norm_g", "ln_g", "ln_b")


def _rows(v):
    n = v.shape[0]
    r = -(-n // (LANE * 16)) * 16
    return jnp.pad(v, (0, r * LANE - n)).reshape(r, LANE)


def kernel(x, c, positions, w_ada, b_ada, w_in, conv_a_w, conv_a_b, ln_a_g, ln_a_b, w_a_out, conv_b_w, w_b_out, q_norm_g, kv_norm_g, w_uq, w_ukv, w_c_out, w_o, ln_g, ln_b, loss_target, m_w_ada, m_b_ada, m_w_in, m_conv_a_w, m_conv_a_b, m_ln_a_g, m_ln_a_b, m_w_a_out, m_conv_b_w, m_w_b_out, m_q_norm_g, m_kv_norm_g, m_w_uq, m_w_ukv, m_w_c_out, m_w_o, m_ln_g, m_ln_b, v_w_ada, v_b_ada, v_w_in, v_conv_a_w, v_conv_a_b, v_ln_a_g, v_ln_a_b, v_w_a_out, v_conv_b_w, v_w_b_out, v_q_norm_g, v_kv_norm_g, v_w_uq, v_w_ukv, v_w_c_out, v_w_o, v_ln_g, v_ln_b):
    P = dict(w_ada=w_ada, b_ada=b_ada, w_in=w_in, conv_a_w=conv_a_w, conv_a_b=conv_a_b, ln_a_g=ln_a_g, ln_a_b=ln_a_b,
             w_a_out=w_a_out, conv_b_w=conv_b_w, w_b_out=w_b_out, q_norm_g=q_norm_g, kv_norm_g=kv_norm_g, w_uq=w_uq,
             w_ukv=w_ukv, w_c_out=w_c_out, w_o=w_o, ln_g=ln_g, ln_b=ln_b)
    Mo = dict(w_ada=m_w_ada, b_ada=m_b_ada, w_in=m_w_in, conv_a_w=m_conv_a_w, conv_a_b=m_conv_a_b, ln_a_g=m_ln_a_g,
              ln_a_b=m_ln_a_b, w_a_out=m_w_a_out, conv_b_w=m_conv_b_w, w_b_out=m_w_b_out, q_norm_g=m_q_norm_g,
              kv_norm_g=m_kv_norm_g, w_uq=m_w_uq, w_ukv=m_w_ukv, w_c_out=m_w_c_out, w_o=m_w_o, ln_g=m_ln_g, ln_b=m_ln_b)
    Vo = dict(w_ada=v_w_ada, b_ada=v_b_ada, w_in=v_w_in, conv_a_w=v_conv_a_w, conv_a_b=v_conv_a_b, ln_a_g=v_ln_a_g,
              ln_a_b=v_ln_a_b, w_a_out=v_w_a_out, conv_b_w=v_conv_b_w, w_b_out=v_w_b_out, q_norm_g=v_q_norm_g,
              kv_norm_g=v_kv_norm_g, w_uq=v_w_uq, w_ukv=v_w_ukv, w_c_out=v_w_c_out, w_o=v_w_o, ln_g=v_ln_g, ln_b=v_ln_b)
    ORDER = ("w_ada", "b_ada", "w_in", "conv_a_w", "conv_a_b", "ln_a_g", "ln_a_b", "w_a_out", "conv_b_w", "w_b_out",
             "q_norm_g", "kv_norm_g", "w_uq", "w_ukv", "w_c_out", "w_o", "ln_g", "ln_b")
    L = w_ada.shape[0]
    S = x.shape[1]
    me = 4 * lax.axis_index("x") + 2 * lax.axis_index("y") + lax.axis_index("c")
    x2 = x[0]
    tgt = loss_target[0]

    small_in = _rows(jnp.concatenate([c.reshape(-1), conv_a_w.reshape(-1), conv_b_w.reshape(-1)]))
    w_in_b = jnp.pad(w_in.astype(BF), ((0, 0), (0, 0), (0, W_IN_PAD - IN_SHARD)))
    misc_b = jnp.concatenate([w_a_out, w_b_out, w_c_out, w_ukv, jnp.pad(w_uq, ((0, 0), (0, 0), (0, LANE - UQ_SHARD)))],
                             axis=1).astype(BF)
    w_o_b = w_o.astype(BF)
    gathered = [None] * L
    pending, _ = exchange_begin("gather0", [w_in_b[0]], 1, small_in)
    pending_rest, _ = exchange_begin("gather0_rest", [misc_b[0], w_o_b[0]], 2, small_in)
    sg = exchange("gather_small", [small_in], [])[0]
    sgf = sg.reshape(NDEV, -1)
    c_all = sgf[:, :D_MODEL]
    o1 = D_MODEL + L * 31 * 64
    conv_a_full = sgf[:, D_MODEL:o1].reshape(NDEV, L, 31, 64).transpose(1, 2, 0, 3).reshape(L, 31, CW)
    conv_b_full = sgf[:, o1:o1 + L * 3 * 64].reshape(NDEV, L, 3, 64).transpose(1, 2, 0, 3).reshape(L, 3, CW)

    c_act = rowwise("silu_c", _silu, 16, 16, [(jnp.pad(c_all, ((0, 8), (0, 0))), D_MODEL, 0)], [], [(D_MODEL, BF)])[0]
    ncol = w_ada.shape[2]
    w_ada_b = w_ada.astype(BF).transpose(1, 0, 2).reshape(D_MODEL, L * ncol)
    b_mine = lax.dynamic_slice_in_dim(b_ada, me * ncol, ncol, axis=1).reshape(1, L * ncol)
    ada_part = mm(c_act, w_ada_b, name="mm_ada", bias=b_mine)
    ada_rows = -(-(L * ncol) // (LANE * 8)) * 8
    ada_send = jnp.pad(ada_part[:NDEV].reshape(NDEV, -1, LANE), ((0, 0), (0, ada_rows - L * ncol // LANE), (0, 0)))
    ada_recv = exchange("a2a_ada", [], [ada_send])[0]
    ada = ada_recv[:, :L * ncol // LANE].reshape(NDEV, L, ncol).transpose(1, 0, 2).reshape(L, 3, D_MODEL)

    inv_freq = ROPE_THETA ** (-jnp.arange(0, ROPE, 2, dtype=F32) / ROPE)
    ang = positions[0].astype(F32)[:, None] * inv_freq
    tabs = (jnp.tile(jnp.cos(ang), (1, 2 * LANE // ROPE)), jnp.tile(jnp.sin(ang), (1, 2 * LANE // ROPE)))

    straight = np.arange(D_MODEL)
    fwd_in = [(0, 0, 0, 0, D_MODEL, _to_gathered(IN_PERM, IN_SHARD, W_IN_PAD))]
    fwd_misc = [(0, ROW_A, 0, 0, CW, straight), (0, ROW_B, 1, 0, CW, straight), (0, ROW_C, 2, 0, CW, straight),
                (0, ROW_UKV, 3, 0, KVL, UKV_PERM), (0, ROW_UQ, 4, 0, QL, _to_gathered(UQ_PERM, UQ_SHARD, LANE))]
    rev_in = [(0, 0, 0, 0, D_MODEL, _from_full(_inverse(IN_PERM, D_IN), IN_SHARD, W_IN_PAD))]
    rev_misc = [(0, 0, 0, ROW_A, CW, straight), (1, 0, 0, ROW_B, CW, straight), (2, 0, 0, ROW_C, CW, straight),
                (3, 0, 0, ROW_UKV, KVL, _from_full(_inverse(UKV_PERM, HEADS * (NOPE + VH)), LANE, LANE)),
                (4, 0, 0, ROW_UQ, QL, _from_full(_inverse(UQ_PERM, HEADS * (NOPE + ROPE)), UQ_SHARD, LANE))]

    def layer_weights(l, deps):
        w_in_p = col_gather("relayout_w_in", [gathered[l][0]], [(D_MODEL, NP)], fwd_in, deps)[0]

        def late(after):
            if len(gathered[l]) == 1:
                gathered[l] += exchange_end(pending_rest, after)
            _, g_misc, g_o = gathered[l]
            a_out, b_out, c_out, ukv, uq = col_gather(
                "relayout_misc", [g_misc],
                [(CW, D_MODEL)] * 3 + [(KVL, HEADS * (NOPE + VH)), (QL, HEADS * (NOPE + ROPE))], fwd_misc, deps)
            return {"a_out": a_out, "b_out": b_out, "c_out": c_out, "uq": uq, "ukv": ukv,
                    "o": g_o.reshape(D_MODEL, D_MODEL)}

        return {
            "in": w_in_p, "late": late,
            "conv_a": jnp.pad(conv_a_full[l], ((0, 1), (0, 0))), "conv_b": jnp.pad(conv_b_full[l], ((0, 5), (0, 0))),
            "vec_a": jnp.stack([conv_a_b[l], ln_a_g[l], ln_a_b[l]]), "gq": q_norm_g[l][None], "gkv": kv_norm_g[l][None],
            "lnv": jnp.stack([ln_g[l], ln_b[l]]),
        }

    h = x2
    saved, weights = [], []
    gathered[0] = exchange_end(pending, ada)
    for l in range(L):
        ada_l, deps = ada[l], ()
        if l + 1 < L:
            pending, token = exchange_begin("gather%d" % (l + 1), [w_in_b[l + 1], misc_b[l + 1], w_o_b[l + 1]], 3,
                                            gathered[l][0])
            ada_l, deps = ada_l + token[0, 0], (token,)
        h, sv, Wl = layer_fwd(h, ada_l, layer_weights(l, deps), tabs, S)
        if l + 1 < L:
            gathered[l + 1] = exchange_end(pending, h)
        saved.append(sv)
        weights.append(Wl)

    def loss_fn(y, t):
        e = y - t
        return e * (1.0 / D_MODEL), _colsum(e * e)

    dy, sq = rowwise("loss", loss_fn, S, 256, [(h, D_MODEL, 0), (tgt, D_MODEL, 0)], [], [(D_MODEL, F32)], [D_MODEL])
    loss = lax.psum(0.5 * jnp.sum(sq) / D_MODEL, ("x", "y", "c"))

    grads, d_adas, recv = [None] * L, [None] * L, [None] * L
    pending, token = None, None

    def send_rest(g):
        send_misc = col_gather("unrelayout_misc", [g["w_a_out"], g["w_b_out"], g["w_c_out"], g["w_ukv"], g["w_uq"]],
                               [(NDEV, MISC_ROWS, LANE)], rev_misc)[0]
        return [send_misc, g["w_o"].reshape(NDEV, D_MODEL // NDEV, D_MODEL)]

    rest0 = []

    def early_rest(g):
        handle, tok = exchange_begin("scatter0_rest", send_rest(g), 0, g["w_o"])
        rest0.append(handle)
        return (tok,)

    for l in reversed(range(L)):
        ada_l = ada[l] if token is None else ada[l] + token[0, 0]
        dy, g, d_adas[l] = layer_bwd(dy, saved[l], ada_l, weights[l], tabs, S, early_rest if l == 0 else None)
        grads[l] = g
        if pending is not None:
            recv[l + 1] = exchange_end(pending, dy)
        send_in = col_gather("unrelayout_w_in", [g["w_in"]], [(NDEV, D_MODEL, W_IN_PAD)], rev_in)[0]
        if l == 0:
            pending, token = exchange_begin("scatter0", [send_in], 0, recv[1][0])
        else:
            pending, token = exchange_begin("scatter%d" % l, [send_in] + send_rest(g), 0,
                                            dy if l + 1 == L else recv[l + 1][0])
    grad_x = dy[None]

    small_parts = [jnp.stack([grads[l][n].reshape(-1) for l in range(L)]).reshape(-1) for n in SMALL]
    small_parts.append(jnp.stack([grads[l]["conv_a_w"][:31].reshape(-1) for l in range(L)]).reshape(-1))
    small_parts.append(jnp.stack([grads[l]["conv_b_w"][:3].reshape(-1) for l in range(L)]).reshape(-1))
    small_parts.append(jnp.stack([d_adas[l].reshape(-1) for l in range(L)]).reshape(-1))
    small_sizes = [int(p.shape[0]) for p in small_parts]
    gsmall = exchange("gather_small_grads", [_rows(jnp.concatenate(small_parts))], [])[0]
    gsum = sum_slots("sum_small", gsmall).reshape(-1)
    recv[0] = [None] + exchange_end(rest0[0], gsum)
    Gr = {}
    offs = np.cumsum([0] + small_sizes)
    for i, n in enumerate(SMALL):
        Gr[n] = gsum[offs[i]:offs[i + 1]].reshape(L, -1)
    ca = gsum[offs[7]:offs[8]].reshape(L, 31, CW)
    cbw = gsum[offs[8]:offs[9]].reshape(L, 3, CW)
    Gr["conv_a_w"] = lax.dynamic_slice_in_dim(ca, me * 64, 64, axis=2)
    Gr["conv_b_w"] = lax.dynamic_slice_in_dim(cbw, me * 64, 64, axis=2)
    Gr["b_ada"] = gsum[offs[9]:offs[10]].reshape(L, 3 * D_MODEL)
    d_ada_all = gsmall.reshape(NDEV, -1)[:, offs[9]:offs[10]].reshape(NDEV, L, 3 * D_MODEL)
    d_mine = lax.dynamic_slice_in_dim(d_ada_all, me * ncol, ncol, axis=2).reshape(NDEV, L * ncol)
    g_ada = mm(c_act, jnp.pad(d_mine, ((0, 8), (0, 0))).astype(BF), name="mm_gw_ada", trans_a=True)
    Gr["w_ada"] = g_ada.reshape(D_MODEL, L, ncol).transpose(1, 0, 2)

    D, NM, NV = {}, {}, {}
    D["w_ada"], NM["w_ada"], NV["w_ada"] = adamw("adamw_w_ada", P["w_ada"], Gr["w_ada"], Mo["w_ada"], Vo["w_ada"])
    Gr["w_o"], D["w_o"], NM["w_o"], NV["w_o"] = sum_adamw(
        "sum_adamw_w_o", [recv[l][2] for l in range(L)], P["w_o"], Mo["w_o"], Vo["w_o"])
    misc = lambda T_: jnp.concatenate([T_["w_a_out"], T_["w_b_out"], T_["w_c_out"], T_["w_ukv"],
                                       jnp.pad(T_["w_uq"], ((0, 0), (0, 0), (0, LANE - UQ_SHARD)))], axis=1)
    res = sum_adamw("sum_adamw_misc", [recv[l][1] for l in range(L)], misc(P), misc(Mo), misc(Vo))
    for T_, r in zip((Gr, D, NM, NV), res):
        T_["w_a_out"], T_["w_b_out"], T_["w_c_out"] = r[:, ROW_A:ROW_B], r[:, ROW_B:ROW_C], r[:, ROW_C:ROW_UKV]
        T_["w_ukv"], T_["w_uq"] = r[:, ROW_UKV:ROW_UQ], r[:, ROW_UQ:MISC_ROWS, :UQ_SHARD]
    recv[0][0] = exchange_end(pending, res[1])[0]
    Gr["w_in"], D["w_in"], NM["w_in"], NV["w_in"] = sum_adamw(
        "sum_adamw_w_in", [recv[l][0] for l in range(L)], P["w_in"], Mo["w_in"], Vo["w_in"])
    packed =("b_ada", "conv_a_w", "conv_b_w") + SMALL
    pk = lambda T_: _rows(jnp.concatenate([T_[n].reshape(-1) for n in packed]))[None]
    dS, mS, vS = adamw("adamw_small", pk(P), pk(Gr), pk(Mo), pk(Vo))
    o = 0
    for n in packed:
        sz = int(np.prod(P[n].shape))
        D[n] = dS.reshape(-1)[o:o + sz].reshape(P[n].shape)
        NM[n] = mS.reshape(-1)[o:o + sz].reshape(P[n].shape)
        NV[n] = vS.reshape(-1)[o:o + sz].reshape(P[n].shape)
        o += sz
    return (loss, grad_x, *[Gr[n] for n in ORDER], *[D[n] for n in ORDER], *[NM[n] for n in ORDER],
            *[NV[n] for n in ORDER])
```

```python
import functools
import math

import numpy as np
import jax
import jax.numpy as jnp
from jax import lax
from jax.experimental import pallas as pl
from jax.experimental.pallas import tpu as pltpu

BF = jnp.bfloat16
F32 = jnp.float32
MESH = pl.DeviceIdType.MESH
NDEV = 8

HEADS, NOPE, ROPE, VH = 8, 64, 32, 64
HP = 128
ROPE_THETA = 10000.0
LN_EPS = 1e-5
RMS_EPS = 1e-6
LR, B1, B2, EPS, WD, STEP = 0.001, 0.9, 0.999, 1e-08, 0.01, 10

LANE = 128
VMEM_LIMIT = 56 * 1024 * 1024

D_MODEL, CW, QL, KVL = 1024, 512, 384, 256
OFF_M, OFF_A, OFF_AG, OFF_B, OFF_CG, OFF_GB, OFF_BG = 0, 3072, 4096, 4608, 5632, 6144, 6656
OFF_KV, OFF_KR, OFF_Q, NP = 7168, 7424, 7680, 8192
D_IN = 7840


def _cparams(**kw):
    return pltpu.CompilerParams(vmem_limit_bytes=VMEM_LIMIT, **kw)


def _sigmoid(x):
    return jax.nn.sigmoid(x)


def _silu(x):
    return x * _sigmoid(x)


def _dsilu(x):
    s = _sigmoid(x)
    return s * (1.0 + x * (1.0 - s))


def _pick_tile(n, cap, mult):
    if n <= cap:
        return n
    for t in range(cap - cap % mult, 0, -mult):
        if n % t == 0:
            return t
    raise ValueError((n, cap, mult))


def mm(a, b, *, name, trans_a=False, trans_b=False, out_dtype=F32, bias=None, tm=1024, tn=1024, tk=2048, deps=()):
    if trans_a:
        K, M = a.shape
    else:
        M, K = a.shape
    if trans_b:
        N, K2 = b.shape
    else:
        K2, N = b.shape
    assert K == K2 and not (trans_a and trans_b), (a.shape, b.shape)
    tm, tn = _pick_tile(M, tm, 16), _pick_tile(N, tn, LANE)
    tk = _pick_tile(K, tk, LANE if trans_b else 16)
    assert M % tm == 0 and N % tn == 0 and K % tk == 0, (M, N, K, tm, tn, tk)
    nk = K // tk
    dims = (((0 if trans_a else 1,), (1 if trans_b else 0,)), ((), ()))
    has_bias = bias is not None

    def body(*refs):
        a_ref, b_ref = refs[0], refs[1]
        bias_ref = refs[2] if has_bias else None
        o_ref = refs[(3 if has_bias else 2) + len(deps)]
        p = lax.dot_general(a_ref[...], b_ref[...], dims, preferred_element_type=F32)

        def finish(v):
            if has_bias:
                v = v + bias_ref[...]
            o_ref[...] = v.astype(o_ref.dtype)

        if nk == 1:
            finish(p)
        else:
            acc = refs[-1]
            k = pl.program_id(2)

            @pl.when(k == 0)
            def _():
                acc[...] = p

            @pl.when(k > 0)
            def _():
                acc[...] += p

            @pl.when(k == nk - 1)
            def _():
                finish(acc[...])

    if trans_a:
        a_spec = pl.BlockSpec((tk, tm), lambda i, j, k: (k, i))
    else:
        a_spec = pl.BlockSpec((tm, tk), lambda i, j, k: (i, k))
    if trans_b:
        b_spec = pl.BlockSpec((tn, tk), lambda i, j, k: (j, k))
    else:
        b_spec = pl.BlockSpec((tk, tn), lambda i, j, k: (k, j))
    in_specs = [a_spec, b_spec]
    args = [a, b]
    if has_bias:
        in_specs.append(pl.BlockSpec((1, tn), lambda i, j, k: (0, j)))
        args.append(bias)
    in_specs += [ANY_SPEC] * len(deps)
    args += list(deps)
    return pl.pallas_call(
        body, name=name, grid=(M // tm, N // tn, nk),
        in_specs=in_specs, out_specs=pl.BlockSpec((tm, tn), lambda i, j, k: (i, j)),
        out_shape=jax.ShapeDtypeStruct((M, N), out_dtype),
        scratch_shapes=[pltpu.VMEM((tm, tn), F32)] if nk > 1 else [],
        compiler_params=_cparams(),
    )(*args)


def rowwise(name, fn, S, T, row_ins, full_ins, row_outs, acc_outs=(), into=None):
    n_in = len(row_ins) + len(full_ins)
    n_ro, n_ao = len(row_outs), len(acc_outs)
    alias = into is not None and into[0] is not None

    def body(*refs):
        vals = [r[...] for r in refs[:n_in]]
        vals = [v.astype(F32) if v.dtype == BF else v for v in vals]
        outs = fn(*vals)
        if not isinstance(outs, (tuple, list)):
            outs = (outs,)
        assert len(outs) == n_ro + n_ao, (name, len(outs))
        o0 = n_in + (1 if alias else 0)
        for r, v in zip(refs[o0:o0 + n_ro], outs[:n_ro]):
            r[...] = v.astype(r.dtype)
        first = pl.program_id(0) == 0
        for r, v in zip(refs[o0 + n_ro:], outs[n_ro:]):
            def init(r=r, v=v):
                r[...] = v

            def accum(r=r, v=v):
                r[...] += v

            pl.when(first)(init)
            pl.when(jnp.logical_not(first))(accum)

    in_specs, args = [], []
    for arr, W, off in row_ins:
        assert off % W == 0 and arr.shape[0] == S, (name, arr.shape, W, off)
        in_specs.append(pl.BlockSpec((T, W), functools.partial(lambda i, cb: (i, cb), cb=off // W)))
        args.append(arr)
    for arr in full_ins:
        in_specs.append(pl.BlockSpec(arr.shape, lambda i: (0, 0)))
        args.append(arr)
    out_specs = [pl.BlockSpec((T, W), lambda i: (i, 0)) for W, _ in row_outs]
    out_shape = [jax.ShapeDtypeStruct((S, W), dt) for W, dt in row_outs]
    aliases = {}
    if into is not None:
        buf, total, off = into
        W0, dt0 = row_outs[0]
        assert off % W0 == 0
        out_specs[0] = pl.BlockSpec((T, W0), functools.partial(lambda i, cb: (i, cb), cb=off // W0))
        out_shape[0] = jax.ShapeDtypeStruct((S, total), dt0)
        if alias:
            in_specs.append(ANY_SPEC)
            args.append(buf)
            aliases = {n_in: 0}
    out_specs += [pl.BlockSpec((1, W), lambda i: (0, 0)) for W in acc_outs]
    out_shape += [jax.ShapeDtypeStruct((1, W), F32) for W in acc_outs]
    return pl.pallas_call(
        body, name=name, grid=(S // T,), in_specs=in_specs, out_specs=out_specs, out_shape=out_shape,
        input_output_aliases=aliases, compiler_params=_cparams(),
    )(*args)


def _colsum(v):
    return jnp.sum(v, axis=0, keepdims=True)


def _ln_stats(r):
    mu = jnp.mean(r, axis=-1, keepdims=True)
    d = r - mu
    var = jnp.mean(d * d, axis=-1, keepdims=True)
    rstd = lax.rsqrt(var + LN_EPS)
    return d * rstd, rstd


def _ln_bwd(dn, n, rstd):
    return rstd * (dn - jnp.mean(dn, axis=-1, keepdims=True) - n * jnp.mean(dn * n, axis=-1, keepdims=True))


CPAD = 32
TC = 64


def _pre(mode, x1, x2):
    return x1 * _sigmoid(x2) if mode == "glu" else x1 * x2


def _shifted(ext, sft):
    n = TC + CPAD
    return pltpu.roll(ext, (n - sft) % n, 0)[0:TC]


def _interleaved_specs(S, off):
    return [pl.BlockSpec((S, LANE), functools.partial(lambda j, o: (0, o + 2 * j), o=off // LANE)),
            pl.BlockSpec((S, LANE), functools.partial(lambda j, o: (0, o + 2 * j + 1), o=off // LANE))]


def conv_fwd(name, src, off, w_pad, taps, mode, S, C):
    nchunk = S // TC

    def body(x1_ref, x2_ref, w_ref, o_ref, a_pad):
        a_pad[0:CPAD, :] = jnp.zeros((CPAD, LANE), F32)

        def fill(i, _):
            r = pl.multiple_of(i * 256, 256)
            a_pad[pl.ds(CPAD + r, 256), :] = _pre(mode, x1_ref[pl.ds(r, 256), :].astype(F32),
                                                  x2_ref[pl.ds(r, 256), :].astype(F32))
            return 0

        lax.fori_loop(0, S // 256, fill, 0)

        def chunk(i, _):
            base = pl.multiple_of(i * TC, TC)
            ext = a_pad[pl.ds(base, TC + CPAD), :]
            acc = jnp.zeros((TC, LANE), F32)
            for k in range(taps):
                acc = acc + w_ref[pl.ds(k, 1), :] * _shifted(ext, CPAD - (taps - 1) + k)
            o_ref[pl.ds(base, TC), :] = acc
            return 0

        lax.fori_loop(0, nchunk, chunk, 0)

    kp = w_pad.shape[0]
    return pl.pallas_call(
        body, name=name, grid=(C // LANE,),
        in_specs=_interleaved_specs(S, off) + [pl.BlockSpec((kp, LANE), lambda j: (0, j))],
        out_specs=pl.BlockSpec((S, LANE), lambda j: (0, j)),
        out_shape=jax.ShapeDtypeStruct((S, C), F32),
        scratch_shapes=[pltpu.VMEM((S + CPAD, LANE), F32)],
        compiler_params=_cparams(),
    )(src, src, w_pad)


def conv_bwd(name, src, off, dc, w_pad, taps, mode, S, C, buf):
    nchunk = S // TC
    kp = w_pad.shape[0]

    def body(x1_ref, x2_ref, dc_ref, w_ref, _, d_ref, dw_ref, a_pad, dc_pad, dw_acc):
        a_pad[0:CPAD, :] = jnp.zeros((CPAD, LANE), F32)
        dc_pad[S:S + CPAD, :] = jnp.zeros((CPAD, LANE), F32)
        dw_acc[...] = jnp.zeros(dw_acc.shape, F32)

        def fill(i, _):
            r = pl.multiple_of(i * 256, 256)
            a_pad[pl.ds(CPAD + r, 256), :] = _pre(mode, x1_ref[pl.ds(r, 256), :].astype(F32),
                                                  x2_ref[pl.ds(r, 256), :].astype(F32))
            dc_pad[pl.ds(r, 256), :] = dc_ref[pl.ds(r, 256), :]
            return 0

        lax.fori_loop(0, S // 256, fill, 0)

        def chunk(i, _):
            base = pl.multiple_of(i * TC, TC)
            ext_d = dc_pad[pl.ds(base, TC + CPAD), :]
            ext_a = a_pad[pl.ds(base, TC + CPAD), :]
            dcv = ext_d[0:TC]
            da = jnp.zeros((TC, LANE), F32)
            for k in range(taps):
                da = da + w_ref[pl.ds(k, 1), :] * _shifted(ext_d, taps - 1 - k)
                prod = dcv * _shifted(ext_a, CPAD - (taps - 1) + k)
                fold = prod[0:8]
                for g in range(1, TC // 8):
                    fold = fold + prod[8 * g:8 * g + 8]
                dw_acc[pl.ds(8 * k, 8), :] += fold
            x1 = x1_ref[pl.ds(base, TC), :].astype(F32)
            x2 = x2_ref[pl.ds(base, TC), :].astype(F32)
            if mode == "glu":
                s = _sigmoid(x2)
                d1, d2 = da * s, da * x1 * s * (1.0 - s)
            else:
                d1, d2 = da * x2, da * x1
            d_ref[pl.ds(base, TC), 0:LANE] = d1.astype(BF)
            d_ref[pl.ds(base, TC), LANE:2 * LANE] = d2.astype(BF)
            return 0

        lax.fori_loop(0, nchunk, chunk, 0)
        dw_ref[...] = jnp.zeros(dw_ref.shape, F32)
        for k in range(taps):
            dw_ref[pl.ds(k, 1), :] = jnp.sum(dw_acc[pl.ds(8 * k, 8), :], axis=0, keepdims=True)

    blk = pl.BlockSpec((S, LANE), lambda j: (0, j))
    return pl.pallas_call(
        body, name=name, grid=(C // LANE,),
        in_specs=_interleaved_specs(S, off) + [blk, pl.BlockSpec((kp, LANE), lambda j: (0, j)), ANY_SPEC],
        out_specs=[pl.BlockSpec((S, 2 * LANE), functools.partial(lambda j, o: (0, o + j), o=off // (2 * LANE))),
                   pl.BlockSpec((kp, LANE), lambda j: (0, j))],
        out_shape=[jax.ShapeDtypeStruct(buf.shape, BF), jax.ShapeDtypeStruct((kp, C), F32)],
        input_output_aliases={4: 0},
        scratch_shapes=[pltpu.VMEM((S + CPAD, LANE), F32), pltpu.VMEM((S + CPAD, LANE), F32),
                        pltpu.VMEM((8 * kp, LANE), F32)],
        compiler_params=_cparams(),
    )(src, src, dc, w_pad, buf)


FWD_TILES = (512, 512)
BWD_TILES = (512, 512)
QUADS = HEADS // 4
QW, KVW = 4 * (NOPE + ROPE), 4 * (NOPE + VH)
SCALE = (NOPE + ROPE) ** -0.5
NT_DIMS = (((1,), (1,)), ((), ()))
TN_DIMS = (((0,), (0,)), ((), ()))


def _lane_mask(width, group, dtype):
    lane = lax.broadcasted_iota(jnp.int32, (1, LANE), 1)
    return jnp.where(lane // width == group, 1.0, 0.0).astype(dtype)


def _visible(tq, tk, off):
    row = lax.broadcasted_iota(jnp.int32, (tq, tk), 0)
    col = lax.broadcasted_iota(jnp.int32, (tq, tk), 1)
    return col <= row + off


def _attn_tiles(S, tq, tk):
    tk = tk if S % tk == 0 else 256
    return min(tq, tk), tk


def attn_fwd(q, kv, kpe, S):
    tq, tk = _attn_tiles(S, *FWD_TILES)
    nq = S // tq

    def body(q_ref, kv_ref, kp_ref, o_ref, lse_ref):
        for t in range(2):
            cols = slice(t * LANE, (t + 1) * LANE)
            for hh in range(2):
                def q_block(qi, _, t=t, hh=hh, cols=cols):
                    r0 = pl.multiple_of(qi * tq, tq)
                    qcat = jnp.concatenate([q_ref[pl.ds(r0, tq), cols] * _lane_mask(NOPE, hh, BF),
                                            q_ref[pl.ds(r0, tq), 2 * LANE:3 * LANE] * _lane_mask(ROPE, 2 * t + hh, BF)],
                                           axis=1)
                    nfull = (qi * tq) // tk

                    def step(kj, carry, masked):
                        m, l, acc = carry
                        c0 = pl.multiple_of(kj * tk, tk)
                        kc = jnp.concatenate([kv_ref[pl.ds(c0, tk), cols], kp_ref[pl.ds(c0, tk), :]], axis=1)
                        vt = kv_ref[pl.ds(c0, tk), (2 + t) * LANE:(3 + t) * LANE]
                        s = lax.dot_general(qcat, kc, NT_DIMS, preferred_element_type=F32) * SCALE
                        if masked:
                            s = jnp.where(_visible(tq, tk, qi * tq - nfull * tk), s, -jnp.inf)
                        m_new = jnp.maximum(m, jnp.max(s, axis=-1, keepdims=True))
                        p = jnp.exp(s - m_new)
                        alpha = jnp.exp(m - m_new)
                        l = alpha * l + jnp.sum(p, axis=-1, keepdims=True)
                        acc = alpha * acc + jnp.dot(p.astype(BF), vt, preferred_element_type=F32)
                        return m_new, l, acc

                    init = (jnp.full((tq, 1), -jnp.inf, F32), jnp.zeros((tq, 1), F32), jnp.zeros((tq, LANE), F32))
                    carry = lax.fori_loop(0, nfull, lambda kj, c: step(kj, c, False), init)
                    m, l, acc = step(nfull, carry, True)
                    mine = _lane_mask(NOPE, hh, F32)
                    if hh == 0:
                        o_ref[pl.ds(r0, tq), cols] = (acc / l) * mine
                        lse_ref[pl.ds(r0, tq), cols] = (m + jnp.log(l)) * mine
                    else:
                        o_ref[pl.ds(r0, tq), cols] += (acc / l) * mine
                        lse_ref[pl.ds(r0, tq), cols] += (m + jnp.log(l)) * mine
                    return 0

                lax.fori_loop(0, nq, q_block, 0)

    return pl.pallas_call(
        body, name="attn_fwd", grid=(QUADS,),
        in_specs=[pl.BlockSpec((S, QW), lambda g: (0, g)), pl.BlockSpec((S, KVW), lambda g: (0, g)),
                  pl.BlockSpec((S, LANE), lambda g: (0, 0))],
        out_specs=[pl.BlockSpec((S, 2 * LANE), lambda g: (0, g))] * 2,
        out_shape=[jax.ShapeDtypeStruct((S, HEADS * VH), F32)] * 2,
        compiler_params=_cparams(),
    )(q, kv, kpe)


def attn_bwd(q, kv, kpe, o, lse, do, S):
    tq, tk = _attn_tiles(S, *BWD_TILES)
    nq = S // tq

    def body(q_ref, kv_ref, kp_ref, o_ref, lse_ref, do_ref, dq_ref, dkv_ref, dkp_ref, dq_acc, dk_acc, dv_acc):
        for t in range(2):
            cols = slice(t * LANE, (t + 1) * LANE)
            dk_acc[...] = jnp.zeros(dk_acc.shape, F32)
            dv_acc[...] = jnp.zeros(dv_acc.shape, F32)
            for hh in range(2):
                def q_block(qi, _, t=t, hh=hh, cols=cols):
                    r0 = pl.multiple_of(qi * tq, tq)
                    mine = _lane_mask(NOPE, hh, F32)
                    qcat = jnp.concatenate([q_ref[pl.ds(r0, tq), cols] * _lane_mask(NOPE, hh, BF),
                                            q_ref[pl.ds(r0, tq), 2 * LANE:3 * LANE] * _lane_mask(ROPE, 2 * t + hh, BF)],
                                           axis=1)
                    dof = do_ref[pl.ds(r0, tq), cols] * mine
                    dob = dof.astype(BF)
                    delta = jnp.sum(dof * o_ref[pl.ds(r0, tq), cols], axis=-1, keepdims=True)
                    lse_h = lse_ref[pl.ds(r0, tq), cols][:, hh * NOPE:hh * NOPE + 1]
                    nfull = (qi * tq) // tk
                    dq_acc[...] = jnp.zeros(dq_acc.shape, F32)

                    def step(kj, _, masked):
                        c0 = pl.multiple_of(kj * tk, tk)
                        kc = jnp.concatenate([kv_ref[pl.ds(c0, tk), cols], kp_ref[pl.ds(c0, tk), :]], axis=1)
                        vt = kv_ref[pl.ds(c0, tk), (2 + t) * LANE:(3 + t) * LANE]
                        s = lax.dot_general(qcat, kc, NT_DIMS, preferred_element_type=F32) * SCALE
                        if masked:
                            s = jnp.where(_visible(tq, tk, qi * tq - nfull * tk), s, -jnp.inf)
                        p = jnp.exp(s - lse_h)
                        dp = lax.dot_general(dob, vt, NT_DIMS, preferred_element_type=F32)
                        ds = (p * (dp - delta) * SCALE).astype(BF)
                        dv_acc[pl.ds(c0, tk), :] += lax.dot_general(p.astype(BF), dob, TN_DIMS,
                                                                    preferred_element_type=F32)
                        dk_acc[pl.ds(c0, tk), :] += lax.dot_general(ds, qcat, TN_DIMS, preferred_element_type=F32)
                        dq_acc[...] += jnp.dot(ds, kc, preferred_element_type=F32)
                        return 0

                    lax.fori_loop(0, nfull, lambda kj, c: step(kj, c, False), 0)
                    step(nfull, 0, True)
                    d = dq_acc[...]
                    pe = d[:, LANE:] * _lane_mask(ROPE, 2 * t + hh, F32)
                    if hh == 0:
                        dq_ref[pl.ds(r0, tq), cols] = d[:, :LANE] * mine
                    else:
                        dq_ref[pl.ds(r0, tq), cols] += d[:, :LANE] * mine
                    if t == 0 and hh == 0:
                        dq_ref[pl.ds(r0, tq), 2 * LANE:3 * LANE] = pe
                    else:
                        dq_ref[pl.ds(r0, tq), 2 * LANE:3 * LANE] += pe
                    return 0

                lax.fori_loop(0, nq, q_block, 0)
            dkv_ref[:, t * LANE:(t + 1) * LANE] = dk_acc[:, :LANE].astype(BF)
            dkv_ref[:, (2 + t) * LANE:(3 + t) * LANE] = dv_acc[...].astype(BF)
            if t == 0:
                dkp_ref[...] = dk_acc[:, LANE:]
            else:
                dkp_ref[...] += dk_acc[:, LANE:]

    qspec = pl.BlockSpec((S, QW), lambda g: (0, g))
    kvspec = pl.BlockSpec((S, KVW), lambda g: (0, g))
    ospec = pl.BlockSpec((S, 2 * LANE), lambda g: (0, g))
    return pl.pallas_call(
        body, name="attn_bwd", grid=(QUADS,),
        in_specs=[qspec, kvspec, pl.BlockSpec((S, LANE), lambda g: (0, 0)), ospec, ospec, ospec],
        out_specs=[qspec, kvspec, pl.BlockSpec((S, LANE), lambda g: (0, g))],
        out_shape=[jax.ShapeDtypeStruct((S, HEADS * (NOPE + ROPE)), F32), jax.ShapeDtypeStruct((S, HEADS * (NOPE + VH)), BF),
                   jax.ShapeDtypeStruct((S, HEADS * ROPE), F32)],
        scratch_shapes=[pltpu.VMEM((tq, 2 * LANE), F32), pltpu.VMEM((S, 2 * LANE), F32), pltpu.VMEM((S, LANE), F32)],
        compiler_params=_cparams(),
    )(q, kv, kpe, o, lse, do)


def exchange(name, gathers, a2as):
    n_g, n = len(gathers), len(gathers) + len(a2as)

    def body(*refs):
        ins, outs = refs[:n], refs[n:2 * n]
        send_sems, recv_sems, loc_sems = refs[2 * n:]
        x, y, c = lax.axis_index("x"), lax.axis_index("y"), lax.axis_index("c")
        me = 4 * x + 2 * y + c

        def peer(k):
            px = 1 - x if k & 4 else x
            py = 1 - y if k & 2 else y
            pc = 1 - c if k & 1 else c
            return (px, py, pc), 4 * px + 2 * py + pc

        def remote(a, k):
            pid, pflat = peer(k)
            src = ins[a] if a < n_g else ins[a].at[pflat]
            return pltpu.make_async_remote_copy(
                src_ref=src, dst_ref=outs[a].at[me], send_sem=send_sems.at[a, k - 1], recv_sem=recv_sems.at[a, k - 1],
                device_id=pid, device_id_type=MESH)

        def arrival(a, k):
            pid, pflat = peer(k)
            src = ins[a] if a < n_g else ins[a].at[pflat]
            return pltpu.make_async_remote_copy(
                src_ref=src, dst_ref=outs[a].at[pflat], send_sem=send_sems.at[a, k - 1], recv_sem=recv_sems.at[a, k - 1],
                device_id=pid, device_id_type=MESH)

        local = []
        for a in range(n):
            own = ins[a] if a < n_g else ins[a].at[me]
            cp = pltpu.make_async_copy(own, outs[a].at[me], loc_sems.at[a])
            cp.start()
            local.append(cp)
        sent = []
        for k in (1, 2, 4, 3, 5, 6, 7):
            for a in range(n):
                cp = remote(a, k)
                cp.start()
                sent.append(cp)
        for k in range(1, 8):
            for a in range(n):
                arrival(a, k).wait_recv()
        for cp in sent:
            cp.wait_send()
        for cp in local:
            cp.wait()

    out_shape = [jax.ShapeDtypeStruct((NDEV,) + g.shape, g.dtype) for g in gathers]
    out_shape += [jax.ShapeDtypeStruct(a.shape, a.dtype) for a in a2as]
    any_spec = pl.BlockSpec(memory_space=pl.ANY)
    return pl.pallas_call(
        body, name=name, in_specs=[any_spec] * n, out_specs=[any_spec] * n, out_shape=out_shape,
        scratch_shapes=[pltpu.SemaphoreType.DMA((n, NDEV - 1)), pltpu.SemaphoreType.DMA((n, NDEV - 1)),
                        pltpu.SemaphoreType.DMA((n,))],
    )(*gathers, *a2as)


def _peer(k, x, y, c):
    px = 1 - x if k & 4 else x
    py = 1 - y if k & 2 else y
    pc = 1 - c if k & 1 else c
    return (px, py, pc), 4 * px + 2 * py + pc


PEER_ORDER = (1, 2, 4, 3, 5, 6, 7)
HBM_SPEC = pl.BlockSpec(memory_space=pltpu.HBM)
SEM_SPEC = pl.BlockSpec(memory_space=pltpu.SEMAPHORE)
ANY_SPEC = pl.BlockSpec(memory_space=pl.ANY)


def _split_copies(ins, lands, n_g, send_sems, recv_sems):
    x, y, c = lax.axis_index("x"), lax.axis_index("y"), lax.axis_index("c")
    me = 4 * x + 2 * y + c

    def outgoing(a, k):
        pid, pflat = _peer(k, x, y, c)
        src = ins[a] if a < n_g else ins[a].at[pflat]
        return pltpu.make_async_remote_copy(
            src_ref=src, dst_ref=lands[a].at[me], send_sem=send_sems.at[a * (NDEV - 1) + k - 1],
            recv_sem=recv_sems.at[a * (NDEV - 1) + k - 1],
            device_id=pid, device_id_type=MESH)

    def arrival(a, k):
        pid, pflat = _peer(k, x, y, c)
        src = ins[a] if a < n_g else ins[a].at[pflat]
        return pltpu.make_async_remote_copy(
            src_ref=src, dst_ref=lands[a].at[pflat], send_sem=send_sems.at[a * (NDEV - 1) + k - 1],
            recv_sem=recv_sems.at[a * (NDEV - 1) + k - 1],
            device_id=pid, device_id_type=MESH)

    return outgoing, arrival


def exchange_begin(name, srcs, n_g, dep):
    n = len(srcs)
    land_shapes = [((NDEV,) + s.shape) if a < n_g else s.shape for a, s in enumerate(srcs)]

    def own_body(*refs):
        ins, outs = refs[:n], refs[n + 1:2 * n + 1]
        stage, sems = refs[2 * n + 1:3 * n + 1], refs[-1]
        me = 4 * lax.axis_index("x") + 2 * lax.axis_index("y") + lax.axis_index("c")
        cps = [pltpu.make_async_copy(ins[a] if a < n_g else ins[a].at[me], stage[a], sems.at[a]) for a in range(n)]
        for cp in cps:
            cp.start()
        for cp in cps:
            cp.wait()
        cps = [pltpu.make_async_copy(stage[a], outs[a].at[me], sems.at[a]) for a in range(n)]
        for cp in cps:
            cp.start()
        for cp in cps:
            cp.wait()

    lands = pl.pallas_call(
        own_body, name=name + "_own", in_specs=[ANY_SPEC] * (n + 1), out_specs=[ANY_SPEC] * n,
        out_shape=[jax.ShapeDtypeStruct(sh, s.dtype) for sh, s in zip(land_shapes, srcs)],
        scratch_shapes=[pltpu.VMEM(sh[1:], s.dtype) for sh, s in zip(land_shapes, srcs)] + [pltpu.SemaphoreType.DMA((n,))],
        compiler_params=_cparams(),
    )(*srcs, dep)

    def start_body(*refs):
        ins, lz = refs[:n], refs[n:2 * n]
        send_sems, recv_sems, token = refs[2 * n], refs[2 * n + 1], refs[-1]
        outgoing, _ = _split_copies(ins, lz, n_g, send_sems, recv_sems)
        for k in PEER_ORDER:
            for a in range(n):
                outgoing(a, k).start()
        token[...] = jnp.zeros(token.shape, F32)

    hbm = lambda t: pltpu.HBM(t.shape, t.dtype)
    res = pl.pallas_call(
        start_body, name=name + "_start",
        out_shape=(pltpu.SemaphoreType.DMA((n * (NDEV - 1),)), pltpu.SemaphoreType.DMA((n * (NDEV - 1),)),
                   *[hbm(s) for s in srcs], *[hbm(t) for t in lands], jax.ShapeDtypeStruct((8, LANE), F32)),
        in_specs=[HBM_SPEC] * (2 * n),
        out_specs=(SEM_SPEC, SEM_SPEC, *[HBM_SPEC] * (2 * n), pl.BlockSpec(memory_space=pltpu.VMEM)),
        input_output_aliases={i: 2 + i for i in range(2 * n)},
        compiler_params=pltpu.CompilerParams(has_side_effects=pltpu.SideEffectType.DATAFLOW_SIDE_EFFECTING),
    )(*[pltpu.with_memory_space_constraint(t, pltpu.HBM) for t in list(srcs) + list(lands)])
    return (name, n, n_g, res[:-1]), res[-1]


def exchange_end(handle, after):
    name, n, n_g, (send_sems, recv_sems, *bufs) = handle

    def wait_body(*refs):
        ins, lz = refs[:n], refs[n:2 * n]
        ss, rs = refs[2 * n], refs[2 * n + 1]
        outgoing, arrival = _split_copies(ins, lz, n_g, ss, rs)
        for k in range(1, NDEV):
            for a in range(n):
                arrival(a, k).wait_recv()
        for k in range(1, NDEV):
            for a in range(n):
                outgoing(a, k).wait_send()

    res = pl.pallas_call(
        wait_body, name=name + "_wait", out_shape=tuple(pltpu.HBM(t.shape, t.dtype) for t in bufs),
        in_specs=[HBM_SPEC] * (2 * n) + [SEM_SPEC, SEM_SPEC, ANY_SPEC], out_specs=[HBM_SPEC] * (2 * n),
        input_output_aliases={i: i for i in range(2 * n)},
        compiler_params=pltpu.CompilerParams(has_side_effects=pltpu.SideEffectType.DATAFLOW_SIDE_EFFECTING),
    )(*bufs, send_sems, recv_sems, after)
    return list(res[n:])


def _pick_rows(R, mult, cap):
    best = None
    for n in range(1, R + 1):
        if R % n == 0 and (R // n) % mult == 0 and R // n <= cap:
            best = R // n
            break
    assert best is not None, (R, mult, cap)
    return best


def sum_slots(name, x):
    _, R, _ = x.shape
    tr = _pick_rows(R, 16, 2304)

    def body(x_ref, o_ref):
        acc = x_ref[0].astype(F32)
        for d in range(1, NDEV):
            acc = acc + x_ref[d].astype(F32)
        o_ref[...] = acc

    return pl.pallas_call(
        body, name=name, grid=(R // tr,),
        in_specs=[pl.BlockSpec((NDEV, tr, LANE), lambda i: (0, i, 0))],
        out_specs=pl.BlockSpec((tr, LANE), lambda i: (i, 0)),
        out_shape=jax.ShapeDtypeStruct((R, LANE), F32), compiler_params=_cparams(),
    )(x)


def adamw(name, w, g, m, v):
    L, R, C = w.shape
    tr = _pick_rows(R, 8, 256) if R % 8 == 0 else R

    def body(w_ref, g_ref, m_ref, v_ref, d_ref, nm_ref, nv_ref):
        gg = g_ref[...]
        nm = B1 * m_ref[...] + (1.0 - B1) * gg
        nv = B2 * v_ref[...] + (1.0 - B2) * jnp.square(gg)
        m_hat = nm / (1.0 - B1 ** STEP)
        v_hat = nv / (1.0 - B2 ** STEP)
        d_ref[...] = -LR * (m_hat / (jnp.sqrt(v_hat) + EPS) + WD * w_ref[...])
        nm_ref[...] = nm
        nv_ref[...] = nv

    blk = pl.BlockSpec((1, tr, C), lambda l, i: (l, i, 0))
    shp = jax.ShapeDtypeStruct(w.shape, F32)
    return pl.pallas_call(
        body, name=name, grid=(L, R // tr), in_specs=[blk] * 4, out_specs=[blk] * 3, out_shape=[shp] * 3,
        compiler_params=_cparams(),
    )(w, g, m, v)


IN_SHARD = D_IN // NDEV
UQ_SHARD = HEADS * (NOPE + ROPE) // NDEV
W_IN_PAD = 1024
ROW_A, ROW_B, ROW_C, ROW_UKV, ROW_UQ, MISC_ROWS = 0, 512, 1024, 1536, 1792, 2176


def _in_perm_index():
    ar = np.arange
    z = lambda n: np.full((n,), -1, np.int64)
    mix = lambda lo1, lo2: np.concatenate([ar(lo + LANE * j, lo + LANE * (j + 1)) for j in range(CW // LANE)
                                           for lo in (lo1, lo2)])
    return np.concatenate([ar(4768, 7840), mix(0, 512), ar(1024, 1536), mix(1536, 2560), ar(4256, 4768), ar(2048, 2560),
                           ar(3072, 3584), ar(3968, 4224), ar(4224, 4256), z(OFF_Q - OFF_KR - ROPE), ar(3584, 3968),
                           z(NP - OFF_Q - QL)])


def _head_perm_index(a, b):
    parts = []
    for g in range(QUADS):
        h = np.arange(4 * g, 4 * g + 4)[:, None] * (a + b)
        parts += [(h + np.arange(a)[None]).reshape(-1), (h + a + np.arange(b)[None]).reshape(-1)]
    return np.concatenate(parts)


def _inverse(perm, n):
    inv = np.full((n,), -1, np.int64)
    inv[perm[perm >= 0]] = np.nonzero(perm >= 0)[0]
    return inv


IN_PERM = _in_perm_index()
UQ_PERM = _head_perm_index(NOPE, ROPE)
UKV_PERM = _head_perm_index(NOPE, VH)


def _to_gathered(perm, shard, pad):
    return np.where(perm >= 0, (perm // shard) * pad + perm % shard, -1)


def _from_full(inv, shard, pad):
    j, i = np.divmod(np.arange(NDEV * pad), pad)
    return np.where(i < shard, inv[np.minimum(j * shard + i, inv.shape[0] - 1)], -1)


def col_gather(name, srcs, out_shapes, jobs, deps=()):
    ns, nj, nd, no = len(srcs), len(jobs), len(deps), len(out_shapes)
    tables = [jnp.asarray(np.asarray(job[5], np.int32)[None, :]) for job in jobs]

    def view(ref, col0, width, r0, rc):
        n = ref.shape[-1]
        if len(ref.shape) == 3:
            return ref.at[col0 // n, pl.ds(r0, rc), pl.ds(col0 % n, width)]
        return ref.at[pl.ds(r0, rc), pl.ds(col0, width)]

    def slabs(shape):
        if len(shape) == 3:
            return [((d,), d * shape[2], (d + 1) * shape[2]) for d in range(shape[0])]
        w = 1024 if shape[1] > 1024 and shape[1] % 1024 == 0 else shape[1]
        return [((slice(None), pl.ds(c, w)), c, c + w) for c in range(0, shape[1], w)]

    src_slabs = [slabs(s.shape) for s in srcs]
    out_slabs = [slabs(sh) for sh in out_shapes]
    work, first_use, last_touch = [], {}, {}
    for ji, (si, srow, oi, orow, nrows, tgt) in enumerate(jobs):
        tgt = np.asarray(tgt)
        tw = 256 if out_shapes[oi][-1] % 256 == 0 else LANE
        sw = 256 if srcs[si].shape[-1] % 256 == 0 else LANE
        for t in range(tgt.shape[0] // tw):
            tt = tgt[t * tw:(t + 1) * tw]
            tiles = sorted(set((tt[tt >= 0] // sw).tolist()))
            straight = bool(tiles) and tt[0] >= 0 and tt[0] % LANE == 0 and np.array_equal(tt, tt[0] + np.arange(tw))
            cols = [(int(tt[0]) + k * LANE, LANE) for k in range(tw // LANE)] if straight else [(s * sw, sw) for s in tiles]
            need = sorted({(si, k) for c0, _ in cols for k, (_, lo, hi) in enumerate(src_slabs[si]) if lo <= c0 < hi})
            touch = [(oi, k) for k, (_, lo, hi) in enumerate(out_slabs[oi]) if lo <= t * tw < hi][0]
            for key in need:
                first_use.setdefault(key, len(work))
            last_touch[touch] = len(work)
            work.append((ji, t, tw, sw, tiles, straight, need, touch))
    in_order = sorted(first_use, key=first_use.get)
    in_sem = {key: i for i, key in enumerate(in_order)}
    out_keys = sorted(last_touch)
    out_sem = {key: i for i, key in enumerate(out_keys)}

    def body(*refs):
        src_hbm, tab_refs = refs[:ns], refs[ns:ns + nj]
        out_hbm = refs[ns + nj + nd:ns + nj + nd + no]
        scratch = refs[ns + nj + nd + no:]
        src_refs, out_refs, in_sems, out_sems = scratch[:ns], scratch[ns:ns + no], scratch[-2], scratch[-1]
        loads = {}
        for key in in_order:
            si, k = key
            idx = src_slabs[si][k][0]
            loads[key] = pltpu.make_async_copy(src_hbm[si].at[idx], src_refs[si].at[idx], in_sems.at[in_sem[key]])
            loads[key].start()
        arrived, stores = set(), []
        for wi, (ji, t, tw, sw, tiles, straight, need, touch) in enumerate(work):
            si, srow, oi, orow, nrows, tgt = jobs[ji]
            sref, oref = src_refs[si], out_refs[oi]
            rc = 256 if nrows % 256 == 0 else LANE
            for key in need:
                if key not in arrived:
                    loads[key].wait()
                    arrived.add(key)
            onehots = []
            if tiles and not straight:
                want = tab_refs[ji][:, t * tw:(t + 1) * tw]
                row = lax.broadcasted_iota(jnp.int32, (sw, tw), 0)
                onehots = [jnp.where(want == row + s * sw, 1.0, 0.0).astype(BF) for s in tiles]
            first = int(np.asarray(tgt)[t * tw])

            def chunk(ci, _, t=t, tw=tw, sw=sw, tiles=tiles, straight=straight, onehots=onehots, first=first,
                      sref=sref, oref=oref, srow=srow, orow=orow, rc=rc):
                r0 = ci * rc
                ro = pl.multiple_of(orow + r0, LANE)
                rs = pl.multiple_of(srow + r0, LANE)
                if not tiles:
                    view(oref, t * tw, tw, ro, rc)[...] = jnp.zeros((rc, tw), BF)
                elif straight:
                    for k in range(tw // LANE):
                        view(oref, t * tw + k * LANE, LANE, ro, rc)[...] = view(sref, first + k * LANE, LANE, rs, rc)[...]
                else:
                    acc = None
                    for s, oh in zip(tiles, onehots):
                        p = jnp.dot(view(sref, s * sw, sw, rs, rc)[...], oh, preferred_element_type=F32)
                        acc = p if acc is None else acc + p
                    view(oref, t * tw, tw, ro, rc)[...] = acc.astype(BF)
                return 0

            lax.fori_loop(0, nrows // rc, chunk, 0)
            if last_touch[touch] == wi:
                idx = out_slabs[touch[0]][touch[1]][0]
                cp = pltpu.make_async_copy(out_refs[touch[0]].at[idx], out_hbm[touch[0]].at[idx], out_sems.at[out_sem[touch]])
                cp.start()
                stores.append(cp)
        for cp in stores:
            cp.wait()

    return pl.pallas_call(
        body, name=name, in_specs=[ANY_SPEC] * ns + [pl.BlockSpec(memory_space=pltpu.VMEM)] * nj + [ANY_SPEC] * nd,
        out_specs=[ANY_SPEC] * no, out_shape=[jax.ShapeDtypeStruct(s, BF) for s in out_shapes],
        scratch_shapes=[pltpu.VMEM(s.shape, BF) for s in srcs] + [pltpu.VMEM(s, BF) for s in out_shapes]
        + [pltpu.SemaphoreType.DMA((len(in_order),)), pltpu.SemaphoreType.DMA((len(out_keys),))],
        compiler_params=_cparams(),
    )(*srcs, *tables, *deps)


def sum_adamw(name, recvs, w, m, v):
    L, R, C = w.shape
    CP = recvs[0].shape[-1]
    tr = _pick_rows(R, 16, 128)

    def body(*refs):
        r_refs = refs[:L]
        w_ref, m_ref, v_ref, g_ref, d_ref, nm_ref, nv_ref, gsum = refs[L:]
        layer = pl.program_id(0)
        for k in range(L):
            def total(k=k):
                acc = r_refs[k][0].astype(F32)
                for d in range(1, NDEV):
                    acc = acc + r_refs[k][d].astype(F32)
                gsum[...] = acc
            pl.when(layer == k)(total)
        gg = gsum[:, 0:C]
        nm = B1 * m_ref[...] + (1.0 - B1) * gg
        nv = B2 * v_ref[...] + (1.0 - B2) * jnp.square(gg)
        m_hat = nm / (1.0 - B1 ** STEP)
        v_hat = nv / (1.0 - B2 ** STEP)
        g_ref[...] = gg
        d_ref[...] = -LR * (m_hat / (jnp.sqrt(v_hat) + EPS) + WD * w_ref[...])
        nm_ref[...] = nm
        nv_ref[...] = nv

    r_specs = [pl.BlockSpec((NDEV, tr, CP), functools.partial(lambda l, i, k: (0, jnp.where(l == k, i, 0), 0), k=k))
               for k in range(L)]
    blk = pl.BlockSpec((None, tr, C), lambda l, i: (l, i, 0))
    shp = jax.ShapeDtypeStruct(w.shape, F32)
    return pl.pallas_call(
        body, name=name, grid=(L, R // tr), in_specs=r_specs + [blk] * 3, out_specs=[blk] * 4, out_shape=[shp] * 4,
        scratch_shapes=[pltpu.VMEM((tr, CP), F32)], compiler_params=_cparams(),
    )(*recvs, w, m, v)


ALPHA = 8.0 ** 0.25


def _rope_fn(sign):
    def fn(x, cos, sin):
        W = x.shape[-1]
        lane = lax.broadcasted_iota(jnp.int32, x.shape, 1)
        first_half = (lane % ROPE) < (ROPE // 2)
        rot = jnp.where(first_half, -pltpu.roll(x, W - ROPE // 2, 1), pltpu.roll(x, ROPE // 2, 1))
        return x * cos + sign * rot * sin
    return fn


def layer_fwd(x, ada3, W, tabs, S):
    cos, sin = tabs
    T = 256
    u = rowwise("modulate", lambda xv, a: xv * (1.0 + a[1:2, :]) + a[0:1, :], S, T,
                [(x, D_MODEL, 0)], [ada3], [(D_MODEL, BF)])[0]
    proj = mm(u, W["in"], name="mm_proj", tm=1024, tn=1024, out_dtype=BF)
    W = {**W, **W["late"](proj)}

    ca = conv_fwd("conv_a_fwd", proj, OFF_A, W["conv_a"], 31, "glu", S, CW)

    def a_post(c, ag, vec):
        n, _ = _ln_stats(c + vec[0:1, :])
        return _silu(n * vec[1:2, :] + vec[2:3, :]) * _silu(ag)

    h_a = rowwise("mix_a_post", a_post, S, T, [(ca, CW, 0), (proj, CW, OFF_AG)], [W["vec_a"]], [(CW, BF)])[0]
    y_a = mm(h_a, W["a_out"], name="mm_branch_out", out_dtype=BF)

    cb = conv_fwd("conv_b_fwd", proj, OFF_B, W["conv_b"], 3, "mul", S, CW)
    h_b = rowwise("mix_b_post", lambda c, gb, bg: gb * c * _silu(bg), S, T,
                  [(cb, CW, 0), (proj, CW, OFF_GB), (proj, CW, OFF_BG)], [], [(CW, BF)])[0]
    y_b = mm(h_b, W["b_out"], name="mm_branch_out", out_dtype=BF)

    def rms2(ql, kvl, gq, gkv):
        rq = lax.rsqrt(jnp.mean(ql * ql, axis=-1, keepdims=True) + RMS_EPS)
        rk = lax.rsqrt(jnp.mean(kvl * kvl, axis=-1, keepdims=True) + RMS_EPS)
        return ql * rq * gq, kvl * rk * gkv

    qn, kvn = rowwise("rms_fwd", rms2, S, T, [(proj, QL, OFF_Q), (proj, KVL, OFF_KV)], [W["gq"], W["gkv"]],
                      [(QL, BF), (KVL, BF)])
    q = mm(qn, W["uq"], name="mm_q")
    kv = mm(kvn, W["ukv"], name="mm_kv", out_dtype=BF)
    rope = _rope_fn(1.0)

    def rope_fwd(qv, kr, c1, s1):
        parts = []
        for g in range(QUADS):
            parts.append(qv[:, g * QW:g * QW + 2 * LANE].astype(BF))
            parts.append(rope(qv[:, g * QW + 2 * LANE:(g + 1) * QW], c1, s1).astype(BF))
        kp = rope(kr, c1, s1)
        kp = kp + pltpu.roll(kp, ROPE, 1) + pltpu.roll(kp, 2 * ROPE, 1) + pltpu.roll(kp, 3 * ROPE, 1)
        return jnp.concatenate(parts, axis=1), kp

    q_b, kpe = rowwise("rope_fwd", rope_fwd, S, T,
                       [(q, HEADS * (NOPE + ROPE), 0), (proj, LANE, OFF_KR), (cos, LANE, 0), (sin, LANE, 0)], [],
                       [(HEADS * (NOPE + ROPE), BF), (LANE, BF)])
    o, lse = attn_fwd(q_b, kv, kpe, S)
    h_c = rowwise("mix_c_post", lambda ov, cg: ov * _silu(cg), S, T, [(o, CW, 0), (proj, CW, OFF_CG)], [],
                  [(CW, BF)])[0]
    y_c = mm(h_c, W["c_out"], name="mm_branch_out", out_dtype=BF)

    def merge(la, lb, lc, ya, yb, yc):
        return _sigmoid(la) * ya + _sigmoid(lb) * yb + _sigmoid(lc) * yc

    m = rowwise("merge_fwd", merge, S, 128,
                [(proj, D_MODEL, 0), (proj, D_MODEL, 1024), (proj, D_MODEL, 2048), (y_a, D_MODEL, 0),
                 (y_b, D_MODEL, 0), (y_c, D_MODEL, 0)], [], [(D_MODEL, BF)])[0]
    out = mm(m, W["o"], name="mm_out")

    def ln_fwd(xv, ov, a, lnv):
        n, _ = _ln_stats(ALPHA * xv + a[2:3, :] * ov)
        return n * lnv[0:1, :] + lnv[1:2, :]

    x_next = rowwise("ln_fwd", ln_fwd, S, 128, [(x, D_MODEL, 0), (out, D_MODEL, 0)], [ada3, W["lnv"]],
                     [(D_MODEL, F32)])[0]
    saved = dict(x=x, u=u, proj=proj, ca=ca, cb=cb, h_a=h_a, h_b=h_b, h_c=h_c, y_a=y_a, y_b=y_b, y_c=y_c, qn=qn,
                 kvn=kvn, q_b=q_b, kv=kv, kpe=kpe, lse=lse, o=o, m=m, out=out)
    return x_next, saved, W


def layer_bwd(dxn, sv, ada3, W, tabs, S, before_in=None):
    cos, sin = tabs
    T = 256
    x, proj = sv["x"], sv["proj"]
    G = {}

    def ln_bwd(xv, ov, dy, a, lnv):
        gate = a[2:3, :]
        n, rstd = _ln_stats(ALPHA * xv + gate * ov)
        dr = _ln_bwd(dy * lnv[0:1, :], n, rstd)
        return ALPHA * dr, gate * dr, _colsum(dy * n), _colsum(dy), _colsum(dr * ov)

    dres, d_out, G["ln_g"], G["ln_b"], d_gate = rowwise(
        "ln_bwd", ln_bwd, S, 128, [(x, D_MODEL, 0), (sv["out"], D_MODEL, 0), (dxn, D_MODEL, 0)], [ada3, W["lnv"]],
        [(D_MODEL, F32), (D_MODEL, BF)], [D_MODEL] * 3)
    dm = mm(d_out, W["o"], name="mm_dm", trans_b=True, out_dtype=BF)
    G["w_o"] = mm(sv["m"], d_out, name="mm_gw_o", trans_a=True, out_dtype=BF)

    def merge_bwd(dmv, la, lb, lc, ya, yb, yc):
        outs, dls = [], []
        for lg, yv in ((la, ya), (lb, yb), (lc, yc)):
            s = _sigmoid(lg)
            outs.append(dmv * s)
            dls.append((dmv * yv * s * (1.0 - s)).astype(BF))
        return (jnp.concatenate(dls, axis=1),) + tuple(outs)

    d_proj, dy_a, dy_b, dy_c = rowwise(
        "merge_bwd", merge_bwd, S, 128,
        [(dm, D_MODEL, 0), (proj, D_MODEL, 0), (proj, D_MODEL, 1024), (proj, D_MODEL, 2048), (sv["y_a"], D_MODEL, 0),
         (sv["y_b"], D_MODEL, 0), (sv["y_c"], D_MODEL, 0)], [], [(3 * D_MODEL, BF)] + [(D_MODEL, BF)] * 3,
        into=(None, NP, OFF_M))

    dh = {}
    for br, dy in (("a", dy_a), ("b", dy_b), ("c", dy_c)):
        dh[br] = mm(dy, W[br + "_out"], name="mm_dh", trans_b=True, out_dtype=BF)
        G["w_%s_out" % br] = mm(sv["h_" + br], dy, name="mm_gw_branch", trans_a=True, out_dtype=BF)

    def a_post_bwd(c, ag, dhv, vec):
        n, rstd = _ln_stats(c + vec[0:1, :])
        z = n * vec[1:2, :] + vec[2:3, :]
        d_ag = dhv * _silu(z) * _dsilu(ag)
        dz = dhv * _silu(ag) * _dsilu(z)
        dc = _ln_bwd(dz * vec[1:2, :], n, rstd)
        return d_ag, dc, _colsum(dc), _colsum(dz * n), _colsum(dz)

    d_proj, dca, G["conv_a_b"], G["ln_a_g"], G["ln_a_b"] = rowwise(
        "mix_a_post_bwd", a_post_bwd, S, T, [(sv["ca"], CW, 0), (proj, CW, OFF_AG), (dh["a"], CW, 0)], [W["vec_a"]],
        [(CW, BF), (CW, F32)], [CW] * 3, into=(d_proj, NP, OFF_AG))
    d_proj, G["conv_a_w"] = conv_bwd("conv_a_bwd", proj, OFF_A, dca, W["conv_a"], 31, "glu", S, CW, d_proj)

    def b_post_bwd(c, gb, bg, dhv):
        sg = _silu(bg)
        d_gb_bg = jnp.concatenate([(dhv * sg * c).astype(BF), (dhv * gb * c * _dsilu(bg)).astype(BF)], axis=1)
        return d_gb_bg, dhv * sg * gb

    d_proj, dcb = rowwise("mix_b_post_bwd", b_post_bwd, S, T,
                          [(sv["cb"], CW, 0), (proj, CW, OFF_GB), (proj, CW, OFF_BG), (dh["b"], CW, 0)], [],
                          [(2 * CW, BF), (CW, F32)], into=(d_proj, NP, OFF_GB))
    d_proj, G["conv_b_w"] = conv_bwd("conv_b_bwd", proj, OFF_B, dcb, W["conv_b"], 3, "mul", S, CW, d_proj)

    d_proj, d_o = rowwise("mix_c_post_bwd", lambda ov, cg, dhv: (dhv * ov * _dsilu(cg), dhv * _silu(cg)), S, T,
                          [(sv["o"], CW, 0), (proj, CW, OFF_CG), (dh["c"], CW, 0)], [], [(CW, BF), (CW, F32)],
                          into=(d_proj, NP, OFF_CG))
    dq, d_kv, dkp_heads = attn_bwd(sv["q_b"], sv["kv"], sv["kpe"], sv["o"], sv["lse"], d_o, S)
    ropeT = _rope_fn(-1.0)

    def rope_bwd(dqv, dkp, c1, s1):
        parts = []
        for g in range(QUADS):
            parts.append(dqv[:, g * QW:g * QW + 2 * LANE].astype(BF))
            parts.append(ropeT(dqv[:, g * QW + 2 * LANE:(g + 1) * QW], c1, s1).astype(BF))
        f = dkp[:, :LANE] + dkp[:, LANE:]
        f = f + pltpu.roll(f, 64, 1)
        f = f + pltpu.roll(f, 32, 1)
        lane = lax.broadcasted_iota(jnp.int32, f.shape, 1)
        return jnp.concatenate(parts, axis=1), jnp.where(lane < ROPE, ropeT(f, c1, s1), 0.0)

    d_q, dk_pe = rowwise("rope_bwd", rope_bwd, S, T,
                         [(dq, HEADS * (NOPE + ROPE), 0), (dkp_heads, HEADS * ROPE, 0), (cos, LANE, 0), (sin, LANE, 0)],
                         [], [(HEADS * (NOPE + ROPE), BF), (LANE, BF)])
    d_qn = mm(d_q, W["uq"], name="mm_dqn", trans_b=True, out_dtype=BF)
    d_kvn = mm(d_kv, W["ukv"], name="mm_dkvn", trans_b=True, out_dtype=BF)
    G["w_uq"] = mm(sv["qn"], d_q, name="mm_gw_uq", trans_a=True, out_dtype=BF)
    G["w_ukv"] = mm(sv["kvn"], d_kv, name="mm_gw_ukv", trans_a=True, out_dtype=BF)

    def rms_bwd(ql, kvl, dqn, dkn, dkp, gq, gkv):
        res = []
        for xv, dy, g in ((ql, dqn, gq), (kvl, dkn, gkv)):
            r = lax.rsqrt(jnp.mean(xv * xv, axis=-1, keepdims=True) + RMS_EPS)
            dxh = dy * g
            res.append(((r * (dxh - xv * (r * r) * jnp.mean(dxh * xv, axis=-1, keepdims=True))).astype(BF),
                        _colsum(dy * xv * r)))
        pad = jnp.zeros((ql.shape[0], LANE), BF)
        return jnp.concatenate([res[1][0], dkp, pad, res[0][0], pad], axis=1), res[0][1], res[1][1]

    d_proj, G["q_norm_g"], G["kv_norm_g"] = rowwise(
        "rms_bwd", rms_bwd, S, T,
        [(proj, QL, OFF_Q), (proj, KVL, OFF_KV), (d_qn, QL, 0), (d_kvn, KVL, 0), (dk_pe, LANE, 0)],
        [W["gq"], W["gkv"]], [(NP - OFF_KV, BF)], [QL, KVL], into=(d_proj, NP, OFF_KV))
    deps = before_in(G) if before_in is not None else ()
    du = mm(d_proj, W["in"], name="mm_du", trans_b=True, tk=1024, deps=deps)
    G["w_in"] = mm(sv["u"], d_proj, name="mm_gw_in", trans_a=True, out_dtype=BF, deps=deps)

    def mod_bwd(duv, xv, dr, a):
        return duv * (1.0 + a[1:2, :]) + dr, _colsum(duv), _colsum(duv * xv)

    dx, d_shift, d_scale = rowwise("mod_bwd", mod_bwd, S, 128, [(du, D_MODEL, 0), (x, D_MODEL, 0), (dres, D_MODEL, 0)],
                                   [ada3], [(D_MODEL, F32)], [D_MODEL] * 2)
    d_ada = jnp.concatenate([d_shift, d_scale, d_gate], axis=1)
    return dx, G, d_ada


SMALL = ("conv_a_b", "ln_a_g", "ln_a_b", "q_norm_g", "kv_norm_g", "ln_g", "ln_b")


def _rows(v):
    n = v.shape[0]
    r = -(-n // (LANE * 16)) * 16
    return jnp.pad(v, (0, r * LANE - n)).reshape(r, LANE)


def kernel(x, c, positions, w_ada, b_ada, w_in, conv_a_w, conv_a_b, ln_a_g, ln_a_b, w_a_out, conv_b_w, w_b_out, q_norm_g, kv_norm_g, w_uq, w_ukv, w_c_out, w_o, ln_g, ln_b, loss_target, m_w_ada, m_b_ada, m_w_in, m_conv_a_w, m_conv_a_b, m_ln_a_g, m_ln_a_b, m_w_a_out, m_conv_b_w, m_w_b_out, m_q_norm_g, m_kv_norm_g, m_w_uq, m_w_ukv, m_w_c_out, m_w_o, m_ln_g, m_ln_b, v_w_ada, v_b_ada, v_w_in, v_conv_a_w, v_conv_a_b, v_ln_a_g, v_ln_a_b, v_w_a_out, v_conv_b_w, v_w_b_out, v_q_norm_g, v_kv_norm_g, v_w_uq, v_w_ukv, v_w_c_out, v_w_o, v_ln_g, v_ln_b):
    P = dict(w_ada=w_ada, b_ada=b_ada, w_in=w_in, conv_a_w=conv_a_w, conv_a_b=conv_a_b, ln_a_g=ln_a_g, ln_a_b=ln_a_b,
             w_a_out=w_a_out, conv_b_w=conv_b_w, w_b_out=w_b_out, q_norm_g=q_norm_g, kv_norm_g=kv_norm_g, w_uq=w_uq,
             w_ukv=w_ukv, w_c_out=w_c_out, w_o=w_o, ln_g=ln_g, ln_b=ln_b)
    Mo = dict(w_ada=m_w_ada, b_ada=m_b_ada, w_in=m_w_in, conv_a_w=m_conv_a_w, conv_a_b=m_conv_a_b, ln_a_g=m_ln_a_g,
              ln_a_b=m_ln_a_b, w_a_out=m_w_a_out, conv_b_w=m_conv_b_w, w_b_out=m_w_b_out, q_norm_g=m_q_norm_g,
              kv_norm_g=m_kv_norm_g, w_uq=m_w_uq, w_ukv=m_w_ukv, w_c_out=m_w_c_out, w_o=m_w_o, ln_g=m_ln_g, ln_b=m_ln_b)
    Vo = dict(w_ada=v_w_ada, b_ada=v_b_ada, w_in=v_w_in, conv_a_w=v_conv_a_w, conv_a_b=v_conv_a_b, ln_a_g=v_ln_a_g,
              ln_a_b=v_ln_a_b, w_a_out=v_w_a_out, conv_b_w=v_conv_b_w, w_b_out=v_w_b_out, q_norm_g=v_q_norm_g,
              kv_norm_g=v_kv_norm_g, w_uq=v_w_uq, w_ukv=v_w_ukv, w_c_out=v_w_c_out, w_o=v_w_o, ln_g=v_ln_g, ln_b=v_ln_b)
    ORDER = ("w_ada", "b_ada", "w_in", "conv_a_w", "conv_a_b", "ln_a_g", "ln_a_b", "w_a_out", "conv_b_w", "w_b_out",
             "q_norm_g", "kv_norm_g", "w_uq", "w_ukv", "w_c_out", "w_o", "ln_g", "ln_b")
    L = w_ada.shape[0]
    S = x.shape[1]
    me = 4 * lax.axis_index("x") + 2 * lax.axis_index("y") + lax.axis_index("c")
    x2 = x[0]
    tgt = loss_target[0]

    small_in = _rows(jnp.concatenate([c.reshape(-1), conv_a_w.reshape(-1), conv_b_w.reshape(-1)]))
    w_in_b = jnp.pad(w_in.astype(BF), ((0, 0), (0, 0), (0, W_IN_PAD - IN_SHARD)))
    misc_b = jnp.concatenate([w_a_out, w_b_out, w_c_out, w_ukv, jnp.pad(w_uq, ((0, 0), (0, 0), (0, LANE - UQ_SHARD)))],
                             axis=1).astype(BF)
    w_o_b = w_o.astype(BF)
    gathered = [None] * L
    pending, _ = exchange_begin("gather0", [w_in_b[0]], 1, small_in)
    pending_rest, _ = exchange_begin("gather0_rest", [misc_b[0], w_o_b[0]], 2, small_in)
    sg = exchange("gather_small", [small_in], [])[0]
    sgf = sg.reshape(NDEV, -1)
    c_all = sgf[:, :D_MODEL]
    o1 = D_MODEL + L * 31 * 64
    conv_a_full = sgf[:, D_MODEL:o1].reshape(NDEV, L, 31, 64).transpose(1, 2, 0, 3).reshape(L, 31, CW)
    conv_b_full = sgf[:, o1:o1 + L * 3 * 64].reshape(NDEV, L, 3, 64).transpose(1, 2, 0, 3).reshape(L, 3, CW)

    c_act = rowwise("silu_c", _silu, 16, 16, [(jnp.pad(c_all, ((0, 8), (0, 0))), D_MODEL, 0)], [], [(D_MODEL, BF)])[0]
    ncol = w_ada.shape[2]
    w_ada_b = w_ada.astype(BF).transpose(1, 0, 2).reshape(D_MODEL, L * ncol)
    b_mine = lax.dynamic_slice_in_dim(b_ada, me * ncol, ncol, axis=1).reshape(1, L * ncol)
    ada_part = mm(c_act, w_ada_b, name="mm_ada", bias=b_mine)
    ada_rows = -(-(L * ncol) // (LANE * 8)) * 8
    ada_send = jnp.pad(ada_part[:NDEV].reshape(NDEV, -1, LANE), ((0, 0), (0, ada_rows - L * ncol // LANE), (0, 0)))
    ada_recv = exchange("a2a_ada", [], [ada_send])[0]
    ada = ada_recv[:, :L * ncol // LANE].reshape(NDEV, L, ncol).transpose(1, 0, 2).reshape(L, 3, D_MODEL)

    inv_freq = ROPE_THETA ** (-jnp.arange(0, ROPE, 2, dtype=F32) / ROPE)
    ang = positions[0].astype(F32)[:, None] * inv_freq
    tabs = (jnp.tile(jnp.cos(ang), (1, 2 * LANE // ROPE)), jnp.tile(jnp.sin(ang), (1, 2 * LANE // ROPE)))

    straight = np.arange(D_MODEL)
    fwd_in = [(0, 0, 0, 0, D_MODEL, _to_gathered(IN_PERM, IN_SHARD, W_IN_PAD))]
    fwd_misc = [(0, ROW_A, 0, 0, CW, straight), (0, ROW_B, 1, 0, CW, straight), (0, ROW_C, 2, 0, CW, straight),
                (0, ROW_UKV, 3, 0, KVL, UKV_PERM), (0, ROW_UQ, 4, 0, QL, _to_gathered(UQ_PERM, UQ_SHARD, LANE))]
    rev_in = [(0, 0, 0, 0, D_MODEL, _from_full(_inverse(IN_PERM, D_IN), IN_SHARD, W_IN_PAD))]
    rev_misc = [(0, 0, 0, ROW_A, CW, straight), (1, 0, 0, ROW_B, CW, straight), (2, 0, 0, ROW_C, CW, straight),
                (3, 0, 0, ROW_UKV, KVL, _from_full(_inverse(UKV_PERM, HEADS * (NOPE + VH)), LANE, LANE)),
                (4, 0, 0, ROW_UQ, QL, _from_full(_inverse(UQ_PERM, HEADS * (NOPE + ROPE)), UQ_SHARD, LANE))]

    def layer_weights(l, deps):
        w_in_p = col_gather("relayout_w_in", [gathered[l][0]], [(D_MODEL, NP)], fwd_in, deps)[0]

        def late(after):
            if len(gathered[l]) == 1:
                gathered[l] += exchange_end(pending_rest, after)
            _, g_misc, g_o = gathered[l]
            a_out, b_out, c_out, ukv, uq = col_gather(
                "relayout_misc", [g_misc],
                [(CW, D_MODEL)] * 3 + [(KVL, HEADS * (NOPE + VH)), (QL, HEADS * (NOPE + ROPE))], fwd_misc, deps)
            return {"a_out": a_out, "b_out": b_out, "c_out": c_out, "uq": uq, "ukv": ukv,
                    "o": g_o.reshape(D_MODEL, D_MODEL)}

        return {
            "in": w_in_p, "late": late,
            "conv_a": jnp.pad(conv_a_full[l], ((0, 1), (0, 0))), "conv_b": jnp.pad(conv_b_full[l], ((0, 5), (0, 0))),
            "vec_a": jnp.stack([conv_a_b[l], ln_a_g[l], ln_a_b[l]]), "gq": q_norm_g[l][None], "gkv": kv_norm_g[l][None],
            "lnv": jnp.stack([ln_g[l], ln_b[l]]),
        }

    h = x2
    saved, weights = [], []
    gathered[0] = exchange_end(pending, ada)
    for l in range(L):
        ada_l, deps = ada[l], ()
        if l + 1 < L:
            pending, token = exchange_begin("gather%d" % (l + 1), [w_in_b[l + 1], misc_b[l + 1], w_o_b[l + 1]], 3,
                                            gathered[l][0])
            ada_l, deps = ada_l + token[0, 0], (token,)
        h, sv, Wl = layer_fwd(h, ada_l, layer_weights(l, deps), tabs, S)
        if l + 1 < L:
            gathered[l + 1] = exchange_end(pending, h)
        saved.append(sv)
        weights.append(Wl)

    def loss_fn(y, t):
        e = y - t
        return e * (1.0 / D_MODEL), _colsum(e * e)

    dy, sq = rowwise("loss", loss_fn, S, 256, [(h, D_MODEL, 0), (tgt, D_MODEL, 0)], [], [(D_MODEL, F32)], [D_MODEL])
    loss = lax.psum(0.5 * jnp.sum(sq) / D_MODEL, ("x", "y", "c"))

    grads, d_adas, recv = [None] * L, [None] * L, [None] * L
    pending, token = None, None

    def send_rest(g):
        send_misc = col_gather("unrelayout_misc", [g["w_a_out"], g["w_b_out"], g["w_c_out"], g["w_ukv"], g["w_uq"]],
                               [(NDEV, MISC_ROWS, LANE)], rev_misc)[0]
        return [send_misc, g["w_o"].reshape(NDEV, D_MODEL // NDEV, D_MODEL)]

    rest0 = []

    def early_rest(g):
        handle, tok = exchange_begin("scatter0_rest", send_rest(g), 0, g["w_o"])
        rest0.append(handle)
        return (tok,)

    for l in reversed(range(L)):
        ada_l = ada[l] if token is None else ada[l] + token[0, 0]
        dy, g, d_adas[l] = layer_bwd(dy, saved[l], ada_l, weights[l], tabs, S, early_rest if l == 0 else None)
        grads[l] = g
        if pending is not None:
            recv[l + 1] = exchange_end(pending, dy)
        send_in = col_gather("unrelayout_w_in", [g["w_in"]], [(NDEV, D_MODEL, W_IN_PAD)], rev_in)[0]
        if l == 0:
            pending, token = exchange_begin("scatter0", [send_in], 0, recv[1][0])
        else:
            pending, token = exchange_begin("scatter%d" % l, [send_in] + send_rest(g), 0,
                                            dy if l + 1 == L else recv[l + 1][0])
    grad_x = dy[None]

    small_parts = [jnp.stack([grads[l][n].reshape(-1) for l in range(L)]).reshape(-1) for n in SMALL]
    small_parts.append(jnp.stack([grads[l]["conv_a_w"][:31].reshape(-1) for l in range(L)]).reshape(-1))
    small_parts.append(jnp.stack([grads[l]["conv_b_w"][:3].reshape(-1) for l in range(L)]).reshape(-1))
    small_parts.append(jnp.stack([d_adas[l].reshape(-1) for l in range(L)]).reshape(-1))
    small_sizes = [int(p.shape[0]) for p in small_parts]
    gsmall = exchange("gather_small_grads", [_rows(jnp.concatenate(small_parts))], [])[0]
    gsum = sum_slots("sum_small", gsmall).reshape(-1)
    recv[0] = [None] + exchange_end(rest0[0], gsum)
    Gr = {}
    offs = np.cumsum([0] + small_sizes)
    for i, n in enumerate(SMALL):
        Gr[n] = gsum[offs[i]:offs[i + 1]].reshape(L, -1)
    ca = gsum[offs[7]:offs[8]].reshape(L, 31, CW)
    cbw = gsum[offs[8]:offs[9]].reshape(L, 3, CW)
    Gr["conv_a_w"] = lax.dynamic_slice_in_dim(ca, me * 64, 64, axis=2)
    Gr["conv_b_w"] = lax.dynamic_slice_in_dim(cbw, me * 64, 64, axis=2)
    Gr["b_ada"] = gsum[offs[9]:offs[10]].reshape(L, 3 * D_MODEL)
    d_ada_all = gsmall.reshape(NDEV, -1)[:, offs[9]:offs[10]].reshape(NDEV, L, 3 * D_MODEL)
    d_mine = lax.dynamic_slice_in_dim(d_ada_all, me * ncol, ncol, axis=2).reshape(NDEV, L * ncol)
    g_ada = mm(c_act, jnp.pad(d_mine, ((0, 8), (0, 0))).astype(BF), name="mm_gw_ada", trans_a=True)
    Gr["w_ada"] = g_ada.reshape(D_MODEL, L, ncol).transpose(1, 0, 2)

    D, NM, NV = {}, {}, {}
    D["w_ada"], NM["w_ada"], NV["w_ada"] = adamw("adamw_w_ada", P["w_ada"], Gr["w_ada"], Mo["w_ada"], Vo["w_ada"])
    Gr["w_o"], D["w_o"], NM["w_o"], NV["w_o"] = sum_adamw(
        "sum_adamw_w_o", [recv[l][2] for l in range(L)], P["w_o"], Mo["w_o"], Vo["w_o"])
    misc = lambda T_: jnp.concatenate([T_["w_a_out"], T_["w_b_out"], T_["w_c_out"], T_["w_ukv"],
                                       jnp.pad(T_["w_uq"], ((0, 0), (0, 0), (0, LANE - UQ_SHARD)))], axis=1)
    res = sum_adamw("sum_adamw_misc", [recv[l][1] for l in range(L)], misc(P), misc(Mo), misc(Vo))
    for T_, r in zip((Gr, D, NM, NV), res):
        T_["w_a_out"], T_["w_b_out"], T_["w_c_out"] = r[:, ROW_A:ROW_B], r[:, ROW_B:ROW_C], r[:, ROW_C:ROW_UKV]
        T_["w_ukv"], T_["w_uq"] = r[:, ROW_UKV:ROW_UQ], r[:, ROW_UQ:MISC_ROWS, :UQ_SHARD]
    recv[0][0] = exchange_end(pending, res[1])[0]
    Gr["w_in"], D["w_in"], NM["w_in"], NV["w_in"] = sum_adamw(
        "sum_adamw_w_in", [recv[l][0] for l in range(L)], P["w_in"], Mo["w_in"], Vo["w_in"])
    packed =("b_ada", "conv_a_w", "conv_b_w") + SMALL
    pk = lambda T_: _rows(jnp.concatenate([T_[n].reshape(-1) for n in packed]))[None]
    dS, mS, vS = adamw("adamw_small", pk(P), pk(Gr), pk(Mo), pk(Vo))
    o = 0
    for n in packed:
        sz = int(np.prod(P[n].shape))
        D[n] = dS.reshape(-1)[o:o + sz].reshape(P[n].shape)
        NM[n] = mS.reshape(-1)[o:o + sz].reshape(P[n].shape)
        NV[n] = vS.reshape(-1)[o:o + sz].reshape(P[n].shape)
        o += sz
    return (loss, grad_x, *[Gr[n] for n in ORDER], *[D[n] for n in ORDER], *[NM[n] for n in ORDER],
            *[NV[n] for n in ORDER])
```

```python
import functools
import math

import numpy as np
import jax
import jax.numpy as jnp
from jax import lax
from jax.experimental import pallas as pl
from jax.experimental.pallas import tpu as pltpu

BF = jnp.bfloat16
F32 = jnp.float32
MESH = pl.DeviceIdType.MESH
NDEV = 8

HEADS, NOPE, ROPE, VH = 8, 64, 32, 64
HP = 128
ROPE_THETA = 10000.0
LN_EPS = 1e-5
RMS_EPS = 1e-6
LR, B1, B2, EPS, WD, STEP = 0.001, 0.9, 0.999, 1e-08, 0.01, 10

LANE = 128
VMEM_LIMIT = 56 * 1024 * 1024

D_MODEL, CW, QL, KVL = 1024, 512, 384, 256
OFF_M, OFF_A, OFF_AG, OFF_B, OFF_CG, OFF_GB, OFF_BG = 0, 3072, 4096, 4608, 5632, 6144, 6656
OFF_KV, OFF_KR, OFF_Q, NP = 7168, 7424, 7680, 8192
D_IN = 7840


def _cparams(**kw):
    return pltpu.CompilerParams(vmem_limit_bytes=VMEM_LIMIT, **kw)


def _sigmoid(x):
    return jax.nn.sigmoid(x)


def _silu(x):
    return x * _sigmoid(x)


def _dsilu(x):
    s = _sigmoid(x)
    return s * (1.0 + x * (1.0 - s))


def _pick_tile(n, cap, mult):
    if n <= cap:
        return n
    for t in range(cap - cap % mult, 0, -mult):
        if n % t == 0:
            return t
    raise ValueError((n, cap, mult))


def mm(a, b, *, name, trans_a=False, trans_b=False, out_dtype=F32, bias=None, tm=1024, tn=1024, tk=2048, deps=()):
    if trans_a:
        K, M = a.shape
    else:
        M, K = a.shape
    if trans_b:
        N, K2 = b.shape
    else:
        K2, N = b.shape
    assert K == K2 and not (trans_a and trans_b), (a.shape, b.shape)
    tm, tn = _pick_tile(M, tm, 16), _pick_tile(N, tn, LANE)
    tk = _pick_tile(K, tk, LANE if trans_b else 16)
    assert M % tm == 0 and N % tn == 0 and K % tk == 0, (M, N, K, tm, tn, tk)
    nk = K // tk
    dims = (((0 if trans_a else 1,), (1 if trans_b else 0,)), ((), ()))
    has_bias = bias is not None

    def body(*refs):
        a_ref, b_ref = refs[0], refs[1]
        bias_ref = refs[2] if has_bias else None
        o_ref = refs[(3 if has_bias else 2) + len(deps)]
        p = lax.dot_general(a_ref[...], b_ref[...], dims, preferred_element_type=F32)

        def finish(v):
            if has_bias:
                v = v + bias_ref[...]
            o_ref[...] = v.astype(o_ref.dtype)

        if nk == 1:
            finish(p)
        else:
            acc = refs[-1]
            k = pl.program_id(2)

            @pl.when(k == 0)
            def _():
                acc[...] = p

            @pl.when(k > 0)
            def _():
                acc[...] += p

            @pl.when(k == nk - 1)
            def _():
                finish(acc[...])

    if trans_a:
        a_spec = pl.BlockSpec((tk, tm), lambda i, j, k: (k, i))
    else:
        a_spec = pl.BlockSpec((tm, tk), lambda i, j, k: (i, k))
    if trans_b:
        b_spec = pl.BlockSpec((tn, tk), lambda i, j, k: (j, k))
    else:
        b_spec = pl.BlockSpec((tk, tn), lambda i, j, k: (k, j))
    in_specs = [a_spec, b_spec]
    args = [a, b]
    if has_bias:
        in_specs.append(pl.BlockSpec((1, tn), lambda i, j, k: (0, j)))
        args.append(bias)
    in_specs += [ANY_SPEC] * len(deps)
    args += list(deps)
    return pl.pallas_call(
        body, name=name, grid=(M // tm, N // tn, nk),
        in_specs=in_specs, out_specs=pl.BlockSpec((tm, tn), lambda i, j, k: (i, j)),
        out_shape=jax.ShapeDtypeStruct((M, N), out_dtype),
        scratch_shapes=[pltpu.VMEM((tm, tn), F32)] if nk > 1 else [],
        compiler_params=_cparams(),
    )(*args)


def rowwise(name, fn, S, T, row_ins, full_ins, row_outs, acc_outs=(), into=None):
    n_in = len(row_ins) + len(full_ins)
    n_ro, n_ao = len(row_outs), len(acc_outs)
    alias = into is not None and into[0] is not None

    def body(*refs):
        vals = [r[...] for r in refs[:n_in]]
        vals = [v.astype(F32) if v.dtype == BF else v for v in vals]
        outs = fn(*vals)
        if not isinstance(outs, (tuple, list)):
            outs = (outs,)
        assert len(outs) == n_ro + n_ao, (name, len(outs))
        o0 = n_in + (1 if alias else 0)
        for r, v in zip(refs[o0:o0 + n_ro], outs[:n_ro]):
            r[...] = v.astype(r.dtype)
        first = pl.program_id(0) == 0
        for r, v in zip(refs[o0 + n_ro:], outs[n_ro:]):
            def init(r=r, v=v):
                r[...] = v

            def accum(r=r, v=v):
                r[...] += v

            pl.when(first)(init)
            pl.when(jnp.logical_not(first))(accum)

    in_specs, args = [], []
    for arr, W, off in row_ins:
        assert off % W == 0 and arr.shape[0] == S, (name, arr.shape, W, off)
        in_specs.append(pl.BlockSpec((T, W), functools.partial(lambda i, cb: (i, cb), cb=off // W)))
        args.append(arr)
    for arr in full_ins:
        in_specs.append(pl.BlockSpec(arr.shape, lambda i: (0, 0)))
        args.append(arr)
    out_specs = [pl.BlockSpec((T, W), lambda i: (i, 0)) for W, _ in row_outs]
    out_shape = [jax.ShapeDtypeStruct((S, W), dt) for W, dt in row_outs]
    aliases = {}
    if into is not None:
        buf, total, off = into
        W0, dt0 = row_outs[0]
        assert off % W0 == 0
        out_specs[0] = pl.BlockSpec((T, W0), functools.partial(lambda i, cb: (i, cb), cb=off // W0))
        out_shape[0] = jax.ShapeDtypeStruct((S, total), dt0)
        if alias:
            in_specs.append(ANY_SPEC)
            args.append(buf)
            aliases = {n_in: 0}
    out_specs += [pl.BlockSpec((1, W), lambda i: (0, 0)) for W in acc_outs]
    out_shape += [jax.ShapeDtypeStruct((1, W), F32) for W in acc_outs]
    return pl.pallas_call(
        body, name=name, grid=(S // T,), in_specs=in_specs, out_specs=out_specs, out_shape=out_shape,
        input_output_aliases=aliases, compiler_params=_cparams(),
    )(*args)


def _colsum(v):
    return jnp.sum(v, axis=0, keepdims=True)


def _ln_stats(r):
    mu = jnp.mean(r, axis=-1, keepdims=True)
    d = r - mu
    var = jnp.mean(d * d, axis=-1, keepdims=True)
    rstd = lax.rsqrt(var + LN_EPS)
    return d * rstd, rstd


def _ln_bwd(dn, n, rstd):
    return rstd * (dn - jnp.mean(dn, axis=-1, keepdims=True) - n * jnp.mean(dn * n, axis=-1, keepdims=True))


CPAD = 32
TC = 64


def _pre(mode, x1, x2):
    return x1 * _sigmoid(x2) if mode == "glu" else x1 * x2


def _shifted(ext, sft):
    n = TC + CPAD
    return pltpu.roll(ext, (n - sft) % n, 0)[0:TC]


def _interleaved_specs(S, off):
    return [pl.BlockSpec((S, LANE), functools.partial(lambda j, o: (0, o + 2 * j), o=off // LANE)),
            pl.BlockSpec((S, LANE), functools.partial(lambda j, o: (0, o + 2 * j + 1), o=off // LANE))]


def conv_fwd(name, src, off, w_pad, taps, mode, S, C):
    nchunk = S // TC

    def body(x1_ref, x2_ref, w_ref, o_ref, a_pad):
        a_pad[0:CPAD, :] = jnp.zeros((CPAD, LANE), F32)

        def fill(i, _):
            r = pl.multiple_of(i * 256, 256)
            a_pad[pl.ds(CPAD + r, 256), :] = _pre(mode, x1_ref[pl.ds(r, 256), :].astype(F32),
                                                  x2_ref[pl.ds(r, 256), :].astype(F32))
            return 0

        lax.fori_loop(0, S // 256, fill, 0)

        def chunk(i, _):
            base = pl.multiple_of(i * TC, TC)
            ext = a_pad[pl.ds(base, TC + CPAD), :]
            acc = jnp.zeros((TC, LANE), F32)
            for k in range(taps):
                acc = acc + w_ref[pl.ds(k, 1), :] * _shifted(ext, CPAD - (taps - 1) + k)
            o_ref[pl.ds(base, TC), :] = acc
            return 0

        lax.fori_loop(0, nchunk, chunk, 0)

    kp = w_pad.shape[0]
    return pl.pallas_call(
        body, name=name, grid=(C // LANE,),
        in_specs=_interleaved_specs(S, off) + [pl.BlockSpec((kp, LANE), lambda j: (0, j))],
        out_specs=pl.BlockSpec((S, LANE), lambda j: (0, j)),
        out_shape=jax.ShapeDtypeStruct((S, C), F32),
        scratch_shapes=[pltpu.VMEM((S + CPAD, LANE), F32)],
        compiler_params=_cparams(),
    )(src, src, w_pad)


def conv_bwd(name, src, off, dc, w_pad, taps, mode, S, C, buf):
    nchunk = S // TC
    kp = w_pad.shape[0]

    def body(x1_ref, x2_ref, dc_ref, w_ref, _, d_ref, dw_ref, a_pad, dc_pad, dw_acc):
        a_pad[0:CPAD, :] = jnp.zeros((CPAD, LANE), F32)
        dc_pad[S:S + CPAD, :] = jnp.zeros((CPAD, LANE), F32)
        dw_acc[...] = jnp.zeros(dw_acc.shape, F32)

        def fill(i, _):
            r = pl.multiple_of(i * 256, 256)
            a_pad[pl.ds(CPAD + r, 256), :] = _pre(mode, x1_ref[pl.ds(r, 256), :].astype(F32),
                                                  x2_ref[pl.ds(r, 256), :].astype(F32))
            dc_pad[pl.ds(r, 256), :] = dc_ref[pl.ds(r, 256), :]
            return 0

        lax.fori_loop(0, S // 256, fill, 0)

        def chunk(i, _):
            base = pl.multiple_of(i * TC, TC)
            ext_d = dc_pad[pl.ds(base, TC + CPAD), :]
            ext_a = a_pad[pl.ds(base, TC + CPAD), :]
            dcv = ext_d[0:TC]
            da = jnp.zeros((TC, LANE), F32)
            for k in range(taps):
                da = da + w_ref[pl.ds(k, 1), :] * _shifted(ext_d, taps - 1 - k)
                prod = dcv * _shifted(ext_a, CPAD - (taps - 1) + k)
                fold = prod[0:8]
                for g in range(1, TC // 8):
                    fold = fold + prod[8 * g:8 * g + 8]
                dw_acc[pl.ds(8 * k, 8), :] += fold
            x1 = x1_ref[pl.ds(base, TC), :].astype(F32)
            x2 = x2_ref[pl.ds(base, TC), :].astype(F32)
            if mode == "glu":
                s = _sigmoid(x2)
                d1, d2 = da * s, da * x1 * s * (1.0 - s)
            else:
                d1, d2 = da * x2, da * x1
            d_ref[pl.ds(base, TC), 0:LANE] = d1.astype(BF)
            d_ref[pl.ds(base, TC), LANE:2 * LANE] = d2.astype(BF)
            return 0

        lax.fori_loop(0, nchunk, chunk, 0)
        dw_ref[...] = jnp.zeros(dw_ref.shape, F32)
        for k in range(taps):
            dw_ref[pl.ds(k, 1), :] = jnp.sum(dw_acc[pl.ds(8 * k, 8), :], axis=0, keepdims=True)

    blk = pl.BlockSpec((S, LANE), lambda j: (0, j))
    return pl.pallas_call(
        body, name=name, grid=(C // LANE,),
        in_specs=_interleaved_specs(S, off) + [blk, pl.BlockSpec((kp, LANE), lambda j: (0, j)), ANY_SPEC],
        out_specs=[pl.BlockSpec((S, 2 * LANE), functools.partial(lambda j, o: (0, o + j), o=off // (2 * LANE))),
                   pl.BlockSpec((kp, LANE), lambda j: (0, j))],
        out_shape=[jax.ShapeDtypeStruct(buf.shape, BF), jax.ShapeDtypeStruct((kp, C), F32)],
        input_output_aliases={4: 0},
        scratch_shapes=[pltpu.VMEM((S + CPAD, LANE), F32), pltpu.VMEM((S + CPAD, LANE), F32),
                        pltpu.VMEM((8 * kp, LANE), F32)],
        compiler_params=_cparams(),
    )(src, src, dc, w_pad, buf)


FWD_TILES = (512, 512)
BWD_TILES = (512, 512)
QUADS = HEADS // 4
QW, KVW = 4 * (NOPE + ROPE), 4 * (NOPE + VH)
SCALE = (NOPE + ROPE) ** -0.5
NT_DIMS = (((1,), (1,)), ((), ()))
TN_DIMS = (((0,), (0,)), ((), ()))


def _lane_mask(width, group, dtype):
    lane = lax.broadcasted_iota(jnp.int32, (1, LANE), 1)
    return jnp.where(lane // width == group, 1.0, 0.0).astype(dtype)


def _visible(tq, tk, off):
    row = lax.broadcasted_iota(jnp.int32, (tq, tk), 0)
    col = lax.broadcasted_iota(jnp.int32, (tq, tk), 1)
    return col <= row + off


def _attn_tiles(S, tq, tk):
    tk = tk if S % tk == 0 else 256
    return min(tq, tk), tk


def attn_fwd(q, kv, kpe, S):
    tq, tk = _attn_tiles(S, *FWD_TILES)
    nq = S // tq

    def body(q_ref, kv_ref, kp_ref, o_ref, lse_ref):
        for t in range(2):
            cols = slice(t * LANE, (t + 1) * LANE)
            for hh in range(2):
                def q_block(qi, _, t=t, hh=hh, cols=cols):
                    r0 = pl.multiple_of(qi * tq, tq)
                    qcat = jnp.concatenate([q_ref[pl.ds(r0, tq), cols] * _lane_mask(NOPE, hh, BF),
                                            q_ref[pl.ds(r0, tq), 2 * LANE:3 * LANE] * _lane_mask(ROPE, 2 * t + hh, BF)],
                                           axis=1)
                    nfull = (qi * tq) // tk

                    def step(kj, carry, masked):
                        m, l, acc = carry
                        c0 = pl.multiple_of(kj * tk, tk)
                        kc = jnp.concatenate([kv_ref[pl.ds(c0, tk), cols], kp_ref[pl.ds(c0, tk), :]], axis=1)
                        vt = kv_ref[pl.ds(c0, tk), (2 + t) * LANE:(3 + t) * LANE]
                        s = lax.dot_general(qcat, kc, NT_DIMS, preferred_element_type=F32) * SCALE
                        if masked:
                            s = jnp.where(_visible(tq, tk, qi * tq - nfull * tk), s, -jnp.inf)
                        m_new = jnp.maximum(m, jnp.max(s, axis=-1, keepdims=True))
                        p = jnp.exp(s - m_new)
                        alpha = jnp.exp(m - m_new)
                        l = alpha * l + jnp.sum(p, axis=-1, keepdims=True)
                        acc = alpha * acc + jnp.dot(p.astype(BF), vt, preferred_element_type=F32)
                        return m_new, l, acc

                    init = (jnp.full((tq, 1), -jnp.inf, F32), jnp.zeros((tq, 1), F32), jnp.zeros((tq, LANE), F32))
                    carry = lax.fori_loop(0, nfull, lambda kj, c: step(kj, c, False), init)
                    m, l, acc = step(nfull, carry, True)
                    mine = _lane_mask(NOPE, hh, F32)
                    if hh == 0:
                        o_ref[pl.ds(r0, tq), cols] = (acc / l) * mine
                        lse_ref[pl.ds(r0, tq), cols] = (m + jnp.log(l)) * mine
                    else:
                        o_ref[pl.ds(r0, tq), cols] += (acc / l) * mine
                        lse_ref[pl.ds(r0, tq), cols] += (m + jnp.log(l)) * mine
                    return 0

                lax.fori_loop(0, nq, q_block, 0)

    return pl.pallas_call(
        body, name="attn_fwd", grid=(QUADS,),
        in_specs=[pl.BlockSpec((S, QW), lambda g: (0, g)), pl.BlockSpec((S, KVW), lambda g: (0, g)),
                  pl.BlockSpec((S, LANE), lambda g: (0, 0))],
        out_specs=[pl.BlockSpec((S, 2 * LANE), lambda g: (0, g))] * 2,
        out_shape=[jax.ShapeDtypeStruct((S, HEADS * VH), F32)] * 2,
        compiler_params=_cparams(),
    )(q, kv, kpe)


def attn_bwd(q, kv, kpe, o, lse, do, S):
    tq, tk = _attn_tiles(S, *BWD_TILES)
    nq = S // tq

    def body(q_ref, kv_ref, kp_ref, o_ref, lse_ref, do_ref, dq_ref, dkv_ref, dkp_ref, dq_acc, dk_acc, dv_acc):
        for t in range(2):
            cols = slice(t * LANE, (t + 1) * LANE)
            dk_acc[...] = jnp.zeros(dk_acc.shape, F32)
            dv_acc[...] = jnp.zeros(dv_acc.shape, F32)
            for hh in range(2):
                def q_block(qi, _, t=t, hh=hh, cols=cols):
                    r0 = pl.multiple_of(qi * tq, tq)
                    mine = _lane_mask(NOPE, hh, F32)
                    qcat = jnp.concatenate([q_ref[pl.ds(r0, tq), cols] * _lane_mask(NOPE, hh, BF),
                                            q_ref[pl.ds(r0, tq), 2 * LANE:3 * LANE] * _lane_mask(ROPE, 2 * t + hh, BF)],
                                           axis=1)
                    dof = do_ref[pl.ds(r0, tq), cols] * mine
                    dob = dof.astype(BF)
                    delta = jnp.sum(dof * o_ref[pl.ds(r0, tq), cols], axis=-1, keepdims=True)
                    lse_h = lse_ref[pl.ds(r0, tq), cols][:, hh * NOPE:hh * NOPE + 1]
                    nfull = (qi * tq) // tk
                    dq_acc[...] = jnp.zeros(dq_acc.shape, F32)

                    def step(kj, _, masked):
                        c0 = pl.multiple_of(kj * tk, tk)
                        kc = jnp.concatenate([kv_ref[pl.ds(c0, tk), cols], kp_ref[pl.ds(c0, tk), :]], axis=1)
                        vt = kv_ref[pl.ds(c0, tk), (2 + t) * LANE:(3 + t) * LANE]
                        s = lax.dot_general(qcat, kc, NT_DIMS, preferred_element_type=F32) * SCALE
                        if masked:
                            s = jnp.where(_visible(tq, tk, qi * tq - nfull * tk), s, -jnp.inf)
                        p = jnp.exp(s - lse_h)
                        dp = lax.dot_general(dob, vt, NT_DIMS, preferred_element_type=F32)
                        ds = (p * (dp - delta) * SCALE).astype(BF)
                        dv_acc[pl.ds(c0, tk), :] += lax.dot_general(p.astype(BF), dob, TN_DIMS,
                                                                    preferred_element_type=F32)
                        dk_acc[pl.ds(c0, tk), :] += lax.dot_general(ds, qcat, TN_DIMS, preferred_element_type=F32)
                        dq_acc[...] += jnp.dot(ds, kc, preferred_element_type=F32)
                        return 0

                    lax.fori_loop(0, nfull, lambda kj, c: step(kj, c, False), 0)
                    step(nfull, 0, True)
                    d = dq_acc[...]
                    pe = d[:, LANE:] * _lane_mask(ROPE, 2 * t + hh, F32)
                    if hh == 0:
                        dq_ref[pl.ds(r0, tq), cols] = d[:, :LANE] * mine
                    else:
                        dq_ref[pl.ds(r0, tq), cols] += d[:, :LANE] * mine
                    if t == 0 and hh == 0:
                        dq_ref[pl.ds(r0, tq), 2 * LANE:3 * LANE] = pe
                    else:
                        dq_ref[pl.ds(r0, tq), 2 * LANE:3 * LANE] += pe
                    return 0

                lax.fori_loop(0, nq, q_block, 0)
            dkv_ref[:, t * LANE:(t + 1) * LANE] = dk_acc[:, :LANE].astype(BF)
            dkv_ref[:, (2 + t) * LANE:(3 + t) * LANE] = dv_acc[...].astype(BF)
            if t == 0:
                dkp_ref[...] = dk_acc[:, LANE:]
            else:
                dkp_ref[...] += dk_acc[:, LANE:]

    qspec = pl.BlockSpec((S, QW), lambda g: (0, g))
    kvspec = pl.BlockSpec((S, KVW), lambda g: (0, g))
    ospec = pl.BlockSpec((S, 2 * LANE), lambda g: (0, g))
    return pl.pallas_call(
        body, name="attn_bwd", grid=(QUADS,),
        in_specs=[qspec, kvspec, pl.BlockSpec((S, LANE), lambda g: (0, 0)), ospec, ospec, ospec],
        out_specs=[qspec, kvspec, pl.BlockSpec((S, LANE), lambda g: (0, g))],
        out_shape=[jax.ShapeDtypeStruct((S, HEADS * (NOPE + ROPE)), F32), jax.ShapeDtypeStruct((S, HEADS * (NOPE + VH)), BF),
                   jax.ShapeDtypeStruct((S, HEADS * ROPE), F32)],
        scratch_shapes=[pltpu.VMEM((tq, 2 * LANE), F32), pltpu.VMEM((S, 2 * LANE), F32), pltpu.VMEM((S, LANE), F32)],
        compiler_params=_cparams(),
    )(q, kv, kpe, o, lse, do)


def exchange(name, gathers, a2as):
    n_g, n = len(gathers), len(gathers) + len(a2as)

    def body(*refs):
        ins, outs = refs[:n], refs[n:2 * n]
        send_sems, recv_sems, loc_sems = refs[2 * n:]
        x, y, c = lax.axis_index("x"), lax.axis_index("y"), lax.axis_index("c")
        me = 4 * x + 2 * y + c

        def peer(k):
            px = 1 - x if k & 4 else x
            py = 1 - y if k & 2 else y
            pc = 1 - c if k & 1 else c
            return (px, py, pc), 4 * px + 2 * py + pc

        def remote(a, k):
            pid, pflat = peer(k)
            src = ins[a] if a < n_g else ins[a].at[pflat]
            return pltpu.make_async_remote_copy(
                src_ref=src, dst_ref=outs[a].at[me], send_sem=send_sems.at[a, k - 1], recv_sem=recv_sems.at[a, k - 1],
                device_id=pid, device_id_type=MESH)

        def arrival(a, k):
            pid, pflat = peer(k)
            src = ins[a] if a < n_g else ins[a].at[pflat]
            return pltpu.make_async_remote_copy(
                src_ref=src, dst_ref=outs[a].at[pflat], send_sem=send_sems.at[a, k - 1], recv_sem=recv_sems.at[a, k - 1],
                device_id=pid, device_id_type=MESH)

        local = []
        for a in range(n):
            own = ins[a] if a < n_g else ins[a].at[me]
            cp = pltpu.make_async_copy(own, outs[a].at[me], loc_sems.at[a])
            cp.start()
            local.append(cp)
        sent = []
        for k in (1, 2, 4, 3, 5, 6, 7):
            for a in range(n):
                cp = remote(a, k)
                cp.start()
                sent.append(cp)
        for k in range(1, 8):
            for a in range(n):
                arrival(a, k).wait_recv()
        for cp in sent:
            cp.wait_send()
        for cp in local:
            cp.wait()

    out_shape = [jax.ShapeDtypeStruct((NDEV,) + g.shape, g.dtype) for g in gathers]
    out_shape += [jax.ShapeDtypeStruct(a.shape, a.dtype) for a in a2as]
    any_spec = pl.BlockSpec(memory_space=pl.ANY)
    return pl.pallas_call(
        body, name=name, in_specs=[any_spec] * n, out_specs=[any_spec] * n, out_shape=out_shape,
        scratch_shapes=[pltpu.SemaphoreType.DMA((n, NDEV - 1)), pltpu.SemaphoreType.DMA((n, NDEV - 1)),
                        pltpu.SemaphoreType.DMA((n,))],
    )(*gathers, *a2as)


def gather_two_level(name, block):
    def body(x_ref, out_ref, stage, send_sems, recv_sems, loc_sem):
        x, y, c = lax.axis_index("x"), lax.axis_index("y"), lax.axis_index("c")
        me, sibling = (x, y, c), (x, y, 1 - c)
        chips = [(1 - x, y), (x, 1 - y), (1 - x, 1 - y)]

        def slot(px, py, pc):
            return out_ref.at[4 * px + 2 * py + pc]

        def copy(k, owner, to, src=None):
            return pltpu.make_async_remote_copy(
                src_ref=slot(*owner) if src is None else src, dst_ref=slot(*owner), send_sem=send_sems.at[k],
                recv_sem=recv_sems.at[k], device_id=to, device_id_type=MESH)

        load = pltpu.make_async_copy(x_ref, stage, loc_sem)
        load.start()
        first = [copy(0, me, sibling, src=x_ref)] + [copy(1 + j, me, (*chip, c), src=x_ref) for j, chip in enumerate(chips)]
        for cp in first:
            cp.start()
        load.wait()
        store = pltpu.make_async_copy(stage, slot(*me), loc_sem)
        store.start()
        passed = [copy(4 + j, (*chip, c), sibling) for j, chip in enumerate(chips)]
        for j, chip in enumerate(chips):
            copy(1 + j, (*chip, c), me).wait_recv()
            passed[j].start()
        copy(0, sibling, me).wait_recv()
        for j, chip in enumerate(chips):
            copy(4 + j, (*chip, 1 - c), me).wait_recv()
        for cp in first + passed:
            cp.wait_send()
        store.wait()

    return pl.pallas_call(
        body, name=name, in_specs=[pl.BlockSpec(memory_space=pl.ANY)], out_specs=pl.BlockSpec(memory_space=pl.ANY),
        out_shape=jax.ShapeDtypeStruct((NDEV,) + block.shape, block.dtype),
        scratch_shapes=[pltpu.VMEM(block.shape, block.dtype), pltpu.SemaphoreType.DMA((NDEV - 1,)),
                        pltpu.SemaphoreType.DMA((NDEV - 1,)), pltpu.SemaphoreType.DMA],
        compiler_params=_cparams(),
    )(block)


def _peer(k, x, y, c):
    px = 1 - x if k & 4 else x
    py = 1 - y if k & 2 else y
    pc = 1 - c if k & 1 else c
    return (px, py, pc), 4 * px + 2 * py + pc


PEER_ORDER = (1, 2, 4, 3, 5, 6, 7)
HBM_SPEC = pl.BlockSpec(memory_space=pltpu.HBM)
SEM_SPEC = pl.BlockSpec(memory_space=pltpu.SEMAPHORE)
ANY_SPEC = pl.BlockSpec(memory_space=pl.ANY)


def _split_copies(ins, lands, n_g, send_sems, recv_sems):
    x, y, c = lax.axis_index("x"), lax.axis_index("y"), lax.axis_index("c")
    me = 4 * x + 2 * y + c

    def outgoing(a, k):
        pid, pflat = _peer(k, x, y, c)
        src = ins[a] if a < n_g else ins[a].at[pflat]
        return pltpu.make_async_remote_copy(
            src_ref=src, dst_ref=lands[a].at[me], send_sem=send_sems.at[a * (NDEV - 1) + k - 1],
            recv_sem=recv_sems.at[a * (NDEV - 1) + k - 1],
            device_id=pid, device_id_type=MESH)

    def arrival(a, k):
        pid, pflat = _peer(k, x, y, c)
        src = ins[a] if a < n_g else ins[a].at[pflat]
        return pltpu.make_async_remote_copy(
            src_ref=src, dst_ref=lands[a].at[pflat], send_sem=send_sems.at[a * (NDEV - 1) + k - 1],
            recv_sem=recv_sems.at[a * (NDEV - 1) + k - 1],
            device_id=pid, device_id_type=MESH)

    return outgoing, arrival


def exchange_begin(name, srcs, n_g, dep):
    n = len(srcs)
    land_shapes = [((NDEV,) + s.shape) if a < n_g else s.shape for a, s in enumerate(srcs)]

    def own_body(*refs):
        ins, outs = refs[:n], refs[n + 1:2 * n + 1]
        stage, sems = refs[2 * n + 1:3 * n + 1], refs[-1]
        me = 4 * lax.axis_index("x") + 2 * lax.axis_index("y") + lax.axis_index("c")
        cps = [pltpu.make_async_copy(ins[a] if a < n_g else ins[a].at[me], stage[a], sems.at[a]) for a in range(n)]
        for cp in cps:
            cp.start()
        for cp in cps:
            cp.wait()
        cps = [pltpu.make_async_copy(stage[a], outs[a].at[me], sems.at[a]) for a in range(n)]
        for cp in cps:
            cp.start()
        for cp in cps:
            cp.wait()

    lands = pl.pallas_call(
        own_body, name=name + "_own", in_specs=[ANY_SPEC] * (n + 1), out_specs=[ANY_SPEC] * n,
        out_shape=[jax.ShapeDtypeStruct(sh, s.dtype) for sh, s in zip(land_shapes, srcs)],
        scratch_shapes=[pltpu.VMEM(sh[1:], s.dtype) for sh, s in zip(land_shapes, srcs)] + [pltpu.SemaphoreType.DMA((n,))],
        compiler_params=_cparams(),
    )(*srcs, dep)

    def start_body(*refs):
        ins, lz = refs[:n], refs[n:2 * n]
        send_sems, recv_sems, token = refs[2 * n], refs[2 * n + 1], refs[-1]
        outgoing, _ = _split_copies(ins, lz, n_g, send_sems, recv_sems)
        for k in PEER_ORDER:
            for a in range(n):
                outgoing(a, k).start()
        token[...] = jnp.zeros(token.shape, F32)

    hbm = lambda t: pltpu.HBM(t.shape, t.dtype)
    res = pl.pallas_call(
        start_body, name=name + "_start",
        out_shape=(pltpu.SemaphoreType.DMA((n * (NDEV - 1),)), pltpu.SemaphoreType.DMA((n * (NDEV - 1),)),
                   *[hbm(s) for s in srcs], *[hbm(t) for t in lands], jax.ShapeDtypeStruct((8, LANE), F32)),
        in_specs=[HBM_SPEC] * (2 * n),
        out_specs=(SEM_SPEC, SEM_SPEC, *[HBM_SPEC] * (2 * n), pl.BlockSpec(memory_space=pltpu.VMEM)),
        input_output_aliases={i: 2 + i for i in range(2 * n)},
        compiler_params=pltpu.CompilerParams(has_side_effects=pltpu.SideEffectType.DATAFLOW_SIDE_EFFECTING),
    )(*[pltpu.with_memory_space_constraint(t, pltpu.HBM) for t in list(srcs) + list(lands)])
    return (name, n, n_g, res[:-1]), res[-1]


def exchange_end(handle, after):
    name, n, n_g, (send_sems, recv_sems, *bufs) = handle

    def wait_body(*refs):
        ins, lz = refs[:n], refs[n:2 * n]
        ss, rs = refs[2 * n], refs[2 * n + 1]
        outgoing, arrival = _split_copies(ins, lz, n_g, ss, rs)
        for k in range(1, NDEV):
            for a in range(n):
                arrival(a, k).wait_recv()
        for k in range(1, NDEV):
            for a in range(n):
                outgoing(a, k).wait_send()

    res = pl.pallas_call(
        wait_body, name=name + "_wait", out_shape=tuple(pltpu.HBM(t.shape, t.dtype) for t in bufs),
        in_specs=[HBM_SPEC] * (2 * n) + [SEM_SPEC, SEM_SPEC, ANY_SPEC], out_specs=[HBM_SPEC] * (2 * n),
        input_output_aliases={i: i for i in range(2 * n)},
        compiler_params=pltpu.CompilerParams(has_side_effects=pltpu.SideEffectType.DATAFLOW_SIDE_EFFECTING),
    )(*bufs, send_sems, recv_sems, after)
    return list(res[n:])


def _pick_rows(R, mult, cap):
    best = None
    for n in range(1, R + 1):
        if R % n == 0 and (R // n) % mult == 0 and R // n <= cap:
            best = R // n
            break
    assert best is not None, (R, mult, cap)
    return best


def sum_slots(name, x):
    _, R, _ = x.shape
    tr = _pick_rows(R, 16, 2304)

    def body(x_ref, o_ref):
        acc = x_ref[0].astype(F32)
        for d in range(1, NDEV):
            acc = acc + x_ref[d].astype(F32)
        o_ref[...] = acc

    return pl.pallas_call(
        body, name=name, grid=(R // tr,),
        in_specs=[pl.BlockSpec((NDEV, tr, LANE), lambda i: (0, i, 0))],
        out_specs=pl.BlockSpec((tr, LANE), lambda i: (i, 0)),
        out_shape=jax.ShapeDtypeStruct((R, LANE), F32), compiler_params=_cparams(),
    )(x)


def adamw(name, w, g, m, v):
    L, R, C = w.shape
    tr = _pick_rows(R, 8, 256) if R % 8 == 0 else R

    def body(w_ref, g_ref, m_ref, v_ref, d_ref, nm_ref, nv_ref):
        gg = g_ref[...]
        nm = B1 * m_ref[...] + (1.0 - B1) * gg
        nv = B2 * v_ref[...] + (1.0 - B2) * jnp.square(gg)
        m_hat = nm / (1.0 - B1 ** STEP)
        v_hat = nv / (1.0 - B2 ** STEP)
        d_ref[...] = -LR * (m_hat / (jnp.sqrt(v_hat) + EPS) + WD * w_ref[...])
        nm_ref[...] = nm
        nv_ref[...] = nv

    blk = pl.BlockSpec((1, tr, C), lambda l, i: (l, i, 0))
    shp = jax.ShapeDtypeStruct(w.shape, F32)
    return pl.pallas_call(
        body, name=name, grid=(L, R // tr), in_specs=[blk] * 4, out_specs=[blk] * 3, out_shape=[shp] * 3,
        compiler_params=_cparams(),
    )(w, g, m, v)


IN_SHARD = D_IN // NDEV
UQ_SHARD = HEADS * (NOPE + ROPE) // NDEV
W_IN_PAD = 1024
ROW_A, ROW_B, ROW_C, ROW_UKV, ROW_UQ, MISC_ROWS = 0, 512, 1024, 1536, 1792, 2176


def _in_perm_index():
    ar = np.arange
    z = lambda n: np.full((n,), -1, np.int64)
    mix = lambda lo1, lo2: np.concatenate([ar(lo + LANE * j, lo + LANE * (j + 1)) for j in range(CW // LANE)
                                           for lo in (lo1, lo2)])
    return np.concatenate([ar(4768, 7840), mix(0, 512), ar(1024, 1536), mix(1536, 2560), ar(4256, 4768), ar(2048, 2560),
                           ar(3072, 3584), ar(3968, 4224), ar(4224, 4256), z(OFF_Q - OFF_KR - ROPE), ar(3584, 3968),
                           z(NP - OFF_Q - QL)])


def _head_perm_index(a, b):
    parts = []
    for g in range(QUADS):
        h = np.arange(4 * g, 4 * g + 4)[:, None] * (a + b)
        parts += [(h + np.arange(a)[None]).reshape(-1), (h + a + np.arange(b)[None]).reshape(-1)]
    return np.concatenate(parts)


def _inverse(perm, n):
    inv = np.full((n,), -1, np.int64)
    inv[perm[perm >= 0]] = np.nonzero(perm >= 0)[0]
    return inv


IN_PERM = _in_perm_index()
UQ_PERM = _head_perm_index(NOPE, ROPE)
UKV_PERM = _head_perm_index(NOPE, VH)


def _to_gathered(perm, shard, pad):
    return np.where(perm >= 0, (perm // shard) * pad + perm % shard, -1)


def _from_full(inv, shard, pad):
    j, i = np.divmod(np.arange(NDEV * pad), pad)
    return np.where(i < shard, inv[np.minimum(j * shard + i, inv.shape[0] - 1)], -1)


def col_gather(name, srcs, out_shapes, jobs, deps=()):
    ns, nj, nd, no = len(srcs), len(jobs), len(deps), len(out_shapes)
    tables = [jnp.asarray(np.asarray(job[5], np.int32)[None, :]) for job in jobs]

    def view(ref, col0, width, r0, rc):
        n = ref.shape[-1]
        if len(ref.shape) == 3:
            return ref.at[col0 // n, pl.ds(r0, rc), pl.ds(col0 % n, width)]
        return ref.at[pl.ds(r0, rc), pl.ds(col0, width)]

    def slabs(shape):
        if len(shape) == 3:
            return [((d,), d * shape[2], (d + 1) * shape[2]) for d in range(shape[0])]
        w = 1024 if shape[1] > 1024 and shape[1] % 1024 == 0 else shape[1]
        return [((slice(None), pl.ds(c, w)), c, c + w) for c in range(0, shape[1], w)]

    src_slabs = [slabs(s.shape) for s in srcs]
    out_slabs = [slabs(sh) for sh in out_shapes]
    work, first_use, last_touch = [], {}, {}
    for ji, (si, srow, oi, orow, nrows, tgt) in enumerate(jobs):
        tgt = np.asarray(tgt)
        tw = 256 if out_shapes[oi][-1] % 256 == 0 else LANE
        sw = 256 if srcs[si].shape[-1] % 256 == 0 else LANE
        for t in range(tgt.shape[0] // tw):
            tt = tgt[t * tw:(t + 1) * tw]
            tiles = sorted(set((tt[tt >= 0] // sw).tolist()))
            straight = bool(tiles) and tt[0] >= 0 and tt[0] % LANE == 0 and np.array_equal(tt, tt[0] + np.arange(tw))
            cols = [(int(tt[0]) + k * LANE, LANE) for k in range(tw // LANE)] if straight else [(s * sw, sw) for s in tiles]
            need = sorted({(si, k) for c0, _ in cols for k, (_, lo, hi) in enumerate(src_slabs[si]) if lo <= c0 < hi})
            touch = [(oi, k) for k, (_, lo, hi) in enumerate(out_slabs[oi]) if lo <= t * tw < hi][0]
            for key in need:
                first_use.setdefault(key, len(work))
            last_touch[touch] = len(work)
            work.append((ji, t, tw, sw, tiles, straight, need, touch))
    in_order = sorted(first_use, key=first_use.get)
    in_sem = {key: i for i, key in enumerate(in_order)}
    out_keys = sorted(last_touch)
    out_sem = {key: i for i, key in enumerate(out_keys)}

    def body(*refs):
        src_hbm, tab_refs = refs[:ns], refs[ns:ns + nj]
        out_hbm = refs[ns + nj + nd:ns + nj + nd + no]
        scratch = refs[ns + nj + nd + no:]
        src_refs, out_refs, in_sems, out_sems = scratch[:ns], scratch[ns:ns + no], scratch[-2], scratch[-1]
        loads = {}
        for key in in_order:
            si, k = key
            idx = src_slabs[si][k][0]
            loads[key] = pltpu.make_async_copy(src_hbm[si].at[idx], src_refs[si].at[idx], in_sems.at[in_sem[key]])
            loads[key].start()
        arrived, stores = set(), []
        for wi, (ji, t, tw, sw, tiles, straight, need, touch) in enumerate(work):
            si, srow, oi, orow, nrows, tgt = jobs[ji]
            sref, oref = src_refs[si], out_refs[oi]
            rc = nrows if nrows <= 1024 else 1024
            for key in need:
                if key not in arrived:
                    loads[key].wait()
                    arrived.add(key)
            onehots = []
            if tiles and not straight:
                want = tab_refs[ji][:, t * tw:(t + 1) * tw]
                row = lax.broadcasted_iota(jnp.int32, (sw, tw), 0)
                onehots = [jnp.where(want == row + s * sw, 1.0, 0.0).astype(BF) for s in tiles]
            first = int(np.asarray(tgt)[t * tw])

            def chunk(ci, _, t=t, tw=tw, sw=sw, tiles=tiles, straight=straight, onehots=onehots, first=first,
                      sref=sref, oref=oref, srow=srow, orow=orow, rc=rc):
                r0 = ci * rc
                ro = pl.multiple_of(orow + r0, LANE)
                rs = pl.multiple_of(srow + r0, LANE)
                if not tiles:
                    view(oref, t * tw, tw, ro, rc)[...] = jnp.zeros((rc, tw), BF)
                elif straight:
                    for k in range(tw // LANE):
                        view(oref, t * tw + k * LANE, LANE, ro, rc)[...] = view(sref, first + k * LANE, LANE, rs, rc)[...]
                else:
                    acc = None
                    for s, oh in zip(tiles, onehots):
                        p = jnp.dot(view(sref, s * sw, sw, rs, rc)[...], oh, preferred_element_type=F32)
                        acc = p if acc is None else acc + p
                    view(oref, t * tw, tw, ro, rc)[...] = acc.astype(BF)
                return 0

            lax.fori_loop(0, nrows // rc, chunk, 0)
            if last_touch[touch] == wi:
                idx = out_slabs[touch[0]][touch[1]][0]
                cp = pltpu.make_async_copy(out_refs[touch[0]].at[idx], out_hbm[touch[0]].at[idx], out_sems.at[out_sem[touch]])
                cp.start()
                stores.append(cp)
        for cp in stores:
            cp.wait()

    return pl.pallas_call(
        body, name=name, in_specs=[ANY_SPEC] * ns + [pl.BlockSpec(memory_space=pltpu.VMEM)] * nj + [ANY_SPEC] * nd,
        out_specs=[ANY_SPEC] * no, out_shape=[jax.ShapeDtypeStruct(s, BF) for s in out_shapes],
        scratch_shapes=[pltpu.VMEM(s.shape, BF) for s in srcs] + [pltpu.VMEM(s, BF) for s in out_shapes]
        + [pltpu.SemaphoreType.DMA((len(in_order),)), pltpu.SemaphoreType.DMA((len(out_keys),))],
        compiler_params=_cparams(),
    )(*srcs, *tables, *deps)


def sum_adamw(name, recvs, w, m, v):
    L, R, C = w.shape
    CP = recvs[0].shape[-1]
    tr = _pick_rows(R, 16, 128)

    def body(*refs):
        r_refs = refs[:L]
        w_ref, m_ref, v_ref, g_ref, d_ref, nm_ref, nv_ref, gsum = refs[L:]
        layer = pl.program_id(0)
        for k in range(L):
            def total(k=k):
                acc = r_refs[k][0].astype(F32)
                for d in range(1, NDEV):
                    acc = acc + r_refs[k][d].astype(F32)
                gsum[...] = acc
            pl.when(layer == k)(total)
        gg = gsum[:, 0:C]
        nm = B1 * m_ref[...] + (1.0 - B1) * gg
        nv = B2 * v_ref[...] + (1.0 - B2) * jnp.square(gg)
        m_hat = nm / (1.0 - B1 ** STEP)
        v_hat = nv / (1.0 - B2 ** STEP)
        g_ref[...] = gg
        d_ref[...] = -LR * (m_hat / (jnp.sqrt(v_hat) + EPS) + WD * w_ref[...])
        nm_ref[...] = nm
        nv_ref[...] = nv

    r_specs = [pl.BlockSpec((NDEV, tr, CP), functools.partial(lambda l, i, k: (0, jnp.where(l == k, i, 0), 0), k=k))
               for k in range(L)]
    blk = pl.BlockSpec((None, tr, C), lambda l, i: (l, i, 0))
    shp = jax.ShapeDtypeStruct(w.shape, F32)
    return pl.pallas_call(
        body, name=name, grid=(L, R // tr), in_specs=r_specs + [blk] * 3, out_specs=[blk] * 4, out_shape=[shp] * 4,
        scratch_shapes=[pltpu.VMEM((tr, CP), F32)], compiler_params=_cparams(),
    )(*recvs, w, m, v)


ALPHA = 8.0 ** 0.25


def _rope_fn(sign):
    def fn(x, cos, sin):
        W = x.shape[-1]
        lane = lax.broadcasted_iota(jnp.int32, x.shape, 1)
        first_half = (lane % ROPE) < (ROPE // 2)
        rot = jnp.where(first_half, -pltpu.roll(x, W - ROPE // 2, 1), pltpu.roll(x, ROPE // 2, 1))
        return x * cos + sign * rot * sin
    return fn


def layer_fwd(x, ada3, W, tabs, S):
    cos, sin = tabs
    T = 256
    u = rowwise("modulate", lambda xv, a: xv * (1.0 + a[1:2, :]) + a[0:1, :], S, T,
                [(x, D_MODEL, 0)], [ada3], [(D_MODEL, BF)])[0]
    proj = mm(u, W["in"], name="mm_proj", tm=1024, tn=1024, out_dtype=BF)
    W = {**W, **W["late"](proj)}

    ca = conv_fwd("conv_a_fwd", proj, OFF_A, W["conv_a"], 31, "glu", S, CW)

    def a_post(c, ag, vec):
        n, _ = _ln_stats(c + vec[0:1, :])
        return _silu(n * vec[1:2, :] + vec[2:3, :]) * _silu(ag)

    h_a = rowwise("mix_a_post", a_post, S, T, [(ca, CW, 0), (proj, CW, OFF_AG)], [W["vec_a"]], [(CW, BF)])[0]
    y_a = mm(h_a, W["a_out"], name="mm_branch_out", out_dtype=BF)

    cb = conv_fwd("conv_b_fwd", proj, OFF_B, W["conv_b"], 3, "mul", S, CW)
    h_b = rowwise("mix_b_post", lambda c, gb, bg: gb * c * _silu(bg), S, T,
                  [(cb, CW, 0), (proj, CW, OFF_GB), (proj, CW, OFF_BG)], [], [(CW, BF)])[0]
    y_b = mm(h_b, W["b_out"], name="mm_branch_out", out_dtype=BF)

    def rms2(ql, kvl, gq, gkv):
        rq = lax.rsqrt(jnp.mean(ql * ql, axis=-1, keepdims=True) + RMS_EPS)
        rk = lax.rsqrt(jnp.mean(kvl * kvl, axis=-1, keepdims=True) + RMS_EPS)
        return ql * rq * gq, kvl * rk * gkv

    qn, kvn = rowwise("rms_fwd", rms2, S, T, [(proj, QL, OFF_Q), (proj, KVL, OFF_KV)], [W["gq"], W["gkv"]],
                      [(QL, BF), (KVL, BF)])
    q = mm(qn, W["uq"], name="mm_q")
    kv = mm(kvn, W["ukv"], name="mm_kv", out_dtype=BF)
    rope = _rope_fn(1.0)

    def rope_fwd(qv, kr, c1, s1):
        parts = []
        for g in range(QUADS):
            parts.append(qv[:, g * QW:g * QW + 2 * LANE].astype(BF))
            parts.append(rope(qv[:, g * QW + 2 * LANE:(g + 1) * QW], c1, s1).astype(BF))
        kp = rope(kr, c1, s1)
        kp = kp + pltpu.roll(kp, ROPE, 1) + pltpu.roll(kp, 2 * ROPE, 1) + pltpu.roll(kp, 3 * ROPE, 1)
        return jnp.concatenate(parts, axis=1), kp

    q_b, kpe = rowwise("rope_fwd", rope_fwd, S, T,
                       [(q, HEADS * (NOPE + ROPE), 0), (proj, LANE, OFF_KR), (cos, LANE, 0), (sin, LANE, 0)], [],
                       [(HEADS * (NOPE + ROPE), BF), (LANE, BF)])
    o, lse = attn_fwd(q_b, kv, kpe, S)
    h_c = rowwise("mix_c_post", lambda ov, cg: ov * _silu(cg), S, T, [(o, CW, 0), (proj, CW, OFF_CG)], [],
                  [(CW, BF)])[0]
    y_c = mm(h_c, W["c_out"], name="mm_branch_out", out_dtype=BF)

    def merge(la, lb, lc, ya, yb, yc):
        return _sigmoid(la) * ya + _sigmoid(lb) * yb + _sigmoid(lc) * yc

    m = rowwise("merge_fwd", merge, S, 128,
                [(proj, D_MODEL, 0), (proj, D_MODEL, 1024), (proj, D_MODEL, 2048), (y_a, D_MODEL, 0),
                 (y_b, D_MODEL, 0), (y_c, D_MODEL, 0)], [], [(D_MODEL, BF)])[0]
    out = mm(m, W["o"], name="mm_out")

    def ln_fwd(xv, ov, a, lnv):
        n, _ = _ln_stats(ALPHA * xv + a[2:3, :] * ov)
        return n * lnv[0:1, :] + lnv[1:2, :]

    x_next = rowwise("ln_fwd", ln_fwd, S, 128, [(x, D_MODEL, 0), (out, D_MODEL, 0)], [ada3, W["lnv"]],
                     [(D_MODEL, F32)])[0]
    saved = dict(x=x, u=u, proj=proj, ca=ca, cb=cb, h_a=h_a, h_b=h_b, h_c=h_c, y_a=y_a, y_b=y_b, y_c=y_c, qn=qn,
                 kvn=kvn, q_b=q_b, kv=kv, kpe=kpe, lse=lse, o=o, m=m, out=out)
    return x_next, saved, W


def layer_bwd(dxn, sv, ada3, W, tabs, S, before_in=None):
    cos, sin = tabs
    T = 256
    x, proj = sv["x"], sv["proj"]
    G = {}

    def ln_bwd(xv, ov, dy, a, lnv):
        gate = a[2:3, :]
        n, rstd = _ln_stats(ALPHA * xv + gate * ov)
        dr = _ln_bwd(dy * lnv[0:1, :], n, rstd)
        return ALPHA * dr, gate * dr, _colsum(dy * n), _colsum(dy), _colsum(dr * ov)

    dres, d_out, G["ln_g"], G["ln_b"], d_gate = rowwise(
        "ln_bwd", ln_bwd, S, 128, [(x, D_MODEL, 0), (sv["out"], D_MODEL, 0), (dxn, D_MODEL, 0)], [ada3, W["lnv"]],
        [(D_MODEL, F32), (D_MODEL, BF)], [D_MODEL] * 3)
    dm = mm(d_out, W["o"], name="mm_dm", trans_b=True, out_dtype=BF)
    G["w_o"] = mm(sv["m"], d_out, name="mm_gw_o", trans_a=True, out_dtype=BF)

    def merge_bwd(dmv, la, lb, lc, ya, yb, yc):
        outs, dls = [], []
        for lg, yv in ((la, ya), (lb, yb), (lc, yc)):
            s = _sigmoid(lg)
            outs.append(dmv * s)
            dls.append((dmv * yv * s * (1.0 - s)).astype(BF))
        return (jnp.concatenate(dls, axis=1),) + tuple(outs)

    d_proj, dy_a, dy_b, dy_c = rowwise(
        "merge_bwd", merge_bwd, S, 128,
        [(dm, D_MODEL, 0), (proj, D_MODEL, 0), (proj, D_MODEL, 1024), (proj, D_MODEL, 2048), (sv["y_a"], D_MODEL, 0),
         (sv["y_b"], D_MODEL, 0), (sv["y_c"], D_MODEL, 0)], [], [(3 * D_MODEL, BF)] + [(D_MODEL, BF)] * 3,
        into=(None, NP, OFF_M))

    dh = {}
    for br, dy in (("a", dy_a), ("b", dy_b), ("c", dy_c)):
        dh[br] = mm(dy, W[br + "_out"], name="mm_dh", trans_b=True, out_dtype=BF)
        G["w_%s_out" % br] = mm(sv["h_" + br], dy, name="mm_gw_branch", trans_a=True, out_dtype=BF)

    def a_post_bwd(c, ag, dhv, vec):
        n, rstd = _ln_stats(c + vec[0:1, :])
        z = n * vec[1:2, :] + vec[2:3, :]
        d_ag = dhv * _silu(z) * _dsilu(ag)
        dz = dhv * _silu(ag) * _dsilu(z)
        dc = _ln_bwd(dz * vec[1:2, :], n, rstd)
        return d_ag, dc, _colsum(dc), _colsum(dz * n), _colsum(dz)

    d_proj, dca, G["conv_a_b"], G["ln_a_g"], G["ln_a_b"] = rowwise(
        "mix_a_post_bwd", a_post_bwd, S, T, [(sv["ca"], CW, 0), (proj, CW, OFF_AG), (dh["a"], CW, 0)], [W["vec_a"]],
        [(CW, BF), (CW, F32)], [CW] * 3, into=(d_proj, NP, OFF_AG))
    d_proj, G["conv_a_w"] = conv_bwd("conv_a_bwd", proj, OFF_A, dca, W["conv_a"], 31, "glu", S, CW, d_proj)

    def b_post_bwd(c, gb, bg, dhv):
        sg = _silu(bg)
        d_gb_bg = jnp.concatenate([(dhv * sg * c).astype(BF), (dhv * gb * c * _dsilu(bg)).astype(BF)], axis=1)
        return d_gb_bg, dhv * sg * gb

    d_proj, dcb = rowwise("mix_b_post_bwd", b_post_bwd, S, T,
                          [(sv["cb"], CW, 0), (proj, CW, OFF_GB), (proj, CW, OFF_BG), (dh["b"], CW, 0)], [],
                          [(2 * CW, BF), (CW, F32)], into=(d_proj, NP, OFF_GB))
    d_proj, G["conv_b_w"] = conv_bwd("conv_b_bwd", proj, OFF_B, dcb, W["conv_b"], 3, "mul", S, CW, d_proj)

    d_proj, d_o = rowwise("mix_c_post_bwd", lambda ov, cg, dhv: (dhv * ov * _dsilu(cg), dhv * _silu(cg)), S, T,
                          [(sv["o"], CW, 0), (proj, CW, OFF_CG), (dh["c"], CW, 0)], [], [(CW, BF), (CW, F32)],
                          into=(d_proj, NP, OFF_CG))
    dq, d_kv, dkp_heads = attn_bwd(sv["q_b"], sv["kv"], sv["kpe"], sv["o"], sv["lse"], d_o, S)
    ropeT = _rope_fn(-1.0)

    def rope_bwd(dqv, dkp, c1, s1):
        parts = []
        for g in range(QUADS):
            parts.append(dqv[:, g * QW:g * QW + 2 * LANE].astype(BF))
            parts.append(ropeT(dqv[:, g * QW + 2 * LANE:(g + 1) * QW], c1, s1).astype(BF))
        f = dkp[:, :LANE] + dkp[:, LANE:]
        f = f + pltpu.roll(f, 64, 1)
        f = f + pltpu.roll(f, 32, 1)
        lane = lax.broadcasted_iota(jnp.int32, f.shape, 1)
        return jnp.concatenate(parts, axis=1), jnp.where(lane < ROPE, ropeT(f, c1, s1), 0.0)

    d_q, dk_pe = rowwise("rope_bwd", rope_bwd, S, T,
                         [(dq, HEADS * (NOPE + ROPE), 0), (dkp_heads, HEADS * ROPE, 0), (cos, LANE, 0), (sin, LANE, 0)],
                         [], [(HEADS * (NOPE + ROPE), BF), (LANE, BF)])
    d_qn = mm(d_q, W["uq"], name="mm_dqn", trans_b=True, out_dtype=BF)
    d_kvn = mm(d_kv, W["ukv"], name="mm_dkvn", trans_b=True, out_dtype=BF)
    G["w_uq"] = mm(sv["qn"], d_q, name="mm_gw_uq", trans_a=True, out_dtype=BF)
    G["w_ukv"] = mm(sv["kvn"], d_kv, name="mm_gw_ukv", trans_a=True, out_dtype=BF)

    def rms_bwd(ql, kvl, dqn, dkn, dkp, gq, gkv):
        res = []
        for xv, dy, g in ((ql, dqn, gq), (kvl, dkn, gkv)):
            r = lax.rsqrt(jnp.mean(xv * xv, axis=-1, keepdims=True) + RMS_EPS)
            dxh = dy * g
            res.append(((r * (dxh - xv * (r * r) * jnp.mean(dxh * xv, axis=-1, keepdims=True))).astype(BF),
                        _colsum(dy * xv * r)))
        pad = jnp.zeros((ql.shape[0], LANE), BF)
        return jnp.concatenate([res[1][0], dkp, pad, res[0][0], pad], axis=1), res[0][1], res[1][1]

    d_proj, G["q_norm_g"], G["kv_norm_g"] = rowwise(
        "rms_bwd", rms_bwd, S, T,
        [(proj, QL, OFF_Q), (proj, KVL, OFF_KV), (d_qn, QL, 0), (d_kvn, KVL, 0), (dk_pe, LANE, 0)],
        [W["gq"], W["gkv"]], [(NP - OFF_KV, BF)], [QL, KVL], into=(d_proj, NP, OFF_KV))
    deps = before_in(G) if before_in is not None else ()
    du = mm(d_proj, W["in"], name="mm_du", trans_b=True, tk=1024, deps=deps)
    G["w_in"] = mm(sv["u"], d_proj, name="mm_gw_in", trans_a=True, out_dtype=BF, deps=deps)

    def mod_bwd(duv, xv, dr, a):
        return duv * (1.0 + a[1:2, :]) + dr, _colsum(duv), _colsum(duv * xv)

    dx, d_shift, d_scale = rowwise("mod_bwd", mod_bwd, S, 128, [(du, D_MODEL, 0), (x, D_MODEL, 0), (dres, D_MODEL, 0)],
                                   [ada3], [(D_MODEL, F32)], [D_MODEL] * 2)
    d_ada = jnp.concatenate([d_shift, d_scale, d_gate], axis=1)
    return dx, G, d_ada


SMALL = ("conv_a_b", "ln_a_g", "ln_a_b", "q_norm_g", "kv_norm_g", "ln_g", "ln_b")


def _rows(v):
    n = v.shape[0]
    r = -(-n // (LANE * 16)) * 16
    return jnp.pad(v, (0, r * LANE - n)).reshape(r, LANE)


def kernel(x, c, positions, w_ada, b_ada, w_in, conv_a_w, conv_a_b, ln_a_g, ln_a_b, w_a_out, conv_b_w, w_b_out, q_norm_g, kv_norm_g, w_uq, w_ukv, w_c_out, w_o, ln_g, ln_b, loss_target, m_w_ada, m_b_ada, m_w_in, m_conv_a_w, m_conv_a_b, m_ln_a_g, m_ln_a_b, m_w_a_out, m_conv_b_w, m_w_b_out, m_q_norm_g, m_kv_norm_g, m_w_uq, m_w_ukv, m_w_c_out, m_w_o, m_ln_g, m_ln_b, v_w_ada, v_b_ada, v_w_in, v_conv_a_w, v_conv_a_b, v_ln_a_g, v_ln_a_b, v_w_a_out, v_conv_b_w, v_w_b_out, v_q_norm_g, v_kv_norm_g, v_w_uq, v_w_ukv, v_w_c_out, v_w_o, v_ln_g, v_ln_b):
    P = dict(w_ada=w_ada, b_ada=b_ada, w_in=w_in, conv_a_w=conv_a_w, conv_a_b=conv_a_b, ln_a_g=ln_a_g, ln_a_b=ln_a_b,
             w_a_out=w_a_out, conv_b_w=conv_b_w, w_b_out=w_b_out, q_norm_g=q_norm_g, kv_norm_g=kv_norm_g, w_uq=w_uq,
             w_ukv=w_ukv, w_c_out=w_c_out, w_o=w_o, ln_g=ln_g, ln_b=ln_b)
    Mo = dict(w_ada=m_w_ada, b_ada=m_b_ada, w_in=m_w_in, conv_a_w=m_conv_a_w, conv_a_b=m_conv_a_b, ln_a_g=m_ln_a_g,
              ln_a_b=m_ln_a_b, w_a_out=m_w_a_out, conv_b_w=m_conv_b_w, w_b_out=m_w_b_out, q_norm_g=m_q_norm_g,
              kv_norm_g=m_kv_norm_g, w_uq=m_w_uq, w_ukv=m_w_ukv, w_c_out=m_w_c_out, w_o=m_w_o, ln_g=m_ln_g, ln_b=m_ln_b)
    Vo = dict(w_ada=v_w_ada, b_ada=v_b_ada, w_in=v_w_in, conv_a_w=v_conv_a_w, conv_a_b=v_conv_a_b, ln_a_g=v_ln_a_g,
              ln_a_b=v_ln_a_b, w_a_out=v_w_a_out, conv_b_w=v_conv_b_w, w_b_out=v_w_b_out, q_norm_g=v_q_norm_g,
              kv_norm_g=v_kv_norm_g, w_uq=v_w_uq, w_ukv=v_w_ukv, w_c_out=v_w_c_out, w_o=v_w_o, ln_g=v_ln_g, ln_b=v_ln_b)
    ORDER = ("w_ada", "b_ada", "w_in", "conv_a_w", "conv_a_b", "ln_a_g", "ln_a_b", "w_a_out", "conv_b_w", "w_b_out",
             "q_norm_g", "kv_norm_g", "w_uq", "w_ukv", "w_c_out", "w_o", "ln_g", "ln_b")
    L = w_ada.shape[0]
    S = x.shape[1]
    me = 4 * lax.axis_index("x") + 2 * lax.axis_index("y") + lax.axis_index("c")
    x2 = x[0]
    tgt = loss_target[0]

    small_in = _rows(jnp.concatenate([c.reshape(-1), conv_a_w.reshape(-1), conv_b_w.reshape(-1)]))
    w_in_b = jnp.pad(w_in.astype(BF), ((0, 0), (0, 0), (0, W_IN_PAD - IN_SHARD)))
    misc_b = jnp.concatenate([w_a_out, w_b_out, w_c_out, w_ukv, jnp.pad(w_uq, ((0, 0), (0, 0), (0, LANE - UQ_SHARD)))],
                             axis=1).astype(BF)
    w_o_b = w_o.astype(BF)
    gathered = [None] * L
    gathered[0] = [gather_two_level("gather0_w_in", w_in_b[0])]
    pending_rest, _ = exchange_begin("gather0_rest", [misc_b[0], w_o_b[0]], 2, gathered[0][0])
    sg = exchange("gather_small", [small_in], [])[0]
    sgf = sg.reshape(NDEV, -1)
    c_all = sgf[:, :D_MODEL]
    o1 = D_MODEL + L * 31 * 64
    conv_a_full = sgf[:, D_MODEL:o1].reshape(NDEV, L, 31, 64).transpose(1, 2, 0, 3).reshape(L, 31, CW)
    conv_b_full = sgf[:, o1:o1 + L * 3 * 64].reshape(NDEV, L, 3, 64).transpose(1, 2, 0, 3).reshape(L, 3, CW)

    c_act = rowwise("silu_c", _silu, 16, 16, [(jnp.pad(c_all, ((0, 8), (0, 0))), D_MODEL, 0)], [], [(D_MODEL, BF)])[0]
    ncol = w_ada.shape[2]
    w_ada_b = w_ada.astype(BF).transpose(1, 0, 2).reshape(D_MODEL, L * ncol)
    b_mine = lax.dynamic_slice_in_dim(b_ada, me * ncol, ncol, axis=1).reshape(1, L * ncol)
    ada_part = mm(c_act, w_ada_b, name="mm_ada", bias=b_mine)
    ada_rows = -(-(L * ncol) // (LANE * 8)) * 8
    ada_send = jnp.pad(ada_part[:NDEV].reshape(NDEV, -1, LANE), ((0, 0), (0, ada_rows - L * ncol // LANE), (0, 0)))
    ada_recv = exchange("a2a_ada", [], [ada_send])[0]
    ada = ada_recv[:, :L * ncol // LANE].reshape(NDEV, L, ncol).transpose(1, 0, 2).reshape(L, 3, D_MODEL)

    inv_freq = ROPE_THETA ** (-jnp.arange(0, ROPE, 2, dtype=F32) / ROPE)
    ang = positions[0].astype(F32)[:, None] * inv_freq
    tabs = (jnp.tile(jnp.cos(ang), (1, 2 * LANE // ROPE)), jnp.tile(jnp.sin(ang), (1, 2 * LANE // ROPE)))

    straight = np.arange(D_MODEL)
    fwd_in = [(0, 0, 0, 0, D_MODEL, _to_gathered(IN_PERM, IN_SHARD, W_IN_PAD))]
    fwd_misc = [(0, ROW_A, 0, 0, CW, straight), (0, ROW_B, 1, 0, CW, straight), (0, ROW_C, 2, 0, CW, straight),
                (0, ROW_UKV, 3, 0, KVL, UKV_PERM), (0, ROW_UQ, 4, 0, QL, _to_gathered(UQ_PERM, UQ_SHARD, LANE))]
    rev_in = [(0, 0, 0, 0, D_MODEL, _from_full(_inverse(IN_PERM, D_IN), IN_SHARD, W_IN_PAD))]
    rev_misc = [(0, 0, 0, ROW_A, CW, straight), (1, 0, 0, ROW_B, CW, straight), (2, 0, 0, ROW_C, CW, straight),
                (3, 0, 0, ROW_UKV, KVL, _from_full(_inverse(UKV_PERM, HEADS * (NOPE + VH)), LANE, LANE)),
                (4, 0, 0, ROW_UQ, QL, _from_full(_inverse(UQ_PERM, HEADS * (NOPE + ROPE)), UQ_SHARD, LANE))]

    def layer_weights(l, deps):
        w_in_p = col_gather("relayout_w_in", [gathered[l][0]], [(D_MODEL, NP)], fwd_in, deps)[0]

        def late(after):
            if len(gathered[l]) == 1:
                gathered[l] += exchange_end(pending_rest, after)
            _, g_misc, g_o = gathered[l]
            a_out, b_out, c_out, ukv, uq = col_gather(
                "relayout_misc", [g_misc],
                [(CW, D_MODEL)] * 3 + [(KVL, HEADS * (NOPE + VH)), (QL, HEADS * (NOPE + ROPE))], fwd_misc, deps)
            return {"a_out": a_out, "b_out": b_out, "c_out": c_out, "uq": uq, "ukv": ukv,
                    "o": g_o.reshape(D_MODEL, D_MODEL)}

        return {
            "in": w_in_p, "late": late,
            "conv_a": jnp.pad(conv_a_full[l], ((0, 1), (0, 0))), "conv_b": jnp.pad(conv_b_full[l], ((0, 5), (0, 0))),
            "vec_a": jnp.stack([conv_a_b[l], ln_a_g[l], ln_a_b[l]]), "gq": q_norm_g[l][None], "gkv": kv_norm_g[l][None],
            "lnv": jnp.stack([ln_g[l], ln_b[l]]),
        }

    h = x2
    saved, weights = [], []
    for l in range(L):
        ada_l, deps = ada[l], ()
        if l + 1 < L:
            pending, token = exchange_begin("gather%d" % (l + 1), [w_in_b[l + 1], misc_b[l + 1], w_o_b[l + 1]], 3,
                                            gathered[l][0])
            ada_l, deps = ada_l + token[0, 0], (token,)
        h, sv, Wl = layer_fwd(h, ada_l, layer_weights(l, deps), tabs, S)
        if l + 1 < L:
            gathered[l + 1] = exchange_end(pending, h)
        saved.append(sv)
        weights.append(Wl)

    def loss_fn(y, t):
        e = y - t
        return e * (1.0 / D_MODEL), _colsum(e * e)

    dy, sq = rowwise("loss", loss_fn, S, 256, [(h, D_MODEL, 0), (tgt, D_MODEL, 0)], [], [(D_MODEL, F32)], [D_MODEL])
    loss = lax.psum(0.5 * jnp.sum(sq) / D_MODEL, ("x", "y", "c"))

    grads, d_adas, recv = [None] * L, [None] * L, [None] * L
    pending, token = None, None

    def send_rest(g):
        send_misc = col_gather("unrelayout_misc", [g["w_a_out"], g["w_b_out"], g["w_c_out"], g["w_ukv"], g["w_uq"]],
                               [(NDEV, MISC_ROWS, LANE)], rev_misc)[0]
        return [send_misc, g["w_o"].reshape(NDEV, D_MODEL // NDEV, D_MODEL)]

    rest0 = []

    def early_rest(g):
        handle, tok = exchange_begin("scatter0_rest", send_rest(g), 0, g["w_o"])
        rest0.append(handle)
        return (tok,)

    for l in reversed(range(L)):
        ada_l = ada[l] if token is None else ada[l] + token[0, 0]
        dy, g, d_adas[l] = layer_bwd(dy, saved[l], ada_l, weights[l], tabs, S, early_rest if l == 0 else None)
        grads[l] = g
        if pending is not None:
            recv[l + 1] = exchange_end(pending, dy)
        send_in = col_gather("unrelayout_w_in", [g["w_in"]], [(NDEV, D_MODEL, W_IN_PAD)], rev_in)[0]
        if l == 0:
            pending, token = exchange_begin("scatter0", [send_in], 0, recv[1][0])
        else:
            pending, token = exchange_begin("scatter%d" % l, [send_in] + send_rest(g), 0,
                                            dy if l + 1 == L else recv[l + 1][0])
    grad_x = dy[None]

    small_parts = [jnp.stack([grads[l][n].reshape(-1) for l in range(L)]).reshape(-1) for n in SMALL]
    small_parts.append(jnp.stack([grads[l]["conv_a_w"][:31].reshape(-1) for l in range(L)]).reshape(-1))
    small_parts.append(jnp.stack([grads[l]["conv_b_w"][:3].reshape(-1) for l in range(L)]).reshape(-1))
    small_parts.append(jnp.stack([d_adas[l].reshape(-1) for l in range(L)]).reshape(-1))
    small_sizes = [int(p.shape[0]) for p in small_parts]
    gsmall = exchange("gather_small_grads", [_rows(jnp.concatenate(small_parts))], [])[0]
    gsum = sum_slots("sum_small", gsmall).reshape(-1)
    recv[0] = [None] + exchange_end(rest0[0], gsum)
    Gr = {}
    offs = np.cumsum([0] + small_sizes)
    for i, n in enumerate(SMALL):
        Gr[n] = gsum[offs[i]:offs[i + 1]].reshape(L, -1)
    ca = gsum[offs[7]:offs[8]].reshape(L, 31, CW)
    cbw = gsum[offs[8]:offs[9]].reshape(L, 3, CW)
    Gr["conv_a_w"] = lax.dynamic_slice_in_dim(ca, me * 64, 64, axis=2)
    Gr["conv_b_w"] = lax.dynamic_slice_in_dim(cbw, me * 64, 64, axis=2)
    Gr["b_ada"] = gsum[offs[9]:offs[10]].reshape(L, 3 * D_MODEL)
    d_ada_all = gsmall.reshape(NDEV, -1)[:, offs[9]:offs[10]].reshape(NDEV, L, 3 * D_MODEL)
    d_mine = lax.dynamic_slice_in_dim(d_ada_all, me * ncol, ncol, axis=2).reshape(NDEV, L * ncol)
    g_ada = mm(c_act, jnp.pad(d_mine, ((0, 8), (0, 0))).astype(BF), name="mm_gw_ada", trans_a=True)
    Gr["w_ada"] = g_ada.reshape(D_MODEL, L, ncol).transpose(1, 0, 2)

    D, NM, NV = {}, {}, {}
    D["w_ada"], NM["w_ada"], NV["w_ada"] = adamw("adamw_w_ada", P["w_ada"], Gr["w_ada"], Mo["w_ada"], Vo["w_ada"])
    Gr["w_o"], D["w_o"], NM["w_o"], NV["w_o"] = sum_adamw(
        "sum_adamw_w_o", [recv[l][2] for l in range(L)], P["w_o"], Mo["w_o"], Vo["w_o"])
    misc = lambda T_: jnp.concatenate([T_["w_a_out"], T_["w_b_out"], T_["w_c_out"], T_["w_ukv"],
                                       jnp.pad(T_["w_uq"], ((0, 0), (0, 0), (0, LANE - UQ_SHARD)))], axis=1)
    res = sum_adamw("sum_adamw_misc", [recv[l][1] for l in range(L)], misc(P), misc(Mo), misc(Vo))
    for T_, r in zip((Gr, D, NM, NV), res):
        T_["w_a_out"], T_["w_b_out"], T_["w_c_out"] = r[:, ROW_A:ROW_B], r[:, ROW_B:ROW_C], r[:, ROW_C:ROW_UKV]
        T_["w_ukv"], T_["w_uq"] = r[:, ROW_UKV:ROW_UQ], r[:, ROW_UQ:MISC_ROWS, :UQ_SHARD]
    recv[0][0] = exchange_end(pending, res[1])[0]
    Gr["w_in"], D["w_in"], NM["w_in"], NV["w_in"] = sum_adamw(
        "sum_adamw_w_in", [recv[l][0] for l in range(L)], P["w_in"], Mo["w_in"], Vo["w_in"])
    packed =("b_ada", "conv_a_w", "conv_b_w") + SMALL
    pk = lambda T_: _rows(jnp.concatenate([T_[n].reshape(-1) for n in packed]))[None]
    dS, mS, vS = adamw("adamw_small", pk(P), pk(Gr), pk(Mo), pk(Vo))
    o = 0
    for n in packed:
        sz = int(np.prod(P[n].shape))
        D[n] = dS.reshape(-1)[o:o + sz].reshape(P[n].shape)
        NM[n] = mS.reshape(-1)[o:o + sz].reshape(P[n].shape)
        NV[n] = vS.reshape(-1)[o:o + sz].reshape(P[n].shape)
        o += sz
    return (loss, grad_x, *[Gr[n] for n in ORDER], *[D[n] for n in ORDER], *[NM[n] for n in ORDER],
            *[NV[n] for n in ORDER])
```

```python
import functools
import math

import numpy as np
import jax
import jax.numpy as jnp
from jax import lax
from jax.experimental import pallas as pl
from jax.experimental.pallas import tpu as pltpu

BF = jnp.bfloat16
F32 = jnp.float32
MESH = pl.DeviceIdType.MESH
NDEV = 8

HEADS, NOPE, ROPE, VH = 8, 64, 32, 64
HP = 128
ROPE_THETA = 10000.0
LN_EPS = 1e-5
RMS_EPS = 1e-6
LR, B1, B2, EPS, WD, STEP = 0.001, 0.9, 0.999, 1e-08, 0.01, 10

LANE = 128
VMEM_LIMIT = 56 * 1024 * 1024

D_MODEL, CW, QL, KVL = 1024, 512, 384, 256
OFF_M, OFF_A, OFF_AG, OFF_B, OFF_CG, OFF_GB, OFF_BG = 0, 3072, 4096, 4608, 5632, 6144, 6656
OFF_KV, OFF_KR, OFF_Q, NP = 7168, 7424, 7680, 8192
D_IN = 7840


def _cparams(**kw):
    return pltpu.CompilerParams(vmem_limit_bytes=VMEM_LIMIT, **kw)


def _sigmoid(x):
    return jax.nn.sigmoid(x)


def _silu(x):
    return x * _sigmoid(x)


def _dsilu(x):
    s = _sigmoid(x)
    return s * (1.0 + x * (1.0 - s))


def _pick_tile(n, cap, mult):
    if n <= cap:
        return n
    for t in range(cap - cap % mult, 0, -mult):
        if n % t == 0:
            return t
    raise ValueError((n, cap, mult))


def mm(a, b, *, name, trans_a=False, trans_b=False, out_dtype=F32, bias=None, tm=1024, tn=1024, tk=2048, deps=()):
    if trans_a:
        K, M = a.shape
    else:
        M, K = a.shape
    if trans_b:
        N, K2 = b.shape
    else:
        K2, N = b.shape
    assert K == K2 and not (trans_a and trans_b), (a.shape, b.shape)
    tm, tn = _pick_tile(M, tm, 16), _pick_tile(N, tn, LANE)
    tk = _pick_tile(K, tk, LANE if trans_b else 16)
    assert M % tm == 0 and N % tn == 0 and K % tk == 0, (M, N, K, tm, tn, tk)
    nk = K // tk
    dims = (((0 if trans_a else 1,), (1 if trans_b else 0,)), ((), ()))
    has_bias = bias is not None

    def body(*refs):
        a_ref, b_ref = refs[0], refs[1]
        bias_ref = refs[2] if has_bias else None
        o_ref = refs[(3 if has_bias else 2) + len(deps)]
        p = lax.dot_general(a_ref[...], b_ref[...], dims, preferred_element_type=F32)

        def finish(v):
            if has_bias:
                v = v + bias_ref[...]
            o_ref[...] = v.astype(o_ref.dtype)

        if nk == 1:
            finish(p)
        else:
            acc = refs[-1]
            k = pl.program_id(2)

            @pl.when(k == 0)
            def _():
                acc[...] = p

            @pl.when(k > 0)
            def _():
                acc[...] += p

            @pl.when(k == nk - 1)
            def _():
                finish(acc[...])

    if trans_a:
        a_spec = pl.BlockSpec((tk, tm), lambda i, j, k: (k, i))
    else:
        a_spec = pl.BlockSpec((tm, tk), lambda i, j, k: (i, k))
    if trans_b:
        b_spec = pl.BlockSpec((tn, tk), lambda i, j, k: (j, k))
    else:
        b_spec = pl.BlockSpec((tk, tn), lambda i, j, k: (k, j))
    in_specs = [a_spec, b_spec]
    args = [a, b]
    if has_bias:
        in_specs.append(pl.BlockSpec((1, tn), lambda i, j, k: (0, j)))
        args.append(bias)
    in_specs += [ANY_SPEC] * len(deps)
    args += list(deps)
    return pl.pallas_call(
        body, name=name, grid=(M // tm, N // tn, nk),
        in_specs=in_specs, out_specs=pl.BlockSpec((tm, tn), lambda i, j, k: (i, j)),
        out_shape=jax.ShapeDtypeStruct((M, N), out_dtype),
        scratch_shapes=[pltpu.VMEM((tm, tn), F32)] if nk > 1 else [],
        compiler_params=_cparams(),
    )(*args)


def rowwise(name, fn, S, T, row_ins, full_ins, row_outs, acc_outs=(), into=None):
    n_in = len(row_ins) + len(full_ins)
    n_ro, n_ao = len(row_outs), len(acc_outs)
    alias = into is not None and into[0] is not None

    def body(*refs):
        vals = [r[...] for r in refs[:n_in]]
        vals = [v.astype(F32) if v.dtype == BF else v for v in vals]
        outs = fn(*vals)
        if not isinstance(outs, (tuple, list)):
            outs = (outs,)
        assert len(outs) == n_ro + n_ao, (name, len(outs))
        o0 = n_in + (1 if alias else 0)
        for r, v in zip(refs[o0:o0 + n_ro], outs[:n_ro]):
            r[...] = v.astype(r.dtype)
        first = pl.program_id(0) == 0
        for r, v in zip(refs[o0 + n_ro:], outs[n_ro:]):
            def init(r=r, v=v):
                r[...] = v

            def accum(r=r, v=v):
                r[...] += v

            pl.when(first)(init)
            pl.when(jnp.logical_not(first))(accum)

    in_specs, args = [], []
    for arr, W, off in row_ins:
        assert off % W == 0 and arr.shape[0] == S, (name, arr.shape, W, off)
        in_specs.append(pl.BlockSpec((T, W), functools.partial(lambda i, cb: (i, cb), cb=off // W)))
        args.append(arr)
    for arr in full_ins:
        in_specs.append(pl.BlockSpec(arr.shape, lambda i: (0, 0)))
        args.append(arr)
    out_specs = [pl.BlockSpec((T, W), lambda i: (i, 0)) for W, _ in row_outs]
    out_shape = [jax.ShapeDtypeStruct((S, W), dt) for W, dt in row_outs]
    aliases = {}
    if into is not None:
        buf, total, off = into
        W0, dt0 = row_outs[0]
        assert off % W0 == 0
        out_specs[0] = pl.BlockSpec((T, W0), functools.partial(lambda i, cb: (i, cb), cb=off // W0))
        out_shape[0] = jax.ShapeDtypeStruct((S, total), dt0)
        if alias:
            in_specs.append(ANY_SPEC)
            args.append(buf)
            aliases = {n_in: 0}
    out_specs += [pl.BlockSpec((1, W), lambda i: (0, 0)) for W in acc_outs]
    out_shape += [jax.ShapeDtypeStruct((1, W), F32) for W in acc_outs]
    return pl.pallas_call(
        body, name=name, grid=(S // T,), in_specs=in_specs, out_specs=out_specs, out_shape=out_shape,
        input_output_aliases=aliases, compiler_params=_cparams(),
    )(*args)


def _colsum(v):
    return jnp.sum(v, axis=0, keepdims=True)


def _ln_stats(r):
    mu = jnp.mean(r, axis=-1, keepdims=True)
    d = r - mu
    var = jnp.mean(d * d, axis=-1, keepdims=True)
    rstd = lax.rsqrt(var + LN_EPS)
    return d * rstd, rstd


def _ln_bwd(dn, n, rstd):
    return rstd * (dn - jnp.mean(dn, axis=-1, keepdims=True) - n * jnp.mean(dn * n, axis=-1, keepdims=True))


CPAD = 32
TC = 64


def _pre(mode, x1, x2):
    return x1 * _sigmoid(x2) if mode == "glu" else x1 * x2


def _shifted(ext, sft):
    n = TC + CPAD
    return pltpu.roll(ext, (n - sft) % n, 0)[0:TC]


def _interleaved_specs(S, off):
    return [pl.BlockSpec((S, LANE), functools.partial(lambda j, o: (0, o + 2 * j), o=off // LANE)),
            pl.BlockSpec((S, LANE), functools.partial(lambda j, o: (0, o + 2 * j + 1), o=off // LANE))]


def conv_fwd(name, src, off, w_pad, taps, mode, S, C):
    nchunk = S // TC

    def body(x1_ref, x2_ref, w_ref, o_ref, a_pad):
        a_pad[0:CPAD, :] = jnp.zeros((CPAD, LANE), F32)

        def fill(i, _):
            r = pl.multiple_of(i * 256, 256)
            a_pad[pl.ds(CPAD + r, 256), :] = _pre(mode, x1_ref[pl.ds(r, 256), :].astype(F32),
                                                  x2_ref[pl.ds(r, 256), :].astype(F32))
            return 0

        lax.fori_loop(0, S // 256, fill, 0)

        def chunk(i, _):
            base = pl.multiple_of(i * TC, TC)
            ext = a_pad[pl.ds(base, TC + CPAD), :]
            acc = jnp.zeros((TC, LANE), F32)
            for k in range(taps):
                acc = acc + w_ref[pl.ds(k, 1), :] * _shifted(ext, CPAD - (taps - 1) + k)
            o_ref[pl.ds(base, TC), :] = acc
            return 0

        lax.fori_loop(0, nchunk, chunk, 0)

    kp = w_pad.shape[0]
    return pl.pallas_call(
        body, name=name, grid=(C // LANE,),
        in_specs=_interleaved_specs(S, off) + [pl.BlockSpec((kp, LANE), lambda j: (0, j))],
        out_specs=pl.BlockSpec((S, LANE), lambda j: (0, j)),
        out_shape=jax.ShapeDtypeStruct((S, C), F32),
        scratch_shapes=[pltpu.VMEM((S + CPAD, LANE), F32)],
        compiler_params=_cparams(),
    )(src, src, w_pad)


def conv_bwd(name, src, off, dc, w_pad, taps, mode, S, C, buf):
    nchunk = S // TC
    kp = w_pad.shape[0]

    def body(x1_ref, x2_ref, dc_ref, w_ref, _, d_ref, dw_ref, a_pad, dc_pad, dw_acc):
        a_pad[0:CPAD, :] = jnp.zeros((CPAD, LANE), F32)
        dc_pad[S:S + CPAD, :] = jnp.zeros((CPAD, LANE), F32)
        dw_acc[...] = jnp.zeros(dw_acc.shape, F32)

        def fill(i, _):
            r = pl.multiple_of(i * 256, 256)
            a_pad[pl.ds(CPAD + r, 256), :] = _pre(mode, x1_ref[pl.ds(r, 256), :].astype(F32),
                                                  x2_ref[pl.ds(r, 256), :].astype(F32))
            dc_pad[pl.ds(r, 256), :] = dc_ref[pl.ds(r, 256), :]
            return 0

        lax.fori_loop(0, S // 256, fill, 0)

        def chunk(i, _):
            base = pl.multiple_of(i * TC, TC)
            ext_d = dc_pad[pl.ds(base, TC + CPAD), :]
            ext_a = a_pad[pl.ds(base, TC + CPAD), :]
            dcv = ext_d[0:TC]
            da = jnp.zeros((TC, LANE), F32)
            for k in range(taps):
                da = da + w_ref[pl.ds(k, 1), :] * _shifted(ext_d, taps - 1 - k)
                prod = dcv * _shifted(ext_a, CPAD - (taps - 1) + k)
                fold = prod[0:8]
                for g in range(1, TC // 8):
                    fold = fold + prod[8 * g:8 * g + 8]
                dw_acc[pl.ds(8 * k, 8), :] += fold
            x1 = x1_ref[pl.ds(base, TC), :].astype(F32)
            x2 = x2_ref[pl.ds(base, TC), :].astype(F32)
            if mode == "glu":
                s = _sigmoid(x2)
                d1, d2 = da * s, da * x1 * s * (1.0 - s)
            else:
                d1, d2 = da * x2, da * x1
            d_ref[pl.ds(base, TC), 0:LANE] = d1.astype(BF)
            d_ref[pl.ds(base, TC), LANE:2 * LANE] = d2.astype(BF)
            return 0

        lax.fori_loop(0, nchunk, chunk, 0)
        dw_ref[...] = jnp.zeros(dw_ref.shape, F32)
        for k in range(taps):
            dw_ref[pl.ds(k, 1), :] = jnp.sum(dw_acc[pl.ds(8 * k, 8), :], axis=0, keepdims=True)

    blk = pl.BlockSpec((S, LANE), lambda j: (0, j))
    return pl.pallas_call(
        body, name=name, grid=(C // LANE,),
        in_specs=_interleaved_specs(S, off) + [blk, pl.BlockSpec((kp, LANE), lambda j: (0, j)), ANY_SPEC],
        out_specs=[pl.BlockSpec((S, 2 * LANE), functools.partial(lambda j, o: (0, o + j), o=off // (2 * LANE))),
                   pl.BlockSpec((kp, LANE), lambda j: (0, j))],
        out_shape=[jax.ShapeDtypeStruct(buf.shape, BF), jax.ShapeDtypeStruct((kp, C), F32)],
        input_output_aliases={4: 0},
        scratch_shapes=[pltpu.VMEM((S + CPAD, LANE), F32), pltpu.VMEM((S + CPAD, LANE), F32),
                        pltpu.VMEM((8 * kp, LANE), F32)],
        compiler_params=_cparams(),
    )(src, src, dc, w_pad, buf)


FWD_TILES = (512, 512)
BWD_TILES = (512, 512)
QUADS = HEADS // 4
QW, KVW = 4 * (NOPE + ROPE), 4 * (NOPE + VH)
SCALE = (NOPE + ROPE) ** -0.5
NT_DIMS = (((1,), (1,)), ((), ()))
TN_DIMS = (((0,), (0,)), ((), ()))


def _lane_mask(width, group, dtype):
    lane = lax.broadcasted_iota(jnp.int32, (1, LANE), 1)
    return jnp.where(lane // width == group, 1.0, 0.0).astype(dtype)


def _visible(tq, tk, off):
    row = lax.broadcasted_iota(jnp.int32, (tq, tk), 0)
    col = lax.broadcasted_iota(jnp.int32, (tq, tk), 1)
    return col <= row + off


def _attn_tiles(S, tq, tk):
    tk = tk if S % tk == 0 else 256
    return min(tq, tk), tk


def attn_fwd(q, kv, kpe, S):
    tq, tk = _attn_tiles(S, *FWD_TILES)
    nq = S // tq

    def body(q_ref, kv_ref, kp_ref, o_ref, lse_ref):
        for t in range(2):
            cols = slice(t * LANE, (t + 1) * LANE)
            for hh in range(2):
                def q_block(qi, _, t=t, hh=hh, cols=cols):
                    r0 = pl.multiple_of(qi * tq, tq)
                    qcat = jnp.concatenate([q_ref[pl.ds(r0, tq), cols] * _lane_mask(NOPE, hh, BF),
                                            q_ref[pl.ds(r0, tq), 2 * LANE:3 * LANE] * _lane_mask(ROPE, 2 * t + hh, BF)],
                                           axis=1)
                    nfull = (qi * tq) // tk

                    def step(kj, carry, masked):
                        m, l, acc = carry
                        c0 = pl.multiple_of(kj * tk, tk)
                        kc = jnp.concatenate([kv_ref[pl.ds(c0, tk), cols], kp_ref[pl.ds(c0, tk), :]], axis=1)
                        vt = kv_ref[pl.ds(c0, tk), (2 + t) * LANE:(3 + t) * LANE]
                        s = lax.dot_general(qcat, kc, NT_DIMS, preferred_element_type=F32) * SCALE
                        if masked:
                            s = jnp.where(_visible(tq, tk, qi * tq - nfull * tk), s, -jnp.inf)
                        m_new = jnp.maximum(m, jnp.max(s, axis=-1, keepdims=True))
                        p = jnp.exp(s - m_new)
                        alpha = jnp.exp(m - m_new)
                        l = alpha * l + jnp.sum(p, axis=-1, keepdims=True)
                        acc = alpha * acc + jnp.dot(p.astype(BF), vt, preferred_element_type=F32)
                        return m_new, l, acc

                    init = (jnp.full((tq, 1), -jnp.inf, F32), jnp.zeros((tq, 1), F32), jnp.zeros((tq, LANE), F32))
                    carry = lax.fori_loop(0, nfull, lambda kj, c: step(kj, c, False), init)
                    m, l, acc = step(nfull, carry, True)
                    mine = _lane_mask(NOPE, hh, F32)
                    if hh == 0:
                        o_ref[pl.ds(r0, tq), cols] = (acc / l) * mine
                        lse_ref[pl.ds(r0, tq), cols] = (m + jnp.log(l)) * mine
                    else:
                        o_ref[pl.ds(r0, tq), cols] += (acc / l) * mine
                        lse_ref[pl.ds(r0, tq), cols] += (m + jnp.log(l)) * mine
                    return 0

                lax.fori_loop(0, nq, q_block, 0)

    return pl.pallas_call(
        body, name="attn_fwd", grid=(QUADS,),
        in_specs=[pl.BlockSpec((S, QW), lambda g: (0, g)), pl.BlockSpec((S, KVW), lambda g: (0, g)),
                  pl.BlockSpec((S, LANE), lambda g: (0, 0))],
        out_specs=[pl.BlockSpec((S, 2 * LANE), lambda g: (0, g))] * 2,
        out_shape=[jax.ShapeDtypeStruct((S, HEADS * VH), F32)] * 2,
        compiler_params=_cparams(),
    )(q, kv, kpe)


def attn_bwd(q, kv, kpe, o, lse, do, S):
    tq, tk = _attn_tiles(S, *BWD_TILES)
    nq = S // tq

    def body(q_ref, kv_ref, kp_ref, o_ref, lse_ref, do_ref, dq_ref, dkv_ref, dkp_ref, dq_acc, dk_acc, dv_acc):
        for t in range(2):
            cols = slice(t * LANE, (t + 1) * LANE)
            dk_acc[...] = jnp.zeros(dk_acc.shape, F32)
            dv_acc[...] = jnp.zeros(dv_acc.shape, F32)
            for hh in range(2):
                def q_block(qi, _, t=t, hh=hh, cols=cols):
                    r0 = pl.multiple_of(qi * tq, tq)
                    mine = _lane_mask(NOPE, hh, F32)
                    qcat = jnp.concatenate([q_ref[pl.ds(r0, tq), cols] * _lane_mask(NOPE, hh, BF),
                                            q_ref[pl.ds(r0, tq), 2 * LANE:3 * LANE] * _lane_mask(ROPE, 2 * t + hh, BF)],
                                           axis=1)
                    dof = do_ref[pl.ds(r0, tq), cols] * mine
                    dob = dof.astype(BF)
                    delta = jnp.sum(dof * o_ref[pl.ds(r0, tq), cols], axis=-1, keepdims=True)
                    lse_h = lse_ref[pl.ds(r0, tq), cols][:, hh * NOPE:hh * NOPE + 1]
                    nfull = (qi * tq) // tk
                    dq_acc[...] = jnp.zeros(dq_acc.shape, F32)

                    def step(kj, _, masked):
                        c0 = pl.multiple_of(kj * tk, tk)
                        kc = jnp.concatenate([kv_ref[pl.ds(c0, tk), cols], kp_ref[pl.ds(c0, tk), :]], axis=1)
                        vt = kv_ref[pl.ds(c0, tk), (2 + t) * LANE:(3 + t) * LANE]
                        s = lax.dot_general(qcat, kc, NT_DIMS, preferred_element_type=F32) * SCALE
                        if masked:
                            s = jnp.where(_visible(tq, tk, qi * tq - nfull * tk), s, -jnp.inf)
                        p = jnp.exp(s - lse_h)
                        dp = lax.dot_general(dob, vt, NT_DIMS, preferred_element_type=F32)
                        ds = (p * (dp - delta) * SCALE).astype(BF)
                        dv_acc[pl.ds(c0, tk), :] += lax.dot_general(p.astype(BF), dob, TN_DIMS,
                                                                    preferred_element_type=F32)
                        dk_acc[pl.ds(c0, tk), :] += lax.dot_general(ds, qcat, TN_DIMS, preferred_element_type=F32)
                        dq_acc[...] += jnp.dot(ds, kc, preferred_element_type=F32)
                        return 0

                    lax.fori_loop(0, nfull, lambda kj, c: step(kj, c, False), 0)
                    step(nfull, 0, True)
                    d = dq_acc[...]
                    pe = d[:, LANE:] * _lane_mask(ROPE, 2 * t + hh, F32)
                    if hh == 0:
                        dq_ref[pl.ds(r0, tq), cols] = d[:, :LANE] * mine
                    else:
                        dq_ref[pl.ds(r0, tq), cols] += d[:, :LANE] * mine
                    if t == 0 and hh == 0:
                        dq_ref[pl.ds(r0, tq), 2 * LANE:3 * LANE] = pe
                    else:
                        dq_ref[pl.ds(r0, tq), 2 * LANE:3 * LANE] += pe
                    return 0

                lax.fori_loop(0, nq, q_block, 0)
            dkv_ref[:, t * LANE:(t + 1) * LANE] = dk_acc[:, :LANE].astype(BF)
            dkv_ref[:, (2 + t) * LANE:(3 + t) * LANE] = dv_acc[...].astype(BF)
            if t == 0:
                dkp_ref[...] = dk_acc[:, LANE:]
            else:
                dkp_ref[...] += dk_acc[:, LANE:]

    qspec = pl.BlockSpec((S, QW), lambda g: (0, g))
    kvspec = pl.BlockSpec((S, KVW), lambda g: (0, g))
    ospec = pl.BlockSpec((S, 2 * LANE), lambda g: (0, g))
    return pl.pallas_call(
        body, name="attn_bwd", grid=(QUADS,),
        in_specs=[qspec, kvspec, pl.BlockSpec((S, LANE), lambda g: (0, 0)), ospec, ospec, ospec],
        out_specs=[qspec, kvspec, pl.BlockSpec((S, LANE), lambda g: (0, g))],
        out_shape=[jax.ShapeDtypeStruct((S, HEADS * (NOPE + ROPE)), F32), jax.ShapeDtypeStruct((S, HEADS * (NOPE + VH)), BF),
                   jax.ShapeDtypeStruct((S, HEADS * ROPE), F32)],
        scratch_shapes=[pltpu.VMEM((tq, 2 * LANE), F32), pltpu.VMEM((S, 2 * LANE), F32), pltpu.VMEM((S, LANE), F32)],
        compiler_params=_cparams(),
    )(q, kv, kpe, o, lse, do)


def exchange(name, gathers, a2as):
    n_g, n = len(gathers), len(gathers) + len(a2as)

    def body(*refs):
        ins, outs = refs[:n], refs[n:2 * n]
        send_sems, recv_sems, loc_sems = refs[2 * n:]
        x, y, c = lax.axis_index("x"), lax.axis_index("y"), lax.axis_index("c")
        me = 4 * x + 2 * y + c

        def peer(k):
            px = 1 - x if k & 4 else x
            py = 1 - y if k & 2 else y
            pc = 1 - c if k & 1 else c
            return (px, py, pc), 4 * px + 2 * py + pc

        def remote(a, k):
            pid, pflat = peer(k)
            src = ins[a] if a < n_g else ins[a].at[pflat]
            return pltpu.make_async_remote_copy(
                src_ref=src, dst_ref=outs[a].at[me], send_sem=send_sems.at[a, k - 1], recv_sem=recv_sems.at[a, k - 1],
                device_id=pid, device_id_type=MESH)

        def arrival(a, k):
            pid, pflat = peer(k)
            src = ins[a] if a < n_g else ins[a].at[pflat]
            return pltpu.make_async_remote_copy(
                src_ref=src, dst_ref=outs[a].at[pflat], send_sem=send_sems.at[a, k - 1], recv_sem=recv_sems.at[a, k - 1],
                device_id=pid, device_id_type=MESH)

        local = []
        for a in range(n):
            own = ins[a] if a < n_g else ins[a].at[me]
            cp = pltpu.make_async_copy(own, outs[a].at[me], loc_sems.at[a])
            cp.start()
            local.append(cp)
        sent = []
        for k in (1, 2, 4, 3, 5, 6, 7):
            for a in range(n):
                cp = remote(a, k)
                cp.start()
                sent.append(cp)
        for k in range(1, 8):
            for a in range(n):
                arrival(a, k).wait_recv()
        for cp in sent:
            cp.wait_send()
        for cp in local:
            cp.wait()

    out_shape = [jax.ShapeDtypeStruct((NDEV,) + g.shape, g.dtype) for g in gathers]
    out_shape += [jax.ShapeDtypeStruct(a.shape, a.dtype) for a in a2as]
    any_spec = pl.BlockSpec(memory_space=pl.ANY)
    return pl.pallas_call(
        body, name=name, in_specs=[any_spec] * n, out_specs=[any_spec] * n, out_shape=out_shape,
        scratch_shapes=[pltpu.SemaphoreType.DMA((n, NDEV - 1)), pltpu.SemaphoreType.DMA((n, NDEV - 1)),
                        pltpu.SemaphoreType.DMA((n,))],
    )(*gathers, *a2as)


def gather_two_level(name, block):
    def body(x_ref, out_ref, stage, send_sems, recv_sems, loc_sem):
        x, y, c = lax.axis_index("x"), lax.axis_index("y"), lax.axis_index("c")
        me, sibling = (x, y, c), (x, y, 1 - c)
        chips = [(1 - x, y), (x, 1 - y), (1 - x, 1 - y)]

        def slot(px, py, pc):
            return out_ref.at[4 * px + 2 * py + pc]

        def copy(k, owner, to, src=None):
            return pltpu.make_async_remote_copy(
                src_ref=slot(*owner) if src is None else src, dst_ref=slot(*owner), send_sem=send_sems.at[k],
                recv_sem=recv_sems.at[k], device_id=to, device_id_type=MESH)

        load = pltpu.make_async_copy(x_ref, stage, loc_sem)
        load.start()
        first = [copy(0, me, sibling, src=x_ref)] + [copy(1 + j, me, (*chip, c), src=x_ref) for j, chip in enumerate(chips)]
        for cp in first:
            cp.start()
        load.wait()
        store = pltpu.make_async_copy(stage, slot(*me), loc_sem)
        store.start()
        passed = [copy(4 + j, (*chip, c), sibling) for j, chip in enumerate(chips)]
        for j, chip in enumerate(chips):
            copy(1 + j, (*chip, c), me).wait_recv()
            passed[j].start()
        copy(0, sibling, me).wait_recv()
        for j, chip in enumerate(chips):
            copy(4 + j, (*chip, 1 - c), me).wait_recv()
        for cp in first + passed:
            cp.wait_send()
        store.wait()

    return pl.pallas_call(
        body, name=name, in_specs=[pl.BlockSpec(memory_space=pl.ANY)], out_specs=pl.BlockSpec(memory_space=pl.ANY),
        out_shape=jax.ShapeDtypeStruct((NDEV,) + block.shape, block.dtype),
        scratch_shapes=[pltpu.VMEM(block.shape, block.dtype), pltpu.SemaphoreType.DMA((NDEV - 1,)),
                        pltpu.SemaphoreType.DMA((NDEV - 1,)), pltpu.SemaphoreType.DMA],
        compiler_params=_cparams(),
    )(block)


def _peer(k, x, y, c):
    px = 1 - x if k & 4 else x
    py = 1 - y if k & 2 else y
    pc = 1 - c if k & 1 else c
    return (px, py, pc), 4 * px + 2 * py + pc


PEER_ORDER = (1, 2, 4, 3, 5, 6, 7)
HBM_SPEC = pl.BlockSpec(memory_space=pltpu.HBM)
SEM_SPEC = pl.BlockSpec(memory_space=pltpu.SEMAPHORE)
ANY_SPEC = pl.BlockSpec(memory_space=pl.ANY)


def _split_copies(ins, lands, n_g, send_sems, recv_sems):
    x, y, c = lax.axis_index("x"), lax.axis_index("y"), lax.axis_index("c")
    me = 4 * x + 2 * y + c

    def outgoing(a, k):
        pid, pflat = _peer(k, x, y, c)
        src = ins[a] if a < n_g else ins[a].at[pflat]
        return pltpu.make_async_remote_copy(
            src_ref=src, dst_ref=lands[a].at[me], send_sem=send_sems.at[a * (NDEV - 1) + k - 1],
            recv_sem=recv_sems.at[a * (NDEV - 1) + k - 1],
            device_id=pid, device_id_type=MESH)

    def arrival(a, k):
        pid, pflat = _peer(k, x, y, c)
        src = ins[a] if a < n_g else ins[a].at[pflat]
        return pltpu.make_async_remote_copy(
            src_ref=src, dst_ref=lands[a].at[pflat], send_sem=send_sems.at[a * (NDEV - 1) + k - 1],
            recv_sem=recv_sems.at[a * (NDEV - 1) + k - 1],
            device_id=pid, device_id_type=MESH)

    return outgoing, arrival


def exchange_begin(name, srcs, n_g, dep):
    n = len(srcs)
    land_shapes = [((NDEV,) + s.shape) if a < n_g else s.shape for a, s in enumerate(srcs)]

    def own_body(*refs):
        ins, outs = refs[:n], refs[n + 1:2 * n + 1]
        stage, sems = refs[2 * n + 1:3 * n + 1], refs[-1]
        me = 4 * lax.axis_index("x") + 2 * lax.axis_index("y") + lax.axis_index("c")
        cps = [pltpu.make_async_copy(ins[a] if a < n_g else ins[a].at[me], stage[a], sems.at[a]) for a in range(n)]
        for cp in cps:
            cp.start()
        for cp in cps:
            cp.wait()
        cps = [pltpu.make_async_copy(stage[a], outs[a].at[me], sems.at[a]) for a in range(n)]
        for cp in cps:
            cp.start()
        for cp in cps:
            cp.wait()

    lands = pl.pallas_call(
        own_body, name=name + "_own", in_specs=[ANY_SPEC] * (n + 1), out_specs=[ANY_SPEC] * n,
        out_shape=[jax.ShapeDtypeStruct(sh, s.dtype) for sh, s in zip(land_shapes, srcs)],
        scratch_shapes=[pltpu.VMEM(sh[1:], s.dtype) for sh, s in zip(land_shapes, srcs)] + [pltpu.SemaphoreType.DMA((n,))],
        compiler_params=_cparams(),
    )(*srcs, dep)

    def start_body(*refs):
        ins, lz = refs[:n], refs[n:2 * n]
        send_sems, recv_sems, token = refs[2 * n], refs[2 * n + 1], refs[-1]
        outgoing, _ = _split_copies(ins, lz, n_g, send_sems, recv_sems)
        for k in PEER_ORDER:
            for a in range(n):
                outgoing(a, k).start()
        token[...] = jnp.zeros(token.shape, F32)

    hbm = lambda t: pltpu.HBM(t.shape, t.dtype)
    res = pl.pallas_call(
        start_body, name=name + "_start",
        out_shape=(pltpu.SemaphoreType.DMA((n * (NDEV - 1),)), pltpu.SemaphoreType.DMA((n * (NDEV - 1),)),
                   *[hbm(s) for s in srcs], *[hbm(t) for t in lands], jax.ShapeDtypeStruct((8, LANE), F32)),
        in_specs=[HBM_SPEC] * (2 * n),
        out_specs=(SEM_SPEC, SEM_SPEC, *[HBM_SPEC] * (2 * n), pl.BlockSpec(memory_space=pltpu.VMEM)),
        input_output_aliases={i: 2 + i for i in range(2 * n)},
        compiler_params=pltpu.CompilerParams(has_side_effects=pltpu.SideEffectType.DATAFLOW_SIDE_EFFECTING),
    )(*[pltpu.with_memory_space_constraint(t, pltpu.HBM) for t in list(srcs) + list(lands)])
    return (name, n, n_g, res[:-1]), res[-1]


def exchange_end(handle, after):
    name, n, n_g, (send_sems, recv_sems, *bufs) = handle

    def wait_body(*refs):
        ins, lz = refs[:n], refs[n:2 * n]
        ss, rs = refs[2 * n], refs[2 * n + 1]
        outgoing, arrival = _split_copies(ins, lz, n_g, ss, rs)
        for k in range(1, NDEV):
            for a in range(n):
                arrival(a, k).wait_recv()
        for k in range(1, NDEV):
            for a in range(n):
                outgoing(a, k).wait_send()

    res = pl.pallas_call(
        wait_body, name=name + "_wait", out_shape=tuple(pltpu.HBM(t.shape, t.dtype) for t in bufs),
        in_specs=[HBM_SPEC] * (2 * n) + [SEM_SPEC, SEM_SPEC, ANY_SPEC], out_specs=[HBM_SPEC] * (2 * n),
        input_output_aliases={i: i for i in range(2 * n)},
        compiler_params=pltpu.CompilerParams(has_side_effects=pltpu.SideEffectType.DATAFLOW_SIDE_EFFECTING),
    )(*bufs, send_sems, recv_sems, after)
    return list(res[n:])


def _pick_rows(R, mult, cap):
    best = None
    for n in range(1, R + 1):
        if R % n == 0 and (R // n) % mult == 0 and R // n <= cap:
            best = R // n
            break
    assert best is not None, (R, mult, cap)
    return best


def sum_slots(name, x):
    _, R, _ = x.shape
    tr = _pick_rows(R, 16, 2304)

    def body(x_ref, o_ref):
        acc = x_ref[0].astype(F32)
        for d in range(1, NDEV):
            acc = acc + x_ref[d].astype(F32)
        o_ref[...] = acc

    return pl.pallas_call(
        body, name=name, grid=(R // tr,),
        in_specs=[pl.BlockSpec((NDEV, tr, LANE), lambda i: (0, i, 0))],
        out_specs=pl.BlockSpec((tr, LANE), lambda i: (i, 0)),
        out_shape=jax.ShapeDtypeStruct((R, LANE), F32), compiler_params=_cparams(),
    )(x)


def adamw(name, w, g, m, v):
    L, R, C = w.shape
    tr = _pick_rows(R, 8, 256) if R % 8 == 0 else R

    def body(w_ref, g_ref, m_ref, v_ref, d_ref, nm_ref, nv_ref):
        gg = g_ref[...]
        nm = B1 * m_ref[...] + (1.0 - B1) * gg
        nv = B2 * v_ref[...] + (1.0 - B2) * jnp.square(gg)
        m_hat = nm / (1.0 - B1 ** STEP)
        v_hat = nv / (1.0 - B2 ** STEP)
        d_ref[...] = -LR * (m_hat / (jnp.sqrt(v_hat) + EPS) + WD * w_ref[...])
        nm_ref[...] = nm
        nv_ref[...] = nv

    blk = pl.BlockSpec((1, tr, C), lambda l, i: (l, i, 0))
    shp = jax.ShapeDtypeStruct(w.shape, F32)
    return pl.pallas_call(
        body, name=name, grid=(L, R // tr), in_specs=[blk] * 4, out_specs=[blk] * 3, out_shape=[shp] * 3,
        compiler_params=_cparams(),
    )(w, g, m, v)


IN_SHARD = D_IN // NDEV
UQ_SHARD = HEADS * (NOPE + ROPE) // NDEV
W_IN_PAD = 1024
ROW_A, ROW_B, ROW_C, ROW_UKV, ROW_UQ, MISC_ROWS = 0, 512, 1024, 1536, 1792, 2176


def _in_perm_index():
    ar = np.arange
    z = lambda n: np.full((n,), -1, np.int64)
    mix = lambda lo1, lo2: np.concatenate([ar(lo + LANE * j, lo + LANE * (j + 1)) for j in range(CW // LANE)
                                           for lo in (lo1, lo2)])
    return np.concatenate([ar(4768, 7840), mix(0, 512), ar(1024, 1536), mix(1536, 2560), ar(4256, 4768), ar(2048, 2560),
                           ar(3072, 3584), ar(3968, 4224), ar(4224, 4256), z(OFF_Q - OFF_KR - ROPE), ar(3584, 3968),
                           z(NP - OFF_Q - QL)])


def _head_perm_index(a, b):
    parts = []
    for g in range(QUADS):
        h = np.arange(4 * g, 4 * g + 4)[:, None] * (a + b)
        parts += [(h + np.arange(a)[None]).reshape(-1), (h + a + np.arange(b)[None]).reshape(-1)]
    return np.concatenate(parts)


def _inverse(perm, n):
    inv = np.full((n,), -1, np.int64)
    inv[perm[perm >= 0]] = np.nonzero(perm >= 0)[0]
    return inv


IN_PERM = _in_perm_index()
UQ_PERM = _head_perm_index(NOPE, ROPE)
UKV_PERM = _head_perm_index(NOPE, VH)


def _to_gathered(perm, shard, pad):
    return np.where(perm >= 0, (perm // shard) * pad + perm % shard, -1)


def _from_full(inv, shard, pad):
    j, i = np.divmod(np.arange(NDEV * pad), pad)
    return np.where(i < shard, inv[np.minimum(j * shard + i, inv.shape[0] - 1)], -1)


def col_gather(name, srcs, out_shapes, jobs, deps=()):
    ns, nj, nd, no = len(srcs), len(jobs), len(deps), len(out_shapes)
    tables = [jnp.asarray(np.asarray(job[5], np.int32)[None, :]) for job in jobs]

    def view(ref, col0, width, r0, rc):
        n = ref.shape[-1]
        if len(ref.shape) == 3:
            return ref.at[col0 // n, pl.ds(r0, rc), pl.ds(col0 % n, width)]
        return ref.at[pl.ds(r0, rc), pl.ds(col0, width)]

    def slabs(shape):
        if len(shape) == 3:
            return [((d,), d * shape[2], (d + 1) * shape[2]) for d in range(shape[0])]
        w = 1024 if shape[1] > 1024 and shape[1] % 1024 == 0 else shape[1]
        return [((slice(None), pl.ds(c, w)), c, c + w) for c in range(0, shape[1], w)]

    src_slabs = [slabs(s.shape) for s in srcs]
    out_slabs = [slabs(sh) for sh in out_shapes]
    work, first_use, last_touch = [], {}, {}
    for ji, (si, srow, oi, orow, nrows, tgt) in enumerate(jobs):
        tgt = np.asarray(tgt)
        tw = 256 if out_shapes[oi][-1] % 256 == 0 else LANE
        sw = 256 if srcs[si].shape[-1] % 256 == 0 else LANE
        for t in range(tgt.shape[0] // tw):
            tt = tgt[t * tw:(t + 1) * tw]
            tiles = sorted(set((tt[tt >= 0] // sw).tolist()))
            straight = bool(tiles) and tt[0] >= 0 and tt[0] % LANE == 0 and np.array_equal(tt, tt[0] + np.arange(tw))
            cols = [(int(tt[0]) + k * LANE, LANE) for k in range(tw // LANE)] if straight else [(s * sw, sw) for s in tiles]
            need = sorted({(si, k) for c0, _ in cols for k, (_, lo, hi) in enumerate(src_slabs[si]) if lo <= c0 < hi})
            touch = [(oi, k) for k, (_, lo, hi) in enumerate(out_slabs[oi]) if lo <= t * tw < hi][0]
            for key in need:
                first_use.setdefault(key, len(work))
            last_touch[touch] = len(work)
            work.append((ji, t, tw, sw, tiles, straight, need, touch))
    in_order = sorted(first_use, key=first_use.get)
    in_sem = {key: i for i, key in enumerate(in_order)}
    out_keys = sorted(last_touch)
    out_sem = {key: i for i, key in enumerate(out_keys)}

    def body(*refs):
        src_hbm, tab_refs = refs[:ns], refs[ns:ns + nj]
        out_hbm = refs[ns + nj + nd:ns + nj + nd + no]
        scratch = refs[ns + nj + nd + no:]
        src_refs, out_refs, in_sems, out_sems = scratch[:ns], scratch[ns:ns + no], scratch[-2], scratch[-1]
        loads = {}
        for key in in_order:
            si, k = key
            idx = src_slabs[si][k][0]
            loads[key] = pltpu.make_async_copy(src_hbm[si].at[idx], src_refs[si].at[idx], in_sems.at[in_sem[key]])
            loads[key].start()
        arrived, stores = set(), []
        for wi, (ji, t, tw, sw, tiles, straight, need, touch) in enumerate(work):
            si, srow, oi, orow, nrows, tgt = jobs[ji]
            sref, oref = src_refs[si], out_refs[oi]
            rc = nrows if nrows <= 1024 else 1024
            for key in need:
                if key not in arrived:
                    loads[key].wait()
                    arrived.add(key)
            onehots = []
            if tiles and not straight:
                want = tab_refs[ji][:, t * tw:(t + 1) * tw]
                row = lax.broadcasted_iota(jnp.int32, (sw, tw), 0)
                onehots = [jnp.where(want == row + s * sw, 1.0, 0.0).astype(BF) for s in tiles]
            first = int(np.asarray(tgt)[t * tw])

            def chunk(ci, _, t=t, tw=tw, sw=sw, tiles=tiles, straight=straight, onehots=onehots, first=first,
                      sref=sref, oref=oref, srow=srow, orow=orow, rc=rc):
                r0 = ci * rc
                ro = pl.multiple_of(orow + r0, LANE)
                rs = pl.multiple_of(srow + r0, LANE)
                if not tiles:
                    view(oref, t * tw, tw, ro, rc)[...] = jnp.zeros((rc, tw), BF)
                elif straight:
                    for k in range(tw // LANE):
                        view(oref, t * tw + k * LANE, LANE, ro, rc)[...] = view(sref, first + k * LANE, LANE, rs, rc)[...]
                else:
                    acc = None
                    for s, oh in zip(tiles, onehots):
                        p = jnp.dot(view(sref, s * sw, sw, rs, rc)[...], oh, preferred_element_type=F32)
                        acc = p if acc is None else acc + p
                    view(oref, t * tw, tw, ro, rc)[...] = acc.astype(BF)
                return 0

            lax.fori_loop(0, nrows // rc, chunk, 0)
            if last_touch[touch] == wi:
                idx = out_slabs[touch[0]][touch[1]][0]
                cp = pltpu.make_async_copy(out_refs[touch[0]].at[idx], out_hbm[touch[0]].at[idx], out_sems.at[out_sem[touch]])
                cp.start()
                stores.append(cp)
        for cp in stores:
            cp.wait()

    return pl.pallas_call(
        body, name=name, in_specs=[ANY_SPEC] * ns + [pl.BlockSpec(memory_space=pltpu.VMEM)] * nj + [ANY_SPEC] * nd,
        out_specs=[ANY_SPEC] * no, out_shape=[jax.ShapeDtypeStruct(s, BF) for s in out_shapes],
        scratch_shapes=[pltpu.VMEM(s.shape, BF) for s in srcs] + [pltpu.VMEM(s, BF) for s in out_shapes]
        + [pltpu.SemaphoreType.DMA((len(in_order),)), pltpu.SemaphoreType.DMA((len(out_keys),))],
        compiler_params=_cparams(),
    )(*srcs, *tables, *deps)


def sum_adamw(name, recvs, w, m, v):
    L, R, C = w.shape
    CP = recvs[0].shape[-1]
    tr = _pick_rows(R, 16, 128)

    def body(*refs):
        r_refs = refs[:L]
        w_ref, m_ref, v_ref, g_ref, d_ref, nm_ref, nv_ref, gsum = refs[L:]
        layer = pl.program_id(0)
        for k in range(L):
            def total(k=k):
                acc = r_refs[k][0].astype(F32)
                for d in range(1, NDEV):
                    acc = acc + r_refs[k][d].astype(F32)
                gsum[...] = acc
            pl.when(layer == k)(total)
        gg = gsum[:, 0:C]
        nm = B1 * m_ref[...] + (1.0 - B1) * gg
        nv = B2 * v_ref[...] + (1.0 - B2) * jnp.square(gg)
        m_hat = nm / (1.0 - B1 ** STEP)
        v_hat = nv / (1.0 - B2 ** STEP)
        g_ref[...] = gg
        d_ref[...] = -LR * (m_hat / (jnp.sqrt(v_hat) + EPS) + WD * w_ref[...])
        nm_ref[...] = nm
        nv_ref[...] = nv

    r_specs = [pl.BlockSpec((NDEV, tr, CP), functools.partial(lambda l, i, k: (0, jnp.where(l == k, i, 0), 0), k=k))
               for k in range(L)]
    blk = pl.BlockSpec((None, tr, C), lambda l, i: (l, i, 0))
    shp = jax.ShapeDtypeStruct(w.shape, F32)
    return pl.pallas_call(
        body, name=name, grid=(L, R // tr), in_specs=r_specs + [blk] * 3, out_specs=[blk] * 4, out_shape=[shp] * 4,
        scratch_shapes=[pltpu.VMEM((tr, CP), F32)], compiler_params=_cparams(),
    )(*recvs, w, m, v)


ALPHA = 8.0 ** 0.25


def _rope_fn(sign):
    def fn(x, cos, sin):
        W = x.shape[-1]
        lane = lax.broadcasted_iota(jnp.int32, x.shape, 1)
        first_half = (lane % ROPE) < (ROPE // 2)
        rot = jnp.where(first_half, -pltpu.roll(x, W - ROPE // 2, 1), pltpu.roll(x, ROPE // 2, 1))
        return x * cos + sign * rot * sin
    return fn


def layer_fwd(x, ada3, W, tabs, S):
    cos, sin = tabs
    T = 256
    u = rowwise("modulate", lambda xv, a: xv * (1.0 + a[1:2, :]) + a[0:1, :], S, T,
                [(x, D_MODEL, 0)], [ada3], [(D_MODEL, BF)])[0]
    proj = mm(u, W["in"], name="mm_proj", tm=1024, tn=1024, out_dtype=BF)
    W = {**W, **W["late"](proj)}

    ca = conv_fwd("conv_a_fwd", proj, OFF_A, W["conv_a"], 31, "glu", S, CW)

    def a_post(c, ag, vec):
        n, _ = _ln_stats(c + vec[0:1, :])
        return _silu(n * vec[1:2, :] + vec[2:3, :]) * _silu(ag)

    h_a = rowwise("mix_a_post", a_post, S, T, [(ca, CW, 0), (proj, CW, OFF_AG)], [W["vec_a"]], [(CW, BF)])[0]
    y_a = mm(h_a, W["a_out"], name="mm_branch_out", out_dtype=BF)

    cb = conv_fwd("conv_b_fwd", proj, OFF_B, W["conv_b"], 3, "mul", S, CW)
    h_b = rowwise("mix_b_post", lambda c, gb, bg: gb * c * _silu(bg), S, T,
                  [(cb, CW, 0), (proj, CW, OFF_GB), (proj, CW, OFF_BG)], [], [(CW, BF)])[0]
    y_b = mm(h_b, W["b_out"], name="mm_branch_out", out_dtype=BF)

    def rms2(ql, kvl, gq, gkv):
        rq = lax.rsqrt(jnp.mean(ql * ql, axis=-1, keepdims=True) + RMS_EPS)
        rk = lax.rsqrt(jnp.mean(kvl * kvl, axis=-1, keepdims=True) + RMS_EPS)
        return ql * rq * gq, kvl * rk * gkv

    qn, kvn = rowwise("rms_fwd", rms2, S, T, [(proj, QL, OFF_Q), (proj, KVL, OFF_KV)], [W["gq"], W["gkv"]],
                      [(QL, BF), (KVL, BF)])
    q = mm(qn, W["uq"], name="mm_q")
    kv = mm(kvn, W["ukv"], name="mm_kv", out_dtype=BF)
    rope = _rope_fn(1.0)

    def rope_fwd(qv, kr, c1, s1):
        parts = []
        for g in range(QUADS):
            parts.append(qv[:, g * QW:g * QW + 2 * LANE].astype(BF))
            parts.append(rope(qv[:, g * QW + 2 * LANE:(g + 1) * QW], c1, s1).astype(BF))
        kp = rope(kr, c1, s1)
        kp = kp + pltpu.roll(kp, ROPE, 1) + pltpu.roll(kp, 2 * ROPE, 1) + pltpu.roll(kp, 3 * ROPE, 1)
        return jnp.concatenate(parts, axis=1), kp

    q_b, kpe = rowwise("rope_fwd", rope_fwd, S, T,
                       [(q, HEADS * (NOPE + ROPE), 0), (proj, LANE, OFF_KR), (cos, LANE, 0), (sin, LANE, 0)], [],
                       [(HEADS * (NOPE + ROPE), BF), (LANE, BF)])
    o, lse = attn_fwd(q_b, kv, kpe, S)
    h_c = rowwise("mix_c_post", lambda ov, cg: ov * _silu(cg), S, T, [(o, CW, 0), (proj, CW, OFF_CG)], [],
                  [(CW, BF)])[0]
    y_c = mm(h_c, W["c_out"], name="mm_branch_out", out_dtype=BF)

    def merge(la, lb, lc, ya, yb, yc):
        return _sigmoid(la) * ya + _sigmoid(lb) * yb + _sigmoid(lc) * yc

    m = rowwise("merge_fwd", merge, S, 128,
                [(proj, D_MODEL, 0), (proj, D_MODEL, 1024), (proj, D_MODEL, 2048), (y_a, D_MODEL, 0),
                 (y_b, D_MODEL, 0), (y_c, D_MODEL, 0)], [], [(D_MODEL, BF)])[0]
    out = mm(m, W["o"], name="mm_out")

    def ln_fwd(xv, ov, a, lnv):
        n, _ = _ln_stats(ALPHA * xv + a[2:3, :] * ov)
        return n * lnv[0:1, :] + lnv[1:2, :]

    x_next = rowwise("ln_fwd", ln_fwd, S, 128, [(x, D_MODEL, 0), (out, D_MODEL, 0)], [ada3, W["lnv"]],
                     [(D_MODEL, F32)])[0]
    saved = dict(x=x, u=u, proj=proj, ca=ca, cb=cb, h_a=h_a, h_b=h_b, h_c=h_c, y_a=y_a, y_b=y_b, y_c=y_c, qn=qn,
                 kvn=kvn, q_b=q_b, kv=kv, kpe=kpe, lse=lse, o=o, m=m, out=out)
    return x_next, saved, W


def layer_bwd(dxn, sv, ada3, W, tabs, S, before_in=None):
    cos, sin = tabs
    T = 256
    x, proj = sv["x"], sv["proj"]
    G = {}

    def ln_bwd(xv, ov, dy, a, lnv):
        gate = a[2:3, :]
        n, rstd = _ln_stats(ALPHA * xv + gate * ov)
        dr = _ln_bwd(dy * lnv[0:1, :], n, rstd)
        return ALPHA * dr, gate * dr, _colsum(dy * n), _colsum(dy), _colsum(dr * ov)

    dres, d_out, G["ln_g"], G["ln_b"], d_gate = rowwise(
        "ln_bwd", ln_bwd, S, 128, [(x, D_MODEL, 0), (sv["out"], D_MODEL, 0), (dxn, D_MODEL, 0)], [ada3, W["lnv"]],
        [(D_MODEL, F32), (D_MODEL, BF)], [D_MODEL] * 3)
    dm = mm(d_out, W["o"], name="mm_dm", trans_b=True, out_dtype=BF)
    G["w_o"] = mm(sv["m"], d_out, name="mm_gw_o", trans_a=True, out_dtype=BF)

    def merge_bwd(dmv, la, lb, lc, ya, yb, yc):
        outs, dls = [], []
        for lg, yv in ((la, ya), (lb, yb), (lc, yc)):
            s = _sigmoid(lg)
            outs.append(dmv * s)
            dls.append((dmv * yv * s * (1.0 - s)).astype(BF))
        return (jnp.concatenate(dls, axis=1),) + tuple(outs)

    d_proj, dy_a, dy_b, dy_c = rowwise(
        "merge_bwd", merge_bwd, S, 128,
        [(dm, D_MODEL, 0), (proj, D_MODEL, 0), (proj, D_MODEL, 1024), (proj, D_MODEL, 2048), (sv["y_a"], D_MODEL, 0),
         (sv["y_b"], D_MODEL, 0), (sv["y_c"], D_MODEL, 0)], [], [(3 * D_MODEL, BF)] + [(D_MODEL, BF)] * 3,
        into=(None, NP, OFF_M))

    dh = {}
    for br, dy in (("a", dy_a), ("b", dy_b), ("c", dy_c)):
        dh[br] = mm(dy, W[br + "_out"], name="mm_dh", trans_b=True, out_dtype=BF)
        G["w_%s_out" % br] = mm(sv["h_" + br], dy, name="mm_gw_branch", trans_a=True, out_dtype=BF)

    def a_post_bwd(c, ag, dhv, vec):
        n, rstd = _ln_stats(c + vec[0:1, :])
        z = n * vec[1:2, :] + vec[2:3, :]
        d_ag = dhv * _silu(z) * _dsilu(ag)
        dz = dhv * _silu(ag) * _dsilu(z)
        dc = _ln_bwd(dz * vec[1:2, :], n, rstd)
        return d_ag, dc, _colsum(dc), _colsum(dz * n), _colsum(dz)

    d_proj, dca, G["conv_a_b"], G["ln_a_g"], G["ln_a_b"] = rowwise(
        "mix_a_post_bwd", a_post_bwd, S, T, [(sv["ca"], CW, 0), (proj, CW, OFF_AG), (dh["a"], CW, 0)], [W["vec_a"]],
        [(CW, BF), (CW, F32)], [CW] * 3, into=(d_proj, NP, OFF_AG))
    d_proj, G["conv_a_w"] = conv_bwd("conv_a_bwd", proj, OFF_A, dca, W["conv_a"], 31, "glu", S, CW, d_proj)

    def b_post_bwd(c, gb, bg, dhv):
        sg = _silu(bg)
        d_gb_bg = jnp.concatenate([(dhv * sg * c).astype(BF), (dhv * gb * c * _dsilu(bg)).astype(BF)], axis=1)
        return d_gb_bg, dhv * sg * gb

    d_proj, dcb = rowwise("mix_b_post_bwd", b_post_bwd, S, T,
                          [(sv["cb"], CW, 0), (proj, CW, OFF_GB), (proj, CW, OFF_BG), (dh["b"], CW, 0)], [],
                          [(2 * CW, BF), (CW, F32)], into=(d_proj, NP, OFF_GB))
    d_proj, G["conv_b_w"] = conv_bwd("conv_b_bwd", proj, OFF_B, dcb, W["conv_b"], 3, "mul", S, CW, d_proj)

    d_proj, d_o = rowwise("mix_c_post_bwd", lambda ov, cg, dhv: (dhv * ov * _dsilu(cg), dhv * _silu(cg)), S, T,
                          [(sv["o"], CW, 0), (proj, CW, OFF_CG), (dh["c"], CW, 0)], [], [(CW, BF), (CW, F32)],
                          into=(d_proj, NP, OFF_CG))
    dq, d_kv, dkp_heads = attn_bwd(sv["q_b"], sv["kv"], sv["kpe"], sv["o"], sv["lse"], d_o, S)
    ropeT = _rope_fn(-1.0)

    def rope_bwd(dqv, dkp, c1, s1):
        parts = []
        for g in range(QUADS):
            parts.append(dqv[:, g * QW:g * QW + 2 * LANE].astype(BF))
            parts.append(ropeT(dqv[:, g * QW + 2 * LANE:(g + 1) * QW], c1, s1).astype(BF))
        f = dkp[:, :LANE] + dkp[:, LANE:]
        f = f + pltpu.roll(f, 64, 1)
        f = f + pltpu.roll(f, 32, 1)
        lane = lax.broadcasted_iota(jnp.int32, f.shape, 1)
        return jnp.concatenate(parts, axis=1), jnp.where(lane < ROPE, ropeT(f, c1, s1), 0.0)

    d_q, dk_pe = rowwise("rope_bwd", rope_bwd, S, T,
                         [(dq, HEADS * (NOPE + ROPE), 0), (dkp_heads, HEADS * ROPE, 0), (cos, LANE, 0), (sin, LANE, 0)],
                         [], [(HEADS * (NOPE + ROPE), BF), (LANE, BF)])
    d_qn = mm(d_q, W["uq"], name="mm_dqn", trans_b=True, out_dtype=BF)
    d_kvn = mm(d_kv, W["ukv"], name="mm_dkvn", trans_b=True, out_dtype=BF)
    G["w_uq"] = mm(sv["qn"], d_q, name="mm_gw_uq", trans_a=True, out_dtype=BF)
    G["w_ukv"] = mm(sv["kvn"], d_kv, name="mm_gw_ukv", trans_a=True, out_dtype=BF)

    def rms_bwd(ql, kvl, dqn, dkn, dkp, gq, gkv):
        res = []
        for xv, dy, g in ((ql, dqn, gq), (kvl, dkn, gkv)):
            r = lax.rsqrt(jnp.mean(xv * xv, axis=-1, keepdims=True) + RMS_EPS)
            dxh = dy * g
            res.append(((r * (dxh - xv * (r * r) * jnp.mean(dxh * xv, axis=-1, keepdims=True))).astype(BF),
                        _colsum(dy * xv * r)))
        pad = jnp.zeros((ql.shape[0], LANE), BF)
        return jnp.concatenate([res[1][0], dkp, pad, res[0][0], pad], axis=1), res[0][1], res[1][1]

    d_proj, G["q_norm_g"], G["kv_norm_g"] = rowwise(
        "rms_bwd", rms_bwd, S, T,
        [(proj, QL, OFF_Q), (proj, KVL, OFF_KV), (d_qn, QL, 0), (d_kvn, KVL, 0), (dk_pe, LANE, 0)],
        [W["gq"], W["gkv"]], [(NP - OFF_KV, BF)], [QL, KVL], into=(d_proj, NP, OFF_KV))
    deps = before_in(G) if before_in is not None else ()
    du = mm(d_proj, W["in"], name="mm_du", trans_b=True, tk=1024, deps=deps)
    G["w_in"] = mm(sv["u"], d_proj, name="mm_gw_in", trans_a=True, out_dtype=BF, deps=deps)

    def mod_bwd(duv, xv, dr, a):
        return duv * (1.0 + a[1:2, :]) + dr, _colsum(duv), _colsum(duv * xv)

    dx, d_shift, d_scale = rowwise("mod_bwd", mod_bwd, S, 128, [(du, D_MODEL, 0), (x, D_MODEL, 0), (dres, D_MODEL, 0)],
                                   [ada3], [(D_MODEL, F32)], [D_MODEL] * 2)
    d_ada = jnp.concatenate([d_shift, d_scale, d_gate], axis=1)
    return dx, G, d_ada


SMALL = ("conv_a_b", "ln_a_g", "ln_a_b", "q_norm_g", "kv_norm_g", "ln_g", "ln_b")


def _rows(v):
    n = v.shape[0]
    r = -(-n // (LANE * 16)) * 16
    return jnp.pad(v, (0, r * LANE - n)).reshape(r, LANE)


def kernel(x, c, positions, w_ada, b_ada, w_in, conv_a_w, conv_a_b, ln_a_g, ln_a_b, w_a_out, conv_b_w, w_b_out, q_norm_g, kv_norm_g, w_uq, w_ukv, w_c_out, w_o, ln_g, ln_b, loss_target, m_w_ada, m_b_ada, m_w_in, m_conv_a_w, m_conv_a_b, m_ln_a_g, m_ln_a_b, m_w_a_out, m_conv_b_w, m_w_b_out, m_q_norm_g, m_kv_norm_g, m_w_uq, m_w_ukv, m_w_c_out, m_w_o, m_ln_g, m_ln_b, v_w_ada, v_b_ada, v_w_in, v_conv_a_w, v_conv_a_b, v_ln_a_g, v_ln_a_b, v_w_a_out, v_conv_b_w, v_w_b_out, v_q_norm_g, v_kv_norm_g, v_w_uq, v_w_ukv, v_w_c_out, v_w_o, v_ln_g, v_ln_b):
    P = dict(w_ada=w_ada, b_ada=b_ada, w_in=w_in, conv_a_w=conv_a_w, conv_a_b=conv_a_b, ln_a_g=ln_a_g, ln_a_b=ln_a_b,
             w_a_out=w_a_out, conv_b_w=conv_b_w, w_b_out=w_b_out, q_norm_g=q_norm_g, kv_norm_g=kv_norm_g, w_uq=w_uq,
             w_ukv=w_ukv, w_c_out=w_c_out, w_o=w_o, ln_g=ln_g, ln_b=ln_b)
    Mo = dict(w_ada=m_w_ada, b_ada=m_b_ada, w_in=m_w_in, conv_a_w=m_conv_a_w, conv_a_b=m_conv_a_b, ln_a_g=m_ln_a_g,
              ln_a_b=m_ln_a_b, w_a_out=m_w_a_out, conv_b_w=m_conv_b_w, w_b_out=m_w_b_out, q_norm_g=m_q_norm_g,
              kv_norm_g=m_kv_norm_g, w_uq=m_w_uq, w_ukv=m_w_ukv, w_c_out=m_w_c_out, w_o=m_w_o, ln_g=m_ln_g, ln_b=m_ln_b)
    Vo = dict(w_ada=v_w_ada, b_ada=v_b_ada, w_in=v_w_in, conv_a_w=v_conv_a_w, conv_a_b=v_conv_a_b, ln_a_g=v_ln_a_g,
              ln_a_b=v_ln_a_b, w_a_out=v_w_a_out, conv_b_w=v_conv_b_w, w_b_out=v_w_b_out, q_norm_g=v_q_norm_g,
              kv_norm_g=v_kv_norm_g, w_uq=v_w_uq, w_ukv=v_w_ukv, w_c_out=v_w_c_out, w_o=v_w_o, ln_g=v_ln_g, ln_b=v_ln_b)
    ORDER = ("w_ada", "b_ada", "w_in", "conv_a_w", "conv_a_b", "ln_a_g", "ln_a_b", "w_a_out", "conv_b_w", "w_b_out",
             "q_norm_g", "kv_norm_g", "w_uq", "w_ukv", "w_c_out", "w_o", "ln_g", "ln_b")
    L = w_ada.shape[0]
    S = x.shape[1]
    me = 4 * lax.axis_index("x") + 2 * lax.axis_index("y") + lax.axis_index("c")
    x2 = x[0]
    tgt = loss_target[0]

    small_in = _rows(jnp.concatenate([c.reshape(-1), conv_a_w.reshape(-1), conv_b_w.reshape(-1)]))
    w_in_b = jnp.pad(w_in.astype(BF), ((0, 0), (0, 0), (0, W_IN_PAD - IN_SHARD)))
    misc_b = jnp.concatenate([w_a_out, w_b_out, w_c_out, w_ukv, jnp.pad(w_uq, ((0, 0), (0, 0), (0, LANE - UQ_SHARD)))],
                             axis=1).astype(BF)
    w_o_b = w_o.astype(BF)
    gathered = [None] * L
    gathered[0] = [gather_two_level("gather0_w_in", w_in_b[0])]
    pending_rest, rest_token = exchange_begin("gather0_rest", [misc_b[0], w_o_b[0]], 2, gathered[0][0])
    sg = exchange("gather_small", [small_in], [])[0]
    sgf = sg.reshape(NDEV, -1)
    c_all = sgf[:, :D_MODEL]
    o1 = D_MODEL + L * 31 * 64
    conv_a_full = sgf[:, D_MODEL:o1].reshape(NDEV, L, 31, 64).transpose(1, 2, 0, 3).reshape(L, 31, CW)
    conv_b_full = sgf[:, o1:o1 + L * 3 * 64].reshape(NDEV, L, 3, 64).transpose(1, 2, 0, 3).reshape(L, 3, CW)

    c_act = rowwise("silu_c", _silu, 16, 16, [(jnp.pad(c_all, ((0, 8), (0, 0))), D_MODEL, 0)], [], [(D_MODEL, BF)])[0]
    ncol = w_ada.shape[2]
    w_ada_b = w_ada.astype(BF).transpose(1, 0, 2).reshape(D_MODEL, L * ncol)
    b_mine = lax.dynamic_slice_in_dim(b_ada, me * ncol, ncol, axis=1).reshape(1, L * ncol)
    ada_part = mm(c_act, w_ada_b, name="mm_ada", bias=b_mine)
    ada_rows = -(-(L * ncol) // (LANE * 8)) * 8
    ada_send = jnp.pad(ada_part[:NDEV].reshape(NDEV, -1, LANE), ((0, 0), (0, ada_rows - L * ncol // LANE), (0, 0)))
    ada_recv = exchange("a2a_ada", [], [ada_send])[0]
    ada = ada_recv[:, :L * ncol // LANE].reshape(NDEV, L, ncol).transpose(1, 0, 2).reshape(L, 3, D_MODEL)

    inv_freq = ROPE_THETA ** (-jnp.arange(0, ROPE, 2, dtype=F32) / ROPE)
    ang = positions[0].astype(F32)[:, None] * inv_freq
    tabs = (jnp.tile(jnp.cos(ang), (1, 2 * LANE // ROPE)), jnp.tile(jnp.sin(ang), (1, 2 * LANE // ROPE)))

    straight = np.arange(D_MODEL)
    fwd_in = [(0, 0, 0, 0, D_MODEL, _to_gathered(IN_PERM, IN_SHARD, W_IN_PAD))]
    fwd_misc = [(0, ROW_A, 0, 0, CW, straight), (0, ROW_B, 1, 0, CW, straight), (0, ROW_C, 2, 0, CW, straight),
                (0, ROW_UKV, 3, 0, KVL, UKV_PERM), (0, ROW_UQ, 4, 0, QL, _to_gathered(UQ_PERM, UQ_SHARD, LANE))]
    rev_in = [(0, 0, 0, 0, D_MODEL, _from_full(_inverse(IN_PERM, D_IN), IN_SHARD, W_IN_PAD))]
    rev_misc = [(0, 0, 0, ROW_A, CW, straight), (1, 0, 0, ROW_B, CW, straight), (2, 0, 0, ROW_C, CW, straight),
                (3, 0, 0, ROW_UKV, KVL, _from_full(_inverse(UKV_PERM, HEADS * (NOPE + VH)), LANE, LANE)),
                (4, 0, 0, ROW_UQ, QL, _from_full(_inverse(UQ_PERM, HEADS * (NOPE + ROPE)), UQ_SHARD, LANE))]

    def layer_weights(l, deps):
        w_in_p = col_gather("relayout_w_in", [gathered[l][0]], [(D_MODEL, NP)], fwd_in, deps)[0]

        def late(after):
            if len(gathered[l]) == 1:
                gathered[l] += exchange_end(pending_rest, after)
            _, g_misc, g_o = gathered[l]
            a_out, b_out, c_out, ukv, uq = col_gather(
                "relayout_misc", [g_misc],
                [(CW, D_MODEL)] * 3 + [(KVL, HEADS * (NOPE + VH)), (QL, HEADS * (NOPE + ROPE))], fwd_misc, deps)
            return {"a_out": a_out, "b_out": b_out, "c_out": c_out, "uq": uq, "ukv": ukv,
                    "o": g_o.reshape(D_MODEL, D_MODEL)}

        return {
            "in": w_in_p, "late": late,
            "conv_a": jnp.pad(conv_a_full[l], ((0, 1), (0, 0))), "conv_b": jnp.pad(conv_b_full[l], ((0, 5), (0, 0))),
            "vec_a": jnp.stack([conv_a_b[l], ln_a_g[l], ln_a_b[l]]), "gq": q_norm_g[l][None], "gkv": kv_norm_g[l][None],
            "lnv": jnp.stack([ln_g[l], ln_b[l]]),
        }

    h = x2
    saved, weights = [], []
    for l in range(L):
        ada_l, deps = ada[l], ()
        if l + 1 < L:
            pending, token = exchange_begin("gather%d" % (l + 1), [w_in_b[l + 1], misc_b[l + 1], w_o_b[l + 1]], 3,
                                            rest_token if l == 0 else gathered[l][0])
            ada_l, deps = ada_l + token[0, 0], (token,)
        h, sv, Wl = layer_fwd(h, ada_l, layer_weights(l, deps), tabs, S)
        if l + 1 < L:
            gathered[l + 1] = exchange_end(pending, h)
        saved.append(sv)
        weights.append(Wl)

    def loss_fn(y, t):
        e = y - t
        return e * (1.0 / D_MODEL), _colsum(e * e)

    dy, sq = rowwise("loss", loss_fn, S, 256, [(h, D_MODEL, 0), (tgt, D_MODEL, 0)], [], [(D_MODEL, F32)], [D_MODEL])
    loss = lax.psum(0.5 * jnp.sum(sq) / D_MODEL, ("x", "y", "c"))

    grads, d_adas, recv = [None] * L, [None] * L, [None] * L
    pending, token = None, None

    def send_rest(g):
        send_misc = col_gather("unrelayout_misc", [g["w_a_out"], g["w_b_out"], g["w_c_out"], g["w_ukv"], g["w_uq"]],
                               [(NDEV, MISC_ROWS, LANE)], rev_misc)[0]
        return [send_misc, g["w_o"].reshape(NDEV, D_MODEL // NDEV, D_MODEL)]

    rest0 = []

    def early_rest(g):
        handle, tok = exchange_begin("scatter0_rest", send_rest(g), 0, g["w_o"])
        rest0.append(handle)
        return (tok,)

    for l in reversed(range(L)):
        ada_l = ada[l] if token is None else ada[l] + token[0, 0]
        dy, g, d_adas[l] = layer_bwd(dy, saved[l], ada_l, weights[l], tabs, S, early_rest if l == 0 else None)
        grads[l] = g
        if pending is not None:
            recv[l + 1] = exchange_end(pending, dy)
        send_in = col_gather("unrelayout_w_in", [g["w_in"]], [(NDEV, D_MODEL, W_IN_PAD)], rev_in)[0]
        if l == 0:
            pending, token = exchange_begin("scatter0", [send_in], 0, recv[1][0])
        else:
            pending, token = exchange_begin("scatter%d" % l, [send_in] + send_rest(g), 0,
                                            dy if l + 1 == L else recv[l + 1][0])
    grad_x = dy[None]

    small_parts = [jnp.stack([grads[l][n].reshape(-1) for l in range(L)]).reshape(-1) for n in SMALL]
    small_parts.append(jnp.stack([grads[l]["conv_a_w"][:31].reshape(-1) for l in range(L)]).reshape(-1))
    small_parts.append(jnp.stack([grads[l]["conv_b_w"][:3].reshape(-1) for l in range(L)]).reshape(-1))
    small_parts.append(jnp.stack([d_adas[l].reshape(-1) for l in range(L)]).reshape(-1))
    small_sizes = [int(p.shape[0]) for p in small_parts]
    gsmall = exchange("gather_small_grads", [_rows(jnp.concatenate(small_parts)) + token[0, 0]], [])[0]
    gsum = sum_slots("sum_small", gsmall).reshape(-1)
    recv[0] = [None] + exchange_end(rest0[0], gsum)
    Gr = {}
    offs = np.cumsum([0] + small_sizes)
    for i, n in enumerate(SMALL):
        Gr[n] = gsum[offs[i]:offs[i + 1]].reshape(L, -1)
    ca = gsum[offs[7]:offs[8]].reshape(L, 31, CW)
    cbw = gsum[offs[8]:offs[9]].reshape(L, 3, CW)
    Gr["conv_a_w"] = lax.dynamic_slice_in_dim(ca, me * 64, 64, axis=2)
    Gr["conv_b_w"] = lax.dynamic_slice_in_dim(cbw, me * 64, 64, axis=2)
    Gr["b_ada"] = gsum[offs[9]:offs[10]].reshape(L, 3 * D_MODEL)
    d_ada_all = gsmall.reshape(NDEV, -1)[:, offs[9]:offs[10]].reshape(NDEV, L, 3 * D_MODEL)
    d_mine = lax.dynamic_slice_in_dim(d_ada_all, me * ncol, ncol, axis=2).reshape(NDEV, L * ncol)
    g_ada = mm(c_act, jnp.pad(d_mine, ((0, 8), (0, 0))).astype(BF), name="mm_gw_ada", trans_a=True)
    Gr["w_ada"] = g_ada.reshape(D_MODEL, L, ncol).transpose(1, 0, 2)

    D, NM, NV = {}, {}, {}
    D["w_ada"], NM["w_ada"], NV["w_ada"] = adamw("adamw_w_ada", P["w_ada"], Gr["w_ada"], Mo["w_ada"], Vo["w_ada"])
    Gr["w_o"], D["w_o"], NM["w_o"], NV["w_o"] = sum_adamw(
        "sum_adamw_w_o", [recv[l][2] for l in range(L)], P["w_o"], Mo["w_o"], Vo["w_o"])
    misc = lambda T_: jnp.concatenate([T_["w_a_out"], T_["w_b_out"], T_["w_c_out"], T_["w_ukv"],
                                       jnp.pad(T_["w_uq"], ((0, 0), (0, 0), (0, LANE - UQ_SHARD)))], axis=1)
    res = sum_adamw("sum_adamw_misc", [recv[l][1] for l in range(L)], misc(P), misc(Mo), misc(Vo))
    for T_, r in zip((Gr, D, NM, NV), res):
        T_["w_a_out"], T_["w_b_out"], T_["w_c_out"] = r[:, ROW_A:ROW_B], r[:, ROW_B:ROW_C], r[:, ROW_C:ROW_UKV]
        T_["w_ukv"], T_["w_uq"] = r[:, ROW_UKV:ROW_UQ], r[:, ROW_UQ:MISC_ROWS, :UQ_SHARD]
    recv[0][0] = exchange_end(pending, res[1])[0]
    Gr["w_in"], D["w_in"], NM["w_in"], NV["w_in"] = sum_adamw(
        "sum_adamw_w_in", [recv[l][0] for l in range(L)], P["w_in"] + token[0, 0], Mo["w_in"] + token[0, 0],
        Vo["w_in"] + token[0, 0])
    packed =("b_ada", "conv_a_w", "conv_b_w") + SMALL
    pk = lambda T_: _rows(jnp.concatenate([T_[n].reshape(-1) for n in packed]))[None]
    dS, mS, vS = adamw("adamw_small", pk(P), pk(Gr), pk(Mo), pk(Vo))
    o = 0
    for n in packed:
        sz = int(np.prod(P[n].shape))
        D[n] = dS.reshape(-1)[o:o + sz].reshape(P[n].shape)
        NM[n] = mS.reshape(-1)[o:o + sz].reshape(P[n].shape)
        NV[n] = vS.reshape(-1)[o:o + sz].reshape(P[n].shape)
        o += sz
    return (loss, grad_x, *[Gr[n] for n in ORDER], *[D[n] for n in ORDER], *[NM[n] for n in ORDER],
            *[NV[n] for n in ORDER])
```

```python
import functools
import math

import numpy as np
import jax
import jax.numpy as jnp
from jax import lax
from jax.experimental import pallas as pl
from jax.experimental.pallas import tpu as pltpu

BF = jnp.bfloat16
F32 = jnp.float32
MESH = pl.DeviceIdType.MESH
NDEV = 8

HEADS, NOPE, ROPE, VH = 8, 64, 32, 64
HP = 128
ROPE_THETA = 10000.0
LN_EPS = 1e-5
RMS_EPS = 1e-6
LR, B1, B2, EPS, WD, STEP = 0.001, 0.9, 0.999, 1e-08, 0.01, 10

LANE = 128
VMEM_LIMIT = 56 * 1024 * 1024

D_MODEL, CW, QL, KVL = 1024, 512, 384, 256
OFF_M, OFF_A, OFF_AG, OFF_B, OFF_CG, OFF_GB, OFF_BG = 0, 3072, 4096, 4608, 5632, 6144, 6656
OFF_KV, OFF_KR, OFF_Q, NP = 7168, 7424, 7680, 8192
D_IN = 7840


def _cparams(**kw):
    return pltpu.CompilerParams(vmem_limit_bytes=VMEM_LIMIT, **kw)


def _sigmoid(x):
    return jax.nn.sigmoid(x)


def _silu(x):
    return x * _sigmoid(x)


def _dsilu(x):
    s = _sigmoid(x)
    return s * (1.0 + x * (1.0 - s))


def _pick_tile(n, cap, mult):
    if n <= cap:
        return n
    for t in range(cap - cap % mult, 0, -mult):
        if n % t == 0:
            return t
    raise ValueError((n, cap, mult))


def mm(a, b, *, name, trans_a=False, trans_b=False, out_dtype=F32, bias=None, tm=1024, tn=1024, tk=2048, deps=()):
    if trans_a:
        K, M = a.shape
    else:
        M, K = a.shape
    if trans_b:
        N, K2 = b.shape
    else:
        K2, N = b.shape
    assert K == K2 and not (trans_a and trans_b), (a.shape, b.shape)
    tm, tn = _pick_tile(M, tm, 16), _pick_tile(N, tn, LANE)
    tk = _pick_tile(K, tk, LANE if trans_b else 16)
    assert M % tm == 0 and N % tn == 0 and K % tk == 0, (M, N, K, tm, tn, tk)
    nk = K // tk
    dims = (((0 if trans_a else 1,), (1 if trans_b else 0,)), ((), ()))
    has_bias = bias is not None

    def body(*refs):
        a_ref, b_ref = refs[0], refs[1]
        bias_ref = refs[2] if has_bias else None
        o_ref = refs[(3 if has_bias else 2) + len(deps)]
        p = lax.dot_general(a_ref[...], b_ref[...], dims, preferred_element_type=F32)

        def finish(v):
            if has_bias:
                v = v + bias_ref[...]
            o_ref[...] = v.astype(o_ref.dtype)

        if nk == 1:
            finish(p)
        else:
            acc = refs[-1]
            k = pl.program_id(2)

            @pl.when(k == 0)
            def _():
                acc[...] = p

            @pl.when(k > 0)
            def _():
                acc[...] += p

            @pl.when(k == nk - 1)
            def _():
                finish(acc[...])

    if trans_a:
        a_spec = pl.BlockSpec((tk, tm), lambda i, j, k: (k, i))
    else:
        a_spec = pl.BlockSpec((tm, tk), lambda i, j, k: (i, k))
    if trans_b:
        b_spec = pl.BlockSpec((tn, tk), lambda i, j, k: (j, k))
    else:
        b_spec = pl.BlockSpec((tk, tn), lambda i, j, k: (k, j))
    in_specs = [a_spec, b_spec]
    args = [a, b]
    if has_bias:
        in_specs.append(pl.BlockSpec((1, tn), lambda i, j, k: (0, j)))
        args.append(bias)
    in_specs += [ANY_SPEC] * len(deps)
    args += list(deps)
    return pl.pallas_call(
        body, name=name, grid=(M // tm, N // tn, nk),
        in_specs=in_specs, out_specs=pl.BlockSpec((tm, tn), lambda i, j, k: (i, j)),
        out_shape=jax.ShapeDtypeStruct((M, N), out_dtype),
        scratch_shapes=[pltpu.VMEM((tm, tn), F32)] if nk > 1 else [],
        compiler_params=_cparams(),
    )(*args)


def rowwise(name, fn, S, T, row_ins, full_ins, row_outs, acc_outs=(), into=None):
    n_in = len(row_ins) + len(full_ins)
    n_ro, n_ao = len(row_outs), len(acc_outs)
    alias = into is not None and into[0] is not None

    def body(*refs):
        vals = [r[...] for r in refs[:n_in]]
        vals = [v.astype(F32) if v.dtype == BF else v for v in vals]
        outs = fn(*vals)
        if not isinstance(outs, (tuple, list)):
            outs = (outs,)
        assert len(outs) == n_ro + n_ao, (name, len(outs))
        o0 = n_in + (1 if alias else 0)
        for r, v in zip(refs[o0:o0 + n_ro], outs[:n_ro]):
            r[...] = v.astype(r.dtype)
        first = pl.program_id(0) == 0
        for r, v in zip(refs[o0 + n_ro:], outs[n_ro:]):
            def init(r=r, v=v):
                r[...] = v

            def accum(r=r, v=v):
                r[...] += v

            pl.when(first)(init)
            pl.when(jnp.logical_not(first))(accum)

    in_specs, args = [], []
    for arr, W, off in row_ins:
        assert off % W == 0 and arr.shape[0] == S, (name, arr.shape, W, off)
        in_specs.append(pl.BlockSpec((T, W), functools.partial(lambda i, cb: (i, cb), cb=off // W)))
        args.append(arr)
    for arr in full_ins:
        in_specs.append(pl.BlockSpec(arr.shape, lambda i: (0, 0)))
        args.append(arr)
    out_specs = [pl.BlockSpec((T, W), lambda i: (i, 0)) for W, _ in row_outs]
    out_shape = [jax.ShapeDtypeStruct((S, W), dt) for W, dt in row_outs]
    aliases = {}
    if into is not None:
        buf, total, off = into
        W0, dt0 = row_outs[0]
        assert off % W0 == 0
        out_specs[0] = pl.BlockSpec((T, W0), functools.partial(lambda i, cb: (i, cb), cb=off // W0))
        out_shape[0] = jax.ShapeDtypeStruct((S, total), dt0)
        if alias:
            in_specs.append(ANY_SPEC)
            args.append(buf)
            aliases = {n_in: 0}
    out_specs += [pl.BlockSpec((1, W), lambda i: (0, 0)) for W in acc_outs]
    out_shape += [jax.ShapeDtypeStruct((1, W), F32) for W in acc_outs]
    return pl.pallas_call(
        body, name=name, grid=(S // T,), in_specs=in_specs, out_specs=out_specs, out_shape=out_shape,
        input_output_aliases=aliases, compiler_params=_cparams(),
    )(*args)


def _colsum(v):
    return jnp.sum(v, axis=0, keepdims=True)


def _ln_stats(r):
    mu = jnp.mean(r, axis=-1, keepdims=True)
    d = r - mu
    var = jnp.mean(d * d, axis=-1, keepdims=True)
    rstd = lax.rsqrt(var + LN_EPS)
    return d * rstd, rstd


def _ln_bwd(dn, n, rstd):
    return rstd * (dn - jnp.mean(dn, axis=-1, keepdims=True) - n * jnp.mean(dn * n, axis=-1, keepdims=True))


CPAD = 32
TC = 64


def _pre(mode, x1, x2):
    return x1 * _sigmoid(x2) if mode == "glu" else x1 * x2


def _shifted(ext, sft):
    n = TC + CPAD
    return pltpu.roll(ext, (n - sft) % n, 0)[0:TC]


def _interleaved_specs(S, off):
    return [pl.BlockSpec((S, LANE), functools.partial(lambda j, o: (0, o + 2 * j), o=off // LANE)),
            pl.BlockSpec((S, LANE), functools.partial(lambda j, o: (0, o + 2 * j + 1), o=off // LANE))]


def conv_fwd(name, src, off, w_pad, taps, mode, S, C):
    nchunk = S // TC

    def body(x1_ref, x2_ref, w_ref, o_ref, a_pad):
        a_pad[0:CPAD, :] = jnp.zeros((CPAD, LANE), F32)

        def fill(i, _):
            r = pl.multiple_of(i * 256, 256)
            a_pad[pl.ds(CPAD + r, 256), :] = _pre(mode, x1_ref[pl.ds(r, 256), :].astype(F32),
                                                  x2_ref[pl.ds(r, 256), :].astype(F32))
            return 0

        lax.fori_loop(0, S // 256, fill, 0)

        def chunk(i, _):
            base = pl.multiple_of(i * TC, TC)
            ext = a_pad[pl.ds(base, TC + CPAD), :]
            acc = jnp.zeros((TC, LANE), F32)
            for k in range(taps):
                acc = acc + w_ref[pl.ds(k, 1), :] * _shifted(ext, CPAD - (taps - 1) + k)
            o_ref[pl.ds(base, TC), :] = acc
            return 0

        lax.fori_loop(0, nchunk, chunk, 0)

    kp = w_pad.shape[0]
    return pl.pallas_call(
        body, name=name, grid=(C // LANE,),
        in_specs=_interleaved_specs(S, off) + [pl.BlockSpec((kp, LANE), lambda j: (0, j))],
        out_specs=pl.BlockSpec((S, LANE), lambda j: (0, j)),
        out_shape=jax.ShapeDtypeStruct((S, C), F32),
        scratch_shapes=[pltpu.VMEM((S + CPAD, LANE), F32)],
        compiler_params=_cparams(),
    )(src, src, w_pad)


def conv_bwd(name, src, off, dc, w_pad, taps, mode, S, C, buf):
    nchunk = S // TC
    kp = w_pad.shape[0]

    def body(x1_ref, x2_ref, dc_ref, w_ref, _, d_ref, dw_ref, a_pad, dc_pad, dw_acc):
        a_pad[0:CPAD, :] = jnp.zeros((CPAD, LANE), F32)
        dc_pad[S:S + CPAD, :] = jnp.zeros((CPAD, LANE), F32)
        dw_acc[...] = jnp.zeros(dw_acc.shape, F32)

        def fill(i, _):
            r = pl.multiple_of(i * 256, 256)
            a_pad[pl.ds(CPAD + r, 256), :] = _pre(mode, x1_ref[pl.ds(r, 256), :].astype(F32),
                                                  x2_ref[pl.ds(r, 256), :].astype(F32))
            dc_pad[pl.ds(r, 256), :] = dc_ref[pl.ds(r, 256), :]
            return 0

        lax.fori_loop(0, S // 256, fill, 0)

        def chunk(i, _):
            base = pl.multiple_of(i * TC, TC)
            ext_d = dc_pad[pl.ds(base, TC + CPAD), :]
            ext_a = a_pad[pl.ds(base, TC + CPAD), :]
            dcv = ext_d[0:TC]
            da = jnp.zeros((TC, LANE), F32)
            for k in range(taps):
                da = da + w_ref[pl.ds(k, 1), :] * _shifted(ext_d, taps - 1 - k)
                prod = dcv * _shifted(ext_a, CPAD - (taps - 1) + k)
                fold = prod[0:8]
                for g in range(1, TC // 8):
                    fold = fold + prod[8 * g:8 * g + 8]
                dw_acc[pl.ds(8 * k, 8), :] += fold
            x1 = x1_ref[pl.ds(base, TC), :].astype(F32)
            x2 = x2_ref[pl.ds(base, TC), :].astype(F32)
            if mode == "glu":
                s = _sigmoid(x2)
                d1, d2 = da * s, da * x1 * s * (1.0 - s)
            else:
                d1, d2 = da * x2, da * x1
            d_ref[pl.ds(base, TC), 0:LANE] = d1.astype(BF)
            d_ref[pl.ds(base, TC), LANE:2 * LANE] = d2.astype(BF)
            return 0

        lax.fori_loop(0, nchunk, chunk, 0)
        dw_ref[...] = jnp.zeros(dw_ref.shape, F32)
        for k in range(taps):
            dw_ref[pl.ds(k, 1), :] = jnp.sum(dw_acc[pl.ds(8 * k, 8), :], axis=0, keepdims=True)

    blk = pl.BlockSpec((S, LANE), lambda j: (0, j))
    return pl.pallas_call(
        body, name=name, grid=(C // LANE,),
        in_specs=_interleaved_specs(S, off) + [blk, pl.BlockSpec((kp, LANE), lambda j: (0, j)), ANY_SPEC],
        out_specs=[pl.BlockSpec((S, 2 * LANE), functools.partial(lambda j, o: (0, o + j), o=off // (2 * LANE))),
                   pl.BlockSpec((kp, LANE), lambda j: (0, j))],
        out_shape=[jax.ShapeDtypeStruct(buf.shape, BF), jax.ShapeDtypeStruct((kp, C), F32)],
        input_output_aliases={4: 0},
        scratch_shapes=[pltpu.VMEM((S + CPAD, LANE), F32), pltpu.VMEM((S + CPAD, LANE), F32),
                        pltpu.VMEM((8 * kp, LANE), F32)],
        compiler_params=_cparams(),
    )(src, src, dc, w_pad, buf)


FWD_TILES = (512, 512)
BWD_TILES = (512, 512)
QUADS = HEADS // 4
QW, KVW = 4 * (NOPE + ROPE), 4 * (NOPE + VH)
SCALE = (NOPE + ROPE) ** -0.5
NT_DIMS = (((1,), (1,)), ((), ()))
TN_DIMS = (((0,), (0,)), ((), ()))


def _lane_mask(width, group, dtype):
    lane = lax.broadcasted_iota(jnp.int32, (1, LANE), 1)
    return jnp.where(lane // width == group, 1.0, 0.0).astype(dtype)


def _visible(tq, tk, off):
    row = lax.broadcasted_iota(jnp.int32, (tq, tk), 0)
    col = lax.broadcasted_iota(jnp.int32, (tq, tk), 1)
    return col <= row + off


def _attn_tiles(S, tq, tk):
    tk = tk if S % tk == 0 else 256
    return min(tq, tk), tk


def attn_fwd(q, kv, kpe, S):
    tq, tk = _attn_tiles(S, *FWD_TILES)
    nq = S // tq

    def body(q_ref, kv_ref, kp_ref, o_ref, lse_ref):
        for t in range(2):
            cols = slice(t * LANE, (t + 1) * LANE)
            for hh in range(2):
                def q_block(qi, _, t=t, hh=hh, cols=cols):
                    r0 = pl.multiple_of(qi * tq, tq)
                    qcat = jnp.concatenate([q_ref[pl.ds(r0, tq), cols] * _lane_mask(NOPE, hh, BF),
                                            q_ref[pl.ds(r0, tq), 2 * LANE:3 * LANE] * _lane_mask(ROPE, 2 * t + hh, BF)],
                                           axis=1)
                    nfull = (qi * tq) // tk

                    def step(kj, carry, masked):
                        m, l, acc = carry
                        c0 = pl.multiple_of(kj * tk, tk)
                        kc = jnp.concatenate([kv_ref[pl.ds(c0, tk), cols], kp_ref[pl.ds(c0, tk), :]], axis=1)
                        vt = kv_ref[pl.ds(c0, tk), (2 + t) * LANE:(3 + t) * LANE]
                        s = lax.dot_general(qcat, kc, NT_DIMS, preferred_element_type=F32) * SCALE
                        if masked:
                            s = jnp.where(_visible(tq, tk, qi * tq - nfull * tk), s, -jnp.inf)
                        m_new = jnp.maximum(m, jnp.max(s, axis=-1, keepdims=True))
                        p = jnp.exp(s - m_new)
                        alpha = jnp.exp(m - m_new)
                        l = alpha * l + jnp.sum(p, axis=-1, keepdims=True)
                        acc = alpha * acc + jnp.dot(p.astype(BF), vt, preferred_element_type=F32)
                        return m_new, l, acc

                    init = (jnp.full((tq, 1), -jnp.inf, F32), jnp.zeros((tq, 1), F32), jnp.zeros((tq, LANE), F32))
                    carry = lax.fori_loop(0, nfull, lambda kj, c: step(kj, c, False), init)
                    m, l, acc = step(nfull, carry, True)
                    mine = _lane_mask(NOPE, hh, F32)
                    if hh == 0:
                        o_ref[pl.ds(r0, tq), cols] = (acc / l) * mine
                        lse_ref[pl.ds(r0, tq), cols] = (m + jnp.log(l)) * mine
                    else:
                        o_ref[pl.ds(r0, tq), cols] += (acc / l) * mine
                        lse_ref[pl.ds(r0, tq), cols] += (m + jnp.log(l)) * mine
                    return 0

                lax.fori_loop(0, nq, q_block, 0)

    return pl.pallas_call(
        body, name="attn_fwd", grid=(QUADS,),
        in_specs=[pl.BlockSpec((S, QW), lambda g: (0, g)), pl.BlockSpec((S, KVW), lambda g: (0, g)),
                  pl.BlockSpec((S, LANE), lambda g: (0, 0))],
        out_specs=[pl.BlockSpec((S, 2 * LANE), lambda g: (0, g))] * 2,
        out_shape=[jax.ShapeDtypeStruct((S, HEADS * VH), F32)] * 2,
        compiler_params=_cparams(),
    )(q, kv, kpe)


def attn_bwd(q, kv, kpe, o, lse, do, S):
    tq, tk = _attn_tiles(S, *BWD_TILES)
    nq = S // tq

    def body(q_ref, kv_ref, kp_ref, o_ref, lse_ref, do_ref, dq_ref, dkv_ref, dkp_ref, dq_acc, dk_acc, dv_acc):
        for t in range(2):
            cols = slice(t * LANE, (t + 1) * LANE)
            dk_acc[...] = jnp.zeros(dk_acc.shape, F32)
            dv_acc[...] = jnp.zeros(dv_acc.shape, F32)
            for hh in range(2):
                def q_block(qi, _, t=t, hh=hh, cols=cols):
                    r0 = pl.multiple_of(qi * tq, tq)
                    mine = _lane_mask(NOPE, hh, F32)
                    qcat = jnp.concatenate([q_ref[pl.ds(r0, tq), cols] * _lane_mask(NOPE, hh, BF),
                                            q_ref[pl.ds(r0, tq), 2 * LANE:3 * LANE] * _lane_mask(ROPE, 2 * t + hh, BF)],
                                           axis=1)
                    dof = do_ref[pl.ds(r0, tq), cols] * mine
                    dob = dof.astype(BF)
                    delta = jnp.sum(dof * o_ref[pl.ds(r0, tq), cols], axis=-1, keepdims=True)
                    lse_h = lse_ref[pl.ds(r0, tq), cols][:, hh * NOPE:hh * NOPE + 1]
                    nfull = (qi * tq) // tk
                    dq_acc[...] = jnp.zeros(dq_acc.shape, F32)

                    def step(kj, _, masked):
                        c0 = pl.multiple_of(kj * tk, tk)
                        kc = jnp.concatenate([kv_ref[pl.ds(c0, tk), cols], kp_ref[pl.ds(c0, tk), :]], axis=1)
                        vt = kv_ref[pl.ds(c0, tk), (2 + t) * LANE:(3 + t) * LANE]
                        s = lax.dot_general(qcat, kc, NT_DIMS, preferred_element_type=F32) * SCALE
                        if masked:
                            s = jnp.where(_visible(tq, tk, qi * tq - nfull * tk), s, -jnp.inf)
                        p = jnp.exp(s - lse_h)
                        dp = lax.dot_general(dob, vt, NT_DIMS, preferred_element_type=F32)
                        ds = (p * (dp - delta) * SCALE).astype(BF)
                        dv_acc[pl.ds(c0, tk), :] += lax.dot_general(p.astype(BF), dob, TN_DIMS,
                                                                    preferred_element_type=F32)
                        dk_acc[pl.ds(c0, tk), :] += lax.dot_general(ds, qcat, TN_DIMS, preferred_element_type=F32)
                        dq_acc[...] += jnp.dot(ds, kc, preferred_element_type=F32)
                        return 0

                    lax.fori_loop(0, nfull, lambda kj, c: step(kj, c, False), 0)
                    step(nfull, 0, True)
                    d = dq_acc[...]
                    pe = d[:, LANE:] * _lane_mask(ROPE, 2 * t + hh, F32)
                    if hh == 0:
                        dq_ref[pl.ds(r0, tq), cols] = d[:, :LANE] * mine
                    else:
                        dq_ref[pl.ds(r0, tq), cols] += d[:, :LANE] * mine
                    if t == 0 and hh == 0:
                        dq_ref[pl.ds(r0, tq), 2 * LANE:3 * LANE] = pe
                    else:
                        dq_ref[pl.ds(r0, tq), 2 * LANE:3 * LANE] += pe
                    return 0

                lax.fori_loop(0, nq, q_block, 0)
            dkv_ref[:, t * LANE:(t + 1) * LANE] = dk_acc[:, :LANE].astype(BF)
            dkv_ref[:, (2 + t) * LANE:(3 + t) * LANE] = dv_acc[...].astype(BF)
            if t == 0:
                dkp_ref[...] = dk_acc[:, LANE:]
            else:
                dkp_ref[...] += dk_acc[:, LANE:]

    qspec = pl.BlockSpec((S, QW), lambda g: (0, g))
    kvspec = pl.BlockSpec((S, KVW), lambda g: (0, g))
    ospec = pl.BlockSpec((S, 2 * LANE), lambda g: (0, g))
    return pl.pallas_call(
        body, name="attn_bwd", grid=(QUADS,),
        in_specs=[qspec, kvspec, pl.BlockSpec((S, LANE), lambda g: (0, 0)), ospec, ospec, ospec],
        out_specs=[qspec, kvspec, pl.BlockSpec((S, LANE), lambda g: (0, g))],
        out_shape=[jax.ShapeDtypeStruct((S, HEADS * (NOPE + ROPE)), F32), jax.ShapeDtypeStruct((S, HEADS * (NOPE + VH)), BF),
                   jax.ShapeDtypeStruct((S, HEADS * ROPE), F32)],
        scratch_shapes=[pltpu.VMEM((tq, 2 * LANE), F32), pltpu.VMEM((S, 2 * LANE), F32), pltpu.VMEM((S, LANE), F32)],
        compiler_params=_cparams(),
    )(q, kv, kpe, o, lse, do)


def exchange(name, gathers, a2as):
    n_g, n = len(gathers), len(gathers) + len(a2as)

    def body(*refs):
        ins, outs = refs[:n], refs[n:2 * n]
        send_sems, recv_sems, loc_sems = refs[2 * n:]
        x, y, c = lax.axis_index("x"), lax.axis_index("y"), lax.axis_index("c")
        me = 4 * x + 2 * y + c

        def peer(k):
            px = 1 - x if k & 4 else x
            py = 1 - y if k & 2 else y
            pc = 1 - c if k & 1 else c
            return (px, py, pc), 4 * px + 2 * py + pc

        def remote(a, k):
            pid, pflat = peer(k)
            src = ins[a] if a < n_g else ins[a].at[pflat]
            return pltpu.make_async_remote_copy(
                src_ref=src, dst_ref=outs[a].at[me], send_sem=send_sems.at[a, k - 1], recv_sem=recv_sems.at[a, k - 1],
                device_id=pid, device_id_type=MESH)

        def arrival(a, k):
            pid, pflat = peer(k)
            src = ins[a] if a < n_g else ins[a].at[pflat]
            return pltpu.make_async_remote_copy(
                src_ref=src, dst_ref=outs[a].at[pflat], send_sem=send_sems.at[a, k - 1], recv_sem=recv_sems.at[a, k - 1],
                device_id=pid, device_id_type=MESH)

        local = []
        for a in range(n):
            own = ins[a] if a < n_g else ins[a].at[me]
            cp = pltpu.make_async_copy(own, outs[a].at[me], loc_sems.at[a])
            cp.start()
            local.append(cp)
        sent = []
        for k in (1, 2, 4, 3, 5, 6, 7):
            for a in range(n):
                cp = remote(a, k)
                cp.start()
                sent.append(cp)
        for k in range(1, 8):
            for a in range(n):
                arrival(a, k).wait_recv()
        for cp in sent:
            cp.wait_send()
        for cp in local:
            cp.wait()

    out_shape = [jax.ShapeDtypeStruct((NDEV,) + g.shape, g.dtype) for g in gathers]
    out_shape += [jax.ShapeDtypeStruct(a.shape, a.dtype) for a in a2as]
    any_spec = pl.BlockSpec(memory_space=pl.ANY)
    return pl.pallas_call(
        body, name=name, in_specs=[any_spec] * n, out_specs=[any_spec] * n, out_shape=out_shape,
        scratch_shapes=[pltpu.SemaphoreType.DMA((n, NDEV - 1)), pltpu.SemaphoreType.DMA((n, NDEV - 1)),
                        pltpu.SemaphoreType.DMA((n,))],
    )(*gathers, *a2as)


def gather_two_level(name, block):
    def body(x_ref, out_ref, stage, send_sems, recv_sems, loc_sem):
        x, y, c = lax.axis_index("x"), lax.axis_index("y"), lax.axis_index("c")
        me, sibling = (x, y, c), (x, y, 1 - c)
        chips = [(1 - x, y), (x, 1 - y), (1 - x, 1 - y)]

        def slot(px, py, pc):
            return out_ref.at[4 * px + 2 * py + pc]

        def copy(k, owner, to, src=None):
            return pltpu.make_async_remote_copy(
                src_ref=slot(*owner) if src is None else src, dst_ref=slot(*owner), send_sem=send_sems.at[k],
                recv_sem=recv_sems.at[k], device_id=to, device_id_type=MESH)

        load = pltpu.make_async_copy(x_ref, stage, loc_sem)
        load.start()
        first = [copy(0, me, sibling, src=x_ref)] + [copy(1 + j, me, (*chip, c), src=x_ref) for j, chip in enumerate(chips)]
        for cp in first:
            cp.start()
        load.wait()
        store = pltpu.make_async_copy(stage, slot(*me), loc_sem)
        store.start()
        passed = [copy(4 + j, (*chip, c), sibling) for j, chip in enumerate(chips)]
        for j, chip in enumerate(chips):
            copy(1 + j, (*chip, c), me).wait_recv()
            passed[j].start()
        copy(0, sibling, me).wait_recv()
        for j, chip in enumerate(chips):
            copy(4 + j, (*chip, 1 - c), me).wait_recv()
        for cp in first + passed:
            cp.wait_send()
        store.wait()

    return pl.pallas_call(
        body, name=name, in_specs=[pl.BlockSpec(memory_space=pl.ANY)], out_specs=pl.BlockSpec(memory_space=pl.ANY),
        out_shape=jax.ShapeDtypeStruct((NDEV,) + block.shape, block.dtype),
        scratch_shapes=[pltpu.VMEM(block.shape, block.dtype), pltpu.SemaphoreType.DMA((NDEV - 1,)),
                        pltpu.SemaphoreType.DMA((NDEV - 1,)), pltpu.SemaphoreType.DMA],
        compiler_params=_cparams(),
    )(block)


def _peer(k, x, y, c):
    px = 1 - x if k & 4 else x
    py = 1 - y if k & 2 else y
    pc = 1 - c if k & 1 else c
    return (px, py, pc), 4 * px + 2 * py + pc


PEER_ORDER = (1, 2, 4, 3, 5, 6, 7)
HBM_SPEC = pl.BlockSpec(memory_space=pltpu.HBM)
SEM_SPEC = pl.BlockSpec(memory_space=pltpu.SEMAPHORE)
ANY_SPEC = pl.BlockSpec(memory_space=pl.ANY)


def _split_copies(ins, lands, n_g, send_sems, recv_sems):
    x, y, c = lax.axis_index("x"), lax.axis_index("y"), lax.axis_index("c")
    me = 4 * x + 2 * y + c

    def outgoing(a, k):
        pid, pflat = _peer(k, x, y, c)
        src = ins[a] if a < n_g else ins[a].at[pflat]
        return pltpu.make_async_remote_copy(
            src_ref=src, dst_ref=lands[a].at[me], send_sem=send_sems.at[a * (NDEV - 1) + k - 1],
            recv_sem=recv_sems.at[a * (NDEV - 1) + k - 1],
            device_id=pid, device_id_type=MESH)

    def arrival(a, k):
        pid, pflat = _peer(k, x, y, c)
        src = ins[a] if a < n_g else ins[a].at[pflat]
        return pltpu.make_async_remote_copy(
            src_ref=src, dst_ref=lands[a].at[pflat], send_sem=send_sems.at[a * (NDEV - 1) + k - 1],
            recv_sem=recv_sems.at[a * (NDEV - 1) + k - 1],
            device_id=pid, device_id_type=MESH)

    return outgoing, arrival


def exchange_begin(name, srcs, n_g, dep):
    n = len(srcs)
    land_shapes = [((NDEV,) + s.shape) if a < n_g else s.shape for a, s in enumerate(srcs)]

    def own_body(*refs):
        ins, outs = refs[:n], refs[n + 1:2 * n + 1]
        stage, sems = refs[2 * n + 1:3 * n + 1], refs[-1]
        me = 4 * lax.axis_index("x") + 2 * lax.axis_index("y") + lax.axis_index("c")
        cps = [pltpu.make_async_copy(ins[a] if a < n_g else ins[a].at[me], stage[a], sems.at[a]) for a in range(n)]
        for cp in cps:
            cp.start()
        for cp in cps:
            cp.wait()
        cps = [pltpu.make_async_copy(stage[a], outs[a].at[me], sems.at[a]) for a in range(n)]
        for cp in cps:
            cp.start()
        for cp in cps:
            cp.wait()

    lands = pl.pallas_call(
        own_body, name=name + "_own", in_specs=[ANY_SPEC] * (n + 1), out_specs=[ANY_SPEC] * n,
        out_shape=[jax.ShapeDtypeStruct(sh, s.dtype) for sh, s in zip(land_shapes, srcs)],
        scratch_shapes=[pltpu.VMEM(sh[1:], s.dtype) for sh, s in zip(land_shapes, srcs)] + [pltpu.SemaphoreType.DMA((n,))],
        compiler_params=_cparams(),
    )(*srcs, dep)

    def start_body(*refs):
        ins, lz = refs[:n], refs[n:2 * n]
        send_sems, recv_sems, token = refs[2 * n], refs[2 * n + 1], refs[-1]
        outgoing, _ = _split_copies(ins, lz, n_g, send_sems, recv_sems)
        for k in PEER_ORDER:
            for a in range(n):
                outgoing(a, k).start()
        token[...] = jnp.zeros(token.shape, F32)

    hbm = lambda t: pltpu.HBM(t.shape, t.dtype)
    res = pl.pallas_call(
        start_body, name=name + "_start",
        out_shape=(pltpu.SemaphoreType.DMA((n * (NDEV - 1),)), pltpu.SemaphoreType.DMA((n * (NDEV - 1),)),
                   *[hbm(s) for s in srcs], *[hbm(t) for t in lands], jax.ShapeDtypeStruct((8, LANE), F32)),
        in_specs=[HBM_SPEC] * (2 * n),
        out_specs=(SEM_SPEC, SEM_SPEC, *[HBM_SPEC] * (2 * n), pl.BlockSpec(memory_space=pltpu.VMEM)),
        input_output_aliases={i: 2 + i for i in range(2 * n)},
        compiler_params=pltpu.CompilerParams(has_side_effects=pltpu.SideEffectType.DATAFLOW_SIDE_EFFECTING),
    )(*[pltpu.with_memory_space_constraint(t, pltpu.HBM) for t in list(srcs) + list(lands)])
    return (name, n, n_g, res[:-1]), res[-1]


def exchange_end(handle, after):
    name, n, n_g, (send_sems, recv_sems, *bufs) = handle

    def wait_body(*refs):
        ins, lz = refs[:n], refs[n:2 * n]
        ss, rs = refs[2 * n], refs[2 * n + 1]
        outgoing, arrival = _split_copies(ins, lz, n_g, ss, rs)
        for k in range(1, NDEV):
            for a in range(n):
                arrival(a, k).wait_recv()
        for k in range(1, NDEV):
            for a in range(n):
                outgoing(a, k).wait_send()

    res = pl.pallas_call(
        wait_body, name=name + "_wait", out_shape=tuple(pltpu.HBM(t.shape, t.dtype) for t in bufs),
        in_specs=[HBM_SPEC] * (2 * n) + [SEM_SPEC, SEM_SPEC, ANY_SPEC], out_specs=[HBM_SPEC] * (2 * n),
        input_output_aliases={i: i for i in range(2 * n)},
        compiler_params=pltpu.CompilerParams(has_side_effects=pltpu.SideEffectType.DATAFLOW_SIDE_EFFECTING),
    )(*bufs, send_sems, recv_sems, after)
    return list(res[n:])


def _pick_rows(R, mult, cap):
    best = None
    for n in range(1, R + 1):
        if R % n == 0 and (R // n) % mult == 0 and R // n <= cap:
            best = R // n
            break
    assert best is not None, (R, mult, cap)
    return best


def sum_slots(name, x):
    _, R, _ = x.shape
    tr = _pick_rows(R, 16, 2304)

    def body(x_ref, o_ref):
        acc = x_ref[0].astype(F32)
        for d in range(1, NDEV):
            acc = acc + x_ref[d].astype(F32)
        o_ref[...] = acc

    return pl.pallas_call(
        body, name=name, grid=(R // tr,),
        in_specs=[pl.BlockSpec((NDEV, tr, LANE), lambda i: (0, i, 0))],
        out_specs=pl.BlockSpec((tr, LANE), lambda i: (i, 0)),
        out_shape=jax.ShapeDtypeStruct((R, LANE), F32), compiler_params=_cparams(),
    )(x)


def adamw(name, w, g, m, v):
    L, R, C = w.shape
    tr = _pick_rows(R, 8, 256) if R % 8 == 0 else R

    def body(w_ref, g_ref, m_ref, v_ref, d_ref, nm_ref, nv_ref):
        gg = g_ref[...]
        nm = B1 * m_ref[...] + (1.0 - B1) * gg
        nv = B2 * v_ref[...] + (1.0 - B2) * jnp.square(gg)
        m_hat = nm / (1.0 - B1 ** STEP)
        v_hat = nv / (1.0 - B2 ** STEP)
        d_ref[...] = -LR * (m_hat / (jnp.sqrt(v_hat) + EPS) + WD * w_ref[...])
        nm_ref[...] = nm
        nv_ref[...] = nv

    blk = pl.BlockSpec((1, tr, C), lambda l, i: (l, i, 0))
    shp = jax.ShapeDtypeStruct(w.shape, F32)
    return pl.pallas_call(
        body, name=name, grid=(L, R // tr), in_specs=[blk] * 4, out_specs=[blk] * 3, out_shape=[shp] * 3,
        compiler_params=_cparams(),
    )(w, g, m, v)


IN_SHARD = D_IN // NDEV
UQ_SHARD = HEADS * (NOPE + ROPE) // NDEV
W_IN_PAD = 1024
ROW_A, ROW_B, ROW_C, ROW_UKV, ROW_UQ, MISC_ROWS = 0, 512, 1024, 1536, 1792, 2176


def _in_perm_index():
    ar = np.arange
    z = lambda n: np.full((n,), -1, np.int64)
    mix = lambda lo1, lo2: np.concatenate([ar(lo + LANE * j, lo + LANE * (j + 1)) for j in range(CW // LANE)
                                           for lo in (lo1, lo2)])
    return np.concatenate([ar(4768, 7840), mix(0, 512), ar(1024, 1536), mix(1536, 2560), ar(4256, 4768), ar(2048, 2560),
                           ar(3072, 3584), ar(3968, 4224), ar(4224, 4256), z(OFF_Q - OFF_KR - ROPE), ar(3584, 3968),
                           z(NP - OFF_Q - QL)])


def _head_perm_index(a, b):
    parts = []
    for g in range(QUADS):
        h = np.arange(4 * g, 4 * g + 4)[:, None] * (a + b)
        parts += [(h + np.arange(a)[None]).reshape(-1), (h + a + np.arange(b)[None]).reshape(-1)]
    return np.concatenate(parts)


def _inverse(perm, n):
    inv = np.full((n,), -1, np.int64)
    inv[perm[perm >= 0]] = np.nonzero(perm >= 0)[0]
    return inv


IN_PERM = _in_perm_index()
UQ_PERM = _head_perm_index(NOPE, ROPE)
UKV_PERM = _head_perm_index(NOPE, VH)


def _to_gathered(perm, shard, pad):
    return np.where(perm >= 0, (perm // shard) * pad + perm % shard, -1)


def _from_full(inv, shard, pad):
    j, i = np.divmod(np.arange(NDEV * pad), pad)
    return np.where(i < shard, inv[np.minimum(j * shard + i, inv.shape[0] - 1)], -1)


def col_gather(name, srcs, out_shapes, jobs, deps=()):
    ns, nj, nd, no = len(srcs), len(jobs), len(deps), len(out_shapes)
    tables = [jnp.asarray(np.asarray(job[5], np.int32)[None, :]) for job in jobs]

    def view(ref, col0, width, r0, rc):
        n = ref.shape[-1]
        if len(ref.shape) == 3:
            return ref.at[col0 // n, pl.ds(r0, rc), pl.ds(col0 % n, width)]
        return ref.at[pl.ds(r0, rc), pl.ds(col0, width)]

    def slabs(shape):
        if len(shape) == 3:
            return [((d,), d * shape[2], (d + 1) * shape[2]) for d in range(shape[0])]
        w = 1024 if shape[1] > 1024 and shape[1] % 1024 == 0 else shape[1]
        return [((slice(None), pl.ds(c, w)), c, c + w) for c in range(0, shape[1], w)]

    src_slabs = [slabs(s.shape) for s in srcs]
    out_slabs = [slabs(sh) for sh in out_shapes]
    work, first_use, last_touch = [], {}, {}
    for ji, (si, srow, oi, orow, nrows, tgt) in enumerate(jobs):
        tgt = np.asarray(tgt)
        tw = 256 if out_shapes[oi][-1] % 256 == 0 else LANE
        sw = 256 if srcs[si].shape[-1] % 256 == 0 else LANE
        for t in range(tgt.shape[0] // tw):
            tt = tgt[t * tw:(t + 1) * tw]
            tiles = sorted(set((tt[tt >= 0] // sw).tolist()))
            straight = bool(tiles) and tt[0] >= 0 and tt[0] % LANE == 0 and np.array_equal(tt, tt[0] + np.arange(tw))
            cols = [(int(tt[0]) + k * LANE, LANE) for k in range(tw // LANE)] if straight else [(s * sw, sw) for s in tiles]
            need = sorted({(si, k) for c0, _ in cols for k, (_, lo, hi) in enumerate(src_slabs[si]) if lo <= c0 < hi})
            touch = [(oi, k) for k, (_, lo, hi) in enumerate(out_slabs[oi]) if lo <= t * tw < hi][0]
            for key in need:
                first_use.setdefault(key, len(work))
            last_touch[touch] = len(work)
            work.append((ji, t, tw, sw, tiles, straight, need, touch))
    in_order = sorted(first_use, key=first_use.get)
    in_sem = {key: i for i, key in enumerate(in_order)}
    out_keys = sorted(last_touch)
    out_sem = {key: i for i, key in enumerate(out_keys)}

    def body(*refs):
        src_hbm, tab_refs = refs[:ns], refs[ns:ns + nj]
        out_hbm = refs[ns + nj + nd:ns + nj + nd + no]
        scratch = refs[ns + nj + nd + no:]
        src_refs, out_refs, in_sems, out_sems = scratch[:ns], scratch[ns:ns + no], scratch[-2], scratch[-1]
        loads = {}
        for key in in_order:
            si, k = key
            idx = src_slabs[si][k][0]
            loads[key] = pltpu.make_async_copy(src_hbm[si].at[idx], src_refs[si].at[idx], in_sems.at[in_sem[key]])
            loads[key].start()
        arrived, stores = set(), []
        for wi, (ji, t, tw, sw, tiles, straight, need, touch) in enumerate(work):
            si, srow, oi, orow, nrows, tgt = jobs[ji]
            sref, oref = src_refs[si], out_refs[oi]
            rc = nrows if nrows <= 1024 else 1024
            for key in need:
                if key not in arrived:
                    loads[key].wait()
                    arrived.add(key)
            onehots = []
            if tiles and not straight:
                want = tab_refs[ji][:, t * tw:(t + 1) * tw]
                row = lax.broadcasted_iota(jnp.int32, (sw, tw), 0)
                onehots = [jnp.where(want == row + s * sw, 1.0, 0.0).astype(BF) for s in tiles]
            first = int(np.asarray(tgt)[t * tw])

            def chunk(ci, _, t=t, tw=tw, sw=sw, tiles=tiles, straight=straight, onehots=onehots, first=first,
                      sref=sref, oref=oref, srow=srow, orow=orow, rc=rc):
                r0 = ci * rc
                ro = pl.multiple_of(orow + r0, LANE)
                rs = pl.multiple_of(srow + r0, LANE)
                if not tiles:
                    view(oref, t * tw, tw, ro, rc)[...] = jnp.zeros((rc, tw), BF)
                elif straight:
                    for k in range(tw // LANE):
                        view(oref, t * tw + k * LANE, LANE, ro, rc)[...] = view(sref, first + k * LANE, LANE, rs, rc)[...]
                else:
                    acc = None
                    for s, oh in zip(tiles, onehots):
                        p = jnp.dot(view(sref, s * sw, sw, rs, rc)[...], oh, preferred_element_type=F32)
                        acc = p if acc is None else acc + p
                    view(oref, t * tw, tw, ro, rc)[...] = acc.astype(BF)
                return 0

            lax.fori_loop(0, nrows // rc, chunk, 0)
            if last_touch[touch] == wi:
                idx = out_slabs[touch[0]][touch[1]][0]
                cp = pltpu.make_async_copy(out_refs[touch[0]].at[idx], out_hbm[touch[0]].at[idx], out_sems.at[out_sem[touch]])
                cp.start()
                stores.append(cp)
        for cp in stores:
            cp.wait()

    return pl.pallas_call(
        body, name=name, in_specs=[ANY_SPEC] * ns + [pl.BlockSpec(memory_space=pltpu.VMEM)] * nj + [ANY_SPEC] * nd,
        out_specs=[ANY_SPEC] * no, out_shape=[jax.ShapeDtypeStruct(s, BF) for s in out_shapes],
        scratch_shapes=[pltpu.VMEM(s.shape, BF) for s in srcs] + [pltpu.VMEM(s, BF) for s in out_shapes]
        + [pltpu.SemaphoreType.DMA((len(in_order),)), pltpu.SemaphoreType.DMA((len(out_keys),))],
        compiler_params=_cparams(),
    )(*srcs, *tables, *deps)


def sum_adamw(name, recvs, w, m, v, lo=0, prev=None):
    _, R, C = w.shape
    L = len(recvs)
    CP = recvs[0].shape[-1]
    tr = _pick_rows(R, 16, 128)
    n_prev = 0 if prev is None else 4

    def body(*refs):
        r_refs = refs[:L]
        w_ref, m_ref, v_ref = refs[L:L + 3]
        g_ref, d_ref, nm_ref, nv_ref, gsum = refs[L + 3 + n_prev:]
        layer = pl.program_id(0)
        for k in range(L):
            def total(k=k):
                acc = r_refs[k][0].astype(F32)
                for d in range(1, NDEV):
                    acc = acc + r_refs[k][d].astype(F32)
                gsum[...] = acc
            pl.when(layer == k)(total)
        gg = gsum[:, 0:C]
        nm = B1 * m_ref[...] + (1.0 - B1) * gg
        nv = B2 * v_ref[...] + (1.0 - B2) * jnp.square(gg)
        m_hat = nm / (1.0 - B1 ** STEP)
        v_hat = nv / (1.0 - B2 ** STEP)
        g_ref[...] = gg
        d_ref[...] = -LR * (m_hat / (jnp.sqrt(v_hat) + EPS) + WD * w_ref[...])
        nm_ref[...] = nm
        nv_ref[...] = nv

    r_specs = [pl.BlockSpec((NDEV, tr, CP), functools.partial(lambda l, i, k: (0, jnp.where(l == k, i, 0), 0), k=k))
               for k in range(L)]
    blk = pl.BlockSpec((None, tr, C), lambda l, i: (l + lo, i, 0))
    shp = jax.ShapeDtypeStruct(w.shape, F32)
    return pl.pallas_call(
        body, name=name, grid=(L, R // tr), in_specs=r_specs + [blk] * 3 + [ANY_SPEC] * n_prev, out_specs=[blk] * 4,
        out_shape=[shp] * 4, input_output_aliases={L + 3 + i: i for i in range(n_prev)},
        scratch_shapes=[pltpu.VMEM((tr, CP), F32)], compiler_params=_cparams(),
    )(*recvs, w, m, v, *(prev or ()))


ALPHA = 8.0 ** 0.25


def _rope_fn(sign):
    def fn(x, cos, sin):
        W = x.shape[-1]
        lane = lax.broadcasted_iota(jnp.int32, x.shape, 1)
        first_half = (lane % ROPE) < (ROPE // 2)
        rot = jnp.where(first_half, -pltpu.roll(x, W - ROPE // 2, 1), pltpu.roll(x, ROPE // 2, 1))
        return x * cos + sign * rot * sin
    return fn


def layer_fwd(x, ada3, W, tabs, S):
    cos, sin = tabs
    T = 256
    u = rowwise("modulate", lambda xv, a: xv * (1.0 + a[1:2, :]) + a[0:1, :], S, T,
                [(x, D_MODEL, 0)], [ada3], [(D_MODEL, BF)])[0]
    proj = mm(u, W["in"], name="mm_proj", tm=1024, tn=1024, out_dtype=BF)
    W = {**W, **W["late"](proj)}

    ca = conv_fwd("conv_a_fwd", proj, OFF_A, W["conv_a"], 31, "glu", S, CW)

    def a_post(c, ag, vec):
        n, _ = _ln_stats(c + vec[0:1, :])
        return _silu(n * vec[1:2, :] + vec[2:3, :]) * _silu(ag)

    h_a = rowwise("mix_a_post", a_post, S, T, [(ca, CW, 0), (proj, CW, OFF_AG)], [W["vec_a"]], [(CW, BF)])[0]
    y_a = mm(h_a, W["a_out"], name="mm_branch_out", out_dtype=BF)

    cb = conv_fwd("conv_b_fwd", proj, OFF_B, W["conv_b"], 3, "mul", S, CW)
    h_b = rowwise("mix_b_post", lambda c, gb, bg: gb * c * _silu(bg), S, T,
                  [(cb, CW, 0), (proj, CW, OFF_GB), (proj, CW, OFF_BG)], [], [(CW, BF)])[0]
    y_b = mm(h_b, W["b_out"], name="mm_branch_out", out_dtype=BF)

    def rms2(ql, kvl, gq, gkv):
        rq = lax.rsqrt(jnp.mean(ql * ql, axis=-1, keepdims=True) + RMS_EPS)
        rk = lax.rsqrt(jnp.mean(kvl * kvl, axis=-1, keepdims=True) + RMS_EPS)
        return ql * rq * gq, kvl * rk * gkv

    qn, kvn = rowwise("rms_fwd", rms2, S, T, [(proj, QL, OFF_Q), (proj, KVL, OFF_KV)], [W["gq"], W["gkv"]],
                      [(QL, BF), (KVL, BF)])
    q = mm(qn, W["uq"], name="mm_q")
    kv = mm(kvn, W["ukv"], name="mm_kv", out_dtype=BF)
    rope = _rope_fn(1.0)

    def rope_fwd(qv, kr, c1, s1):
        parts = []
        for g in range(QUADS):
            parts.append(qv[:, g * QW:g * QW + 2 * LANE].astype(BF))
            parts.append(rope(qv[:, g * QW + 2 * LANE:(g + 1) * QW], c1, s1).astype(BF))
        kp = rope(kr, c1, s1)
        kp = kp + pltpu.roll(kp, ROPE, 1) + pltpu.roll(kp, 2 * ROPE, 1) + pltpu.roll(kp, 3 * ROPE, 1)
        return jnp.concatenate(parts, axis=1), kp

    q_b, kpe = rowwise("rope_fwd", rope_fwd, S, T,
                       [(q, HEADS * (NOPE + ROPE), 0), (proj, LANE, OFF_KR), (cos, LANE, 0), (sin, LANE, 0)], [],
                       [(HEADS * (NOPE + ROPE), BF), (LANE, BF)])
    o, lse = attn_fwd(q_b, kv, kpe, S)
    h_c = rowwise("mix_c_post", lambda ov, cg: ov * _silu(cg), S, T, [(o, CW, 0), (proj, CW, OFF_CG)], [],
                  [(CW, BF)])[0]
    y_c = mm(h_c, W["c_out"], name="mm_branch_out", out_dtype=BF)

    def merge(la, lb, lc, ya, yb, yc):
        return _sigmoid(la) * ya + _sigmoid(lb) * yb + _sigmoid(lc) * yc

    m = rowwise("merge_fwd", merge, S, 128,
                [(proj, D_MODEL, 0), (proj, D_MODEL, 1024), (proj, D_MODEL, 2048), (y_a, D_MODEL, 0),
                 (y_b, D_MODEL, 0), (y_c, D_MODEL, 0)], [], [(D_MODEL, BF)])[0]
    out = mm(m, W["o"], name="mm_out")

    def ln_fwd(xv, ov, a, lnv):
        n, _ = _ln_stats(ALPHA * xv + a[2:3, :] * ov)
        return n * lnv[0:1, :] + lnv[1:2, :]

    x_next = rowwise("ln_fwd", ln_fwd, S, 128, [(x, D_MODEL, 0), (out, D_MODEL, 0)], [ada3, W["lnv"]],
                     [(D_MODEL, F32)])[0]
    saved = dict(x=x, u=u, proj=proj, ca=ca, cb=cb, h_a=h_a, h_b=h_b, h_c=h_c, y_a=y_a, y_b=y_b, y_c=y_c, qn=qn,
                 kvn=kvn, q_b=q_b, kv=kv, kpe=kpe, lse=lse, o=o, m=m, out=out)
    return x_next, saved, W


def layer_bwd(dxn, sv, ada3, W, tabs, S, before_in=None):
    cos, sin = tabs
    T = 256
    x, proj = sv["x"], sv["proj"]
    G = {}

    def ln_bwd(xv, ov, dy, a, lnv):
        gate = a[2:3, :]
        n, rstd = _ln_stats(ALPHA * xv + gate * ov)
        dr = _ln_bwd(dy * lnv[0:1, :], n, rstd)
        return ALPHA * dr, gate * dr, _colsum(dy * n), _colsum(dy), _colsum(dr * ov)

    dres, d_out, G["ln_g"], G["ln_b"], d_gate = rowwise(
        "ln_bwd", ln_bwd, S, 128, [(x, D_MODEL, 0), (sv["out"], D_MODEL, 0), (dxn, D_MODEL, 0)], [ada3, W["lnv"]],
        [(D_MODEL, F32), (D_MODEL, BF)], [D_MODEL] * 3)
    dm = mm(d_out, W["o"], name="mm_dm", trans_b=True, out_dtype=BF)
    G["w_o"] = mm(sv["m"], d_out, name="mm_gw_o", trans_a=True, out_dtype=BF)

    def merge_bwd(dmv, la, lb, lc, ya, yb, yc):
        outs, dls = [], []
        for lg, yv in ((la, ya), (lb, yb), (lc, yc)):
            s = _sigmoid(lg)
            outs.append(dmv * s)
            dls.append((dmv * yv * s * (1.0 - s)).astype(BF))
        return (jnp.concatenate(dls, axis=1),) + tuple(outs)

    d_proj, dy_a, dy_b, dy_c = rowwise(
        "merge_bwd", merge_bwd, S, 128,
        [(dm, D_MODEL, 0), (proj, D_MODEL, 0), (proj, D_MODEL, 1024), (proj, D_MODEL, 2048), (sv["y_a"], D_MODEL, 0),
         (sv["y_b"], D_MODEL, 0), (sv["y_c"], D_MODEL, 0)], [], [(3 * D_MODEL, BF)] + [(D_MODEL, BF)] * 3,
        into=(None, NP, OFF_M))

    dh = {}
    for br, dy in (("a", dy_a), ("b", dy_b), ("c", dy_c)):
        dh[br] = mm(dy, W[br + "_out"], name="mm_dh", trans_b=True, out_dtype=BF)
        G["w_%s_out" % br] = mm(sv["h_" + br], dy, name="mm_gw_branch", trans_a=True, out_dtype=BF)

    def a_post_bwd(c, ag, dhv, vec):
        n, rstd = _ln_stats(c + vec[0:1, :])
        z = n * vec[1:2, :] + vec[2:3, :]
        d_ag = dhv * _silu(z) * _dsilu(ag)
        dz = dhv * _silu(ag) * _dsilu(z)
        dc = _ln_bwd(dz * vec[1:2, :], n, rstd)
        return d_ag, dc, _colsum(dc), _colsum(dz * n), _colsum(dz)

    d_proj, dca, G["conv_a_b"], G["ln_a_g"], G["ln_a_b"] = rowwise(
        "mix_a_post_bwd", a_post_bwd, S, T, [(sv["ca"], CW, 0), (proj, CW, OFF_AG), (dh["a"], CW, 0)], [W["vec_a"]],
        [(CW, BF), (CW, F32)], [CW] * 3, into=(d_proj, NP, OFF_AG))
    d_proj, G["conv_a_w"] = conv_bwd("conv_a_bwd", proj, OFF_A, dca, W["conv_a"], 31, "glu", S, CW, d_proj)

    def b_post_bwd(c, gb, bg, dhv):
        sg = _silu(bg)
        d_gb_bg = jnp.concatenate([(dhv * sg * c).astype(BF), (dhv * gb * c * _dsilu(bg)).astype(BF)], axis=1)
        return d_gb_bg, dhv * sg * gb

    d_proj, dcb = rowwise("mix_b_post_bwd", b_post_bwd, S, T,
                          [(sv["cb"], CW, 0), (proj, CW, OFF_GB), (proj, CW, OFF_BG), (dh["b"], CW, 0)], [],
                          [(2 * CW, BF), (CW, F32)], into=(d_proj, NP, OFF_GB))
    d_proj, G["conv_b_w"] = conv_bwd("conv_b_bwd", proj, OFF_B, dcb, W["conv_b"], 3, "mul", S, CW, d_proj)

    d_proj, d_o = rowwise("mix_c_post_bwd", lambda ov, cg, dhv: (dhv * ov * _dsilu(cg), dhv * _silu(cg)), S, T,
                          [(sv["o"], CW, 0), (proj, CW, OFF_CG), (dh["c"], CW, 0)], [], [(CW, BF), (CW, F32)],
                          into=(d_proj, NP, OFF_CG))
    dq, d_kv, dkp_heads = attn_bwd(sv["q_b"], sv["kv"], sv["kpe"], sv["o"], sv["lse"], d_o, S)
    ropeT = _rope_fn(-1.0)

    def rope_bwd(dqv, dkp, c1, s1):
        parts = []
        for g in range(QUADS):
            parts.append(dqv[:, g * QW:g * QW + 2 * LANE].astype(BF))
            parts.append(ropeT(dqv[:, g * QW + 2 * LANE:(g + 1) * QW], c1, s1).astype(BF))
        f = dkp[:, :LANE] + dkp[:, LANE:]
        f = f + pltpu.roll(f, 64, 1)
        f = f + pltpu.roll(f, 32, 1)
        lane = lax.broadcasted_iota(jnp.int32, f.shape, 1)
        return jnp.concatenate(parts, axis=1), jnp.where(lane < ROPE, ropeT(f, c1, s1), 0.0)

    d_q, dk_pe = rowwise("rope_bwd", rope_bwd, S, T,
                         [(dq, HEADS * (NOPE + ROPE), 0), (dkp_heads, HEADS * ROPE, 0), (cos, LANE, 0), (sin, LANE, 0)],
                         [], [(HEADS * (NOPE + ROPE), BF), (LANE, BF)])
    d_qn = mm(d_q, W["uq"], name="mm_dqn", trans_b=True, out_dtype=BF)
    d_kvn = mm(d_kv, W["ukv"], name="mm_dkvn", trans_b=True, out_dtype=BF)
    G["w_uq"] = mm(sv["qn"], d_q, name="mm_gw_uq", trans_a=True, out_dtype=BF)
    G["w_ukv"] = mm(sv["kvn"], d_kv, name="mm_gw_ukv", trans_a=True, out_dtype=BF)

    def rms_bwd(ql, kvl, dqn, dkn, dkp, gq, gkv):
        res = []
        for xv, dy, g in ((ql, dqn, gq), (kvl, dkn, gkv)):
            r = lax.rsqrt(jnp.mean(xv * xv, axis=-1, keepdims=True) + RMS_EPS)
            dxh = dy * g
            res.append(((r * (dxh - xv * (r * r) * jnp.mean(dxh * xv, axis=-1, keepdims=True))).astype(BF),
                        _colsum(dy * xv * r)))
        pad = jnp.zeros((ql.shape[0], LANE), BF)
        return jnp.concatenate([res[1][0], dkp, pad, res[0][0], pad], axis=1), res[0][1], res[1][1]

    d_proj, G["q_norm_g"], G["kv_norm_g"] = rowwise(
        "rms_bwd", rms_bwd, S, T,
        [(proj, QL, OFF_Q), (proj, KVL, OFF_KV), (d_qn, QL, 0), (d_kvn, KVL, 0), (dk_pe, LANE, 0)],
        [W["gq"], W["gkv"]], [(NP - OFF_KV, BF)], [QL, KVL], into=(d_proj, NP, OFF_KV))
    deps = before_in(G) if before_in is not None else ()
    du = mm(d_proj, W["in"], name="mm_du", trans_b=True, tk=1024, deps=deps)
    G["w_in"] = mm(sv["u"], d_proj, name="mm_gw_in", trans_a=True, out_dtype=BF, deps=deps)

    def mod_bwd(duv, xv, dr, a):
        return duv * (1.0 + a[1:2, :]) + dr, _colsum(duv), _colsum(duv * xv)

    dx, d_shift, d_scale = rowwise("mod_bwd", mod_bwd, S, 128, [(du, D_MODEL, 0), (x, D_MODEL, 0), (dres, D_MODEL, 0)],
                                   [ada3], [(D_MODEL, F32)], [D_MODEL] * 2)
    d_ada = jnp.concatenate([d_shift, d_scale, d_gate], axis=1)
    return dx, G, d_ada


SMALL = ("conv_a_b", "ln_a_g", "ln_a_b", "q_norm_g", "kv_norm_g", "ln_g", "ln_b")


def _rows(v):
    n = v.shape[0]
    r = -(-n // (LANE * 16)) * 16
    return jnp.pad(v, (0, r * LANE - n)).reshape(r, LANE)


def kernel(x, c, positions, w_ada, b_ada, w_in, conv_a_w, conv_a_b, ln_a_g, ln_a_b, w_a_out, conv_b_w, w_b_out, q_norm_g, kv_norm_g, w_uq, w_ukv, w_c_out, w_o, ln_g, ln_b, loss_target, m_w_ada, m_b_ada, m_w_in, m_conv_a_w, m_conv_a_b, m_ln_a_g, m_ln_a_b, m_w_a_out, m_conv_b_w, m_w_b_out, m_q_norm_g, m_kv_norm_g, m_w_uq, m_w_ukv, m_w_c_out, m_w_o, m_ln_g, m_ln_b, v_w_ada, v_b_ada, v_w_in, v_conv_a_w, v_conv_a_b, v_ln_a_g, v_ln_a_b, v_w_a_out, v_conv_b_w, v_w_b_out, v_q_norm_g, v_kv_norm_g, v_w_uq, v_w_ukv, v_w_c_out, v_w_o, v_ln_g, v_ln_b):
    P = dict(w_ada=w_ada, b_ada=b_ada, w_in=w_in, conv_a_w=conv_a_w, conv_a_b=conv_a_b, ln_a_g=ln_a_g, ln_a_b=ln_a_b,
             w_a_out=w_a_out, conv_b_w=conv_b_w, w_b_out=w_b_out, q_norm_g=q_norm_g, kv_norm_g=kv_norm_g, w_uq=w_uq,
             w_ukv=w_ukv, w_c_out=w_c_out, w_o=w_o, ln_g=ln_g, ln_b=ln_b)
    Mo = dict(w_ada=m_w_ada, b_ada=m_b_ada, w_in=m_w_in, conv_a_w=m_conv_a_w, conv_a_b=m_conv_a_b, ln_a_g=m_ln_a_g,
              ln_a_b=m_ln_a_b, w_a_out=m_w_a_out, conv_b_w=m_conv_b_w, w_b_out=m_w_b_out, q_norm_g=m_q_norm_g,
              kv_norm_g=m_kv_norm_g, w_uq=m_w_uq, w_ukv=m_w_ukv, w_c_out=m_w_c_out, w_o=m_w_o, ln_g=m_ln_g, ln_b=m_ln_b)
    Vo = dict(w_ada=v_w_ada, b_ada=v_b_ada, w_in=v_w_in, conv_a_w=v_conv_a_w, conv_a_b=v_conv_a_b, ln_a_g=v_ln_a_g,
              ln_a_b=v_ln_a_b, w_a_out=v_w_a_out, conv_b_w=v_conv_b_w, w_b_out=v_w_b_out, q_norm_g=v_q_norm_g,
              kv_norm_g=v_kv_norm_g, w_uq=v_w_uq, w_ukv=v_w_ukv, w_c_out=v_w_c_out, w_o=v_w_o, ln_g=v_ln_g, ln_b=v_ln_b)
    ORDER = ("w_ada", "b_ada", "w_in", "conv_a_w", "conv_a_b", "ln_a_g", "ln_a_b", "w_a_out", "conv_b_w", "w_b_out",
             "q_norm_g", "kv_norm_g", "w_uq", "w_ukv", "w_c_out", "w_o", "ln_g", "ln_b")
    L = w_ada.shape[0]
    S = x.shape[1]
    me = 4 * lax.axis_index("x") + 2 * lax.axis_index("y") + lax.axis_index("c")
    x2 = x[0]
    tgt = loss_target[0]

    small_in = _rows(jnp.concatenate([c.reshape(-1), conv_a_w.reshape(-1), conv_b_w.reshape(-1)]))
    w_in_b = jnp.pad(w_in.astype(BF), ((0, 0), (0, 0), (0, W_IN_PAD - IN_SHARD)))
    misc_b = jnp.concatenate([w_a_out, w_b_out, w_c_out, w_ukv, jnp.pad(w_uq, ((0, 0), (0, 0), (0, LANE - UQ_SHARD)))],
                             axis=1).astype(BF)
    w_o_b = w_o.astype(BF)
    gathered = [None] * L
    gathered[0] = [gather_two_level("gather0_w_in", w_in_b[0])]
    pending_rest, rest_token = exchange_begin("gather0_rest", [misc_b[0], w_o_b[0]], 2, gathered[0][0])
    sg = exchange("gather_small", [small_in], [])[0]
    sgf = sg.reshape(NDEV, -1)
    c_all = sgf[:, :D_MODEL]
    o1 = D_MODEL + L * 31 * 64
    conv_a_full = sgf[:, D_MODEL:o1].reshape(NDEV, L, 31, 64).transpose(1, 2, 0, 3).reshape(L, 31, CW)
    conv_b_full = sgf[:, o1:o1 + L * 3 * 64].reshape(NDEV, L, 3, 64).transpose(1, 2, 0, 3).reshape(L, 3, CW)

    c_act = rowwise("silu_c", _silu, 16, 16, [(jnp.pad(c_all, ((0, 8), (0, 0))), D_MODEL, 0)], [], [(D_MODEL, BF)])[0]
    ncol = w_ada.shape[2]
    w_ada_b = w_ada.astype(BF).transpose(1, 0, 2).reshape(D_MODEL, L * ncol)
    b_mine = lax.dynamic_slice_in_dim(b_ada, me * ncol, ncol, axis=1).reshape(1, L * ncol)
    ada_part = mm(c_act, w_ada_b, name="mm_ada", bias=b_mine)
    ada_rows = -(-(L * ncol) // (LANE * 8)) * 8
    ada_send = jnp.pad(ada_part[:NDEV].reshape(NDEV, -1, LANE), ((0, 0), (0, ada_rows - L * ncol // LANE), (0, 0)))
    ada_recv = exchange("a2a_ada", [], [ada_send])[0]
    ada = ada_recv[:, :L * ncol // LANE].reshape(NDEV, L, ncol).transpose(1, 0, 2).reshape(L, 3, D_MODEL)

    inv_freq = ROPE_THETA ** (-jnp.arange(0, ROPE, 2, dtype=F32) / ROPE)
    ang = positions[0].astype(F32)[:, None] * inv_freq
    tabs = (jnp.tile(jnp.cos(ang), (1, 2 * LANE // ROPE)), jnp.tile(jnp.sin(ang), (1, 2 * LANE // ROPE)))

    straight = np.arange(D_MODEL)
    fwd_in = [(0, 0, 0, 0, D_MODEL, _to_gathered(IN_PERM, IN_SHARD, W_IN_PAD))]
    fwd_misc = [(0, ROW_A, 0, 0, CW, straight), (0, ROW_B, 1, 0, CW, straight), (0, ROW_C, 2, 0, CW, straight),
                (0, ROW_UKV, 3, 0, KVL, UKV_PERM), (0, ROW_UQ, 4, 0, QL, _to_gathered(UQ_PERM, UQ_SHARD, LANE))]
    rev_in = [(0, 0, 0, 0, D_MODEL, _from_full(_inverse(IN_PERM, D_IN), IN_SHARD, W_IN_PAD))]
    rev_misc = [(0, 0, 0, ROW_A, CW, straight), (1, 0, 0, ROW_B, CW, straight), (2, 0, 0, ROW_C, CW, straight),
                (3, 0, 0, ROW_UKV, KVL, _from_full(_inverse(UKV_PERM, HEADS * (NOPE + VH)), LANE, LANE)),
                (4, 0, 0, ROW_UQ, QL, _from_full(_inverse(UQ_PERM, HEADS * (NOPE + ROPE)), UQ_SHARD, LANE))]

    def layer_weights(l, deps):
        w_in_p = col_gather("relayout_w_in", [gathered[l][0]], [(D_MODEL, NP)], fwd_in, deps)[0]

        def late(after):
            if len(gathered[l]) == 1:
                gathered[l] += exchange_end(pending_rest, after)
            _, g_misc, g_o = gathered[l]
            a_out, b_out, c_out, ukv, uq = col_gather(
                "relayout_misc", [g_misc],
                [(CW, D_MODEL)] * 3 + [(KVL, HEADS * (NOPE + VH)), (QL, HEADS * (NOPE + ROPE))], fwd_misc, deps)
            return {"a_out": a_out, "b_out": b_out, "c_out": c_out, "uq": uq, "ukv": ukv,
                    "o": g_o.reshape(D_MODEL, D_MODEL)}

        return {
            "in": w_in_p, "late": late,
            "conv_a": jnp.pad(conv_a_full[l], ((0, 1), (0, 0))), "conv_b": jnp.pad(conv_b_full[l], ((0, 5), (0, 0))),
            "vec_a": jnp.stack([conv_a_b[l], ln_a_g[l], ln_a_b[l]]), "gq": q_norm_g[l][None], "gkv": kv_norm_g[l][None],
            "lnv": jnp.stack([ln_g[l], ln_b[l]]),
        }

    h = x2
    saved, weights = [], []
    for l in range(L):
        ada_l, deps = ada[l], ()
        if l + 1 < L:
            pending, token = exchange_begin("gather%d" % (l + 1), [w_in_b[l + 1], misc_b[l + 1], w_o_b[l + 1]], 3,
                                            rest_token if l == 0 else gathered[l][0])
            ada_l, deps = ada_l + token[0, 0], (token,)
        h, sv, Wl = layer_fwd(h, ada_l, layer_weights(l, deps), tabs, S)
        if l + 1 < L:
            gathered[l + 1] = exchange_end(pending, h)
        saved.append(sv)
        weights.append(Wl)

    def loss_fn(y, t):
        e = y - t
        return e * (1.0 / D_MODEL), _colsum(e * e)

    dy, sq = rowwise("loss", loss_fn, S, 256, [(h, D_MODEL, 0), (tgt, D_MODEL, 0)], [], [(D_MODEL, F32)], [D_MODEL])
    loss = lax.psum(0.5 * jnp.sum(sq) / D_MODEL, ("x", "y", "c"))

    grads, d_adas, recv = [None] * L, [None] * L, [None] * L
    pending, token = None, None

    def send_rest(g):
        send_misc = col_gather("unrelayout_misc", [g["w_a_out"], g["w_b_out"], g["w_c_out"], g["w_ukv"], g["w_uq"]],
                               [(NDEV, MISC_ROWS, LANE)], rev_misc)[0]
        return [send_misc, g["w_o"].reshape(NDEV, D_MODEL // NDEV, D_MODEL)]

    rest0 = []

    def early_rest(g):
        handle, tok = exchange_begin("scatter0_rest", send_rest(g), 0, g["w_o"])
        rest0.append(handle)
        return (tok,)

    for l in reversed(range(L)):
        ada_l = ada[l] if token is None else ada[l] + token[0, 0]
        dy, g, d_adas[l] = layer_bwd(dy, saved[l], ada_l, weights[l], tabs, S, early_rest if l == 0 else None)
        grads[l] = g
        if pending is not None:
            recv[l + 1] = exchange_end(pending, dy)
        send_in = col_gather("unrelayout_w_in", [g["w_in"]], [(NDEV, D_MODEL, W_IN_PAD)], rev_in)[0]
        if l == 0:
            small_parts = [jnp.stack([grads[i][n].reshape(-1) for i in range(L)]).reshape(-1) for n in SMALL]
            small_parts.append(jnp.stack([grads[i]["conv_a_w"][:31].reshape(-1) for i in range(L)]).reshape(-1))
            small_parts.append(jnp.stack([grads[i]["conv_b_w"][:3].reshape(-1) for i in range(L)]).reshape(-1))
            small_parts.append(jnp.stack([d_adas[i].reshape(-1) for i in range(L)]).reshape(-1))
            small_sizes = [int(p.shape[0]) for p in small_parts]
            gsmall = exchange("gather_small_grads", [_rows(jnp.concatenate(small_parts))], [])[0]
            pending, token = exchange_begin("scatter0", [send_in], 0, gsmall)
        else:
            pending, token = exchange_begin("scatter%d" % l, [send_in] + send_rest(g), 0,
                                            dy if l + 1 == L else recv[l + 1][0])
    grad_x = dy[None]

    gsmall = gsmall + token[0, 0]
    gsum = sum_slots("sum_small", gsmall).reshape(-1)
    recv[0] = [None] + exchange_end(rest0[0], gsum)
    Gr = {}
    offs = np.cumsum([0] + small_sizes)
    for i, n in enumerate(SMALL):
        Gr[n] = gsum[offs[i]:offs[i + 1]].reshape(L, -1)
    ca = gsum[offs[7]:offs[8]].reshape(L, 31, CW)
    cbw = gsum[offs[8]:offs[9]].reshape(L, 3, CW)
    Gr["conv_a_w"] = lax.dynamic_slice_in_dim(ca, me * 64, 64, axis=2)
    Gr["conv_b_w"] = lax.dynamic_slice_in_dim(cbw, me * 64, 64, axis=2)
    Gr["b_ada"] = gsum[offs[9]:offs[10]].reshape(L, 3 * D_MODEL)
    d_ada_all = gsmall.reshape(NDEV, -1)[:, offs[9]:offs[10]].reshape(NDEV, L, 3 * D_MODEL)
    d_mine = lax.dynamic_slice_in_dim(d_ada_all, me * ncol, ncol, axis=2).reshape(NDEV, L * ncol)
    g_ada = mm(c_act, jnp.pad(d_mine, ((0, 8), (0, 0))).astype(BF), name="mm_gw_ada", trans_a=True)
    Gr["w_ada"] = g_ada.reshape(D_MODEL, L, ncol).transpose(1, 0, 2)

    D, NM, NV = {}, {}, {}
    D["w_ada"], NM["w_ada"], NV["w_ada"] = adamw("adamw_w_ada", P["w_ada"], Gr["w_ada"], Mo["w_ada"], Vo["w_ada"])
    Gr["w_o"], D["w_o"], NM["w_o"], NV["w_o"] = sum_adamw(
        "sum_adamw_w_o", [recv[l][2] for l in range(L)], P["w_o"], Mo["w_o"], Vo["w_o"])
    misc = lambda T_: jnp.concatenate([T_["w_a_out"], T_["w_b_out"], T_["w_c_out"], T_["w_ukv"],
                                       jnp.pad(T_["w_uq"], ((0, 0), (0, 0), (0, LANE - UQ_SHARD)))], axis=1)
    res = sum_adamw("sum_adamw_misc", [recv[l][1] for l in range(L)], misc(P), misc(Mo), misc(Vo))
    for T_, r in zip((Gr, D, NM, NV), res):
        T_["w_a_out"], T_["w_b_out"], T_["w_c_out"] = r[:, ROW_A:ROW_B], r[:, ROW_B:ROW_C], r[:, ROW_C:ROW_UKV]
        T_["w_ukv"], T_["w_uq"] = r[:, ROW_UKV:ROW_UQ], r[:, ROW_UQ:MISC_ROWS, :UQ_SHARD]
    upper = sum_adamw("sum_adamw_w_in_upper", [recv[l][0] for l in range(1, L)], P["w_in"], Mo["w_in"], Vo["w_in"], lo=1)
    recv[0][0] = exchange_end(pending, upper[1])[0]
    Gr["w_in"], D["w_in"], NM["w_in"], NV["w_in"] = sum_adamw(
        "sum_adamw_w_in", [recv[0][0]], P["w_in"], Mo["w_in"], Vo["w_in"], lo=0, prev=upper)
    packed =("b_ada", "conv_a_w", "conv_b_w") + SMALL
    pk = lambda T_: _rows(jnp.concatenate([T_[n].reshape(-1) for n in packed]))[None]
    dS, mS, vS = adamw("adamw_small", pk(P), pk(Gr), pk(Mo), pk(Vo))
    o = 0
    for n in packed:
        sz = int(np.prod(P[n].shape))
        D[n] = dS.reshape(-1)[o:o + sz].reshape(P[n].shape)
        NM[n] = mS.reshape(-1)[o:o + sz].reshape(P[n].shape)
        NV[n] = vS.reshape(-1)[o:o + sz].reshape(P[n].shape)
        o += sz
    return (loss, grad_x, *[Gr[n] for n in ORDER], *[D[n] for n in ORDER], *[NM[n] for n in ORDER],
            *[NV[n] for n in ORDER])
```

```python
import functools
import math

import numpy as np
import jax
import jax.numpy as jnp
from jax import lax
from jax.experimental import pallas as pl
from jax.experimental.pallas import tpu as pltpu

BF = jnp.bfloat16
F32 = jnp.float32
MESH = pl.DeviceIdType.MESH
NDEV = 8

HEADS, NOPE, ROPE, VH = 8, 64, 32, 64
HP = 128
ROPE_THETA = 10000.0
LN_EPS = 1e-5
RMS_EPS = 1e-6
LR, B1, B2, EPS, WD, STEP = 0.001, 0.9, 0.999, 1e-08, 0.01, 10

LANE = 128
VMEM_LIMIT = 56 * 1024 * 1024

D_MODEL, CW, QL, KVL = 1024, 512, 384, 256
OFF_M, OFF_A, OFF_AG, OFF_B, OFF_CG, OFF_GB, OFF_BG = 0, 3072, 4096, 4608, 5632, 6144, 6656
OFF_KV, OFF_KR, OFF_Q, NP = 7168, 7424, 7680, 8192
D_IN = 7840


def _cparams(**kw):
    return pltpu.CompilerParams(vmem_limit_bytes=VMEM_LIMIT, **kw)


def _sigmoid(x):
    return jax.nn.sigmoid(x)


def _silu(x):
    return x * _sigmoid(x)


def _dsilu(x):
    s = _sigmoid(x)
    return s * (1.0 + x * (1.0 - s))


def _pick_tile(n, cap, mult):
    if n <= cap:
        return n
    for t in range(cap - cap % mult, 0, -mult):
        if n % t == 0:
            return t
    raise ValueError((n, cap, mult))


def mm(a, b, *, name, trans_a=False, trans_b=False, out_dtype=F32, bias=None, tm=1024, tn=1024, tk=2048, deps=()):
    if trans_a:
        K, M = a.shape
    else:
        M, K = a.shape
    if trans_b:
        N, K2 = b.shape
    else:
        K2, N = b.shape
    assert K == K2 and not (trans_a and trans_b), (a.shape, b.shape)
    tm, tn = _pick_tile(M, tm, 16), _pick_tile(N, tn, LANE)
    tk = _pick_tile(K, tk, LANE if trans_b else 16)
    assert M % tm == 0 and N % tn == 0 and K % tk == 0, (M, N, K, tm, tn, tk)
    nk = K // tk
    dims = (((0 if trans_a else 1,), (1 if trans_b else 0,)), ((), ()))
    has_bias = bias is not None

    def body(*refs):
        a_ref, b_ref = refs[0], refs[1]
        bias_ref = refs[2] if has_bias else None
        o_ref = refs[(3 if has_bias else 2) + len(deps)]
        p = lax.dot_general(a_ref[...], b_ref[...], dims, preferred_element_type=F32)

        def finish(v):
            if has_bias:
                v = v + bias_ref[...]
            o_ref[...] = v.astype(o_ref.dtype)

        if nk == 1:
            finish(p)
        else:
            acc = refs[-1]
            k = pl.program_id(2)

            @pl.when(k == 0)
            def _():
                acc[...] = p

            @pl.when(k > 0)
            def _():
                acc[...] += p

            @pl.when(k == nk - 1)
            def _():
                finish(acc[...])

    if trans_a:
        a_spec = pl.BlockSpec((tk, tm), lambda i, j, k: (k, i))
    else:
        a_spec = pl.BlockSpec((tm, tk), lambda i, j, k: (i, k))
    if trans_b:
        b_spec = pl.BlockSpec((tn, tk), lambda i, j, k: (j, k))
    else:
        b_spec = pl.BlockSpec((tk, tn), lambda i, j, k: (k, j))
    in_specs = [a_spec, b_spec]
    args = [a, b]
    if has_bias:
        in_specs.append(pl.BlockSpec((1, tn), lambda i, j, k: (0, j)))
        args.append(bias)
    in_specs += [ANY_SPEC] * len(deps)
    args += list(deps)
    return pl.pallas_call(
        body, name=name, grid=(M // tm, N // tn, nk),
        in_specs=in_specs, out_specs=pl.BlockSpec((tm, tn), lambda i, j, k: (i, j)),
        out_shape=jax.ShapeDtypeStruct((M, N), out_dtype),
        scratch_shapes=[pltpu.VMEM((tm, tn), F32)] if nk > 1 else [],
        compiler_params=_cparams(),
    )(*args)


def rowwise(name, fn, S, T, row_ins, full_ins, row_outs, acc_outs=(), into=None):
    n_in = len(row_ins) + len(full_ins)
    n_ro, n_ao = len(row_outs), len(acc_outs)
    alias = into is not None and into[0] is not None
    T = min(T, S)

    def body(*refs):
        vals = [r[...] for r in refs[:n_in]]
        vals = [v.astype(F32) if v.dtype == BF else v for v in vals]
        outs = fn(*vals)
        if not isinstance(outs, (tuple, list)):
            outs = (outs,)
        assert len(outs) == n_ro + n_ao, (name, len(outs))
        o0 = n_in + (1 if alias else 0)
        for r, v in zip(refs[o0:o0 + n_ro], outs[:n_ro]):
            r[...] = v.astype(r.dtype)
        first = pl.program_id(0) == 0
        for r, v in zip(refs[o0 + n_ro:], outs[n_ro:]):
            def init(r=r, v=v):
                r[...] = v

            def accum(r=r, v=v):
                r[...] += v

            pl.when(first)(init)
            pl.when(jnp.logical_not(first))(accum)

    in_specs, args = [], []
    for arr, W, off in row_ins:
        assert off % W == 0 and arr.shape[0] == S, (name, arr.shape, W, off)
        in_specs.append(pl.BlockSpec((T, W), functools.partial(lambda i, cb: (i, cb), cb=off // W)))
        args.append(arr)
    for arr in full_ins:
        in_specs.append(pl.BlockSpec(arr.shape, lambda i: (0, 0)))
        args.append(arr)
    out_specs = [pl.BlockSpec((T, W), lambda i: (i, 0)) for W, _ in row_outs]
    out_shape = [jax.ShapeDtypeStruct((S, W), dt) for W, dt in row_outs]
    aliases = {}
    if into is not None:
        buf, total, off = into
        W0, dt0 = row_outs[0]
        assert off % W0 == 0
        out_specs[0] = pl.BlockSpec((T, W0), functools.partial(lambda i, cb: (i, cb), cb=off // W0))
        out_shape[0] = jax.ShapeDtypeStruct((S, total), dt0)
        if alias:
            in_specs.append(ANY_SPEC)
            args.append(buf)
            aliases = {n_in: 0}
    out_specs += [pl.BlockSpec((1, W), lambda i: (0, 0)) for W in acc_outs]
    out_shape += [jax.ShapeDtypeStruct((1, W), F32) for W in acc_outs]
    return pl.pallas_call(
        body, name=name, grid=(S // T,), in_specs=in_specs, out_specs=out_specs, out_shape=out_shape,
        input_output_aliases=aliases, compiler_params=_cparams(),
    )(*args)


def _colsum(v):
    return jnp.sum(v, axis=0, keepdims=True)


def _ln_stats(r):
    mu = jnp.mean(r, axis=-1, keepdims=True)
    d = r - mu
    var = jnp.mean(d * d, axis=-1, keepdims=True)
    rstd = lax.rsqrt(var + LN_EPS)
    return d * rstd, rstd


def _ln_bwd(dn, n, rstd):
    return rstd * (dn - jnp.mean(dn, axis=-1, keepdims=True) - n * jnp.mean(dn * n, axis=-1, keepdims=True))


CPAD = 32
TC = 64


def _pre(mode, x1, x2):
    return x1 * _sigmoid(x2) if mode == "glu" else x1 * x2


def _shifted(ext, sft):
    n = TC + CPAD
    return pltpu.roll(ext, (n - sft) % n, 0)[0:TC]


def _interleaved_specs(S, off):
    return [pl.BlockSpec((S, LANE), functools.partial(lambda j, o: (0, o + 2 * j), o=off // LANE)),
            pl.BlockSpec((S, LANE), functools.partial(lambda j, o: (0, o + 2 * j + 1), o=off // LANE))]


def conv_fwd(name, src, off, w_pad, taps, mode, S, C):
    nchunk = S // TC

    def body(x1_ref, x2_ref, w_ref, o_ref, a_pad):
        a_pad[0:CPAD, :] = jnp.zeros((CPAD, LANE), F32)

        def fill(i, _):
            r = pl.multiple_of(i * 256, 256)
            a_pad[pl.ds(CPAD + r, 256), :] = _pre(mode, x1_ref[pl.ds(r, 256), :].astype(F32),
                                                  x2_ref[pl.ds(r, 256), :].astype(F32))
            return 0

        lax.fori_loop(0, S // 256, fill, 0)

        def chunk(i, _):
            base = pl.multiple_of(i * TC, TC)
            ext = a_pad[pl.ds(base, TC + CPAD), :]
            acc = jnp.zeros((TC, LANE), F32)
            for k in range(taps):
                acc = acc + w_ref[pl.ds(k, 1), :] * _shifted(ext, CPAD - (taps - 1) + k)
            o_ref[pl.ds(base, TC), :] = acc
            return 0

        lax.fori_loop(0, nchunk, chunk, 0)

    kp = w_pad.shape[0]
    return pl.pallas_call(
        body, name=name, grid=(C // LANE,),
        in_specs=_interleaved_specs(S, off) + [pl.BlockSpec((kp, LANE), lambda j: (0, j))],
        out_specs=pl.BlockSpec((S, LANE), lambda j: (0, j)),
        out_shape=jax.ShapeDtypeStruct((S, C), F32),
        scratch_shapes=[pltpu.VMEM((S + CPAD, LANE), F32)],
        compiler_params=_cparams(),
    )(src, src, w_pad)


def conv_bwd(name, src, off, dc, w_pad, taps, mode, S, C, buf):
    nchunk = S // TC
    kp = w_pad.shape[0]

    def body(x1_ref, x2_ref, dc_ref, w_ref, _, d_ref, dw_ref, a_pad, dc_pad, dw_acc):
        a_pad[0:CPAD, :] = jnp.zeros((CPAD, LANE), F32)
        dc_pad[S:S + CPAD, :] = jnp.zeros((CPAD, LANE), F32)
        dw_acc[...] = jnp.zeros(dw_acc.shape, F32)

        def fill(i, _):
            r = pl.multiple_of(i * 256, 256)
            a_pad[pl.ds(CPAD + r, 256), :] = _pre(mode, x1_ref[pl.ds(r, 256), :].astype(F32),
                                                  x2_ref[pl.ds(r, 256), :].astype(F32))
            dc_pad[pl.ds(r, 256), :] = dc_ref[pl.ds(r, 256), :]
            return 0

        lax.fori_loop(0, S // 256, fill, 0)

        def chunk(i, _):
            base = pl.multiple_of(i * TC, TC)
            ext_d = dc_pad[pl.ds(base, TC + CPAD), :]
            ext_a = a_pad[pl.ds(base, TC + CPAD), :]
            dcv = ext_d[0:TC]
            da = jnp.zeros((TC, LANE), F32)
            for k in range(taps):
                da = da + w_ref[pl.ds(k, 1), :] * _shifted(ext_d, taps - 1 - k)
                prod = dcv * _shifted(ext_a, CPAD - (taps - 1) + k)
                fold = prod[0:8]
                for g in range(1, TC // 8):
                    fold = fold + prod[8 * g:8 * g + 8]
                dw_acc[pl.ds(8 * k, 8), :] += fold
            x1 = x1_ref[pl.ds(base, TC), :].astype(F32)
            x2 = x2_ref[pl.ds(base, TC), :].astype(F32)
            if mode == "glu":
                s = _sigmoid(x2)
                d1, d2 = da * s, da * x1 * s * (1.0 - s)
            else:
                d1, d2 = da * x2, da * x1
            d_ref[pl.ds(base, TC), 0:LANE] = d1.astype(BF)
            d_ref[pl.ds(base, TC), LANE:2 * LANE] = d2.astype(BF)
            return 0

        lax.fori_loop(0, nchunk, chunk, 0)
        dw_ref[...] = jnp.zeros(dw_ref.shape, F32)
        for k in range(taps):
            dw_ref[pl.ds(k, 1), :] = jnp.sum(dw_acc[pl.ds(8 * k, 8), :], axis=0, keepdims=True)

    blk = pl.BlockSpec((S, LANE), lambda j: (0, j))
    return pl.pallas_call(
        body, name=name, grid=(C // LANE,),
        in_specs=_interleaved_specs(S, off) + [blk, pl.BlockSpec((kp, LANE), lambda j: (0, j)), ANY_SPEC],
        out_specs=[pl.BlockSpec((S, 2 * LANE), functools.partial(lambda j, o: (0, o + j), o=off // (2 * LANE))),
                   pl.BlockSpec((kp, LANE), lambda j: (0, j))],
        out_shape=[jax.ShapeDtypeStruct(buf.shape, BF), jax.ShapeDtypeStruct((kp, C), F32)],
        input_output_aliases={4: 0},
        scratch_shapes=[pltpu.VMEM((S + CPAD, LANE), F32), pltpu.VMEM((S + CPAD, LANE), F32),
                        pltpu.VMEM((8 * kp, LANE), F32)],
        compiler_params=_cparams(),
    )(src, src, dc, w_pad, buf)


FWD_TILES = (512, 512)
BWD_TILES = (512, 512)
QUADS = HEADS // 4
QW, KVW = 4 * (NOPE + ROPE), 4 * (NOPE + VH)
SCALE = (NOPE + ROPE) ** -0.5
NT_DIMS = (((1,), (1,)), ((), ()))
TN_DIMS = (((0,), (0,)), ((), ()))


def _lane_mask(width, group, dtype):
    lane = lax.broadcasted_iota(jnp.int32, (1, LANE), 1)
    return jnp.where(lane // width == group, 1.0, 0.0).astype(dtype)


def _visible(tq, tk, off):
    row = lax.broadcasted_iota(jnp.int32, (tq, tk), 0)
    col = lax.broadcasted_iota(jnp.int32, (tq, tk), 1)
    return col <= row + off


def _attn_tiles(S, tq, tk):
    tk = tk if S % tk == 0 else 256
    return min(tq, tk), tk


def attn_fwd(q, kv, kpe, S):
    tq, tk = _attn_tiles(S, *FWD_TILES)
    nq = S // tq

    def body(q_ref, kv_ref, kp_ref, o_ref, lse_ref):
        for t in range(2):
            cols = slice(t * LANE, (t + 1) * LANE)
            for hh in range(2):
                def q_block(qi, _, t=t, hh=hh, cols=cols):
                    r0 = pl.multiple_of(qi * tq, tq)
                    qcat = jnp.concatenate([q_ref[pl.ds(r0, tq), cols] * _lane_mask(NOPE, hh, BF),
                                            q_ref[pl.ds(r0, tq), 2 * LANE:3 * LANE] * _lane_mask(ROPE, 2 * t + hh, BF)],
                                           axis=1)
                    nfull = (qi * tq) // tk

                    def step(kj, carry, masked):
                        m, l, acc = carry
                        c0 = pl.multiple_of(kj * tk, tk)
                        kc = jnp.concatenate([kv_ref[pl.ds(c0, tk), cols], kp_ref[pl.ds(c0, tk), :]], axis=1)
                        vt = kv_ref[pl.ds(c0, tk), (2 + t) * LANE:(3 + t) * LANE]
                        s = lax.dot_general(qcat, kc, NT_DIMS, preferred_element_type=F32) * SCALE
                        if masked:
                            s = jnp.where(_visible(tq, tk, qi * tq - nfull * tk), s, -jnp.inf)
                        m_new = jnp.maximum(m, jnp.max(s, axis=-1, keepdims=True))
                        p = jnp.exp(s - m_new)
                        alpha = jnp.exp(m - m_new)
                        l = alpha * l + jnp.sum(p, axis=-1, keepdims=True)
                        acc = alpha * acc + jnp.dot(p.astype(BF), vt, preferred_element_type=F32)
                        return m_new, l, acc

                    init = (jnp.full((tq, 1), -jnp.inf, F32), jnp.zeros((tq, 1), F32), jnp.zeros((tq, LANE), F32))
                    carry = lax.fori_loop(0, nfull, lambda kj, c: step(kj, c, False), init)
                    m, l, acc = step(nfull, carry, True)
                    mine = _lane_mask(NOPE, hh, F32)
                    if hh == 0:
                        o_ref[pl.ds(r0, tq), cols] = (acc / l) * mine
                        lse_ref[pl.ds(r0, tq), cols] = (m + jnp.log(l)) * mine
                    else:
                        o_ref[pl.ds(r0, tq), cols] += (acc / l) * mine
                        lse_ref[pl.ds(r0, tq), cols] += (m + jnp.log(l)) * mine
                    return 0

                lax.fori_loop(0, nq, q_block, 0)

    return pl.pallas_call(
        body, name="attn_fwd", grid=(QUADS,),
        in_specs=[pl.BlockSpec((S, QW), lambda g: (0, g)), pl.BlockSpec((S, KVW), lambda g: (0, g)),
                  pl.BlockSpec((S, LANE), lambda g: (0, 0))],
        out_specs=[pl.BlockSpec((S, 2 * LANE), lambda g: (0, g))] * 2,
        out_shape=[jax.ShapeDtypeStruct((S, HEADS * VH), F32)] * 2,
        compiler_params=_cparams(),
    )(q, kv, kpe)


def attn_bwd(q, kv, kpe, o, lse, do, S):
    tq, tk = _attn_tiles(S, *BWD_TILES)
    nq = S // tq

    def body(q_ref, kv_ref, kp_ref, o_ref, lse_ref, do_ref, dq_ref, dkv_ref, dkp_ref, dq_acc, dk_acc, dv_acc):
        for t in range(2):
            cols = slice(t * LANE, (t + 1) * LANE)
            dk_acc[...] = jnp.zeros(dk_acc.shape, F32)
            dv_acc[...] = jnp.zeros(dv_acc.shape, F32)
            for hh in range(2):
                def q_block(qi, _, t=t, hh=hh, cols=cols):
                    r0 = pl.multiple_of(qi * tq, tq)
                    mine = _lane_mask(NOPE, hh, F32)
                    qcat = jnp.concatenate([q_ref[pl.ds(r0, tq), cols] * _lane_mask(NOPE, hh, BF),
                                            q_ref[pl.ds(r0, tq), 2 * LANE:3 * LANE] * _lane_mask(ROPE, 2 * t + hh, BF)],
                                           axis=1)
                    dof = do_ref[pl.ds(r0, tq), cols] * mine
                    dob = dof.astype(BF)
                    delta = jnp.sum(dof * o_ref[pl.ds(r0, tq), cols], axis=-1, keepdims=True)
                    lse_h = lse_ref[pl.ds(r0, tq), cols][:, hh * NOPE:hh * NOPE + 1]
                    nfull = (qi * tq) // tk
                    dq_acc[...] = jnp.zeros(dq_acc.shape, F32)

                    def step(kj, _, masked):
                        c0 = pl.multiple_of(kj * tk, tk)
                        kc = jnp.concatenate([kv_ref[pl.ds(c0, tk), cols], kp_ref[pl.ds(c0, tk), :]], axis=1)
                        vt = kv_ref[pl.ds(c0, tk), (2 + t) * LANE:(3 + t) * LANE]
                        s = lax.dot_general(qcat, kc, NT_DIMS, preferred_element_type=F32) * SCALE
                        if masked:
                            s = jnp.where(_visible(tq, tk, qi * tq - nfull * tk), s, -jnp.inf)
                        p = jnp.exp(s - lse_h)
                        dp = lax.dot_general(dob, vt, NT_DIMS, preferred_element_type=F32)
                        ds = (p * (dp - delta) * SCALE).astype(BF)
                        dv_acc[pl.ds(c0, tk), :] += lax.dot_general(p.astype(BF), dob, TN_DIMS,
                                                                    preferred_element_type=F32)
                        dk_acc[pl.ds(c0, tk), :] += lax.dot_general(ds, qcat, TN_DIMS, preferred_element_type=F32)
                        dq_acc[...] += jnp.dot(ds, kc, preferred_element_type=F32)
                        return 0

                    lax.fori_loop(0, nfull, lambda kj, c: step(kj, c, False), 0)
                    step(nfull, 0, True)
                    d = dq_acc[...]
                    pe = d[:, LANE:] * _lane_mask(ROPE, 2 * t + hh, F32)
                    if hh == 0:
                        dq_ref[pl.ds(r0, tq), cols] = d[:, :LANE] * mine
                    else:
                        dq_ref[pl.ds(r0, tq), cols] += d[:, :LANE] * mine
                    if t == 0 and hh == 0:
                        dq_ref[pl.ds(r0, tq), 2 * LANE:3 * LANE] = pe
                    else:
                        dq_ref[pl.ds(r0, tq), 2 * LANE:3 * LANE] += pe
                    return 0

                lax.fori_loop(0, nq, q_block, 0)
            dkv_ref[:, t * LANE:(t + 1) * LANE] = dk_acc[:, :LANE].astype(BF)
            dkv_ref[:, (2 + t) * LANE:(3 + t) * LANE] = dv_acc[...].astype(BF)
            if t == 0:
                dkp_ref[...] = dk_acc[:, LANE:]
            else:
                dkp_ref[...] += dk_acc[:, LANE:]

    qspec = pl.BlockSpec((S, QW), lambda g: (0, g))
    kvspec = pl.BlockSpec((S, KVW), lambda g: (0, g))
    ospec = pl.BlockSpec((S, 2 * LANE), lambda g: (0, g))
    return pl.pallas_call(
        body, name="attn_bwd", grid=(QUADS,),
        in_specs=[qspec, kvspec, pl.BlockSpec((S, LANE), lambda g: (0, 0)), ospec, ospec, ospec],
        out_specs=[qspec, kvspec, pl.BlockSpec((S, LANE), lambda g: (0, g))],
        out_shape=[jax.ShapeDtypeStruct((S, HEADS * (NOPE + ROPE)), F32), jax.ShapeDtypeStruct((S, HEADS * (NOPE + VH)), BF),
                   jax.ShapeDtypeStruct((S, HEADS * ROPE), F32)],
        scratch_shapes=[pltpu.VMEM((tq, 2 * LANE), F32), pltpu.VMEM((S, 2 * LANE), F32), pltpu.VMEM((S, LANE), F32)],
        compiler_params=_cparams(),
    )(q, kv, kpe, o, lse, do)


def exchange(name, gathers, a2as):
    n_g, n = len(gathers), len(gathers) + len(a2as)

    def body(*refs):
        ins, outs = refs[:n], refs[n:2 * n]
        send_sems, recv_sems, loc_sems = refs[2 * n:]
        x, y, c = lax.axis_index("x"), lax.axis_index("y"), lax.axis_index("c")
        me = 4 * x + 2 * y + c

        def peer(k):
            px = 1 - x if k & 4 else x
            py = 1 - y if k & 2 else y
            pc = 1 - c if k & 1 else c
            return (px, py, pc), 4 * px + 2 * py + pc

        def remote(a, k):
            pid, pflat = peer(k)
            src = ins[a] if a < n_g else ins[a].at[pflat]
            return pltpu.make_async_remote_copy(
                src_ref=src, dst_ref=outs[a].at[me], send_sem=send_sems.at[a, k - 1], recv_sem=recv_sems.at[a, k - 1],
                device_id=pid, device_id_type=MESH)

        def arrival(a, k):
            pid, pflat = peer(k)
            src = ins[a] if a < n_g else ins[a].at[pflat]
            return pltpu.make_async_remote_copy(
                src_ref=src, dst_ref=outs[a].at[pflat], send_sem=send_sems.at[a, k - 1], recv_sem=recv_sems.at[a, k - 1],
                device_id=pid, device_id_type=MESH)

        local = []
        for a in range(n):
            own = ins[a] if a < n_g else ins[a].at[me]
            cp = pltpu.make_async_copy(own, outs[a].at[me], loc_sems.at[a])
            cp.start()
            local.append(cp)
        sent = []
        for k in (1, 2, 4, 3, 5, 6, 7):
            for a in range(n):
                cp = remote(a, k)
                cp.start()
                sent.append(cp)
        for k in range(1, 8):
            for a in range(n):
                arrival(a, k).wait_recv()
        for cp in sent:
            cp.wait_send()
        for cp in local:
            cp.wait()

    out_shape = [jax.ShapeDtypeStruct((NDEV,) + g.shape, g.dtype) for g in gathers]
    out_shape += [jax.ShapeDtypeStruct(a.shape, a.dtype) for a in a2as]
    any_spec = pl.BlockSpec(memory_space=pl.ANY)
    return pl.pallas_call(
        body, name=name, in_specs=[any_spec] * n, out_specs=[any_spec] * n, out_shape=out_shape,
        scratch_shapes=[pltpu.SemaphoreType.DMA((n, NDEV - 1)), pltpu.SemaphoreType.DMA((n, NDEV - 1)),
                        pltpu.SemaphoreType.DMA((n,))],
    )(*gathers, *a2as)


def gather_two_level(name, block):
    def body(x_ref, out_ref, stage, send_sems, recv_sems, loc_sem):
        x, y, c = lax.axis_index("x"), lax.axis_index("y"), lax.axis_index("c")
        me, sibling = (x, y, c), (x, y, 1 - c)
        chips = [(1 - x, y), (x, 1 - y), (1 - x, 1 - y)]

        def slot(px, py, pc):
            return out_ref.at[4 * px + 2 * py + pc]

        def copy(k, owner, to, src=None):
            return pltpu.make_async_remote_copy(
                src_ref=slot(*owner) if src is None else src, dst_ref=slot(*owner), send_sem=send_sems.at[k],
                recv_sem=recv_sems.at[k], device_id=to, device_id_type=MESH)

        load = pltpu.make_async_copy(x_ref, stage, loc_sem)
        load.start()
        first = [copy(0, me, sibling, src=x_ref)] + [copy(1 + j, me, (*chip, c), src=x_ref) for j, chip in enumerate(chips)]
        for cp in first:
            cp.start()
        load.wait()
        store = pltpu.make_async_copy(stage, slot(*me), loc_sem)
        store.start()
        passed = [copy(4 + j, (*chip, c), sibling) for j, chip in enumerate(chips)]
        for j, chip in enumerate(chips):
            copy(1 + j, (*chip, c), me).wait_recv()
            passed[j].start()
        copy(0, sibling, me).wait_recv()
        for j, chip in enumerate(chips):
            copy(4 + j, (*chip, 1 - c), me).wait_recv()
        for cp in first + passed:
            cp.wait_send()
        store.wait()

    return pl.pallas_call(
        body, name=name, in_specs=[pl.BlockSpec(memory_space=pl.ANY)], out_specs=pl.BlockSpec(memory_space=pl.ANY),
        out_shape=jax.ShapeDtypeStruct((NDEV,) + block.shape, block.dtype),
        scratch_shapes=[pltpu.VMEM(block.shape, block.dtype), pltpu.SemaphoreType.DMA((NDEV - 1,)),
                        pltpu.SemaphoreType.DMA((NDEV - 1,)), pltpu.SemaphoreType.DMA],
        compiler_params=_cparams(),
    )(block)


def _peer(k, x, y, c):
    px = 1 - x if k & 4 else x
    py = 1 - y if k & 2 else y
    pc = 1 - c if k & 1 else c
    return (px, py, pc), 4 * px + 2 * py + pc


PEER_ORDER = (1, 2, 4, 3, 5, 6, 7)
HBM_SPEC = pl.BlockSpec(memory_space=pltpu.HBM)
SEM_SPEC = pl.BlockSpec(memory_space=pltpu.SEMAPHORE)
ANY_SPEC = pl.BlockSpec(memory_space=pl.ANY)


def _split_copies(ins, lands, n_g, send_sems, recv_sems):
    x, y, c = lax.axis_index("x"), lax.axis_index("y"), lax.axis_index("c")
    me = 4 * x + 2 * y + c

    def outgoing(a, k):
        pid, pflat = _peer(k, x, y, c)
        src = ins[a] if a < n_g else ins[a].at[pflat]
        return pltpu.make_async_remote_copy(
            src_ref=src, dst_ref=lands[a].at[me], send_sem=send_sems.at[a * (NDEV - 1) + k - 1],
            recv_sem=recv_sems.at[a * (NDEV - 1) + k - 1],
            device_id=pid, device_id_type=MESH)

    def arrival(a, k):
        pid, pflat = _peer(k, x, y, c)
        src = ins[a] if a < n_g else ins[a].at[pflat]
        return pltpu.make_async_remote_copy(
            src_ref=src, dst_ref=lands[a].at[pflat], send_sem=send_sems.at[a * (NDEV - 1) + k - 1],
            recv_sem=recv_sems.at[a * (NDEV - 1) + k - 1],
            device_id=pid, device_id_type=MESH)

    return outgoing, arrival


def exchange_begin(name, srcs, n_g, dep):
    n = len(srcs)
    land_shapes = [((NDEV,) + s.shape) if a < n_g else s.shape for a, s in enumerate(srcs)]

    def own_body(*refs):
        ins, outs = refs[:n], refs[n + 1:2 * n + 1]
        stage, sems = refs[2 * n + 1:3 * n + 1], refs[-1]
        me = 4 * lax.axis_index("x") + 2 * lax.axis_index("y") + lax.axis_index("c")
        cps = [pltpu.make_async_copy(ins[a] if a < n_g else ins[a].at[me], stage[a], sems.at[a]) for a in range(n)]
        for cp in cps:
            cp.start()
        for cp in cps:
            cp.wait()
        cps = [pltpu.make_async_copy(stage[a], outs[a].at[me], sems.at[a]) for a in range(n)]
        for cp in cps:
            cp.start()
        for cp in cps:
            cp.wait()

    lands = pl.pallas_call(
        own_body, name=name + "_own", in_specs=[ANY_SPEC] * (n + 1), out_specs=[ANY_SPEC] * n,
        out_shape=[jax.ShapeDtypeStruct(sh, s.dtype) for sh, s in zip(land_shapes, srcs)],
        scratch_shapes=[pltpu.VMEM(sh[1:], s.dtype) for sh, s in zip(land_shapes, srcs)] + [pltpu.SemaphoreType.DMA((n,))],
        compiler_params=_cparams(),
    )(*srcs, dep)

    def start_body(*refs):
        ins, lz = refs[:n], refs[n:2 * n]
        send_sems, recv_sems, token = refs[2 * n], refs[2 * n + 1], refs[-1]
        outgoing, _ = _split_copies(ins, lz, n_g, send_sems, recv_sems)
        for k in PEER_ORDER:
            for a in range(n):
                outgoing(a, k).start()
        token[...] = jnp.zeros(token.shape, F32)

    hbm = lambda t: pltpu.HBM(t.shape, t.dtype)
    res = pl.pallas_call(
        start_body, name=name + "_start",
        out_shape=(pltpu.SemaphoreType.DMA((n * (NDEV - 1),)), pltpu.SemaphoreType.DMA((n * (NDEV - 1),)),
                   *[hbm(s) for s in srcs], *[hbm(t) for t in lands], jax.ShapeDtypeStruct((8, LANE), F32)),
        in_specs=[HBM_SPEC] * (2 * n),
        out_specs=(SEM_SPEC, SEM_SPEC, *[HBM_SPEC] * (2 * n), pl.BlockSpec(memory_space=pltpu.VMEM)),
        input_output_aliases={i: 2 + i for i in range(2 * n)},
        compiler_params=pltpu.CompilerParams(has_side_effects=pltpu.SideEffectType.DATAFLOW_SIDE_EFFECTING),
    )(*[pltpu.with_memory_space_constraint(t, pltpu.HBM) for t in list(srcs) + list(lands)])
    return (name, n, n_g, res[:-1]), res[-1]


def exchange_end(handle, after):
    name, n, n_g, (send_sems, recv_sems, *bufs) = handle

    def wait_body(*refs):
        ins, lz = refs[:n], refs[n:2 * n]
        ss, rs = refs[2 * n], refs[2 * n + 1]
        outgoing, arrival = _split_copies(ins, lz, n_g, ss, rs)
        for k in range(1, NDEV):
            for a in range(n):
                arrival(a, k).wait_recv()
        for k in range(1, NDEV):
            for a in range(n):
                outgoing(a, k).wait_send()

    res = pl.pallas_call(
        wait_body, name=name + "_wait", out_shape=tuple(pltpu.HBM(t.shape, t.dtype) for t in bufs),
        in_specs=[HBM_SPEC] * (2 * n) + [SEM_SPEC, SEM_SPEC, ANY_SPEC], out_specs=[HBM_SPEC] * (2 * n),
        input_output_aliases={i: i for i in range(2 * n)},
        compiler_params=pltpu.CompilerParams(has_side_effects=pltpu.SideEffectType.DATAFLOW_SIDE_EFFECTING),
    )(*bufs, send_sems, recv_sems, after)
    return list(res[n:])


def _pick_rows(R, mult, cap):
    best = None
    for n in range(1, R + 1):
        if R % n == 0 and (R // n) % mult == 0 and R // n <= cap:
            best = R // n
            break
    assert best is not None, (R, mult, cap)
    return best


def sum_slots(name, x):
    _, R, _ = x.shape
    tr = _pick_rows(R, 16, 2304)

    def body(x_ref, o_ref):
        acc = x_ref[0].astype(F32)
        for d in range(1, NDEV):
            acc = acc + x_ref[d].astype(F32)
        o_ref[...] = acc

    return pl.pallas_call(
        body, name=name, grid=(R // tr,),
        in_specs=[pl.BlockSpec((NDEV, tr, LANE), lambda i: (0, i, 0))],
        out_specs=pl.BlockSpec((tr, LANE), lambda i: (i, 0)),
        out_shape=jax.ShapeDtypeStruct((R, LANE), F32), compiler_params=_cparams(),
    )(x)


def adamw(name, w, g, m, v):
    L, R, C = w.shape
    tr = _pick_rows(R, 8, 256) if R % 8 == 0 else R

    def body(w_ref, g_ref, m_ref, v_ref, d_ref, nm_ref, nv_ref):
        gg = g_ref[...]
        nm = B1 * m_ref[...] + (1.0 - B1) * gg
        nv = B2 * v_ref[...] + (1.0 - B2) * jnp.square(gg)
        m_hat = nm / (1.0 - B1 ** STEP)
        v_hat = nv / (1.0 - B2 ** STEP)
        d_ref[...] = -LR * (m_hat / (jnp.sqrt(v_hat) + EPS) + WD * w_ref[...])
        nm_ref[...] = nm
        nv_ref[...] = nv

    blk = pl.BlockSpec((1, tr, C), lambda l, i: (l, i, 0))
    shp = jax.ShapeDtypeStruct(w.shape, F32)
    return pl.pallas_call(
        body, name=name, grid=(L, R // tr), in_specs=[blk] * 4, out_specs=[blk] * 3, out_shape=[shp] * 3,
        compiler_params=_cparams(),
    )(w, g, m, v)


IN_SHARD = D_IN // NDEV
UQ_SHARD = HEADS * (NOPE + ROPE) // NDEV
W_IN_PAD = 1024
ROW_A, ROW_B, ROW_C, ROW_UKV, ROW_UQ, MISC_ROWS = 0, 512, 1024, 1536, 1792, 2176


def _in_perm_index():
    ar = np.arange
    z = lambda n: np.full((n,), -1, np.int64)
    mix = lambda lo1, lo2: np.concatenate([ar(lo + LANE * j, lo + LANE * (j + 1)) for j in range(CW // LANE)
                                           for lo in (lo1, lo2)])
    return np.concatenate([ar(4768, 7840), mix(0, 512), ar(1024, 1536), mix(1536, 2560), ar(4256, 4768), ar(2048, 2560),
                           ar(3072, 3584), ar(3968, 4224), ar(4224, 4256), z(OFF_Q - OFF_KR - ROPE), ar(3584, 3968),
                           z(NP - OFF_Q - QL)])


def _head_perm_index(a, b):
    parts = []
    for g in range(QUADS):
        h = np.arange(4 * g, 4 * g + 4)[:, None] * (a + b)
        parts += [(h + np.arange(a)[None]).reshape(-1), (h + a + np.arange(b)[None]).reshape(-1)]
    return np.concatenate(parts)


def _inverse(perm, n):
    inv = np.full((n,), -1, np.int64)
    inv[perm[perm >= 0]] = np.nonzero(perm >= 0)[0]
    return inv


IN_PERM = _in_perm_index()
UQ_PERM = _head_perm_index(NOPE, ROPE)
UKV_PERM = _head_perm_index(NOPE, VH)


def _to_gathered(perm, shard, pad):
    return np.where(perm >= 0, (perm // shard) * pad + perm % shard, -1)


def _from_full(inv, shard, pad):
    j, i = np.divmod(np.arange(NDEV * pad), pad)
    return np.where(i < shard, inv[np.minimum(j * shard + i, inv.shape[0] - 1)], -1)


def col_gather(name, srcs, out_shapes, jobs, deps=()):
    ns, nj, nd, no = len(srcs), len(jobs), len(deps), len(out_shapes)
    tables = [jnp.asarray(np.asarray(job[5], np.int32)[None, :]) for job in jobs]

    def view(ref, col0, width, r0, rc):
        n = ref.shape[-1]
        if len(ref.shape) == 3:
            return ref.at[col0 // n, pl.ds(r0, rc), pl.ds(col0 % n, width)]
        return ref.at[pl.ds(r0, rc), pl.ds(col0, width)]

    def slabs(shape):
        if len(shape) == 3:
            return [((d,), d * shape[2], (d + 1) * shape[2]) for d in range(shape[0])]
        w = 1024 if shape[1] > 1024 and shape[1] % 1024 == 0 else shape[1]
        return [((slice(None), pl.ds(c, w)), c, c + w) for c in range(0, shape[1], w)]

    src_slabs = [slabs(s.shape) for s in srcs]
    out_slabs = [slabs(sh) for sh in out_shapes]
    work, first_use, last_touch = [], {}, {}
    for ji, (si, srow, oi, orow, nrows, tgt) in enumerate(jobs):
        tgt = np.asarray(tgt)
        tw = 256 if out_shapes[oi][-1] % 256 == 0 else LANE
        sw = 256 if srcs[si].shape[-1] % 256 == 0 else LANE
        for t in range(tgt.shape[0] // tw):
            tt = tgt[t * tw:(t + 1) * tw]
            tiles = sorted(set((tt[tt >= 0] // sw).tolist()))
            straight = bool(tiles) and tt[0] >= 0 and tt[0] % LANE == 0 and np.array_equal(tt, tt[0] + np.arange(tw))
            cols = [(int(tt[0]) + k * LANE, LANE) for k in range(tw // LANE)] if straight else [(s * sw, sw) for s in tiles]
            need = sorted({(si, k) for c0, _ in cols for k, (_, lo, hi) in enumerate(src_slabs[si]) if lo <= c0 < hi})
            touch = [(oi, k) for k, (_, lo, hi) in enumerate(out_slabs[oi]) if lo <= t * tw < hi][0]
            for key in need:
                first_use.setdefault(key, len(work))
            last_touch[touch] = len(work)
            work.append((ji, t, tw, sw, tiles, straight, need, touch))
    in_order = sorted(first_use, key=first_use.get)
    in_sem = {key: i for i, key in enumerate(in_order)}
    out_keys = sorted(last_touch)
    out_sem = {key: i for i, key in enumerate(out_keys)}

    def body(*refs):
        src_hbm, tab_refs = refs[:ns], refs[ns:ns + nj]
        out_hbm = refs[ns + nj + nd:ns + nj + nd + no]
        scratch = refs[ns + nj + nd + no:]
        src_refs, out_refs, in_sems, out_sems = scratch[:ns], scratch[ns:ns + no], scratch[-2], scratch[-1]
        loads = {}
        for key in in_order:
            si, k = key
            idx = src_slabs[si][k][0]
            loads[key] = pltpu.make_async_copy(src_hbm[si].at[idx], src_refs[si].at[idx], in_sems.at[in_sem[key]])
            loads[key].start()
        arrived, stores = set(), []
        for wi, (ji, t, tw, sw, tiles, straight, need, touch) in enumerate(work):
            si, srow, oi, orow, nrows, tgt = jobs[ji]
            sref, oref = src_refs[si], out_refs[oi]
            rc = nrows if nrows <= 1024 else 1024
            for key in need:
                if key not in arrived:
                    loads[key].wait()
                    arrived.add(key)
            onehots = []
            if tiles and not straight:
                want = tab_refs[ji][:, t * tw:(t + 1) * tw]
                row = lax.broadcasted_iota(jnp.int32, (sw, tw), 0)
                onehots = [jnp.where(want == row + s * sw, 1.0, 0.0).astype(BF) for s in tiles]
            first = int(np.asarray(tgt)[t * tw])

            def chunk(ci, _, t=t, tw=tw, sw=sw, tiles=tiles, straight=straight, onehots=onehots, first=first,
                      sref=sref, oref=oref, srow=srow, orow=orow, rc=rc):
                r0 = ci * rc
                ro = pl.multiple_of(orow + r0, LANE)
                rs = pl.multiple_of(srow + r0, LANE)
                if not tiles:
                    view(oref, t * tw, tw, ro, rc)[...] = jnp.zeros((rc, tw), BF)
                elif straight:
                    for k in range(tw // LANE):
                        view(oref, t * tw + k * LANE, LANE, ro, rc)[...] = view(sref, first + k * LANE, LANE, rs, rc)[...]
                else:
                    acc = None
                    for s, oh in zip(tiles, onehots):
                        p = jnp.dot(view(sref, s * sw, sw, rs, rc)[...], oh, preferred_element_type=F32)
                        acc = p if acc is None else acc + p
                    view(oref, t * tw, tw, ro, rc)[...] = acc.astype(BF)
                return 0

            lax.fori_loop(0, nrows // rc, chunk, 0)
            if last_touch[touch] == wi:
                idx = out_slabs[touch[0]][touch[1]][0]
                cp = pltpu.make_async_copy(out_refs[touch[0]].at[idx], out_hbm[touch[0]].at[idx], out_sems.at[out_sem[touch]])
                cp.start()
                stores.append(cp)
        for cp in stores:
            cp.wait()

    return pl.pallas_call(
        body, name=name, in_specs=[ANY_SPEC] * ns + [pl.BlockSpec(memory_space=pltpu.VMEM)] * nj + [ANY_SPEC] * nd,
        out_specs=[ANY_SPEC] * no, out_shape=[jax.ShapeDtypeStruct(s, BF) for s in out_shapes],
        scratch_shapes=[pltpu.VMEM(s.shape, BF) for s in srcs] + [pltpu.VMEM(s, BF) for s in out_shapes]
        + [pltpu.SemaphoreType.DMA((len(in_order),)), pltpu.SemaphoreType.DMA((len(out_keys),))],
        compiler_params=_cparams(),
    )(*srcs, *tables, *deps)


def sum_adamw(name, recvs, w, m, v, lo=0, prev=None, row0=0):
    _, R, C = w.shape
    L = len(recvs)
    CP = recvs[0].shape[-1]
    tr = _pick_rows(R, 16, 128)
    n_prev = 0 if prev is None else 4

    def body(*refs):
        r_refs = refs[:L]
        w_ref, m_ref, v_ref = refs[L:L + 3]
        g_ref, d_ref, nm_ref, nv_ref, gsum = refs[L + 3 + n_prev:]
        layer = pl.program_id(0)
        for k in range(L):
            def total(k=k):
                acc = r_refs[k][0].astype(F32)
                for d in range(1, NDEV):
                    acc = acc + r_refs[k][d].astype(F32)
                gsum[...] = acc
            pl.when(layer == k)(total)
        gg = gsum[:, 0:C]
        nm = B1 * m_ref[...] + (1.0 - B1) * gg
        nv = B2 * v_ref[...] + (1.0 - B2) * jnp.square(gg)
        m_hat = nm / (1.0 - B1 ** STEP)
        v_hat = nv / (1.0 - B2 ** STEP)
        g_ref[...] = gg
        d_ref[...] = -LR * (m_hat / (jnp.sqrt(v_hat) + EPS) + WD * w_ref[...])
        nm_ref[...] = nm
        nv_ref[...] = nv

    assert row0 % tr == 0
    r_specs = [pl.BlockSpec((NDEV, tr, CP),
                            functools.partial(lambda l, i, k: (0, row0 // tr + jnp.where(l == k, i, 0), 0), k=k))
               for k in range(L)]
    blk = pl.BlockSpec((None, tr, C), lambda l, i: (l + lo, i, 0))
    shp = jax.ShapeDtypeStruct(w.shape, F32)
    return pl.pallas_call(
        body, name=name, grid=(L, R // tr), in_specs=r_specs + [blk] * 3 + [ANY_SPEC] * n_prev, out_specs=[blk] * 4,
        out_shape=[shp] * 4, input_output_aliases={L + 3 + i: i for i in range(n_prev)},
        scratch_shapes=[pltpu.VMEM((tr, CP), F32)], compiler_params=_cparams(),
    )(*recvs, w, m, v, *(prev or ()))


ALPHA = 8.0 ** 0.25
T_WIDE, T_NARROW = 256, 512


def _rope_fn(sign):
    def fn(x, cos, sin):
        W = x.shape[-1]
        lane = lax.broadcasted_iota(jnp.int32, x.shape, 1)
        first_half = (lane % ROPE) < (ROPE // 2)
        rot = jnp.where(first_half, -pltpu.roll(x, W - ROPE // 2, 1), pltpu.roll(x, ROPE // 2, 1))
        return x * cos + sign * rot * sin
    return fn


def layer_fwd(x, ada3, W, tabs, S):
    cos, sin = tabs
    T = T_NARROW
    u = rowwise("modulate", lambda xv, a: xv * (1.0 + a[1:2, :]) + a[0:1, :], S, T,
                [(x, D_MODEL, 0)], [ada3], [(D_MODEL, BF)])[0]
    proj = mm(u, W["in"], name="mm_proj", tm=1024, tn=1024, out_dtype=BF)
    W = {**W, **W["late"](proj)}

    ca = conv_fwd("conv_a_fwd", proj, OFF_A, W["conv_a"], 31, "glu", S, CW)

    def a_post(c, ag, vec):
        n, _ = _ln_stats(c + vec[0:1, :])
        return _silu(n * vec[1:2, :] + vec[2:3, :]) * _silu(ag)

    h_a = rowwise("mix_a_post", a_post, S, T, [(ca, CW, 0), (proj, CW, OFF_AG)], [W["vec_a"]], [(CW, BF)])[0]
    y_a = mm(h_a, W["a_out"], name="mm_branch_out", out_dtype=BF)

    cb = conv_fwd("conv_b_fwd", proj, OFF_B, W["conv_b"], 3, "mul", S, CW)
    h_b = rowwise("mix_b_post", lambda c, gb, bg: gb * c * _silu(bg), S, T,
                  [(cb, CW, 0), (proj, CW, OFF_GB), (proj, CW, OFF_BG)], [], [(CW, BF)])[0]
    y_b = mm(h_b, W["b_out"], name="mm_branch_out", out_dtype=BF)

    def rms2(ql, kvl, gq, gkv):
        rq = lax.rsqrt(jnp.mean(ql * ql, axis=-1, keepdims=True) + RMS_EPS)
        rk = lax.rsqrt(jnp.mean(kvl * kvl, axis=-1, keepdims=True) + RMS_EPS)
        return ql * rq * gq, kvl * rk * gkv

    qn, kvn = rowwise("rms_fwd", rms2, S, T, [(proj, QL, OFF_Q), (proj, KVL, OFF_KV)], [W["gq"], W["gkv"]],
                      [(QL, BF), (KVL, BF)])
    q = mm(qn, W["uq"], name="mm_q")
    kv = mm(kvn, W["ukv"], name="mm_kv", out_dtype=BF)
    rope = _rope_fn(1.0)

    def rope_fwd(qv, kr, c1, s1):
        parts = []
        for g in range(QUADS):
            parts.append(qv[:, g * QW:g * QW + 2 * LANE].astype(BF))
            parts.append(rope(qv[:, g * QW + 2 * LANE:(g + 1) * QW], c1, s1).astype(BF))
        kp = rope(kr, c1, s1)
        kp = kp + pltpu.roll(kp, ROPE, 1) + pltpu.roll(kp, 2 * ROPE, 1) + pltpu.roll(kp, 3 * ROPE, 1)
        return jnp.concatenate(parts, axis=1), kp

    q_b, kpe = rowwise("rope_fwd", rope_fwd, S, T,
                       [(q, HEADS * (NOPE + ROPE), 0), (proj, LANE, OFF_KR), (cos, LANE, 0), (sin, LANE, 0)], [],
                       [(HEADS * (NOPE + ROPE), BF), (LANE, BF)])
    o, lse = attn_fwd(q_b, kv, kpe, S)
    h_c = rowwise("mix_c_post", lambda ov, cg: ov * _silu(cg), S, T, [(o, CW, 0), (proj, CW, OFF_CG)], [],
                  [(CW, BF)])[0]
    y_c = mm(h_c, W["c_out"], name="mm_branch_out", out_dtype=BF)

    def merge(la, lb, lc, ya, yb, yc):
        return _sigmoid(la) * ya + _sigmoid(lb) * yb + _sigmoid(lc) * yc

    m = rowwise("merge_fwd", merge, S, T_WIDE,
                [(proj, D_MODEL, 0), (proj, D_MODEL, 1024), (proj, D_MODEL, 2048), (y_a, D_MODEL, 0),
                 (y_b, D_MODEL, 0), (y_c, D_MODEL, 0)], [], [(D_MODEL, BF)])[0]
    out = mm(m, W["o"], name="mm_out")

    def ln_fwd(xv, ov, a, lnv):
        n, _ = _ln_stats(ALPHA * xv + a[2:3, :] * ov)
        return n * lnv[0:1, :] + lnv[1:2, :]

    x_next = rowwise("ln_fwd", ln_fwd, S, T_WIDE, [(x, D_MODEL, 0), (out, D_MODEL, 0)], [ada3, W["lnv"]],
                     [(D_MODEL, F32)])[0]
    saved = dict(x=x, u=u, proj=proj, ca=ca, cb=cb, h_a=h_a, h_b=h_b, h_c=h_c, y_a=y_a, y_b=y_b, y_c=y_c, qn=qn,
                 kvn=kvn, q_b=q_b, kv=kv, kpe=kpe, lse=lse, o=o, m=m, out=out)
    return x_next, saved, W


def layer_bwd(dxn, sv, ada3, W, tabs, S, before_in=None):
    cos, sin = tabs
    T = T_NARROW
    x, proj = sv["x"], sv["proj"]
    G = {}

    def ln_bwd(xv, ov, dy, a, lnv):
        gate = a[2:3, :]
        n, rstd = _ln_stats(ALPHA * xv + gate * ov)
        dr = _ln_bwd(dy * lnv[0:1, :], n, rstd)
        return ALPHA * dr, gate * dr, _colsum(dy * n), _colsum(dy), _colsum(dr * ov)

    dres, d_out, G["ln_g"], G["ln_b"], d_gate = rowwise(
        "ln_bwd", ln_bwd, S, T_WIDE, [(x, D_MODEL, 0), (sv["out"], D_MODEL, 0), (dxn, D_MODEL, 0)], [ada3, W["lnv"]],
        [(D_MODEL, F32), (D_MODEL, BF)], [D_MODEL] * 3)
    dm = mm(d_out, W["o"], name="mm_dm", trans_b=True, out_dtype=BF)
    G["w_o"] = mm(sv["m"], d_out, name="mm_gw_o", trans_a=True, out_dtype=BF)

    def merge_bwd(dmv, la, lb, lc, ya, yb, yc):
        outs, dls = [], []
        for lg, yv in ((la, ya), (lb, yb), (lc, yc)):
            s = _sigmoid(lg)
            outs.append(dmv * s)
            dls.append((dmv * yv * s * (1.0 - s)).astype(BF))
        return (jnp.concatenate(dls, axis=1),) + tuple(outs)

    d_proj, dy_a, dy_b, dy_c = rowwise(
        "merge_bwd", merge_bwd, S, T_WIDE,
        [(dm, D_MODEL, 0), (proj, D_MODEL, 0), (proj, D_MODEL, 1024), (proj, D_MODEL, 2048), (sv["y_a"], D_MODEL, 0),
         (sv["y_b"], D_MODEL, 0), (sv["y_c"], D_MODEL, 0)], [], [(3 * D_MODEL, BF)] + [(D_MODEL, BF)] * 3,
        into=(None, NP, OFF_M))

    dh = {}
    for br, dy in (("a", dy_a), ("b", dy_b), ("c", dy_c)):
        dh[br] = mm(dy, W[br + "_out"], name="mm_dh", trans_b=True, out_dtype=BF)
        G["w_%s_out" % br] = mm(sv["h_" + br], dy, name="mm_gw_branch", trans_a=True, out_dtype=BF)

    def a_post_bwd(c, ag, dhv, vec):
        n, rstd = _ln_stats(c + vec[0:1, :])
        z = n * vec[1:2, :] + vec[2:3, :]
        d_ag = dhv * _silu(z) * _dsilu(ag)
        dz = dhv * _silu(ag) * _dsilu(z)
        dc = _ln_bwd(dz * vec[1:2, :], n, rstd)
        return d_ag, dc, _colsum(dc), _colsum(dz * n), _colsum(dz)

    d_proj, dca, G["conv_a_b"], G["ln_a_g"], G["ln_a_b"] = rowwise(
        "mix_a_post_bwd", a_post_bwd, S, T, [(sv["ca"], CW, 0), (proj, CW, OFF_AG), (dh["a"], CW, 0)], [W["vec_a"]],
        [(CW, BF), (CW, F32)], [CW] * 3, into=(d_proj, NP, OFF_AG))
    d_proj, G["conv_a_w"] = conv_bwd("conv_a_bwd", proj, OFF_A, dca, W["conv_a"], 31, "glu", S, CW, d_proj)

    def b_post_bwd(c, gb, bg, dhv):
        sg = _silu(bg)
        d_gb_bg = jnp.concatenate([(dhv * sg * c).astype(BF), (dhv * gb * c * _dsilu(bg)).astype(BF)], axis=1)
        return d_gb_bg, dhv * sg * gb

    d_proj, dcb = rowwise("mix_b_post_bwd", b_post_bwd, S, T,
                          [(sv["cb"], CW, 0), (proj, CW, OFF_GB), (proj, CW, OFF_BG), (dh["b"], CW, 0)], [],
                          [(2 * CW, BF), (CW, F32)], into=(d_proj, NP, OFF_GB))
    d_proj, G["conv_b_w"] = conv_bwd("conv_b_bwd", proj, OFF_B, dcb, W["conv_b"], 3, "mul", S, CW, d_proj)

    d_proj, d_o = rowwise("mix_c_post_bwd", lambda ov, cg, dhv: (dhv * ov * _dsilu(cg), dhv * _silu(cg)), S, T,
                          [(sv["o"], CW, 0), (proj, CW, OFF_CG), (dh["c"], CW, 0)], [], [(CW, BF), (CW, F32)],
                          into=(d_proj, NP, OFF_CG))
    dq, d_kv, dkp_heads = attn_bwd(sv["q_b"], sv["kv"], sv["kpe"], sv["o"], sv["lse"], d_o, S)
    ropeT = _rope_fn(-1.0)

    def rope_bwd(dqv, dkp, c1, s1):
        parts = []
        for g in range(QUADS):
            parts.append(dqv[:, g * QW:g * QW + 2 * LANE].astype(BF))
            parts.append(ropeT(dqv[:, g * QW + 2 * LANE:(g + 1) * QW], c1, s1).astype(BF))
        f = dkp[:, :LANE] + dkp[:, LANE:]
        f = f + pltpu.roll(f, 64, 1)
        f = f + pltpu.roll(f, 32, 1)
        lane = lax.broadcasted_iota(jnp.int32, f.shape, 1)
        return jnp.concatenate(parts, axis=1), jnp.where(lane < ROPE, ropeT(f, c1, s1), 0.0)

    d_q, dk_pe = rowwise("rope_bwd", rope_bwd, S, T,
                         [(dq, HEADS * (NOPE + ROPE), 0), (dkp_heads, HEADS * ROPE, 0), (cos, LANE, 0), (sin, LANE, 0)],
                         [], [(HEADS * (NOPE + ROPE), BF), (LANE, BF)])
    d_qn = mm(d_q, W["uq"], name="mm_dqn", trans_b=True, out_dtype=BF)
    d_kvn = mm(d_kv, W["ukv"], name="mm_dkvn", trans_b=True, out_dtype=BF)
    G["w_uq"] = mm(sv["qn"], d_q, name="mm_gw_uq", trans_a=True, out_dtype=BF)
    G["w_ukv"] = mm(sv["kvn"], d_kv, name="mm_gw_ukv", trans_a=True, out_dtype=BF)

    def rms_bwd(ql, kvl, dqn, dkn, dkp, gq, gkv):
        res = []
        for xv, dy, g in ((ql, dqn, gq), (kvl, dkn, gkv)):
            r = lax.rsqrt(jnp.mean(xv * xv, axis=-1, keepdims=True) + RMS_EPS)
            dxh = dy * g
            res.append(((r * (dxh - xv * (r * r) * jnp.mean(dxh * xv, axis=-1, keepdims=True))).astype(BF),
                        _colsum(dy * xv * r)))
        pad = jnp.zeros((ql.shape[0], LANE), BF)
        return jnp.concatenate([res[1][0], dkp, pad, res[0][0], pad], axis=1), res[0][1], res[1][1]

    d_proj, G["q_norm_g"], G["kv_norm_g"] = rowwise(
        "rms_bwd", rms_bwd, S, T,
        [(proj, QL, OFF_Q), (proj, KVL, OFF_KV), (d_qn, QL, 0), (d_kvn, KVL, 0), (dk_pe, LANE, 0)],
        [W["gq"], W["gkv"]], [(NP - OFF_KV, BF)], [QL, KVL], into=(d_proj, NP, OFF_KV))
    deps = before_in(G) if before_in is not None else ()
    du = mm(d_proj, W["in"], name="mm_du", trans_b=True, tk=2048, deps=deps)
    G["w_in"] = mm(sv["u"], d_proj, name="mm_gw_in", trans_a=True, out_dtype=BF, deps=deps)

    def mod_bwd(duv, xv, dr, a):
        return duv * (1.0 + a[1:2, :]) + dr, _colsum(duv), _colsum(duv * xv)

    dx, d_shift, d_scale = rowwise("mod_bwd", mod_bwd, S, T_WIDE, [(du, D_MODEL, 0), (x, D_MODEL, 0), (dres, D_MODEL, 0)],
                                   [ada3], [(D_MODEL, F32)], [D_MODEL] * 2)
    d_ada = jnp.concatenate([d_shift, d_scale, d_gate], axis=1)
    return dx, G, d_ada


SMALL = ("conv_a_b", "ln_a_g", "ln_a_b", "q_norm_g", "kv_norm_g", "ln_g", "ln_b")


def _rows(v):
    n = v.shape[0]
    r = -(-n // (LANE * 16)) * 16
    return jnp.pad(v, (0, r * LANE - n)).reshape(r, LANE)


def kernel(x, c, positions, w_ada, b_ada, w_in, conv_a_w, conv_a_b, ln_a_g, ln_a_b, w_a_out, conv_b_w, w_b_out, q_norm_g, kv_norm_g, w_uq, w_ukv, w_c_out, w_o, ln_g, ln_b, loss_target, m_w_ada, m_b_ada, m_w_in, m_conv_a_w, m_conv_a_b, m_ln_a_g, m_ln_a_b, m_w_a_out, m_conv_b_w, m_w_b_out, m_q_norm_g, m_kv_norm_g, m_w_uq, m_w_ukv, m_w_c_out, m_w_o, m_ln_g, m_ln_b, v_w_ada, v_b_ada, v_w_in, v_conv_a_w, v_conv_a_b, v_ln_a_g, v_ln_a_b, v_w_a_out, v_conv_b_w, v_w_b_out, v_q_norm_g, v_kv_norm_g, v_w_uq, v_w_ukv, v_w_c_out, v_w_o, v_ln_g, v_ln_b):
    P = dict(w_ada=w_ada, b_ada=b_ada, w_in=w_in, conv_a_w=conv_a_w, conv_a_b=conv_a_b, ln_a_g=ln_a_g, ln_a_b=ln_a_b,
             w_a_out=w_a_out, conv_b_w=conv_b_w, w_b_out=w_b_out, q_norm_g=q_norm_g, kv_norm_g=kv_norm_g, w_uq=w_uq,
             w_ukv=w_ukv, w_c_out=w_c_out, w_o=w_o, ln_g=ln_g, ln_b=ln_b)
    Mo = dict(w_ada=m_w_ada, b_ada=m_b_ada, w_in=m_w_in, conv_a_w=m_conv_a_w, conv_a_b=m_conv_a_b, ln_a_g=m_ln_a_g,
              ln_a_b=m_ln_a_b, w_a_out=m_w_a_out, conv_b_w=m_conv_b_w, w_b_out=m_w_b_out, q_norm_g=m_q_norm_g,
              kv_norm_g=m_kv_norm_g, w_uq=m_w_uq, w_ukv=m_w_ukv, w_c_out=m_w_c_out, w_o=m_w_o, ln_g=m_ln_g, ln_b=m_ln_b)
    Vo = dict(w_ada=v_w_ada, b_ada=v_b_ada, w_in=v_w_in, conv_a_w=v_conv_a_w, conv_a_b=v_conv_a_b, ln_a_g=v_ln_a_g,
              ln_a_b=v_ln_a_b, w_a_out=v_w_a_out, conv_b_w=v_conv_b_w, w_b_out=v_w_b_out, q_norm_g=v_q_norm_g,
              kv_norm_g=v_kv_norm_g, w_uq=v_w_uq, w_ukv=v_w_ukv, w_c_out=v_w_c_out, w_o=v_w_o, ln_g=v_ln_g, ln_b=v_ln_b)
    ORDER = ("w_ada", "b_ada", "w_in", "conv_a_w", "conv_a_b", "ln_a_g", "ln_a_b", "w_a_out", "conv_b_w", "w_b_out",
             "q_norm_g", "kv_norm_g", "w_uq", "w_ukv", "w_c_out", "w_o", "ln_g", "ln_b")
    L = w_ada.shape[0]
    S = x.shape[1]
    me = 4 * lax.axis_index("x") + 2 * lax.axis_index("y") + lax.axis_index("c")
    x2 = x[0]
    tgt = loss_target[0]

    small_in = _rows(jnp.concatenate([c.reshape(-1), conv_a_w.reshape(-1), conv_b_w.reshape(-1)]))
    w_in_b = jnp.pad(w_in.astype(BF), ((0, 0), (0, 0), (0, W_IN_PAD - IN_SHARD)))
    misc_b = jnp.concatenate([w_a_out, w_b_out, w_c_out, w_ukv, jnp.pad(w_uq, ((0, 0), (0, 0), (0, LANE - UQ_SHARD)))],
                             axis=1).astype(BF)
    w_o_b = w_o.astype(BF)
    gathered = [None] * L
    gathered[0] = [gather_two_level("gather0_w_in", w_in_b[0])]
    pending_rest, rest_token = exchange_begin("gather0_rest", [misc_b[0], w_o_b[0]], 2, gathered[0][0])
    sg = exchange("gather_small", [small_in], [])[0]
    sgf = sg.reshape(NDEV, -1)
    c_all = sgf[:, :D_MODEL]
    o1 = D_MODEL + L * 31 * 64
    conv_a_full = sgf[:, D_MODEL:o1].reshape(NDEV, L, 31, 64).transpose(1, 2, 0, 3).reshape(L, 31, CW)
    conv_b_full = sgf[:, o1:o1 + L * 3 * 64].reshape(NDEV, L, 3, 64).transpose(1, 2, 0, 3).reshape(L, 3, CW)

    c_act = rowwise("silu_c", _silu, 16, 16, [(jnp.pad(c_all, ((0, 8), (0, 0))), D_MODEL, 0)], [], [(D_MODEL, BF)])[0]
    ncol = w_ada.shape[2]
    w_ada_b = w_ada.astype(BF).transpose(1, 0, 2).reshape(D_MODEL, L * ncol)
    b_mine = lax.dynamic_slice_in_dim(b_ada, me * ncol, ncol, axis=1).reshape(1, L * ncol)
    ada_part = mm(c_act, w_ada_b, name="mm_ada", bias=b_mine)
    ada_rows = -(-(L * ncol) // (LANE * 8)) * 8
    ada_send = jnp.pad(ada_part[:NDEV].reshape(NDEV, -1, LANE), ((0, 0), (0, ada_rows - L * ncol // LANE), (0, 0)))
    ada_recv = exchange("a2a_ada", [], [ada_send])[0]
    ada = ada_recv[:, :L * ncol // LANE].reshape(NDEV, L, ncol).transpose(1, 0, 2).reshape(L, 3, D_MODEL)

    inv_freq = ROPE_THETA ** (-jnp.arange(0, ROPE, 2, dtype=F32) / ROPE)
    ang = positions[0].astype(F32)[:, None] * inv_freq
    tabs = (jnp.tile(jnp.cos(ang), (1, 2 * LANE // ROPE)), jnp.tile(jnp.sin(ang), (1, 2 * LANE // ROPE)))

    straight = np.arange(D_MODEL)
    fwd_in = [(0, 0, 0, 0, D_MODEL, _to_gathered(IN_PERM, IN_SHARD, W_IN_PAD))]
    fwd_misc = [(0, ROW_A, 0, 0, CW, straight), (0, ROW_B, 1, 0, CW, straight), (0, ROW_C, 2, 0, CW, straight),
                (0, ROW_UKV, 3, 0, KVL, UKV_PERM), (0, ROW_UQ, 4, 0, QL, _to_gathered(UQ_PERM, UQ_SHARD, LANE))]
    rev_in = [(0, 0, 0, 0, D_MODEL, _from_full(_inverse(IN_PERM, D_IN), IN_SHARD, W_IN_PAD))]
    rev_misc = [(0, 0, 0, ROW_A, CW, straight), (1, 0, 0, ROW_B, CW, straight), (2, 0, 0, ROW_C, CW, straight),
                (3, 0, 0, ROW_UKV, KVL, _from_full(_inverse(UKV_PERM, HEADS * (NOPE + VH)), LANE, LANE)),
                (4, 0, 0, ROW_UQ, QL, _from_full(_inverse(UQ_PERM, HEADS * (NOPE + ROPE)), UQ_SHARD, LANE))]

    def layer_weights(l, deps):
        w_in_p = col_gather("relayout_w_in", [gathered[l][0]], [(D_MODEL, NP)], fwd_in, deps)[0]

        def late(after):
            if len(gathered[l]) == 1:
                gathered[l] += exchange_end(pending_rest, after)
            _, g_misc, g_o = gathered[l]
            a_out, b_out, c_out, ukv, uq = col_gather(
                "relayout_misc", [g_misc],
                [(CW, D_MODEL)] * 3 + [(KVL, HEADS * (NOPE + VH)), (QL, HEADS * (NOPE + ROPE))], fwd_misc, deps)
            return {"a_out": a_out, "b_out": b_out, "c_out": c_out, "uq": uq, "ukv": ukv,
                    "o": g_o.reshape(D_MODEL, D_MODEL)}

        return {
            "in": w_in_p, "late": late,
            "conv_a": jnp.pad(conv_a_full[l], ((0, 1), (0, 0))), "conv_b": jnp.pad(conv_b_full[l], ((0, 5), (0, 0))),
            "vec_a": jnp.stack([conv_a_b[l], ln_a_g[l], ln_a_b[l]]), "gq": q_norm_g[l][None], "gkv": kv_norm_g[l][None],
            "lnv": jnp.stack([ln_g[l], ln_b[l]]),
        }

    h = x2
    saved, weights = [], []
    for l in range(L):
        ada_l, deps = ada[l], ()
        if l + 1 < L:
            pending, token = exchange_begin("gather%d" % (l + 1), [w_in_b[l + 1], misc_b[l + 1], w_o_b[l + 1]], 3,
                                            rest_token if l == 0 else gathered[l][0])
            ada_l, deps = ada_l + token[0, 0], (token,)
        h, sv, Wl = layer_fwd(h, ada_l, layer_weights(l, deps), tabs, S)
        if l + 1 < L:
            gathered[l + 1] = exchange_end(pending, h)
        saved.append(sv)
        weights.append(Wl)

    def loss_fn(y, t):
        e = y - t
        return e * (1.0 / D_MODEL), _colsum(e * e)

    dy, sq = rowwise("loss", loss_fn, S, 256, [(h, D_MODEL, 0), (tgt, D_MODEL, 0)], [], [(D_MODEL, F32)], [D_MODEL])
    loss = lax.psum(0.5 * jnp.sum(sq) / D_MODEL, ("x", "y", "c"))

    grads, d_adas, recv = [None] * L, [None] * L, [None] * L
    pending, token = None, None

    def send_rest(g):
        send_misc = col_gather("unrelayout_misc", [g["w_a_out"], g["w_b_out"], g["w_c_out"], g["w_ukv"], g["w_uq"]],
                               [(NDEV, MISC_ROWS, LANE)], rev_misc)[0]
        return [send_misc, g["w_o"].reshape(NDEV, D_MODEL // NDEV, D_MODEL)]

    rest0 = []

    def early_rest(g):
        handle, tok = exchange_begin("scatter0_rest", send_rest(g), 0, g["w_o"])
        rest0.append(handle)
        return (tok,)

    for l in reversed(range(L)):
        ada_l = ada[l] if token is None else ada[l] + token[0, 0]
        dy, g, d_adas[l] = layer_bwd(dy, saved[l], ada_l, weights[l], tabs, S, early_rest if l == 0 else None)
        grads[l] = g
        if pending is not None:
            recv[l + 1] = exchange_end(pending, dy)
        send_in = col_gather("unrelayout_w_in", [g["w_in"]], [(NDEV, D_MODEL, W_IN_PAD)], rev_in)[0]
        if l == 0:
            def layer_vec(i):
                parts = [grads[i][n].reshape(-1) for n in SMALL]
                parts += [grads[i]["conv_a_w"][:31].reshape(-1), grads[i]["conv_b_w"][:3].reshape(-1),
                          d_adas[i].reshape(-1)]
                return jnp.concatenate(parts)

            small_sizes = [int(grads[0][n].size) for n in SMALL] + [31 * CW, 3 * CW, 3 * D_MODEL]
            gsmall = exchange("gather_small_grads", [_rows(jnp.concatenate([layer_vec(i) for i in range(L)]))], [])[0]
            pending, token = exchange_begin("scatter0", [send_in], 0, gsmall)
        else:
            pending, token = exchange_begin("scatter%d" % l, [send_in] + send_rest(g), 0,
                                            dy if l + 1 == L else recv[l + 1][0])
    grad_x = dy[None]

    gsmall = gsmall + token[0, 0]
    gsum = sum_slots("sum_small", gsmall).reshape(-1)
    recv[0] = [None] + exchange_end(rest0[0], gsum)
    Gr = {}
    offs = np.cumsum([0] + small_sizes)
    per_layer = int(offs[-1])
    gsum = gsum[:L * per_layer].reshape(L, per_layer)
    for i, n in enumerate(SMALL):
        Gr[n] = gsum[:, offs[i]:offs[i + 1]]
    ca = gsum[:, offs[7]:offs[8]].reshape(L, 31, CW)
    cbw = gsum[:, offs[8]:offs[9]].reshape(L, 3, CW)
    Gr["conv_a_w"] = lax.dynamic_slice_in_dim(ca, me * 64, 64, axis=2)
    Gr["conv_b_w"] = lax.dynamic_slice_in_dim(cbw, me * 64, 64, axis=2)
    Gr["b_ada"] = gsum[:, offs[9]:offs[10]]
    d_ada_all = gsmall.reshape(NDEV, -1)[:, :L * per_layer].reshape(NDEV, L, per_layer)[:, :, offs[9]:offs[10]]
    d_mine = lax.dynamic_slice_in_dim(d_ada_all, me * ncol, ncol, axis=2).reshape(NDEV, L * ncol)
    g_ada = mm(c_act, jnp.pad(d_mine, ((0, 8), (0, 0))).astype(BF), name="mm_gw_ada", trans_a=True)
    Gr["w_ada"] = g_ada.reshape(D_MODEL, L, ncol).transpose(1, 0, 2)

    D, NM, NV = {}, {}, {}
    D["w_ada"], NM["w_ada"], NV["w_ada"] = adamw("adamw_w_ada", P["w_ada"], Gr["w_ada"], Mo["w_ada"], Vo["w_ada"])
    Gr["w_o"], D["w_o"], NM["w_o"], NV["w_o"] = sum_adamw(
        "sum_adamw_w_o", [recv[l][2] for l in range(L)], P["w_o"], Mo["w_o"], Vo["w_o"])
    for n, row0 in (("w_a_out", ROW_A), ("w_b_out", ROW_B), ("w_c_out", ROW_C), ("w_ukv", ROW_UKV), ("w_uq", ROW_UQ)):
        Gr[n], D[n], NM[n], NV[n] = sum_adamw("sum_adamw_" + n, [recv[l][1] for l in range(L)], P[n], Mo[n], Vo[n],
                                              row0=row0)
    upper = sum_adamw("sum_adamw_w_in_upper", [recv[l][0] for l in range(1, L)], P["w_in"], Mo["w_in"], Vo["w_in"], lo=1)
    recv[0][0] = exchange_end(pending, upper[1])[0]
    Gr["w_in"], D["w_in"], NM["w_in"], NV["w_in"] = sum_adamw(
        "sum_adamw_w_in", [recv[0][0]], P["w_in"], Mo["w_in"], Vo["w_in"], lo=0, prev=upper)
    packed =("b_ada", "conv_a_w", "conv_b_w") + SMALL
    pk = lambda T_: _rows(jnp.concatenate([T_[n].reshape(-1) for n in packed]))[None]
    dS, mS, vS = adamw("adamw_small", pk(P), pk(Gr), pk(Mo), pk(Vo))
    o = 0
    for n in packed:
        sz = int(np.prod(P[n].shape))
        D[n] = dS.reshape(-1)[o:o + sz].reshape(P[n].shape)
        NM[n] = mS.reshape(-1)[o:o + sz].reshape(P[n].shape)
        NV[n] = vS.reshape(-1)[o:o + sz].reshape(P[n].shape)
        o += sz
    return (loss, grad_x, *[Gr[n] for n in ORDER], *[D[n] for n in ORDER], *[NM[n] for n in ORDER],
            *[NV[n] for n in ORDER])
```

```python
import functools
import math

import numpy as np
import jax
import jax.numpy as jnp
from jax import lax
from jax.experimental import pallas as pl
from jax.experimental.pallas import tpu as pltpu

BF = jnp.bfloat16
F32 = jnp.float32
MESH = pl.DeviceIdType.MESH
NDEV = 8

HEADS, NOPE, ROPE, VH = 8, 64, 32, 64
HP = 128
ROPE_THETA = 10000.0
LN_EPS = 1e-5
RMS_EPS = 1e-6
LR, B1, B2, EPS, WD, STEP = 0.001, 0.9, 0.999, 1e-08, 0.01, 10

LANE = 128
VMEM_LIMIT = 56 * 1024 * 1024

D_MODEL, CW, QL, KVL = 1024, 512, 384, 256
OFF_M, OFF_A, OFF_AG, OFF_B, OFF_CG, OFF_GB, OFF_BG = 0, 3072, 4096, 4608, 5632, 6144, 6656
OFF_KV, OFF_KR, OFF_Q, NP = 7168, 7424, 7680, 8192
D_IN = 7840


def _cparams(**kw):
    return pltpu.CompilerParams(vmem_limit_bytes=VMEM_LIMIT, **kw)


def _sigmoid(x):
    return jax.nn.sigmoid(x)


def _silu(x):
    return x * _sigmoid(x)


def _dsilu(x):
    s = _sigmoid(x)
    return s * (1.0 + x * (1.0 - s))


def _pick_tile(n, cap, mult):
    if n <= cap:
        return n
    for t in range(cap - cap % mult, 0, -mult):
        if n % t == 0:
            return t
    raise ValueError((n, cap, mult))


def mm(a, b, *, name, trans_a=False, trans_b=False, out_dtype=F32, bias=None, tm=1024, tn=1024, tk=2048, deps=()):
    if trans_a:
        K, M = a.shape
    else:
        M, K = a.shape
    if trans_b:
        N, K2 = b.shape
    else:
        K2, N = b.shape
    assert K == K2 and not (trans_a and trans_b), (a.shape, b.shape)
    tm, tn = _pick_tile(M, tm, 16), _pick_tile(N, tn, LANE)
    tk = _pick_tile(K, tk, LANE if trans_b else 16)
    assert M % tm == 0 and N % tn == 0 and K % tk == 0, (M, N, K, tm, tn, tk)
    nk = K // tk
    dims = (((0 if trans_a else 1,), (1 if trans_b else 0,)), ((), ()))
    has_bias = bias is not None

    def body(*refs):
        a_ref, b_ref = refs[0], refs[1]
        bias_ref = refs[2] if has_bias else None
        o_ref = refs[(3 if has_bias else 2) + len(deps)]
        p = lax.dot_general(a_ref[...], b_ref[...], dims, preferred_element_type=F32)

        def finish(v):
            if has_bias:
                v = v + bias_ref[...]
            o_ref[...] = v.astype(o_ref.dtype)

        if nk == 1:
            finish(p)
        else:
            acc = refs[-1]
            k = pl.program_id(2)

            @pl.when(k == 0)
            def _():
                acc[...] = p

            @pl.when(k > 0)
            def _():
                acc[...] += p

            @pl.when(k == nk - 1)
            def _():
                finish(acc[...])

    if trans_a:
        a_spec = pl.BlockSpec((tk, tm), lambda i, j, k: (k, i))
    else:
        a_spec = pl.BlockSpec((tm, tk), lambda i, j, k: (i, k))
    if trans_b:
        b_spec = pl.BlockSpec((tn, tk), lambda i, j, k: (j, k))
    else:
        b_spec = pl.BlockSpec((tk, tn), lambda i, j, k: (k, j))
    in_specs = [a_spec, b_spec]
    args = [a, b]
    if has_bias:
        in_specs.append(pl.BlockSpec((1, tn), lambda i, j, k: (0, j)))
        args.append(bias)
    in_specs += [ANY_SPEC] * len(deps)
    args += list(deps)
    return pl.pallas_call(
        body, name=name, grid=(M // tm, N // tn, nk),
        in_specs=in_specs, out_specs=pl.BlockSpec((tm, tn), lambda i, j, k: (i, j)),
        out_shape=jax.ShapeDtypeStruct((M, N), out_dtype),
        scratch_shapes=[pltpu.VMEM((tm, tn), F32)] if nk > 1 else [],
        compiler_params=_cparams(),
    )(*args)


def rowwise(name, fn, S, T, row_ins, full_ins, row_outs, acc_outs=(), into=None):
    n_in = len(row_ins) + len(full_ins)
    n_ro, n_ao = len(row_outs), len(acc_outs)
    alias = into is not None and into[0] is not None
    T = min(T, S)

    def body(*refs):
        vals = [r[...] for r in refs[:n_in]]
        vals = [v.astype(F32) if v.dtype == BF else v for v in vals]
        outs = fn(*vals)
        if not isinstance(outs, (tuple, list)):
            outs = (outs,)
        assert len(outs) == n_ro + n_ao, (name, len(outs))
        o0 = n_in + (1 if alias else 0)
        for r, v in zip(refs[o0:o0 + n_ro], outs[:n_ro]):
            r[...] = v.astype(r.dtype)
        first = pl.program_id(0) == 0
        for r, v in zip(refs[o0 + n_ro:], outs[n_ro:]):
            def init(r=r, v=v):
                r[...] = v

            def accum(r=r, v=v):
                r[...] += v

            pl.when(first)(init)
            pl.when(jnp.logical_not(first))(accum)

    in_specs, args = [], []
    for arr, W, off in row_ins:
        assert off % W == 0 and arr.shape[0] == S, (name, arr.shape, W, off)
        in_specs.append(pl.BlockSpec((T, W), functools.partial(lambda i, cb: (i, cb), cb=off // W)))
        args.append(arr)
    for arr in full_ins:
        in_specs.append(pl.BlockSpec(arr.shape, lambda i: (0, 0)))
        args.append(arr)
    out_specs = [pl.BlockSpec((T, W), lambda i: (i, 0)) for W, _ in row_outs]
    out_shape = [jax.ShapeDtypeStruct((S, W), dt) for W, dt in row_outs]
    aliases = {}
    if into is not None:
        buf, total, off = into
        W0, dt0 = row_outs[0]
        assert off % W0 == 0
        out_specs[0] = pl.BlockSpec((T, W0), functools.partial(lambda i, cb: (i, cb), cb=off // W0))
        out_shape[0] = jax.ShapeDtypeStruct((S, total), dt0)
        if alias:
            in_specs.append(ANY_SPEC)
            args.append(buf)
            aliases = {n_in: 0}
    out_specs += [pl.BlockSpec((1, W), lambda i: (0, 0)) for W in acc_outs]
    out_shape += [jax.ShapeDtypeStruct((1, W), F32) for W in acc_outs]
    return pl.pallas_call(
        body, name=name, grid=(S // T,), in_specs=in_specs, out_specs=out_specs, out_shape=out_shape,
        input_output_aliases=aliases, compiler_params=_cparams(),
    )(*args)


def _colsum(v):
    return jnp.sum(v, axis=0, keepdims=True)


def _ln_stats(r):
    mu = jnp.mean(r, axis=-1, keepdims=True)
    d = r - mu
    var = jnp.mean(d * d, axis=-1, keepdims=True)
    rstd = lax.rsqrt(var + LN_EPS)
    return d * rstd, rstd


def _ln_bwd(dn, n, rstd):
    return rstd * (dn - jnp.mean(dn, axis=-1, keepdims=True) - n * jnp.mean(dn * n, axis=-1, keepdims=True))


CPAD = 32
TC = 64


def _pre(mode, x1, x2):
    return x1 * _sigmoid(x2) if mode == "glu" else x1 * x2


def _shifted(ext, sft):
    n = TC + CPAD
    return pltpu.roll(ext, (n - sft) % n, 0)[0:TC]


def _interleaved_specs(S, off):
    return [pl.BlockSpec((S, LANE), functools.partial(lambda j, o: (0, o + 2 * j), o=off // LANE)),
            pl.BlockSpec((S, LANE), functools.partial(lambda j, o: (0, o + 2 * j + 1), o=off // LANE))]


def conv_fwd(name, src, off, w_pad, taps, mode, S, C):
    nchunk = S // TC

    def body(x1_ref, x2_ref, w_ref, o_ref, a_pad):
        a_pad[0:CPAD, :] = jnp.zeros((CPAD, LANE), F32)

        def fill(i, _):
            r = pl.multiple_of(i * 256, 256)
            a_pad[pl.ds(CPAD + r, 256), :] = _pre(mode, x1_ref[pl.ds(r, 256), :].astype(F32),
                                                  x2_ref[pl.ds(r, 256), :].astype(F32))
            return 0

        lax.fori_loop(0, S // 256, fill, 0)

        def chunk(i, _):
            base = pl.multiple_of(i * TC, TC)
            ext = a_pad[pl.ds(base, TC + CPAD), :]
            acc = jnp.zeros((TC, LANE), F32)
            for k in range(taps):
                acc = acc + w_ref[pl.ds(k, 1), :] * _shifted(ext, CPAD - (taps - 1) + k)
            o_ref[pl.ds(base, TC), :] = acc
            return 0

        lax.fori_loop(0, nchunk, chunk, 0)

    kp = w_pad.shape[0]
    return pl.pallas_call(
        body, name=name, grid=(C // LANE,),
        in_specs=_interleaved_specs(S, off) + [pl.BlockSpec((kp, LANE), lambda j: (0, j))],
        out_specs=pl.BlockSpec((S, LANE), lambda j: (0, j)),
        out_shape=jax.ShapeDtypeStruct((S, C), F32),
        scratch_shapes=[pltpu.VMEM((S + CPAD, LANE), F32)],
        compiler_params=_cparams(),
    )(src, src, w_pad)


def conv_bwd(name, src, off, dc, w_pad, taps, mode, S, C, buf):
    nchunk = S // TC
    kp = w_pad.shape[0]

    def body(x1_ref, x2_ref, dc_ref, w_ref, _, d_ref, dw_ref, a_pad, dc_pad, dw_acc):
        a_pad[0:CPAD, :] = jnp.zeros((CPAD, LANE), F32)
        dc_pad[S:S + CPAD, :] = jnp.zeros((CPAD, LANE), F32)
        dw_acc[...] = jnp.zeros(dw_acc.shape, F32)

        def fill(i, _):
            r = pl.multiple_of(i * 256, 256)
            a_pad[pl.ds(CPAD + r, 256), :] = _pre(mode, x1_ref[pl.ds(r, 256), :].astype(F32),
                                                  x2_ref[pl.ds(r, 256), :].astype(F32))
            dc_pad[pl.ds(r, 256), :] = dc_ref[pl.ds(r, 256), :]
            return 0

        lax.fori_loop(0, S // 256, fill, 0)

        def chunk(i, _):
            base = pl.multiple_of(i * TC, TC)
            ext_d = dc_pad[pl.ds(base, TC + CPAD), :]
            ext_a = a_pad[pl.ds(base, TC + CPAD), :]
            dcv = ext_d[0:TC]
            da = jnp.zeros((TC, LANE), F32)
            for k in range(taps):
                da = da + w_ref[pl.ds(k, 1), :] * _shifted(ext_d, taps - 1 - k)
                prod = dcv * _shifted(ext_a, CPAD - (taps - 1) + k)
                fold = prod[0:8]
                for g in range(1, TC // 8):
                    fold = fold + prod[8 * g:8 * g + 8]
                dw_acc[pl.ds(8 * k, 8), :] += fold
            x1 = x1_ref[pl.ds(base, TC), :].astype(F32)
            x2 = x2_ref[pl.ds(base, TC), :].astype(F32)
            if mode == "glu":
                s = _sigmoid(x2)
                d1, d2 = da * s, da * x1 * s * (1.0 - s)
            else:
                d1, d2 = da * x2, da * x1
            d_ref[pl.ds(base, TC), 0:LANE] = d1.astype(BF)
            d_ref[pl.ds(base, TC), LANE:2 * LANE] = d2.astype(BF)
            return 0

        lax.fori_loop(0, nchunk, chunk, 0)
        dw_ref[...] = jnp.zeros(dw_ref.shape, F32)
        for k in range(taps):
            dw_ref[pl.ds(k, 1), :] = jnp.sum(dw_acc[pl.ds(8 * k, 8), :], axis=0, keepdims=True)

    blk = pl.BlockSpec((S, LANE), lambda j: (0, j))
    return pl.pallas_call(
        body, name=name, grid=(C // LANE,),
        in_specs=_interleaved_specs(S, off) + [blk, pl.BlockSpec((kp, LANE), lambda j: (0, j)), ANY_SPEC],
        out_specs=[pl.BlockSpec((S, 2 * LANE), functools.partial(lambda j, o: (0, o + j), o=off // (2 * LANE))),
                   pl.BlockSpec((kp, LANE), lambda j: (0, j))],
        out_shape=[jax.ShapeDtypeStruct(buf.shape, BF), jax.ShapeDtypeStruct((kp, C), F32)],
        input_output_aliases={4: 0},
        scratch_shapes=[pltpu.VMEM((S + CPAD, LANE), F32), pltpu.VMEM((S + CPAD, LANE), F32),
                        pltpu.VMEM((8 * kp, LANE), F32)],
        compiler_params=_cparams(),
    )(src, src, dc, w_pad, buf)


FWD_TILES = (512, 512)
BWD_TILES = (512, 512)
QUADS = HEADS // 4
QW, KVW = 4 * (NOPE + ROPE), 4 * (NOPE + VH)
SCALE = (NOPE + ROPE) ** -0.5
NT_DIMS = (((1,), (1,)), ((), ()))
TN_DIMS = (((0,), (0,)), ((), ()))


def _lane_mask(width, group, dtype):
    lane = lax.broadcasted_iota(jnp.int32, (1, LANE), 1)
    return jnp.where(lane // width == group, 1.0, 0.0).astype(dtype)


def _visible(tq, tk, off):
    row = lax.broadcasted_iota(jnp.int32, (tq, tk), 0)
    col = lax.broadcasted_iota(jnp.int32, (tq, tk), 1)
    return col <= row + off


def _attn_tiles(S, tq, tk):
    tk = tk if S % tk == 0 else 256
    return min(tq, tk), tk


def attn_fwd(q, kv, kpe, S):
    tq, tk = _attn_tiles(S, *FWD_TILES)
    nq = S // tq

    def body(q_ref, kv_ref, kp_ref, o_ref, lse_ref):
        for t in range(2):
            cols = slice(t * LANE, (t + 1) * LANE)
            for hh in range(2):
                def q_block(qi, _, t=t, hh=hh, cols=cols):
                    r0 = pl.multiple_of(qi * tq, tq)
                    qcat = jnp.concatenate([q_ref[pl.ds(r0, tq), cols] * _lane_mask(NOPE, hh, BF),
                                            q_ref[pl.ds(r0, tq), 2 * LANE:3 * LANE] * _lane_mask(ROPE, 2 * t + hh, BF)],
                                           axis=1)
                    nfull = (qi * tq) // tk

                    def step(kj, carry, masked):
                        m, l, acc = carry
                        c0 = pl.multiple_of(kj * tk, tk)
                        kc = jnp.concatenate([kv_ref[pl.ds(c0, tk), cols], kp_ref[pl.ds(c0, tk), :]], axis=1)
                        vt = kv_ref[pl.ds(c0, tk), (2 + t) * LANE:(3 + t) * LANE]
                        s = lax.dot_general(qcat, kc, NT_DIMS, preferred_element_type=F32) * SCALE
                        if masked:
                            s = jnp.where(_visible(tq, tk, qi * tq - nfull * tk), s, -jnp.inf)
                        m_new = jnp.maximum(m, jnp.max(s, axis=-1, keepdims=True))
                        p = jnp.exp(s - m_new)
                        alpha = jnp.exp(m - m_new)
                        l = alpha * l + jnp.sum(p, axis=-1, keepdims=True)
                        acc = alpha * acc + jnp.dot(p.astype(BF), vt, preferred_element_type=F32)
                        return m_new, l, acc

                    init = (jnp.full((tq, 1), -jnp.inf, F32), jnp.zeros((tq, 1), F32), jnp.zeros((tq, LANE), F32))
                    carry = lax.fori_loop(0, nfull, lambda kj, c: step(kj, c, False), init)
                    m, l, acc = step(nfull, carry, True)
                    mine = _lane_mask(NOPE, hh, F32)
                    if hh == 0:
                        o_ref[pl.ds(r0, tq), cols] = (acc / l) * mine
                        lse_ref[pl.ds(r0, tq), cols] = (m + jnp.log(l)) * mine
                    else:
                        o_ref[pl.ds(r0, tq), cols] += (acc / l) * mine
                        lse_ref[pl.ds(r0, tq), cols] += (m + jnp.log(l)) * mine
                    return 0

                lax.fori_loop(0, nq, q_block, 0)

    return pl.pallas_call(
        body, name="attn_fwd", grid=(QUADS,),
        in_specs=[pl.BlockSpec((S, QW), lambda g: (0, g)), pl.BlockSpec((S, KVW), lambda g: (0, g)),
                  pl.BlockSpec((S, LANE), lambda g: (0, 0))],
        out_specs=[pl.BlockSpec((S, 2 * LANE), lambda g: (0, g))] * 2,
        out_shape=[jax.ShapeDtypeStruct((S, HEADS * VH), F32)] * 2,
        compiler_params=_cparams(),
    )(q, kv, kpe)


def attn_bwd(q, kv, kpe, o, lse, do, S):
    tq, tk = _attn_tiles(S, *BWD_TILES)
    nq = S // tq

    def body(q_ref, kv_ref, kp_ref, o_ref, lse_ref, do_ref, dq_ref, dkv_ref, dkp_ref, dq_acc, dk_acc, dv_acc):
        for t in range(2):
            cols = slice(t * LANE, (t + 1) * LANE)
            dk_acc[...] = jnp.zeros(dk_acc.shape, F32)
            dv_acc[...] = jnp.zeros(dv_acc.shape, F32)
            for hh in range(2):
                def q_block(qi, _, t=t, hh=hh, cols=cols):
                    r0 = pl.multiple_of(qi * tq, tq)
                    mine = _lane_mask(NOPE, hh, F32)
                    qcat = jnp.concatenate([q_ref[pl.ds(r0, tq), cols] * _lane_mask(NOPE, hh, BF),
                                            q_ref[pl.ds(r0, tq), 2 * LANE:3 * LANE] * _lane_mask(ROPE, 2 * t + hh, BF)],
                                           axis=1)
                    dof = do_ref[pl.ds(r0, tq), cols] * mine
                    dob = dof.astype(BF)
                    delta = jnp.sum(dof * o_ref[pl.ds(r0, tq), cols], axis=-1, keepdims=True)
                    lse_h = lse_ref[pl.ds(r0, tq), cols][:, hh * NOPE:hh * NOPE + 1]
                    nfull = (qi * tq) // tk
                    dq_acc[...] = jnp.zeros(dq_acc.shape, F32)

                    def step(kj, _, masked):
                        c0 = pl.multiple_of(kj * tk, tk)
                        kc = jnp.concatenate([kv_ref[pl.ds(c0, tk), cols], kp_ref[pl.ds(c0, tk), :]], axis=1)
                        vt = kv_ref[pl.ds(c0, tk), (2 + t) * LANE:(3 + t) * LANE]
                        s = lax.dot_general(qcat, kc, NT_DIMS, preferred_element_type=F32) * SCALE
                        if masked:
                            s = jnp.where(_visible(tq, tk, qi * tq - nfull * tk), s, -jnp.inf)
                        p = jnp.exp(s - lse_h)
                        dp = lax.dot_general(dob, vt, NT_DIMS, preferred_element_type=F32)
                        ds = (p * (dp - delta) * SCALE).astype(BF)
                        dv_acc[pl.ds(c0, tk), :] += lax.dot_general(p.astype(BF), dob, TN_DIMS,
                                                                    preferred_element_type=F32)
                        dk_acc[pl.ds(c0, tk), :] += lax.dot_general(ds, qcat, TN_DIMS, preferred_element_type=F32)
                        dq_acc[...] += jnp.dot(ds, kc, preferred_element_type=F32)
                        return 0

                    lax.fori_loop(0, nfull, lambda kj, c: step(kj, c, False), 0)
                    step(nfull, 0, True)
                    d = dq_acc[...]
                    pe = d[:, LANE:] * _lane_mask(ROPE, 2 * t + hh, F32)
                    if hh == 0:
                        dq_ref[pl.ds(r0, tq), cols] = d[:, :LANE] * mine
                    else:
                        dq_ref[pl.ds(r0, tq), cols] += d[:, :LANE] * mine
                    if t == 0 and hh == 0:
                        dq_ref[pl.ds(r0, tq), 2 * LANE:3 * LANE] = pe
                    else:
                        dq_ref[pl.ds(r0, tq), 2 * LANE:3 * LANE] += pe
                    return 0

                lax.fori_loop(0, nq, q_block, 0)
            dkv_ref[:, t * LANE:(t + 1) * LANE] = dk_acc[:, :LANE].astype(BF)
            dkv_ref[:, (2 + t) * LANE:(3 + t) * LANE] = dv_acc[...].astype(BF)
            if t == 0:
                dkp_ref[...] = dk_acc[:, LANE:]
            else:
                dkp_ref[...] += dk_acc[:, LANE:]

    qspec = pl.BlockSpec((S, QW), lambda g: (0, g))
    kvspec = pl.BlockSpec((S, KVW), lambda g: (0, g))
    ospec = pl.BlockSpec((S, 2 * LANE), lambda g: (0, g))
    return pl.pallas_call(
        body, name="attn_bwd", grid=(QUADS,),
        in_specs=[qspec, kvspec, pl.BlockSpec((S, LANE), lambda g: (0, 0)), ospec, ospec, ospec],
        out_specs=[qspec, kvspec, pl.BlockSpec((S, LANE), lambda g: (0, g))],
        out_shape=[jax.ShapeDtypeStruct((S, HEADS * (NOPE + ROPE)), F32), jax.ShapeDtypeStruct((S, HEADS * (NOPE + VH)), BF),
                   jax.ShapeDtypeStruct((S, HEADS * ROPE), F32)],
        scratch_shapes=[pltpu.VMEM((tq, 2 * LANE), F32), pltpu.VMEM((S, 2 * LANE), F32), pltpu.VMEM((S, LANE), F32)],
        compiler_params=_cparams(),
    )(q, kv, kpe, o, lse, do)


def exchange(name, gathers, a2as):
    n_g, n = len(gathers), len(gathers) + len(a2as)

    def body(*refs):
        ins, outs = refs[:n], refs[n:2 * n]
        send_sems, recv_sems, loc_sems = refs[2 * n:]
        x, y, c = lax.axis_index("x"), lax.axis_index("y"), lax.axis_index("c")
        me = 4 * x + 2 * y + c

        def peer(k):
            px = 1 - x if k & 4 else x
            py = 1 - y if k & 2 else y
            pc = 1 - c if k & 1 else c
            return (px, py, pc), 4 * px + 2 * py + pc

        def remote(a, k):
            pid, pflat = peer(k)
            src = ins[a] if a < n_g else ins[a].at[pflat]
            return pltpu.make_async_remote_copy(
                src_ref=src, dst_ref=outs[a].at[me], send_sem=send_sems.at[a, k - 1], recv_sem=recv_sems.at[a, k - 1],
                device_id=pid, device_id_type=MESH)

        def arrival(a, k):
            pid, pflat = peer(k)
            src = ins[a] if a < n_g else ins[a].at[pflat]
            return pltpu.make_async_remote_copy(
                src_ref=src, dst_ref=outs[a].at[pflat], send_sem=send_sems.at[a, k - 1], recv_sem=recv_sems.at[a, k - 1],
                device_id=pid, device_id_type=MESH)

        local = []
        for a in range(n):
            own = ins[a] if a < n_g else ins[a].at[me]
            cp = pltpu.make_async_copy(own, outs[a].at[me], loc_sems.at[a])
            cp.start()
            local.append(cp)
        sent = []
        for k in (1, 2, 4, 3, 5, 6, 7):
            for a in range(n):
                cp = remote(a, k)
                cp.start()
                sent.append(cp)
        for k in range(1, 8):
            for a in range(n):
                arrival(a, k).wait_recv()
        for cp in sent:
            cp.wait_send()
        for cp in local:
            cp.wait()

    out_shape = [jax.ShapeDtypeStruct((NDEV,) + g.shape, g.dtype) for g in gathers]
    out_shape += [jax.ShapeDtypeStruct(a.shape, a.dtype) for a in a2as]
    any_spec = pl.BlockSpec(memory_space=pl.ANY)
    return pl.pallas_call(
        body, name=name, in_specs=[any_spec] * n, out_specs=[any_spec] * n, out_shape=out_shape,
        scratch_shapes=[pltpu.SemaphoreType.DMA((n, NDEV - 1)), pltpu.SemaphoreType.DMA((n, NDEV - 1)),
                        pltpu.SemaphoreType.DMA((n,))],
    )(*gathers, *a2as)


def gather_two_level(name, block, dep):
    def body(x_ref, _, out_ref, stage, send_sems, recv_sems, loc_sem):
        x, y, c = lax.axis_index("x"), lax.axis_index("y"), lax.axis_index("c")
        me, sibling = (x, y, c), (x, y, 1 - c)
        chips = [(1 - x, y), (x, 1 - y), (1 - x, 1 - y)]

        def slot(px, py, pc):
            return out_ref.at[4 * px + 2 * py + pc]

        def copy(k, owner, to, src=None):
            return pltpu.make_async_remote_copy(
                src_ref=slot(*owner) if src is None else src, dst_ref=slot(*owner), send_sem=send_sems.at[k],
                recv_sem=recv_sems.at[k], device_id=to, device_id_type=MESH)

        load = pltpu.make_async_copy(x_ref, stage, loc_sem)
        load.start()
        first = [copy(0, me, sibling, src=x_ref)] + [copy(1 + j, me, (*chip, c), src=x_ref) for j, chip in enumerate(chips)]
        for cp in first:
            cp.start()
        load.wait()
        store = pltpu.make_async_copy(stage, slot(*me), loc_sem)
        store.start()
        passed = [copy(4 + j, (*chip, c), sibling) for j, chip in enumerate(chips)]
        for j, chip in enumerate(chips):
            copy(1 + j, (*chip, c), me).wait_recv()
            passed[j].start()
        copy(0, sibling, me).wait_recv()
        for j, chip in enumerate(chips):
            copy(4 + j, (*chip, 1 - c), me).wait_recv()
        for cp in first + passed:
            cp.wait_send()
        store.wait()

    return pl.pallas_call(
        body, name=name, in_specs=[pl.BlockSpec(memory_space=pl.ANY)] * 2, out_specs=pl.BlockSpec(memory_space=pl.ANY),
        out_shape=jax.ShapeDtypeStruct((NDEV,) + block.shape, block.dtype),
        scratch_shapes=[pltpu.VMEM(block.shape, block.dtype), pltpu.SemaphoreType.DMA((NDEV - 1,)),
                        pltpu.SemaphoreType.DMA((NDEV - 1,)), pltpu.SemaphoreType.DMA],
        compiler_params=_cparams(),
    )(block, dep)


def _peer(k, x, y, c):
    px = 1 - x if k & 4 else x
    py = 1 - y if k & 2 else y
    pc = 1 - c if k & 1 else c
    return (px, py, pc), 4 * px + 2 * py + pc


PEER_ORDER = (1, 2, 4, 3, 5, 6, 7)
HBM_SPEC = pl.BlockSpec(memory_space=pltpu.HBM)
SEM_SPEC = pl.BlockSpec(memory_space=pltpu.SEMAPHORE)
ANY_SPEC = pl.BlockSpec(memory_space=pl.ANY)


def _split_copies(ins, lands, n_g, send_sems, recv_sems):
    x, y, c = lax.axis_index("x"), lax.axis_index("y"), lax.axis_index("c")
    me = 4 * x + 2 * y + c

    def outgoing(a, k):
        pid, pflat = _peer(k, x, y, c)
        src = ins[a] if a < n_g else ins[a].at[pflat]
        return pltpu.make_async_remote_copy(
            src_ref=src, dst_ref=lands[a].at[me], send_sem=send_sems.at[a * (NDEV - 1) + k - 1],
            recv_sem=recv_sems.at[a * (NDEV - 1) + k - 1],
            device_id=pid, device_id_type=MESH)

    def arrival(a, k):
        pid, pflat = _peer(k, x, y, c)
        src = ins[a] if a < n_g else ins[a].at[pflat]
        return pltpu.make_async_remote_copy(
            src_ref=src, dst_ref=lands[a].at[pflat], send_sem=send_sems.at[a * (NDEV - 1) + k - 1],
            recv_sem=recv_sems.at[a * (NDEV - 1) + k - 1],
            device_id=pid, device_id_type=MESH)

    return outgoing, arrival


def exchange_begin(name, srcs, n_g, dep):
    n = len(srcs)
    land_shapes = [((NDEV,) + s.shape) if a < n_g else s.shape for a, s in enumerate(srcs)]

    def own_body(*refs):
        ins, outs = refs[:n], refs[n + 1:2 * n + 1]
        stage, sems = refs[2 * n + 1:3 * n + 1], refs[-1]
        me = 4 * lax.axis_index("x") + 2 * lax.axis_index("y") + lax.axis_index("c")
        cps = [pltpu.make_async_copy(ins[a] if a < n_g else ins[a].at[me], stage[a], sems.at[a]) for a in range(n)]
        for cp in cps:
            cp.start()
        for cp in cps:
            cp.wait()
        cps = [pltpu.make_async_copy(stage[a], outs[a].at[me], sems.at[a]) for a in range(n)]
        for cp in cps:
            cp.start()
        for cp in cps:
            cp.wait()

    lands = pl.pallas_call(
        own_body, name=name + "_own", in_specs=[ANY_SPEC] * (n + 1), out_specs=[ANY_SPEC] * n,
        out_shape=[jax.ShapeDtypeStruct(sh, s.dtype) for sh, s in zip(land_shapes, srcs)],
        scratch_shapes=[pltpu.VMEM(sh[1:], s.dtype) for sh, s in zip(land_shapes, srcs)] + [pltpu.SemaphoreType.DMA((n,))],
        compiler_params=_cparams(),
    )(*srcs, dep)

    def start_body(*refs):
        ins, lz = refs[:n], refs[n:2 * n]
        send_sems, recv_sems, token = refs[2 * n], refs[2 * n + 1], refs[-1]
        outgoing, _ = _split_copies(ins, lz, n_g, send_sems, recv_sems)
        for k in PEER_ORDER:
            for a in range(n):
                outgoing(a, k).start()
        token[...] = jnp.zeros(token.shape, F32)

    hbm = lambda t: pltpu.HBM(t.shape, t.dtype)
    res = pl.pallas_call(
        start_body, name=name + "_start",
        out_shape=(pltpu.SemaphoreType.DMA((n * (NDEV - 1),)), pltpu.SemaphoreType.DMA((n * (NDEV - 1),)),
                   *[hbm(s) for s in srcs], *[hbm(t) for t in lands], jax.ShapeDtypeStruct((8, LANE), F32)),
        in_specs=[HBM_SPEC] * (2 * n),
        out_specs=(SEM_SPEC, SEM_SPEC, *[HBM_SPEC] * (2 * n), pl.BlockSpec(memory_space=pltpu.VMEM)),
        input_output_aliases={i: 2 + i for i in range(2 * n)},
        compiler_params=pltpu.CompilerParams(has_side_effects=pltpu.SideEffectType.DATAFLOW_SIDE_EFFECTING),
    )(*[pltpu.with_memory_space_constraint(t, pltpu.HBM) for t in list(srcs) + list(lands)])
    return (name, n, n_g, res[:-1]), res[-1]


def exchange_end(handle, after):
    name, n, n_g, (send_sems, recv_sems, *bufs) = handle

    def wait_body(*refs):
        ins, lz = refs[:n], refs[n:2 * n]
        ss, rs = refs[2 * n], refs[2 * n + 1]
        outgoing, arrival = _split_copies(ins, lz, n_g, ss, rs)
        for k in range(1, NDEV):
            for a in range(n):
                arrival(a, k).wait_recv()
        for k in range(1, NDEV):
            for a in range(n):
                outgoing(a, k).wait_send()

    res = pl.pallas_call(
        wait_body, name=name + "_wait", out_shape=tuple(pltpu.HBM(t.shape, t.dtype) for t in bufs),
        in_specs=[HBM_SPEC] * (2 * n) + [SEM_SPEC, SEM_SPEC, ANY_SPEC], out_specs=[HBM_SPEC] * (2 * n),
        input_output_aliases={i: i for i in range(2 * n)},
        compiler_params=pltpu.CompilerParams(has_side_effects=pltpu.SideEffectType.DATAFLOW_SIDE_EFFECTING),
    )(*bufs, send_sems, recv_sems, after)
    return list(res[n:])


def _pick_rows(R, mult, cap):
    best = None
    for n in range(1, R + 1):
        if R % n == 0 and (R // n) % mult == 0 and R // n <= cap:
            best = R // n
            break
    assert best is not None, (R, mult, cap)
    return best


def sum_slots(name, x):
    _, R, _ = x.shape
    tr = _pick_rows(R, 16, 2304)

    def body(x_ref, o_ref):
        acc = x_ref[0].astype(F32)
        for d in range(1, NDEV):
            acc = acc + x_ref[d].astype(F32)
        o_ref[...] = acc

    return pl.pallas_call(
        body, name=name, grid=(R // tr,),
        in_specs=[pl.BlockSpec((NDEV, tr, LANE), lambda i: (0, i, 0))],
        out_specs=pl.BlockSpec((tr, LANE), lambda i: (i, 0)),
        out_shape=jax.ShapeDtypeStruct((R, LANE), F32), compiler_params=_cparams(),
    )(x)


def adamw(name, w, g, m, v):
    L, R, C = w.shape
    tr = _pick_rows(R, 8, 256) if R % 8 == 0 else R

    def body(w_ref, g_ref, m_ref, v_ref, d_ref, nm_ref, nv_ref):
        gg = g_ref[...]
        nm = B1 * m_ref[...] + (1.0 - B1) * gg
        nv = B2 * v_ref[...] + (1.0 - B2) * jnp.square(gg)
        m_hat = nm / (1.0 - B1 ** STEP)
        v_hat = nv / (1.0 - B2 ** STEP)
        d_ref[...] = -LR * (m_hat / (jnp.sqrt(v_hat) + EPS) + WD * w_ref[...])
        nm_ref[...] = nm
        nv_ref[...] = nv

    blk = pl.BlockSpec((1, tr, C), lambda l, i: (l, i, 0))
    shp = jax.ShapeDtypeStruct(w.shape, F32)
    return pl.pallas_call(
        body, name=name, grid=(L, R // tr), in_specs=[blk] * 4, out_specs=[blk] * 3, out_shape=[shp] * 3,
        compiler_params=_cparams(),
    )(w, g, m, v)


IN_SHARD = D_IN // NDEV
UQ_SHARD = HEADS * (NOPE + ROPE) // NDEV
W_IN_PAD = 1024
ROW_A, ROW_B, ROW_C, ROW_UKV, ROW_UQ, MISC_ROWS = 0, 512, 1024, 1536, 1792, 2176


def _in_perm_index():
    ar = np.arange
    z = lambda n: np.full((n,), -1, np.int64)
    mix = lambda lo1, lo2: np.concatenate([ar(lo + LANE * j, lo + LANE * (j + 1)) for j in range(CW // LANE)
                                           for lo in (lo1, lo2)])
    return np.concatenate([ar(4768, 7840), mix(0, 512), ar(1024, 1536), mix(1536, 2560), ar(4256, 4768), ar(2048, 2560),
                           ar(3072, 3584), ar(3968, 4224), ar(4224, 4256), z(OFF_Q - OFF_KR - ROPE), ar(3584, 3968),
                           z(NP - OFF_Q - QL)])


def _head_perm_index(a, b):
    parts = []
    for g in range(QUADS):
        h = np.arange(4 * g, 4 * g + 4)[:, None] * (a + b)
        parts += [(h + np.arange(a)[None]).reshape(-1), (h + a + np.arange(b)[None]).reshape(-1)]
    return np.concatenate(parts)


def _inverse(perm, n):
    inv = np.full((n,), -1, np.int64)
    inv[perm[perm >= 0]] = np.nonzero(perm >= 0)[0]
    return inv


IN_PERM = _in_perm_index()
UQ_PERM = _head_perm_index(NOPE, ROPE)
UKV_PERM = _head_perm_index(NOPE, VH)


def _to_gathered(perm, shard, pad):
    return np.where(perm >= 0, (perm // shard) * pad + perm % shard, -1)


def _from_full(inv, shard, pad):
    j, i = np.divmod(np.arange(NDEV * pad), pad)
    return np.where(i < shard, inv[np.minimum(j * shard + i, inv.shape[0] - 1)], -1)


def col_gather(name, srcs, out_shapes, jobs, deps=()):
    ns, nj, nd, no = len(srcs), len(jobs), len(deps), len(out_shapes)
    tables = [jnp.asarray(np.asarray(job[5], np.int32)[None, :]) for job in jobs]

    def view(ref, col0, width, r0, rc):
        n = ref.shape[-1]
        if len(ref.shape) == 3:
            return ref.at[col0 // n, pl.ds(r0, rc), pl.ds(col0 % n, width)]
        return ref.at[pl.ds(r0, rc), pl.ds(col0, width)]

    def slabs(shape):
        if len(shape) == 3:
            return [((d,), d * shape[2], (d + 1) * shape[2]) for d in range(shape[0])]
        w = 1024 if shape[1] > 1024 and shape[1] % 1024 == 0 else shape[1]
        return [((slice(None), pl.ds(c, w)), c, c + w) for c in range(0, shape[1], w)]

    src_slabs = [slabs(s.shape) for s in srcs]
    out_slabs = [slabs(sh) for sh in out_shapes]
    work, first_use, last_touch = [], {}, {}
    for ji, (si, srow, oi, orow, nrows, tgt) in enumerate(jobs):
        tgt = np.asarray(tgt)
        tw = 256 if out_shapes[oi][-1] % 256 == 0 else LANE
        sw = 256 if srcs[si].shape[-1] % 256 == 0 else LANE
        for t in range(tgt.shape[0] // tw):
            tt = tgt[t * tw:(t + 1) * tw]
            tiles = sorted(set((tt[tt >= 0] // sw).tolist()))
            straight = bool(tiles) and tt[0] >= 0 and tt[0] % LANE == 0 and np.array_equal(tt, tt[0] + np.arange(tw))
            cols = [(int(tt[0]) + k * LANE, LANE) for k in range(tw // LANE)] if straight else [(s * sw, sw) for s in tiles]
            need = sorted({(si, k) for c0, _ in cols for k, (_, lo, hi) in enumerate(src_slabs[si]) if lo <= c0 < hi})
            touch = [(oi, k) for k, (_, lo, hi) in enumerate(out_slabs[oi]) if lo <= t * tw < hi][0]
            for key in need:
                first_use.setdefault(key, len(work))
            last_touch[touch] = len(work)
            work.append((ji, t, tw, sw, tiles, straight, need, touch))
    in_order = sorted(first_use, key=first_use.get)
    in_sem = {key: i for i, key in enumerate(in_order)}
    out_keys = sorted(last_touch)
    out_sem = {key: i for i, key in enumerate(out_keys)}

    def body(*refs):
        src_hbm, tab_refs = refs[:ns], refs[ns:ns + nj]
        out_hbm = refs[ns + nj + nd:ns + nj + nd + no]
        scratch = refs[ns + nj + nd + no:]
        src_refs, out_refs, in_sems, out_sems = scratch[:ns], scratch[ns:ns + no], scratch[-2], scratch[-1]
        loads = {}
        for key in in_order:
            si, k = key
            idx = src_slabs[si][k][0]
            loads[key] = pltpu.make_async_copy(src_hbm[si].at[idx], src_refs[si].at[idx], in_sems.at[in_sem[key]])
            loads[key].start()
        arrived, stores = set(), []
        for wi, (ji, t, tw, sw, tiles, straight, need, touch) in enumerate(work):
            si, srow, oi, orow, nrows, tgt = jobs[ji]
            sref, oref = src_refs[si], out_refs[oi]
            rc = nrows if nrows <= 1024 else 1024
            for key in need:
                if key not in arrived:
                    loads[key].wait()
                    arrived.add(key)
            onehots = []
            if tiles and not straight:
                want = tab_refs[ji][:, t * tw:(t + 1) * tw]
                row = lax.broadcasted_iota(jnp.int32, (sw, tw), 0)
                onehots = [jnp.where(want == row + s * sw, 1.0, 0.0).astype(BF) for s in tiles]
            first = int(np.asarray(tgt)[t * tw])

            def chunk(ci, _, t=t, tw=tw, sw=sw, tiles=tiles, straight=straight, onehots=onehots, first=first,
                      sref=sref, oref=oref, srow=srow, orow=orow, rc=rc):
                r0 = ci * rc
                ro = pl.multiple_of(orow + r0, LANE)
                rs = pl.multiple_of(srow + r0, LANE)
                if not tiles:
                    view(oref, t * tw, tw, ro, rc)[...] = jnp.zeros((rc, tw), BF)
                elif straight:
                    for k in range(tw // LANE):
                        view(oref, t * tw + k * LANE, LANE, ro, rc)[...] = view(sref, first + k * LANE, LANE, rs, rc)[...]
                else:
                    acc = None
                    for s, oh in zip(tiles, onehots):
                        p = jnp.dot(view(sref, s * sw, sw, rs, rc)[...], oh, preferred_element_type=F32)
                        acc = p if acc is None else acc + p
                    view(oref, t * tw, tw, ro, rc)[...] = acc.astype(BF)
                return 0

            lax.fori_loop(0, nrows // rc, chunk, 0)
            if last_touch[touch] == wi:
                idx = out_slabs[touch[0]][touch[1]][0]
                cp = pltpu.make_async_copy(out_refs[touch[0]].at[idx], out_hbm[touch[0]].at[idx], out_sems.at[out_sem[touch]])
                cp.start()
                stores.append(cp)
        for cp in stores:
            cp.wait()

    return pl.pallas_call(
        body, name=name, in_specs=[ANY_SPEC] * ns + [pl.BlockSpec(memory_space=pltpu.VMEM)] * nj + [ANY_SPEC] * nd,
        out_specs=[ANY_SPEC] * no, out_shape=[jax.ShapeDtypeStruct(s, BF) for s in out_shapes],
        scratch_shapes=[pltpu.VMEM(s.shape, BF) for s in srcs] + [pltpu.VMEM(s, BF) for s in out_shapes]
        + [pltpu.SemaphoreType.DMA((len(in_order),)), pltpu.SemaphoreType.DMA((len(out_keys),))],
        compiler_params=_cparams(),
    )(*srcs, *tables, *deps)


def sum_adamw(name, recvs, w, m, v, lo=0, prev=None, row0=0):
    _, R, C = w.shape
    L = len(recvs)
    CP = recvs[0].shape[-1]
    tr = _pick_rows(R, 16, 128)
    n_prev = 0 if prev is None else 4

    def body(*refs):
        r_refs = refs[:L]
        w_ref, m_ref, v_ref = refs[L:L + 3]
        g_ref, d_ref, nm_ref, nv_ref, gsum = refs[L + 3 + n_prev:]
        layer = pl.program_id(0)
        for k in range(L):
            def total(k=k):
                acc = r_refs[k][0].astype(F32)
                for d in range(1, NDEV):
                    acc = acc + r_refs[k][d].astype(F32)
                gsum[...] = acc
            pl.when(layer == k)(total)
        gg = gsum[:, 0:C]
        nm = B1 * m_ref[...] + (1.0 - B1) * gg
        nv = B2 * v_ref[...] + (1.0 - B2) * jnp.square(gg)
        m_hat = nm / (1.0 - B1 ** STEP)
        v_hat = nv / (1.0 - B2 ** STEP)
        g_ref[...] = gg
        d_ref[...] = -LR * (m_hat / (jnp.sqrt(v_hat) + EPS) + WD * w_ref[...])
        nm_ref[...] = nm
        nv_ref[...] = nv

    assert row0 % tr == 0
    r_specs = [pl.BlockSpec((NDEV, tr, CP),
                            functools.partial(lambda l, i, k: (0, row0 // tr + jnp.where(l == k, i, 0), 0), k=k))
               for k in range(L)]
    blk = pl.BlockSpec((None, tr, C), lambda l, i: (l + lo, i, 0))
    shp = jax.ShapeDtypeStruct(w.shape, F32)
    return pl.pallas_call(
        body, name=name, grid=(L, R // tr), in_specs=r_specs + [blk] * 3 + [ANY_SPEC] * n_prev, out_specs=[blk] * 4,
        out_shape=[shp] * 4, input_output_aliases={L + 3 + i: i for i in range(n_prev)},
        scratch_shapes=[pltpu.VMEM((tr, CP), F32)], compiler_params=_cparams(),
    )(*recvs, w, m, v, *(prev or ()))


ALPHA = 8.0 ** 0.25
T_WIDE, T_NARROW = 512, 1024


def _rope_fn(sign):
    def fn(x, cos, sin):
        W = x.shape[-1]
        lane = lax.broadcasted_iota(jnp.int32, x.shape, 1)
        first_half = (lane % ROPE) < (ROPE // 2)
        rot = jnp.where(first_half, -pltpu.roll(x, W - ROPE // 2, 1), pltpu.roll(x, ROPE // 2, 1))
        return x * cos + sign * rot * sin
    return fn


def layer_fwd(x, ada3, W, tabs, S):
    cos, sin = tabs
    T = T_NARROW
    u = rowwise("modulate", lambda xv, a: xv * (1.0 + a[1:2, :]) + a[0:1, :], S, T,
                [(x, D_MODEL, 0)], [ada3], [(D_MODEL, BF)])[0]
    proj = mm(u, W["in"], name="mm_proj", tm=1024, tn=1024, out_dtype=BF)
    W = {**W, **W["late"](proj)}

    ca = conv_fwd("conv_a_fwd", proj, OFF_A, W["conv_a"], 31, "glu", S, CW)

    def a_post(c, ag, vec):
        n, _ = _ln_stats(c + vec[0:1, :])
        return _silu(n * vec[1:2, :] + vec[2:3, :]) * _silu(ag)

    h_a = rowwise("mix_a_post", a_post, S, T, [(ca, CW, 0), (proj, CW, OFF_AG)], [W["vec_a"]], [(CW, BF)])[0]
    y_a = mm(h_a, W["a_out"], name="mm_branch_out", out_dtype=BF)

    cb = conv_fwd("conv_b_fwd", proj, OFF_B, W["conv_b"], 3, "mul", S, CW)
    h_b = rowwise("mix_b_post", lambda c, gb, bg: gb * c * _silu(bg), S, T,
                  [(cb, CW, 0), (proj, CW, OFF_GB), (proj, CW, OFF_BG)], [], [(CW, BF)])[0]
    y_b = mm(h_b, W["b_out"], name="mm_branch_out", out_dtype=BF)

    def rms2(ql, kvl, gq, gkv):
        rq = lax.rsqrt(jnp.mean(ql * ql, axis=-1, keepdims=True) + RMS_EPS)
        rk = lax.rsqrt(jnp.mean(kvl * kvl, axis=-1, keepdims=True) + RMS_EPS)
        return ql * rq * gq, kvl * rk * gkv

    qn, kvn = rowwise("rms_fwd", rms2, S, T, [(proj, QL, OFF_Q), (proj, KVL, OFF_KV)], [W["gq"], W["gkv"]],
                      [(QL, BF), (KVL, BF)])
    q = mm(qn, W["uq"], name="mm_q")
    kv = mm(kvn, W["ukv"], name="mm_kv", out_dtype=BF)
    rope = _rope_fn(1.0)

    def rope_fwd(qv, kr, c1, s1):
        parts = []
        for g in range(QUADS):
            parts.append(qv[:, g * QW:g * QW + 2 * LANE].astype(BF))
            parts.append(rope(qv[:, g * QW + 2 * LANE:(g + 1) * QW], c1, s1).astype(BF))
        kp = rope(kr, c1, s1)
        kp = kp + pltpu.roll(kp, ROPE, 1) + pltpu.roll(kp, 2 * ROPE, 1) + pltpu.roll(kp, 3 * ROPE, 1)
        return jnp.concatenate(parts, axis=1), kp

    q_b, kpe = rowwise("rope_fwd", rope_fwd, S, T,
                       [(q, HEADS * (NOPE + ROPE), 0), (proj, LANE, OFF_KR), (cos, LANE, 0), (sin, LANE, 0)], [],
                       [(HEADS * (NOPE + ROPE), BF), (LANE, BF)])
    o, lse = attn_fwd(q_b, kv, kpe, S)
    h_c = rowwise("mix_c_post", lambda ov, cg: ov * _silu(cg), S, T, [(o, CW, 0), (proj, CW, OFF_CG)], [],
                  [(CW, BF)])[0]
    y_c = mm(h_c, W["c_out"], name="mm_branch_out", out_dtype=BF)

    def merge(la, lb, lc, ya, yb, yc):
        return _sigmoid(la) * ya + _sigmoid(lb) * yb + _sigmoid(lc) * yc

    m = rowwise("merge_fwd", merge, S, T_WIDE,
                [(proj, D_MODEL, 0), (proj, D_MODEL, 1024), (proj, D_MODEL, 2048), (y_a, D_MODEL, 0),
                 (y_b, D_MODEL, 0), (y_c, D_MODEL, 0)], [], [(D_MODEL, BF)])[0]
    out = mm(m, W["o"], name="mm_out")

    def ln_fwd(xv, ov, a, lnv):
        n, _ = _ln_stats(ALPHA * xv + a[2:3, :] * ov)
        return n * lnv[0:1, :] + lnv[1:2, :]

    x_next = rowwise("ln_fwd", ln_fwd, S, T_WIDE, [(x, D_MODEL, 0), (out, D_MODEL, 0)], [ada3, W["lnv"]],
                     [(D_MODEL, F32)])[0]
    saved = dict(x=x, u=u, proj=proj, ca=ca, cb=cb, h_a=h_a, h_b=h_b, h_c=h_c, y_a=y_a, y_b=y_b, y_c=y_c, qn=qn,
                 kvn=kvn, q_b=q_b, kv=kv, kpe=kpe, lse=lse, o=o, m=m, out=out)
    return x_next, saved, W


def layer_bwd(dxn, sv, ada3, W, tabs, S, before_in=None):
    cos, sin = tabs
    T = T_NARROW
    x, proj = sv["x"], sv["proj"]
    G = {}

    def ln_bwd(xv, ov, dy, a, lnv):
        gate = a[2:3, :]
        n, rstd = _ln_stats(ALPHA * xv + gate * ov)
        dr = _ln_bwd(dy * lnv[0:1, :], n, rstd)
        return ALPHA * dr, gate * dr, _colsum(dy * n), _colsum(dy), _colsum(dr * ov)

    dres, d_out, G["ln_g"], G["ln_b"], d_gate = rowwise(
        "ln_bwd", ln_bwd, S, T_WIDE, [(x, D_MODEL, 0), (sv["out"], D_MODEL, 0), (dxn, D_MODEL, 0)], [ada3, W["lnv"]],
        [(D_MODEL, F32), (D_MODEL, BF)], [D_MODEL] * 3)
    dm = mm(d_out, W["o"], name="mm_dm", trans_b=True, out_dtype=BF)
    G["w_o"] = mm(sv["m"], d_out, name="mm_gw_o", trans_a=True, out_dtype=BF)

    def merge_bwd(dmv, la, lb, lc, ya, yb, yc):
        outs, dls = [], []
        for lg, yv in ((la, ya), (lb, yb), (lc, yc)):
            s = _sigmoid(lg)
            outs.append(dmv * s)
            dls.append((dmv * yv * s * (1.0 - s)).astype(BF))
        return (jnp.concatenate(dls, axis=1),) + tuple(outs)

    d_proj, dy_a, dy_b, dy_c = rowwise(
        "merge_bwd", merge_bwd, S, T_WIDE,
        [(dm, D_MODEL, 0), (proj, D_MODEL, 0), (proj, D_MODEL, 1024), (proj, D_MODEL, 2048), (sv["y_a"], D_MODEL, 0),
         (sv["y_b"], D_MODEL, 0), (sv["y_c"], D_MODEL, 0)], [], [(3 * D_MODEL, BF)] + [(D_MODEL, BF)] * 3,
        into=(None, NP, OFF_M))

    dh = {}
    for br, dy in (("a", dy_a), ("b", dy_b), ("c", dy_c)):
        dh[br] = mm(dy, W[br + "_out"], name="mm_dh", trans_b=True, out_dtype=BF)
        G["w_%s_out" % br] = mm(sv["h_" + br], dy, name="mm_gw_branch", trans_a=True, out_dtype=BF)

    def a_post_bwd(c, ag, dhv, vec):
        n, rstd = _ln_stats(c + vec[0:1, :])
        z = n * vec[1:2, :] + vec[2:3, :]
        d_ag = dhv * _silu(z) * _dsilu(ag)
        dz = dhv * _silu(ag) * _dsilu(z)
        dc = _ln_bwd(dz * vec[1:2, :], n, rstd)
        return d_ag, dc, _colsum(dc), _colsum(dz * n), _colsum(dz)

    d_proj, dca, G["conv_a_b"], G["ln_a_g"], G["ln_a_b"] = rowwise(
        "mix_a_post_bwd", a_post_bwd, S, T, [(sv["ca"], CW, 0), (proj, CW, OFF_AG), (dh["a"], CW, 0)], [W["vec_a"]],
        [(CW, BF), (CW, F32)], [CW] * 3, into=(d_proj, NP, OFF_AG))
    d_proj, G["conv_a_w"] = conv_bwd("conv_a_bwd", proj, OFF_A, dca, W["conv_a"], 31, "glu", S, CW, d_proj)

    def b_post_bwd(c, gb, bg, dhv):
        sg = _silu(bg)
        d_gb_bg = jnp.concatenate([(dhv * sg * c).astype(BF), (dhv * gb * c * _dsilu(bg)).astype(BF)], axis=1)
        return d_gb_bg, dhv * sg * gb

    d_proj, dcb = rowwise("mix_b_post_bwd", b_post_bwd, S, T,
                          [(sv["cb"], CW, 0), (proj, CW, OFF_GB), (proj, CW, OFF_BG), (dh["b"], CW, 0)], [],
                          [(2 * CW, BF), (CW, F32)], into=(d_proj, NP, OFF_GB))
    d_proj, G["conv_b_w"] = conv_bwd("conv_b_bwd", proj, OFF_B, dcb, W["conv_b"], 3, "mul", S, CW, d_proj)

    d_proj, d_o = rowwise("mix_c_post_bwd", lambda ov, cg, dhv: (dhv * ov * _dsilu(cg), dhv * _silu(cg)), S, T,
                          [(sv["o"], CW, 0), (proj, CW, OFF_CG), (dh["c"], CW, 0)], [], [(CW, BF), (CW, F32)],
                          into=(d_proj, NP, OFF_CG))
    dq, d_kv, dkp_heads = attn_bwd(sv["q_b"], sv["kv"], sv["kpe"], sv["o"], sv["lse"], d_o, S)
    ropeT = _rope_fn(-1.0)

    def rope_bwd(dqv, dkp, c1, s1):
        parts = []
        for g in range(QUADS):
            parts.append(dqv[:, g * QW:g * QW + 2 * LANE].astype(BF))
            parts.append(ropeT(dqv[:, g * QW + 2 * LANE:(g + 1) * QW], c1, s1).astype(BF))
        f = dkp[:, :LANE] + dkp[:, LANE:]
        f = f + pltpu.roll(f, 64, 1)
        f = f + pltpu.roll(f, 32, 1)
        lane = lax.broadcasted_iota(jnp.int32, f.shape, 1)
        return jnp.concatenate(parts, axis=1), jnp.where(lane < ROPE, ropeT(f, c1, s1), 0.0)

    d_q, dk_pe = rowwise("rope_bwd", rope_bwd, S, T,
                         [(dq, HEADS * (NOPE + ROPE), 0), (dkp_heads, HEADS * ROPE, 0), (cos, LANE, 0), (sin, LANE, 0)],
                         [], [(HEADS * (NOPE + ROPE), BF), (LANE, BF)])
    d_qn = mm(d_q, W["uq"], name="mm_dqn", trans_b=True, out_dtype=BF)
    d_kvn = mm(d_kv, W["ukv"], name="mm_dkvn", trans_b=True, out_dtype=BF)
    G["w_uq"] = mm(sv["qn"], d_q, name="mm_gw_uq", trans_a=True, out_dtype=BF)
    G["w_ukv"] = mm(sv["kvn"], d_kv, name="mm_gw_ukv", trans_a=True, out_dtype=BF)

    def rms_bwd(ql, kvl, dqn, dkn, dkp, gq, gkv):
        res = []
        for xv, dy, g in ((ql, dqn, gq), (kvl, dkn, gkv)):
            r = lax.rsqrt(jnp.mean(xv * xv, axis=-1, keepdims=True) + RMS_EPS)
            dxh = dy * g
            res.append(((r * (dxh - xv * (r * r) * jnp.mean(dxh * xv, axis=-1, keepdims=True))).astype(BF),
                        _colsum(dy * xv * r)))
        pad = jnp.zeros((ql.shape[0], LANE), BF)
        return jnp.concatenate([res[1][0], dkp, pad, res[0][0], pad], axis=1), res[0][1], res[1][1]

    d_proj, G["q_norm_g"], G["kv_norm_g"] = rowwise(
        "rms_bwd", rms_bwd, S, T,
        [(proj, QL, OFF_Q), (proj, KVL, OFF_KV), (d_qn, QL, 0), (d_kvn, KVL, 0), (dk_pe, LANE, 0)],
        [W["gq"], W["gkv"]], [(NP - OFF_KV, BF)], [QL, KVL], into=(d_proj, NP, OFF_KV))
    deps = before_in(G) if before_in is not None else ()
    du = mm(d_proj, W["in"], name="mm_du", trans_b=True, tk=2048, deps=deps)
    G["w_in"] = mm(sv["u"], d_proj, name="mm_gw_in", trans_a=True, out_dtype=BF, deps=deps)

    def mod_bwd(duv, xv, dr, a):
        return duv * (1.0 + a[1:2, :]) + dr, _colsum(duv), _colsum(duv * xv)

    dx, d_shift, d_scale = rowwise("mod_bwd", mod_bwd, S, T_WIDE, [(du, D_MODEL, 0), (x, D_MODEL, 0), (dres, D_MODEL, 0)],
                                   [ada3], [(D_MODEL, F32)], [D_MODEL] * 2)
    d_ada = jnp.concatenate([d_shift, d_scale, d_gate], axis=1)
    return dx, G, d_ada


SMALL = ("conv_a_b", "ln_a_g", "ln_a_b", "q_norm_g", "kv_norm_g", "ln_g", "ln_b")


def _rows(v):
    n = v.shape[0]
    r = -(-n // (LANE * 16)) * 16
    return jnp.pad(v, (0, r * LANE - n)).reshape(r, LANE)


def kernel(x, c, positions, w_ada, b_ada, w_in, conv_a_w, conv_a_b, ln_a_g, ln_a_b, w_a_out, conv_b_w, w_b_out, q_norm_g, kv_norm_g, w_uq, w_ukv, w_c_out, w_o, ln_g, ln_b, loss_target, m_w_ada, m_b_ada, m_w_in, m_conv_a_w, m_conv_a_b, m_ln_a_g, m_ln_a_b, m_w_a_out, m_conv_b_w, m_w_b_out, m_q_norm_g, m_kv_norm_g, m_w_uq, m_w_ukv, m_w_c_out, m_w_o, m_ln_g, m_ln_b, v_w_ada, v_b_ada, v_w_in, v_conv_a_w, v_conv_a_b, v_ln_a_g, v_ln_a_b, v_w_a_out, v_conv_b_w, v_w_b_out, v_q_norm_g, v_kv_norm_g, v_w_uq, v_w_ukv, v_w_c_out, v_w_o, v_ln_g, v_ln_b):
    P = dict(w_ada=w_ada, b_ada=b_ada, w_in=w_in, conv_a_w=conv_a_w, conv_a_b=conv_a_b, ln_a_g=ln_a_g, ln_a_b=ln_a_b,
             w_a_out=w_a_out, conv_b_w=conv_b_w, w_b_out=w_b_out, q_norm_g=q_norm_g, kv_norm_g=kv_norm_g, w_uq=w_uq,
             w_ukv=w_ukv, w_c_out=w_c_out, w_o=w_o, ln_g=ln_g, ln_b=ln_b)
    Mo = dict(w_ada=m_w_ada, b_ada=m_b_ada, w_in=m_w_in, conv_a_w=m_conv_a_w, conv_a_b=m_conv_a_b, ln_a_g=m_ln_a_g,
              ln_a_b=m_ln_a_b, w_a_out=m_w_a_out, conv_b_w=m_conv_b_w, w_b_out=m_w_b_out, q_norm_g=m_q_norm_g,
              kv_norm_g=m_kv_norm_g, w_uq=m_w_uq, w_ukv=m_w_ukv, w_c_out=m_w_c_out, w_o=m_w_o, ln_g=m_ln_g, ln_b=m_ln_b)
    Vo = dict(w_ada=v_w_ada, b_ada=v_b_ada, w_in=v_w_in, conv_a_w=v_conv_a_w, conv_a_b=v_conv_a_b, ln_a_g=v_ln_a_g,
              ln_a_b=v_ln_a_b, w_a_out=v_w_a_out, conv_b_w=v_conv_b_w, w_b_out=v_w_b_out, q_norm_g=v_q_norm_g,
              kv_norm_g=v_kv_norm_g, w_uq=v_w_uq, w_ukv=v_w_ukv, w_c_out=v_w_c_out, w_o=v_w_o, ln_g=v_ln_g, ln_b=v_ln_b)
    ORDER = ("w_ada", "b_ada", "w_in", "conv_a_w", "conv_a_b", "ln_a_g", "ln_a_b", "w_a_out", "conv_b_w", "w_b_out",
             "q_norm_g", "kv_norm_g", "w_uq", "w_ukv", "w_c_out", "w_o", "ln_g", "ln_b")
    L = w_ada.shape[0]
    S = x.shape[1]
    me = 4 * lax.axis_index("x") + 2 * lax.axis_index("y") + lax.axis_index("c")
    x2 = x[0]
    tgt = loss_target[0]

    small_in = _rows(jnp.concatenate([c.reshape(-1), conv_a_w.reshape(-1), conv_b_w.reshape(-1)]))
    w_in_b = jnp.pad(w_in.astype(BF), ((0, 0), (0, 0), (0, W_IN_PAD - IN_SHARD)))
    misc_b = jnp.concatenate([w_a_out, w_b_out, w_c_out, w_ukv, jnp.pad(w_uq, ((0, 0), (0, 0), (0, LANE - UQ_SHARD)))],
                             axis=1).astype(BF)
    w_o_b = w_o.astype(BF)
    gathered = [None] * L
    sg = exchange("gather_small", [small_in], [])[0]
    sgf = sg.reshape(NDEV, -1)
    c_all = sgf[:, :D_MODEL]
    o1 = D_MODEL + L * 31 * 64
    conv_a_full = sgf[:, D_MODEL:o1].reshape(NDEV, L, 31, 64).transpose(1, 2, 0, 3).reshape(L, 31, CW)
    conv_b_full = sgf[:, o1:o1 + L * 3 * 64].reshape(NDEV, L, 3, 64).transpose(1, 2, 0, 3).reshape(L, 3, CW)

    c_act = rowwise("silu_c", _silu, 16, 16, [(jnp.pad(c_all, ((0, 8), (0, 0))), D_MODEL, 0)], [], [(D_MODEL, BF)])[0]
    ncol = w_ada.shape[2]
    w_ada_b = w_ada.astype(BF).transpose(1, 0, 2).reshape(D_MODEL, L * ncol)
    b_mine = lax.dynamic_slice_in_dim(b_ada, me * ncol, ncol, axis=1).reshape(1, L * ncol)
    ada_part = mm(c_act, w_ada_b, name="mm_ada", bias=b_mine)
    ada_rows = -(-(L * ncol) // (LANE * 8)) * 8
    ada_send = jnp.pad(ada_part[:NDEV].reshape(NDEV, -1, LANE), ((0, 0), (0, ada_rows - L * ncol // LANE), (0, 0)))
    ada_recv = exchange("a2a_ada", [], [ada_send])[0]
    ada = ada_recv[:, :L * ncol // LANE].reshape(NDEV, L, ncol).transpose(1, 0, 2).reshape(L, 3, D_MODEL)
    gathered[0] = [gather_two_level("gather0_w_in", w_in_b[0], ada)]
    pending_rest, rest_token = exchange_begin("gather0_rest", [misc_b[0], w_o_b[0]], 2, gathered[0][0])

    inv_freq = ROPE_THETA ** (-jnp.arange(0, ROPE, 2, dtype=F32) / ROPE)
    ang = positions[0].astype(F32)[:, None] * inv_freq
    tabs = (jnp.tile(jnp.cos(ang), (1, 2 * LANE // ROPE)), jnp.tile(jnp.sin(ang), (1, 2 * LANE // ROPE)))

    straight = np.arange(D_MODEL)
    fwd_in = [(0, 0, 0, 0, D_MODEL, _to_gathered(IN_PERM, IN_SHARD, W_IN_PAD))]
    fwd_misc = [(0, ROW_A, 0, 0, CW, straight), (0, ROW_B, 1, 0, CW, straight), (0, ROW_C, 2, 0, CW, straight),
                (0, ROW_UKV, 3, 0, KVL, UKV_PERM), (0, ROW_UQ, 4, 0, QL, _to_gathered(UQ_PERM, UQ_SHARD, LANE))]
    rev_in = [(0, 0, 0, 0, D_MODEL, _from_full(_inverse(IN_PERM, D_IN), IN_SHARD, W_IN_PAD))]
    rev_misc = [(0, 0, 0, ROW_A, CW, straight), (1, 0, 0, ROW_B, CW, straight), (2, 0, 0, ROW_C, CW, straight),
                (3, 0, 0, ROW_UKV, KVL, _from_full(_inverse(UKV_PERM, HEADS * (NOPE + VH)), LANE, LANE)),
                (4, 0, 0, ROW_UQ, QL, _from_full(_inverse(UQ_PERM, HEADS * (NOPE + ROPE)), UQ_SHARD, LANE))]

    def layer_weights(l, deps):
        w_in_p = col_gather("relayout_w_in", [gathered[l][0]], [(D_MODEL, NP)], fwd_in, deps)[0]

        def late(after):
            if len(gathered[l]) == 1:
                gathered[l] += exchange_end(pending_rest, after)
            _, g_misc, g_o = gathered[l]
            a_out, b_out, c_out, ukv, uq = col_gather(
                "relayout_misc", [g_misc],
                [(CW, D_MODEL)] * 3 + [(KVL, HEADS * (NOPE + VH)), (QL, HEADS * (NOPE + ROPE))], fwd_misc, deps)
            return {"a_out": a_out, "b_out": b_out, "c_out": c_out, "uq": uq, "ukv": ukv,
                    "o": g_o.reshape(D_MODEL, D_MODEL)}

        return {
            "in": w_in_p, "late": late,
            "conv_a": jnp.pad(conv_a_full[l], ((0, 1), (0, 0))), "conv_b": jnp.pad(conv_b_full[l], ((0, 5), (0, 0))),
            "vec_a": jnp.stack([conv_a_b[l], ln_a_g[l], ln_a_b[l]]), "gq": q_norm_g[l][None], "gkv": kv_norm_g[l][None],
            "lnv": jnp.stack([ln_g[l], ln_b[l]]),
        }

    h = x2
    saved, weights = [], []
    handles, token = {}, rest_token
    for l in range(1, L):
        handles[l], token = exchange_begin("gather%d" % l, [w_in_b[l], misc_b[l], w_o_b[l]], 3, token)
    for l in range(L):
        ada_l, deps = (ada[l] + token[0, 0], (token,)) if l == 0 else (ada[l], ())
        h, sv, Wl = layer_fwd(h, ada_l, layer_weights(l, deps), tabs, S)
        if l + 1 < L:
            gathered[l + 1] = exchange_end(handles[l + 1], h)
        saved.append(sv)
        weights.append(Wl)

    def loss_fn(y, t):
        e = y - t
        return e * (1.0 / D_MODEL), _colsum(e * e)

    dy, sq = rowwise("loss", loss_fn, S, 256, [(h, D_MODEL, 0), (tgt, D_MODEL, 0)], [], [(D_MODEL, F32)], [D_MODEL])
    loss = lax.psum(0.5 * jnp.sum(sq) / D_MODEL, ("x", "y", "c"))
    loss, dy = lax.optimization_barrier((loss, dy))

    grads, d_adas, recv = [None] * L, [None] * L, [None] * L
    pending, token = None, None

    def send_rest(g):
        send_misc = col_gather("unrelayout_misc", [g["w_a_out"], g["w_b_out"], g["w_c_out"], g["w_ukv"], g["w_uq"]],
                               [(NDEV, MISC_ROWS, LANE)], rev_misc)[0]
        return [send_misc, g["w_o"].reshape(NDEV, D_MODEL // NDEV, D_MODEL)]

    rest0 = []

    def early_rest(g):
        handle, tok = exchange_begin("scatter0_rest", send_rest(g), 0, g["w_o"])
        rest0.append(handle)
        return (tok,)

    for l in reversed(range(L)):
        ada_l = ada[l] if token is None else ada[l] + token[0, 0]
        dy, g, d_adas[l] = layer_bwd(dy, saved[l], ada_l, weights[l], tabs, S, early_rest if l == 0 else None)
        grads[l] = g
        if pending is not None:
            recv[l + 1] = exchange_end(pending, dy)
        send_in = col_gather("unrelayout_w_in", [g["w_in"]], [(NDEV, D_MODEL, W_IN_PAD)], rev_in)[0]
        if l == 0:
            def layer_vec(i):
                parts = [grads[i][n].reshape(-1) for n in SMALL]
                parts += [grads[i]["conv_a_w"][:31].reshape(-1), grads[i]["conv_b_w"][:3].reshape(-1),
                          d_adas[i].reshape(-1)]
                return jnp.concatenate(parts)

            small_sizes = [int(grads[0][n].size) for n in SMALL] + [31 * CW, 3 * CW, 3 * D_MODEL]
            gsmall = exchange("gather_small_grads", [_rows(jnp.concatenate([layer_vec(i) for i in range(L)]))], [])[0]
            pending, token = exchange_begin("scatter0", [send_in], 0, gsmall)
        else:
            pending, token = exchange_begin("scatter%d" % l, [send_in] + send_rest(g), 0,
                                            dy if l + 1 == L else recv[l + 1][0])
    grad_x = dy[None]

    gsmall = gsmall + token[0, 0]
    gsum = sum_slots("sum_small", gsmall).reshape(-1)
    recv[0] = [None] + exchange_end(rest0[0], gsum)
    Gr = {}
    offs = np.cumsum([0] + small_sizes)
    per_layer = int(offs[-1])
    gsum = gsum[:L * per_layer].reshape(L, per_layer)
    for i, n in enumerate(SMALL):
        Gr[n] = gsum[:, offs[i]:offs[i + 1]]
    ca = gsum[:, offs[7]:offs[8]].reshape(L, 31, CW)
    cbw = gsum[:, offs[8]:offs[9]].reshape(L, 3, CW)
    Gr["conv_a_w"] = lax.dynamic_slice_in_dim(ca, me * 64, 64, axis=2)
    Gr["conv_b_w"] = lax.dynamic_slice_in_dim(cbw, me * 64, 64, axis=2)
    Gr["b_ada"] = gsum[:, offs[9]:offs[10]]
    d_ada_all = gsmall.reshape(NDEV, -1)[:, :L * per_layer].reshape(NDEV, L, per_layer)[:, :, offs[9]:offs[10]]
    d_mine = lax.dynamic_slice_in_dim(d_ada_all, me * ncol, ncol, axis=2).reshape(NDEV, L * ncol)
    g_ada = mm(c_act, jnp.pad(d_mine, ((0, 8), (0, 0))).astype(BF), name="mm_gw_ada", trans_a=True)
    Gr["w_ada"] = g_ada.reshape(D_MODEL, L, ncol).transpose(1, 0, 2)

    D, NM, NV = {}, {}, {}
    D["w_ada"], NM["w_ada"], NV["w_ada"] = adamw("adamw_w_ada", P["w_ada"], Gr["w_ada"], Mo["w_ada"], Vo["w_ada"])
    Gr["w_o"], D["w_o"], NM["w_o"], NV["w_o"] = sum_adamw(
        "sum_adamw_w_o", [recv[l][2] for l in range(L)], P["w_o"], Mo["w_o"], Vo["w_o"])
    for n, row0 in (("w_a_out", ROW_A), ("w_b_out", ROW_B), ("w_c_out", ROW_C), ("w_ukv", ROW_UKV), ("w_uq", ROW_UQ)):
        Gr[n], D[n], NM[n], NV[n] = sum_adamw("sum_adamw_" + n, [recv[l][1] for l in range(L)], P[n], Mo[n], Vo[n],
                                              row0=row0)
    upper = sum_adamw("sum_adamw_w_in_upper", [recv[l][0] for l in range(1, L)], P["w_in"], Mo["w_in"], Vo["w_in"], lo=1)
    recv[0][0] = exchange_end(pending, upper[1])[0]
    Gr["w_in"], D["w_in"], NM["w_in"], NV["w_in"] = sum_adamw(
        "sum_adamw_w_in", [recv[0][0]], P["w_in"], Mo["w_in"], Vo["w_in"], lo=0, prev=upper)
    packed =("b_ada", "conv_a_w", "conv_b_w") + SMALL
    pk = lambda T_: _rows(jnp.concatenate([T_[n].reshape(-1) for n in packed]))[None]
    dS, mS, vS = adamw("adamw_small", pk(P), pk(Gr), pk(Mo), pk(Vo))
    o = 0
    for n in packed:
        sz = int(np.prod(P[n].shape))
        D[n] = dS.reshape(-1)[o:o + sz].reshape(P[n].shape)
        NM[n] = mS.reshape(-1)[o:o + sz].reshape(P[n].shape)
        NV[n] = vS.reshape(-1)[o:o + sz].reshape(P[n].shape)
        o += sz
    return (loss, grad_x, *[Gr[n] for n in ORDER], *[D[n] for n in ORDER], *[NM[n] for n in ORDER],
            *[NV[n] for n in ORDER])
```

```python
import functools
import math

import numpy as np
import jax
import jax.numpy as jnp
from jax import lax
from jax.experimental import pallas as pl
from jax.experimental.pallas import tpu as pltpu

BF = jnp.bfloat16
F32 = jnp.float32
MESH = pl.DeviceIdType.MESH
NDEV = 8

HEADS, NOPE, ROPE, VH = 8, 64, 32, 64
HP = 128
ROPE_THETA = 10000.0
LN_EPS = 1e-5
RMS_EPS = 1e-6
LR, B1, B2, EPS, WD, STEP = 0.001, 0.9, 0.999, 1e-08, 0.01, 10

LANE = 128
VMEM_LIMIT = 56 * 1024 * 1024

D_MODEL, CW, QL, KVL = 1024, 512, 384, 256
OFF_M, OFF_A, OFF_AG, OFF_B, OFF_CG, OFF_GB, OFF_BG = 0, 3072, 4096, 4608, 5632, 6144, 6656
OFF_KV, OFF_KR, OFF_Q, NP = 7168, 7424, 7680, 8192
D_IN = 7840


def _cparams(**kw):
    return pltpu.CompilerParams(vmem_limit_bytes=VMEM_LIMIT, **kw)


def _sigmoid(x):
    return jax.nn.sigmoid(x)


def _silu(x):
    return x * _sigmoid(x)


def _dsilu(x):
    s = _sigmoid(x)
    return s * (1.0 + x * (1.0 - s))


def _pick_tile(n, cap, mult):
    if n <= cap:
        return n
    for t in range(cap - cap % mult, 0, -mult):
        if n % t == 0:
            return t
    raise ValueError((n, cap, mult))


def mm(a, b, *, name, trans_a=False, trans_b=False, out_dtype=F32, bias=None, tm=1024, tn=1024, tk=2048, deps=()):
    if trans_a:
        K, M = a.shape
    else:
        M, K = a.shape
    if trans_b:
        N, K2 = b.shape
    else:
        K2, N = b.shape
    assert K == K2 and not (trans_a and trans_b), (a.shape, b.shape)
    tm, tn = _pick_tile(M, tm, 16), _pick_tile(N, tn, LANE)
    tk = _pick_tile(K, tk, LANE if trans_b else 16)
    assert M % tm == 0 and N % tn == 0 and K % tk == 0, (M, N, K, tm, tn, tk)
    nk = K // tk
    dims = (((0 if trans_a else 1,), (1 if trans_b else 0,)), ((), ()))
    has_bias = bias is not None

    def body(*refs):
        a_ref, b_ref = refs[0], refs[1]
        bias_ref = refs[2] if has_bias else None
        o_ref = refs[(3 if has_bias else 2) + len(deps)]
        p = lax.dot_general(a_ref[...], b_ref[...], dims, preferred_element_type=F32)

        def finish(v):
            if has_bias:
                v = v + bias_ref[...]
            o_ref[...] = v.astype(o_ref.dtype)

        if nk == 1:
            finish(p)
        else:
            acc = refs[-1]
            k = pl.program_id(2)

            @pl.when(k == 0)
            def _():
                acc[...] = p

            @pl.when(k > 0)
            def _():
                acc[...] += p

            @pl.when(k == nk - 1)
            def _():
                finish(acc[...])

    if trans_a:
        a_spec = pl.BlockSpec((tk, tm), lambda i, j, k: (k, i))
    else:
        a_spec = pl.BlockSpec((tm, tk), lambda i, j, k: (i, k))
    if trans_b:
        b_spec = pl.BlockSpec((tn, tk), lambda i, j, k: (j, k))
    else:
        b_spec = pl.BlockSpec((tk, tn), lambda i, j, k: (k, j))
    in_specs = [a_spec, b_spec]
    args = [a, b]
    if has_bias:
        in_specs.append(pl.BlockSpec((1, tn), lambda i, j, k: (0, j)))
        args.append(bias)
    in_specs += [ANY_SPEC] * len(deps)
    args += list(deps)
    return pl.pallas_call(
        body, name=name, grid=(M // tm, N // tn, nk),
        in_specs=in_specs, out_specs=pl.BlockSpec((tm, tn), lambda i, j, k: (i, j)),
        out_shape=jax.ShapeDtypeStruct((M, N), out_dtype),
        scratch_shapes=[pltpu.VMEM((tm, tn), F32)] if nk > 1 else [],
        compiler_params=_cparams(),
    )(*args)


def rowwise(name, fn, S, T, row_ins, full_ins, row_outs, acc_outs=(), into=None):
    n_in = len(row_ins) + len(full_ins)
    n_ro, n_ao = len(row_outs), len(acc_outs)
    alias = into is not None and into[0] is not None
    T = min(T, S)

    def body(*refs):
        vals = [r[...] for r in refs[:n_in]]
        vals = [v.astype(F32) if v.dtype == BF else v for v in vals]
        outs = fn(*vals)
        if not isinstance(outs, (tuple, list)):
            outs = (outs,)
        assert len(outs) == n_ro + n_ao, (name, len(outs))
        o0 = n_in + (1 if alias else 0)
        for r, v in zip(refs[o0:o0 + n_ro], outs[:n_ro]):
            r[...] = v.astype(r.dtype)
        first = pl.program_id(0) == 0
        for r, v in zip(refs[o0 + n_ro:], outs[n_ro:]):
            def init(r=r, v=v):
                r[...] = v

            def accum(r=r, v=v):
                r[...] += v

            pl.when(first)(init)
            pl.when(jnp.logical_not(first))(accum)

    in_specs, args = [], []
    for arr, W, off in row_ins:
        assert off % W == 0 and arr.shape[0] == S, (name, arr.shape, W, off)
        in_specs.append(pl.BlockSpec((T, W), functools.partial(lambda i, cb: (i, cb), cb=off // W)))
        args.append(arr)
    for arr in full_ins:
        in_specs.append(pl.BlockSpec(arr.shape, lambda i: (0, 0)))
        args.append(arr)
    out_specs = [pl.BlockSpec((T, W), lambda i: (i, 0)) for W, _ in row_outs]
    out_shape = [jax.ShapeDtypeStruct((S, W), dt) for W, dt in row_outs]
    aliases = {}
    if into is not None:
        buf, total, off = into
        W0, dt0 = row_outs[0]
        assert off % W0 == 0
        out_specs[0] = pl.BlockSpec((T, W0), functools.partial(lambda i, cb: (i, cb), cb=off // W0))
        out_shape[0] = jax.ShapeDtypeStruct((S, total), dt0)
        if alias:
            in_specs.append(ANY_SPEC)
            args.append(buf)
            aliases = {n_in: 0}
    out_specs += [pl.BlockSpec((1, W), lambda i: (0, 0)) for W in acc_outs]
    out_shape += [jax.ShapeDtypeStruct((1, W), F32) for W in acc_outs]
    return pl.pallas_call(
        body, name=name, grid=(S // T,), in_specs=in_specs, out_specs=out_specs, out_shape=out_shape,
        input_output_aliases=aliases, compiler_params=_cparams(),
    )(*args)


def _colsum(v):
    return jnp.sum(v, axis=0, keepdims=True)


def _ln_stats(r):
    mu = jnp.mean(r, axis=-1, keepdims=True)
    d = r - mu
    var = jnp.mean(d * d, axis=-1, keepdims=True)
    rstd = lax.rsqrt(var + LN_EPS)
    return d * rstd, rstd


def _ln_bwd(dn, n, rstd):
    return rstd * (dn - jnp.mean(dn, axis=-1, keepdims=True) - n * jnp.mean(dn * n, axis=-1, keepdims=True))


CPAD = 32
TC = 64


def _pre(mode, x1, x2):
    return x1 * _sigmoid(x2) if mode == "glu" else x1 * x2


def _shifted(ext, sft):
    n = TC + CPAD
    return pltpu.roll(ext, (n - sft) % n, 0)[0:TC]


def _interleaved_specs(S, off):
    return [pl.BlockSpec((S, LANE), functools.partial(lambda j, o: (0, o + 2 * j), o=off // LANE)),
            pl.BlockSpec((S, LANE), functools.partial(lambda j, o: (0, o + 2 * j + 1), o=off // LANE))]


def conv_fwd(name, src, off, w_pad, taps, mode, S, C):
    nchunk = S // TC

    def body(x1_ref, x2_ref, w_ref, o_ref, a_pad):
        a_pad[0:CPAD, :] = jnp.zeros((CPAD, LANE), F32)

        def fill(i, _):
            r = pl.multiple_of(i * 256, 256)
            a_pad[pl.ds(CPAD + r, 256), :] = _pre(mode, x1_ref[pl.ds(r, 256), :].astype(F32),
                                                  x2_ref[pl.ds(r, 256), :].astype(F32))
            return 0

        lax.fori_loop(0, S // 256, fill, 0)

        def chunk(i, _):
            base = pl.multiple_of(i * TC, TC)
            ext = a_pad[pl.ds(base, TC + CPAD), :]
            acc = jnp.zeros((TC, LANE), F32)
            for k in range(taps):
                acc = acc + w_ref[pl.ds(k, 1), :] * _shifted(ext, CPAD - (taps - 1) + k)
            o_ref[pl.ds(base, TC), :] = acc
            return 0

        lax.fori_loop(0, nchunk, chunk, 0)

    kp = w_pad.shape[0]
    return pl.pallas_call(
        body, name=name, grid=(C // LANE,),
        in_specs=_interleaved_specs(S, off) + [pl.BlockSpec((kp, LANE), lambda j: (0, j))],
        out_specs=pl.BlockSpec((S, LANE), lambda j: (0, j)),
        out_shape=jax.ShapeDtypeStruct((S, C), F32),
        scratch_shapes=[pltpu.VMEM((S + CPAD, LANE), F32)],
        compiler_params=_cparams(),
    )(src, src, w_pad)


def conv_bwd(name, src, off, dc, w_pad, taps, mode, S, C, buf):
    nchunk = S // TC
    kp = w_pad.shape[0]

    def body(x1_ref, x2_ref, dc_ref, w_ref, _, d_ref, dw_ref, a_pad, dc_pad, dw_acc):
        a_pad[0:CPAD, :] = jnp.zeros((CPAD, LANE), F32)
        dc_pad[S:S + CPAD, :] = jnp.zeros((CPAD, LANE), F32)
        dw_acc[...] = jnp.zeros(dw_acc.shape, F32)

        def fill(i, _):
            r = pl.multiple_of(i * 256, 256)
            a_pad[pl.ds(CPAD + r, 256), :] = _pre(mode, x1_ref[pl.ds(r, 256), :].astype(F32),
                                                  x2_ref[pl.ds(r, 256), :].astype(F32))
            dc_pad[pl.ds(r, 256), :] = dc_ref[pl.ds(r, 256), :]
            return 0

        lax.fori_loop(0, S // 256, fill, 0)

        def chunk(i, _):
            base = pl.multiple_of(i * TC, TC)
            ext_d = dc_pad[pl.ds(base, TC + CPAD), :]
            ext_a = a_pad[pl.ds(base, TC + CPAD), :]
            dcv = ext_d[0:TC]
            da = jnp.zeros((TC, LANE), F32)
            for k in range(taps):
                da = da + w_ref[pl.ds(k, 1), :] * _shifted(ext_d, taps - 1 - k)
                prod = dcv * _shifted(ext_a, CPAD - (taps - 1) + k)
                fold = prod[0:8]
                for g in range(1, TC // 8):
                    fold = fold + prod[8 * g:8 * g + 8]
                dw_acc[pl.ds(8 * k, 8), :] += fold
            x1 = x1_ref[pl.ds(base, TC), :].astype(F32)
            x2 = x2_ref[pl.ds(base, TC), :].astype(F32)
            if mode == "glu":
                s = _sigmoid(x2)
                d1, d2 = da * s, da * x1 * s * (1.0 - s)
            else:
                d1, d2 = da * x2, da * x1
            d_ref[pl.ds(base, TC), 0:LANE] = d1.astype(BF)
            d_ref[pl.ds(base, TC), LANE:2 * LANE] = d2.astype(BF)
            return 0

        lax.fori_loop(0, nchunk, chunk, 0)
        dw_ref[...] = jnp.zeros(dw_ref.shape, F32)
        for k in range(taps):
            dw_ref[pl.ds(k, 1), :] = jnp.sum(dw_acc[pl.ds(8 * k, 8), :], axis=0, keepdims=True)

    blk = pl.BlockSpec((S, LANE), lambda j: (0, j))
    return pl.pallas_call(
        body, name=name, grid=(C // LANE,),
        in_specs=_interleaved_specs(S, off) + [blk, pl.BlockSpec((kp, LANE), lambda j: (0, j)), ANY_SPEC],
        out_specs=[pl.BlockSpec((S, 2 * LANE), functools.partial(lambda j, o: (0, o + j), o=off // (2 * LANE))),
                   pl.BlockSpec((kp, LANE), lambda j: (0, j))],
        out_shape=[jax.ShapeDtypeStruct(buf.shape, BF), jax.ShapeDtypeStruct((kp, C), F32)],
        input_output_aliases={4: 0},
        scratch_shapes=[pltpu.VMEM((S + CPAD, LANE), F32), pltpu.VMEM((S + CPAD, LANE), F32),
                        pltpu.VMEM((8 * kp, LANE), F32)],
        compiler_params=_cparams(),
    )(src, src, dc, w_pad, buf)


FWD_TILES = (512, 512)
BWD_TILES = (512, 512)
QUADS = HEADS // 4
QW, KVW = 4 * (NOPE + ROPE), 4 * (NOPE + VH)
SCALE = (NOPE + ROPE) ** -0.5
NT_DIMS = (((1,), (1,)), ((), ()))
TN_DIMS = (((0,), (0,)), ((), ()))


def _lane_mask(width, group, dtype):
    lane = lax.broadcasted_iota(jnp.int32, (1, LANE), 1)
    return jnp.where(lane // width == group, 1.0, 0.0).astype(dtype)


def _visible(tq, tk, off):
    row = lax.broadcasted_iota(jnp.int32, (tq, tk), 0)
    col = lax.broadcasted_iota(jnp.int32, (tq, tk), 1)
    return col <= row + off


def _attn_tiles(S, tq, tk):
    tk = tk if S % tk == 0 else 256
    return min(tq, tk), tk


def attn_fwd(q, kv, kpe, S):
    tq, tk = _attn_tiles(S, *FWD_TILES)
    nq = S // tq

    def body(q_ref, kv_ref, kp_ref, o_ref, lse_ref):
        for t in range(2):
            cols = slice(t * LANE, (t + 1) * LANE)
            for hh in range(2):
                def q_block(qi, _, t=t, hh=hh, cols=cols):
                    r0 = pl.multiple_of(qi * tq, tq)
                    qcat = jnp.concatenate([q_ref[pl.ds(r0, tq), cols] * _lane_mask(NOPE, hh, BF),
                                            q_ref[pl.ds(r0, tq), 2 * LANE:3 * LANE] * _lane_mask(ROPE, 2 * t + hh, BF)],
                                           axis=1)
                    nfull = (qi * tq) // tk

                    def step(kj, carry, masked):
                        m, l, acc = carry
                        c0 = pl.multiple_of(kj * tk, tk)
                        kc = jnp.concatenate([kv_ref[pl.ds(c0, tk), cols], kp_ref[pl.ds(c0, tk), :]], axis=1)
                        vt = kv_ref[pl.ds(c0, tk), (2 + t) * LANE:(3 + t) * LANE]
                        s = lax.dot_general(qcat, kc, NT_DIMS, preferred_element_type=F32) * SCALE
                        if masked:
                            s = jnp.where(_visible(tq, tk, qi * tq - nfull * tk), s, -jnp.inf)
                        m_new = jnp.maximum(m, jnp.max(s, axis=-1, keepdims=True))
                        p = jnp.exp(s - m_new)
                        alpha = jnp.exp(m - m_new)
                        l = alpha * l + jnp.sum(p, axis=-1, keepdims=True)
                        acc = alpha * acc + jnp.dot(p.astype(BF), vt, preferred_element_type=F32)
                        return m_new, l, acc

                    init = (jnp.full((tq, 1), -jnp.inf, F32), jnp.zeros((tq, 1), F32), jnp.zeros((tq, LANE), F32))
                    carry = lax.fori_loop(0, nfull, lambda kj, c: step(kj, c, False), init)
                    m, l, acc = step(nfull, carry, True)
                    mine = _lane_mask(NOPE, hh, F32)
                    if hh == 0:
                        o_ref[pl.ds(r0, tq), cols] = (acc / l) * mine
                        lse_ref[pl.ds(r0, tq), cols] = (m + jnp.log(l)) * mine
                    else:
                        o_ref[pl.ds(r0, tq), cols] += (acc / l) * mine
                        lse_ref[pl.ds(r0, tq), cols] += (m + jnp.log(l)) * mine
                    return 0

                lax.fori_loop(0, nq, q_block, 0)

    return pl.pallas_call(
        body, name="attn_fwd", grid=(QUADS,),
        in_specs=[pl.BlockSpec((S, QW), lambda g: (0, g)), pl.BlockSpec((S, KVW), lambda g: (0, g)),
                  pl.BlockSpec((S, LANE), lambda g: (0, 0))],
        out_specs=[pl.BlockSpec((S, 2 * LANE), lambda g: (0, g))] * 2,
        out_shape=[jax.ShapeDtypeStruct((S, HEADS * VH), F32)] * 2,
        compiler_params=_cparams(),
    )(q, kv, kpe)


def attn_bwd(q, kv, kpe, o, lse, do, S):
    tq, tk = _attn_tiles(S, *BWD_TILES)
    nq = S // tq

    def body(q_ref, kv_ref, kp_ref, o_ref, lse_ref, do_ref, dq_ref, dkv_ref, dkp_ref, dq_acc, dk_acc, dv_acc):
        for t in range(2):
            cols = slice(t * LANE, (t + 1) * LANE)
            dk_acc[...] = jnp.zeros(dk_acc.shape, F32)
            dv_acc[...] = jnp.zeros(dv_acc.shape, F32)
            for hh in range(2):
                def q_block(qi, _, t=t, hh=hh, cols=cols):
                    r0 = pl.multiple_of(qi * tq, tq)
                    mine = _lane_mask(NOPE, hh, F32)
                    qcat = jnp.concatenate([q_ref[pl.ds(r0, tq), cols] * _lane_mask(NOPE, hh, BF),
                                            q_ref[pl.ds(r0, tq), 2 * LANE:3 * LANE] * _lane_mask(ROPE, 2 * t + hh, BF)],
                                           axis=1)
                    dof = do_ref[pl.ds(r0, tq), cols] * mine
                    dob = dof.astype(BF)
                    delta = jnp.sum(dof * o_ref[pl.ds(r0, tq), cols], axis=-1, keepdims=True)
                    lse_h = lse_ref[pl.ds(r0, tq), cols][:, hh * NOPE:hh * NOPE + 1]
                    nfull = (qi * tq) // tk
                    dq_acc[...] = jnp.zeros(dq_acc.shape, F32)

                    def step(kj, _, masked):
                        c0 = pl.multiple_of(kj * tk, tk)
                        kc = jnp.concatenate([kv_ref[pl.ds(c0, tk), cols], kp_ref[pl.ds(c0, tk), :]], axis=1)
                        vt = kv_ref[pl.ds(c0, tk), (2 + t) * LANE:(3 + t) * LANE]
                        s = lax.dot_general(qcat, kc, NT_DIMS, preferred_element_type=F32) * SCALE
                        if masked:
                            s = jnp.where(_visible(tq, tk, qi * tq - nfull * tk), s, -jnp.inf)
                        p = jnp.exp(s - lse_h)
                        dp = lax.dot_general(dob, vt, NT_DIMS, preferred_element_type=F32)
                        ds = (p * (dp - delta) * SCALE).astype(BF)
                        dv_acc[pl.ds(c0, tk), :] += lax.dot_general(p.astype(BF), dob, TN_DIMS,
                                                                    preferred_element_type=F32)
                        dk_acc[pl.ds(c0, tk), :] += lax.dot_general(ds, qcat, TN_DIMS, preferred_element_type=F32)
                        dq_acc[...] += jnp.dot(ds, kc, preferred_element_type=F32)
                        return 0

                    lax.fori_loop(0, nfull, lambda kj, c: step(kj, c, False), 0)
                    step(nfull, 0, True)
                    d = dq_acc[...]
                    pe = d[:, LANE:] * _lane_mask(ROPE, 2 * t + hh, F32)
                    if hh == 0:
                        dq_ref[pl.ds(r0, tq), cols] = d[:, :LANE] * mine
                    else:
                        dq_ref[pl.ds(r0, tq), cols] += d[:, :LANE] * mine
                    if t == 0 and hh == 0:
                        dq_ref[pl.ds(r0, tq), 2 * LANE:3 * LANE] = pe
                    else:
                        dq_ref[pl.ds(r0, tq), 2 * LANE:3 * LANE] += pe
                    return 0

                lax.fori_loop(0, nq, q_block, 0)
            dkv_ref[:, t * LANE:(t + 1) * LANE] = dk_acc[:, :LANE].astype(BF)
            dkv_ref[:, (2 + t) * LANE:(3 + t) * LANE] = dv_acc[...].astype(BF)
            if t == 0:
                dkp_ref[...] = dk_acc[:, LANE:]
            else:
                dkp_ref[...] += dk_acc[:, LANE:]

    qspec = pl.BlockSpec((S, QW), lambda g: (0, g))
    kvspec = pl.BlockSpec((S, KVW), lambda g: (0, g))
    ospec = pl.BlockSpec((S, 2 * LANE), lambda g: (0, g))
    return pl.pallas_call(
        body, name="attn_bwd", grid=(QUADS,),
        in_specs=[qspec, kvspec, pl.BlockSpec((S, LANE), lambda g: (0, 0)), ospec, ospec, ospec],
        out_specs=[qspec, kvspec, pl.BlockSpec((S, LANE), lambda g: (0, g))],
        out_shape=[jax.ShapeDtypeStruct((S, HEADS * (NOPE + ROPE)), F32), jax.ShapeDtypeStruct((S, HEADS * (NOPE + VH)), BF),
                   jax.ShapeDtypeStruct((S, HEADS * ROPE), F32)],
        scratch_shapes=[pltpu.VMEM((tq, 2 * LANE), F32), pltpu.VMEM((S, 2 * LANE), F32), pltpu.VMEM((S, LANE), F32)],
        compiler_params=_cparams(),
    )(q, kv, kpe, o, lse, do)


def exchange(name, gathers, a2as):
    n_g, n = len(gathers), len(gathers) + len(a2as)

    def body(*refs):
        ins, outs = refs[:n], refs[n:2 * n]
        send_sems, recv_sems, loc_sems = refs[2 * n:]
        x, y, c = lax.axis_index("x"), lax.axis_index("y"), lax.axis_index("c")
        me = 4 * x + 2 * y + c

        def peer(k):
            px = 1 - x if k & 4 else x
            py = 1 - y if k & 2 else y
            pc = 1 - c if k & 1 else c
            return (px, py, pc), 4 * px + 2 * py + pc

        def remote(a, k):
            pid, pflat = peer(k)
            src = ins[a] if a < n_g else ins[a].at[pflat]
            return pltpu.make_async_remote_copy(
                src_ref=src, dst_ref=outs[a].at[me], send_sem=send_sems.at[a, k - 1], recv_sem=recv_sems.at[a, k - 1],
                device_id=pid, device_id_type=MESH)

        def arrival(a, k):
            pid, pflat = peer(k)
            src = ins[a] if a < n_g else ins[a].at[pflat]
            return pltpu.make_async_remote_copy(
                src_ref=src, dst_ref=outs[a].at[pflat], send_sem=send_sems.at[a, k - 1], recv_sem=recv_sems.at[a, k - 1],
                device_id=pid, device_id_type=MESH)

        local = []
        for a in range(n):
            own = ins[a] if a < n_g else ins[a].at[me]
            cp = pltpu.make_async_copy(own, outs[a].at[me], loc_sems.at[a])
            cp.start()
            local.append(cp)
        sent = []
        for k in (1, 2, 4, 3, 5, 6, 7):
            for a in range(n):
                cp = remote(a, k)
                cp.start()
                sent.append(cp)
        for k in range(1, 8):
            for a in range(n):
                arrival(a, k).wait_recv()
        for cp in sent:
            cp.wait_send()
        for cp in local:
            cp.wait()

    out_shape = [jax.ShapeDtypeStruct((NDEV,) + g.shape, g.dtype) for g in gathers]
    out_shape += [jax.ShapeDtypeStruct(a.shape, a.dtype) for a in a2as]
    any_spec = pl.BlockSpec(memory_space=pl.ANY)
    return pl.pallas_call(
        body, name=name, in_specs=[any_spec] * n, out_specs=[any_spec] * n, out_shape=out_shape,
        scratch_shapes=[pltpu.SemaphoreType.DMA((n, NDEV - 1)), pltpu.SemaphoreType.DMA((n, NDEV - 1)),
                        pltpu.SemaphoreType.DMA((n,))],
    )(*gathers, *a2as)


def gather_two_level(name, block, dep):
    def body(x_ref, _, out_ref, stage, send_sems, recv_sems, loc_sem):
        x, y, c = lax.axis_index("x"), lax.axis_index("y"), lax.axis_index("c")
        me, sibling = (x, y, c), (x, y, 1 - c)
        chips = [(1 - x, y), (x, 1 - y), (1 - x, 1 - y)]

        def slot(px, py, pc):
            return out_ref.at[4 * px + 2 * py + pc]

        def copy(k, owner, to, src=None):
            return pltpu.make_async_remote_copy(
                src_ref=slot(*owner) if src is None else src, dst_ref=slot(*owner), send_sem=send_sems.at[k],
                recv_sem=recv_sems.at[k], device_id=to, device_id_type=MESH)

        load = pltpu.make_async_copy(x_ref, stage, loc_sem)
        load.start()
        first = [copy(0, me, sibling, src=x_ref)] + [copy(1 + j, me, (*chip, c), src=x_ref) for j, chip in enumerate(chips)]
        for cp in first:
            cp.start()
        load.wait()
        store = pltpu.make_async_copy(stage, slot(*me), loc_sem)
        store.start()
        passed = [copy(4 + j, (*chip, c), sibling) for j, chip in enumerate(chips)]
        for j, chip in enumerate(chips):
            copy(1 + j, (*chip, c), me).wait_recv()
            passed[j].start()
        copy(0, sibling, me).wait_recv()
        for j, chip in enumerate(chips):
            copy(4 + j, (*chip, 1 - c), me).wait_recv()
        for cp in first + passed:
            cp.wait_send()
        store.wait()

    return pl.pallas_call(
        body, name=name, in_specs=[pl.BlockSpec(memory_space=pl.ANY)] * 2, out_specs=pl.BlockSpec(memory_space=pl.ANY),
        out_shape=jax.ShapeDtypeStruct((NDEV,) + block.shape, block.dtype),
        scratch_shapes=[pltpu.VMEM(block.shape, block.dtype), pltpu.SemaphoreType.DMA((NDEV - 1,)),
                        pltpu.SemaphoreType.DMA((NDEV - 1,)), pltpu.SemaphoreType.DMA],
        compiler_params=_cparams(),
    )(block, dep)


def _peer(k, x, y, c):
    px = 1 - x if k & 4 else x
    py = 1 - y if k & 2 else y
    pc = 1 - c if k & 1 else c
    return (px, py, pc), 4 * px + 2 * py + pc


PEER_ORDER = (1, 2, 4, 3, 5, 6, 7)
HBM_SPEC = pl.BlockSpec(memory_space=pltpu.HBM)
SEM_SPEC = pl.BlockSpec(memory_space=pltpu.SEMAPHORE)
ANY_SPEC = pl.BlockSpec(memory_space=pl.ANY)


def _split_copies(ins, lands, n_g, send_sems, recv_sems):
    x, y, c = lax.axis_index("x"), lax.axis_index("y"), lax.axis_index("c")
    me = 4 * x + 2 * y + c

    def outgoing(a, k):
        pid, pflat = _peer(k, x, y, c)
        src = ins[a] if a < n_g else ins[a].at[pflat]
        return pltpu.make_async_remote_copy(
            src_ref=src, dst_ref=lands[a].at[me], send_sem=send_sems.at[a * (NDEV - 1) + k - 1],
            recv_sem=recv_sems.at[a * (NDEV - 1) + k - 1],
            device_id=pid, device_id_type=MESH)

    def arrival(a, k):
        pid, pflat = _peer(k, x, y, c)
        src = ins[a] if a < n_g else ins[a].at[pflat]
        return pltpu.make_async_remote_copy(
            src_ref=src, dst_ref=lands[a].at[pflat], send_sem=send_sems.at[a * (NDEV - 1) + k - 1],
            recv_sem=recv_sems.at[a * (NDEV - 1) + k - 1],
            device_id=pid, device_id_type=MESH)

    return outgoing, arrival


def exchange_begin(name, srcs, n_g, dep):
    n = len(srcs)
    land_shapes = [((NDEV,) + s.shape) if a < n_g else s.shape for a, s in enumerate(srcs)]

    def own_body(*refs):
        ins, outs = refs[:n], refs[n + 1:2 * n + 1]
        stage, sems = refs[2 * n + 1:3 * n + 1], refs[-1]
        me = 4 * lax.axis_index("x") + 2 * lax.axis_index("y") + lax.axis_index("c")
        cps = [pltpu.make_async_copy(ins[a] if a < n_g else ins[a].at[me], stage[a], sems.at[a]) for a in range(n)]
        for cp in cps:
            cp.start()
        for cp in cps:
            cp.wait()
        cps = [pltpu.make_async_copy(stage[a], outs[a].at[me], sems.at[a]) for a in range(n)]
        for cp in cps:
            cp.start()
        for cp in cps:
            cp.wait()

    lands = pl.pallas_call(
        own_body, name=name + "_own", in_specs=[ANY_SPEC] * (n + 1), out_specs=[ANY_SPEC] * n,
        out_shape=[jax.ShapeDtypeStruct(sh, s.dtype) for sh, s in zip(land_shapes, srcs)],
        scratch_shapes=[pltpu.VMEM(sh[1:], s.dtype) for sh, s in zip(land_shapes, srcs)] + [pltpu.SemaphoreType.DMA((n,))],
        compiler_params=_cparams(),
    )(*srcs, dep)

    def start_body(*refs):
        ins, lz = refs[:n], refs[n:2 * n]
        send_sems, recv_sems, token = refs[2 * n], refs[2 * n + 1], refs[-1]
        outgoing, _ = _split_copies(ins, lz, n_g, send_sems, recv_sems)
        for k in PEER_ORDER:
            for a in range(n):
                outgoing(a, k).start()
        token[...] = jnp.zeros(token.shape, F32)

    hbm = lambda t: pltpu.HBM(t.shape, t.dtype)
    res = pl.pallas_call(
        start_body, name=name + "_start",
        out_shape=(pltpu.SemaphoreType.DMA((n * (NDEV - 1),)), pltpu.SemaphoreType.DMA((n * (NDEV - 1),)),
                   *[hbm(s) for s in srcs], *[hbm(t) for t in lands], jax.ShapeDtypeStruct((8, LANE), F32)),
        in_specs=[HBM_SPEC] * (2 * n),
        out_specs=(SEM_SPEC, SEM_SPEC, *[HBM_SPEC] * (2 * n), pl.BlockSpec(memory_space=pltpu.VMEM)),
        input_output_aliases={i: 2 + i for i in range(2 * n)},
        compiler_params=pltpu.CompilerParams(has_side_effects=pltpu.SideEffectType.DATAFLOW_SIDE_EFFECTING),
    )(*[pltpu.with_memory_space_constraint(t, pltpu.HBM) for t in list(srcs) + list(lands)])
    return (name, n, n_g, res[:-1]), res[-1]


def exchange_end(handle, after):
    name, n, n_g, (send_sems, recv_sems, *bufs) = handle

    def wait_body(*refs):
        ins, lz = refs[:n], refs[n:2 * n]
        ss, rs = refs[2 * n], refs[2 * n + 1]
        outgoing, arrival = _split_copies(ins, lz, n_g, ss, rs)
        for k in range(1, NDEV):
            for a in range(n):
                arrival(a, k).wait_recv()
        for k in range(1, NDEV):
            for a in range(n):
                outgoing(a, k).wait_send()

    res = pl.pallas_call(
        wait_body, name=name + "_wait", out_shape=tuple(pltpu.HBM(t.shape, t.dtype) for t in bufs),
        in_specs=[HBM_SPEC] * (2 * n) + [SEM_SPEC, SEM_SPEC, ANY_SPEC], out_specs=[HBM_SPEC] * (2 * n),
        input_output_aliases={i: i for i in range(2 * n)},
        compiler_params=pltpu.CompilerParams(has_side_effects=pltpu.SideEffectType.DATAFLOW_SIDE_EFFECTING),
    )(*bufs, send_sems, recv_sems, after)
    return list(res[n:])


def _pick_rows(R, mult, cap):
    best = None
    for n in range(1, R + 1):
        if R % n == 0 and (R // n) % mult == 0 and R // n <= cap:
            best = R // n
            break
    assert best is not None, (R, mult, cap)
    return best


def sum_slots(name, x):
    _, R, _ = x.shape
    tr = _pick_rows(R, 16, 2304)

    def body(x_ref, o_ref):
        acc = x_ref[0].astype(F32)
        for d in range(1, NDEV):
            acc = acc + x_ref[d].astype(F32)
        o_ref[...] = acc

    return pl.pallas_call(
        body, name=name, grid=(R // tr,),
        in_specs=[pl.BlockSpec((NDEV, tr, LANE), lambda i: (0, i, 0))],
        out_specs=pl.BlockSpec((tr, LANE), lambda i: (i, 0)),
        out_shape=jax.ShapeDtypeStruct((R, LANE), F32), compiler_params=_cparams(),
    )(x)


def adamw(name, w, g, m, v):
    L, R, C = w.shape
    tr = _pick_rows(R, 8, 256) if R % 8 == 0 else R

    def body(w_ref, g_ref, m_ref, v_ref, d_ref, nm_ref, nv_ref):
        gg = g_ref[...]
        nm = B1 * m_ref[...] + (1.0 - B1) * gg
        nv = B2 * v_ref[...] + (1.0 - B2) * jnp.square(gg)
        m_hat = nm / (1.0 - B1 ** STEP)
        v_hat = nv / (1.0 - B2 ** STEP)
        d_ref[...] = -LR * (m_hat / (jnp.sqrt(v_hat) + EPS) + WD * w_ref[...])
        nm_ref[...] = nm
        nv_ref[...] = nv

    blk = pl.BlockSpec((1, tr, C), lambda l, i: (l, i, 0))
    shp = jax.ShapeDtypeStruct(w.shape, F32)
    return pl.pallas_call(
        body, name=name, grid=(L, R // tr), in_specs=[blk] * 4, out_specs=[blk] * 3, out_shape=[shp] * 3,
        compiler_params=_cparams(),
    )(w, g, m, v)


IN_SHARD = D_IN // NDEV
UQ_SHARD = HEADS * (NOPE + ROPE) // NDEV
W_IN_PAD = 1024
ROW_A, ROW_B, ROW_C, ROW_UKV, ROW_UQ, MISC_ROWS = 0, 512, 1024, 1536, 1792, 2176


def _in_perm_index():
    ar = np.arange
    z = lambda n: np.full((n,), -1, np.int64)
    mix = lambda lo1, lo2: np.concatenate([ar(lo + LANE * j, lo + LANE * (j + 1)) for j in range(CW // LANE)
                                           for lo in (lo1, lo2)])
    return np.concatenate([ar(4768, 7840), mix(0, 512), ar(1024, 1536), mix(1536, 2560), ar(4256, 4768), ar(2048, 2560),
                           ar(3072, 3584), ar(3968, 4224), ar(4224, 4256), z(OFF_Q - OFF_KR - ROPE), ar(3584, 3968),
                           z(NP - OFF_Q - QL)])


def _head_perm_index(a, b):
    parts = []
    for g in range(QUADS):
        h = np.arange(4 * g, 4 * g + 4)[:, None] * (a + b)
        parts += [(h + np.arange(a)[None]).reshape(-1), (h + a + np.arange(b)[None]).reshape(-1)]
    return np.concatenate(parts)


def _inverse(perm, n):
    inv = np.full((n,), -1, np.int64)
    inv[perm[perm >= 0]] = np.nonzero(perm >= 0)[0]
    return inv


IN_PERM = _in_perm_index()
UQ_PERM = _head_perm_index(NOPE, ROPE)
UKV_PERM = _head_perm_index(NOPE, VH)


def _to_gathered(perm, shard, pad):
    return np.where(perm >= 0, (perm // shard) * pad + perm % shard, -1)


def _from_full(inv, shard, pad):
    j, i = np.divmod(np.arange(NDEV * pad), pad)
    return np.where(i < shard, inv[np.minimum(j * shard + i, inv.shape[0] - 1)], -1)


def col_gather(name, srcs, out_shapes, jobs, deps=()):
    ns, nj, nd, no = len(srcs), len(jobs), len(deps), len(out_shapes)
    tables = [jnp.asarray(np.asarray(job[5], np.int32)[None, :]) for job in jobs]

    def view(ref, col0, width, r0, rc):
        n = ref.shape[-1]
        if len(ref.shape) == 3:
            return ref.at[col0 // n, pl.ds(r0, rc), pl.ds(col0 % n, width)]
        return ref.at[pl.ds(r0, rc), pl.ds(col0, width)]

    def slabs(shape):
        if len(shape) == 3:
            return [((d,), d * shape[2], (d + 1) * shape[2]) for d in range(shape[0])]
        w = 1024 if shape[1] > 1024 and shape[1] % 1024 == 0 else shape[1]
        return [((slice(None), pl.ds(c, w)), c, c + w) for c in range(0, shape[1], w)]

    src_slabs = [slabs(s.shape) for s in srcs]
    out_slabs = [slabs(sh) for sh in out_shapes]
    work, first_use, last_touch = [], {}, {}
    for ji, (si, srow, oi, orow, nrows, tgt) in enumerate(jobs):
        tgt = np.asarray(tgt)
        tw = 256 if out_shapes[oi][-1] % 256 == 0 else LANE
        sw = 256 if srcs[si].shape[-1] % 256 == 0 else LANE
        for t in range(tgt.shape[0] // tw):
            tt = tgt[t * tw:(t + 1) * tw]
            tiles = sorted(set((tt[tt >= 0] // sw).tolist()))
            straight = bool(tiles) and tt[0] >= 0 and tt[0] % LANE == 0 and np.array_equal(tt, tt[0] + np.arange(tw))
            cols = [(int(tt[0]) + k * LANE, LANE) for k in range(tw // LANE)] if straight else [(s * sw, sw) for s in tiles]
            need = sorted({(si, k) for c0, _ in cols for k, (_, lo, hi) in enumerate(src_slabs[si]) if lo <= c0 < hi})
            touch = [(oi, k) for k, (_, lo, hi) in enumerate(out_slabs[oi]) if lo <= t * tw < hi][0]
            for key in need:
                first_use.setdefault(key, len(work))
            last_touch[touch] = len(work)
            work.append((ji, t, tw, sw, tiles, straight, need, touch))
    in_order = sorted(first_use, key=first_use.get)
    in_sem = {key: i for i, key in enumerate(in_order)}
    out_keys = sorted(last_touch)
    out_sem = {key: i for i, key in enumerate(out_keys)}

    def body(*refs):
        src_hbm, tab_refs = refs[:ns], refs[ns:ns + nj]
        out_hbm = refs[ns + nj + nd:ns + nj + nd + no]
        scratch = refs[ns + nj + nd + no:]
        src_refs, out_refs, in_sems, out_sems = scratch[:ns], scratch[ns:ns + no], scratch[-2], scratch[-1]
        loads = {}
        for key in in_order:
            si, k = key
            idx = src_slabs[si][k][0]
            loads[key] = pltpu.make_async_copy(src_hbm[si].at[idx], src_refs[si].at[idx], in_sems.at[in_sem[key]])
            loads[key].start()
        arrived, stores = set(), []
        for wi, (ji, t, tw, sw, tiles, straight, need, touch) in enumerate(work):
            si, srow, oi, orow, nrows, tgt = jobs[ji]
            sref, oref = src_refs[si], out_refs[oi]
            rc = nrows if nrows <= 1024 else 1024
            for key in need:
                if key not in arrived:
                    loads[key].wait()
                    arrived.add(key)
            onehots = []
            if tiles and not straight:
                want = tab_refs[ji][:, t * tw:(t + 1) * tw]
                row = lax.broadcasted_iota(jnp.int32, (sw, tw), 0)
                onehots = [jnp.where(want == row + s * sw, 1.0, 0.0).astype(BF) for s in tiles]
            first = int(np.asarray(tgt)[t * tw])

            def chunk(ci, _, t=t, tw=tw, sw=sw, tiles=tiles, straight=straight, onehots=onehots, first=first,
                      sref=sref, oref=oref, srow=srow, orow=orow, rc=rc):
                r0 = ci * rc
                ro = pl.multiple_of(orow + r0, LANE)
                rs = pl.multiple_of(srow + r0, LANE)
                if not tiles:
                    view(oref, t * tw, tw, ro, rc)[...] = jnp.zeros((rc, tw), BF)
                elif straight:
                    for k in range(tw // LANE):
                        view(oref, t * tw + k * LANE, LANE, ro, rc)[...] = view(sref, first + k * LANE, LANE, rs, rc)[...]
                else:
                    acc = None
                    for s, oh in zip(tiles, onehots):
                        p = jnp.dot(view(sref, s * sw, sw, rs, rc)[...], oh, preferred_element_type=F32)
                        acc = p if acc is None else acc + p
                    view(oref, t * tw, tw, ro, rc)[...] = acc.astype(BF)
                return 0

            lax.fori_loop(0, nrows // rc, chunk, 0)
            if last_touch[touch] == wi:
                idx = out_slabs[touch[0]][touch[1]][0]
                cp = pltpu.make_async_copy(out_refs[touch[0]].at[idx], out_hbm[touch[0]].at[idx], out_sems.at[out_sem[touch]])
                cp.start()
                stores.append(cp)
        for cp in stores:
            cp.wait()

    return pl.pallas_call(
        body, name=name, in_specs=[ANY_SPEC] * ns + [pl.BlockSpec(memory_space=pltpu.VMEM)] * nj + [ANY_SPEC] * nd,
        out_specs=[ANY_SPEC] * no, out_shape=[jax.ShapeDtypeStruct(s, BF) for s in out_shapes],
        scratch_shapes=[pltpu.VMEM(s.shape, BF) for s in srcs] + [pltpu.VMEM(s, BF) for s in out_shapes]
        + [pltpu.SemaphoreType.DMA((len(in_order),)), pltpu.SemaphoreType.DMA((len(out_keys),))],
        compiler_params=_cparams(),
    )(*srcs, *tables, *deps)


def sum_adamw(name, recvs, w, m, v, lo=0, prev=None, row0=0):
    _, R, C = w.shape
    L = len(recvs)
    CP = recvs[0].shape[-1]
    tr = _pick_rows(R, 16, 128)
    n_prev = 0 if prev is None else 4

    def body(*refs):
        r_refs = refs[:L]
        w_ref, m_ref, v_ref = refs[L:L + 3]
        g_ref, d_ref, nm_ref, nv_ref, gsum = refs[L + 3 + n_prev:]
        layer = pl.program_id(0)
        for k in range(L):
            def total(k=k):
                acc = r_refs[k][0].astype(F32)
                for d in range(1, NDEV):
                    acc = acc + r_refs[k][d].astype(F32)
                gsum[...] = acc
            pl.when(layer == k)(total)
        gg = gsum[:, 0:C]
        nm = B1 * m_ref[...] + (1.0 - B1) * gg
        nv = B2 * v_ref[...] + (1.0 - B2) * jnp.square(gg)
        m_hat = nm / (1.0 - B1 ** STEP)
        v_hat = nv / (1.0 - B2 ** STEP)
        g_ref[...] = gg
        d_ref[...] = -LR * (m_hat / (jnp.sqrt(v_hat) + EPS) + WD * w_ref[...])
        nm_ref[...] = nm
        nv_ref[...] = nv

    assert row0 % tr == 0
    r_specs = [pl.BlockSpec((NDEV, tr, CP),
                            functools.partial(lambda l, i, k: (0, row0 // tr + jnp.where(l == k, i, 0), 0), k=k))
               for k in range(L)]
    blk = pl.BlockSpec((None, tr, C), lambda l, i: (l + lo, i, 0))
    shp = jax.ShapeDtypeStruct(w.shape, F32)
    return pl.pallas_call(
        body, name=name, grid=(L, R // tr), in_specs=r_specs + [blk] * 3 + [ANY_SPEC] * n_prev, out_specs=[blk] * 4,
        out_shape=[shp] * 4, input_output_aliases={L + 3 + i: i for i in range(n_prev)},
        scratch_shapes=[pltpu.VMEM((tr, CP), F32)], compiler_params=_cparams(),
    )(*recvs, w, m, v, *(prev or ()))


ALPHA = 8.0 ** 0.25
T_WIDE, T_NARROW = 512, 1024


def _rope_fn(sign):
    def fn(x, cos, sin):
        W = x.shape[-1]
        lane = lax.broadcasted_iota(jnp.int32, x.shape, 1)
        first_half = (lane % ROPE) < (ROPE // 2)
        rot = jnp.where(first_half, -pltpu.roll(x, W - ROPE // 2, 1), pltpu.roll(x, ROPE // 2, 1))
        return x * cos + sign * rot * sin
    return fn


def layer_fwd(x, ada3, W, tabs, S):
    cos, sin = tabs
    T = T_NARROW
    u = rowwise("modulate", lambda xv, a: xv * (1.0 + a[1:2, :]) + a[0:1, :], S, T,
                [(x, D_MODEL, 0)], [ada3], [(D_MODEL, BF)])[0]
    proj = mm(u, W["in"], name="mm_proj", tm=1024, tn=1024, out_dtype=BF)
    W = {**W, **W["late"](proj)}

    ca = conv_fwd("conv_a_fwd", proj, OFF_A, W["conv_a"], 31, "glu", S, CW)

    def a_post(c, ag, vec):
        n, _ = _ln_stats(c + vec[0:1, :])
        return _silu(n * vec[1:2, :] + vec[2:3, :]) * _silu(ag)

    h_a = rowwise("mix_a_post", a_post, S, T, [(ca, CW, 0), (proj, CW, OFF_AG)], [W["vec_a"]], [(CW, BF)])[0]
    y_a = mm(h_a, W["a_out"], name="mm_branch_out", out_dtype=BF)

    cb = conv_fwd("conv_b_fwd", proj, OFF_B, W["conv_b"], 3, "mul", S, CW)
    h_b = rowwise("mix_b_post", lambda c, gb, bg: gb * c * _silu(bg), S, T,
                  [(cb, CW, 0), (proj, CW, OFF_GB), (proj, CW, OFF_BG)], [], [(CW, BF)])[0]
    y_b = mm(h_b, W["b_out"], name="mm_branch_out", out_dtype=BF)

    def rms2(ql, kvl, gq, gkv):
        rq = lax.rsqrt(jnp.mean(ql * ql, axis=-1, keepdims=True) + RMS_EPS)
        rk = lax.rsqrt(jnp.mean(kvl * kvl, axis=-1, keepdims=True) + RMS_EPS)
        return ql * rq * gq, kvl * rk * gkv

    qn, kvn = rowwise("rms_fwd", rms2, S, T, [(proj, QL, OFF_Q), (proj, KVL, OFF_KV)], [W["gq"], W["gkv"]],
                      [(QL, BF), (KVL, BF)])
    q = mm(qn, W["uq"], name="mm_q")
    kv = mm(kvn, W["ukv"], name="mm_kv", out_dtype=BF)
    rope = _rope_fn(1.0)

    def rope_fwd(qv, kr, c1, s1):
        parts = []
        for g in range(QUADS):
            parts.append(qv[:, g * QW:g * QW + 2 * LANE].astype(BF))
            parts.append(rope(qv[:, g * QW + 2 * LANE:(g + 1) * QW], c1, s1).astype(BF))
        kp = rope(kr, c1, s1)
        kp = kp + pltpu.roll(kp, ROPE, 1) + pltpu.roll(kp, 2 * ROPE, 1) + pltpu.roll(kp, 3 * ROPE, 1)
        return jnp.concatenate(parts, axis=1), kp

    q_b, kpe = rowwise("rope_fwd", rope_fwd, S, T,
                       [(q, HEADS * (NOPE + ROPE), 0), (proj, LANE, OFF_KR), (cos, LANE, 0), (sin, LANE, 0)], [],
                       [(HEADS * (NOPE + ROPE), BF), (LANE, BF)])
    o, lse = attn_fwd(q_b, kv, kpe, S)
    h_c = rowwise("mix_c_post", lambda ov, cg: ov * _silu(cg), S, T, [(o, CW, 0), (proj, CW, OFF_CG)], [],
                  [(CW, BF)])[0]
    y_c = mm(h_c, W["c_out"], name="mm_branch_out", out_dtype=BF)

    def merge(la, lb, lc, ya, yb, yc):
        return _sigmoid(la) * ya + _sigmoid(lb) * yb + _sigmoid(lc) * yc

    m = rowwise("merge_fwd", merge, S, T_WIDE,
                [(proj, D_MODEL, 0), (proj, D_MODEL, 1024), (proj, D_MODEL, 2048), (y_a, D_MODEL, 0),
                 (y_b, D_MODEL, 0), (y_c, D_MODEL, 0)], [], [(D_MODEL, BF)])[0]
    out = mm(m, W["o"], name="mm_out")

    def ln_fwd(xv, ov, a, lnv):
        n, _ = _ln_stats(ALPHA * xv + a[2:3, :] * ov)
        return n * lnv[0:1, :] + lnv[1:2, :]

    x_next = rowwise("ln_fwd", ln_fwd, S, T_WIDE, [(x, D_MODEL, 0), (out, D_MODEL, 0)], [ada3, W["lnv"]],
                     [(D_MODEL, F32)])[0]
    saved = dict(x=x, u=u, proj=proj, ca=ca, cb=cb, h_a=h_a, h_b=h_b, h_c=h_c, y_a=y_a, y_b=y_b, y_c=y_c, qn=qn,
                 kvn=kvn, q_b=q_b, kv=kv, kpe=kpe, lse=lse, o=o, m=m, out=out)
    return x_next, saved, W


def layer_bwd(dxn, sv, ada3, W, tabs, S, before_in=None):
    cos, sin = tabs
    T = T_NARROW
    x, proj = sv["x"], sv["proj"]
    G = {}

    def ln_bwd(xv, ov, dy, a, lnv):
        gate = a[2:3, :]
        n, rstd = _ln_stats(ALPHA * xv + gate * ov)
        dr = _ln_bwd(dy * lnv[0:1, :], n, rstd)
        return ALPHA * dr, gate * dr, _colsum(dy * n), _colsum(dy), _colsum(dr * ov)

    dres, d_out, G["ln_g"], G["ln_b"], d_gate = rowwise(
        "ln_bwd", ln_bwd, S, T_WIDE, [(x, D_MODEL, 0), (sv["out"], D_MODEL, 0), (dxn, D_MODEL, 0)], [ada3, W["lnv"]],
        [(D_MODEL, F32), (D_MODEL, BF)], [D_MODEL] * 3)
    dm = mm(d_out, W["o"], name="mm_dm", trans_b=True, out_dtype=BF)
    G["w_o"] = mm(sv["m"], d_out, name="mm_gw_o", trans_a=True, out_dtype=BF)

    def merge_bwd(dmv, la, lb, lc, ya, yb, yc):
        outs, dls = [], []
        for lg, yv in ((la, ya), (lb, yb), (lc, yc)):
            s = _sigmoid(lg)
            outs.append(dmv * s)
            dls.append((dmv * yv * s * (1.0 - s)).astype(BF))
        return (jnp.concatenate(dls, axis=1),) + tuple(outs)

    d_proj, dy_a, dy_b, dy_c = rowwise(
        "merge_bwd", merge_bwd, S, T_WIDE,
        [(dm, D_MODEL, 0), (proj, D_MODEL, 0), (proj, D_MODEL, 1024), (proj, D_MODEL, 2048), (sv["y_a"], D_MODEL, 0),
         (sv["y_b"], D_MODEL, 0), (sv["y_c"], D_MODEL, 0)], [], [(3 * D_MODEL, BF)] + [(D_MODEL, BF)] * 3,
        into=(None, NP, OFF_M))

    dh = {}
    for br, dy in (("a", dy_a), ("b", dy_b), ("c", dy_c)):
        dh[br] = mm(dy, W[br + "_out"], name="mm_dh", trans_b=True, out_dtype=BF)
        G["w_%s_out" % br] = mm(sv["h_" + br], dy, name="mm_gw_branch", trans_a=True, out_dtype=BF)

    def a_post_bwd(c, ag, dhv, vec):
        n, rstd = _ln_stats(c + vec[0:1, :])
        z = n * vec[1:2, :] + vec[2:3, :]
        d_ag = dhv * _silu(z) * _dsilu(ag)
        dz = dhv * _silu(ag) * _dsilu(z)
        dc = _ln_bwd(dz * vec[1:2, :], n, rstd)
        return d_ag, dc, _colsum(dc), _colsum(dz * n), _colsum(dz)

    d_proj, dca, G["conv_a_b"], G["ln_a_g"], G["ln_a_b"] = rowwise(
        "mix_a_post_bwd", a_post_bwd, S, T, [(sv["ca"], CW, 0), (proj, CW, OFF_AG), (dh["a"], CW, 0)], [W["vec_a"]],
        [(CW, BF), (CW, F32)], [CW] * 3, into=(d_proj, NP, OFF_AG))
    d_proj, G["conv_a_w"] = conv_bwd("conv_a_bwd", proj, OFF_A, dca, W["conv_a"], 31, "glu", S, CW, d_proj)

    def b_post_bwd(c, gb, bg, dhv):
        sg = _silu(bg)
        d_gb_bg = jnp.concatenate([(dhv * sg * c).astype(BF), (dhv * gb * c * _dsilu(bg)).astype(BF)], axis=1)
        return d_gb_bg, dhv * sg * gb

    d_proj, dcb = rowwise("mix_b_post_bwd", b_post_bwd, S, T,
                          [(sv["cb"], CW, 0), (proj, CW, OFF_GB), (proj, CW, OFF_BG), (dh["b"], CW, 0)], [],
                          [(2 * CW, BF), (CW, F32)], into=(d_proj, NP, OFF_GB))
    d_proj, G["conv_b_w"] = conv_bwd("conv_b_bwd", proj, OFF_B, dcb, W["conv_b"], 3, "mul", S, CW, d_proj)

    d_proj, d_o = rowwise("mix_c_post_bwd", lambda ov, cg, dhv: (dhv * ov * _dsilu(cg), dhv * _silu(cg)), S, T,
                          [(sv["o"], CW, 0), (proj, CW, OFF_CG), (dh["c"], CW, 0)], [], [(CW, BF), (CW, F32)],
                          into=(d_proj, NP, OFF_CG))
    dq, d_kv, dkp_heads = attn_bwd(sv["q_b"], sv["kv"], sv["kpe"], sv["o"], sv["lse"], d_o, S)
    ropeT = _rope_fn(-1.0)

    def rope_bwd(dqv, dkp, c1, s1):
        parts = []
        for g in range(QUADS):
            parts.append(dqv[:, g * QW:g * QW + 2 * LANE].astype(BF))
            parts.append(ropeT(dqv[:, g * QW + 2 * LANE:(g + 1) * QW], c1, s1).astype(BF))
        f = dkp[:, :LANE] + dkp[:, LANE:]
        f = f + pltpu.roll(f, 64, 1)
        f = f + pltpu.roll(f, 32, 1)
        lane = lax.broadcasted_iota(jnp.int32, f.shape, 1)
        return jnp.concatenate(parts, axis=1), jnp.where(lane < ROPE, ropeT(f, c1, s1), 0.0)

    d_q, dk_pe = rowwise("rope_bwd", rope_bwd, S, T,
                         [(dq, HEADS * (NOPE + ROPE), 0), (dkp_heads, HEADS * ROPE, 0), (cos, LANE, 0), (sin, LANE, 0)],
                         [], [(HEADS * (NOPE + ROPE), BF), (LANE, BF)])
    d_qn = mm(d_q, W["uq"], name="mm_dqn", trans_b=True, out_dtype=BF)
    d_kvn = mm(d_kv, W["ukv"], name="mm_dkvn", trans_b=True, out_dtype=BF)
    G["w_uq"] = mm(sv["qn"], d_q, name="mm_gw_uq", trans_a=True, out_dtype=BF)
    G["w_ukv"] = mm(sv["kvn"], d_kv, name="mm_gw_ukv", trans_a=True, out_dtype=BF)

    def rms_bwd(ql, kvl, dqn, dkn, dkp, gq, gkv):
        res = []
        for xv, dy, g in ((ql, dqn, gq), (kvl, dkn, gkv)):
            r = lax.rsqrt(jnp.mean(xv * xv, axis=-1, keepdims=True) + RMS_EPS)
            dxh = dy * g
            res.append(((r * (dxh - xv * (r * r) * jnp.mean(dxh * xv, axis=-1, keepdims=True))).astype(BF),
                        _colsum(dy * xv * r)))
        pad = jnp.zeros((ql.shape[0], LANE), BF)
        return jnp.concatenate([res[1][0], dkp, pad, res[0][0], pad], axis=1), res[0][1], res[1][1]

    d_proj, G["q_norm_g"], G["kv_norm_g"] = rowwise(
        "rms_bwd", rms_bwd, S, T,
        [(proj, QL, OFF_Q), (proj, KVL, OFF_KV), (d_qn, QL, 0), (d_kvn, KVL, 0), (dk_pe, LANE, 0)],
        [W["gq"], W["gkv"]], [(NP - OFF_KV, BF)], [QL, KVL], into=(d_proj, NP, OFF_KV))
    deps = before_in(G) if before_in is not None else ()
    du = mm(d_proj, W["in"], name="mm_du", trans_b=True, tk=2048, deps=deps)
    G["w_in"] = mm(sv["u"], d_proj, name="mm_gw_in", trans_a=True, out_dtype=BF, deps=deps)

    def mod_bwd(duv, xv, dr, a):
        return duv * (1.0 + a[1:2, :]) + dr, _colsum(duv), _colsum(duv * xv)

    dx, d_shift, d_scale = rowwise("mod_bwd", mod_bwd, S, T_WIDE, [(du, D_MODEL, 0), (x, D_MODEL, 0), (dres, D_MODEL, 0)],
                                   [ada3], [(D_MODEL, F32)], [D_MODEL] * 2)
    d_ada = jnp.concatenate([d_shift, d_scale, d_gate], axis=1)
    return dx, G, d_ada


SMALL = ("conv_a_b", "ln_a_g", "ln_a_b", "q_norm_g", "kv_norm_g", "ln_g", "ln_b")


def _rows(v):
    n = v.shape[0]
    r = -(-n // (LANE * 16)) * 16
    return jnp.pad(v, (0, r * LANE - n)).reshape(r, LANE)


def kernel(x, c, positions, w_ada, b_ada, w_in, conv_a_w, conv_a_b, ln_a_g, ln_a_b, w_a_out, conv_b_w, w_b_out, q_norm_g, kv_norm_g, w_uq, w_ukv, w_c_out, w_o, ln_g, ln_b, loss_target, m_w_ada, m_b_ada, m_w_in, m_conv_a_w, m_conv_a_b, m_ln_a_g, m_ln_a_b, m_w_a_out, m_conv_b_w, m_w_b_out, m_q_norm_g, m_kv_norm_g, m_w_uq, m_w_ukv, m_w_c_out, m_w_o, m_ln_g, m_ln_b, v_w_ada, v_b_ada, v_w_in, v_conv_a_w, v_conv_a_b, v_ln_a_g, v_ln_a_b, v_w_a_out, v_conv_b_w, v_w_b_out, v_q_norm_g, v_kv_norm_g, v_w_uq, v_w_ukv, v_w_c_out, v_w_o, v_ln_g, v_ln_b):
    P = dict(w_ada=w_ada, b_ada=b_ada, w_in=w_in, conv_a_w=conv_a_w, conv_a_b=conv_a_b, ln_a_g=ln_a_g, ln_a_b=ln_a_b,
             w_a_out=w_a_out, conv_b_w=conv_b_w, w_b_out=w_b_out, q_norm_g=q_norm_g, kv_norm_g=kv_norm_g, w_uq=w_uq,
             w_ukv=w_ukv, w_c_out=w_c_out, w_o=w_o, ln_g=ln_g, ln_b=ln_b)
    Mo = dict(w_ada=m_w_ada, b_ada=m_b_ada, w_in=m_w_in, conv_a_w=m_conv_a_w, conv_a_b=m_conv_a_b, ln_a_g=m_ln_a_g,
              ln_a_b=m_ln_a_b, w_a_out=m_w_a_out, conv_b_w=m_conv_b_w, w_b_out=m_w_b_out, q_norm_g=m_q_norm_g,
              kv_norm_g=m_kv_norm_g, w_uq=m_w_uq, w_ukv=m_w_ukv, w_c_out=m_w_c_out, w_o=m_w_o, ln_g=m_ln_g, ln_b=m_ln_b)
    Vo = dict(w_ada=v_w_ada, b_ada=v_b_ada, w_in=v_w_in, conv_a_w=v_conv_a_w, conv_a_b=v_conv_a_b, ln_a_g=v_ln_a_g,
              ln_a_b=v_ln_a_b, w_a_out=v_w_a_out, conv_b_w=v_conv_b_w, w_b_out=v_w_b_out, q_norm_g=v_q_norm_g,
              kv_norm_g=v_kv_norm_g, w_uq=v_w_uq, w_ukv=v_w_ukv, w_c_out=v_w_c_out, w_o=v_w_o, ln_g=v_ln_g, ln_b=v_ln_b)
    ORDER = ("w_ada", "b_ada", "w_in", "conv_a_w", "conv_a_b", "ln_a_g", "ln_a_b", "w_a_out", "conv_b_w", "w_b_out",
             "q_norm_g", "kv_norm_g", "w_uq", "w_ukv", "w_c_out", "w_o", "ln_g", "ln_b")
    L = w_ada.shape[0]
    S = x.shape[1]
    me = 4 * lax.axis_index("x") + 2 * lax.axis_index("y") + lax.axis_index("c")
    x2 = x[0]
    tgt = loss_target[0]

    small_in = _rows(jnp.concatenate([c.reshape(-1), conv_a_w.reshape(-1), conv_b_w.reshape(-1)]))
    w_in_b = jnp.pad(w_in.astype(BF), ((0, 0), (0, 0), (0, W_IN_PAD - IN_SHARD)))
    misc_b = jnp.concatenate([w_a_out, w_b_out, w_c_out, w_ukv, jnp.pad(w_uq, ((0, 0), (0, 0), (0, LANE - UQ_SHARD)))],
                             axis=1).astype(BF)
    w_o_b = w_o.astype(BF)
    gathered = [None] * L
    sg = exchange("gather_small", [small_in], [])[0]
    sgf = sg.reshape(NDEV, -1)
    c_all = sgf[:, :D_MODEL]
    o1 = D_MODEL + L * 31 * 64
    conv_a_full = sgf[:, D_MODEL:o1].reshape(NDEV, L, 31, 64).transpose(1, 2, 0, 3).reshape(L, 31, CW)
    conv_b_full = sgf[:, o1:o1 + L * 3 * 64].reshape(NDEV, L, 3, 64).transpose(1, 2, 0, 3).reshape(L, 3, CW)

    c_act = rowwise("silu_c", _silu, 16, 16, [(jnp.pad(c_all, ((0, 8), (0, 0))), D_MODEL, 0)], [], [(D_MODEL, BF)])[0]
    ncol = w_ada.shape[2]
    w_ada_b = w_ada.astype(BF).transpose(1, 0, 2).reshape(D_MODEL, L * ncol)
    b_mine = lax.dynamic_slice_in_dim(b_ada, me * ncol, ncol, axis=1).reshape(1, L * ncol)
    ada_part = mm(c_act, w_ada_b, name="mm_ada", bias=b_mine)
    ada_rows = -(-(L * ncol) // (LANE * 8)) * 8
    ada_send = jnp.pad(ada_part[:NDEV].reshape(NDEV, -1, LANE), ((0, 0), (0, ada_rows - L * ncol // LANE), (0, 0)))
    ada_recv = exchange("a2a_ada", [], [ada_send])[0]
    ada = ada_recv[:, :L * ncol // LANE].reshape(NDEV, L, ncol).transpose(1, 0, 2).reshape(L, 3, D_MODEL)
    gathered[0] = [gather_two_level("gather0_w_in", w_in_b[0], ada)]
    pending_rest, rest_token = exchange_begin("gather0_rest", [misc_b[0], w_o_b[0]], 2, gathered[0][0])

    inv_freq = ROPE_THETA ** (-jnp.arange(0, ROPE, 2, dtype=F32) / ROPE)
    ang = positions[0].astype(F32)[:, None] * inv_freq
    tabs = (jnp.tile(jnp.cos(ang), (1, 2 * LANE // ROPE)), jnp.tile(jnp.sin(ang), (1, 2 * LANE // ROPE)))

    straight = np.arange(D_MODEL)
    fwd_in = [(0, 0, 0, 0, D_MODEL, _to_gathered(IN_PERM, IN_SHARD, W_IN_PAD))]
    fwd_misc = [(0, ROW_A, 0, 0, CW, straight), (0, ROW_B, 1, 0, CW, straight), (0, ROW_C, 2, 0, CW, straight),
                (0, ROW_UKV, 3, 0, KVL, UKV_PERM), (0, ROW_UQ, 4, 0, QL, _to_gathered(UQ_PERM, UQ_SHARD, LANE))]
    rev_in = [(0, 0, 0, 0, D_MODEL, _from_full(_inverse(IN_PERM, D_IN), IN_SHARD, W_IN_PAD))]
    rev_misc = [(0, 0, 0, ROW_A, CW, straight), (1, 0, 0, ROW_B, CW, straight), (2, 0, 0, ROW_C, CW, straight),
                (3, 0, 0, ROW_UKV, KVL, _from_full(_inverse(UKV_PERM, HEADS * (NOPE + VH)), LANE, LANE)),
                (4, 0, 0, ROW_UQ, QL, _from_full(_inverse(UQ_PERM, HEADS * (NOPE + ROPE)), UQ_SHARD, LANE))]

    def layer_weights(l, deps):
        w_in_p = col_gather("relayout_w_in", [gathered[l][0]], [(D_MODEL, NP)], fwd_in, deps)[0]

        def late(after):
            if len(gathered[l]) == 1:
                gathered[l] += exchange_end(pending_rest, after)
            _, g_misc, g_o = gathered[l]
            a_out, b_out, c_out, ukv, uq = col_gather(
                "relayout_misc", [g_misc],
                [(CW, D_MODEL)] * 3 + [(KVL, HEADS * (NOPE + VH)), (QL, HEADS * (NOPE + ROPE))], fwd_misc, deps)
            return {"a_out": a_out, "b_out": b_out, "c_out": c_out, "uq": uq, "ukv": ukv,
                    "o": g_o.reshape(D_MODEL, D_MODEL)}

        return {
            "in": w_in_p, "late": late,
            "conv_a": jnp.pad(conv_a_full[l], ((0, 1), (0, 0))), "conv_b": jnp.pad(conv_b_full[l], ((0, 5), (0, 0))),
            "vec_a": jnp.stack([conv_a_b[l], ln_a_g[l], ln_a_b[l]]), "gq": q_norm_g[l][None], "gkv": kv_norm_g[l][None],
            "lnv": jnp.stack([ln_g[l], ln_b[l]]),
        }

    h = x2
    saved, weights = [], []
    handles, token = {}, rest_token
    for l in range(1, L):
        handles[l], token = exchange_begin("gather%d" % l, [w_in_b[l], misc_b[l], w_o_b[l]], 3, token)
    for l in range(L):
        ada_l, deps = (ada[l] + token[0, 0], (token,)) if l == 0 else (ada[l], ())
        h, sv, Wl = layer_fwd(h, ada_l, layer_weights(l, deps), tabs, S)
        if l + 1 < L:
            gathered[l + 1] = exchange_end(handles[l + 1], h)
        saved.append(sv)
        weights.append(Wl)

    def loss_fn(y, t):
        e = y - t
        return e * (1.0 / D_MODEL), _colsum(e * e)

    dy, sq = rowwise("loss", loss_fn, S, 256, [(h, D_MODEL, 0), (tgt, D_MODEL, 0)], [], [(D_MODEL, F32)], [D_MODEL])
    loss = lax.psum(0.5 * jnp.sum(sq) / D_MODEL, ("x", "y", "c"))
    loss, dy = lax.optimization_barrier((loss, dy))

    grads, d_adas, recv = [None] * L, [None] * L, [None] * L
    pending, token = None, None

    def send_rest(g):
        send_misc = col_gather("unrelayout_misc", [g["w_a_out"], g["w_b_out"], g["w_c_out"], g["w_ukv"], g["w_uq"]],
                               [(NDEV, MISC_ROWS, LANE)], rev_misc)[0]
        return [send_misc, g["w_o"].reshape(NDEV, D_MODEL // NDEV, D_MODEL)]

    rest0 = []

    def early_rest(g):
        handle, tok = exchange_begin("scatter0_rest", send_rest(g), 0, g["w_o"])
        rest0.append(handle)
        return (tok,)

    for l in reversed(range(L)):
        ada_l = ada[l] if token is None else ada[l] + token[0, 0]
        dy, g, d_adas[l] = layer_bwd(dy, saved[l], ada_l, weights[l], tabs, S, early_rest if l == 0 else None)
        grads[l] = g
        if pending is not None:
            recv[l + 1] = exchange_end(pending, dy)
        send_in = col_gather("unrelayout_w_in", [g["w_in"]], [(NDEV, D_MODEL, W_IN_PAD)], rev_in)[0]
        if l == 0:
            def layer_vec(i):
                parts = [grads[i][n].reshape(-1) for n in SMALL]
                parts += [grads[i]["conv_a_w"][:31].reshape(-1), grads[i]["conv_b_w"][:3].reshape(-1),
                          d_adas[i].reshape(-1)]
                return jnp.concatenate(parts)

            small_sizes = [int(grads[0][n].size) for n in SMALL] + [31 * CW, 3 * CW, 3 * D_MODEL]
            gsmall = exchange("gather_small_grads", [_rows(jnp.concatenate([layer_vec(i) for i in range(L)]))], [])[0]
            pending, token = exchange_begin("scatter0", [send_in], 0, gsmall)
        else:
            pending, token = exchange_begin("scatter%d" % l, [send_in] + send_rest(g), 0,
                                            dy if l + 1 == L else recv[l + 1][0])
    grad_x = dy[None]

    gsmall = gsmall + token[0, 0]
    gsum = sum_slots("sum_small", gsmall).reshape(-1)
    recv[0] = [None] + exchange_end(rest0[0], gsum)
    Gr = {}
    offs = np.cumsum([0] + small_sizes)
    per_layer = int(offs[-1])
    gsum = gsum[:L * per_layer].reshape(L, per_layer)
    for i, n in enumerate(SMALL):
        Gr[n] = gsum[:, offs[i]:offs[i + 1]]
    ca = gsum[:, offs[7]:offs[8]].reshape(L, 31, CW)
    cbw = gsum[:, offs[8]:offs[9]].reshape(L, 3, CW)
    Gr["conv_a_w"] = lax.dynamic_slice_in_dim(ca, me * 64, 64, axis=2)
    Gr["conv_b_w"] = lax.dynamic_slice_in_dim(cbw, me * 64, 64, axis=2)
    Gr["b_ada"] = gsum[:, offs[9]:offs[10]]
    d_ada_all = gsmall.reshape(NDEV, -1)[:, :L * per_layer].reshape(NDEV, L, per_layer)[:, :, offs[9]:offs[10]]
    d_mine = lax.dynamic_slice_in_dim(d_ada_all, me * ncol, ncol, axis=2).reshape(NDEV, L * ncol)
    g_ada = mm(c_act, jnp.pad(d_mine, ((0, 8), (0, 0))).astype(BF), name="mm_gw_ada", trans_a=True)
    Gr["w_ada"] = g_ada.reshape(D_MODEL, L, ncol).transpose(1, 0, 2)

    D, NM, NV = {}, {}, {}
    D["w_ada"], NM["w_ada"], NV["w_ada"] = adamw("adamw_w_ada", P["w_ada"], Gr["w_ada"], Mo["w_ada"], Vo["w_ada"])
    Gr["w_o"], D["w_o"], NM["w_o"], NV["w_o"] = sum_adamw(
        "sum_adamw_w_o", [recv[l][2] for l in range(L)], P["w_o"], Mo["w_o"], Vo["w_o"])
    for n, row0 in (("w_a_out", ROW_A), ("w_b_out", ROW_B), ("w_c_out", ROW_C), ("w_ukv", ROW_UKV), ("w_uq", ROW_UQ)):
        Gr[n], D[n], NM[n], NV[n] = sum_adamw("sum_adamw_" + n, [recv[l][1] for l in range(L)], P[n], Mo[n], Vo[n],
                                              row0=row0)
    w_l, m_l, v_l, _ = lax.optimization_barrier((P["w_in"], Mo["w_in"], Vo["w_in"], token))
    upper = sum_adamw("sum_adamw_w_in_upper", [recv[l][0] for l in range(1, L)], w_l, m_l, v_l, lo=1)
    recv[0][0] = exchange_end(pending, upper[1])[0]
    Gr["w_in"], D["w_in"], NM["w_in"], NV["w_in"] = sum_adamw(
        "sum_adamw_w_in", [recv[0][0]], w_l, m_l, v_l, lo=0, prev=upper)
    packed =("b_ada", "conv_a_w", "conv_b_w") + SMALL
    pk = lambda T_: _rows(jnp.concatenate([T_[n].reshape(-1) for n in packed]))[None]
    dS, mS, vS = adamw("adamw_small", pk(P), pk(Gr), pk(Mo), pk(Vo))
    o = 0
    for n in packed:
        sz = int(np.prod(P[n].shape))
        D[n] = dS.reshape(-1)[o:o + sz].reshape(P[n].shape)
        NM[n] = mS.reshape(-1)[o:o + sz].reshape(P[n].shape)
        NV[n] = vS.reshape(-1)[o:o + sz].reshape(P[n].shape)
        o += sz
    return (loss, grad_x, *[Gr[n] for n in ORDER], *[D[n] for n in ORDER], *[NM[n] for n in ORDER],
            *[NV[n] for n in ORDER])
```

```python
import functools
import math

import numpy as np
import jax
import jax.numpy as jnp
from jax import lax
from jax.experimental import pallas as pl
from jax.experimental.pallas import tpu as pltpu

BF = jnp.bfloat16
F32 = jnp.float32
MESH = pl.DeviceIdType.MESH
NDEV = 8

HEADS, NOPE, ROPE, VH = 8, 64, 32, 64
HP = 128
ROPE_THETA = 10000.0
LN_EPS = 1e-5
RMS_EPS = 1e-6
LR, B1, B2, EPS, WD, STEP = 0.001, 0.9, 0.999, 1e-08, 0.01, 10

LANE = 128
VMEM_LIMIT = 56 * 1024 * 1024

D_MODEL, CW, QL, KVL = 1024, 512, 384, 256
OFF_M, OFF_A, OFF_AG, OFF_B, OFF_CG, OFF_GB, OFF_BG = 0, 3072, 4096, 4608, 5632, 6144, 6656
OFF_KV, OFF_KR, OFF_Q, NP = 7168, 7424, 7680, 8192
D_IN = 7840


def _cparams(**kw):
    return pltpu.CompilerParams(vmem_limit_bytes=VMEM_LIMIT, **kw)


def _sigmoid(x):
    return jax.nn.sigmoid(x)


def _silu(x):
    return x * _sigmoid(x)


def _dsilu(x):
    s = _sigmoid(x)
    return s * (1.0 + x * (1.0 - s))


def _pick_tile(n, cap, mult):
    if n <= cap:
        return n
    for t in range(cap - cap % mult, 0, -mult):
        if n % t == 0:
            return t
    raise ValueError((n, cap, mult))


def mm(a, b, *, name, trans_a=False, trans_b=False, out_dtype=F32, bias=None, tm=512, tn=1024, tk=None, deps=()):
    tk = tk or (512 if trans_a else 2048)
    if trans_a:
        K, M = a.shape
    else:
        M, K = a.shape
    if trans_b:
        N, K2 = b.shape
    else:
        K2, N = b.shape
    assert K == K2 and not (trans_a and trans_b), (a.shape, b.shape)
    tm, tn = _pick_tile(M, tm, 16), _pick_tile(N, tn, LANE)
    tk = _pick_tile(K, tk, LANE if trans_b else 16)
    assert M % tm == 0 and N % tn == 0 and K % tk == 0, (M, N, K, tm, tn, tk)
    nk = K // tk
    dims = (((0 if trans_a else 1,), (1 if trans_b else 0,)), ((), ()))
    has_bias = bias is not None

    def body(*refs):
        a_ref, b_ref = refs[0], refs[1]
        bias_ref = refs[2] if has_bias else None
        o_ref = refs[(3 if has_bias else 2) + len(deps)]
        p = lax.dot_general(a_ref[...], b_ref[...], dims, preferred_element_type=F32)

        def finish(v):
            if has_bias:
                v = v + bias_ref[...]
            o_ref[...] = v.astype(o_ref.dtype)

        if nk == 1:
            finish(p)
        else:
            acc = refs[-1]
            k = pl.program_id(2)

            @pl.when(k == 0)
            def _():
                acc[...] = p

            @pl.when(k > 0)
            def _():
                acc[...] += p

            @pl.when(k == nk - 1)
            def _():
                finish(acc[...])

    if trans_a:
        a_spec = pl.BlockSpec((tk, tm), lambda i, j, k: (k, i))
    else:
        a_spec = pl.BlockSpec((tm, tk), lambda i, j, k: (i, k))
    if trans_b:
        b_spec = pl.BlockSpec((tn, tk), lambda i, j, k: (j, k))
    else:
        b_spec = pl.BlockSpec((tk, tn), lambda i, j, k: (k, j))
    in_specs = [a_spec, b_spec]
    args = [a, b]
    if has_bias:
        in_specs.append(pl.BlockSpec((1, tn), lambda i, j, k: (0, j)))
        args.append(bias)
    in_specs += [ANY_SPEC] * len(deps)
    args += list(deps)
    return pl.pallas_call(
        body, name=name, grid=(M // tm, N // tn, nk),
        in_specs=in_specs, out_specs=pl.BlockSpec((tm, tn), lambda i, j, k: (i, j)),
        out_shape=jax.ShapeDtypeStruct((M, N), out_dtype),
        scratch_shapes=[pltpu.VMEM((tm, tn), F32)] if nk > 1 else [],
        compiler_params=_cparams(),
    )(*args)


def rowwise(name, fn, S, T, row_ins, full_ins, row_outs, acc_outs=(), into=None):
    n_in = len(row_ins) + len(full_ins)
    n_ro, n_ao = len(row_outs), len(acc_outs)
    alias = into is not None and into[0] is not None
    T = min(T, S)

    def body(*refs):
        vals = [r[...] for r in refs[:n_in]]
        vals = [v.astype(F32) if v.dtype == BF else v for v in vals]
        outs = fn(*vals)
        if not isinstance(outs, (tuple, list)):
            outs = (outs,)
        assert len(outs) == n_ro + n_ao, (name, len(outs))
        o0 = n_in + (1 if alias else 0)
        for r, v in zip(refs[o0:o0 + n_ro], outs[:n_ro]):
            r[...] = v.astype(r.dtype)
        first = pl.program_id(0) == 0
        for r, v in zip(refs[o0 + n_ro:], outs[n_ro:]):
            def init(r=r, v=v):
                r[...] = v

            def accum(r=r, v=v):
                r[...] += v

            pl.when(first)(init)
            pl.when(jnp.logical_not(first))(accum)

    in_specs, args = [], []
    for arr, W, off in row_ins:
        assert off % W == 0 and arr.shape[0] == S, (name, arr.shape, W, off)
        in_specs.append(pl.BlockSpec((T, W), functools.partial(lambda i, cb: (i, cb), cb=off // W)))
        args.append(arr)
    for arr in full_ins:
        in_specs.append(pl.BlockSpec(arr.shape, lambda i: (0, 0)))
        args.append(arr)
    out_specs = [pl.BlockSpec((T, W), lambda i: (i, 0)) for W, _ in row_outs]
    out_shape = [jax.ShapeDtypeStruct((S, W), dt) for W, dt in row_outs]
    aliases = {}
    if into is not None:
        buf, total, off = into
        W0, dt0 = row_outs[0]
        assert off % W0 == 0
        out_specs[0] = pl.BlockSpec((T, W0), functools.partial(lambda i, cb: (i, cb), cb=off // W0))
        out_shape[0] = jax.ShapeDtypeStruct((S, total), dt0)
        if alias:
            in_specs.append(ANY_SPEC)
            args.append(buf)
            aliases = {n_in: 0}
    out_specs += [pl.BlockSpec((1, W), lambda i: (0, 0)) for W in acc_outs]
    out_shape += [jax.ShapeDtypeStruct((1, W), F32) for W in acc_outs]
    return pl.pallas_call(
        body, name=name, grid=(S // T,), in_specs=in_specs, out_specs=out_specs, out_shape=out_shape,
        input_output_aliases=aliases, compiler_params=_cparams(),
    )(*args)


def _colsum(v):
    return jnp.sum(v, axis=0, keepdims=True)


def _ln_stats(r):
    mu = jnp.mean(r, axis=-1, keepdims=True)
    d = r - mu
    var = jnp.mean(d * d, axis=-1, keepdims=True)
    rstd = lax.rsqrt(var + LN_EPS)
    return d * rstd, rstd


def _ln_bwd(dn, n, rstd):
    return rstd * (dn - jnp.mean(dn, axis=-1, keepdims=True) - n * jnp.mean(dn * n, axis=-1, keepdims=True))


CPAD = 32
TC = 64


def _pre(mode, x1, x2):
    return x1 * _sigmoid(x2) if mode == "glu" else x1 * x2


def _shifted(ext, sft):
    n = TC + CPAD
    return pltpu.roll(ext, (n - sft) % n, 0)[0:TC]


def _interleaved_specs(S, off):
    return [pl.BlockSpec((S, LANE), functools.partial(lambda j, o: (0, o + 2 * j), o=off // LANE)),
            pl.BlockSpec((S, LANE), functools.partial(lambda j, o: (0, o + 2 * j + 1), o=off // LANE))]


def conv_fwd(name, src, off, w_pad, taps, mode, S, C):
    nchunk = S // TC

    def body(x1_ref, x2_ref, w_ref, o_ref, a_pad):
        a_pad[0:CPAD, :] = jnp.zeros((CPAD, LANE), F32)

        def fill(i, _):
            r = pl.multiple_of(i * 256, 256)
            a_pad[pl.ds(CPAD + r, 256), :] = _pre(mode, x1_ref[pl.ds(r, 256), :].astype(F32),
                                                  x2_ref[pl.ds(r, 256), :].astype(F32))
            return 0

        lax.fori_loop(0, S // 256, fill, 0)

        def chunk(i, _):
            base = pl.multiple_of(i * TC, TC)
            ext = a_pad[pl.ds(base, TC + CPAD), :]
            acc = jnp.zeros((TC, LANE), F32)
            for k in range(taps):
                acc = acc + w_ref[pl.ds(k, 1), :] * _shifted(ext, CPAD - (taps - 1) + k)
            o_ref[pl.ds(base, TC), :] = acc
            return 0

        lax.fori_loop(0, nchunk, chunk, 0)

    kp = w_pad.shape[0]
    return pl.pallas_call(
        body, name=name, grid=(C // LANE,),
        in_specs=_interleaved_specs(S, off) + [pl.BlockSpec((kp, LANE), lambda j: (0, j))],
        out_specs=pl.BlockSpec((S, LANE), lambda j: (0, j)),
        out_shape=jax.ShapeDtypeStruct((S, C), F32),
        scratch_shapes=[pltpu.VMEM((S + CPAD, LANE), F32)],
        compiler_params=_cparams(),
    )(src, src, w_pad)


def conv_bwd(name, src, off, dc, w_pad, taps, mode, S, C, buf):
    nchunk = S // TC
    kp = w_pad.shape[0]

    def body(x1_ref, x2_ref, dc_ref, w_ref, _, d_ref, dw_ref, a_pad, dc_pad, dw_acc):
        a_pad[0:CPAD, :] = jnp.zeros((CPAD, LANE), F32)
        dc_pad[S:S + CPAD, :] = jnp.zeros((CPAD, LANE), F32)
        dw_acc[...] = jnp.zeros(dw_acc.shape, F32)

        def fill(i, _):
            r = pl.multiple_of(i * 256, 256)
            a_pad[pl.ds(CPAD + r, 256), :] = _pre(mode, x1_ref[pl.ds(r, 256), :].astype(F32),
                                                  x2_ref[pl.ds(r, 256), :].astype(F32))
            dc_pad[pl.ds(r, 256), :] = dc_ref[pl.ds(r, 256), :]
            return 0

        lax.fori_loop(0, S // 256, fill, 0)

        def chunk(i, _):
            base = pl.multiple_of(i * TC, TC)
            ext_d = dc_pad[pl.ds(base, TC + CPAD), :]
            ext_a = a_pad[pl.ds(base, TC + CPAD), :]
            dcv = ext_d[0:TC]
            da = jnp.zeros((TC, LANE), F32)
            for k in range(taps):
                da = da + w_ref[pl.ds(k, 1), :] * _shifted(ext_d, taps - 1 - k)
                prod = dcv * _shifted(ext_a, CPAD - (taps - 1) + k)
                fold = prod[0:8]
                for g in range(1, TC // 8):
                    fold = fold + prod[8 * g:8 * g + 8]
                dw_acc[pl.ds(8 * k, 8), :] += fold
            x1 = x1_ref[pl.ds(base, TC), :].astype(F32)
            x2 = x2_ref[pl.ds(base, TC), :].astype(F32)
            if mode == "glu":
                s = _sigmoid(x2)
                d1, d2 = da * s, da * x1 * s * (1.0 - s)
            else:
                d1, d2 = da * x2, da * x1
            d_ref[pl.ds(base, TC), 0:LANE] = d1.astype(BF)
            d_ref[pl.ds(base, TC), LANE:2 * LANE] = d2.astype(BF)
            return 0

        lax.fori_loop(0, nchunk, chunk, 0)
        dw_ref[...] = jnp.zeros(dw_ref.shape, F32)
        for k in range(taps):
            dw_ref[pl.ds(k, 1), :] = jnp.sum(dw_acc[pl.ds(8 * k, 8), :], axis=0, keepdims=True)

    blk = pl.BlockSpec((S, LANE), lambda j: (0, j))
    return pl.pallas_call(
        body, name=name, grid=(C // LANE,),
        in_specs=_interleaved_specs(S, off) + [blk, pl.BlockSpec((kp, LANE), lambda j: (0, j)), ANY_SPEC],
        out_specs=[pl.BlockSpec((S, 2 * LANE), functools.partial(lambda j, o: (0, o + j), o=off // (2 * LANE))),
                   pl.BlockSpec((kp, LANE), lambda j: (0, j))],
        out_shape=[jax.ShapeDtypeStruct(buf.shape, BF), jax.ShapeDtypeStruct((kp, C), F32)],
        input_output_aliases={4: 0},
        scratch_shapes=[pltpu.VMEM((S + CPAD, LANE), F32), pltpu.VMEM((S + CPAD, LANE), F32),
                        pltpu.VMEM((8 * kp, LANE), F32)],
        compiler_params=_cparams(),
    )(src, src, dc, w_pad, buf)


FWD_TILES = (512, 512)
BWD_TILES = (512, 512)
QUADS = HEADS // 4
QW, KVW = 4 * (NOPE + ROPE), 4 * (NOPE + VH)
SCALE = (NOPE + ROPE) ** -0.5
NT_DIMS = (((1,), (1,)), ((), ()))
TN_DIMS = (((0,), (0,)), ((), ()))


def _lane_mask(width, group, dtype):
    lane = lax.broadcasted_iota(jnp.int32, (1, LANE), 1)
    return jnp.where(lane // width == group, 1.0, 0.0).astype(dtype)


def _visible(tq, tk, off):
    row = lax.broadcasted_iota(jnp.int32, (tq, tk), 0)
    col = lax.broadcasted_iota(jnp.int32, (tq, tk), 1)
    return col <= row + off


def _attn_tiles(S, tq, tk):
    tk = tk if S % tk == 0 else 256
    return min(tq, tk), tk


def attn_fwd(q, kv, kpe, S):
    tq, tk = _attn_tiles(S, *FWD_TILES)
    nq = S // tq

    def body(q_ref, kv_ref, kp_ref, o_ref, lse_ref):
        for t in range(2):
            cols = slice(t * LANE, (t + 1) * LANE)
            for hh in range(2):
                def q_block(qi, _, t=t, hh=hh, cols=cols):
                    r0 = pl.multiple_of(qi * tq, tq)
                    qcat = jnp.concatenate([q_ref[pl.ds(r0, tq), cols] * _lane_mask(NOPE, hh, BF),
                                            q_ref[pl.ds(r0, tq), 2 * LANE:3 * LANE] * _lane_mask(ROPE, 2 * t + hh, BF)],
                                           axis=1)
                    nfull = (qi * tq) // tk

                    def step(kj, carry, masked):
                        m, l, acc = carry
                        c0 = pl.multiple_of(kj * tk, tk)
                        kc = jnp.concatenate([kv_ref[pl.ds(c0, tk), cols], kp_ref[pl.ds(c0, tk), :]], axis=1)
                        vt = kv_ref[pl.ds(c0, tk), (2 + t) * LANE:(3 + t) * LANE]
                        s = lax.dot_general(qcat, kc, NT_DIMS, preferred_element_type=F32) * SCALE
                        if masked:
                            s = jnp.where(_visible(tq, tk, qi * tq - nfull * tk), s, -jnp.inf)
                        m_new = jnp.maximum(m, jnp.max(s, axis=-1, keepdims=True))
                        p = jnp.exp(s - m_new)
                        alpha = jnp.exp(m - m_new)
                        l = alpha * l + jnp.sum(p, axis=-1, keepdims=True)
                        acc = alpha * acc + jnp.dot(p.astype(BF), vt, preferred_element_type=F32)
                        return m_new, l, acc

                    init = (jnp.full((tq, 1), -jnp.inf, F32), jnp.zeros((tq, 1), F32), jnp.zeros((tq, LANE), F32))
                    carry = lax.fori_loop(0, nfull, lambda kj, c: step(kj, c, False), init)
                    m, l, acc = step(nfull, carry, True)
                    mine = _lane_mask(NOPE, hh, F32)
                    if hh == 0:
                        o_ref[pl.ds(r0, tq), cols] = (acc / l) * mine
                        lse_ref[pl.ds(r0, tq), cols] = (m + jnp.log(l)) * mine
                    else:
                        o_ref[pl.ds(r0, tq), cols] += (acc / l) * mine
                        lse_ref[pl.ds(r0, tq), cols] += (m + jnp.log(l)) * mine
                    return 0

                lax.fori_loop(0, nq, q_block, 0)

    return pl.pallas_call(
        body, name="attn_fwd", grid=(QUADS,),
        in_specs=[pl.BlockSpec((S, QW), lambda g: (0, g)), pl.BlockSpec((S, KVW), lambda g: (0, g)),
                  pl.BlockSpec((S, LANE), lambda g: (0, 0))],
        out_specs=[pl.BlockSpec((S, 2 * LANE), lambda g: (0, g))] * 2,
        out_shape=[jax.ShapeDtypeStruct((S, HEADS * VH), F32)] * 2,
        compiler_params=_cparams(),
    )(q, kv, kpe)


def attn_bwd(q, kv, kpe, o, lse, do, S):
    tq, tk = _attn_tiles(S, *BWD_TILES)
    nq = S // tq

    def body(q_ref, kv_ref, kp_ref, o_ref, lse_ref, do_ref, dq_ref, dkv_ref, dkp_ref, dq_acc, dk_acc, dv_acc):
        for t in range(2):
            cols = slice(t * LANE, (t + 1) * LANE)
            dk_acc[...] = jnp.zeros(dk_acc.shape, F32)
            dv_acc[...] = jnp.zeros(dv_acc.shape, F32)
            for hh in range(2):
                def q_block(qi, _, t=t, hh=hh, cols=cols):
                    r0 = pl.multiple_of(qi * tq, tq)
                    mine = _lane_mask(NOPE, hh, F32)
                    qcat = jnp.concatenate([q_ref[pl.ds(r0, tq), cols] * _lane_mask(NOPE, hh, BF),
                                            q_ref[pl.ds(r0, tq), 2 * LANE:3 * LANE] * _lane_mask(ROPE, 2 * t + hh, BF)],
                                           axis=1)
                    dof = do_ref[pl.ds(r0, tq), cols] * mine
                    dob = dof.astype(BF)
                    delta = jnp.sum(dof * o_ref[pl.ds(r0, tq), cols], axis=-1, keepdims=True)
                    lse_h = lse_ref[pl.ds(r0, tq), cols][:, hh * NOPE:hh * NOPE + 1]
                    nfull = (qi * tq) // tk
                    dq_acc[...] = jnp.zeros(dq_acc.shape, F32)

                    def step(kj, _, masked):
                        c0 = pl.multiple_of(kj * tk, tk)
                        kc = jnp.concatenate([kv_ref[pl.ds(c0, tk), cols], kp_ref[pl.ds(c0, tk), :]], axis=1)
                        vt = kv_ref[pl.ds(c0, tk), (2 + t) * LANE:(3 + t) * LANE]
                        s = lax.dot_general(qcat, kc, NT_DIMS, preferred_element_type=F32) * SCALE
                        if masked:
                            s = jnp.where(_visible(tq, tk, qi * tq - nfull * tk), s, -jnp.inf)
                        p = jnp.exp(s - lse_h)
                        dp = lax.dot_general(dob, vt, NT_DIMS, preferred_element_type=F32)
                        ds = (p * (dp - delta) * SCALE).astype(BF)
                        dv_acc[pl.ds(c0, tk), :] += lax.dot_general(p.astype(BF), dob, TN_DIMS,
                                                                    preferred_element_type=F32)
                        dk_acc[pl.ds(c0, tk), :] += lax.dot_general(ds, qcat, TN_DIMS, preferred_element_type=F32)
                        dq_acc[...] += jnp.dot(ds, kc, preferred_element_type=F32)
                        return 0

                    lax.fori_loop(0, nfull, lambda kj, c: step(kj, c, False), 0)
                    step(nfull, 0, True)
                    d = dq_acc[...]
                    pe = d[:, LANE:] * _lane_mask(ROPE, 2 * t + hh, F32)
                    if hh == 0:
                        dq_ref[pl.ds(r0, tq), cols] = d[:, :LANE] * mine
                    else:
                        dq_ref[pl.ds(r0, tq), cols] += d[:, :LANE] * mine
                    if t == 0 and hh == 0:
                        dq_ref[pl.ds(r0, tq), 2 * LANE:3 * LANE] = pe
                    else:
                        dq_ref[pl.ds(r0, tq), 2 * LANE:3 * LANE] += pe
                    return 0

                lax.fori_loop(0, nq, q_block, 0)
            dkv_ref[:, t * LANE:(t + 1) * LANE] = dk_acc[:, :LANE].astype(BF)
            dkv_ref[:, (2 + t) * LANE:(3 + t) * LANE] = dv_acc[...].astype(BF)
            if t == 0:
                dkp_ref[...] = dk_acc[:, LANE:]
            else:
                dkp_ref[...] += dk_acc[:, LANE:]

    qspec = pl.BlockSpec((S, QW), lambda g: (0, g))
    kvspec = pl.BlockSpec((S, KVW), lambda g: (0, g))
    ospec = pl.BlockSpec((S, 2 * LANE), lambda g: (0, g))
    return pl.pallas_call(
        body, name="attn_bwd", grid=(QUADS,),
        in_specs=[qspec, kvspec, pl.BlockSpec((S, LANE), lambda g: (0, 0)), ospec, ospec, ospec],
        out_specs=[qspec, kvspec, pl.BlockSpec((S, LANE), lambda g: (0, g))],
        out_shape=[jax.ShapeDtypeStruct((S, HEADS * (NOPE + ROPE)), F32), jax.ShapeDtypeStruct((S, HEADS * (NOPE + VH)), BF),
                   jax.ShapeDtypeStruct((S, HEADS * ROPE), F32)],
        scratch_shapes=[pltpu.VMEM((tq, 2 * LANE), F32), pltpu.VMEM((S, 2 * LANE), F32), pltpu.VMEM((S, LANE), F32)],
        compiler_params=_cparams(),
    )(q, kv, kpe, o, lse, do)


def exchange(name, gathers, a2as):
    n_g, n = len(gathers), len(gathers) + len(a2as)

    def body(*refs):
        ins, outs = refs[:n], refs[n:2 * n]
        send_sems, recv_sems, loc_sems = refs[2 * n:]
        x, y, c = lax.axis_index("x"), lax.axis_index("y"), lax.axis_index("c")
        me = 4 * x + 2 * y + c

        def peer(k):
            px = 1 - x if k & 4 else x
            py = 1 - y if k & 2 else y
            pc = 1 - c if k & 1 else c
            return (px, py, pc), 4 * px + 2 * py + pc

        def remote(a, k):
            pid, pflat = peer(k)
            src = ins[a] if a < n_g else ins[a].at[pflat]
            return pltpu.make_async_remote_copy(
                src_ref=src, dst_ref=outs[a].at[me], send_sem=send_sems.at[a, k - 1], recv_sem=recv_sems.at[a, k - 1],
                device_id=pid, device_id_type=MESH)

        def arrival(a, k):
            pid, pflat = peer(k)
            src = ins[a] if a < n_g else ins[a].at[pflat]
            return pltpu.make_async_remote_copy(
                src_ref=src, dst_ref=outs[a].at[pflat], send_sem=send_sems.at[a, k - 1], recv_sem=recv_sems.at[a, k - 1],
                device_id=pid, device_id_type=MESH)

        local = []
        for a in range(n):
            own = ins[a] if a < n_g else ins[a].at[me]
            cp = pltpu.make_async_copy(own, outs[a].at[me], loc_sems.at[a])
            cp.start()
            local.append(cp)
        sent = []
        for k in (1, 2, 4, 3, 5, 6, 7):
            for a in range(n):
                cp = remote(a, k)
                cp.start()
                sent.append(cp)
        for k in range(1, 8):
            for a in range(n):
                arrival(a, k).wait_recv()
        for cp in sent:
            cp.wait_send()
        for cp in local:
            cp.wait()

    out_shape = [jax.ShapeDtypeStruct((NDEV,) + g.shape, g.dtype) for g in gathers]
    out_shape += [jax.ShapeDtypeStruct(a.shape, a.dtype) for a in a2as]
    any_spec = pl.BlockSpec(memory_space=pl.ANY)
    return pl.pallas_call(
        body, name=name, in_specs=[any_spec] * n, out_specs=[any_spec] * n, out_shape=out_shape,
        scratch_shapes=[pltpu.SemaphoreType.DMA((n, NDEV - 1)), pltpu.SemaphoreType.DMA((n, NDEV - 1)),
                        pltpu.SemaphoreType.DMA((n,))],
    )(*gathers, *a2as)


def gather_two_level(name, block, dep):
    def body(x_ref, _, out_ref, stage, send_sems, recv_sems, loc_sem):
        x, y, c = lax.axis_index("x"), lax.axis_index("y"), lax.axis_index("c")
        me, sibling = (x, y, c), (x, y, 1 - c)
        chips = [(1 - x, y), (x, 1 - y), (1 - x, 1 - y)]

        def slot(px, py, pc):
            return out_ref.at[4 * px + 2 * py + pc]

        def copy(k, owner, to, src=None):
            return pltpu.make_async_remote_copy(
                src_ref=slot(*owner) if src is None else src, dst_ref=slot(*owner), send_sem=send_sems.at[k],
                recv_sem=recv_sems.at[k], device_id=to, device_id_type=MESH)

        load = pltpu.make_async_copy(x_ref, stage, loc_sem)
        load.start()
        first = [copy(0, me, sibling, src=x_ref)] + [copy(1 + j, me, (*chip, c), src=x_ref) for j, chip in enumerate(chips)]
        for cp in first:
            cp.start()
        load.wait()
        store = pltpu.make_async_copy(stage, slot(*me), loc_sem)
        store.start()
        passed = [copy(4 + j, (*chip, c), sibling) for j, chip in enumerate(chips)]
        for j, chip in enumerate(chips):
            copy(1 + j, (*chip, c), me).wait_recv()
            passed[j].start()
        copy(0, sibling, me).wait_recv()
        for j, chip in enumerate(chips):
            copy(4 + j, (*chip, 1 - c), me).wait_recv()
        for cp in first + passed:
            cp.wait_send()
        store.wait()

    return pl.pallas_call(
        body, name=name, in_specs=[pl.BlockSpec(memory_space=pl.ANY)] * 2, out_specs=pl.BlockSpec(memory_space=pl.ANY),
        out_shape=jax.ShapeDtypeStruct((NDEV,) + block.shape, block.dtype),
        scratch_shapes=[pltpu.VMEM(block.shape, block.dtype), pltpu.SemaphoreType.DMA((NDEV - 1,)),
                        pltpu.SemaphoreType.DMA((NDEV - 1,)), pltpu.SemaphoreType.DMA],
        compiler_params=_cparams(),
    )(block, dep)


def _peer(k, x, y, c):
    px = 1 - x if k & 4 else x
    py = 1 - y if k & 2 else y
    pc = 1 - c if k & 1 else c
    return (px, py, pc), 4 * px + 2 * py + pc


PEER_ORDER = (1, 2, 4, 3, 5, 6, 7)
HBM_SPEC = pl.BlockSpec(memory_space=pltpu.HBM)
SEM_SPEC = pl.BlockSpec(memory_space=pltpu.SEMAPHORE)
ANY_SPEC = pl.BlockSpec(memory_space=pl.ANY)


def _split_copies(ins, lands, n_g, send_sems, recv_sems):
    x, y, c = lax.axis_index("x"), lax.axis_index("y"), lax.axis_index("c")
    me = 4 * x + 2 * y + c

    def outgoing(a, k):
        pid, pflat = _peer(k, x, y, c)
        src = ins[a] if a < n_g else ins[a].at[pflat]
        return pltpu.make_async_remote_copy(
            src_ref=src, dst_ref=lands[a].at[me], send_sem=send_sems.at[a * (NDEV - 1) + k - 1],
            recv_sem=recv_sems.at[a * (NDEV - 1) + k - 1],
            device_id=pid, device_id_type=MESH)

    def arrival(a, k):
        pid, pflat = _peer(k, x, y, c)
        src = ins[a] if a < n_g else ins[a].at[pflat]
        return pltpu.make_async_remote_copy(
            src_ref=src, dst_ref=lands[a].at[pflat], send_sem=send_sems.at[a * (NDEV - 1) + k - 1],
            recv_sem=recv_sems.at[a * (NDEV - 1) + k - 1],
            device_id=pid, device_id_type=MESH)

    return outgoing, arrival


def exchange_begin(name, srcs, n_g, dep):
    n = len(srcs)
    land_shapes = [((NDEV,) + s.shape) if a < n_g else s.shape for a, s in enumerate(srcs)]

    def own_body(*refs):
        ins, outs = refs[:n], refs[n + 1:2 * n + 1]
        stage, sems = refs[2 * n + 1:3 * n + 1], refs[-1]
        me = 4 * lax.axis_index("x") + 2 * lax.axis_index("y") + lax.axis_index("c")
        cps = [pltpu.make_async_copy(ins[a] if a < n_g else ins[a].at[me], stage[a], sems.at[a]) for a in range(n)]
        for cp in cps:
            cp.start()
        for cp in cps:
            cp.wait()
        cps = [pltpu.make_async_copy(stage[a], outs[a].at[me], sems.at[a]) for a in range(n)]
        for cp in cps:
            cp.start()
        for cp in cps:
            cp.wait()

    lands = pl.pallas_call(
        own_body, name=name + "_own", in_specs=[ANY_SPEC] * (n + 1), out_specs=[ANY_SPEC] * n,
        out_shape=[jax.ShapeDtypeStruct(sh, s.dtype) for sh, s in zip(land_shapes, srcs)],
        scratch_shapes=[pltpu.VMEM(sh[1:], s.dtype) for sh, s in zip(land_shapes, srcs)] + [pltpu.SemaphoreType.DMA((n,))],
        compiler_params=_cparams(),
    )(*srcs, dep)

    def start_body(*refs):
        ins, lz = refs[:n], refs[n:2 * n]
        send_sems, recv_sems, token = refs[2 * n], refs[2 * n + 1], refs[-1]
        outgoing, _ = _split_copies(ins, lz, n_g, send_sems, recv_sems)
        for k in PEER_ORDER:
            for a in range(n):
                outgoing(a, k).start()
        token[...] = jnp.zeros(token.shape, F32)

    hbm = lambda t: pltpu.HBM(t.shape, t.dtype)
    res = pl.pallas_call(
        start_body, name=name + "_start",
        out_shape=(pltpu.SemaphoreType.DMA((n * (NDEV - 1),)), pltpu.SemaphoreType.DMA((n * (NDEV - 1),)),
                   *[hbm(s) for s in srcs], *[hbm(t) for t in lands], jax.ShapeDtypeStruct((8, LANE), F32)),
        in_specs=[HBM_SPEC] * (2 * n),
        out_specs=(SEM_SPEC, SEM_SPEC, *[HBM_SPEC] * (2 * n), pl.BlockSpec(memory_space=pltpu.VMEM)),
        input_output_aliases={i: 2 + i for i in range(2 * n)},
        compiler_params=pltpu.CompilerParams(has_side_effects=pltpu.SideEffectType.DATAFLOW_SIDE_EFFECTING),
    )(*[pltpu.with_memory_space_constraint(t, pltpu.HBM) for t in list(srcs) + list(lands)])
    return (name, n, n_g, res[:-1]), res[-1]


def exchange_end(handle, after):
    name, n, n_g, (send_sems, recv_sems, *bufs) = handle

    def wait_body(*refs):
        ins, lz = refs[:n], refs[n:2 * n]
        ss, rs = refs[2 * n], refs[2 * n + 1]
        outgoing, arrival = _split_copies(ins, lz, n_g, ss, rs)
        for k in range(1, NDEV):
            for a in range(n):
                arrival(a, k).wait_recv()
        for k in range(1, NDEV):
            for a in range(n):
                outgoing(a, k).wait_send()

    res = pl.pallas_call(
        wait_body, name=name + "_wait", out_shape=tuple(pltpu.HBM(t.shape, t.dtype) for t in bufs),
        in_specs=[HBM_SPEC] * (2 * n) + [SEM_SPEC, SEM_SPEC, ANY_SPEC], out_specs=[HBM_SPEC] * (2 * n),
        input_output_aliases={i: i for i in range(2 * n)},
        compiler_params=pltpu.CompilerParams(has_side_effects=pltpu.SideEffectType.DATAFLOW_SIDE_EFFECTING),
    )(*bufs, send_sems, recv_sems, after)
    return list(res[n:])


def _pick_rows(R, mult, cap):
    best = None
    for n in range(1, R + 1):
        if R % n == 0 and (R // n) % mult == 0 and R // n <= cap:
            best = R // n
            break
    assert best is not None, (R, mult, cap)
    return best


def sum_slots(name, x):
    _, R, _ = x.shape
    tr = _pick_rows(R, 16, 2304)

    def body(x_ref, o_ref):
        acc = x_ref[0].astype(F32)
        for d in range(1, NDEV):
            acc = acc + x_ref[d].astype(F32)
        o_ref[...] = acc

    return pl.pallas_call(
        body, name=name, grid=(R // tr,),
        in_specs=[pl.BlockSpec((NDEV, tr, LANE), lambda i: (0, i, 0))],
        out_specs=pl.BlockSpec((tr, LANE), lambda i: (i, 0)),
        out_shape=jax.ShapeDtypeStruct((R, LANE), F32), compiler_params=_cparams(),
    )(x)


def adamw(name, w, g, m, v):
    L, R, C = w.shape
    tr = _pick_rows(R, 8, 256) if R % 8 == 0 else R

    def body(w_ref, g_ref, m_ref, v_ref, d_ref, nm_ref, nv_ref):
        gg = g_ref[...]
        nm = B1 * m_ref[...] + (1.0 - B1) * gg
        nv = B2 * v_ref[...] + (1.0 - B2) * jnp.square(gg)
        m_hat = nm / (1.0 - B1 ** STEP)
        v_hat = nv / (1.0 - B2 ** STEP)
        d_ref[...] = -LR * (m_hat / (jnp.sqrt(v_hat) + EPS) + WD * w_ref[...])
        nm_ref[...] = nm
        nv_ref[...] = nv

    blk = pl.BlockSpec((1, tr, C), lambda l, i: (l, i, 0))
    shp = jax.ShapeDtypeStruct(w.shape, F32)
    return pl.pallas_call(
        body, name=name, grid=(L, R // tr), in_specs=[blk] * 4, out_specs=[blk] * 3, out_shape=[shp] * 3,
        compiler_params=_cparams(),
    )(w, g, m, v)


IN_SHARD = D_IN // NDEV
UQ_SHARD = HEADS * (NOPE + ROPE) // NDEV
W_IN_PAD = 1024
ROW_A, ROW_B, ROW_C, ROW_UKV, ROW_UQ, MISC_ROWS = 0, 512, 1024, 1536, 1792, 2176


def _in_perm_index():
    ar = np.arange
    z = lambda n: np.full((n,), -1, np.int64)
    mix = lambda lo1, lo2: np.concatenate([ar(lo + LANE * j, lo + LANE * (j + 1)) for j in range(CW // LANE)
                                           for lo in (lo1, lo2)])
    return np.concatenate([ar(4768, 7840), mix(0, 512), ar(1024, 1536), mix(1536, 2560), ar(4256, 4768), ar(2048, 2560),
                           ar(3072, 3584), ar(3968, 4224), ar(4224, 4256), z(OFF_Q - OFF_KR - ROPE), ar(3584, 3968),
                           z(NP - OFF_Q - QL)])


def _head_perm_index(a, b):
    parts = []
    for g in range(QUADS):
        h = np.arange(4 * g, 4 * g + 4)[:, None] * (a + b)
        parts += [(h + np.arange(a)[None]).reshape(-1), (h + a + np.arange(b)[None]).reshape(-1)]
    return np.concatenate(parts)


def _inverse(perm, n):
    inv = np.full((n,), -1, np.int64)
    inv[perm[perm >= 0]] = np.nonzero(perm >= 0)[0]
    return inv


IN_PERM = _in_perm_index()
UQ_PERM = _head_perm_index(NOPE, ROPE)
UKV_PERM = _head_perm_index(NOPE, VH)


def _to_gathered(perm, shard, pad):
    return np.where(perm >= 0, (perm // shard) * pad + perm % shard, -1)


def _from_full(inv, shard, pad):
    j, i = np.divmod(np.arange(NDEV * pad), pad)
    return np.where(i < shard, inv[np.minimum(j * shard + i, inv.shape[0] - 1)], -1)


def col_gather(name, srcs, out_shapes, jobs, deps=()):
    ns, nj, nd, no = len(srcs), len(jobs), len(deps), len(out_shapes)
    tables = [jnp.asarray(np.asarray(job[5], np.int32)[None, :]) for job in jobs]

    def view(ref, col0, width, r0, rc):
        n = ref.shape[-1]
        if len(ref.shape) == 3:
            return ref.at[col0 // n, pl.ds(r0, rc), pl.ds(col0 % n, width)]
        return ref.at[pl.ds(r0, rc), pl.ds(col0, width)]

    def slabs(shape):
        if len(shape) == 3:
            return [((d,), d * shape[2], (d + 1) * shape[2]) for d in range(shape[0])]
        w = 1024 if shape[1] > 1024 and shape[1] % 1024 == 0 else shape[1]
        return [((slice(None), pl.ds(c, w)), c, c + w) for c in range(0, shape[1], w)]

    src_slabs = [slabs(s.shape) for s in srcs]
    out_slabs = [slabs(sh) for sh in out_shapes]
    work, first_use, last_touch = [], {}, {}
    for ji, (si, srow, oi, orow, nrows, tgt) in enumerate(jobs):
        tgt = np.asarray(tgt)
        tw = 256 if out_shapes[oi][-1] % 256 == 0 else LANE
        sw = 256 if srcs[si].shape[-1] % 256 == 0 else LANE
        for t in range(tgt.shape[0] // tw):
            tt = tgt[t * tw:(t + 1) * tw]
            tiles = sorted(set((tt[tt >= 0] // sw).tolist()))
            straight = bool(tiles) and tt[0] >= 0 and tt[0] % LANE == 0 and np.array_equal(tt, tt[0] + np.arange(tw))
            cols = [(int(tt[0]) + k * LANE, LANE) for k in range(tw // LANE)] if straight else [(s * sw, sw) for s in tiles]
            need = sorted({(si, k) for c0, _ in cols for k, (_, lo, hi) in enumerate(src_slabs[si]) if lo <= c0 < hi})
            touch = [(oi, k) for k, (_, lo, hi) in enumerate(out_slabs[oi]) if lo <= t * tw < hi][0]
            for key in need:
                first_use.setdefault(key, len(work))
            last_touch[touch] = len(work)
            work.append((ji, t, tw, sw, tiles, straight, need, touch))
    in_order = sorted(first_use, key=first_use.get)
    in_sem = {key: i for i, key in enumerate(in_order)}
    out_keys = sorted(last_touch)
    out_sem = {key: i for i, key in enumerate(out_keys)}

    def body(*refs):
        src_hbm, tab_refs = refs[:ns], refs[ns:ns + nj]
        out_hbm = refs[ns + nj + nd:ns + nj + nd + no]
        scratch = refs[ns + nj + nd + no:]
        src_refs, out_refs, in_sems, out_sems = scratch[:ns], scratch[ns:ns + no], scratch[-2], scratch[-1]
        loads = {}
        for key in in_order:
            si, k = key
            idx = src_slabs[si][k][0]
            loads[key] = pltpu.make_async_copy(src_hbm[si].at[idx], src_refs[si].at[idx], in_sems.at[in_sem[key]])
            loads[key].start()
        arrived, stores = set(), []
        for wi, (ji, t, tw, sw, tiles, straight, need, touch) in enumerate(work):
            si, srow, oi, orow, nrows, tgt = jobs[ji]
            sref, oref = src_refs[si], out_refs[oi]
            rc = nrows if nrows <= 1024 else 1024
            for key in need:
                if key not in arrived:
                    loads[key].wait()
                    arrived.add(key)
            onehots = []
            if tiles and not straight:
                want = tab_refs[ji][:, t * tw:(t + 1) * tw]
                row = lax.broadcasted_iota(jnp.int32, (sw, tw), 0)
                onehots = [jnp.where(want == row + s * sw, 1.0, 0.0).astype(BF) for s in tiles]
            first = int(np.asarray(tgt)[t * tw])

            def chunk(ci, _, t=t, tw=tw, sw=sw, tiles=tiles, straight=straight, onehots=onehots, first=first,
                      sref=sref, oref=oref, srow=srow, orow=orow, rc=rc):
                r0 = ci * rc
                ro = pl.multiple_of(orow + r0, LANE)
                rs = pl.multiple_of(srow + r0, LANE)
                if not tiles:
                    view(oref, t * tw, tw, ro, rc)[...] = jnp.zeros((rc, tw), BF)
                elif straight:
                    for k in range(tw // LANE):
                        view(oref, t * tw + k * LANE, LANE, ro, rc)[...] = view(sref, first + k * LANE, LANE, rs, rc)[...]
                else:
                    acc = None
                    for s, oh in zip(tiles, onehots):
                        p = jnp.dot(view(sref, s * sw, sw, rs, rc)[...], oh, preferred_element_type=F32)
                        acc = p if acc is None else acc + p
                    view(oref, t * tw, tw, ro, rc)[...] = acc.astype(BF)
                return 0

            lax.fori_loop(0, nrows // rc, chunk, 0)
            if last_touch[touch] == wi:
                idx = out_slabs[touch[0]][touch[1]][0]
                cp = pltpu.make_async_copy(out_refs[touch[0]].at[idx], out_hbm[touch[0]].at[idx], out_sems.at[out_sem[touch]])
                cp.start()
                stores.append(cp)
        for cp in stores:
            cp.wait()

    return pl.pallas_call(
        body, name=name, in_specs=[ANY_SPEC] * ns + [pl.BlockSpec(memory_space=pltpu.VMEM)] * nj + [ANY_SPEC] * nd,
        out_specs=[ANY_SPEC] * no, out_shape=[jax.ShapeDtypeStruct(s, BF) for s in out_shapes],
        scratch_shapes=[pltpu.VMEM(s.shape, BF) for s in srcs] + [pltpu.VMEM(s, BF) for s in out_shapes]
        + [pltpu.SemaphoreType.DMA((len(in_order),)), pltpu.SemaphoreType.DMA((len(out_keys),))],
        compiler_params=_cparams(),
    )(*srcs, *tables, *deps)


def sum_adamw(name, recvs, w, m, v, lo=0, prev=None, row0=0):
    _, R, C = w.shape
    L = len(recvs)
    CP = recvs[0].shape[-1]
    tr = _pick_rows(R, 16, 128)
    n_prev = 0 if prev is None else 4

    def body(*refs):
        r_refs = refs[:L]
        w_ref, m_ref, v_ref = refs[L:L + 3]
        g_ref, d_ref, nm_ref, nv_ref, gsum = refs[L + 3 + n_prev:]
        layer = pl.program_id(0)
        for k in range(L):
            def total(k=k):
                acc = r_refs[k][0].astype(F32)
                for d in range(1, NDEV):
                    acc = acc + r_refs[k][d].astype(F32)
                gsum[...] = acc
            pl.when(layer == k)(total)
        gg = gsum[:, 0:C]
        nm = B1 * m_ref[...] + (1.0 - B1) * gg
        nv = B2 * v_ref[...] + (1.0 - B2) * jnp.square(gg)
        m_hat = nm / (1.0 - B1 ** STEP)
        v_hat = nv / (1.0 - B2 ** STEP)
        g_ref[...] = gg
        d_ref[...] = -LR * (m_hat / (jnp.sqrt(v_hat) + EPS) + WD * w_ref[...])
        nm_ref[...] = nm
        nv_ref[...] = nv

    assert row0 % tr == 0
    r_specs = [pl.BlockSpec((NDEV, tr, CP),
                            functools.partial(lambda l, i, k: (0, row0 // tr + jnp.where(l == k, i, 0), 0), k=k))
               for k in range(L)]
    blk = pl.BlockSpec((None, tr, C), lambda l, i: (l + lo, i, 0))
    shp = jax.ShapeDtypeStruct(w.shape, F32)
    return pl.pallas_call(
        body, name=name, grid=(L, R // tr), in_specs=r_specs + [blk] * 3 + [ANY_SPEC] * n_prev, out_specs=[blk] * 4,
        out_shape=[shp] * 4, input_output_aliases={L + 3 + i: i for i in range(n_prev)},
        scratch_shapes=[pltpu.VMEM((tr, CP), F32)], compiler_params=_cparams(),
    )(*recvs, w, m, v, *(prev or ()))


ALPHA = 8.0 ** 0.25
T_WIDE, T_NARROW = 512, 1024


def _rope_fn(sign):
    def fn(x, cos, sin):
        W = x.shape[-1]
        lane = lax.broadcasted_iota(jnp.int32, x.shape, 1)
        first_half = (lane % ROPE) < (ROPE // 2)
        rot = jnp.where(first_half, -pltpu.roll(x, W - ROPE // 2, 1), pltpu.roll(x, ROPE // 2, 1))
        return x * cos + sign * rot * sin
    return fn


def layer_fwd(x, ada3, W, tabs, S):
    cos, sin = tabs
    T = T_NARROW
    u = rowwise("modulate", lambda xv, a: xv * (1.0 + a[1:2, :]) + a[0:1, :], S, T,
                [(x, D_MODEL, 0)], [ada3], [(D_MODEL, BF)])[0]
    proj = mm(u, W["in"], name="mm_proj", tm=1024, tn=1024, out_dtype=BF)
    W = {**W, **W["late"](proj)}

    ca = conv_fwd("conv_a_fwd", proj, OFF_A, W["conv_a"], 31, "glu", S, CW)

    def a_post(c, ag, vec):
        n, _ = _ln_stats(c + vec[0:1, :])
        return _silu(n * vec[1:2, :] + vec[2:3, :]) * _silu(ag)

    h_a = rowwise("mix_a_post", a_post, S, T, [(ca, CW, 0), (proj, CW, OFF_AG)], [W["vec_a"]], [(CW, BF)])[0]
    y_a = mm(h_a, W["a_out"], name="mm_branch_out", out_dtype=BF)

    cb = conv_fwd("conv_b_fwd", proj, OFF_B, W["conv_b"], 3, "mul", S, CW)
    h_b = rowwise("mix_b_post", lambda c, gb, bg: gb * c * _silu(bg), S, T,
                  [(cb, CW, 0), (proj, CW, OFF_GB), (proj, CW, OFF_BG)], [], [(CW, BF)])[0]
    y_b = mm(h_b, W["b_out"], name="mm_branch_out", out_dtype=BF)

    def rms2(ql, kvl, gq, gkv):
        rq = lax.rsqrt(jnp.mean(ql * ql, axis=-1, keepdims=True) + RMS_EPS)
        rk = lax.rsqrt(jnp.mean(kvl * kvl, axis=-1, keepdims=True) + RMS_EPS)
        return ql * rq * gq, kvl * rk * gkv

    qn, kvn = rowwise("rms_fwd", rms2, S, T, [(proj, QL, OFF_Q), (proj, KVL, OFF_KV)], [W["gq"], W["gkv"]],
                      [(QL, BF), (KVL, BF)])
    q = mm(qn, W["uq"], name="mm_q")
    kv = mm(kvn, W["ukv"], name="mm_kv", out_dtype=BF)
    rope = _rope_fn(1.0)

    def rope_fwd(qv, kr, c1, s1):
        parts = []
        for g in range(QUADS):
            parts.append(qv[:, g * QW:g * QW + 2 * LANE].astype(BF))
            parts.append(rope(qv[:, g * QW + 2 * LANE:(g + 1) * QW], c1, s1).astype(BF))
        kp = rope(kr, c1, s1)
        kp = kp + pltpu.roll(kp, ROPE, 1) + pltpu.roll(kp, 2 * ROPE, 1) + pltpu.roll(kp, 3 * ROPE, 1)
        return jnp.concatenate(parts, axis=1), kp

    q_b, kpe = rowwise("rope_fwd", rope_fwd, S, T,
                       [(q, HEADS * (NOPE + ROPE), 0), (proj, LANE, OFF_KR), (cos, LANE, 0), (sin, LANE, 0)], [],
                       [(HEADS * (NOPE + ROPE), BF), (LANE, BF)])
    o, lse = attn_fwd(q_b, kv, kpe, S)
    h_c = rowwise("mix_c_post", lambda ov, cg: ov * _silu(cg), S, T, [(o, CW, 0), (proj, CW, OFF_CG)], [],
                  [(CW, BF)])[0]
    y_c = mm(h_c, W["c_out"], name="mm_branch_out", out_dtype=BF)

    def merge(la, lb, lc, ya, yb, yc):
        return _sigmoid(la) * ya + _sigmoid(lb) * yb + _sigmoid(lc) * yc

    m = rowwise("merge_fwd", merge, S, T_WIDE,
                [(proj, D_MODEL, 0), (proj, D_MODEL, 1024), (proj, D_MODEL, 2048), (y_a, D_MODEL, 0),
                 (y_b, D_MODEL, 0), (y_c, D_MODEL, 0)], [], [(D_MODEL, BF)])[0]
    out = mm(m, W["o"], name="mm_out")

    def ln_fwd(xv, ov, a, lnv):
        n, _ = _ln_stats(ALPHA * xv + a[2:3, :] * ov)
        return n * lnv[0:1, :] + lnv[1:2, :]

    x_next = rowwise("ln_fwd", ln_fwd, S, T_WIDE, [(x, D_MODEL, 0), (out, D_MODEL, 0)], [ada3, W["lnv"]],
                     [(D_MODEL, F32)])[0]
    saved = dict(x=x, u=u, proj=proj, ca=ca, cb=cb, h_a=h_a, h_b=h_b, h_c=h_c, y_a=y_a, y_b=y_b, y_c=y_c, qn=qn,
                 kvn=kvn, q_b=q_b, kv=kv, kpe=kpe, lse=lse, o=o, m=m, out=out)
    return x_next, saved, W


def layer_bwd(dxn, sv, ada3, W, tabs, S, before_in=None):
    cos, sin = tabs
    T = T_NARROW
    x, proj = sv["x"], sv["proj"]
    G = {}

    def ln_bwd(xv, ov, dy, a, lnv):
        gate = a[2:3, :]
        n, rstd = _ln_stats(ALPHA * xv + gate * ov)
        dr = _ln_bwd(dy * lnv[0:1, :], n, rstd)
        return ALPHA * dr, gate * dr, _colsum(dy * n), _colsum(dy), _colsum(dr * ov)

    dres, d_out, G["ln_g"], G["ln_b"], d_gate = rowwise(
        "ln_bwd", ln_bwd, S, T_WIDE, [(x, D_MODEL, 0), (sv["out"], D_MODEL, 0), (dxn, D_MODEL, 0)], [ada3, W["lnv"]],
        [(D_MODEL, F32), (D_MODEL, BF)], [D_MODEL] * 3)
    dm = mm(d_out, W["o"], name="mm_dm", trans_b=True, out_dtype=BF)
    G["w_o"] = mm(sv["m"], d_out, name="mm_gw_o", trans_a=True, out_dtype=BF)

    def merge_bwd(dmv, la, lb, lc, ya, yb, yc):
        outs, dls = [], []
        for lg, yv in ((la, ya), (lb, yb), (lc, yc)):
            s = _sigmoid(lg)
            outs.append(dmv * s)
            dls.append((dmv * yv * s * (1.0 - s)).astype(BF))
        return (jnp.concatenate(dls, axis=1),) + tuple(outs)

    d_proj, dy_a, dy_b, dy_c = rowwise(
        "merge_bwd", merge_bwd, S, T_WIDE,
        [(dm, D_MODEL, 0), (proj, D_MODEL, 0), (proj, D_MODEL, 1024), (proj, D_MODEL, 2048), (sv["y_a"], D_MODEL, 0),
         (sv["y_b"], D_MODEL, 0), (sv["y_c"], D_MODEL, 0)], [], [(3 * D_MODEL, BF)] + [(D_MODEL, BF)] * 3,
        into=(None, NP, OFF_M))

    dh = {}
    for br, dy in (("a", dy_a), ("b", dy_b), ("c", dy_c)):
        dh[br] = mm(dy, W[br + "_out"], name="mm_dh", trans_b=True, out_dtype=BF)
        G["w_%s_out" % br] = mm(sv["h_" + br], dy, name="mm_gw_branch", trans_a=True, out_dtype=BF)

    def a_post_bwd(c, ag, dhv, vec):
        n, rstd = _ln_stats(c + vec[0:1, :])
        z = n * vec[1:2, :] + vec[2:3, :]
        d_ag = dhv * _silu(z) * _dsilu(ag)
        dz = dhv * _silu(ag) * _dsilu(z)
        dc = _ln_bwd(dz * vec[1:2, :], n, rstd)
        return d_ag, dc, _colsum(dc), _colsum(dz * n), _colsum(dz)

    d_proj, dca, G["conv_a_b"], G["ln_a_g"], G["ln_a_b"] = rowwise(
        "mix_a_post_bwd", a_post_bwd, S, T, [(sv["ca"], CW, 0), (proj, CW, OFF_AG), (dh["a"], CW, 0)], [W["vec_a"]],
        [(CW, BF), (CW, F32)], [CW] * 3, into=(d_proj, NP, OFF_AG))
    d_proj, G["conv_a_w"] = conv_bwd("conv_a_bwd", proj, OFF_A, dca, W["conv_a"], 31, "glu", S, CW, d_proj)

    def b_post_bwd(c, gb, bg, dhv):
        sg = _silu(bg)
        d_gb_bg = jnp.concatenate([(dhv * sg * c).astype(BF), (dhv * gb * c * _dsilu(bg)).astype(BF)], axis=1)
        return d_gb_bg, dhv * sg * gb

    d_proj, dcb = rowwise("mix_b_post_bwd", b_post_bwd, S, T,
                          [(sv["cb"], CW, 0), (proj, CW, OFF_GB), (proj, CW, OFF_BG), (dh["b"], CW, 0)], [],
                          [(2 * CW, BF), (CW, F32)], into=(d_proj, NP, OFF_GB))
    d_proj, G["conv_b_w"] = conv_bwd("conv_b_bwd", proj, OFF_B, dcb, W["conv_b"], 3, "mul", S, CW, d_proj)

    d_proj, d_o = rowwise("mix_c_post_bwd", lambda ov, cg, dhv: (dhv * ov * _dsilu(cg), dhv * _silu(cg)), S, T,
                          [(sv["o"], CW, 0), (proj, CW, OFF_CG), (dh["c"], CW, 0)], [], [(CW, BF), (CW, F32)],
                          into=(d_proj, NP, OFF_CG))
    dq, d_kv, dkp_heads = attn_bwd(sv["q_b"], sv["kv"], sv["kpe"], sv["o"], sv["lse"], d_o, S)
    ropeT = _rope_fn(-1.0)

    def rope_bwd(dqv, dkp, c1, s1):
        parts = []
        for g in range(QUADS):
            parts.append(dqv[:, g * QW:g * QW + 2 * LANE].astype(BF))
            parts.append(ropeT(dqv[:, g * QW + 2 * LANE:(g + 1) * QW], c1, s1).astype(BF))
        f = dkp[:, :LANE] + dkp[:, LANE:]
        f = f + pltpu.roll(f, 64, 1)
        f = f + pltpu.roll(f, 32, 1)
        lane = lax.broadcasted_iota(jnp.int32, f.shape, 1)
        return jnp.concatenate(parts, axis=1), jnp.where(lane < ROPE, ropeT(f, c1, s1), 0.0)

    d_q, dk_pe = rowwise("rope_bwd", rope_bwd, S, T,
                         [(dq, HEADS * (NOPE + ROPE), 0), (dkp_heads, HEADS * ROPE, 0), (cos, LANE, 0), (sin, LANE, 0)],
                         [], [(HEADS * (NOPE + ROPE), BF), (LANE, BF)])
    d_qn = mm(d_q, W["uq"], name="mm_dqn", trans_b=True, out_dtype=BF)
    d_kvn = mm(d_kv, W["ukv"], name="mm_dkvn", trans_b=True, out_dtype=BF)
    G["w_uq"] = mm(sv["qn"], d_q, name="mm_gw_uq", trans_a=True, out_dtype=BF)
    G["w_ukv"] = mm(sv["kvn"], d_kv, name="mm_gw_ukv", trans_a=True, out_dtype=BF)

    def rms_bwd(ql, kvl, dqn, dkn, dkp, gq, gkv):
        res = []
        for xv, dy, g in ((ql, dqn, gq), (kvl, dkn, gkv)):
            r = lax.rsqrt(jnp.mean(xv * xv, axis=-1, keepdims=True) + RMS_EPS)
            dxh = dy * g
            res.append(((r * (dxh - xv * (r * r) * jnp.mean(dxh * xv, axis=-1, keepdims=True))).astype(BF),
                        _colsum(dy * xv * r)))
        pad = jnp.zeros((ql.shape[0], LANE), BF)
        return jnp.concatenate([res[1][0], dkp, pad, res[0][0], pad], axis=1), res[0][1], res[1][1]

    d_proj, G["q_norm_g"], G["kv_norm_g"] = rowwise(
        "rms_bwd", rms_bwd, S, T,
        [(proj, QL, OFF_Q), (proj, KVL, OFF_KV), (d_qn, QL, 0), (d_kvn, KVL, 0), (dk_pe, LANE, 0)],
        [W["gq"], W["gkv"]], [(NP - OFF_KV, BF)], [QL, KVL], into=(d_proj, NP, OFF_KV))
    deps = before_in(G) if before_in is not None else ()
    du = mm(d_proj, W["in"], name="mm_du", trans_b=True, tm=1024, tk=2048, deps=deps)
    G["w_in"] = mm(sv["u"], d_proj, name="mm_gw_in", trans_a=True, out_dtype=BF, tm=1024, tk=2048, deps=deps)

    def mod_bwd(duv, xv, dr, a):
        return duv * (1.0 + a[1:2, :]) + dr, _colsum(duv), _colsum(duv * xv)

    dx, d_shift, d_scale = rowwise("mod_bwd", mod_bwd, S, T_WIDE, [(du, D_MODEL, 0), (x, D_MODEL, 0), (dres, D_MODEL, 0)],
                                   [ada3], [(D_MODEL, F32)], [D_MODEL] * 2)
    d_ada = jnp.concatenate([d_shift, d_scale, d_gate], axis=1)
    return dx, G, d_ada


SMALL = ("conv_a_b", "ln_a_g", "ln_a_b", "q_norm_g", "kv_norm_g", "ln_g", "ln_b")


def _rows(v):
    n = v.shape[0]
    r = -(-n // (LANE * 16)) * 16
    return jnp.pad(v, (0, r * LANE - n)).reshape(r, LANE)


def kernel(x, c, positions, w_ada, b_ada, w_in, conv_a_w, conv_a_b, ln_a_g, ln_a_b, w_a_out, conv_b_w, w_b_out, q_norm_g, kv_norm_g, w_uq, w_ukv, w_c_out, w_o, ln_g, ln_b, loss_target, m_w_ada, m_b_ada, m_w_in, m_conv_a_w, m_conv_a_b, m_ln_a_g, m_ln_a_b, m_w_a_out, m_conv_b_w, m_w_b_out, m_q_norm_g, m_kv_norm_g, m_w_uq, m_w_ukv, m_w_c_out, m_w_o, m_ln_g, m_ln_b, v_w_ada, v_b_ada, v_w_in, v_conv_a_w, v_conv_a_b, v_ln_a_g, v_ln_a_b, v_w_a_out, v_conv_b_w, v_w_b_out, v_q_norm_g, v_kv_norm_g, v_w_uq, v_w_ukv, v_w_c_out, v_w_o, v_ln_g, v_ln_b):
    P = dict(w_ada=w_ada, b_ada=b_ada, w_in=w_in, conv_a_w=conv_a_w, conv_a_b=conv_a_b, ln_a_g=ln_a_g, ln_a_b=ln_a_b,
             w_a_out=w_a_out, conv_b_w=conv_b_w, w_b_out=w_b_out, q_norm_g=q_norm_g, kv_norm_g=kv_norm_g, w_uq=w_uq,
             w_ukv=w_ukv, w_c_out=w_c_out, w_o=w_o, ln_g=ln_g, ln_b=ln_b)
    Mo = dict(w_ada=m_w_ada, b_ada=m_b_ada, w_in=m_w_in, conv_a_w=m_conv_a_w, conv_a_b=m_conv_a_b, ln_a_g=m_ln_a_g,
              ln_a_b=m_ln_a_b, w_a_out=m_w_a_out, conv_b_w=m_conv_b_w, w_b_out=m_w_b_out, q_norm_g=m_q_norm_g,
              kv_norm_g=m_kv_norm_g, w_uq=m_w_uq, w_ukv=m_w_ukv, w_c_out=m_w_c_out, w_o=m_w_o, ln_g=m_ln_g, ln_b=m_ln_b)
    Vo = dict(w_ada=v_w_ada, b_ada=v_b_ada, w_in=v_w_in, conv_a_w=v_conv_a_w, conv_a_b=v_conv_a_b, ln_a_g=v_ln_a_g,
              ln_a_b=v_ln_a_b, w_a_out=v_w_a_out, conv_b_w=v_conv_b_w, w_b_out=v_w_b_out, q_norm_g=v_q_norm_g,
              kv_norm_g=v_kv_norm_g, w_uq=v_w_uq, w_ukv=v_w_ukv, w_c_out=v_w_c_out, w_o=v_w_o, ln_g=v_ln_g, ln_b=v_ln_b)
    ORDER = ("w_ada", "b_ada", "w_in", "conv_a_w", "conv_a_b", "ln_a_g", "ln_a_b", "w_a_out", "conv_b_w", "w_b_out",
             "q_norm_g", "kv_norm_g", "w_uq", "w_ukv", "w_c_out", "w_o", "ln_g", "ln_b")
    L = w_ada.shape[0]
    S = x.shape[1]
    me = 4 * lax.axis_index("x") + 2 * lax.axis_index("y") + lax.axis_index("c")
    x2 = x[0]
    tgt = loss_target[0]

    small_in = _rows(jnp.concatenate([c.reshape(-1), conv_a_w.reshape(-1), conv_b_w.reshape(-1)]))
    w_in_b = jnp.pad(w_in.astype(BF), ((0, 0), (0, 0), (0, W_IN_PAD - IN_SHARD)))
    misc_b = jnp.concatenate([w_a_out, w_b_out, w_c_out, w_ukv, jnp.pad(w_uq, ((0, 0), (0, 0), (0, LANE - UQ_SHARD)))],
                             axis=1).astype(BF)
    w_o_b = w_o.astype(BF)
    gathered = [None] * L
    sg = exchange("gather_small", [small_in], [])[0]
    sgf = sg.reshape(NDEV, -1)
    c_all = sgf[:, :D_MODEL]
    o1 = D_MODEL + L * 31 * 64
    conv_a_full = sgf[:, D_MODEL:o1].reshape(NDEV, L, 31, 64).transpose(1, 2, 0, 3).reshape(L, 31, CW)
    conv_b_full = sgf[:, o1:o1 + L * 3 * 64].reshape(NDEV, L, 3, 64).transpose(1, 2, 0, 3).reshape(L, 3, CW)

    c_act = rowwise("silu_c", _silu, 16, 16, [(jnp.pad(c_all, ((0, 8), (0, 0))), D_MODEL, 0)], [], [(D_MODEL, BF)])[0]
    ncol = w_ada.shape[2]
    w_ada_b = w_ada.astype(BF).transpose(1, 0, 2).reshape(D_MODEL, L * ncol)
    b_mine = lax.dynamic_slice_in_dim(b_ada, me * ncol, ncol, axis=1).reshape(1, L * ncol)
    ada_part = mm(c_act, w_ada_b, name="mm_ada", bias=b_mine)
    ada_rows = -(-(L * ncol) // (LANE * 8)) * 8
    ada_send = jnp.pad(ada_part[:NDEV].reshape(NDEV, -1, LANE), ((0, 0), (0, ada_rows - L * ncol // LANE), (0, 0)))
    ada_recv = exchange("a2a_ada", [], [ada_send])[0]
    ada = ada_recv[:, :L * ncol // LANE].reshape(NDEV, L, ncol).transpose(1, 0, 2).reshape(L, 3, D_MODEL)
    gathered[0] = [gather_two_level("gather0_w_in", w_in_b[0], ada)]
    pending_rest, rest_token = exchange_begin("gather0_rest", [misc_b[0], w_o_b[0]], 2, gathered[0][0])

    inv_freq = ROPE_THETA ** (-jnp.arange(0, ROPE, 2, dtype=F32) / ROPE)
    ang = positions[0].astype(F32)[:, None] * inv_freq
    tabs = (jnp.tile(jnp.cos(ang), (1, 2 * LANE // ROPE)), jnp.tile(jnp.sin(ang), (1, 2 * LANE // ROPE)))

    straight = np.arange(D_MODEL)
    fwd_in = [(0, 0, 0, 0, D_MODEL, _to_gathered(IN_PERM, IN_SHARD, W_IN_PAD))]
    fwd_misc = [(0, ROW_A, 0, 0, CW, straight), (0, ROW_B, 1, 0, CW, straight), (0, ROW_C, 2, 0, CW, straight),
                (0, ROW_UKV, 3, 0, KVL, UKV_PERM), (0, ROW_UQ, 4, 0, QL, _to_gathered(UQ_PERM, UQ_SHARD, LANE))]
    rev_in = [(0, 0, 0, 0, D_MODEL, _from_full(_inverse(IN_PERM, D_IN), IN_SHARD, W_IN_PAD))]
    rev_misc = [(0, 0, 0, ROW_A, CW, straight), (1, 0, 0, ROW_B, CW, straight), (2, 0, 0, ROW_C, CW, straight),
                (3, 0, 0, ROW_UKV, KVL, _from_full(_inverse(UKV_PERM, HEADS * (NOPE + VH)), LANE, LANE)),
                (4, 0, 0, ROW_UQ, QL, _from_full(_inverse(UQ_PERM, HEADS * (NOPE + ROPE)), UQ_SHARD, LANE))]

    def layer_weights(l, deps):
        w_in_p = col_gather("relayout_w_in", [gathered[l][0]], [(D_MODEL, NP)], fwd_in, deps)[0]

        def late(after):
            if len(gathered[l]) == 1:
                gathered[l] += exchange_end(pending_rest, after)
            _, g_misc, g_o = gathered[l]
            a_out, b_out, c_out, ukv, uq = col_gather(
                "relayout_misc", [g_misc],
                [(CW, D_MODEL)] * 3 + [(KVL, HEADS * (NOPE + VH)), (QL, HEADS * (NOPE + ROPE))], fwd_misc, deps)
            return {"a_out": a_out, "b_out": b_out, "c_out": c_out, "uq": uq, "ukv": ukv,
                    "o": g_o.reshape(D_MODEL, D_MODEL)}

        return {
            "in": w_in_p, "late": late,
            "conv_a": jnp.pad(conv_a_full[l], ((0, 1), (0, 0))), "conv_b": jnp.pad(conv_b_full[l], ((0, 5), (0, 0))),
            "vec_a": jnp.stack([conv_a_b[l], ln_a_g[l], ln_a_b[l]]), "gq": q_norm_g[l][None], "gkv": kv_norm_g[l][None],
            "lnv": jnp.stack([ln_g[l], ln_b[l]]),
        }

    h = x2
    saved, weights = [], []
    handles, token = {}, rest_token
    for l in range(1, L):
        handles[l], token = exchange_begin("gather%d" % l, [w_in_b[l], misc_b[l], w_o_b[l]], 3, token)
    for l in range(L):
        ada_l, deps = (ada[l] + token[0, 0], (token,)) if l == 0 else (ada[l], ())
        h, sv, Wl = layer_fwd(h, ada_l, layer_weights(l, deps), tabs, S)
        if l + 1 < L:
            gathered[l + 1] = exchange_end(handles[l + 1], h)
        saved.append(sv)
        weights.append(Wl)

    def loss_fn(y, t):
        e = y - t
        return e * (1.0 / D_MODEL), _colsum(e * e)

    dy, sq = rowwise("loss", loss_fn, S, 256, [(h, D_MODEL, 0), (tgt, D_MODEL, 0)], [], [(D_MODEL, F32)], [D_MODEL])
    loss = lax.psum(0.5 * jnp.sum(sq) / D_MODEL, ("x", "y", "c"))
    loss, dy = lax.optimization_barrier((loss, dy))

    grads, d_adas, recv = [None] * L, [None] * L, [None] * L
    pending, token = None, None

    def send_rest(g):
        send_misc = col_gather("unrelayout_misc", [g["w_a_out"], g["w_b_out"], g["w_c_out"], g["w_ukv"], g["w_uq"]],
                               [(NDEV, MISC_ROWS, LANE)], rev_misc)[0]
        return [send_misc, g["w_o"].reshape(NDEV, D_MODEL // NDEV, D_MODEL)]

    rest0 = []

    def early_rest(g):
        handle, tok = exchange_begin("scatter0_rest", send_rest(g), 0, g["w_o"])
        rest0.append(handle)
        return (tok,)

    for l in reversed(range(L)):
        ada_l = ada[l] if token is None else ada[l] + token[0, 0]
        dy, g, d_adas[l] = layer_bwd(dy, saved[l], ada_l, weights[l], tabs, S, early_rest if l == 0 else None)
        grads[l] = g
        if pending is not None:
            recv[l + 1] = exchange_end(pending, dy)
        send_in = col_gather("unrelayout_w_in", [g["w_in"]], [(NDEV, D_MODEL, W_IN_PAD)], rev_in)[0]
        if l == 0:
            def layer_vec(i):
                parts = [grads[i][n].reshape(-1) for n in SMALL]
                parts += [grads[i]["conv_a_w"][:31].reshape(-1), grads[i]["conv_b_w"][:3].reshape(-1),
                          d_adas[i].reshape(-1)]
                return jnp.concatenate(parts)

            small_sizes = [int(grads[0][n].size) for n in SMALL] + [31 * CW, 3 * CW, 3 * D_MODEL]
            gsmall = exchange("gather_small_grads", [_rows(jnp.concatenate([layer_vec(i) for i in range(L)]))], [])[0]
            pending, token = exchange_begin("scatter0", [send_in], 0, gsmall)
        else:
            pending, token = exchange_begin("scatter%d" % l, [send_in] + send_rest(g), 0,
                                            dy if l + 1 == L else recv[l + 1][0])
    grad_x = dy[None]

    gsmall = gsmall + token[0, 0]
    gsum = sum_slots("sum_small", gsmall).reshape(-1)
    recv[0] = [None] + exchange_end(rest0[0], gsum)
    Gr = {}
    offs = np.cumsum([0] + small_sizes)
    per_layer = int(offs[-1])
    gsum = gsum[:L * per_layer].reshape(L, per_layer)
    for i, n in enumerate(SMALL):
        Gr[n] = gsum[:, offs[i]:offs[i + 1]]
    ca = gsum[:, offs[7]:offs[8]].reshape(L, 31, CW)
    cbw = gsum[:, offs[8]:offs[9]].reshape(L, 3, CW)
    Gr["conv_a_w"] = lax.dynamic_slice_in_dim(ca, me * 64, 64, axis=2)
    Gr["conv_b_w"] = lax.dynamic_slice_in_dim(cbw, me * 64, 64, axis=2)
    Gr["b_ada"] = gsum[:, offs[9]:offs[10]]
    d_ada_all = gsmall.reshape(NDEV, -1)[:, :L * per_layer].reshape(NDEV, L, per_layer)[:, :, offs[9]:offs[10]]
    d_mine = lax.dynamic_slice_in_dim(d_ada_all, me * ncol, ncol, axis=2).reshape(NDEV, L * ncol)
    g_ada = mm(c_act, jnp.pad(d_mine, ((0, 8), (0, 0))).astype(BF), name="mm_gw_ada", trans_a=True)
    Gr["w_ada"] = g_ada.reshape(D_MODEL, L, ncol).transpose(1, 0, 2)

    D, NM, NV = {}, {}, {}
    D["w_ada"], NM["w_ada"], NV["w_ada"] = adamw("adamw_w_ada", P["w_ada"], Gr["w_ada"], Mo["w_ada"], Vo["w_ada"])
    Gr["w_o"], D["w_o"], NM["w_o"], NV["w_o"] = sum_adamw(
        "sum_adamw_w_o", [recv[l][2] for l in range(L)], P["w_o"], Mo["w_o"], Vo["w_o"])
    for n, row0 in (("w_a_out", ROW_A), ("w_b_out", ROW_B), ("w_c_out", ROW_C), ("w_ukv", ROW_UKV), ("w_uq", ROW_UQ)):
        Gr[n], D[n], NM[n], NV[n] = sum_adamw("sum_adamw_" + n, [recv[l][1] for l in range(L)], P[n], Mo[n], Vo[n],
                                              row0=row0)
    w_l, m_l, v_l, _ = lax.optimization_barrier((P["w_in"], Mo["w_in"], Vo["w_in"], token))
    upper = sum_adamw("sum_adamw_w_in_upper", [recv[l][0] for l in range(1, L)], w_l, m_l, v_l, lo=1)
    recv[0][0] = exchange_end(pending, upper[1])[0]
    Gr["w_in"], D["w_in"], NM["w_in"], NV["w_in"] = sum_adamw(
        "sum_adamw_w_in", [recv[0][0]], w_l, m_l, v_l, lo=0, prev=upper)
    packed =("b_ada", "conv_a_w", "conv_b_w") + SMALL
    pk = lambda T_: _rows(jnp.concatenate([T_[n].reshape(-1) for n in packed]))[None]
    dS, mS, vS = adamw("adamw_small", pk(P), pk(Gr), pk(Mo), pk(Vo))
    o = 0
    for n in packed:
        sz = int(np.prod(P[n].shape))
        D[n] = dS.reshape(-1)[o:o + sz].reshape(P[n].shape)
        NM[n] = mS.reshape(-1)[o:o + sz].reshape(P[n].shape)
        NV[n] = vS.reshape(-1)[o:o + sz].reshape(P[n].shape)
        o += sz
    return (loss, grad_x, *[Gr[n] for n in ORDER], *[D[n] for n in ORDER], *[NM[n] for n in ORDER],
            *[NV[n] for n in ORDER])
```

```python
import functools
import math

import numpy as np
import jax
import jax.numpy as jnp
from jax import lax
from jax.experimental import pallas as pl
from jax.experimental.pallas import tpu as pltpu

BF = jnp.bfloat16
F32 = jnp.float32
MESH = pl.DeviceIdType.MESH
NDEV = 8

HEADS, NOPE, ROPE, VH = 8, 64, 32, 64
HP = 128
ROPE_THETA = 10000.0
LN_EPS = 1e-5
RMS_EPS = 1e-6
LR, B1, B2, EPS, WD, STEP = 0.001, 0.9, 0.999, 1e-08, 0.01, 10

LANE = 128
VMEM_LIMIT = 56 * 1024 * 1024

D_MODEL, CW, QL, KVL = 1024, 512, 384, 256
OFF_M, OFF_A, OFF_AG, OFF_B, OFF_CG, OFF_GB, OFF_BG = 0, 3072, 4096, 4608, 5632, 6144, 6656
OFF_KV, OFF_KR, OFF_Q, NP = 7168, 7424, 7680, 8192
D_IN = 7840


def _cparams(**kw):
    return pltpu.CompilerParams(vmem_limit_bytes=VMEM_LIMIT, **kw)


def _sigmoid(x):
    return jax.nn.sigmoid(x)


def _silu(x):
    return x * _sigmoid(x)


def _dsilu(x):
    s = _sigmoid(x)
    return s * (1.0 + x * (1.0 - s))


def _pick_tile(n, cap, mult):
    if n <= cap:
        return n
    for t in range(cap - cap % mult, 0, -mult):
        if n % t == 0:
            return t
    raise ValueError((n, cap, mult))


def mm(a, b, *, name, trans_a=False, trans_b=False, out_dtype=F32, bias=None, tm=1024, tn=1024, tk=2048, deps=()):
    if trans_a:
        K, M = a.shape
    else:
        M, K = a.shape
    if trans_b:
        N, K2 = b.shape
    else:
        K2, N = b.shape
    assert K == K2 and not (trans_a and trans_b), (a.shape, b.shape)
    tm, tn = _pick_tile(M, tm, 16), _pick_tile(N, tn, LANE)
    tk = _pick_tile(K, tk, LANE if trans_b else 16)
    assert M % tm == 0 and N % tn == 0 and K % tk == 0, (M, N, K, tm, tn, tk)
    nk = K // tk
    dims = (((0 if trans_a else 1,), (1 if trans_b else 0,)), ((), ()))
    has_bias = bias is not None

    def body(*refs):
        a_ref, b_ref = refs[0], refs[1]
        bias_ref = refs[2] if has_bias else None
        o_ref = refs[(3 if has_bias else 2) + len(deps)]
        p = lax.dot_general(a_ref[...], b_ref[...], dims, preferred_element_type=F32)

        def finish(v):
            if has_bias:
                v = v + bias_ref[...]
            o_ref[...] = v.astype(o_ref.dtype)

        if nk == 1:
            finish(p)
        else:
            acc = refs[-1]
            k = pl.program_id(2)

            @pl.when(k == 0)
            def _():
                acc[...] = p

            @pl.when(k > 0)
            def _():
                acc[...] += p

            @pl.when(k == nk - 1)
            def _():
                finish(acc[...])

    if trans_a:
        a_spec = pl.BlockSpec((tk, tm), lambda i, j, k: (k, i))
    else:
        a_spec = pl.BlockSpec((tm, tk), lambda i, j, k: (i, k))
    if trans_b:
        b_spec = pl.BlockSpec((tn, tk), lambda i, j, k: (j, k))
    else:
        b_spec = pl.BlockSpec((tk, tn), lambda i, j, k: (k, j))
    in_specs = [a_spec, b_spec]
    args = [a, b]
    if has_bias:
        in_specs.append(pl.BlockSpec((1, tn), lambda i, j, k: (0, j)))
        args.append(bias)
    in_specs += [ANY_SPEC] * len(deps)
    args += list(deps)
    return pl.pallas_call(
        body, name=name, grid=(M // tm, N // tn, nk),
        in_specs=in_specs, out_specs=pl.BlockSpec((tm, tn), lambda i, j, k: (i, j)),
        out_shape=jax.ShapeDtypeStruct((M, N), out_dtype),
        scratch_shapes=[pltpu.VMEM((tm, tn), F32)] if nk > 1 else [],
        compiler_params=_cparams(),
    )(*args)


def rowwise(name, fn, S, T, row_ins, full_ins, row_outs, acc_outs=(), into=None):
    n_in = len(row_ins) + len(full_ins)
    n_ro, n_ao = len(row_outs), len(acc_outs)
    alias = into is not None and into[0] is not None
    T = min(T, S)

    def body(*refs):
        vals = [r[...] for r in refs[:n_in]]
        vals = [v.astype(F32) if v.dtype == BF else v for v in vals]
        outs = fn(*vals)
        if not isinstance(outs, (tuple, list)):
            outs = (outs,)
        assert len(outs) == n_ro + n_ao, (name, len(outs))
        o0 = n_in + (1 if alias else 0)
        for r, v in zip(refs[o0:o0 + n_ro], outs[:n_ro]):
            r[...] = v.astype(r.dtype)
        first = pl.program_id(0) == 0
        for r, v in zip(refs[o0 + n_ro:], outs[n_ro:]):
            def init(r=r, v=v):
                r[...] = v

            def accum(r=r, v=v):
                r[...] += v

            pl.when(first)(init)
            pl.when(jnp.logical_not(first))(accum)

    in_specs, args = [], []
    for arr, W, off in row_ins:
        assert off % W == 0 and arr.shape[0] == S, (name, arr.shape, W, off)
        in_specs.append(pl.BlockSpec((T, W), functools.partial(lambda i, cb: (i, cb), cb=off // W)))
        args.append(arr)
    for arr in full_ins:
        in_specs.append(pl.BlockSpec(arr.shape, lambda i: (0, 0)))
        args.append(arr)
    out_specs = [pl.BlockSpec((T, W), lambda i: (i, 0)) for W, _ in row_outs]
    out_shape = [jax.ShapeDtypeStruct((S, W), dt) for W, dt in row_outs]
    aliases = {}
    if into is not None:
        buf, total, off = into
        W0, dt0 = row_outs[0]
        assert off % W0 == 0
        out_specs[0] = pl.BlockSpec((T, W0), functools.partial(lambda i, cb: (i, cb), cb=off // W0))
        out_shape[0] = jax.ShapeDtypeStruct((S, total), dt0)
        if alias:
            in_specs.append(ANY_SPEC)
            args.append(buf)
            aliases = {n_in: 0}
    out_specs += [pl.BlockSpec((1, W), lambda i: (0, 0)) for W in acc_outs]
    out_shape += [jax.ShapeDtypeStruct((1, W), F32) for W in acc_outs]
    return pl.pallas_call(
        body, name=name, grid=(S // T,), in_specs=in_specs, out_specs=out_specs, out_shape=out_shape,
        input_output_aliases=aliases, compiler_params=_cparams(),
    )(*args)


def _colsum(v):
    return jnp.sum(v, axis=0, keepdims=True)


def _ln_stats(r):
    mu = jnp.mean(r, axis=-1, keepdims=True)
    d = r - mu
    var = jnp.mean(d * d, axis=-1, keepdims=True)
    rstd = lax.rsqrt(var + LN_EPS)
    return d * rstd, rstd


def _ln_bwd(dn, n, rstd):
    return rstd * (dn - jnp.mean(dn, axis=-1, keepdims=True) - n * jnp.mean(dn * n, axis=-1, keepdims=True))


CPAD = 32
TC = 64


def _pre(mode, x1, x2):
    return x1 * _sigmoid(x2) if mode == "glu" else x1 * x2


def _shifted(ext, sft):
    n = TC + CPAD
    return pltpu.roll(ext, (n - sft) % n, 0)[0:TC]


def _interleaved_specs(S, off):
    return [pl.BlockSpec((S, LANE), functools.partial(lambda j, o: (0, o + 2 * j), o=off // LANE)),
            pl.BlockSpec((S, LANE), functools.partial(lambda j, o: (0, o + 2 * j + 1), o=off // LANE))]


def conv_fwd(name, src, off, w_pad, taps, mode, S, C):
    nchunk = S // TC

    def body(x1_ref, x2_ref, w_ref, o_ref, a_pad):
        a_pad[0:CPAD, :] = jnp.zeros((CPAD, LANE), F32)

        def fill(i, _):
            r = pl.multiple_of(i * 256, 256)
            a_pad[pl.ds(CPAD + r, 256), :] = _pre(mode, x1_ref[pl.ds(r, 256), :].astype(F32),
                                                  x2_ref[pl.ds(r, 256), :].astype(F32))
            return 0

        lax.fori_loop(0, S // 256, fill, 0)

        def chunk(i, _):
            base = pl.multiple_of(i * TC, TC)
            ext = a_pad[pl.ds(base, TC + CPAD), :]
            acc = jnp.zeros((TC, LANE), F32)
            for k in range(taps):
                acc = acc + w_ref[pl.ds(k, 1), :] * _shifted(ext, CPAD - (taps - 1) + k)
            o_ref[pl.ds(base, TC), :] = acc
            return 0

        lax.fori_loop(0, nchunk, chunk, 0)

    kp = w_pad.shape[0]
    return pl.pallas_call(
        body, name=name, grid=(C // LANE,),
        in_specs=_interleaved_specs(S, off) + [pl.BlockSpec((kp, LANE), lambda j: (0, j))],
        out_specs=pl.BlockSpec((S, LANE), lambda j: (0, j)),
        out_shape=jax.ShapeDtypeStruct((S, C), F32),
        scratch_shapes=[pltpu.VMEM((S + CPAD, LANE), F32)],
        compiler_params=_cparams(),
    )(src, src, w_pad)


def conv_bwd(name, src, off, dc, w_pad, taps, mode, S, C, buf):
    nchunk = S // TC
    kp = w_pad.shape[0]

    def body(x1_ref, x2_ref, dc_ref, w_ref, _, d_ref, dw_ref, a_pad, dc_pad, dw_acc):
        a_pad[0:CPAD, :] = jnp.zeros((CPAD, LANE), F32)
        dc_pad[S:S + CPAD, :] = jnp.zeros((CPAD, LANE), F32)
        dw_acc[...] = jnp.zeros(dw_acc.shape, F32)

        def fill(i, _):
            r = pl.multiple_of(i * 256, 256)
            a_pad[pl.ds(CPAD + r, 256), :] = _pre(mode, x1_ref[pl.ds(r, 256), :].astype(F32),
                                                  x2_ref[pl.ds(r, 256), :].astype(F32))
            dc_pad[pl.ds(r, 256), :] = dc_ref[pl.ds(r, 256), :]
            return 0

        lax.fori_loop(0, S // 256, fill, 0)

        def chunk(i, _):
            base = pl.multiple_of(i * TC, TC)
            ext_d = dc_pad[pl.ds(base, TC + CPAD), :]
            ext_a = a_pad[pl.ds(base, TC + CPAD), :]
            dcv = ext_d[0:TC]
            da = jnp.zeros((TC, LANE), F32)
            for k in range(taps):
                da = da + w_ref[pl.ds(k, 1), :] * _shifted(ext_d, taps - 1 - k)
                prod = dcv * _shifted(ext_a, CPAD - (taps - 1) + k)
                fold = prod[0:8]
                for g in range(1, TC // 8):
                    fold = fold + prod[8 * g:8 * g + 8]
                dw_acc[pl.ds(8 * k, 8), :] += fold
            x1 = x1_ref[pl.ds(base, TC), :].astype(F32)
            x2 = x2_ref[pl.ds(base, TC), :].astype(F32)
            if mode == "glu":
                s = _sigmoid(x2)
                d1, d2 = da * s, da * x1 * s * (1.0 - s)
            else:
                d1, d2 = da * x2, da * x1
            d_ref[pl.ds(base, TC), 0:LANE] = d1.astype(BF)
            d_ref[pl.ds(base, TC), LANE:2 * LANE] = d2.astype(BF)
            return 0

        lax.fori_loop(0, nchunk, chunk, 0)
        dw_ref[...] = jnp.zeros(dw_ref.shape, F32)
        for k in range(taps):
            dw_ref[pl.ds(k, 1), :] = jnp.sum(dw_acc[pl.ds(8 * k, 8), :], axis=0, keepdims=True)

    blk = pl.BlockSpec((S, LANE), lambda j: (0, j))
    return pl.pallas_call(
        body, name=name, grid=(C // LANE,),
        in_specs=_interleaved_specs(S, off) + [blk, pl.BlockSpec((kp, LANE), lambda j: (0, j)), ANY_SPEC],
        out_specs=[pl.BlockSpec((S, 2 * LANE), functools.partial(lambda j, o: (0, o + j), o=off // (2 * LANE))),
                   pl.BlockSpec((kp, LANE), lambda j: (0, j))],
        out_shape=[jax.ShapeDtypeStruct(buf.shape, BF), jax.ShapeDtypeStruct((kp, C), F32)],
        input_output_aliases={4: 0},
        scratch_shapes=[pltpu.VMEM((S + CPAD, LANE), F32), pltpu.VMEM((S + CPAD, LANE), F32),
                        pltpu.VMEM((8 * kp, LANE), F32)],
        compiler_params=_cparams(),
    )(src, src, dc, w_pad, buf)


FWD_TILES = (512, 512)
BWD_TILES = (512, 512)
QUADS = HEADS // 4
QW, KVW = 4 * (NOPE + ROPE), 4 * (NOPE + VH)
SCALE = (NOPE + ROPE) ** -0.5
NT_DIMS = (((1,), (1,)), ((), ()))
TN_DIMS = (((0,), (0,)), ((), ()))


def _lane_mask(width, group, dtype):
    lane = lax.broadcasted_iota(jnp.int32, (1, LANE), 1)
    return jnp.where(lane // width == group, 1.0, 0.0).astype(dtype)


def _visible(tq, tk, off):
    row = lax.broadcasted_iota(jnp.int32, (tq, tk), 0)
    col = lax.broadcasted_iota(jnp.int32, (tq, tk), 1)
    return col <= row + off


def _attn_tiles(S, tq, tk):
    tk = tk if S % tk == 0 else 256
    return min(tq, tk), tk


def attn_fwd(q, kv, kpe, S):
    tq, tk = _attn_tiles(S, *FWD_TILES)
    nq = S // tq

    def body(q_ref, kv_ref, kp_ref, o_ref, lse_ref):
        for t in range(2):
            cols = slice(t * LANE, (t + 1) * LANE)
            for hh in range(2):
                def q_block(qi, _, t=t, hh=hh, cols=cols):
                    r0 = pl.multiple_of(qi * tq, tq)
                    qcat = jnp.concatenate([q_ref[pl.ds(r0, tq), cols] * _lane_mask(NOPE, hh, BF),
                                            q_ref[pl.ds(r0, tq), 2 * LANE:3 * LANE] * _lane_mask(ROPE, 2 * t + hh, BF)],
                                           axis=1)
                    nfull = (qi * tq) // tk

                    def step(kj, carry, masked):
                        m, l, acc = carry
                        c0 = pl.multiple_of(kj * tk, tk)
                        kc = jnp.concatenate([kv_ref[pl.ds(c0, tk), cols], kp_ref[pl.ds(c0, tk), :]], axis=1)
                        vt = kv_ref[pl.ds(c0, tk), (2 + t) * LANE:(3 + t) * LANE]
                        s = lax.dot_general(qcat, kc, NT_DIMS, preferred_element_type=F32) * SCALE
                        if masked:
                            s = jnp.where(_visible(tq, tk, qi * tq - nfull * tk), s, -jnp.inf)
                        m_new = jnp.maximum(m, jnp.max(s, axis=-1, keepdims=True))
                        p = jnp.exp(s - m_new)
                        alpha = jnp.exp(m - m_new)
                        l = alpha * l + jnp.sum(p, axis=-1, keepdims=True)
                        acc = alpha * acc + jnp.dot(p.astype(BF), vt, preferred_element_type=F32)
                        return m_new, l, acc

                    init = (jnp.full((tq, 1), -jnp.inf, F32), jnp.zeros((tq, 1), F32), jnp.zeros((tq, LANE), F32))
                    carry = lax.fori_loop(0, nfull, lambda kj, c: step(kj, c, False), init)
                    m, l, acc = step(nfull, carry, True)
                    mine = _lane_mask(NOPE, hh, F32)
                    if hh == 0:
                        o_ref[pl.ds(r0, tq), cols] = (acc / l) * mine
                        lse_ref[pl.ds(r0, tq), cols] = (m + jnp.log(l)) * mine
                    else:
                        o_ref[pl.ds(r0, tq), cols] += (acc / l) * mine
                        lse_ref[pl.ds(r0, tq), cols] += (m + jnp.log(l)) * mine
                    return 0

                lax.fori_loop(0, nq, q_block, 0)

    return pl.pallas_call(
        body, name="attn_fwd", grid=(QUADS,),
        in_specs=[pl.BlockSpec((S, QW), lambda g: (0, g)), pl.BlockSpec((S, KVW), lambda g: (0, g)),
                  pl.BlockSpec((S, LANE), lambda g: (0, 0))],
        out_specs=[pl.BlockSpec((S, 2 * LANE), lambda g: (0, g))] * 2,
        out_shape=[jax.ShapeDtypeStruct((S, HEADS * VH), F32)] * 2,
        compiler_params=_cparams(),
    )(q, kv, kpe)


def attn_bwd(q, kv, kpe, o, lse, do, S):
    tq, tk = _attn_tiles(S, *BWD_TILES)
    nq = S // tq

    def body(q_ref, kv_ref, kp_ref, o_ref, lse_ref, do_ref, dq_ref, dkv_ref, dkp_ref, dq_acc, dk_acc, dv_acc):
        for t in range(2):
            cols = slice(t * LANE, (t + 1) * LANE)
            dk_acc[...] = jnp.zeros(dk_acc.shape, F32)
            dv_acc[...] = jnp.zeros(dv_acc.shape, F32)
            for hh in range(2):
                def q_block(qi, _, t=t, hh=hh, cols=cols):
                    r0 = pl.multiple_of(qi * tq, tq)
                    mine = _lane_mask(NOPE, hh, F32)
                    qcat = jnp.concatenate([q_ref[pl.ds(r0, tq), cols] * _lane_mask(NOPE, hh, BF),
                                            q_ref[pl.ds(r0, tq), 2 * LANE:3 * LANE] * _lane_mask(ROPE, 2 * t + hh, BF)],
                                           axis=1)
                    dof = do_ref[pl.ds(r0, tq), cols] * mine
                    dob = dof.astype(BF)
                    delta = jnp.sum(dof * o_ref[pl.ds(r0, tq), cols], axis=-1, keepdims=True)
                    lse_h = lse_ref[pl.ds(r0, tq), cols][:, hh * NOPE:hh * NOPE + 1]
                    nfull = (qi * tq) // tk
                    dq_acc[...] = jnp.zeros(dq_acc.shape, F32)

                    def step(kj, _, masked):
                        c0 = pl.multiple_of(kj * tk, tk)
                        kc = jnp.concatenate([kv_ref[pl.ds(c0, tk), cols], kp_ref[pl.ds(c0, tk), :]], axis=1)
                        vt = kv_ref[pl.ds(c0, tk), (2 + t) * LANE:(3 + t) * LANE]
                        s = lax.dot_general(qcat, kc, NT_DIMS, preferred_element_type=F32) * SCALE
                        if masked:
                            s = jnp.where(_visible(tq, tk, qi * tq - nfull * tk), s, -jnp.inf)
                        p = jnp.exp(s - lse_h)
                        dp = lax.dot_general(dob, vt, NT_DIMS, preferred_element_type=F32)
                        ds = (p * (dp - delta) * SCALE).astype(BF)
                        dv_acc[pl.ds(c0, tk), :] += lax.dot_general(p.astype(BF), dob, TN_DIMS,
                                                                    preferred_element_type=F32)
                        dk_acc[pl.ds(c0, tk), :] += lax.dot_general(ds, qcat, TN_DIMS, preferred_element_type=F32)
                        dq_acc[...] += jnp.dot(ds, kc, preferred_element_type=F32)
                        return 0

                    lax.fori_loop(0, nfull, lambda kj, c: step(kj, c, False), 0)
                    step(nfull, 0, True)
                    d = dq_acc[...]
                    pe = d[:, LANE:] * _lane_mask(ROPE, 2 * t + hh, F32)
                    if hh == 0:
                        dq_ref[pl.ds(r0, tq), cols] = d[:, :LANE] * mine
                    else:
                        dq_ref[pl.ds(r0, tq), cols] += d[:, :LANE] * mine
                    if t == 0 and hh == 0:
                        dq_ref[pl.ds(r0, tq), 2 * LANE:3 * LANE] = pe
                    else:
                        dq_ref[pl.ds(r0, tq), 2 * LANE:3 * LANE] += pe
                    return 0

                lax.fori_loop(0, nq, q_block, 0)
            dkv_ref[:, t * LANE:(t + 1) * LANE] = dk_acc[:, :LANE].astype(BF)
            dkv_ref[:, (2 + t) * LANE:(3 + t) * LANE] = dv_acc[...].astype(BF)
            if t == 0:
                dkp_ref[...] = dk_acc[:, LANE:]
            else:
                dkp_ref[...] += dk_acc[:, LANE:]

    qspec = pl.BlockSpec((S, QW), lambda g: (0, g))
    kvspec = pl.BlockSpec((S, KVW), lambda g: (0, g))
    ospec = pl.BlockSpec((S, 2 * LANE), lambda g: (0, g))
    return pl.pallas_call(
        body, name="attn_bwd", grid=(QUADS,),
        in_specs=[qspec, kvspec, pl.BlockSpec((S, LANE), lambda g: (0, 0)), ospec, ospec, ospec],
        out_specs=[qspec, kvspec, pl.BlockSpec((S, LANE), lambda g: (0, g))],
        out_shape=[jax.ShapeDtypeStruct((S, HEADS * (NOPE + ROPE)), F32), jax.ShapeDtypeStruct((S, HEADS * (NOPE + VH)), BF),
                   jax.ShapeDtypeStruct((S, HEADS * ROPE), F32)],
        scratch_shapes=[pltpu.VMEM((tq, 2 * LANE), F32), pltpu.VMEM((S, 2 * LANE), F32), pltpu.VMEM((S, LANE), F32)],
        compiler_params=_cparams(),
    )(q, kv, kpe, o, lse, do)


def exchange(name, gathers, a2as):
    n_g, n = len(gathers), len(gathers) + len(a2as)

    def body(*refs):
        ins, outs = refs[:n], refs[n:2 * n]
        send_sems, recv_sems, loc_sems = refs[2 * n:]
        x, y, c = lax.axis_index("x"), lax.axis_index("y"), lax.axis_index("c")
        me = 4 * x + 2 * y + c

        def peer(k):
            px = 1 - x if k & 4 else x
            py = 1 - y if k & 2 else y
            pc = 1 - c if k & 1 else c
            return (px, py, pc), 4 * px + 2 * py + pc

        def remote(a, k):
            pid, pflat = peer(k)
            src = ins[a] if a < n_g else ins[a].at[pflat]
            return pltpu.make_async_remote_copy(
                src_ref=src, dst_ref=outs[a].at[me], send_sem=send_sems.at[a, k - 1], recv_sem=recv_sems.at[a, k - 1],
                device_id=pid, device_id_type=MESH)

        def arrival(a, k):
            pid, pflat = peer(k)
            src = ins[a] if a < n_g else ins[a].at[pflat]
            return pltpu.make_async_remote_copy(
                src_ref=src, dst_ref=outs[a].at[pflat], send_sem=send_sems.at[a, k - 1], recv_sem=recv_sems.at[a, k - 1],
                device_id=pid, device_id_type=MESH)

        local = []
        for a in range(n):
            own = ins[a] if a < n_g else ins[a].at[me]
            cp = pltpu.make_async_copy(own, outs[a].at[me], loc_sems.at[a])
            cp.start()
            local.append(cp)
        sent = []
        for k in (1, 2, 4, 3, 5, 6, 7):
            for a in range(n):
                cp = remote(a, k)
                cp.start()
                sent.append(cp)
        for k in range(1, 8):
            for a in range(n):
                arrival(a, k).wait_recv()
        for cp in sent:
            cp.wait_send()
        for cp in local:
            cp.wait()

    out_shape = [jax.ShapeDtypeStruct((NDEV,) + g.shape, g.dtype) for g in gathers]
    out_shape += [jax.ShapeDtypeStruct(a.shape, a.dtype) for a in a2as]
    any_spec = pl.BlockSpec(memory_space=pl.ANY)
    return pl.pallas_call(
        body, name=name, in_specs=[any_spec] * n, out_specs=[any_spec] * n, out_shape=out_shape,
        scratch_shapes=[pltpu.SemaphoreType.DMA((n, NDEV - 1)), pltpu.SemaphoreType.DMA((n, NDEV - 1)),
                        pltpu.SemaphoreType.DMA((n,))],
    )(*gathers, *a2as)


def gather_two_level(name, block, dep):
    def body(x_ref, _, out_ref, stage, send_sems, recv_sems, loc_sem):
        x, y, c = lax.axis_index("x"), lax.axis_index("y"), lax.axis_index("c")
        me, sibling = (x, y, c), (x, y, 1 - c)
        chips = [(1 - x, y), (x, 1 - y), (1 - x, 1 - y)]

        def slot(px, py, pc):
            return out_ref.at[4 * px + 2 * py + pc]

        def copy(k, owner, to, src=None):
            return pltpu.make_async_remote_copy(
                src_ref=slot(*owner) if src is None else src, dst_ref=slot(*owner), send_sem=send_sems.at[k],
                recv_sem=recv_sems.at[k], device_id=to, device_id_type=MESH)

        load = pltpu.make_async_copy(x_ref, stage, loc_sem)
        load.start()
        first = [copy(0, me, sibling, src=x_ref)] + [copy(1 + j, me, (*chip, c), src=x_ref) for j, chip in enumerate(chips)]
        for cp in first:
            cp.start()
        load.wait()
        store = pltpu.make_async_copy(stage, slot(*me), loc_sem)
        store.start()
        passed = [copy(4 + j, (*chip, c), sibling) for j, chip in enumerate(chips)]
        for j, chip in enumerate(chips):
            copy(1 + j, (*chip, c), me).wait_recv()
            passed[j].start()
        copy(0, sibling, me).wait_recv()
        for j, chip in enumerate(chips):
            copy(4 + j, (*chip, 1 - c), me).wait_recv()
        for cp in first + passed:
            cp.wait_send()
        store.wait()

    return pl.pallas_call(
        body, name=name, in_specs=[pl.BlockSpec(memory_space=pl.ANY)] * 2, out_specs=pl.BlockSpec(memory_space=pl.ANY),
        out_shape=jax.ShapeDtypeStruct((NDEV,) + block.shape, block.dtype),
        scratch_shapes=[pltpu.VMEM(block.shape, block.dtype), pltpu.SemaphoreType.DMA((NDEV - 1,)),
                        pltpu.SemaphoreType.DMA((NDEV - 1,)), pltpu.SemaphoreType.DMA],
        compiler_params=_cparams(),
    )(block, dep)


def _peer(k, x, y, c):
    px = 1 - x if k & 4 else x
    py = 1 - y if k & 2 else y
    pc = 1 - c if k & 1 else c
    return (px, py, pc), 4 * px + 2 * py + pc


PEER_ORDER = (1, 2, 4, 3, 5, 6, 7)
HBM_SPEC = pl.BlockSpec(memory_space=pltpu.HBM)
SEM_SPEC = pl.BlockSpec(memory_space=pltpu.SEMAPHORE)
ANY_SPEC = pl.BlockSpec(memory_space=pl.ANY)


def _split_copies(ins, lands, n_g, send_sems, recv_sems):
    x, y, c = lax.axis_index("x"), lax.axis_index("y"), lax.axis_index("c")
    me = 4 * x + 2 * y + c

    def outgoing(a, k):
        pid, pflat = _peer(k, x, y, c)
        src = ins[a] if a < n_g else ins[a].at[pflat]
        return pltpu.make_async_remote_copy(
            src_ref=src, dst_ref=lands[a].at[me], send_sem=send_sems.at[a * (NDEV - 1) + k - 1],
            recv_sem=recv_sems.at[a * (NDEV - 1) + k - 1],
            device_id=pid, device_id_type=MESH)

    def arrival(a, k):
        pid, pflat = _peer(k, x, y, c)
        src = ins[a] if a < n_g else ins[a].at[pflat]
        return pltpu.make_async_remote_copy(
            src_ref=src, dst_ref=lands[a].at[pflat], send_sem=send_sems.at[a * (NDEV - 1) + k - 1],
            recv_sem=recv_sems.at[a * (NDEV - 1) + k - 1],
            device_id=pid, device_id_type=MESH)

    return outgoing, arrival


def exchange_begin(name, srcs, n_g, dep):
    n = len(srcs)
    land_shapes = [((NDEV,) + s.shape) if a < n_g else s.shape for a, s in enumerate(srcs)]

    def own_body(*refs):
        ins, outs = refs[:n], refs[n + 1:2 * n + 1]
        stage, sems = refs[2 * n + 1:3 * n + 1], refs[-1]
        me = 4 * lax.axis_index("x") + 2 * lax.axis_index("y") + lax.axis_index("c")
        cps = [pltpu.make_async_copy(ins[a] if a < n_g else ins[a].at[me], stage[a], sems.at[a]) for a in range(n)]
        for cp in cps:
            cp.start()
        for cp in cps:
            cp.wait()
        cps = [pltpu.make_async_copy(stage[a], outs[a].at[me], sems.at[a]) for a in range(n)]
        for cp in cps:
            cp.start()
        for cp in cps:
            cp.wait()

    lands = pl.pallas_call(
        own_body, name=name + "_own", in_specs=[ANY_SPEC] * (n + 1), out_specs=[ANY_SPEC] * n,
        out_shape=[jax.ShapeDtypeStruct(sh, s.dtype) for sh, s in zip(land_shapes, srcs)],
        scratch_shapes=[pltpu.VMEM(sh[1:], s.dtype) for sh, s in zip(land_shapes, srcs)] + [pltpu.SemaphoreType.DMA((n,))],
        compiler_params=_cparams(),
    )(*srcs, dep)

    def start_body(*refs):
        ins, lz = refs[:n], refs[n:2 * n]
        send_sems, recv_sems, token = refs[2 * n], refs[2 * n + 1], refs[-1]
        outgoing, _ = _split_copies(ins, lz, n_g, send_sems, recv_sems)
        for k in PEER_ORDER:
            for a in range(n):
                outgoing(a, k).start()
        token[...] = jnp.zeros(token.shape, F32)

    hbm = lambda t: pltpu.HBM(t.shape, t.dtype)
    res = pl.pallas_call(
        start_body, name=name + "_start",
        out_shape=(pltpu.SemaphoreType.DMA((n * (NDEV - 1),)), pltpu.SemaphoreType.DMA((n * (NDEV - 1),)),
                   *[hbm(s) for s in srcs], *[hbm(t) for t in lands], jax.ShapeDtypeStruct((8, LANE), F32)),
        in_specs=[HBM_SPEC] * (2 * n),
        out_specs=(SEM_SPEC, SEM_SPEC, *[HBM_SPEC] * (2 * n), pl.BlockSpec(memory_space=pltpu.VMEM)),
        input_output_aliases={i: 2 + i for i in range(2 * n)},
        compiler_params=pltpu.CompilerParams(has_side_effects=pltpu.SideEffectType.DATAFLOW_SIDE_EFFECTING),
    )(*[pltpu.with_memory_space_constraint(t, pltpu.HBM) for t in list(srcs) + list(lands)])
    return (name, n, n_g, res[:-1]), res[-1]


def exchange_end(handle, after):
    name, n, n_g, (send_sems, recv_sems, *bufs) = handle

    def wait_body(*refs):
        ins, lz = refs[:n], refs[n:2 * n]
        ss, rs = refs[2 * n], refs[2 * n + 1]
        outgoing, arrival = _split_copies(ins, lz, n_g, ss, rs)
        for k in range(1, NDEV):
            for a in range(n):
                arrival(a, k).wait_recv()
        for k in range(1, NDEV):
            for a in range(n):
                outgoing(a, k).wait_send()

    res = pl.pallas_call(
        wait_body, name=name + "_wait", out_shape=tuple(pltpu.HBM(t.shape, t.dtype) for t in bufs),
        in_specs=[HBM_SPEC] * (2 * n) + [SEM_SPEC, SEM_SPEC, ANY_SPEC], out_specs=[HBM_SPEC] * (2 * n),
        input_output_aliases={i: i for i in range(2 * n)},
        compiler_params=pltpu.CompilerParams(has_side_effects=pltpu.SideEffectType.DATAFLOW_SIDE_EFFECTING),
    )(*bufs, send_sems, recv_sems, after)
    return list(res[n:])


def _pick_rows(R, mult, cap):
    best = None
    for n in range(1, R + 1):
        if R % n == 0 and (R // n) % mult == 0 and R // n <= cap:
            best = R // n
            break
    assert best is not None, (R, mult, cap)
    return best


def sum_slots(name, x):
    _, R, _ = x.shape
    tr = _pick_rows(R, 16, 2304)

    def body(x_ref, o_ref):
        acc = x_ref[0].astype(F32)
        for d in range(1, NDEV):
            acc = acc + x_ref[d].astype(F32)
        o_ref[...] = acc

    return pl.pallas_call(
        body, name=name, grid=(R // tr,),
        in_specs=[pl.BlockSpec((NDEV, tr, LANE), lambda i: (0, i, 0))],
        out_specs=pl.BlockSpec((tr, LANE), lambda i: (i, 0)),
        out_shape=jax.ShapeDtypeStruct((R, LANE), F32), compiler_params=_cparams(),
    )(x)


def adamw(name, w, g, m, v):
    L, R, C = w.shape
    tr = _pick_rows(R, 8, 256) if R % 8 == 0 else R

    def body(w_ref, g_ref, m_ref, v_ref, d_ref, nm_ref, nv_ref):
        gg = g_ref[...]
        nm = B1 * m_ref[...] + (1.0 - B1) * gg
        nv = B2 * v_ref[...] + (1.0 - B2) * jnp.square(gg)
        m_hat = nm / (1.0 - B1 ** STEP)
        v_hat = nv / (1.0 - B2 ** STEP)
        d_ref[...] = -LR * (m_hat / (jnp.sqrt(v_hat) + EPS) + WD * w_ref[...])
        nm_ref[...] = nm
        nv_ref[...] = nv

    blk = pl.BlockSpec((1, tr, C), lambda l, i: (l, i, 0))
    shp = jax.ShapeDtypeStruct(w.shape, F32)
    return pl.pallas_call(
        body, name=name, grid=(L, R // tr), in_specs=[blk] * 4, out_specs=[blk] * 3, out_shape=[shp] * 3,
        compiler_params=_cparams(),
    )(w, g, m, v)


IN_SHARD = D_IN // NDEV
UQ_SHARD = HEADS * (NOPE + ROPE) // NDEV
W_IN_PAD = 1024
ROW_A, ROW_B, ROW_C, ROW_UKV, ROW_UQ, MISC_ROWS = 0, 512, 1024, 1536, 1792, 2176


def _in_perm_index():
    ar = np.arange
    z = lambda n: np.full((n,), -1, np.int64)
    mix = lambda lo1, lo2: np.concatenate([ar(lo + LANE * j, lo + LANE * (j + 1)) for j in range(CW // LANE)
                                           for lo in (lo1, lo2)])
    return np.concatenate([ar(4768, 7840), mix(0, 512), ar(1024, 1536), mix(1536, 2560), ar(4256, 4768), ar(2048, 2560),
                           ar(3072, 3584), ar(3968, 4224), ar(4224, 4256), z(OFF_Q - OFF_KR - ROPE), ar(3584, 3968),
                           z(NP - OFF_Q - QL)])


def _head_perm_index(a, b):
    parts = []
    for g in range(QUADS):
        h = np.arange(4 * g, 4 * g + 4)[:, None] * (a + b)
        parts += [(h + np.arange(a)[None]).reshape(-1), (h + a + np.arange(b)[None]).reshape(-1)]
    return np.concatenate(parts)


def _inverse(perm, n):
    inv = np.full((n,), -1, np.int64)
    inv[perm[perm >= 0]] = np.nonzero(perm >= 0)[0]
    return inv


IN_PERM = _in_perm_index()
UQ_PERM = _head_perm_index(NOPE, ROPE)
UKV_PERM = _head_perm_index(NOPE, VH)


def _to_gathered(perm, shard, pad):
    return np.where(perm >= 0, (perm // shard) * pad + perm % shard, -1)


def _from_full(inv, shard, pad):
    j, i = np.divmod(np.arange(NDEV * pad), pad)
    return np.where(i < shard, inv[np.minimum(j * shard + i, inv.shape[0] - 1)], -1)


def col_gather(name, srcs, out_shapes, jobs, deps=()):
    ns, nj, nd, no = len(srcs), len(jobs), len(deps), len(out_shapes)
    tables = [jnp.asarray(np.asarray(job[5], np.int32)[None, :]) for job in jobs]

    def view(ref, col0, width, r0, rc):
        n = ref.shape[-1]
        if len(ref.shape) == 3:
            return ref.at[col0 // n, pl.ds(r0, rc), pl.ds(col0 % n, width)]
        return ref.at[pl.ds(r0, rc), pl.ds(col0, width)]

    def slabs(shape):
        if len(shape) == 3:
            return [((d,), d * shape[2], (d + 1) * shape[2]) for d in range(shape[0])]
        w = 1024 if shape[1] > 1024 and shape[1] % 1024 == 0 else shape[1]
        return [((slice(None), pl.ds(c, w)), c, c + w) for c in range(0, shape[1], w)]

    src_slabs = [slabs(s.shape) for s in srcs]
    out_slabs = [slabs(sh) for sh in out_shapes]
    work, first_use, last_touch = [], {}, {}
    for ji, (si, srow, oi, orow, nrows, tgt) in enumerate(jobs):
        tgt = np.asarray(tgt)
        tw = 256 if out_shapes[oi][-1] % 256 == 0 else LANE
        sw = 256 if srcs[si].shape[-1] % 256 == 0 else LANE
        for t in range(tgt.shape[0] // tw):
            tt = tgt[t * tw:(t + 1) * tw]
            tiles = sorted(set((tt[tt >= 0] // sw).tolist()))
            straight = bool(tiles) and tt[0] >= 0 and tt[0] % LANE == 0 and np.array_equal(tt, tt[0] + np.arange(tw))
            cols = [(int(tt[0]) + k * LANE, LANE) for k in range(tw // LANE)] if straight else [(s * sw, sw) for s in tiles]
            need = sorted({(si, k) for c0, _ in cols for k, (_, lo, hi) in enumerate(src_slabs[si]) if lo <= c0 < hi})
            touch = [(oi, k) for k, (_, lo, hi) in enumerate(out_slabs[oi]) if lo <= t * tw < hi][0]
            for key in need:
                first_use.setdefault(key, len(work))
            last_touch[touch] = len(work)
            work.append((ji, t, tw, sw, tiles, straight, need, touch))
    in_order = sorted(first_use, key=first_use.get)
    in_sem = {key: i for i, key in enumerate(in_order)}
    out_keys = sorted(last_touch)
    out_sem = {key: i for i, key in enumerate(out_keys)}

    def body(*refs):
        src_hbm, tab_refs = refs[:ns], refs[ns:ns + nj]
        out_hbm = refs[ns + nj + nd:ns + nj + nd + no]
        scratch = refs[ns + nj + nd + no:]
        src_refs, out_refs, in_sems, out_sems = scratch[:ns], scratch[ns:ns + no], scratch[-2], scratch[-1]
        loads = {}
        for key in in_order:
            si, k = key
            idx = src_slabs[si][k][0]
            loads[key] = pltpu.make_async_copy(src_hbm[si].at[idx], src_refs[si].at[idx], in_sems.at[in_sem[key]])
            loads[key].start()
        arrived, stores = set(), []
        for wi, (ji, t, tw, sw, tiles, straight, need, touch) in enumerate(work):
            si, srow, oi, orow, nrows, tgt = jobs[ji]
            sref, oref = src_refs[si], out_refs[oi]
            rc = nrows if nrows <= 1024 else 1024
            for key in need:
                if key not in arrived:
                    loads[key].wait()
                    arrived.add(key)
            onehots = []
            if tiles and not straight:
                want = tab_refs[ji][:, t * tw:(t + 1) * tw]
                row = lax.broadcasted_iota(jnp.int32, (sw, tw), 0)
                onehots = [jnp.where(want == row + s * sw, 1.0, 0.0).astype(BF) for s in tiles]
            first = int(np.asarray(tgt)[t * tw])

            def chunk(ci, _, t=t, tw=tw, sw=sw, tiles=tiles, straight=straight, onehots=onehots, first=first,
                      sref=sref, oref=oref, srow=srow, orow=orow, rc=rc):
                r0 = ci * rc
                ro = pl.multiple_of(orow + r0, LANE)
                rs = pl.multiple_of(srow + r0, LANE)
                if not tiles:
                    view(oref, t * tw, tw, ro, rc)[...] = jnp.zeros((rc, tw), BF)
                elif straight:
                    for k in range(tw // LANE):
                        view(oref, t * tw + k * LANE, LANE, ro, rc)[...] = view(sref, first + k * LANE, LANE, rs, rc)[...]
                else:
                    acc = None
                    for s, oh in zip(tiles, onehots):
                        p = jnp.dot(view(sref, s * sw, sw, rs, rc)[...], oh, preferred_element_type=F32)
                        acc = p if acc is None else acc + p
                    view(oref, t * tw, tw, ro, rc)[...] = acc.astype(BF)
                return 0

            lax.fori_loop(0, nrows // rc, chunk, 0)
            if last_touch[touch] == wi:
                idx = out_slabs[touch[0]][touch[1]][0]
                cp = pltpu.make_async_copy(out_refs[touch[0]].at[idx], out_hbm[touch[0]].at[idx], out_sems.at[out_sem[touch]])
                cp.start()
                stores.append(cp)
        for cp in stores:
            cp.wait()

    return pl.pallas_call(
        body, name=name, in_specs=[ANY_SPEC] * ns + [pl.BlockSpec(memory_space=pltpu.VMEM)] * nj + [ANY_SPEC] * nd,
        out_specs=[ANY_SPEC] * no, out_shape=[jax.ShapeDtypeStruct(s, BF) for s in out_shapes],
        scratch_shapes=[pltpu.VMEM(s.shape, BF) for s in srcs] + [pltpu.VMEM(s, BF) for s in out_shapes]
        + [pltpu.SemaphoreType.DMA((len(in_order),)), pltpu.SemaphoreType.DMA((len(out_keys),))],
        compiler_params=_cparams(),
    )(*srcs, *tables, *deps)


def sum_adamw(name, recvs, w, m, v, lo=0, prev=None, row0=0):
    _, R, C = w.shape
    L = len(recvs)
    CP = recvs[0].shape[-1]
    tr = _pick_rows(R, 16, 128)
    n_prev = 0 if prev is None else 4

    def body(*refs):
        r_refs = refs[:L]
        w_ref, m_ref, v_ref = refs[L:L + 3]
        g_ref, d_ref, nm_ref, nv_ref, gsum = refs[L + 3 + n_prev:]
        layer = pl.program_id(0)
        for k in range(L):
            def total(k=k):
                acc = r_refs[k][0].astype(F32)
                for d in range(1, NDEV):
                    acc = acc + r_refs[k][d].astype(F32)
                gsum[...] = acc
            pl.when(layer == k)(total)
        gg = gsum[:, 0:C]
        nm = B1 * m_ref[...] + (1.0 - B1) * gg
        nv = B2 * v_ref[...] + (1.0 - B2) * jnp.square(gg)
        m_hat = nm / (1.0 - B1 ** STEP)
        v_hat = nv / (1.0 - B2 ** STEP)
        g_ref[...] = gg
        d_ref[...] = -LR * (m_hat / (jnp.sqrt(v_hat) + EPS) + WD * w_ref[...])
        nm_ref[...] = nm
        nv_ref[...] = nv

    assert row0 % tr == 0
    r_specs = [pl.BlockSpec((NDEV, tr, CP),
                            functools.partial(lambda l, i, k: (0, row0 // tr + jnp.where(l == k, i, 0), 0), k=k))
               for k in range(L)]
    blk = pl.BlockSpec((None, tr, C), lambda l, i: (l + lo, i, 0))
    shp = jax.ShapeDtypeStruct(w.shape, F32)
    return pl.pallas_call(
        body, name=name, grid=(L, R // tr), in_specs=r_specs + [blk] * 3 + [ANY_SPEC] * n_prev, out_specs=[blk] * 4,
        out_shape=[shp] * 4, input_output_aliases={L + 3 + i: i for i in range(n_prev)},
        scratch_shapes=[pltpu.VMEM((tr, CP), F32)], compiler_params=_cparams(),
    )(*recvs, w, m, v, *(prev or ()))


ALPHA = 8.0 ** 0.25
T_WIDE, T_NARROW = 512, 1024


def _rope_fn(sign):
    def fn(x, cos, sin):
        W = x.shape[-1]
        lane = lax.broadcasted_iota(jnp.int32, x.shape, 1)
        first_half = (lane % ROPE) < (ROPE // 2)
        rot = jnp.where(first_half, -pltpu.roll(x, W - ROPE // 2, 1), pltpu.roll(x, ROPE // 2, 1))
        return x * cos + sign * rot * sin
    return fn


def _modulate(xv, a):
    return xv * (1.0 + a[1:2, :]) + a[0:1, :]


def layer_fwd(x, ada3, W, tabs, S, u=None, ada_next=None):
    cos, sin = tabs
    T = T_NARROW
    if u is None:
        u = rowwise("modulate", _modulate, S, T, [(x, D_MODEL, 0)], [ada3], [(D_MODEL, BF)])[0]
    proj = mm(u, W["in"], name="mm_proj", tm=1024, tn=1024, out_dtype=BF)
    W = {**W, **W["late"](proj)}

    ca = conv_fwd("conv_a_fwd", proj, OFF_A, W["conv_a"], 31, "glu", S, CW)

    def a_post(c, ag, vec):
        n, _ = _ln_stats(c + vec[0:1, :])
        return _silu(n * vec[1:2, :] + vec[2:3, :]) * _silu(ag)

    h_a = rowwise("mix_a_post", a_post, S, T, [(ca, CW, 0), (proj, CW, OFF_AG)], [W["vec_a"]], [(CW, BF)])[0]
    y_a = mm(h_a, W["a_out"], name="mm_branch_out", out_dtype=BF)

    cb = conv_fwd("conv_b_fwd", proj, OFF_B, W["conv_b"], 3, "mul", S, CW)
    h_b = rowwise("mix_b_post", lambda c, gb, bg: gb * c * _silu(bg), S, T,
                  [(cb, CW, 0), (proj, CW, OFF_GB), (proj, CW, OFF_BG)], [], [(CW, BF)])[0]
    y_b = mm(h_b, W["b_out"], name="mm_branch_out", out_dtype=BF)

    def rms2(ql, kvl, gq, gkv):
        rq = lax.rsqrt(jnp.mean(ql * ql, axis=-1, keepdims=True) + RMS_EPS)
        rk = lax.rsqrt(jnp.mean(kvl * kvl, axis=-1, keepdims=True) + RMS_EPS)
        return ql * rq * gq, kvl * rk * gkv

    qn, kvn = rowwise("rms_fwd", rms2, S, T, [(proj, QL, OFF_Q), (proj, KVL, OFF_KV)], [W["gq"], W["gkv"]],
                      [(QL, BF), (KVL, BF)])
    q = mm(qn, W["uq"], name="mm_q")
    kv = mm(kvn, W["ukv"], name="mm_kv", out_dtype=BF)
    rope = _rope_fn(1.0)

    def rope_fwd(qv, kr, c1, s1):
        parts = []
        for g in range(QUADS):
            parts.append(qv[:, g * QW:g * QW + 2 * LANE].astype(BF))
            parts.append(rope(qv[:, g * QW + 2 * LANE:(g + 1) * QW], c1, s1).astype(BF))
        kp = rope(kr, c1, s1)
        kp = kp + pltpu.roll(kp, ROPE, 1) + pltpu.roll(kp, 2 * ROPE, 1) + pltpu.roll(kp, 3 * ROPE, 1)
        return jnp.concatenate(parts, axis=1), kp

    q_b, kpe = rowwise("rope_fwd", rope_fwd, S, T,
                       [(q, HEADS * (NOPE + ROPE), 0), (proj, LANE, OFF_KR), (cos, LANE, 0), (sin, LANE, 0)], [],
                       [(HEADS * (NOPE + ROPE), BF), (LANE, BF)])
    o, lse = attn_fwd(q_b, kv, kpe, S)
    h_c = rowwise("mix_c_post", lambda ov, cg: ov * _silu(cg), S, T, [(o, CW, 0), (proj, CW, OFF_CG)], [],
                  [(CW, BF)])[0]
    y_c = mm(h_c, W["c_out"], name="mm_branch_out", out_dtype=BF)

    def merge(la, lb, lc, ya, yb, yc):
        return _sigmoid(la) * ya + _sigmoid(lb) * yb + _sigmoid(lc) * yc

    m = rowwise("merge_fwd", merge, S, T_WIDE,
                [(proj, D_MODEL, 0), (proj, D_MODEL, 1024), (proj, D_MODEL, 2048), (y_a, D_MODEL, 0),
                 (y_b, D_MODEL, 0), (y_c, D_MODEL, 0)], [], [(D_MODEL, BF)])[0]
    out = mm(m, W["o"], name="mm_out")

    def ln_fwd(xv, ov, a, lnv, *nxt):
        n, _ = _ln_stats(ALPHA * xv + a[2:3, :] * ov)
        y = n * lnv[0:1, :] + lnv[1:2, :]
        return (y, _modulate(y, nxt[0])) if nxt else y

    res = rowwise("ln_fwd", ln_fwd, S, T_WIDE, [(x, D_MODEL, 0), (out, D_MODEL, 0)],
                  [ada3, W["lnv"]] + ([] if ada_next is None else [ada_next]),
                  [(D_MODEL, F32)] + ([] if ada_next is None else [(D_MODEL, BF)]))
    saved = dict(x=x, u=u, proj=proj, ca=ca, cb=cb, h_a=h_a, h_b=h_b, h_c=h_c, y_a=y_a, y_b=y_b, y_c=y_c, qn=qn,
                 kvn=kvn, q_b=q_b, kv=kv, kpe=kpe, lse=lse, o=o, m=m, out=out)
    return res[0], saved, W, (res[1] if ada_next is not None else None)


def layer_bwd(dxn, sv, ada3, W, tabs, S, before_in=None):
    cos, sin = tabs
    T = T_NARROW
    x, proj = sv["x"], sv["proj"]
    G = {}

    def ln_bwd(xv, ov, dy, a, lnv):
        gate = a[2:3, :]
        n, rstd = _ln_stats(ALPHA * xv + gate * ov)
        dr = _ln_bwd(dy * lnv[0:1, :], n, rstd)
        return ALPHA * dr, gate * dr, _colsum(dy * n), _colsum(dy), _colsum(dr * ov)

    dres, d_out, G["ln_g"], G["ln_b"], d_gate = rowwise(
        "ln_bwd", ln_bwd, S, T_WIDE, [(x, D_MODEL, 0), (sv["out"], D_MODEL, 0), (dxn, D_MODEL, 0)], [ada3, W["lnv"]],
        [(D_MODEL, F32), (D_MODEL, BF)], [D_MODEL] * 3)
    dm = mm(d_out, W["o"], name="mm_dm", trans_b=True, out_dtype=BF)
    G["w_o"] = mm(sv["m"], d_out, name="mm_gw_o", trans_a=True, out_dtype=BF)

    def merge_bwd(dmv, la, lb, lc, ya, yb, yc):
        outs, dls = [], []
        for lg, yv in ((la, ya), (lb, yb), (lc, yc)):
            s = _sigmoid(lg)
            outs.append(dmv * s)
            dls.append((dmv * yv * s * (1.0 - s)).astype(BF))
        return (jnp.concatenate(dls, axis=1),) + tuple(outs)

    d_proj, dy_a, dy_b, dy_c = rowwise(
        "merge_bwd", merge_bwd, S, T_WIDE,
        [(dm, D_MODEL, 0), (proj, D_MODEL, 0), (proj, D_MODEL, 1024), (proj, D_MODEL, 2048), (sv["y_a"], D_MODEL, 0),
         (sv["y_b"], D_MODEL, 0), (sv["y_c"], D_MODEL, 0)], [], [(3 * D_MODEL, BF)] + [(D_MODEL, BF)] * 3,
        into=(None, NP, OFF_M))

    dh = {}
    for br, dy in (("a", dy_a), ("b", dy_b), ("c", dy_c)):
        dh[br] = mm(dy, W[br + "_out"], name="mm_dh", trans_b=True, out_dtype=BF)
        G["w_%s_out" % br] = mm(sv["h_" + br], dy, name="mm_gw_branch", trans_a=True, out_dtype=BF)

    def a_post_bwd(c, ag, dhv, vec):
        n, rstd = _ln_stats(c + vec[0:1, :])
        z = n * vec[1:2, :] + vec[2:3, :]
        d_ag = dhv * _silu(z) * _dsilu(ag)
        dz = dhv * _silu(ag) * _dsilu(z)
        dc = _ln_bwd(dz * vec[1:2, :], n, rstd)
        return d_ag, dc, _colsum(dc), _colsum(dz * n), _colsum(dz)

    d_proj, dca, G["conv_a_b"], G["ln_a_g"], G["ln_a_b"] = rowwise(
        "mix_a_post_bwd", a_post_bwd, S, T, [(sv["ca"], CW, 0), (proj, CW, OFF_AG), (dh["a"], CW, 0)], [W["vec_a"]],
        [(CW, BF), (CW, F32)], [CW] * 3, into=(d_proj, NP, OFF_AG))
    d_proj, G["conv_a_w"] = conv_bwd("conv_a_bwd", proj, OFF_A, dca, W["conv_a"], 31, "glu", S, CW, d_proj)

    def b_post_bwd(c, gb, bg, dhv):
        sg = _silu(bg)
        d_gb_bg = jnp.concatenate([(dhv * sg * c).astype(BF), (dhv * gb * c * _dsilu(bg)).astype(BF)], axis=1)
        return d_gb_bg, dhv * sg * gb

    d_proj, dcb = rowwise("mix_b_post_bwd", b_post_bwd, S, T,
                          [(sv["cb"], CW, 0), (proj, CW, OFF_GB), (proj, CW, OFF_BG), (dh["b"], CW, 0)], [],
                          [(2 * CW, BF), (CW, F32)], into=(d_proj, NP, OFF_GB))
    d_proj, G["conv_b_w"] = conv_bwd("conv_b_bwd", proj, OFF_B, dcb, W["conv_b"], 3, "mul", S, CW, d_proj)

    d_proj, d_o = rowwise("mix_c_post_bwd", lambda ov, cg, dhv: (dhv * ov * _dsilu(cg), dhv * _silu(cg)), S, T,
                          [(sv["o"], CW, 0), (proj, CW, OFF_CG), (dh["c"], CW, 0)], [], [(CW, BF), (CW, F32)],
                          into=(d_proj, NP, OFF_CG))
    dq, d_kv, dkp_heads = attn_bwd(sv["q_b"], sv["kv"], sv["kpe"], sv["o"], sv["lse"], d_o, S)
    ropeT = _rope_fn(-1.0)

    def rope_bwd(dqv, dkp, c1, s1):
        parts = []
        for g in range(QUADS):
            parts.append(dqv[:, g * QW:g * QW + 2 * LANE].astype(BF))
            parts.append(ropeT(dqv[:, g * QW + 2 * LANE:(g + 1) * QW], c1, s1).astype(BF))
        f = dkp[:, :LANE] + dkp[:, LANE:]
        f = f + pltpu.roll(f, 64, 1)
        f = f + pltpu.roll(f, 32, 1)
        lane = lax.broadcasted_iota(jnp.int32, f.shape, 1)
        return jnp.concatenate(parts, axis=1), jnp.where(lane < ROPE, ropeT(f, c1, s1), 0.0)

    d_q, dk_pe = rowwise("rope_bwd", rope_bwd, S, T,
                         [(dq, HEADS * (NOPE + ROPE), 0), (dkp_heads, HEADS * ROPE, 0), (cos, LANE, 0), (sin, LANE, 0)],
                         [], [(HEADS * (NOPE + ROPE), BF), (LANE, BF)])
    d_qn = mm(d_q, W["uq"], name="mm_dqn", trans_b=True, out_dtype=BF)
    d_kvn = mm(d_kv, W["ukv"], name="mm_dkvn", trans_b=True, out_dtype=BF)
    G["w_uq"] = mm(sv["qn"], d_q, name="mm_gw_uq", trans_a=True, out_dtype=BF)
    G["w_ukv"] = mm(sv["kvn"], d_kv, name="mm_gw_ukv", trans_a=True, out_dtype=BF)

    def rms_bwd(ql, kvl, dqn, dkn, dkp, gq, gkv):
        res = []
        for xv, dy, g in ((ql, dqn, gq), (kvl, dkn, gkv)):
            r = lax.rsqrt(jnp.mean(xv * xv, axis=-1, keepdims=True) + RMS_EPS)
            dxh = dy * g
            res.append(((r * (dxh - xv * (r * r) * jnp.mean(dxh * xv, axis=-1, keepdims=True))).astype(BF),
                        _colsum(dy * xv * r)))
        pad = jnp.zeros((ql.shape[0], LANE), BF)
        return jnp.concatenate([res[1][0], dkp, pad, res[0][0], pad], axis=1), res[0][1], res[1][1]

    d_proj, G["q_norm_g"], G["kv_norm_g"] = rowwise(
        "rms_bwd", rms_bwd, S, T,
        [(proj, QL, OFF_Q), (proj, KVL, OFF_KV), (d_qn, QL, 0), (d_kvn, KVL, 0), (dk_pe, LANE, 0)],
        [W["gq"], W["gkv"]], [(NP - OFF_KV, BF)], [QL, KVL], into=(d_proj, NP, OFF_KV))
    deps = before_in(G) if before_in is not None else ()
    du = mm(d_proj, W["in"], name="mm_du", trans_b=True, tm=1024, tk=2048, deps=deps)
    G["w_in"] = mm(sv["u"], d_proj, name="mm_gw_in", trans_a=True, out_dtype=BF, tm=1024, tk=2048, deps=deps)

    def mod_bwd(duv, xv, dr, a):
        return duv * (1.0 + a[1:2, :]) + dr, _colsum(duv), _colsum(duv * xv)

    dx, d_shift, d_scale = rowwise("mod_bwd", mod_bwd, S, T_WIDE, [(du, D_MODEL, 0), (x, D_MODEL, 0), (dres, D_MODEL, 0)],
                                   [ada3], [(D_MODEL, F32)], [D_MODEL] * 2)
    d_ada = jnp.concatenate([d_shift, d_scale, d_gate], axis=1)
    return dx, G, d_ada


SMALL = ("conv_a_b", "ln_a_g", "ln_a_b", "q_norm_g", "kv_norm_g", "ln_g", "ln_b")


def _rows(v):
    n = v.shape[0]
    r = -(-n // (LANE * 16)) * 16
    return jnp.pad(v, (0, r * LANE - n)).reshape(r, LANE)


def kernel(x, c, positions, w_ada, b_ada, w_in, conv_a_w, conv_a_b, ln_a_g, ln_a_b, w_a_out, conv_b_w, w_b_out, q_norm_g, kv_norm_g, w_uq, w_ukv, w_c_out, w_o, ln_g, ln_b, loss_target, m_w_ada, m_b_ada, m_w_in, m_conv_a_w, m_conv_a_b, m_ln_a_g, m_ln_a_b, m_w_a_out, m_conv_b_w, m_w_b_out, m_q_norm_g, m_kv_norm_g, m_w_uq, m_w_ukv, m_w_c_out, m_w_o, m_ln_g, m_ln_b, v_w_ada, v_b_ada, v_w_in, v_conv_a_w, v_conv_a_b, v_ln_a_g, v_ln_a_b, v_w_a_out, v_conv_b_w, v_w_b_out, v_q_norm_g, v_kv_norm_g, v_w_uq, v_w_ukv, v_w_c_out, v_w_o, v_ln_g, v_ln_b):
    P = dict(w_ada=w_ada, b_ada=b_ada, w_in=w_in, conv_a_w=conv_a_w, conv_a_b=conv_a_b, ln_a_g=ln_a_g, ln_a_b=ln_a_b,
             w_a_out=w_a_out, conv_b_w=conv_b_w, w_b_out=w_b_out, q_norm_g=q_norm_g, kv_norm_g=kv_norm_g, w_uq=w_uq,
             w_ukv=w_ukv, w_c_out=w_c_out, w_o=w_o, ln_g=ln_g, ln_b=ln_b)
    Mo = dict(w_ada=m_w_ada, b_ada=m_b_ada, w_in=m_w_in, conv_a_w=m_conv_a_w, conv_a_b=m_conv_a_b, ln_a_g=m_ln_a_g,
              ln_a_b=m_ln_a_b, w_a_out=m_w_a_out, conv_b_w=m_conv_b_w, w_b_out=m_w_b_out, q_norm_g=m_q_norm_g,
              kv_norm_g=m_kv_norm_g, w_uq=m_w_uq, w_ukv=m_w_ukv, w_c_out=m_w_c_out, w_o=m_w_o, ln_g=m_ln_g, ln_b=m_ln_b)
    Vo = dict(w_ada=v_w_ada, b_ada=v_b_ada, w_in=v_w_in, conv_a_w=v_conv_a_w, conv_a_b=v_conv_a_b, ln_a_g=v_ln_a_g,
              ln_a_b=v_ln_a_b, w_a_out=v_w_a_out, conv_b_w=v_conv_b_w, w_b_out=v_w_b_out, q_norm_g=v_q_norm_g,
              kv_norm_g=v_kv_norm_g, w_uq=v_w_uq, w_ukv=v_w_ukv, w_c_out=v_w_c_out, w_o=v_w_o, ln_g=v_ln_g, ln_b=v_ln_b)
    ORDER = ("w_ada", "b_ada", "w_in", "conv_a_w", "conv_a_b", "ln_a_g", "ln_a_b", "w_a_out", "conv_b_w", "w_b_out",
             "q_norm_g", "kv_norm_g", "w_uq", "w_ukv", "w_c_out", "w_o", "ln_g", "ln_b")
    L = w_ada.shape[0]
    S = x.shape[1]
    me = 4 * lax.axis_index("x") + 2 * lax.axis_index("y") + lax.axis_index("c")
    x2 = x[0]
    tgt = loss_target[0]

    small_in = _rows(jnp.concatenate([c.reshape(-1), conv_a_w.reshape(-1), conv_b_w.reshape(-1)]))
    w_in_b = jnp.pad(w_in.astype(BF), ((0, 0), (0, 0), (0, W_IN_PAD - IN_SHARD)))
    misc_b = jnp.concatenate([w_a_out, w_b_out, w_c_out, w_ukv, jnp.pad(w_uq, ((0, 0), (0, 0), (0, LANE - UQ_SHARD)))],
                             axis=1).astype(BF)
    w_o_b = w_o.astype(BF)
    gathered = [None] * L
    sg = exchange("gather_small", [small_in], [])[0]
    sgf = sg.reshape(NDEV, -1)
    c_all = sgf[:, :D_MODEL]
    o1 = D_MODEL + L * 31 * 64
    conv_a_full = sgf[:, D_MODEL:o1].reshape(NDEV, L, 31, 64).transpose(1, 2, 0, 3).reshape(L, 31, CW)
    conv_b_full = sgf[:, o1:o1 + L * 3 * 64].reshape(NDEV, L, 3, 64).transpose(1, 2, 0, 3).reshape(L, 3, CW)

    c_act = rowwise("silu_c", _silu, 16, 16, [(jnp.pad(c_all, ((0, 8), (0, 0))), D_MODEL, 0)], [], [(D_MODEL, BF)])[0]
    ncol = w_ada.shape[2]
    w_ada_b = w_ada.astype(BF).transpose(1, 0, 2).reshape(D_MODEL, L * ncol)
    b_mine = lax.dynamic_slice_in_dim(b_ada, me * ncol, ncol, axis=1).reshape(1, L * ncol)
    ada_part = mm(c_act, w_ada_b, name="mm_ada", bias=b_mine)
    ada_rows = -(-(L * ncol) // (LANE * 8)) * 8
    ada_send = jnp.pad(ada_part[:NDEV].reshape(NDEV, -1, LANE), ((0, 0), (0, ada_rows - L * ncol // LANE), (0, 0)))
    ada_recv = exchange("a2a_ada", [], [ada_send])[0]
    ada = ada_recv[:, :L * ncol // LANE].reshape(NDEV, L, ncol).transpose(1, 0, 2).reshape(L, 3, D_MODEL)
    gathered[0] = [gather_two_level("gather0_w_in", w_in_b[0], ada)]
    pending_rest, rest_token = exchange_begin("gather0_rest", [misc_b[0], w_o_b[0]], 2, gathered[0][0])

    inv_freq = ROPE_THETA ** (-jnp.arange(0, ROPE, 2, dtype=F32) / ROPE)
    ang = positions[0].astype(F32)[:, None] * inv_freq
    tabs = (jnp.tile(jnp.cos(ang), (1, 2 * LANE // ROPE)), jnp.tile(jnp.sin(ang), (1, 2 * LANE // ROPE)))

    straight = np.arange(D_MODEL)
    fwd_in = [(0, 0, 0, 0, D_MODEL, _to_gathered(IN_PERM, IN_SHARD, W_IN_PAD))]
    fwd_misc = [(0, ROW_A, 0, 0, CW, straight), (0, ROW_B, 1, 0, CW, straight), (0, ROW_C, 2, 0, CW, straight),
                (0, ROW_UKV, 3, 0, KVL, UKV_PERM), (0, ROW_UQ, 4, 0, QL, _to_gathered(UQ_PERM, UQ_SHARD, LANE))]
    rev_in = [(0, 0, 0, 0, D_MODEL, _from_full(_inverse(IN_PERM, D_IN), IN_SHARD, W_IN_PAD))]
    rev_misc = [(0, 0, 0, ROW_A, CW, straight), (1, 0, 0, ROW_B, CW, straight), (2, 0, 0, ROW_C, CW, straight),
                (3, 0, 0, ROW_UKV, KVL, _from_full(_inverse(UKV_PERM, HEADS * (NOPE + VH)), LANE, LANE)),
                (4, 0, 0, ROW_UQ, QL, _from_full(_inverse(UQ_PERM, HEADS * (NOPE + ROPE)), UQ_SHARD, LANE))]

    def layer_weights(l, deps):
        w_in_p = col_gather("relayout_w_in", [gathered[l][0]], [(D_MODEL, NP)], fwd_in, deps)[0]

        def late(after):
            if len(gathered[l]) == 1:
                gathered[l] += exchange_end(pending_rest, after)
            _, g_misc, g_o = gathered[l]
            a_out, b_out, c_out, ukv, uq = col_gather(
                "relayout_misc", [g_misc],
                [(CW, D_MODEL)] * 3 + [(KVL, HEADS * (NOPE + VH)), (QL, HEADS * (NOPE + ROPE))], fwd_misc, deps)
            return {"a_out": a_out, "b_out": b_out, "c_out": c_out, "uq": uq, "ukv": ukv,
                    "o": g_o.reshape(D_MODEL, D_MODEL)}

        return {
            "in": w_in_p, "late": late,
            "conv_a": jnp.pad(conv_a_full[l], ((0, 1), (0, 0))), "conv_b": jnp.pad(conv_b_full[l], ((0, 5), (0, 0))),
            "vec_a": jnp.stack([conv_a_b[l], ln_a_g[l], ln_a_b[l]]), "gq": q_norm_g[l][None], "gkv": kv_norm_g[l][None],
            "lnv": jnp.stack([ln_g[l], ln_b[l]]),
        }

    h = x2
    saved, weights = [], []
    handles, token, u_next = {}, rest_token, None
    for l in range(1, L):
        handles[l], token = exchange_begin("gather%d" % l, [w_in_b[l], misc_b[l], w_o_b[l]], 3, token)
    for l in range(L):
        ada_l, deps = (ada[l] + token[0, 0], (token,)) if l == 0 else (ada[l], ())
        h, sv, Wl, u_next = layer_fwd(h, ada_l, layer_weights(l, deps), tabs, S, u_next, ada[l + 1] if l + 1 < L else None)
        if l + 1 < L:
            gathered[l + 1] = exchange_end(handles[l + 1], h)
        saved.append(sv)
        weights.append(Wl)

    def loss_fn(y, t):
        e = y - t
        return e * (1.0 / D_MODEL), _colsum(e * e)

    dy, sq = rowwise("loss", loss_fn, S, 256, [(h, D_MODEL, 0), (tgt, D_MODEL, 0)], [], [(D_MODEL, F32)], [D_MODEL])
    loss = lax.psum(0.5 * jnp.sum(sq) / D_MODEL, ("x", "y", "c"))
    loss, dy = lax.optimization_barrier((loss, dy))

    grads, d_adas, recv = [None] * L, [None] * L, [None] * L
    pending, token = None, None

    def send_rest(g):
        send_misc = col_gather("unrelayout_misc", [g["w_a_out"], g["w_b_out"], g["w_c_out"], g["w_ukv"], g["w_uq"]],
                               [(NDEV, MISC_ROWS, LANE)], rev_misc)[0]
        return [send_misc, g["w_o"].reshape(NDEV, D_MODEL // NDEV, D_MODEL)]

    rest0 = []

    def early_rest(g):
        handle, tok = exchange_begin("scatter0_rest", send_rest(g), 0, g["w_o"])
        rest0.append(handle)
        return (tok,)

    for l in reversed(range(L)):
        ada_l = ada[l] if token is None else ada[l] + token[0, 0]
        dy, g, d_adas[l] = layer_bwd(dy, saved[l], ada_l, weights[l], tabs, S, early_rest if l == 0 else None)
        grads[l] = g
        if pending is not None:
            recv[l + 1] = exchange_end(pending, dy)
        send_in = col_gather("unrelayout_w_in", [g["w_in"]], [(NDEV, D_MODEL, W_IN_PAD)], rev_in)[0]
        if l == 0:
            def layer_vec(i):
                return jnp.concatenate([grads[i][n].reshape(-1) for n in SMALL] + [d_adas[i].reshape(-1)])

            def to_owners(name, taps):
                full = jnp.stack([grads[i][name][:taps] for i in range(L)])
                return full.reshape(L, taps, NDEV, CW // NDEV).transpose(2, 0, 1, 3).reshape(NDEV, -1)

            conv_send = jnp.concatenate([to_owners("conv_a_w", 31), to_owners("conv_b_w", 3)], axis=1)
            conv_rows = -(-conv_send.shape[1] // (LANE * 16)) * 16
            conv_send = jnp.pad(conv_send, ((0, 0), (0, conv_rows * LANE - conv_send.shape[1])))
            small_sizes = [int(grads[0][n].size) for n in SMALL] + [3 * D_MODEL]
            gsmall, conv_recv = exchange("gather_small_grads", [_rows(jnp.concatenate([layer_vec(i) for i in range(L)]))],
                                         [conv_send.reshape(NDEV, conv_rows, LANE)])
            pending, token = exchange_begin("scatter0", [send_in], 0, gsmall)
        else:
            pending, token = exchange_begin("scatter%d" % l, [send_in] + send_rest(g), 0,
                                            dy if l + 1 == L else recv[l + 1][0])
    grad_x = dy[None]

    gsmall = gsmall + token[0, 0]
    gsum = sum_slots("sum_small", gsmall).reshape(-1)
    recv[0] = [None] + exchange_end(rest0[0], gsum)
    Gr = {}
    offs = np.cumsum([0] + small_sizes)
    per_layer = int(offs[-1])
    gsum = gsum[:L * per_layer].reshape(L, per_layer)
    for i, n in enumerate(SMALL):
        Gr[n] = gsum[:, offs[i]:offs[i + 1]]
    csum = sum_slots("sum_conv", conv_recv + token[0, 0]).reshape(-1)
    n_a = L * 31 * (CW // NDEV)
    Gr["conv_a_w"] = csum[:n_a].reshape(L, 31, CW // NDEV)
    Gr["conv_b_w"] = csum[n_a:n_a + L * 3 * (CW // NDEV)].reshape(L, 3, CW // NDEV)
    Gr["b_ada"] = gsum[:, offs[7]:offs[8]]
    d_ada_all = gsmall.reshape(NDEV, -1)[:, :L * per_layer].reshape(NDEV, L, per_layer)[:, :, offs[7]:offs[8]]
    d_mine = lax.dynamic_slice_in_dim(d_ada_all, me * ncol, ncol, axis=2).reshape(NDEV, L * ncol)
    g_ada = mm(c_act, jnp.pad(d_mine, ((0, 8), (0, 0))).astype(BF), name="mm_gw_ada", trans_a=True)
    Gr["w_ada"] = g_ada.reshape(D_MODEL, L, ncol).transpose(1, 0, 2)

    D, NM, NV = {}, {}, {}
    D["w_ada"], NM["w_ada"], NV["w_ada"] = adamw("adamw_w_ada", P["w_ada"], Gr["w_ada"], Mo["w_ada"], Vo["w_ada"])
    Gr["w_o"], D["w_o"], NM["w_o"], NV["w_o"] = sum_adamw(
        "sum_adamw_w_o", [recv[l][2] for l in range(L)], P["w_o"], Mo["w_o"], Vo["w_o"])
    for n, row0 in (("w_a_out", ROW_A), ("w_b_out", ROW_B), ("w_c_out", ROW_C), ("w_ukv", ROW_UKV), ("w_uq", ROW_UQ)):
        Gr[n], D[n], NM[n], NV[n] = sum_adamw("sum_adamw_" + n, [recv[l][1] for l in range(L)], P[n], Mo[n], Vo[n],
                                              row0=row0)
    w_l, m_l, v_l, _ = lax.optimization_barrier((P["w_in"], Mo["w_in"], Vo["w_in"], token))
    upper = sum_adamw("sum_adamw_w_in_upper", [recv[l][0] for l in range(1, L)], w_l, m_l, v_l, lo=1)
    recv[0][0] = exchange_end(pending, upper[1])[0]
    Gr["w_in"], D["w_in"], NM["w_in"], NV["w_in"] = sum_adamw(
        "sum_adamw_w_in", [recv[0][0]], w_l, m_l, v_l, lo=0, prev=upper)
    packed =("b_ada", "conv_a_w", "conv_b_w") + SMALL
    pk = lambda T_: _rows(jnp.concatenate([T_[n].reshape(-1) for n in packed]))[None]
    dS, mS, vS = adamw("adamw_small", pk(P), pk(Gr), pk(Mo), pk(Vo))
    o = 0
    for n in packed:
        sz = int(np.prod(P[n].shape))
        D[n] = dS.reshape(-1)[o:o + sz].reshape(P[n].shape)
        NM[n] = mS.reshape(-1)[o:o + sz].reshape(P[n].shape)
        NV[n] = vS.reshape(-1)[o:o + sz].reshape(P[n].shape)
        o += sz
    return (loss, grad_x, *[Gr[n] for n in ORDER], *[D[n] for n in ORDER], *[NM[n] for n in ORDER],
            *[NV[n] for n in ORDER])
```

```python
import functools
import math

import numpy as np
import jax
import jax.numpy as jnp
from jax import lax
from jax.experimental import pallas as pl
from jax.experimental.pallas import tpu as pltpu

BF = jnp.bfloat16
F32 = jnp.float32
MESH = pl.DeviceIdType.MESH
NDEV = 8

HEADS, NOPE, ROPE, VH = 8, 64, 32, 64
HP = 128
ROPE_THETA = 10000.0
LN_EPS = 1e-5
RMS_EPS = 1e-6
LR, B1, B2, EPS, WD, STEP = 0.001, 0.9, 0.999, 1e-08, 0.01, 10

LANE = 128
VMEM_LIMIT = 56 * 1024 * 1024

D_MODEL, CW, QL, KVL = 1024, 512, 384, 256
OFF_M, OFF_A, OFF_AG, OFF_B, OFF_CG, OFF_GB, OFF_BG = 0, 3072, 4096, 4608, 5632, 6144, 6656
OFF_KV, OFF_KR, OFF_Q, NP = 7168, 7424, 7680, 8192
D_IN = 7840


def _cparams(**kw):
    return pltpu.CompilerParams(vmem_limit_bytes=VMEM_LIMIT, **kw)


def _sigmoid(x):
    return jax.nn.sigmoid(x)


def _silu(x):
    return x * _sigmoid(x)


def _dsilu(x):
    s = _sigmoid(x)
    return s * (1.0 + x * (1.0 - s))


def _pick_tile(n, cap, mult):
    if n <= cap:
        return n
    for t in range(cap - cap % mult, 0, -mult):
        if n % t == 0:
            return t
    raise ValueError((n, cap, mult))


def mm(a, b, *, name, trans_a=False, trans_b=False, out_dtype=F32, bias=None, tm=1024, tn=1024, tk=2048, deps=()):
    if trans_a:
        K, M = a.shape
    else:
        M, K = a.shape
    if trans_b:
        N, K2 = b.shape
    else:
        K2, N = b.shape
    assert K == K2 and not (trans_a and trans_b), (a.shape, b.shape)
    tm, tn = _pick_tile(M, tm, 16), _pick_tile(N, tn, LANE)
    tk = _pick_tile(K, tk, LANE if trans_b else 16)
    assert M % tm == 0 and N % tn == 0 and K % tk == 0, (M, N, K, tm, tn, tk)
    nk = K // tk
    dims = (((0 if trans_a else 1,), (1 if trans_b else 0,)), ((), ()))
    has_bias = bias is not None

    def body(*refs):
        a_ref, b_ref = refs[0], refs[1]
        bias_ref = refs[2] if has_bias else None
        o_ref = refs[(3 if has_bias else 2) + len(deps)]
        p = lax.dot_general(a_ref[...], b_ref[...], dims, preferred_element_type=F32)

        def finish(v):
            if has_bias:
                v = v + bias_ref[...]
            o_ref[...] = v.astype(o_ref.dtype)

        if nk == 1:
            finish(p)
        else:
            acc = refs[-1]
            k = pl.program_id(2)

            @pl.when(k == 0)
            def _():
                acc[...] = p

            @pl.when(k > 0)
            def _():
                acc[...] += p

            @pl.when(k == nk - 1)
            def _():
                finish(acc[...])

    if trans_a:
        a_spec = pl.BlockSpec((tk, tm), lambda i, j, k: (k, i))
    else:
        a_spec = pl.BlockSpec((tm, tk), lambda i, j, k: (i, k))
    if trans_b:
        b_spec = pl.BlockSpec((tn, tk), lambda i, j, k: (j, k))
    else:
        b_spec = pl.BlockSpec((tk, tn), lambda i, j, k: (k, j))
    in_specs = [a_spec, b_spec]
    args = [a, b]
    if has_bias:
        in_specs.append(pl.BlockSpec((1, tn), lambda i, j, k: (0, j)))
        args.append(bias)
    in_specs += [ANY_SPEC] * len(deps)
    args += list(deps)
    return pl.pallas_call(
        body, name=name, grid=(M // tm, N // tn, nk),
        in_specs=in_specs, out_specs=pl.BlockSpec((tm, tn), lambda i, j, k: (i, j)),
        out_shape=jax.ShapeDtypeStruct((M, N), out_dtype),
        scratch_shapes=[pltpu.VMEM((tm, tn), F32)] if nk > 1 else [],
        compiler_params=_cparams(),
    )(*args)


def rowwise(name, fn, S, T, row_ins, full_ins, row_outs, acc_outs=(), into=None):
    n_in = len(row_ins) + len(full_ins)
    n_ro, n_ao = len(row_outs), len(acc_outs)
    alias = into is not None and into[0] is not None
    T = min(T, S)

    def body(*refs):
        vals = [r[...] for r in refs[:n_in]]
        vals = [v.astype(F32) if v.dtype == BF else v for v in vals]
        outs = fn(*vals)
        if not isinstance(outs, (tuple, list)):
            outs = (outs,)
        assert len(outs) == n_ro + n_ao, (name, len(outs))
        o0 = n_in + (1 if alias else 0)
        for r, v in zip(refs[o0:o0 + n_ro], outs[:n_ro]):
            r[...] = v.astype(r.dtype)
        first = pl.program_id(0) == 0
        for r, v in zip(refs[o0 + n_ro:], outs[n_ro:]):
            def init(r=r, v=v):
                r[...] = v

            def accum(r=r, v=v):
                r[...] += v

            pl.when(first)(init)
            pl.when(jnp.logical_not(first))(accum)

    in_specs, args = [], []
    for arr, W, off in row_ins:
        assert off % W == 0 and arr.shape[0] == S, (name, arr.shape, W, off)
        in_specs.append(pl.BlockSpec((T, W), functools.partial(lambda i, cb: (i, cb), cb=off // W)))
        args.append(arr)
    for arr in full_ins:
        in_specs.append(pl.BlockSpec(arr.shape, lambda i: (0, 0)))
        args.append(arr)
    out_specs = [pl.BlockSpec((T, W), lambda i: (i, 0)) for W, _ in row_outs]
    out_shape = [jax.ShapeDtypeStruct((S, W), dt) for W, dt in row_outs]
    aliases = {}
    if into is not None:
        buf, total, off = into
        W0, dt0 = row_outs[0]
        assert off % W0 == 0
        out_specs[0] = pl.BlockSpec((T, W0), functools.partial(lambda i, cb: (i, cb), cb=off // W0))
        out_shape[0] = jax.ShapeDtypeStruct((S, total), dt0)
        if alias:
            in_specs.append(ANY_SPEC)
            args.append(buf)
            aliases = {n_in: 0}
    out_specs += [pl.BlockSpec((1, W), lambda i: (0, 0)) for W in acc_outs]
    out_shape += [jax.ShapeDtypeStruct((1, W), F32) for W in acc_outs]
    return pl.pallas_call(
        body, name=name, grid=(S // T,), in_specs=in_specs, out_specs=out_specs, out_shape=out_shape,
        input_output_aliases=aliases, compiler_params=_cparams(),
    )(*args)


def _colsum(v):
    return jnp.sum(v, axis=0, keepdims=True)


def _ln_stats(r):
    mu = jnp.mean(r, axis=-1, keepdims=True)
    d = r - mu
    var = jnp.mean(d * d, axis=-1, keepdims=True)
    rstd = lax.rsqrt(var + LN_EPS)
    return d * rstd, rstd


def _ln_bwd(dn, n, rstd):
    return rstd * (dn - jnp.mean(dn, axis=-1, keepdims=True) - n * jnp.mean(dn * n, axis=-1, keepdims=True))


CPAD = 32
TC = 64


def _pre(mode, x1, x2):
    return x1 * _sigmoid(x2) if mode == "glu" else x1 * x2


def _shifted(ext, sft):
    n = TC + CPAD
    return pltpu.roll(ext, (n - sft) % n, 0)[0:TC]


def _interleaved_specs(S, off):
    return [pl.BlockSpec((S, LANE), functools.partial(lambda j, o: (0, o + 2 * j), o=off // LANE)),
            pl.BlockSpec((S, LANE), functools.partial(lambda j, o: (0, o + 2 * j + 1), o=off // LANE))]


def conv_fwd(name, src, off, w_pad, taps, mode, S, C):
    nchunk = S // TC

    def body(x1_ref, x2_ref, w_ref, o_ref, a_pad):
        a_pad[0:CPAD, :] = jnp.zeros((CPAD, LANE), F32)

        def fill(i, _):
            r = pl.multiple_of(i * 256, 256)
            a_pad[pl.ds(CPAD + r, 256), :] = _pre(mode, x1_ref[pl.ds(r, 256), :].astype(F32),
                                                  x2_ref[pl.ds(r, 256), :].astype(F32))
            return 0

        lax.fori_loop(0, S // 256, fill, 0)

        def chunk(i, _):
            base = pl.multiple_of(i * TC, TC)
            ext = a_pad[pl.ds(base, TC + CPAD), :]
            acc = jnp.zeros((TC, LANE), F32)
            for k in range(taps):
                acc = acc + w_ref[pl.ds(k, 1), :] * _shifted(ext, CPAD - (taps - 1) + k)
            o_ref[pl.ds(base, TC), :] = acc
            return 0

        lax.fori_loop(0, nchunk, chunk, 0)

    kp = w_pad.shape[0]
    return pl.pallas_call(
        body, name=name, grid=(C // LANE,),
        in_specs=_interleaved_specs(S, off) + [pl.BlockSpec((kp, LANE), lambda j: (0, j))],
        out_specs=pl.BlockSpec((S, LANE), lambda j: (0, j)),
        out_shape=jax.ShapeDtypeStruct((S, C), F32),
        scratch_shapes=[pltpu.VMEM((S + CPAD, LANE), F32)],
        compiler_params=_cparams(),
    )(src, src, w_pad)


def conv_bwd(name, src, off, dc, w_pad, taps, mode, S, C, buf):
    nchunk = S // TC
    kp = w_pad.shape[0]

    def body(x1_ref, x2_ref, dc_ref, w_ref, _, d_ref, dw_ref, a_pad, dc_pad, dw_acc):
        a_pad[0:CPAD, :] = jnp.zeros((CPAD, LANE), F32)
        dc_pad[S:S + CPAD, :] = jnp.zeros((CPAD, LANE), F32)
        dw_acc[...] = jnp.zeros(dw_acc.shape, F32)

        def fill(i, _):
            r = pl.multiple_of(i * 256, 256)
            a_pad[pl.ds(CPAD + r, 256), :] = _pre(mode, x1_ref[pl.ds(r, 256), :].astype(F32),
                                                  x2_ref[pl.ds(r, 256), :].astype(F32))
            dc_pad[pl.ds(r, 256), :] = dc_ref[pl.ds(r, 256), :]
            return 0

        lax.fori_loop(0, S // 256, fill, 0)

        def chunk(i, _):
            base = pl.multiple_of(i * TC, TC)
            ext_d = dc_pad[pl.ds(base, TC + CPAD), :]
            ext_a = a_pad[pl.ds(base, TC + CPAD), :]
            dcv = ext_d[0:TC]
            da = jnp.zeros((TC, LANE), F32)
            for k in range(taps):
                da = da + w_ref[pl.ds(k, 1), :] * _shifted(ext_d, taps - 1 - k)
                prod = dcv * _shifted(ext_a, CPAD - (taps - 1) + k)
                fold = prod[0:8]
                for g in range(1, TC // 8):
                    fold = fold + prod[8 * g:8 * g + 8]
                dw_acc[pl.ds(8 * k, 8), :] += fold
            x1 = x1_ref[pl.ds(base, TC), :].astype(F32)
            x2 = x2_ref[pl.ds(base, TC), :].astype(F32)
            if mode == "glu":
                s = _sigmoid(x2)
                d1, d2 = da * s, da * x1 * s * (1.0 - s)
            else:
                d1, d2 = da * x2, da * x1
            d_ref[pl.ds(base, TC), 0:LANE] = d1.astype(BF)
            d_ref[pl.ds(base, TC), LANE:2 * LANE] = d2.astype(BF)
            return 0

        lax.fori_loop(0, nchunk, chunk, 0)
        dw_ref[...] = jnp.zeros(dw_ref.shape, F32)
        for k in range(taps):
            dw_ref[pl.ds(k, 1), :] = jnp.sum(dw_acc[pl.ds(8 * k, 8), :], axis=0, keepdims=True)

    blk = pl.BlockSpec((S, LANE), lambda j: (0, j))
    return pl.pallas_call(
        body, name=name, grid=(C // LANE,),
        in_specs=_interleaved_specs(S, off) + [blk, pl.BlockSpec((kp, LANE), lambda j: (0, j)), ANY_SPEC],
        out_specs=[pl.BlockSpec((S, 2 * LANE), functools.partial(lambda j, o: (0, o + j), o=off // (2 * LANE))),
                   pl.BlockSpec((kp, LANE), lambda j: (0, j))],
        out_shape=[jax.ShapeDtypeStruct(buf.shape, BF), jax.ShapeDtypeStruct((kp, C), F32)],
        input_output_aliases={4: 0},
        scratch_shapes=[pltpu.VMEM((S + CPAD, LANE), F32), pltpu.VMEM((S + CPAD, LANE), F32),
                        pltpu.VMEM((8 * kp, LANE), F32)],
        compiler_params=_cparams(),
    )(src, src, dc, w_pad, buf)


FWD_TILES = (512, 512)
BWD_TILES = (512, 512)
QUADS = HEADS // 4
QW, KVW = 4 * (NOPE + ROPE), 4 * (NOPE + VH)
SCALE = (NOPE + ROPE) ** -0.5
NT_DIMS = (((1,), (1,)), ((), ()))
TN_DIMS = (((0,), (0,)), ((), ()))


def _lane_mask(width, group, dtype):
    lane = lax.broadcasted_iota(jnp.int32, (1, LANE), 1)
    return jnp.where(lane // width == group, 1.0, 0.0).astype(dtype)


def _visible(tq, tk, off):
    row = lax.broadcasted_iota(jnp.int32, (tq, tk), 0)
    col = lax.broadcasted_iota(jnp.int32, (tq, tk), 1)
    return col <= row + off


def _attn_tiles(S, tq, tk):
    tk = tk if S % tk == 0 else 256
    return min(tq, tk), tk


def attn_fwd(q, kv, kpe, S):
    tq, tk = _attn_tiles(S, *FWD_TILES)
    nq = S // tq

    def body(q_ref, kv_ref, kp_ref, o_ref, lse_ref):
        for t in range(2):
            cols = slice(t * LANE, (t + 1) * LANE)
            for hh in range(2):
                def q_block(qi, _, t=t, hh=hh, cols=cols):
                    r0 = pl.multiple_of(qi * tq, tq)
                    qcat = jnp.concatenate([q_ref[pl.ds(r0, tq), cols] * _lane_mask(NOPE, hh, BF),
                                            q_ref[pl.ds(r0, tq), 2 * LANE:3 * LANE] * _lane_mask(ROPE, 2 * t + hh, BF)],
                                           axis=1)
                    nfull = (qi * tq) // tk

                    def step(kj, carry, masked):
                        m, l, acc = carry
                        c0 = pl.multiple_of(kj * tk, tk)
                        kc = jnp.concatenate([kv_ref[pl.ds(c0, tk), cols], kp_ref[pl.ds(c0, tk), :]], axis=1)
                        vt = kv_ref[pl.ds(c0, tk), (2 + t) * LANE:(3 + t) * LANE]
                        s = lax.dot_general(qcat, kc, NT_DIMS, preferred_element_type=F32) * SCALE
                        if masked:
                            s = jnp.where(_visible(tq, tk, qi * tq - nfull * tk), s, -jnp.inf)
                        m_new = jnp.maximum(m, jnp.max(s, axis=-1, keepdims=True))
                        p = jnp.exp(s - m_new)
                        alpha = jnp.exp(m - m_new)
                        l = alpha * l + jnp.sum(p, axis=-1, keepdims=True)
                        acc = alpha * acc + jnp.dot(p.astype(BF), vt, preferred_element_type=F32)
                        return m_new, l, acc

                    init = (jnp.full((tq, 1), -jnp.inf, F32), jnp.zeros((tq, 1), F32), jnp.zeros((tq, LANE), F32))
                    carry = lax.fori_loop(0, nfull, lambda kj, c: step(kj, c, False), init)
                    m, l, acc = step(nfull, carry, True)
                    mine = _lane_mask(NOPE, hh, F32)
                    if hh == 0:
                        o_ref[pl.ds(r0, tq), cols] = (acc / l) * mine
                        lse_ref[pl.ds(r0, tq), cols] = (m + jnp.log(l)) * mine
                    else:
                        o_ref[pl.ds(r0, tq), cols] += (acc / l) * mine
                        lse_ref[pl.ds(r0, tq), cols] += (m + jnp.log(l)) * mine
                    return 0

                lax.fori_loop(0, nq, q_block, 0)

    return pl.pallas_call(
        body, name="attn_fwd", grid=(QUADS,),
        in_specs=[pl.BlockSpec((S, QW), lambda g: (0, g)), pl.BlockSpec((S, KVW), lambda g: (0, g)),
                  pl.BlockSpec((S, LANE), lambda g: (0, 0))],
        out_specs=[pl.BlockSpec((S, 2 * LANE), lambda g: (0, g))] * 2,
        out_shape=[jax.ShapeDtypeStruct((S, HEADS * VH), F32)] * 2,
        compiler_params=_cparams(),
    )(q, kv, kpe)


def attn_bwd(q, kv, kpe, o, lse, do, S):
    tq, tk = _attn_tiles(S, *BWD_TILES)
    nq = S // tq

    def body(q_ref, kv_ref, kp_ref, o_ref, lse_ref, do_ref, dq_ref, dkv_ref, dkp_ref, dq_acc, dk_acc, dv_acc):
        for t in range(2):
            cols = slice(t * LANE, (t + 1) * LANE)
            dk_acc[...] = jnp.zeros(dk_acc.shape, F32)
            dv_acc[...] = jnp.zeros(dv_acc.shape, F32)
            for hh in range(2):
                def q_block(qi, _, t=t, hh=hh, cols=cols):
                    r0 = pl.multiple_of(qi * tq, tq)
                    mine = _lane_mask(NOPE, hh, F32)
                    qcat = jnp.concatenate([q_ref[pl.ds(r0, tq), cols] * _lane_mask(NOPE, hh, BF),
                                            q_ref[pl.ds(r0, tq), 2 * LANE:3 * LANE] * _lane_mask(ROPE, 2 * t + hh, BF)],
                                           axis=1)
                    dof = do_ref[pl.ds(r0, tq), cols] * mine
                    dob = dof.astype(BF)
                    delta = jnp.sum(dof * o_ref[pl.ds(r0, tq), cols], axis=-1, keepdims=True)
                    lse_h = lse_ref[pl.ds(r0, tq), cols][:, hh * NOPE:hh * NOPE + 1]
                    nfull = (qi * tq) // tk
                    dq_acc[...] = jnp.zeros(dq_acc.shape, F32)

                    def step(kj, _, masked):
                        c0 = pl.multiple_of(kj * tk, tk)
                        kc = jnp.concatenate([kv_ref[pl.ds(c0, tk), cols], kp_ref[pl.ds(c0, tk), :]], axis=1)
                        vt = kv_ref[pl.ds(c0, tk), (2 + t) * LANE:(3 + t) * LANE]
                        s = lax.dot_general(qcat, kc, NT_DIMS, preferred_element_type=F32) * SCALE
                        if masked:
                            s = jnp.where(_visible(tq, tk, qi * tq - nfull * tk), s, -jnp.inf)
                        p = jnp.exp(s - lse_h)
                        dp = lax.dot_general(dob, vt, NT_DIMS, preferred_element_type=F32)
                        ds = (p * (dp - delta) * SCALE).astype(BF)
                        dv_acc[pl.ds(c0, tk), :] += lax.dot_general(p.astype(BF), dob, TN_DIMS,
                                                                    preferred_element_type=F32)
                        dk_acc[pl.ds(c0, tk), :] += lax.dot_general(ds, qcat, TN_DIMS, preferred_element_type=F32)
                        dq_acc[...] += jnp.dot(ds, kc, preferred_element_type=F32)
                        return 0

                    lax.fori_loop(0, nfull, lambda kj, c: step(kj, c, False), 0)
                    step(nfull, 0, True)
                    d = dq_acc[...]
                    pe = d[:, LANE:] * _lane_mask(ROPE, 2 * t + hh, F32)
                    if hh == 0:
                        dq_ref[pl.ds(r0, tq), cols] = d[:, :LANE] * mine
                    else:
                        dq_ref[pl.ds(r0, tq), cols] += d[:, :LANE] * mine
                    if t == 0 and hh == 0:
                        dq_ref[pl.ds(r0, tq), 2 * LANE:3 * LANE] = pe
                    else:
                        dq_ref[pl.ds(r0, tq), 2 * LANE:3 * LANE] += pe
                    return 0

                lax.fori_loop(0, nq, q_block, 0)
            dkv_ref[:, t * LANE:(t + 1) * LANE] = dk_acc[:, :LANE].astype(BF)
            dkv_ref[:, (2 + t) * LANE:(3 + t) * LANE] = dv_acc[...].astype(BF)
            if t == 0:
                dkp_ref[...] = dk_acc[:, LANE:]
            else:
                dkp_ref[...] += dk_acc[:, LANE:]

    qspec = pl.BlockSpec((S, QW), lambda g: (0, g))
    kvspec = pl.BlockSpec((S, KVW), lambda g: (0, g))
    ospec = pl.BlockSpec((S, 2 * LANE), lambda g: (0, g))
    return pl.pallas_call(
        body, name="attn_bwd", grid=(QUADS,),
        in_specs=[qspec, kvspec, pl.BlockSpec((S, LANE), lambda g: (0, 0)), ospec, ospec, ospec],
        out_specs=[qspec, kvspec, pl.BlockSpec((S, LANE), lambda g: (0, g))],
        out_shape=[jax.ShapeDtypeStruct((S, HEADS * (NOPE + ROPE)), F32), jax.ShapeDtypeStruct((S, HEADS * (NOPE + VH)), BF),
                   jax.ShapeDtypeStruct((S, HEADS * ROPE), F32)],
        scratch_shapes=[pltpu.VMEM((tq, 2 * LANE), F32), pltpu.VMEM((S, 2 * LANE), F32), pltpu.VMEM((S, LANE), F32)],
        compiler_params=_cparams(),
    )(q, kv, kpe, o, lse, do)


def exchange(name, gathers, a2as):
    n_g, n = len(gathers), len(gathers) + len(a2as)

    def body(*refs):
        ins, outs = refs[:n], refs[n:2 * n]
        send_sems, recv_sems, loc_sems = refs[2 * n:]
        x, y, c = lax.axis_index("x"), lax.axis_index("y"), lax.axis_index("c")
        me = 4 * x + 2 * y + c

        def peer(k):
            px = 1 - x if k & 4 else x
            py = 1 - y if k & 2 else y
            pc = 1 - c if k & 1 else c
            return (px, py, pc), 4 * px + 2 * py + pc

        def remote(a, k):
            pid, pflat = peer(k)
            src = ins[a] if a < n_g else ins[a].at[pflat]
            return pltpu.make_async_remote_copy(
                src_ref=src, dst_ref=outs[a].at[me], send_sem=send_sems.at[a, k - 1], recv_sem=recv_sems.at[a, k - 1],
                device_id=pid, device_id_type=MESH)

        def arrival(a, k):
            pid, pflat = peer(k)
            src = ins[a] if a < n_g else ins[a].at[pflat]
            return pltpu.make_async_remote_copy(
                src_ref=src, dst_ref=outs[a].at[pflat], send_sem=send_sems.at[a, k - 1], recv_sem=recv_sems.at[a, k - 1],
                device_id=pid, device_id_type=MESH)

        local = []
        for a in range(n):
            own = ins[a] if a < n_g else ins[a].at[me]
            cp = pltpu.make_async_copy(own, outs[a].at[me], loc_sems.at[a])
            cp.start()
            local.append(cp)
        sent = []
        for k in (1, 2, 4, 3, 5, 6, 7):
            for a in range(n):
                cp = remote(a, k)
                cp.start()
                sent.append(cp)
        for k in range(1, 8):
            for a in range(n):
                arrival(a, k).wait_recv()
        for cp in sent:
            cp.wait_send()
        for cp in local:
            cp.wait()

    out_shape = [jax.ShapeDtypeStruct((NDEV,) + g.shape, g.dtype) for g in gathers]
    out_shape += [jax.ShapeDtypeStruct(a.shape, a.dtype) for a in a2as]
    any_spec = pl.BlockSpec(memory_space=pl.ANY)
    return pl.pallas_call(
        body, name=name, in_specs=[any_spec] * n, out_specs=[any_spec] * n, out_shape=out_shape,
        scratch_shapes=[pltpu.SemaphoreType.DMA((n, NDEV - 1)), pltpu.SemaphoreType.DMA((n, NDEV - 1)),
                        pltpu.SemaphoreType.DMA((n,))],
    )(*gathers, *a2as)


def gather_two_level(name, block, dep):
    def body(x_ref, _, out_ref, stage, send_sems, recv_sems, loc_sem):
        x, y, c = lax.axis_index("x"), lax.axis_index("y"), lax.axis_index("c")
        me, sibling = (x, y, c), (x, y, 1 - c)
        chips = [(1 - x, y), (x, 1 - y), (1 - x, 1 - y)]

        def slot(px, py, pc):
            return out_ref.at[4 * px + 2 * py + pc]

        def copy(k, owner, to, src=None):
            return pltpu.make_async_remote_copy(
                src_ref=slot(*owner) if src is None else src, dst_ref=slot(*owner), send_sem=send_sems.at[k],
                recv_sem=recv_sems.at[k], device_id=to, device_id_type=MESH)

        load = pltpu.make_async_copy(x_ref, stage, loc_sem)
        load.start()
        first = [copy(0, me, sibling, src=x_ref)] + [copy(1 + j, me, (*chip, c), src=x_ref) for j, chip in enumerate(chips)]
        for cp in first:
            cp.start()
        load.wait()
        store = pltpu.make_async_copy(stage, slot(*me), loc_sem)
        store.start()
        passed = [copy(4 + j, (*chip, c), sibling) for j, chip in enumerate(chips)]
        for j, chip in enumerate(chips):
            copy(1 + j, (*chip, c), me).wait_recv()
            passed[j].start()
        copy(0, sibling, me).wait_recv()
        for j, chip in enumerate(chips):
            copy(4 + j, (*chip, 1 - c), me).wait_recv()
        for cp in first + passed:
            cp.wait_send()
        store.wait()

    return pl.pallas_call(
        body, name=name, in_specs=[pl.BlockSpec(memory_space=pl.ANY)] * 2, out_specs=pl.BlockSpec(memory_space=pl.ANY),
        out_shape=jax.ShapeDtypeStruct((NDEV,) + block.shape, block.dtype),
        scratch_shapes=[pltpu.VMEM(block.shape, block.dtype), pltpu.SemaphoreType.DMA((NDEV - 1,)),
                        pltpu.SemaphoreType.DMA((NDEV - 1,)), pltpu.SemaphoreType.DMA],
        compiler_params=_cparams(),
    )(block, dep)


def _peer(k, x, y, c):
    px = 1 - x if k & 4 else x
    py = 1 - y if k & 2 else y
    pc = 1 - c if k & 1 else c
    return (px, py, pc), 4 * px + 2 * py + pc


PEER_ORDER = (1, 2, 4, 3, 5, 6, 7)
HBM_SPEC = pl.BlockSpec(memory_space=pltpu.HBM)
SEM_SPEC = pl.BlockSpec(memory_space=pltpu.SEMAPHORE)
ANY_SPEC = pl.BlockSpec(memory_space=pl.ANY)


def _split_copies(ins, lands, n_g, send_sems, recv_sems):
    x, y, c = lax.axis_index("x"), lax.axis_index("y"), lax.axis_index("c")
    me = 4 * x + 2 * y + c

    def outgoing(a, k):
        pid, pflat = _peer(k, x, y, c)
        src = ins[a] if a < n_g else ins[a].at[pflat]
        return pltpu.make_async_remote_copy(
            src_ref=src, dst_ref=lands[a].at[me], send_sem=send_sems.at[a * (NDEV - 1) + k - 1],
            recv_sem=recv_sems.at[a * (NDEV - 1) + k - 1],
            device_id=pid, device_id_type=MESH)

    def arrival(a, k):
        pid, pflat = _peer(k, x, y, c)
        src = ins[a] if a < n_g else ins[a].at[pflat]
        return pltpu.make_async_remote_copy(
            src_ref=src, dst_ref=lands[a].at[pflat], send_sem=send_sems.at[a * (NDEV - 1) + k - 1],
            recv_sem=recv_sems.at[a * (NDEV - 1) + k - 1],
            device_id=pid, device_id_type=MESH)

    return outgoing, arrival


def exchange_begin(name, srcs, n_g, dep):
    n = len(srcs)
    land_shapes = [((NDEV,) + s.shape) if a < n_g else s.shape for a, s in enumerate(srcs)]

    def own_body(*refs):
        ins, outs = refs[:n], refs[n + 1:2 * n + 1]
        stage, sems = refs[2 * n + 1:3 * n + 1], refs[-1]
        me = 4 * lax.axis_index("x") + 2 * lax.axis_index("y") + lax.axis_index("c")
        cps = [pltpu.make_async_copy(ins[a] if a < n_g else ins[a].at[me], stage[a], sems.at[a]) for a in range(n)]
        for cp in cps:
            cp.start()
        for cp in cps:
            cp.wait()
        cps = [pltpu.make_async_copy(stage[a], outs[a].at[me], sems.at[a]) for a in range(n)]
        for cp in cps:
            cp.start()
        for cp in cps:
            cp.wait()

    lands = pl.pallas_call(
        own_body, name=name + "_own", in_specs=[ANY_SPEC] * (n + 1), out_specs=[ANY_SPEC] * n,
        out_shape=[jax.ShapeDtypeStruct(sh, s.dtype) for sh, s in zip(land_shapes, srcs)],
        scratch_shapes=[pltpu.VMEM(sh[1:], s.dtype) for sh, s in zip(land_shapes, srcs)] + [pltpu.SemaphoreType.DMA((n,))],
        compiler_params=_cparams(),
    )(*srcs, dep)

    def start_body(*refs):
        ins, lz = refs[:n], refs[n:2 * n]
        send_sems, recv_sems, token = refs[2 * n], refs[2 * n + 1], refs[-1]
        outgoing, _ = _split_copies(ins, lz, n_g, send_sems, recv_sems)
        for k in PEER_ORDER:
            for a in range(n):
                outgoing(a, k).start()
        token[...] = jnp.zeros(token.shape, F32)

    hbm = lambda t: pltpu.HBM(t.shape, t.dtype)
    res = pl.pallas_call(
        start_body, name=name + "_start",
        out_shape=(pltpu.SemaphoreType.DMA((n * (NDEV - 1),)), pltpu.SemaphoreType.DMA((n * (NDEV - 1),)),
                   *[hbm(s) for s in srcs], *[hbm(t) for t in lands], jax.ShapeDtypeStruct((8, LANE), F32)),
        in_specs=[HBM_SPEC] * (2 * n),
        out_specs=(SEM_SPEC, SEM_SPEC, *[HBM_SPEC] * (2 * n), pl.BlockSpec(memory_space=pltpu.VMEM)),
        input_output_aliases={i: 2 + i for i in range(2 * n)},
        compiler_params=pltpu.CompilerParams(has_side_effects=pltpu.SideEffectType.DATAFLOW_SIDE_EFFECTING),
    )(*[pltpu.with_memory_space_constraint(t, pltpu.HBM) for t in list(srcs) + list(lands)])
    return (name, n, n_g, res[:-1]), res[-1]


def exchange_end(handle, after):
    name, n, n_g, (send_sems, recv_sems, *bufs) = handle

    def wait_body(*refs):
        ins, lz = refs[:n], refs[n:2 * n]
        ss, rs = refs[2 * n], refs[2 * n + 1]
        outgoing, arrival = _split_copies(ins, lz, n_g, ss, rs)
        for k in range(1, NDEV):
            for a in range(n):
                arrival(a, k).wait_recv()
        for k in range(1, NDEV):
            for a in range(n):
                outgoing(a, k).wait_send()

    res = pl.pallas_call(
        wait_body, name=name + "_wait", out_shape=tuple(pltpu.HBM(t.shape, t.dtype) for t in bufs),
        in_specs=[HBM_SPEC] * (2 * n) + [SEM_SPEC, SEM_SPEC, ANY_SPEC], out_specs=[HBM_SPEC] * (2 * n),
        input_output_aliases={i: i for i in range(2 * n)},
        compiler_params=pltpu.CompilerParams(has_side_effects=pltpu.SideEffectType.DATAFLOW_SIDE_EFFECTING),
    )(*bufs, send_sems, recv_sems, after)
    return list(res[n:])


def _pick_rows(R, mult, cap):
    best = None
    for n in range(1, R + 1):
        if R % n == 0 and (R // n) % mult == 0 and R // n <= cap:
            best = R // n
            break
    assert best is not None, (R, mult, cap)
    return best


def sum_slots(name, x):
    _, R, _ = x.shape
    tr = _pick_rows(R, 16, 2304)

    def body(x_ref, o_ref):
        acc = x_ref[0].astype(F32)
        for d in range(1, NDEV):
            acc = acc + x_ref[d].astype(F32)
        o_ref[...] = acc

    return pl.pallas_call(
        body, name=name, grid=(R // tr,),
        in_specs=[pl.BlockSpec((NDEV, tr, LANE), lambda i: (0, i, 0))],
        out_specs=pl.BlockSpec((tr, LANE), lambda i: (i, 0)),
        out_shape=jax.ShapeDtypeStruct((R, LANE), F32), compiler_params=_cparams(),
    )(x)


def adamw(name, w, g, m, v):
    L, R, C = w.shape
    tr = _pick_rows(R, 8, 256) if R % 8 == 0 else R

    def body(w_ref, g_ref, m_ref, v_ref, d_ref, nm_ref, nv_ref):
        gg = g_ref[...]
        nm = B1 * m_ref[...] + (1.0 - B1) * gg
        nv = B2 * v_ref[...] + (1.0 - B2) * jnp.square(gg)
        m_hat = nm / (1.0 - B1 ** STEP)
        v_hat = nv / (1.0 - B2 ** STEP)
        d_ref[...] = -LR * (m_hat / (jnp.sqrt(v_hat) + EPS) + WD * w_ref[...])
        nm_ref[...] = nm
        nv_ref[...] = nv

    blk = pl.BlockSpec((1, tr, C), lambda l, i: (l, i, 0))
    shp = jax.ShapeDtypeStruct(w.shape, F32)
    return pl.pallas_call(
        body, name=name, grid=(L, R // tr), in_specs=[blk] * 4, out_specs=[blk] * 3, out_shape=[shp] * 3,
        compiler_params=_cparams(),
    )(w, g, m, v)


IN_SHARD = D_IN // NDEV
UQ_SHARD = HEADS * (NOPE + ROPE) // NDEV
W_IN_PAD = 1024
ROW_A, ROW_B, ROW_C, ROW_UKV, ROW_UQ, MISC_ROWS = 0, 512, 1024, 1536, 1792, 2176


def _in_perm_index():
    ar = np.arange
    z = lambda n: np.full((n,), -1, np.int64)
    mix = lambda lo1, lo2: np.concatenate([ar(lo + LANE * j, lo + LANE * (j + 1)) for j in range(CW // LANE)
                                           for lo in (lo1, lo2)])
    return np.concatenate([ar(4768, 7840), mix(0, 512), ar(1024, 1536), mix(1536, 2560), ar(4256, 4768), ar(2048, 2560),
                           ar(3072, 3584), ar(3968, 4224), ar(4224, 4256), z(OFF_Q - OFF_KR - ROPE), ar(3584, 3968),
                           z(NP - OFF_Q - QL)])


def _head_perm_index(a, b):
    parts = []
    for g in range(QUADS):
        h = np.arange(4 * g, 4 * g + 4)[:, None] * (a + b)
        parts += [(h + np.arange(a)[None]).reshape(-1), (h + a + np.arange(b)[None]).reshape(-1)]
    return np.concatenate(parts)


def _inverse(perm, n):
    inv = np.full((n,), -1, np.int64)
    inv[perm[perm >= 0]] = np.nonzero(perm >= 0)[0]
    return inv


IN_PERM = _in_perm_index()
UQ_PERM = _head_perm_index(NOPE, ROPE)
UKV_PERM = _head_perm_index(NOPE, VH)


def _to_gathered(perm, shard, pad):
    return np.where(perm >= 0, (perm // shard) * pad + perm % shard, -1)


def _from_full(inv, shard, pad):
    j, i = np.divmod(np.arange(NDEV * pad), pad)
    return np.where(i < shard, inv[np.minimum(j * shard + i, inv.shape[0] - 1)], -1)


def col_gather(name, srcs, out_shapes, jobs, deps=()):
    ns, nj, nd, no = len(srcs), len(jobs), len(deps), len(out_shapes)
    tables = [jnp.asarray(np.asarray(job[5], np.int32)[None, :]) for job in jobs]

    def view(ref, col0, width, r0, rc):
        n = ref.shape[-1]
        if len(ref.shape) == 3:
            return ref.at[col0 // n, pl.ds(r0, rc), pl.ds(col0 % n, width)]
        return ref.at[pl.ds(r0, rc), pl.ds(col0, width)]

    def slabs(shape):
        if len(shape) == 3:
            return [((d,), d * shape[2], (d + 1) * shape[2]) for d in range(shape[0])]
        w = 1024 if shape[1] > 1024 and shape[1] % 1024 == 0 else shape[1]
        return [((slice(None), pl.ds(c, w)), c, c + w) for c in range(0, shape[1], w)]

    src_slabs = [slabs(s.shape) for s in srcs]
    out_slabs = [slabs(sh) for sh in out_shapes]
    work, first_use, last_touch = [], {}, {}
    for ji, (si, srow, oi, orow, nrows, tgt) in enumerate(jobs):
        tgt = np.asarray(tgt)
        tw = 256 if out_shapes[oi][-1] % 256 == 0 else LANE
        sw = 256 if srcs[si].shape[-1] % 256 == 0 else LANE
        for t in range(tgt.shape[0] // tw):
            tt = tgt[t * tw:(t + 1) * tw]
            tiles = sorted(set((tt[tt >= 0] // sw).tolist()))
            straight = bool(tiles) and tt[0] >= 0 and tt[0] % LANE == 0 and np.array_equal(tt, tt[0] + np.arange(tw))
            cols = [(int(tt[0]) + k * LANE, LANE) for k in range(tw // LANE)] if straight else [(s * sw, sw) for s in tiles]
            need = sorted({(si, k) for c0, _ in cols for k, (_, lo, hi) in enumerate(src_slabs[si]) if lo <= c0 < hi})
            touch = [(oi, k) for k, (_, lo, hi) in enumerate(out_slabs[oi]) if lo <= t * tw < hi][0]
            for key in need:
                first_use.setdefault(key, len(work))
            last_touch[touch] = len(work)
            work.append((ji, t, tw, sw, tiles, straight, need, touch))
    in_order = sorted(first_use, key=first_use.get)
    in_sem = {key: i for i, key in enumerate(in_order)}
    out_keys = sorted(last_touch)
    out_sem = {key: i for i, key in enumerate(out_keys)}

    def body(*refs):
        src_hbm, tab_refs = refs[:ns], refs[ns:ns + nj]
        out_hbm = refs[ns + nj + nd:ns + nj + nd + no]
        scratch = refs[ns + nj + nd + no:]
        src_refs, out_refs, in_sems, out_sems = scratch[:ns], scratch[ns:ns + no], scratch[-2], scratch[-1]
        loads = {}
        for key in in_order:
            si, k = key
            idx = src_slabs[si][k][0]
            loads[key] = pltpu.make_async_copy(src_hbm[si].at[idx], src_refs[si].at[idx], in_sems.at[in_sem[key]])
            loads[key].start()
        arrived, stores = set(), []
        for wi, (ji, t, tw, sw, tiles, straight, need, touch) in enumerate(work):
            si, srow, oi, orow, nrows, tgt = jobs[ji]
            sref, oref = src_refs[si], out_refs[oi]
            rc = nrows if nrows <= 1024 else 1024
            for key in need:
                if key not in arrived:
                    loads[key].wait()
                    arrived.add(key)
            onehots = []
            if tiles and not straight:
                want = tab_refs[ji][:, t * tw:(t + 1) * tw]
                row = lax.broadcasted_iota(jnp.int32, (sw, tw), 0)
                onehots = [jnp.where(want == row + s * sw, 1.0, 0.0).astype(BF) for s in tiles]
            first = int(np.asarray(tgt)[t * tw])

            def chunk(ci, _, t=t, tw=tw, sw=sw, tiles=tiles, straight=straight, onehots=onehots, first=first,
                      sref=sref, oref=oref, srow=srow, orow=orow, rc=rc):
                r0 = ci * rc
                ro = pl.multiple_of(orow + r0, LANE)
                rs = pl.multiple_of(srow + r0, LANE)
                if not tiles:
                    view(oref, t * tw, tw, ro, rc)[...] = jnp.zeros((rc, tw), BF)
                elif straight:
                    for k in range(tw // LANE):
                        view(oref, t * tw + k * LANE, LANE, ro, rc)[...] = view(sref, first + k * LANE, LANE, rs, rc)[...]
                else:
                    acc = None
                    for s, oh in zip(tiles, onehots):
                        p = jnp.dot(view(sref, s * sw, sw, rs, rc)[...], oh, preferred_element_type=F32)
                        acc = p if acc is None else acc + p
                    view(oref, t * tw, tw, ro, rc)[...] = acc.astype(BF)
                return 0

            lax.fori_loop(0, nrows // rc, chunk, 0)
            if last_touch[touch] == wi:
                idx = out_slabs[touch[0]][touch[1]][0]
                cp = pltpu.make_async_copy(out_refs[touch[0]].at[idx], out_hbm[touch[0]].at[idx], out_sems.at[out_sem[touch]])
                cp.start()
                stores.append(cp)
        for cp in stores:
            cp.wait()

    return pl.pallas_call(
        body, name=name, in_specs=[ANY_SPEC] * ns + [pl.BlockSpec(memory_space=pltpu.VMEM)] * nj + [ANY_SPEC] * nd,
        out_specs=[ANY_SPEC] * no, out_shape=[jax.ShapeDtypeStruct(s, BF) for s in out_shapes],
        scratch_shapes=[pltpu.VMEM(s.shape, BF) for s in srcs] + [pltpu.VMEM(s, BF) for s in out_shapes]
        + [pltpu.SemaphoreType.DMA((len(in_order),)), pltpu.SemaphoreType.DMA((len(out_keys),))],
        compiler_params=_cparams(),
    )(*srcs, *tables, *deps)


def sum_adamw(name, recvs, w, m, v, lo=0, prev=None, row0=0):
    _, R, C = w.shape
    L = len(recvs)
    CP = recvs[0].shape[-1]
    tr = _pick_rows(R, 16, 128)
    n_prev = 0 if prev is None else 4

    def body(*refs):
        r_refs = refs[:L]
        w_ref, m_ref, v_ref = refs[L:L + 3]
        g_ref, d_ref, nm_ref, nv_ref, gsum = refs[L + 3 + n_prev:]
        layer = pl.program_id(0)
        for k in range(L):
            def total(k=k):
                acc = r_refs[k][0].astype(F32)
                for d in range(1, NDEV):
                    acc = acc + r_refs[k][d].astype(F32)
                gsum[...] = acc
            pl.when(layer == k)(total)
        gg = gsum[:, 0:C]
        nm = B1 * m_ref[...] + (1.0 - B1) * gg
        nv = B2 * v_ref[...] + (1.0 - B2) * jnp.square(gg)
        m_hat = nm / (1.0 - B1 ** STEP)
        v_hat = nv / (1.0 - B2 ** STEP)
        g_ref[...] = gg
        d_ref[...] = -LR * (m_hat / (jnp.sqrt(v_hat) + EPS) + WD * w_ref[...])
        nm_ref[...] = nm
        nv_ref[...] = nv

    assert row0 % tr == 0
    r_specs = [pl.BlockSpec((NDEV, tr, CP),
                            functools.partial(lambda l, i, k: (0, row0 // tr + jnp.where(l == k, i, 0), 0), k=k))
               for k in range(L)]
    blk = pl.BlockSpec((None, tr, C), lambda l, i: (l + lo, i, 0))
    shp = jax.ShapeDtypeStruct(w.shape, F32)
    return pl.pallas_call(
        body, name=name, grid=(L, R // tr), in_specs=r_specs + [blk] * 3 + [ANY_SPEC] * n_prev, out_specs=[blk] * 4,
        out_shape=[shp] * 4, input_output_aliases={L + 3 + i: i for i in range(n_prev)},
        scratch_shapes=[pltpu.VMEM((tr, CP), F32)], compiler_params=_cparams(),
    )(*recvs, w, m, v, *(prev or ()))


ALPHA = 8.0 ** 0.25
T_WIDE, T_NARROW = 512, 1024


def _rope_fn(sign):
    def fn(x, cos, sin):
        W = x.shape[-1]
        lane = lax.broadcasted_iota(jnp.int32, x.shape, 1)
        first_half = (lane % ROPE) < (ROPE // 2)
        rot = jnp.where(first_half, -pltpu.roll(x, W - ROPE // 2, 1), pltpu.roll(x, ROPE // 2, 1))
        return x * cos + sign * rot * sin
    return fn


def _modulate(xv, a):
    return xv * (1.0 + a[1:2, :]) + a[0:1, :]


def layer_fwd(x, ada3, W, tabs, S, u=None, ada_next=None):
    cos, sin = tabs
    T = T_NARROW
    if u is None:
        u = rowwise("modulate", _modulate, S, T, [(x, D_MODEL, 0)], [ada3], [(D_MODEL, BF)])[0]
    proj = mm(u, W["in"], name="mm_proj", tm=1024, tn=1024, out_dtype=BF)
    W = {**W, **W["late"](proj)}

    ca = conv_fwd("conv_a_fwd", proj, OFF_A, W["conv_a"], 31, "glu", S, CW)

    def a_post(c, ag, vec):
        n, _ = _ln_stats(c + vec[0:1, :])
        return _silu(n * vec[1:2, :] + vec[2:3, :]) * _silu(ag)

    h_a = rowwise("mix_a_post", a_post, S, T, [(ca, CW, 0), (proj, CW, OFF_AG)], [W["vec_a"]], [(CW, BF)])[0]
    y_a = mm(h_a, W["a_out"], name="mm_branch_out", out_dtype=BF)

    cb = conv_fwd("conv_b_fwd", proj, OFF_B, W["conv_b"], 3, "mul", S, CW)
    h_b = rowwise("mix_b_post", lambda c, gb, bg: gb * c * _silu(bg), S, T,
                  [(cb, CW, 0), (proj, CW, OFF_GB), (proj, CW, OFF_BG)], [], [(CW, BF)])[0]
    y_b = mm(h_b, W["b_out"], name="mm_branch_out", out_dtype=BF)

    def rms2(ql, kvl, gq, gkv):
        rq = lax.rsqrt(jnp.mean(ql * ql, axis=-1, keepdims=True) + RMS_EPS)
        rk = lax.rsqrt(jnp.mean(kvl * kvl, axis=-1, keepdims=True) + RMS_EPS)
        return ql * rq * gq, kvl * rk * gkv

    qn, kvn = rowwise("rms_fwd", rms2, S, T, [(proj, QL, OFF_Q), (proj, KVL, OFF_KV)], [W["gq"], W["gkv"]],
                      [(QL, BF), (KVL, BF)])
    q = mm(qn, W["uq"], name="mm_q")
    kv = mm(kvn, W["ukv"], name="mm_kv", out_dtype=BF)
    rope = _rope_fn(1.0)

    def rope_fwd(qv, kr, c1, s1):
        parts = []
        for g in range(QUADS):
            parts.append(qv[:, g * QW:g * QW + 2 * LANE].astype(BF))
            parts.append(rope(qv[:, g * QW + 2 * LANE:(g + 1) * QW], c1, s1).astype(BF))
        kp = rope(kr, c1, s1)
        kp = kp + pltpu.roll(kp, ROPE, 1) + pltpu.roll(kp, 2 * ROPE, 1) + pltpu.roll(kp, 3 * ROPE, 1)
        return jnp.concatenate(parts, axis=1), kp

    q_b, kpe = rowwise("rope_fwd", rope_fwd, S, T,
                       [(q, HEADS * (NOPE + ROPE), 0), (proj, LANE, OFF_KR), (cos, LANE, 0), (sin, LANE, 0)], [],
                       [(HEADS * (NOPE + ROPE), BF), (LANE, BF)])
    o, lse = attn_fwd(q_b, kv, kpe, S)
    h_c = rowwise("mix_c_post", lambda ov, cg: ov * _silu(cg), S, T, [(o, CW, 0), (proj, CW, OFF_CG)], [],
                  [(CW, BF)])[0]
    y_c = mm(h_c, W["c_out"], name="mm_branch_out", out_dtype=BF)

    def merge(la, lb, lc, ya, yb, yc):
        return _sigmoid(la) * ya + _sigmoid(lb) * yb + _sigmoid(lc) * yc

    m = rowwise("merge_fwd", merge, S, T_WIDE,
                [(proj, D_MODEL, 0), (proj, D_MODEL, 1024), (proj, D_MODEL, 2048), (y_a, D_MODEL, 0),
                 (y_b, D_MODEL, 0), (y_c, D_MODEL, 0)], [], [(D_MODEL, BF)])[0]
    out = mm(m, W["o"], name="mm_out")

    def ln_fwd(xv, ov, a, lnv, *nxt):
        n, _ = _ln_stats(ALPHA * xv + a[2:3, :] * ov)
        y = n * lnv[0:1, :] + lnv[1:2, :]
        return (y, _modulate(y, nxt[0])) if nxt else y

    res = rowwise("ln_fwd", ln_fwd, S, T_WIDE, [(x, D_MODEL, 0), (out, D_MODEL, 0)],
                  [ada3, W["lnv"]] + ([] if ada_next is None else [ada_next]),
                  [(D_MODEL, F32)] + ([] if ada_next is None else [(D_MODEL, BF)]))
    saved = dict(x=x, u=u, proj=proj, ca=ca, cb=cb, h_a=h_a, h_b=h_b, h_c=h_c, y_a=y_a, y_b=y_b, y_c=y_c, qn=qn,
                 kvn=kvn, q_b=q_b, kv=kv, kpe=kpe, lse=lse, o=o, m=m, out=out)
    return res[0], saved, W, (res[1] if ada_next is not None else None)


def layer_bwd(dxn, sv, ada3, W, tabs, S, before_in=None):
    cos, sin = tabs
    T = T_NARROW
    x, proj = sv["x"], sv["proj"]
    G = {}

    def ln_bwd(xv, ov, dy, a, lnv):
        gate = a[2:3, :]
        n, rstd = _ln_stats(ALPHA * xv + gate * ov)
        dr = _ln_bwd(dy * lnv[0:1, :], n, rstd)
        return ALPHA * dr, gate * dr, _colsum(dy * n), _colsum(dy), _colsum(dr * ov)

    dres, d_out, G["ln_g"], G["ln_b"], d_gate = rowwise(
        "ln_bwd", ln_bwd, S, T_WIDE, [(x, D_MODEL, 0), (sv["out"], D_MODEL, 0), (dxn, D_MODEL, 0)], [ada3, W["lnv"]],
        [(D_MODEL, F32), (D_MODEL, BF)], [D_MODEL] * 3)
    dm = mm(d_out, W["o"], name="mm_dm", trans_b=True, out_dtype=BF)
    G["w_o"] = mm(sv["m"], d_out, name="mm_gw_o", trans_a=True, out_dtype=BF)

    def merge_bwd(dmv, la, lb, lc, ya, yb, yc):
        outs, dls = [], []
        for lg, yv in ((la, ya), (lb, yb), (lc, yc)):
            s = _sigmoid(lg)
            outs.append(dmv * s)
            dls.append((dmv * yv * s * (1.0 - s)).astype(BF))
        return (jnp.concatenate(dls, axis=1),) + tuple(outs)

    d_proj, dy_a, dy_b, dy_c = rowwise(
        "merge_bwd", merge_bwd, S, T_WIDE,
        [(dm, D_MODEL, 0), (proj, D_MODEL, 0), (proj, D_MODEL, 1024), (proj, D_MODEL, 2048), (sv["y_a"], D_MODEL, 0),
         (sv["y_b"], D_MODEL, 0), (sv["y_c"], D_MODEL, 0)], [], [(3 * D_MODEL, BF)] + [(D_MODEL, BF)] * 3,
        into=(None, NP, OFF_M))

    dh = {}
    for br, dy in (("a", dy_a), ("b", dy_b), ("c", dy_c)):
        dh[br] = mm(dy, W[br + "_out"], name="mm_dh", trans_b=True, out_dtype=BF)
        G["w_%s_out" % br] = mm(sv["h_" + br], dy, name="mm_gw_branch", trans_a=True, out_dtype=BF)

    def a_post_bwd(c, ag, dhv, vec):
        n, rstd = _ln_stats(c + vec[0:1, :])
        z = n * vec[1:2, :] + vec[2:3, :]
        d_ag = dhv * _silu(z) * _dsilu(ag)
        dz = dhv * _silu(ag) * _dsilu(z)
        dc = _ln_bwd(dz * vec[1:2, :], n, rstd)
        return d_ag, dc, _colsum(dc), _colsum(dz * n), _colsum(dz)

    d_proj, dca, G["conv_a_b"], G["ln_a_g"], G["ln_a_b"] = rowwise(
        "mix_a_post_bwd", a_post_bwd, S, T, [(sv["ca"], CW, 0), (proj, CW, OFF_AG), (dh["a"], CW, 0)], [W["vec_a"]],
        [(CW, BF), (CW, F32)], [CW] * 3, into=(d_proj, NP, OFF_AG))
    d_proj, G["conv_a_w"] = conv_bwd("conv_a_bwd", proj, OFF_A, dca, W["conv_a"], 31, "glu", S, CW, d_proj)

    def b_post_bwd(c, gb, bg, dhv):
        sg = _silu(bg)
        d_gb_bg = jnp.concatenate([(dhv * sg * c).astype(BF), (dhv * gb * c * _dsilu(bg)).astype(BF)], axis=1)
        return d_gb_bg, dhv * sg * gb

    d_proj, dcb = rowwise("mix_b_post_bwd", b_post_bwd, S, T,
                          [(sv["cb"], CW, 0), (proj, CW, OFF_GB), (proj, CW, OFF_BG), (dh["b"], CW, 0)], [],
                          [(2 * CW, BF), (CW, F32)], into=(d_proj, NP, OFF_GB))
    d_proj, G["conv_b_w"] = conv_bwd("conv_b_bwd", proj, OFF_B, dcb, W["conv_b"], 3, "mul", S, CW, d_proj)

    d_proj, d_o = rowwise("mix_c_post_bwd", lambda ov, cg, dhv: (dhv * ov * _dsilu(cg), dhv * _silu(cg)), S, T,
                          [(sv["o"], CW, 0), (proj, CW, OFF_CG), (dh["c"], CW, 0)], [], [(CW, BF), (CW, F32)],
                          into=(d_proj, NP, OFF_CG))
    dq, d_kv, dkp_heads = attn_bwd(sv["q_b"], sv["kv"], sv["kpe"], sv["o"], sv["lse"], d_o, S)
    ropeT = _rope_fn(-1.0)

    def rope_bwd(dqv, dkp, c1, s1):
        parts = []
        for g in range(QUADS):
            parts.append(dqv[:, g * QW:g * QW + 2 * LANE].astype(BF))
            parts.append(ropeT(dqv[:, g * QW + 2 * LANE:(g + 1) * QW], c1, s1).astype(BF))
        f = dkp[:, :LANE] + dkp[:, LANE:]
        f = f + pltpu.roll(f, 64, 1)
        f = f + pltpu.roll(f, 32, 1)
        lane = lax.broadcasted_iota(jnp.int32, f.shape, 1)
        return jnp.concatenate(parts, axis=1), jnp.where(lane < ROPE, ropeT(f, c1, s1), 0.0)

    d_q, dk_pe = rowwise("rope_bwd", rope_bwd, S, T,
                         [(dq, HEADS * (NOPE + ROPE), 0), (dkp_heads, HEADS * ROPE, 0), (cos, LANE, 0), (sin, LANE, 0)],
                         [], [(HEADS * (NOPE + ROPE), BF), (LANE, BF)])
    d_qn = mm(d_q, W["uq"], name="mm_dqn", trans_b=True, out_dtype=BF)
    d_kvn = mm(d_kv, W["ukv"], name="mm_dkvn", trans_b=True, out_dtype=BF)
    G["w_uq"] = mm(sv["qn"], d_q, name="mm_gw_uq", trans_a=True, out_dtype=BF)
    G["w_ukv"] = mm(sv["kvn"], d_kv, name="mm_gw_ukv", trans_a=True, out_dtype=BF)

    def rms_bwd(ql, kvl, dqn, dkn, dkp, gq, gkv):
        res = []
        for xv, dy, g in ((ql, dqn, gq), (kvl, dkn, gkv)):
            r = lax.rsqrt(jnp.mean(xv * xv, axis=-1, keepdims=True) + RMS_EPS)
            dxh = dy * g
            res.append(((r * (dxh - xv * (r * r) * jnp.mean(dxh * xv, axis=-1, keepdims=True))).astype(BF),
                        _colsum(dy * xv * r)))
        pad = jnp.zeros((ql.shape[0], LANE), BF)
        return jnp.concatenate([res[1][0], dkp, pad, res[0][0], pad], axis=1), res[0][1], res[1][1]

    d_proj, G["q_norm_g"], G["kv_norm_g"] = rowwise(
        "rms_bwd", rms_bwd, S, T,
        [(proj, QL, OFF_Q), (proj, KVL, OFF_KV), (d_qn, QL, 0), (d_kvn, KVL, 0), (dk_pe, LANE, 0)],
        [W["gq"], W["gkv"]], [(NP - OFF_KV, BF)], [QL, KVL], into=(d_proj, NP, OFF_KV))
    deps = before_in(G) if before_in is not None else ()
    du = mm(d_proj, W["in"], name="mm_du", trans_b=True, tm=1024, tk=2048, deps=deps)
    G["w_in"] = mm(sv["u"], d_proj, name="mm_gw_in", trans_a=True, out_dtype=BF, tm=1024, tk=2048, deps=deps)

    def mod_bwd(duv, xv, dr, a):
        return duv * (1.0 + a[1:2, :]) + dr, _colsum(duv), _colsum(duv * xv)

    dx, d_shift, d_scale = rowwise("mod_bwd", mod_bwd, S, T_WIDE, [(du, D_MODEL, 0), (x, D_MODEL, 0), (dres, D_MODEL, 0)],
                                   [ada3], [(D_MODEL, F32)], [D_MODEL] * 2)
    d_ada = jnp.concatenate([d_shift, d_scale, d_gate], axis=1)
    return dx, G, d_ada


SMALL = ("conv_a_b", "ln_a_g", "ln_a_b", "q_norm_g", "kv_norm_g", "ln_g", "ln_b")


def _rows(v):
    n = v.shape[0]
    r = -(-n // (LANE * 16)) * 16
    return jnp.pad(v, (0, r * LANE - n)).reshape(r, LANE)


def kernel(x, c, positions, w_ada, b_ada, w_in, conv_a_w, conv_a_b, ln_a_g, ln_a_b, w_a_out, conv_b_w, w_b_out, q_norm_g, kv_norm_g, w_uq, w_ukv, w_c_out, w_o, ln_g, ln_b, loss_target, m_w_ada, m_b_ada, m_w_in, m_conv_a_w, m_conv_a_b, m_ln_a_g, m_ln_a_b, m_w_a_out, m_conv_b_w, m_w_b_out, m_q_norm_g, m_kv_norm_g, m_w_uq, m_w_ukv, m_w_c_out, m_w_o, m_ln_g, m_ln_b, v_w_ada, v_b_ada, v_w_in, v_conv_a_w, v_conv_a_b, v_ln_a_g, v_ln_a_b, v_w_a_out, v_conv_b_w, v_w_b_out, v_q_norm_g, v_kv_norm_g, v_w_uq, v_w_ukv, v_w_c_out, v_w_o, v_ln_g, v_ln_b):
    P = dict(w_ada=w_ada, b_ada=b_ada, w_in=w_in, conv_a_w=conv_a_w, conv_a_b=conv_a_b, ln_a_g=ln_a_g, ln_a_b=ln_a_b,
             w_a_out=w_a_out, conv_b_w=conv_b_w, w_b_out=w_b_out, q_norm_g=q_norm_g, kv_norm_g=kv_norm_g, w_uq=w_uq,
             w_ukv=w_ukv, w_c_out=w_c_out, w_o=w_o, ln_g=ln_g, ln_b=ln_b)
    Mo = dict(w_ada=m_w_ada, b_ada=m_b_ada, w_in=m_w_in, conv_a_w=m_conv_a_w, conv_a_b=m_conv_a_b, ln_a_g=m_ln_a_g,
              ln_a_b=m_ln_a_b, w_a_out=m_w_a_out, conv_b_w=m_conv_b_w, w_b_out=m_w_b_out, q_norm_g=m_q_norm_g,
              kv_norm_g=m_kv_norm_g, w_uq=m_w_uq, w_ukv=m_w_ukv, w_c_out=m_w_c_out, w_o=m_w_o, ln_g=m_ln_g, ln_b=m_ln_b)
    Vo = dict(w_ada=v_w_ada, b_ada=v_b_ada, w_in=v_w_in, conv_a_w=v_conv_a_w, conv_a_b=v_conv_a_b, ln_a_g=v_ln_a_g,
              ln_a_b=v_ln_a_b, w_a_out=v_w_a_out, conv_b_w=v_conv_b_w, w_b_out=v_w_b_out, q_norm_g=v_q_norm_g,
              kv_norm_g=v_kv_norm_g, w_uq=v_w_uq, w_ukv=v_w_ukv, w_c_out=v_w_c_out, w_o=v_w_o, ln_g=v_ln_g, ln_b=v_ln_b)
    ORDER = ("w_ada", "b_ada", "w_in", "conv_a_w", "conv_a_b", "ln_a_g", "ln_a_b", "w_a_out", "conv_b_w", "w_b_out",
             "q_norm_g", "kv_norm_g", "w_uq", "w_ukv", "w_c_out", "w_o", "ln_g", "ln_b")
    L = w_ada.shape[0]
    S = x.shape[1]
    me = 4 * lax.axis_index("x") + 2 * lax.axis_index("y") + lax.axis_index("c")
    x2 = x[0]
    tgt = loss_target[0]

    small_in = _rows(jnp.concatenate([c.reshape(-1), conv_a_w.reshape(-1), conv_b_w.reshape(-1)]))
    w_in_b = jnp.pad(w_in.astype(BF), ((0, 0), (0, 0), (0, W_IN_PAD - IN_SHARD)))
    misc_b = jnp.concatenate([w_a_out, w_b_out, w_c_out, w_ukv, jnp.pad(w_uq, ((0, 0), (0, 0), (0, LANE - UQ_SHARD)))],
                             axis=1).astype(BF)
    w_o_b = w_o.astype(BF)
    gathered = [None] * L
    sg = exchange("gather_small", [small_in], [])[0]
    sgf = sg.reshape(NDEV, -1)
    c_all = sgf[:, :D_MODEL]
    o1 = D_MODEL + L * 31 * 64
    conv_a_full = sgf[:, D_MODEL:o1].reshape(NDEV, L, 31, 64).transpose(1, 2, 0, 3).reshape(L, 31, CW)
    conv_b_full = sgf[:, o1:o1 + L * 3 * 64].reshape(NDEV, L, 3, 64).transpose(1, 2, 0, 3).reshape(L, 3, CW)

    c_act = rowwise("silu_c", _silu, 16, 16, [(jnp.pad(c_all, ((0, 8), (0, 0))), D_MODEL, 0)], [], [(D_MODEL, BF)])[0]
    ncol = w_ada.shape[2]
    w_ada_b = w_ada.astype(BF).transpose(1, 0, 2).reshape(D_MODEL, L * ncol)
    b_mine = lax.dynamic_slice_in_dim(b_ada, me * ncol, ncol, axis=1).reshape(1, L * ncol)
    ada_part = mm(c_act, w_ada_b, name="mm_ada", bias=b_mine)
    ada_rows = -(-(L * ncol) // (LANE * 8)) * 8
    ada_send = jnp.pad(ada_part[:NDEV].reshape(NDEV, -1, LANE), ((0, 0), (0, ada_rows - L * ncol // LANE), (0, 0)))
    ada_recv = exchange("a2a_ada", [], [ada_send])[0]
    ada = ada_recv[:, :L * ncol // LANE].reshape(NDEV, L, ncol).transpose(1, 0, 2).reshape(L, 3, D_MODEL)
    gathered[0] = [gather_two_level("gather0_w_in", w_in_b[0], ada)]
    pending_rest, rest_token = exchange_begin("gather0_rest", [misc_b[0], w_o_b[0]], 2, gathered[0][0])

    inv_freq = ROPE_THETA ** (-jnp.arange(0, ROPE, 2, dtype=F32) / ROPE)
    ang = positions[0].astype(F32)[:, None] * inv_freq
    tabs = (jnp.tile(jnp.cos(ang), (1, 2 * LANE // ROPE)), jnp.tile(jnp.sin(ang), (1, 2 * LANE // ROPE)))

    straight = np.arange(D_MODEL)
    fwd_in = [(0, 0, 0, 0, D_MODEL, _to_gathered(IN_PERM, IN_SHARD, W_IN_PAD))]
    fwd_misc = [(0, ROW_A, 0, 0, CW, straight), (0, ROW_B, 1, 0, CW, straight), (0, ROW_C, 2, 0, CW, straight),
                (0, ROW_UKV, 3, 0, KVL, UKV_PERM), (0, ROW_UQ, 4, 0, QL, _to_gathered(UQ_PERM, UQ_SHARD, LANE))]
    rev_in = [(0, 0, 0, 0, D_MODEL, _from_full(_inverse(IN_PERM, D_IN), IN_SHARD, W_IN_PAD))]
    rev_misc = [(0, 0, 0, ROW_A, CW, straight), (1, 0, 0, ROW_B, CW, straight), (2, 0, 0, ROW_C, CW, straight),
                (3, 0, 0, ROW_UKV, KVL, _from_full(_inverse(UKV_PERM, HEADS * (NOPE + VH)), LANE, LANE)),
                (4, 0, 0, ROW_UQ, QL, _from_full(_inverse(UQ_PERM, HEADS * (NOPE + ROPE)), UQ_SHARD, LANE))]

    def layer_weights(l, deps):
        w_in_p = col_gather("relayout_w_in", [gathered[l][0]], [(D_MODEL, NP)], fwd_in, deps)[0]

        def late(after):
            if len(gathered[l]) == 1:
                gathered[l] += exchange_end(pending_rest, after)
            _, g_misc, g_o = gathered[l]
            a_out, b_out, c_out, ukv, uq = col_gather(
                "relayout_misc", [g_misc],
                [(CW, D_MODEL)] * 3 + [(KVL, HEADS * (NOPE + VH)), (QL, HEADS * (NOPE + ROPE))], fwd_misc, deps)
            return {"a_out": a_out, "b_out": b_out, "c_out": c_out, "uq": uq, "ukv": ukv,
                    "o": g_o.reshape(D_MODEL, D_MODEL)}

        return {
            "in": w_in_p, "late": late,
            "conv_a": jnp.pad(conv_a_full[l], ((0, 1), (0, 0))), "conv_b": jnp.pad(conv_b_full[l], ((0, 5), (0, 0))),
            "vec_a": jnp.stack([conv_a_b[l], ln_a_g[l], ln_a_b[l]]), "gq": q_norm_g[l][None], "gkv": kv_norm_g[l][None],
            "lnv": jnp.stack([ln_g[l], ln_b[l]]),
        }

    h = x2
    saved, weights = [], []
    handles, token, u_next = {}, rest_token, None
    for l in range(1, L):
        handles[l], token = exchange_begin("gather%d" % l, [w_in_b[l], misc_b[l], w_o_b[l]], 3, token)
    for l in range(L):
        ada_l, deps = (ada[l] + token[0, 0], (token,)) if l == 0 else (ada[l], ())
        h, sv, Wl, u_next = layer_fwd(h, ada_l, layer_weights(l, deps), tabs, S, u_next, ada[l + 1] if l + 1 < L else None)
        if l + 1 < L:
            gathered[l + 1] = exchange_end(handles[l + 1], h)
        saved.append(sv)
        weights.append(Wl)

    def loss_fn(y, t):
        e = y - t
        return e * (1.0 / D_MODEL), _colsum(e * e)

    dy, sq = rowwise("loss", loss_fn, S, 256, [(h, D_MODEL, 0), (tgt, D_MODEL, 0)], [], [(D_MODEL, F32)], [D_MODEL])
    loss = lax.psum(0.5 * jnp.sum(sq) / D_MODEL, ("x", "y", "c"))
    loss, dy = lax.optimization_barrier((loss, dy))

    grads, d_adas, recv = [None] * L, [None] * L, [None] * L
    pending, token = None, None

    def send_rest(g):
        send_misc = col_gather("unrelayout_misc", [g["w_a_out"], g["w_b_out"], g["w_c_out"], g["w_ukv"], g["w_uq"]],
                               [(NDEV, MISC_ROWS, LANE)], rev_misc)[0]
        return [send_misc, g["w_o"].reshape(NDEV, D_MODEL // NDEV, D_MODEL)]

    rest0 = []

    def early_rest(g):
        handle, tok = exchange_begin("scatter0_rest", send_rest(g), 0, g["w_o"])
        rest0.append(handle)
        return (tok,)

    for l in reversed(range(L)):
        ada_l = ada[l] if token is None else ada[l] + token[0, 0]
        dy, g, d_adas[l] = layer_bwd(dy, saved[l], ada_l, weights[l], tabs, S, early_rest if l == 0 else None)
        grads[l] = g
        if pending is not None:
            recv[l + 1] = exchange_end(pending, dy)
        send_in = col_gather("unrelayout_w_in", [g["w_in"]], [(NDEV, D_MODEL, W_IN_PAD)], rev_in)[0]
        if l == 0:
            def layer_vec(i):
                return jnp.concatenate([grads[i][n].reshape(-1) for n in SMALL] + [d_adas[i].reshape(-1)])

            def to_owners(name, taps):
                full = jnp.stack([grads[i][name][:taps] for i in range(L)])
                return full.reshape(L, taps, NDEV, CW // NDEV).transpose(2, 0, 1, 3).reshape(NDEV, -1)

            conv_send = jnp.concatenate([to_owners("conv_a_w", 31), to_owners("conv_b_w", 3)], axis=1)
            conv_rows = -(-conv_send.shape[1] // (LANE * 16)) * 16
            conv_send = jnp.pad(conv_send, ((0, 0), (0, conv_rows * LANE - conv_send.shape[1])))
            small_sizes = [int(grads[0][n].size) for n in SMALL] + [3 * D_MODEL]
            gsmall, conv_recv = exchange("gather_small_grads", [_rows(jnp.concatenate([layer_vec(i) for i in range(L)]))],
                                         [conv_send.reshape(NDEV, conv_rows, LANE)])
            pending, token = exchange_begin("scatter0", [send_in], 0, gsmall)
        else:
            pending, token = exchange_begin("scatter%d" % l, [send_in] + send_rest(g), 0,
                                            dy if l + 1 == L else recv[l + 1][0])
    grad_x = dy[None]

    gsmall = gsmall + token[0, 0]
    gsum = sum_slots("sum_small", gsmall).reshape(-1)
    recv[0] = [None] + exchange_end(rest0[0], gsum)
    Gr = {}
    offs = np.cumsum([0] + small_sizes)
    per_layer = int(offs[-1])
    gsum = gsum[:L * per_layer].reshape(L, per_layer)
    for i, n in enumerate(SMALL):
        Gr[n] = gsum[:, offs[i]:offs[i + 1]]
    csum = sum_slots("sum_conv", conv_recv + token[0, 0]).reshape(-1)
    n_a = L * 31 * (CW // NDEV)
    Gr["conv_a_w"] = csum[:n_a].reshape(L, 31, CW // NDEV)
    Gr["conv_b_w"] = csum[n_a:n_a + L * 3 * (CW // NDEV)].reshape(L, 3, CW // NDEV)
    Gr["b_ada"] = gsum[:, offs[7]:offs[8]]
    d_ada_all = gsmall.reshape(NDEV, -1)[:, :L * per_layer].reshape(NDEV, L, per_layer)[:, :, offs[7]:offs[8]]
    d_mine = lax.dynamic_slice_in_dim(d_ada_all, me * ncol, ncol, axis=2).reshape(NDEV, L * ncol)
    g_ada = mm(c_act, jnp.pad(d_mine, ((0, 8), (0, 0))).astype(BF), name="mm_gw_ada", trans_a=True)
    Gr["w_ada"] = g_ada.reshape(D_MODEL, L, ncol).transpose(1, 0, 2)

    D, NM, NV = {}, {}, {}
    D["w_ada"], NM["w_ada"], NV["w_ada"] = adamw("adamw_w_ada", P["w_ada"], Gr["w_ada"], Mo["w_ada"], Vo["w_ada"])
    Gr["w_o"], D["w_o"], NM["w_o"], NV["w_o"] = sum_adamw(
        "sum_adamw_w_o", [recv[l][2] for l in range(L)], P["w_o"], Mo["w_o"], Vo["w_o"])
    for n, row0 in (("w_a_out", ROW_A), ("w_b_out", ROW_B), ("w_c_out", ROW_C), ("w_ukv", ROW_UKV), ("w_uq", ROW_UQ)):
        Gr[n], D[n], NM[n], NV[n] = sum_adamw("sum_adamw_" + n, [recv[l][1] for l in range(L)], P[n], Mo[n], Vo[n],
                                              row0=row0)
    w_l, m_l, v_l, _ = lax.optimization_barrier((P["w_in"], Mo["w_in"], Vo["w_in"], token))
    upper = sum_adamw("sum_adamw_w_in_upper", [recv[l][0] for l in range(1, L)], w_l, m_l, v_l, lo=1)
    recv[0][0] = exchange_end(pending, upper[1])[0]
    Gr["w_in"], D["w_in"], NM["w_in"], NV["w_in"] = sum_adamw(
        "sum_adamw_w_in", [recv[0][0]], w_l, m_l, v_l, lo=0, prev=upper)
    for n in ("b_ada", "conv_a_w", "conv_b_w") + SMALL:
        shape = P[n].shape if P[n].ndim == 3 else (1,) + P[n].shape
        res = adamw("adamw_" + n, *[t.reshape(shape) for t in (P[n], Gr[n], Mo[n], Vo[n])])
        D[n], NM[n], NV[n] = [t.reshape(P[n].shape) for t in res]
    return (loss, grad_x, *[Gr[n] for n in ORDER], *[D[n] for n in ORDER], *[NM[n] for n in ORDER],
            *[NV[n] for n in ORDER])
```

```python
import functools
import math

import numpy as np
import jax
import jax.numpy as jnp
from jax import lax
from jax.experimental import pallas as pl
from jax.experimental.pallas import tpu as pltpu

BF = jnp.bfloat16
F32 = jnp.float32
MESH = pl.DeviceIdType.MESH
NDEV = 8

HEADS, NOPE, ROPE, VH = 8, 64, 32, 64
HP = 128
ROPE_THETA = 10000.0
LN_EPS = 1e-5
RMS_EPS = 1e-6
LR, B1, B2, EPS, WD, STEP = 0.001, 0.9, 0.999, 1e-08, 0.01, 10

LANE = 128
VMEM_LIMIT = 56 * 1024 * 1024

D_MODEL, CW, QL, KVL = 1024, 512, 384, 256
OFF_M, OFF_A, OFF_AG, OFF_B, OFF_CG, OFF_GB, OFF_BG = 0, 3072, 4096, 4608, 5632, 6144, 6656
OFF_KV, OFF_KR, OFF_Q, NP = 7168, 7424, 7680, 8192
D_IN = 7840


def _cparams(**kw):
    return pltpu.CompilerParams(vmem_limit_bytes=VMEM_LIMIT, **kw)


def _sigmoid(x):
    return jax.nn.sigmoid(x)


def _silu(x):
    return x * _sigmoid(x)


def _dsilu(x):
    s = _sigmoid(x)
    return s * (1.0 + x * (1.0 - s))


def _pick_tile(n, cap, mult):
    if n <= cap:
        return n
    for t in range(cap - cap % mult, 0, -mult):
        if n % t == 0:
            return t
    raise ValueError((n, cap, mult))


def mm(a, b, *, name, trans_a=False, trans_b=False, out_dtype=F32, bias=None, tm=1024, tn=1024, tk=2048, deps=()):
    if trans_a:
        K, M = a.shape
    else:
        M, K = a.shape
    if trans_b:
        N, K2 = b.shape
    else:
        K2, N = b.shape
    assert K == K2 and not (trans_a and trans_b), (a.shape, b.shape)
    tm, tn = _pick_tile(M, tm, 16), _pick_tile(N, tn, LANE)
    tk = _pick_tile(K, tk, LANE if trans_b else 16)
    assert M % tm == 0 and N % tn == 0 and K % tk == 0, (M, N, K, tm, tn, tk)
    nk = K // tk
    dims = (((0 if trans_a else 1,), (1 if trans_b else 0,)), ((), ()))
    has_bias = bias is not None

    def body(*refs):
        a_ref, b_ref = refs[0], refs[1]
        bias_ref = refs[2] if has_bias else None
        o_ref = refs[(3 if has_bias else 2) + len(deps)]
        p = lax.dot_general(a_ref[...], b_ref[...], dims, preferred_element_type=F32)

        def finish(v):
            if has_bias:
                v = v + bias_ref[...]
            o_ref[...] = v.astype(o_ref.dtype)

        if nk == 1:
            finish(p)
        else:
            acc = refs[-1]
            k = pl.program_id(2)

            @pl.when(k == 0)
            def _():
                acc[...] = p

            @pl.when(k > 0)
            def _():
                acc[...] += p

            @pl.when(k == nk - 1)
            def _():
                finish(acc[...])

    if trans_a:
        a_spec = pl.BlockSpec((tk, tm), lambda i, j, k: (k, i))
    else:
        a_spec = pl.BlockSpec((tm, tk), lambda i, j, k: (i, k))
    if trans_b:
        b_spec = pl.BlockSpec((tn, tk), lambda i, j, k: (j, k))
    else:
        b_spec = pl.BlockSpec((tk, tn), lambda i, j, k: (k, j))
    in_specs = [a_spec, b_spec]
    args = [a, b]
    if has_bias:
        in_specs.append(pl.BlockSpec((1, tn), lambda i, j, k: (0, j)))
        args.append(bias)
    in_specs += [ANY_SPEC] * len(deps)
    args += list(deps)
    return pl.pallas_call(
        body, name=name, grid=(M // tm, N // tn, nk),
        in_specs=in_specs, out_specs=pl.BlockSpec((tm, tn), lambda i, j, k: (i, j)),
        out_shape=jax.ShapeDtypeStruct((M, N), out_dtype),
        scratch_shapes=[pltpu.VMEM((tm, tn), F32)] if nk > 1 else [],
        compiler_params=_cparams(),
    )(*args)


def rowwise(name, fn, S, T, row_ins, full_ins, row_outs, acc_outs=(), into=None):
    n_in = len(row_ins) + len(full_ins)
    n_ro, n_ao = len(row_outs), len(acc_outs)
    alias = into is not None and into[0] is not None
    T = min(T, S)

    def body(*refs):
        vals = [r[...] for r in refs[:n_in]]
        vals = [v.astype(F32) if v.dtype == BF else v for v in vals]
        outs = fn(*vals)
        if not isinstance(outs, (tuple, list)):
            outs = (outs,)
        assert len(outs) == n_ro + n_ao, (name, len(outs))
        o0 = n_in + (1 if alias else 0)
        for r, v in zip(refs[o0:o0 + n_ro], outs[:n_ro]):
            r[...] = v.astype(r.dtype)
        first = pl.program_id(0) == 0
        for r, v in zip(refs[o0 + n_ro:], outs[n_ro:]):
            def init(r=r, v=v):
                r[...] = v

            def accum(r=r, v=v):
                r[...] += v

            pl.when(first)(init)
            pl.when(jnp.logical_not(first))(accum)

    in_specs, args = [], []
    for arr, W, off in row_ins:
        assert off % W == 0 and arr.shape[0] == S, (name, arr.shape, W, off)
        in_specs.append(pl.BlockSpec((T, W), functools.partial(lambda i, cb: (i, cb), cb=off // W)))
        args.append(arr)
    for arr in full_ins:
        in_specs.append(pl.BlockSpec(arr.shape, lambda i: (0, 0)))
        args.append(arr)
    out_specs = [pl.BlockSpec((T, W), lambda i: (i, 0)) for W, _ in row_outs]
    out_shape = [jax.ShapeDtypeStruct((S, W), dt) for W, dt in row_outs]
    aliases = {}
    if into is not None:
        buf, total, off = into
        W0, dt0 = row_outs[0]
        assert off % W0 == 0
        out_specs[0] = pl.BlockSpec((T, W0), functools.partial(lambda i, cb: (i, cb), cb=off // W0))
        out_shape[0] = jax.ShapeDtypeStruct((S, total), dt0)
        if alias:
            in_specs.append(ANY_SPEC)
            args.append(buf)
            aliases = {n_in: 0}
    out_specs += [pl.BlockSpec((1, W), lambda i: (0, 0)) for W in acc_outs]
    out_shape += [jax.ShapeDtypeStruct((1, W), F32) for W in acc_outs]
    return pl.pallas_call(
        body, name=name, grid=(S // T,), in_specs=in_specs, out_specs=out_specs, out_shape=out_shape,
        input_output_aliases=aliases, compiler_params=_cparams(),
    )(*args)


def _colsum(v):
    return jnp.sum(v, axis=0, keepdims=True)


def _ln_stats(r):
    mu = jnp.mean(r, axis=-1, keepdims=True)
    d = r - mu
    var = jnp.mean(d * d, axis=-1, keepdims=True)
    rstd = lax.rsqrt(var + LN_EPS)
    return d * rstd, rstd


def _ln_bwd(dn, n, rstd):
    return rstd * (dn - jnp.mean(dn, axis=-1, keepdims=True) - n * jnp.mean(dn * n, axis=-1, keepdims=True))


CPAD = 32
TC = 64


def _pre(mode, x1, x2):
    return x1 * _sigmoid(x2) if mode == "glu" else x1 * x2


def _sublane_shifts(ext):
    n = TC + CPAD
    return [ext] + [pltpu.roll(ext, n - r, 0) for r in range(1, 8)]


def _shifted(shifts, sft):
    q, r = divmod(sft, 8)
    return shifts[r][8 * q:8 * q + TC]


def _interleaved_specs(S, off):
    return [pl.BlockSpec((S, LANE), functools.partial(lambda j, o: (0, o + 2 * j), o=off // LANE)),
            pl.BlockSpec((S, LANE), functools.partial(lambda j, o: (0, o + 2 * j + 1), o=off // LANE))]


def conv_fwd(name, src, off, w_pad, taps, mode, S, C):
    nchunk = S // TC

    def body(x1_ref, x2_ref, w_ref, o_ref, a_pad):
        a_pad[0:CPAD, :] = jnp.zeros((CPAD, LANE), F32)

        def fill(i, _):
            r = pl.multiple_of(i * 256, 256)
            a_pad[pl.ds(CPAD + r, 256), :] = _pre(mode, x1_ref[pl.ds(r, 256), :].astype(F32),
                                                  x2_ref[pl.ds(r, 256), :].astype(F32))
            return 0

        lax.fori_loop(0, S // 256, fill, 0)

        def chunk(i, _):
            base = pl.multiple_of(i * TC, TC)
            shifts = _sublane_shifts(a_pad[pl.ds(base, TC + CPAD), :])
            acc = jnp.zeros((TC, LANE), F32)
            for k in range(taps):
                acc = acc + w_ref[pl.ds(k, 1), :] * _shifted(shifts, CPAD - (taps - 1) + k)
            o_ref[pl.ds(base, TC), :] = acc
            return 0

        lax.fori_loop(0, nchunk, chunk, 0)

    kp = w_pad.shape[0]
    return pl.pallas_call(
        body, name=name, grid=(C // LANE,),
        in_specs=_interleaved_specs(S, off) + [pl.BlockSpec((kp, LANE), lambda j: (0, j))],
        out_specs=pl.BlockSpec((S, LANE), lambda j: (0, j)),
        out_shape=jax.ShapeDtypeStruct((S, C), F32),
        scratch_shapes=[pltpu.VMEM((S + CPAD, LANE), F32)],
        compiler_params=_cparams(),
    )(src, src, w_pad)


def conv_bwd(name, src, off, dc, w_pad, taps, mode, S, C, buf):
    nchunk = S // TC
    kp = w_pad.shape[0]

    def body(x1_ref, x2_ref, dc_ref, w_ref, _, d_ref, dw_ref, a_pad, dc_pad, dw_acc):
        a_pad[0:CPAD, :] = jnp.zeros((CPAD, LANE), F32)
        dc_pad[S:S + CPAD, :] = jnp.zeros((CPAD, LANE), F32)
        dw_acc[...] = jnp.zeros(dw_acc.shape, F32)

        def fill(i, _):
            r = pl.multiple_of(i * 256, 256)
            a_pad[pl.ds(CPAD + r, 256), :] = _pre(mode, x1_ref[pl.ds(r, 256), :].astype(F32),
                                                  x2_ref[pl.ds(r, 256), :].astype(F32))
            dc_pad[pl.ds(r, 256), :] = dc_ref[pl.ds(r, 256), :]
            return 0

        lax.fori_loop(0, S // 256, fill, 0)

        def chunk(i, _):
            base = pl.multiple_of(i * TC, TC)
            shifts_d = _sublane_shifts(dc_pad[pl.ds(base, TC + CPAD), :])
            shifts_a = _sublane_shifts(a_pad[pl.ds(base, TC + CPAD), :])
            dcv = shifts_d[0][0:TC]
            da = jnp.zeros((TC, LANE), F32)
            for k in range(taps):
                da = da + w_ref[pl.ds(k, 1), :] * _shifted(shifts_d, taps - 1 - k)
                prod = dcv * _shifted(shifts_a, CPAD - (taps - 1) + k)
                fold = prod[0:8]
                for g in range(1, TC // 8):
                    fold = fold + prod[8 * g:8 * g + 8]
                dw_acc[pl.ds(8 * k, 8), :] += fold
            x1 = x1_ref[pl.ds(base, TC), :].astype(F32)
            x2 = x2_ref[pl.ds(base, TC), :].astype(F32)
            if mode == "glu":
                s = _sigmoid(x2)
                d1, d2 = da * s, da * x1 * s * (1.0 - s)
            else:
                d1, d2 = da * x2, da * x1
            d_ref[pl.ds(base, TC), 0:LANE] = d1.astype(BF)
            d_ref[pl.ds(base, TC), LANE:2 * LANE] = d2.astype(BF)
            return 0

        lax.fori_loop(0, nchunk, chunk, 0)
        dw_ref[...] = jnp.zeros(dw_ref.shape, F32)
        for k in range(taps):
            dw_ref[pl.ds(k, 1), :] = jnp.sum(dw_acc[pl.ds(8 * k, 8), :], axis=0, keepdims=True)

    blk = pl.BlockSpec((S, LANE), lambda j: (0, j))
    return pl.pallas_call(
        body, name=name, grid=(C // LANE,),
        in_specs=_interleaved_specs(S, off) + [blk, pl.BlockSpec((kp, LANE), lambda j: (0, j)), ANY_SPEC],
        out_specs=[pl.BlockSpec((S, 2 * LANE), functools.partial(lambda j, o: (0, o + j), o=off // (2 * LANE))),
                   pl.BlockSpec((kp, LANE), lambda j: (0, j))],
        out_shape=[jax.ShapeDtypeStruct(buf.shape, BF), jax.ShapeDtypeStruct((kp, C), F32)],
        input_output_aliases={4: 0},
        scratch_shapes=[pltpu.VMEM((S + CPAD, LANE), F32), pltpu.VMEM((S + CPAD, LANE), F32),
                        pltpu.VMEM((8 * kp, LANE), F32)],
        compiler_params=_cparams(),
    )(src, src, dc, w_pad, buf)


FWD_TILES = (512, 512)
BWD_TILES = (512, 512)
QUADS = HEADS // 4
QW, KVW = 4 * (NOPE + ROPE), 4 * (NOPE + VH)
SCALE = (NOPE + ROPE) ** -0.5
NT_DIMS = (((1,), (1,)), ((), ()))
TN_DIMS = (((0,), (0,)), ((), ()))


def _lane_mask(width, group, dtype):
    lane = lax.broadcasted_iota(jnp.int32, (1, LANE), 1)
    return jnp.where(lane // width == group, 1.0, 0.0).astype(dtype)


def _visible(tq, tk, off):
    row = lax.broadcasted_iota(jnp.int32, (tq, tk), 0)
    col = lax.broadcasted_iota(jnp.int32, (tq, tk), 1)
    return col <= row + off


def _attn_tiles(S, tq, tk):
    tk = tk if S % tk == 0 else 256
    return min(tq, tk), tk


def attn_fwd(q, kv, kpe, S):
    tq, tk = _attn_tiles(S, *FWD_TILES)
    nq = S // tq

    def body(q_ref, kv_ref, kp_ref, o_ref, lse_ref):
        for t in range(2):
            cols = slice(t * LANE, (t + 1) * LANE)
            for hh in range(2):
                def q_block(qi, _, t=t, hh=hh, cols=cols):
                    r0 = qi * tq
                    qcat = jnp.concatenate([q_ref[pl.ds(r0, tq), cols] * _lane_mask(NOPE, hh, BF),
                                            q_ref[pl.ds(r0, tq), 2 * LANE:3 * LANE] * _lane_mask(ROPE, 2 * t + hh, BF)],
                                           axis=1)
                    nfull = (qi * tq) // tk

                    def step(kj, carry, masked):
                        m, l, acc = carry
                        c0 = kj * tk
                        kc = jnp.concatenate([kv_ref[pl.ds(c0, tk), cols], kp_ref[pl.ds(c0, tk), :]], axis=1)
                        vt = kv_ref[pl.ds(c0, tk), (2 + t) * LANE:(3 + t) * LANE]
                        s = lax.dot_general(qcat, kc, NT_DIMS, preferred_element_type=F32) * SCALE
                        if masked:
                            s = jnp.where(_visible(tq, tk, qi * tq - nfull * tk), s, -jnp.inf)
                        m_new = jnp.maximum(m, jnp.max(s, axis=-1, keepdims=True))
                        p = jnp.exp(s - m_new)
                        alpha = jnp.exp(m - m_new)
                        l = alpha * l + jnp.sum(p, axis=-1, keepdims=True)
                        acc = alpha * acc + jnp.dot(p.astype(BF), vt, preferred_element_type=F32)
                        return m_new, l, acc

                    carry = (jnp.full((tq, 1), -jnp.inf, F32), jnp.zeros((tq, 1), F32), jnp.zeros((tq, LANE), F32))
                    for kj in range(nfull):
                        carry = step(kj, carry, False)
                    m, l, acc = step(nfull, carry, True)
                    mine = _lane_mask(NOPE, hh, F32)
                    if hh == 0:
                        o_ref[pl.ds(r0, tq), cols] = (acc / l) * mine
                        lse_ref[pl.ds(r0, tq), cols] = (m + jnp.log(l)) * mine
                    else:
                        o_ref[pl.ds(r0, tq), cols] += (acc / l) * mine
                        lse_ref[pl.ds(r0, tq), cols] += (m + jnp.log(l)) * mine
                    return 0

                for qi in range(nq):
                    q_block(qi, 0)

    return pl.pallas_call(
        body, name="attn_fwd", grid=(QUADS,),
        in_specs=[pl.BlockSpec((S, QW), lambda g: (0, g)), pl.BlockSpec((S, KVW), lambda g: (0, g)),
                  pl.BlockSpec((S, LANE), lambda g: (0, 0))],
        out_specs=[pl.BlockSpec((S, 2 * LANE), lambda g: (0, g))] * 2,
        out_shape=[jax.ShapeDtypeStruct((S, HEADS * VH), F32)] * 2,
        compiler_params=_cparams(),
    )(q, kv, kpe)


def attn_bwd(q, kv, kpe, o, lse, do, S):
    tq, tk = _attn_tiles(S, *BWD_TILES)
    nq = S // tq

    def body(q_ref, kv_ref, kp_ref, o_ref, lse_ref, do_ref, dq_ref, dkv_ref, dkp_ref, dq_acc, dk_acc, dv_acc):
        for t in range(2):
            cols = slice(t * LANE, (t + 1) * LANE)
            dk_acc[...] = jnp.zeros(dk_acc.shape, F32)
            dv_acc[...] = jnp.zeros(dv_acc.shape, F32)
            for hh in range(2):
                def q_block(qi, _, t=t, hh=hh, cols=cols):
                    r0 = pl.multiple_of(qi * tq, tq)
                    mine = _lane_mask(NOPE, hh, F32)
                    qcat = jnp.concatenate([q_ref[pl.ds(r0, tq), cols] * _lane_mask(NOPE, hh, BF),
                                            q_ref[pl.ds(r0, tq), 2 * LANE:3 * LANE] * _lane_mask(ROPE, 2 * t + hh, BF)],
                                           axis=1)
                    dof = do_ref[pl.ds(r0, tq), cols] * mine
                    dob = dof.astype(BF)
                    delta = jnp.sum(dof * o_ref[pl.ds(r0, tq), cols], axis=-1, keepdims=True)
                    lse_h = lse_ref[pl.ds(r0, tq), cols][:, hh * NOPE:hh * NOPE + 1]
                    nfull = (qi * tq) // tk
                    dq_acc[...] = jnp.zeros(dq_acc.shape, F32)

                    def step(kj, _, masked):
                        c0 = pl.multiple_of(kj * tk, tk)
                        kc = jnp.concatenate([kv_ref[pl.ds(c0, tk), cols], kp_ref[pl.ds(c0, tk), :]], axis=1)
                        vt = kv_ref[pl.ds(c0, tk), (2 + t) * LANE:(3 + t) * LANE]
                        s = lax.dot_general(qcat, kc, NT_DIMS, preferred_element_type=F32) * SCALE
                        if masked:
                            s = jnp.where(_visible(tq, tk, qi * tq - nfull * tk), s, -jnp.inf)
                        p = jnp.exp(s - lse_h)
                        dp = lax.dot_general(dob, vt, NT_DIMS, preferred_element_type=F32)
                        ds = (p * (dp - delta) * SCALE).astype(BF)
                        dv_acc[pl.ds(c0, tk), :] += lax.dot_general(p.astype(BF), dob, TN_DIMS,
                                                                    preferred_element_type=F32)
                        dk_acc[pl.ds(c0, tk), :] += lax.dot_general(ds, qcat, TN_DIMS, preferred_element_type=F32)
                        dq_acc[...] += jnp.dot(ds, kc, preferred_element_type=F32)
                        return 0

                    lax.fori_loop(0, nfull, lambda kj, c: step(kj, c, False), 0)
                    step(nfull, 0, True)
                    d = dq_acc[...]
                    pe = d[:, LANE:] * _lane_mask(ROPE, 2 * t + hh, F32)
                    if hh == 0:
                        dq_ref[pl.ds(r0, tq), cols] = d[:, :LANE] * mine
                    else:
                        dq_ref[pl.ds(r0, tq), cols] += d[:, :LANE] * mine
                    if t == 0 and hh == 0:
                        dq_ref[pl.ds(r0, tq), 2 * LANE:3 * LANE] = pe
                    else:
                        dq_ref[pl.ds(r0, tq), 2 * LANE:3 * LANE] += pe
                    return 0

                lax.fori_loop(0, nq, q_block, 0)
            dkv_ref[:, t * LANE:(t + 1) * LANE] = dk_acc[:, :LANE].astype(BF)
            dkv_ref[:, (2 + t) * LANE:(3 + t) * LANE] = dv_acc[...].astype(BF)
            if t == 0:
                dkp_ref[...] = dk_acc[:, LANE:]
            else:
                dkp_ref[...] += dk_acc[:, LANE:]

    qspec = pl.BlockSpec((S, QW), lambda g: (0, g))
    kvspec = pl.BlockSpec((S, KVW), lambda g: (0, g))
    ospec = pl.BlockSpec((S, 2 * LANE), lambda g: (0, g))
    return pl.pallas_call(
        body, name="attn_bwd", grid=(QUADS,),
        in_specs=[qspec, kvspec, pl.BlockSpec((S, LANE), lambda g: (0, 0)), ospec, ospec, ospec],
        out_specs=[qspec, kvspec, pl.BlockSpec((S, LANE), lambda g: (0, g))],
        out_shape=[jax.ShapeDtypeStruct((S, HEADS * (NOPE + ROPE)), F32), jax.ShapeDtypeStruct((S, HEADS * (NOPE + VH)), BF),
                   jax.ShapeDtypeStruct((S, HEADS * ROPE), F32)],
        scratch_shapes=[pltpu.VMEM((tq, 2 * LANE), F32), pltpu.VMEM((S, 2 * LANE), F32), pltpu.VMEM((S, LANE), F32)],
        compiler_params=_cparams(),
    )(q, kv, kpe, o, lse, do)


def exchange(name, gathers, a2as):
    n_g, n = len(gathers), len(gathers) + len(a2as)

    def body(*refs):
        ins, outs = refs[:n], refs[n:2 * n]
        send_sems, recv_sems, loc_sems = refs[2 * n:]
        x, y, c = lax.axis_index("x"), lax.axis_index("y"), lax.axis_index("c")
        me = 4 * x + 2 * y + c

        def peer(k):
            px = 1 - x if k & 4 else x
            py = 1 - y if k & 2 else y
            pc = 1 - c if k & 1 else c
            return (px, py, pc), 4 * px + 2 * py + pc

        def remote(a, k):
            pid, pflat = peer(k)
            src = ins[a] if a < n_g else ins[a].at[pflat]
            return pltpu.make_async_remote_copy(
                src_ref=src, dst_ref=outs[a].at[me], send_sem=send_sems.at[a, k - 1], recv_sem=recv_sems.at[a, k - 1],
                device_id=pid, device_id_type=MESH)

        def arrival(a, k):
            pid, pflat = peer(k)
            src = ins[a] if a < n_g else ins[a].at[pflat]
            return pltpu.make_async_remote_copy(
                src_ref=src, dst_ref=outs[a].at[pflat], send_sem=send_sems.at[a, k - 1], recv_sem=recv_sems.at[a, k - 1],
                device_id=pid, device_id_type=MESH)

        local = []
        for a in range(n):
            own = ins[a] if a < n_g else ins[a].at[me]
            cp = pltpu.make_async_copy(own, outs[a].at[me], loc_sems.at[a])
            cp.start()
            local.append(cp)
        sent = []
        for k in (1, 2, 4, 3, 5, 6, 7):
            for a in range(n):
                cp = remote(a, k)
                cp.start()
                sent.append(cp)
        for k in range(1, 8):
            for a in range(n):
                arrival(a, k).wait_recv()
        for cp in sent:
            cp.wait_send()
        for cp in local:
            cp.wait()

    out_shape = [jax.ShapeDtypeStruct((NDEV,) + g.shape, g.dtype) for g in gathers]
    out_shape += [jax.ShapeDtypeStruct(a.shape, a.dtype) for a in a2as]
    any_spec = pl.BlockSpec(memory_space=pl.ANY)
    return pl.pallas_call(
        body, name=name, in_specs=[any_spec] * n, out_specs=[any_spec] * n, out_shape=out_shape,
        scratch_shapes=[pltpu.SemaphoreType.DMA((n, NDEV - 1)), pltpu.SemaphoreType.DMA((n, NDEV - 1)),
                        pltpu.SemaphoreType.DMA((n,))],
    )(*gathers, *a2as)


def gather_two_level(name, block, dep):
    def body(x_ref, _, out_ref, stage, send_sems, recv_sems, loc_sem):
        x, y, c = lax.axis_index("x"), lax.axis_index("y"), lax.axis_index("c")
        me, sibling = (x, y, c), (x, y, 1 - c)
        chips = [(1 - x, y), (x, 1 - y), (1 - x, 1 - y)]

        def slot(px, py, pc):
            return out_ref.at[4 * px + 2 * py + pc]

        def copy(k, owner, to, src=None):
            return pltpu.make_async_remote_copy(
                src_ref=slot(*owner) if src is None else src, dst_ref=slot(*owner), send_sem=send_sems.at[k],
                recv_sem=recv_sems.at[k], device_id=to, device_id_type=MESH)

        load = pltpu.make_async_copy(x_ref, stage, loc_sem)
        load.start()
        first = [copy(0, me, sibling, src=x_ref)] + [copy(1 + j, me, (*chip, c), src=x_ref) for j, chip in enumerate(chips)]
        for cp in first:
            cp.start()
        load.wait()
        store = pltpu.make_async_copy(stage, slot(*me), loc_sem)
        store.start()
        passed = [copy(4 + j, (*chip, c), sibling) for j, chip in enumerate(chips)]
        for j, chip in enumerate(chips):
            copy(1 + j, (*chip, c), me).wait_recv()
            passed[j].start()
        copy(0, sibling, me).wait_recv()
        for j, chip in enumerate(chips):
            copy(4 + j, (*chip, 1 - c), me).wait_recv()
        for cp in first + passed:
            cp.wait_send()
        store.wait()

    return pl.pallas_call(
        body, name=name, in_specs=[pl.BlockSpec(memory_space=pl.ANY)] * 2, out_specs=pl.BlockSpec(memory_space=pl.ANY),
        out_shape=jax.ShapeDtypeStruct((NDEV,) + block.shape, block.dtype),
        scratch_shapes=[pltpu.VMEM(block.shape, block.dtype), pltpu.SemaphoreType.DMA((NDEV - 1,)),
                        pltpu.SemaphoreType.DMA((NDEV - 1,)), pltpu.SemaphoreType.DMA],
        compiler_params=_cparams(),
    )(block, dep)


def _peer(k, x, y, c):
    px = 1 - x if k & 4 else x
    py = 1 - y if k & 2 else y
    pc = 1 - c if k & 1 else c
    return (px, py, pc), 4 * px + 2 * py + pc


PEER_ORDER = (1, 2, 4, 3, 5, 6, 7)
HBM_SPEC = pl.BlockSpec(memory_space=pltpu.HBM)
SEM_SPEC = pl.BlockSpec(memory_space=pltpu.SEMAPHORE)
ANY_SPEC = pl.BlockSpec(memory_space=pl.ANY)


def _split_copies(ins, lands, n_g, send_sems, recv_sems):
    x, y, c = lax.axis_index("x"), lax.axis_index("y"), lax.axis_index("c")
    me = 4 * x + 2 * y + c

    def outgoing(a, k):
        pid, pflat = _peer(k, x, y, c)
        src = ins[a] if a < n_g else ins[a].at[pflat]
        return pltpu.make_async_remote_copy(
            src_ref=src, dst_ref=lands[a].at[me], send_sem=send_sems.at[a * (NDEV - 1) + k - 1],
            recv_sem=recv_sems.at[a * (NDEV - 1) + k - 1],
            device_id=pid, device_id_type=MESH)

    def arrival(a, k):
        pid, pflat = _peer(k, x, y, c)
        src = ins[a] if a < n_g else ins[a].at[pflat]
        return pltpu.make_async_remote_copy(
            src_ref=src, dst_ref=lands[a].at[pflat], send_sem=send_sems.at[a * (NDEV - 1) + k - 1],
            recv_sem=recv_sems.at[a * (NDEV - 1) + k - 1],
            device_id=pid, device_id_type=MESH)

    return outgoing, arrival


def exchange_begin(name, srcs, n_g, dep):
    n = len(srcs)
    land_shapes = [((NDEV,) + s.shape) if a < n_g else s.shape for a, s in enumerate(srcs)]

    def own_body(*refs):
        ins, outs = refs[:n], refs[n + 1:2 * n + 1]
        stage, sems = refs[2 * n + 1:3 * n + 1], refs[-1]
        me = 4 * lax.axis_index("x") + 2 * lax.axis_index("y") + lax.axis_index("c")
        cps = [pltpu.make_async_copy(ins[a] if a < n_g else ins[a].at[me], stage[a], sems.at[a]) for a in range(n)]
        for cp in cps:
            cp.start()
        for cp in cps:
            cp.wait()
        cps = [pltpu.make_async_copy(stage[a], outs[a].at[me], sems.at[a]) for a in range(n)]
        for cp in cps:
            cp.start()
        for cp in cps:
            cp.wait()

    lands = pl.pallas_call(
        own_body, name=name + "_own", in_specs=[ANY_SPEC] * (n + 1), out_specs=[ANY_SPEC] * n,
        out_shape=[jax.ShapeDtypeStruct(sh, s.dtype) for sh, s in zip(land_shapes, srcs)],
        scratch_shapes=[pltpu.VMEM(sh[1:], s.dtype) for sh, s in zip(land_shapes, srcs)] + [pltpu.SemaphoreType.DMA((n,))],
        compiler_params=_cparams(),
    )(*srcs, dep)

    def start_body(*refs):
        ins, lz = refs[:n], refs[n:2 * n]
        send_sems, recv_sems, token = refs[2 * n], refs[2 * n + 1], refs[-1]
        outgoing, _ = _split_copies(ins, lz, n_g, send_sems, recv_sems)
        for k in PEER_ORDER:
            for a in range(n):
                outgoing(a, k).start()
        token[...] = jnp.zeros(token.shape, F32)

    hbm = lambda t: pltpu.HBM(t.shape, t.dtype)
    res = pl.pallas_call(
        start_body, name=name + "_start",
        out_shape=(pltpu.SemaphoreType.DMA((n * (NDEV - 1),)), pltpu.SemaphoreType.DMA((n * (NDEV - 1),)),
                   *[hbm(s) for s in srcs], *[hbm(t) for t in lands], jax.ShapeDtypeStruct((8, LANE), F32)),
        in_specs=[HBM_SPEC] * (2 * n),
        out_specs=(SEM_SPEC, SEM_SPEC, *[HBM_SPEC] * (2 * n), pl.BlockSpec(memory_space=pltpu.VMEM)),
        input_output_aliases={i: 2 + i for i in range(2 * n)},
        compiler_params=pltpu.CompilerParams(has_side_effects=pltpu.SideEffectType.DATAFLOW_SIDE_EFFECTING),
    )(*[pltpu.with_memory_space_constraint(t, pltpu.HBM) for t in list(srcs) + list(lands)])
    return (name, n, n_g, res[:-1]), res[-1]


def exchange_end(handle, after):
    name, n, n_g, (send_sems, recv_sems, *bufs) = handle

    def wait_body(*refs):
        ins, lz = refs[:n], refs[n:2 * n]
        ss, rs = refs[2 * n], refs[2 * n + 1]
        outgoing, arrival = _split_copies(ins, lz, n_g, ss, rs)
        for k in range(1, NDEV):
            for a in range(n):
                arrival(a, k).wait_recv()
        for k in range(1, NDEV):
            for a in range(n):
                outgoing(a, k).wait_send()

    res = pl.pallas_call(
        wait_body, name=name + "_wait", out_shape=tuple(pltpu.HBM(t.shape, t.dtype) for t in bufs),
        in_specs=[HBM_SPEC] * (2 * n) + [SEM_SPEC, SEM_SPEC, ANY_SPEC], out_specs=[HBM_SPEC] * (2 * n),
        input_output_aliases={i: i for i in range(2 * n)},
        compiler_params=pltpu.CompilerParams(has_side_effects=pltpu.SideEffectType.DATAFLOW_SIDE_EFFECTING),
    )(*bufs, send_sems, recv_sems, after)
    return list(res[n:])


def _pick_rows(R, mult, cap):
    best = None
    for n in range(1, R + 1):
        if R % n == 0 and (R // n) % mult == 0 and R // n <= cap:
            best = R // n
            break
    assert best is not None, (R, mult, cap)
    return best


def sum_slots(name, x):
    _, R, _ = x.shape
    tr = _pick_rows(R, 16, 2304)

    def body(x_ref, o_ref):
        acc = x_ref[0].astype(F32)
        for d in range(1, NDEV):
            acc = acc + x_ref[d].astype(F32)
        o_ref[...] = acc

    return pl.pallas_call(
        body, name=name, grid=(R // tr,),
        in_specs=[pl.BlockSpec((NDEV, tr, LANE), lambda i: (0, i, 0))],
        out_specs=pl.BlockSpec((tr, LANE), lambda i: (i, 0)),
        out_shape=jax.ShapeDtypeStruct((R, LANE), F32), compiler_params=_cparams(),
    )(x)


def adamw(name, w, g, m, v):
    L, R, C = w.shape
    tr = _pick_rows(R, 8, 256) if R % 8 == 0 else R

    def body(w_ref, g_ref, m_ref, v_ref, d_ref, nm_ref, nv_ref):
        gg = g_ref[...]
        nm = B1 * m_ref[...] + (1.0 - B1) * gg
        nv = B2 * v_ref[...] + (1.0 - B2) * jnp.square(gg)
        m_hat = nm / (1.0 - B1 ** STEP)
        v_hat = nv / (1.0 - B2 ** STEP)
        d_ref[...] = -LR * (m_hat / (jnp.sqrt(v_hat) + EPS) + WD * w_ref[...])
        nm_ref[...] = nm
        nv_ref[...] = nv

    blk = pl.BlockSpec((1, tr, C), lambda l, i: (l, i, 0))
    shp = jax.ShapeDtypeStruct(w.shape, F32)
    return pl.pallas_call(
        body, name=name, grid=(L, R // tr), in_specs=[blk] * 4, out_specs=[blk] * 3, out_shape=[shp] * 3,
        compiler_params=_cparams(),
    )(w, g, m, v)


IN_SHARD = D_IN // NDEV
UQ_SHARD = HEADS * (NOPE + ROPE) // NDEV
W_IN_PAD = 1024
ROW_A, ROW_B, ROW_C, ROW_UKV, ROW_UQ, MISC_ROWS = 0, 512, 1024, 1536, 1792, 2176


def _in_perm_index():
    ar = np.arange
    z = lambda n: np.full((n,), -1, np.int64)
    mix = lambda lo1, lo2: np.concatenate([ar(lo + LANE * j, lo + LANE * (j + 1)) for j in range(CW // LANE)
                                           for lo in (lo1, lo2)])
    return np.concatenate([ar(4768, 7840), mix(0, 512), ar(1024, 1536), mix(1536, 2560), ar(4256, 4768), ar(2048, 2560),
                           ar(3072, 3584), ar(3968, 4224), ar(4224, 4256), z(OFF_Q - OFF_KR - ROPE), ar(3584, 3968),
                           z(NP - OFF_Q - QL)])


def _head_perm_index(a, b):
    parts = []
    for g in range(QUADS):
        h = np.arange(4 * g, 4 * g + 4)[:, None] * (a + b)
        parts += [(h + np.arange(a)[None]).reshape(-1), (h + a + np.arange(b)[None]).reshape(-1)]
    return np.concatenate(parts)


def _inverse(perm, n):
    inv = np.full((n,), -1, np.int64)
    inv[perm[perm >= 0]] = np.nonzero(perm >= 0)[0]
    return inv


IN_PERM = _in_perm_index()
UQ_PERM = _head_perm_index(NOPE, ROPE)
UKV_PERM = _head_perm_index(NOPE, VH)


def _to_gathered(perm, shard, pad):
    return np.where(perm >= 0, (perm // shard) * pad + perm % shard, -1)


def _from_full(inv, shard, pad):
    j, i = np.divmod(np.arange(NDEV * pad), pad)
    return np.where(i < shard, inv[np.minimum(j * shard + i, inv.shape[0] - 1)], -1)


def col_gather(name, srcs, out_shapes, jobs, deps=()):
    ns, nj, nd, no = len(srcs), len(jobs), len(deps), len(out_shapes)
    tables = [jnp.asarray(np.asarray(job[5], np.int32)[None, :]) for job in jobs]

    def view(ref, col0, width, r0, rc):
        n = ref.shape[-1]
        if len(ref.shape) == 3:
            return ref.at[col0 // n, pl.ds(r0, rc), pl.ds(col0 % n, width)]
        return ref.at[pl.ds(r0, rc), pl.ds(col0, width)]

    def slabs(shape):
        if len(shape) == 3:
            return [((d,), d * shape[2], (d + 1) * shape[2]) for d in range(shape[0])]
        w = 1024 if shape[1] > 1024 and shape[1] % 1024 == 0 else shape[1]
        return [((slice(None), pl.ds(c, w)), c, c + w) for c in range(0, shape[1], w)]

    src_slabs = [slabs(s.shape) for s in srcs]
    out_slabs = [slabs(sh) for sh in out_shapes]
    work, first_use, last_touch = [], {}, {}
    for ji, (si, srow, oi, orow, nrows, tgt) in enumerate(jobs):
        tgt = np.asarray(tgt)
        tw = 256 if out_shapes[oi][-1] % 256 == 0 else LANE
        sw = 256 if srcs[si].shape[-1] % 256 == 0 else LANE
        for t in range(tgt.shape[0] // tw):
            tt = tgt[t * tw:(t + 1) * tw]
            tiles = sorted(set((tt[tt >= 0] // sw).tolist()))
            straight = bool(tiles) and tt[0] >= 0 and tt[0] % LANE == 0 and np.array_equal(tt, tt[0] + np.arange(tw))
            cols = [(int(tt[0]) + k * LANE, LANE) for k in range(tw // LANE)] if straight else [(s * sw, sw) for s in tiles]
            need = sorted({(si, k) for c0, _ in cols for k, (_, lo, hi) in enumerate(src_slabs[si]) if lo <= c0 < hi})
            touch = [(oi, k) for k, (_, lo, hi) in enumerate(out_slabs[oi]) if lo <= t * tw < hi][0]
            for key in need:
                first_use.setdefault(key, len(work))
            last_touch[touch] = len(work)
            work.append((ji, t, tw, sw, tiles, straight, need, touch))
    in_order = sorted(first_use, key=first_use.get)
    in_sem = {key: i for i, key in enumerate(in_order)}
    out_keys = sorted(last_touch)
    out_sem = {key: i for i, key in enumerate(out_keys)}

    def body(*refs):
        src_hbm, tab_refs = refs[:ns], refs[ns:ns + nj]
        out_hbm = refs[ns + nj + nd:ns + nj + nd + no]
        scratch = refs[ns + nj + nd + no:]
        src_refs, out_refs, in_sems, out_sems = scratch[:ns], scratch[ns:ns + no], scratch[-2], scratch[-1]
        loads = {}
        for key in in_order:
            si, k = key
            idx = src_slabs[si][k][0]
            loads[key] = pltpu.make_async_copy(src_hbm[si].at[idx], src_refs[si].at[idx], in_sems.at[in_sem[key]])
            loads[key].start()
        arrived, stores = set(), []
        for wi, (ji, t, tw, sw, tiles, straight, need, touch) in enumerate(work):
            si, srow, oi, orow, nrows, tgt = jobs[ji]
            sref, oref = src_refs[si], out_refs[oi]
            rc = nrows if nrows <= 1024 else 1024
            for key in need:
                if key not in arrived:
                    loads[key].wait()
                    arrived.add(key)
            onehots = []
            if tiles and not straight:
                want = tab_refs[ji][:, t * tw:(t + 1) * tw]
                row = lax.broadcasted_iota(jnp.int32, (sw, tw), 0)
                onehots = [jnp.where(want == row + s * sw, 1.0, 0.0).astype(BF) for s in tiles]
            first = int(np.asarray(tgt)[t * tw])

            def chunk(ci, _, t=t, tw=tw, sw=sw, tiles=tiles, straight=straight, onehots=onehots, first=first,
                      sref=sref, oref=oref, srow=srow, orow=orow, rc=rc):
                r0 = ci * rc
                ro = pl.multiple_of(orow + r0, LANE)
                rs = pl.multiple_of(srow + r0, LANE)
                if not tiles:
                    view(oref, t * tw, tw, ro, rc)[...] = jnp.zeros((rc, tw), BF)
                elif straight:
                    for k in range(tw // LANE):
                        view(oref, t * tw + k * LANE, LANE, ro, rc)[...] = view(sref, first + k * LANE, LANE, rs, rc)[...]
                else:
                    acc = None
                    for s, oh in zip(tiles, onehots):
                        p = jnp.dot(view(sref, s * sw, sw, rs, rc)[...], oh, preferred_element_type=F32)
                        acc = p if acc is None else acc + p
                    view(oref, t * tw, tw, ro, rc)[...] = acc.astype(BF)
                return 0

            lax.fori_loop(0, nrows // rc, chunk, 0)
            if last_touch[touch] == wi:
                idx = out_slabs[touch[0]][touch[1]][0]
                cp = pltpu.make_async_copy(out_refs[touch[0]].at[idx], out_hbm[touch[0]].at[idx], out_sems.at[out_sem[touch]])
                cp.start()
                stores.append(cp)
        for cp in stores:
            cp.wait()

    return pl.pallas_call(
        body, name=name, in_specs=[ANY_SPEC] * ns + [pl.BlockSpec(memory_space=pltpu.VMEM)] * nj + [ANY_SPEC] * nd,
        out_specs=[ANY_SPEC] * no, out_shape=[jax.ShapeDtypeStruct(s, BF) for s in out_shapes],
        scratch_shapes=[pltpu.VMEM(s.shape, BF) for s in srcs] + [pltpu.VMEM(s, BF) for s in out_shapes]
        + [pltpu.SemaphoreType.DMA((len(in_order),)), pltpu.SemaphoreType.DMA((len(out_keys),))],
        compiler_params=_cparams(),
    )(*srcs, *tables, *deps)


def sum_adamw(name, recvs, w, m, v, lo=0, prev=None, row0=0):
    _, R, C = w.shape
    L = len(recvs)
    CP = recvs[0].shape[-1]
    tr = _pick_rows(R, 16, 128)
    n_prev = 0 if prev is None else 4

    def body(*refs):
        r_refs = refs[:L]
        w_ref, m_ref, v_ref = refs[L:L + 3]
        g_ref, d_ref, nm_ref, nv_ref, gsum = refs[L + 3 + n_prev:]
        layer = pl.program_id(0)
        for k in range(L):
            def total(k=k):
                acc = r_refs[k][0].astype(F32)
                for d in range(1, NDEV):
                    acc = acc + r_refs[k][d].astype(F32)
                gsum[...] = acc
            pl.when(layer == k)(total)
        gg = gsum[:, 0:C]
        nm = B1 * m_ref[...] + (1.0 - B1) * gg
        nv = B2 * v_ref[...] + (1.0 - B2) * jnp.square(gg)
        m_hat = nm / (1.0 - B1 ** STEP)
        v_hat = nv / (1.0 - B2 ** STEP)
        g_ref[...] = gg
        d_ref[...] = -LR * (m_hat / (jnp.sqrt(v_hat) + EPS) + WD * w_ref[...])
        nm_ref[...] = nm
        nv_ref[...] = nv

    assert row0 % tr == 0
    r_specs = [pl.BlockSpec((NDEV, tr, CP),
                            functools.partial(lambda l, i, k: (0, row0 // tr + jnp.where(l == k, i, 0), 0), k=k))
               for k in range(L)]
    blk = pl.BlockSpec((None, tr, C), lambda l, i: (l + lo, i, 0))
    shp = jax.ShapeDtypeStruct(w.shape, F32)
    return pl.pallas_call(
        body, name=name, grid=(L, R // tr), in_specs=r_specs + [blk] * 3 + [ANY_SPEC] * n_prev, out_specs=[blk] * 4,
        out_shape=[shp] * 4, input_output_aliases={L + 3 + i: i for i in range(n_prev)},
        scratch_shapes=[pltpu.VMEM((tr, CP), F32)], compiler_params=_cparams(),
    )(*recvs, w, m, v, *(prev or ()))


ALPHA = 8.0 ** 0.25
T_WIDE, T_NARROW = 512, 1024


def _rope_fn(sign):
    def fn(x, cos, sin):
        W = x.shape[-1]
        lane = lax.broadcasted_iota(jnp.int32, x.shape, 1)
        first_half = (lane % ROPE) < (ROPE // 2)
        rot = jnp.where(first_half, -pltpu.roll(x, W - ROPE // 2, 1), pltpu.roll(x, ROPE // 2, 1))
        return x * cos + sign * rot * sin
    return fn


def _modulate(xv, a):
    return xv * (1.0 + a[1:2, :]) + a[0:1, :]


def layer_fwd(x, ada3, W, tabs, S, u=None, ada_next=None):
    cos, sin = tabs
    T = T_NARROW
    if u is None:
        u = rowwise("modulate", _modulate, S, T, [(x, D_MODEL, 0)], [ada3], [(D_MODEL, BF)])[0]
    proj = mm(u, W["in"], name="mm_proj", tm=1024, tn=1024, out_dtype=BF)
    W = {**W, **W["late"](proj)}

    ca = conv_fwd("conv_a_fwd", proj, OFF_A, W["conv_a"], 31, "glu", S, CW)

    def a_post(c, ag, vec):
        n, _ = _ln_stats(c + vec[0:1, :])
        return _silu(n * vec[1:2, :] + vec[2:3, :]) * _silu(ag)

    h_a = rowwise("mix_a_post", a_post, S, T, [(ca, CW, 0), (proj, CW, OFF_AG)], [W["vec_a"]], [(CW, BF)])[0]
    y_a = mm(h_a, W["a_out"], name="mm_branch_out", out_dtype=BF)

    cb = conv_fwd("conv_b_fwd", proj, OFF_B, W["conv_b"], 3, "mul", S, CW)
    h_b = rowwise("mix_b_post", lambda c, gb, bg: gb * c * _silu(bg), S, T,
                  [(cb, CW, 0), (proj, CW, OFF_GB), (proj, CW, OFF_BG)], [], [(CW, BF)])[0]
    y_b = mm(h_b, W["b_out"], name="mm_branch_out", out_dtype=BF)

    def rms2(ql, kvl, gq, gkv):
        rq = lax.rsqrt(jnp.mean(ql * ql, axis=-1, keepdims=True) + RMS_EPS)
        rk = lax.rsqrt(jnp.mean(kvl * kvl, axis=-1, keepdims=True) + RMS_EPS)
        return ql * rq * gq, kvl * rk * gkv

    qn, kvn = rowwise("rms_fwd", rms2, S, T, [(proj, QL, OFF_Q), (proj, KVL, OFF_KV)], [W["gq"], W["gkv"]],
                      [(QL, BF), (KVL, BF)])
    q = mm(qn, W["uq"], name="mm_q")
    kv = mm(kvn, W["ukv"], name="mm_kv", out_dtype=BF)
    rope = _rope_fn(1.0)

    def rope_fwd(qv, kr, c1, s1):
        parts = []
        for g in range(QUADS):
            parts.append(qv[:, g * QW:g * QW + 2 * LANE].astype(BF))
            parts.append(rope(qv[:, g * QW + 2 * LANE:(g + 1) * QW], c1, s1).astype(BF))
        kp = rope(kr, c1, s1)
        kp = kp + pltpu.roll(kp, ROPE, 1) + pltpu.roll(kp, 2 * ROPE, 1) + pltpu.roll(kp, 3 * ROPE, 1)
        return jnp.concatenate(parts, axis=1), kp

    q_b, kpe = rowwise("rope_fwd", rope_fwd, S, T,
                       [(q, HEADS * (NOPE + ROPE), 0), (proj, LANE, OFF_KR), (cos, LANE, 0), (sin, LANE, 0)], [],
                       [(HEADS * (NOPE + ROPE), BF), (LANE, BF)])
    o, lse = attn_fwd(q_b, kv, kpe, S)
    h_c = rowwise("mix_c_post", lambda ov, cg: ov * _silu(cg), S, T, [(o, CW, 0), (proj, CW, OFF_CG)], [],
                  [(CW, BF)])[0]
    y_c = mm(h_c, W["c_out"], name="mm_branch_out", out_dtype=BF)

    def merge(la, lb, lc, ya, yb, yc):
        return _sigmoid(la) * ya + _sigmoid(lb) * yb + _sigmoid(lc) * yc

    m = rowwise("merge_fwd", merge, S, T_WIDE,
                [(proj, D_MODEL, 0), (proj, D_MODEL, 1024), (proj, D_MODEL, 2048), (y_a, D_MODEL, 0),
                 (y_b, D_MODEL, 0), (y_c, D_MODEL, 0)], [], [(D_MODEL, BF)])[0]
    out = mm(m, W["o"], name="mm_out")

    def ln_fwd(xv, ov, a, lnv, *nxt):
        n, _ = _ln_stats(ALPHA * xv + a[2:3, :] * ov)
        y = n * lnv[0:1, :] + lnv[1:2, :]
        return (y, _modulate(y, nxt[0])) if nxt else y

    res = rowwise("ln_fwd", ln_fwd, S, T_WIDE, [(x, D_MODEL, 0), (out, D_MODEL, 0)],
                  [ada3, W["lnv"]] + ([] if ada_next is None else [ada_next]),
                  [(D_MODEL, F32)] + ([] if ada_next is None else [(D_MODEL, BF)]))
    saved = dict(x=x, u=u, proj=proj, ca=ca, cb=cb, h_a=h_a, h_b=h_b, h_c=h_c, y_a=y_a, y_b=y_b, y_c=y_c, qn=qn,
                 kvn=kvn, q_b=q_b, kv=kv, kpe=kpe, lse=lse, o=o, m=m, out=out)
    return res[0], saved, W, (res[1] if ada_next is not None else None)


def layer_bwd(dxn, sv, ada3, W, tabs, S, before_in=None):
    cos, sin = tabs
    T = T_NARROW
    x, proj = sv["x"], sv["proj"]
    G = {}

    def ln_bwd(xv, ov, dy, a, lnv):
        gate = a[2:3, :]
        n, rstd = _ln_stats(ALPHA * xv + gate * ov)
        dr = _ln_bwd(dy * lnv[0:1, :], n, rstd)
        return ALPHA * dr, gate * dr, _colsum(dy * n), _colsum(dy), _colsum(dr * ov)

    dres, d_out, G["ln_g"], G["ln_b"], d_gate = rowwise(
        "ln_bwd", ln_bwd, S, T_WIDE, [(x, D_MODEL, 0), (sv["out"], D_MODEL, 0), (dxn, D_MODEL, 0)], [ada3, W["lnv"]],
        [(D_MODEL, F32), (D_MODEL, BF)], [D_MODEL] * 3)
    dm = mm(d_out, W["o"], name="mm_dm", trans_b=True, out_dtype=BF)
    G["w_o"] = mm(sv["m"], d_out, name="mm_gw_o", trans_a=True, out_dtype=BF)

    def merge_bwd(dmv, la, lb, lc, ya, yb, yc):
        outs, dls = [], []
        for lg, yv in ((la, ya), (lb, yb), (lc, yc)):
            s = _sigmoid(lg)
            outs.append(dmv * s)
            dls.append((dmv * yv * s * (1.0 - s)).astype(BF))
        return (jnp.concatenate(dls, axis=1),) + tuple(outs)

    d_proj, dy_a, dy_b, dy_c = rowwise(
        "merge_bwd", merge_bwd, S, T_WIDE,
        [(dm, D_MODEL, 0), (proj, D_MODEL, 0), (proj, D_MODEL, 1024), (proj, D_MODEL, 2048), (sv["y_a"], D_MODEL, 0),
         (sv["y_b"], D_MODEL, 0), (sv["y_c"], D_MODEL, 0)], [], [(3 * D_MODEL, BF)] + [(D_MODEL, BF)] * 3,
        into=(None, NP, OFF_M))

    dh = {}
    for br, dy in (("a", dy_a), ("b", dy_b), ("c", dy_c)):
        dh[br] = mm(dy, W[br + "_out"], name="mm_dh", trans_b=True, out_dtype=BF)
        G["w_%s_out" % br] = mm(sv["h_" + br], dy, name="mm_gw_branch", trans_a=True, out_dtype=BF)

    def a_post_bwd(c, ag, dhv, vec):
        n, rstd = _ln_stats(c + vec[0:1, :])
        z = n * vec[1:2, :] + vec[2:3, :]
        d_ag = dhv * _silu(z) * _dsilu(ag)
        dz = dhv * _silu(ag) * _dsilu(z)
        dc = _ln_bwd(dz * vec[1:2, :], n, rstd)
        return d_ag, dc, _colsum(dc), _colsum(dz * n), _colsum(dz)

    d_proj, dca, G["conv_a_b"], G["ln_a_g"], G["ln_a_b"] = rowwise(
        "mix_a_post_bwd", a_post_bwd, S, T, [(sv["ca"], CW, 0), (proj, CW, OFF_AG), (dh["a"], CW, 0)], [W["vec_a"]],
        [(CW, BF), (CW, F32)], [CW] * 3, into=(d_proj, NP, OFF_AG))
    d_proj, G["conv_a_w"] = conv_bwd("conv_a_bwd", proj, OFF_A, dca, W["conv_a"], 31, "glu", S, CW, d_proj)

    def b_post_bwd(c, gb, bg, dhv):
        sg = _silu(bg)
        d_gb_bg = jnp.concatenate([(dhv * sg * c).astype(BF), (dhv * gb * c * _dsilu(bg)).astype(BF)], axis=1)
        return d_gb_bg, dhv * sg * gb

    d_proj, dcb = rowwise("mix_b_post_bwd", b_post_bwd, S, T,
                          [(sv["cb"], CW, 0), (proj, CW, OFF_GB), (proj, CW, OFF_BG), (dh["b"], CW, 0)], [],
                          [(2 * CW, BF), (CW, F32)], into=(d_proj, NP, OFF_GB))
    d_proj, G["conv_b_w"] = conv_bwd("conv_b_bwd", proj, OFF_B, dcb, W["conv_b"], 3, "mul", S, CW, d_proj)

    d_proj, d_o = rowwise("mix_c_post_bwd", lambda ov, cg, dhv: (dhv * ov * _dsilu(cg), dhv * _silu(cg)), S, T,
                          [(sv["o"], CW, 0), (proj, CW, OFF_CG), (dh["c"], CW, 0)], [], [(CW, BF), (CW, F32)],
                          into=(d_proj, NP, OFF_CG))
    dq, d_kv, dkp_heads = attn_bwd(sv["q_b"], sv["kv"], sv["kpe"], sv["o"], sv["lse"], d_o, S)
    ropeT = _rope_fn(-1.0)

    def rope_bwd(dqv, dkp, c1, s1):
        parts = []
        for g in range(QUADS):
            parts.append(dqv[:, g * QW:g * QW + 2 * LANE].astype(BF))
            parts.append(ropeT(dqv[:, g * QW + 2 * LANE:(g + 1) * QW], c1, s1).astype(BF))
        f = dkp[:, :LANE] + dkp[:, LANE:]
        f = f + pltpu.roll(f, 64, 1)
        f = f + pltpu.roll(f, 32, 1)
        lane = lax.broadcasted_iota(jnp.int32, f.shape, 1)
        return jnp.concatenate(parts, axis=1), jnp.where(lane < ROPE, ropeT(f, c1, s1), 0.0)

    d_q, dk_pe = rowwise("rope_bwd", rope_bwd, S, T,
                         [(dq, HEADS * (NOPE + ROPE), 0), (dkp_heads, HEADS * ROPE, 0), (cos, LANE, 0), (sin, LANE, 0)],
                         [], [(HEADS * (NOPE + ROPE), BF), (LANE, BF)])
    d_qn = mm(d_q, W["uq"], name="mm_dqn", trans_b=True, out_dtype=BF)
    d_kvn = mm(d_kv, W["ukv"], name="mm_dkvn", trans_b=True, out_dtype=BF)
    G["w_uq"] = mm(sv["qn"], d_q, name="mm_gw_uq", trans_a=True, out_dtype=BF)
    G["w_ukv"] = mm(sv["kvn"], d_kv, name="mm_gw_ukv", trans_a=True, out_dtype=BF)

    def rms_bwd(ql, kvl, dqn, dkn, dkp, gq, gkv):
        res = []
        for xv, dy, g in ((ql, dqn, gq), (kvl, dkn, gkv)):
            r = lax.rsqrt(jnp.mean(xv * xv, axis=-1, keepdims=True) + RMS_EPS)
            dxh = dy * g
            res.append(((r * (dxh - xv * (r * r) * jnp.mean(dxh * xv, axis=-1, keepdims=True))).astype(BF),
                        _colsum(dy * xv * r)))
        pad = jnp.zeros((ql.shape[0], LANE), BF)
        return jnp.concatenate([res[1][0], dkp, pad, res[0][0], pad], axis=1), res[0][1], res[1][1]

    d_proj, G["q_norm_g"], G["kv_norm_g"] = rowwise(
        "rms_bwd", rms_bwd, S, T,
        [(proj, QL, OFF_Q), (proj, KVL, OFF_KV), (d_qn, QL, 0), (d_kvn, KVL, 0), (dk_pe, LANE, 0)],
        [W["gq"], W["gkv"]], [(NP - OFF_KV, BF)], [QL, KVL], into=(d_proj, NP, OFF_KV))
    deps = before_in(G) if before_in is not None else ()
    du = mm(d_proj, W["in"], name="mm_du", trans_b=True, tm=1024, tk=2048, deps=deps)
    G["w_in"] = mm(sv["u"], d_proj, name="mm_gw_in", trans_a=True, out_dtype=BF, tm=1024, tk=2048, deps=deps)

    def mod_bwd(duv, xv, dr, a):
        return duv * (1.0 + a[1:2, :]) + dr, _colsum(duv), _colsum(duv * xv)

    dx, d_shift, d_scale = rowwise("mod_bwd", mod_bwd, S, T_WIDE, [(du, D_MODEL, 0), (x, D_MODEL, 0), (dres, D_MODEL, 0)],
                                   [ada3], [(D_MODEL, F32)], [D_MODEL] * 2)
    d_ada = jnp.concatenate([d_shift, d_scale, d_gate], axis=1)
    return dx, G, d_ada


SMALL = ("conv_a_b", "ln_a_g", "ln_a_b", "q_norm_g", "kv_norm_g", "ln_g", "ln_b")


def _rows(v):
    n = v.shape[0]
    r = -(-n // (LANE * 16)) * 16
    return jnp.pad(v, (0, r * LANE - n)).reshape(r, LANE)


def kernel(x, c, positions, w_ada, b_ada, w_in, conv_a_w, conv_a_b, ln_a_g, ln_a_b, w_a_out, conv_b_w, w_b_out, q_norm_g, kv_norm_g, w_uq, w_ukv, w_c_out, w_o, ln_g, ln_b, loss_target, m_w_ada, m_b_ada, m_w_in, m_conv_a_w, m_conv_a_b, m_ln_a_g, m_ln_a_b, m_w_a_out, m_conv_b_w, m_w_b_out, m_q_norm_g, m_kv_norm_g, m_w_uq, m_w_ukv, m_w_c_out, m_w_o, m_ln_g, m_ln_b, v_w_ada, v_b_ada, v_w_in, v_conv_a_w, v_conv_a_b, v_ln_a_g, v_ln_a_b, v_w_a_out, v_conv_b_w, v_w_b_out, v_q_norm_g, v_kv_norm_g, v_w_uq, v_w_ukv, v_w_c_out, v_w_o, v_ln_g, v_ln_b):
    P = dict(w_ada=w_ada, b_ada=b_ada, w_in=w_in, conv_a_w=conv_a_w, conv_a_b=conv_a_b, ln_a_g=ln_a_g, ln_a_b=ln_a_b,
             w_a_out=w_a_out, conv_b_w=conv_b_w, w_b_out=w_b_out, q_norm_g=q_norm_g, kv_norm_g=kv_norm_g, w_uq=w_uq,
             w_ukv=w_ukv, w_c_out=w_c_out, w_o=w_o, ln_g=ln_g, ln_b=ln_b)
    Mo = dict(w_ada=m_w_ada, b_ada=m_b_ada, w_in=m_w_in, conv_a_w=m_conv_a_w, conv_a_b=m_conv_a_b, ln_a_g=m_ln_a_g,
              ln_a_b=m_ln_a_b, w_a_out=m_w_a_out, conv_b_w=m_conv_b_w, w_b_out=m_w_b_out, q_norm_g=m_q_norm_g,
              kv_norm_g=m_kv_norm_g, w_uq=m_w_uq, w_ukv=m_w_ukv, w_c_out=m_w_c_out, w_o=m_w_o, ln_g=m_ln_g, ln_b=m_ln_b)
    Vo = dict(w_ada=v_w_ada, b_ada=v_b_ada, w_in=v_w_in, conv_a_w=v_conv_a_w, conv_a_b=v_conv_a_b, ln_a_g=v_ln_a_g,
              ln_a_b=v_ln_a_b, w_a_out=v_w_a_out, conv_b_w=v_conv_b_w, w_b_out=v_w_b_out, q_norm_g=v_q_norm_g,
              kv_norm_g=v_kv_norm_g, w_uq=v_w_uq, w_ukv=v_w_ukv, w_c_out=v_w_c_out, w_o=v_w_o, ln_g=v_ln_g, ln_b=v_ln_b)
    ORDER = ("w_ada", "b_ada", "w_in", "conv_a_w", "conv_a_b", "ln_a_g", "ln_a_b", "w_a_out", "conv_b_w", "w_b_out",
             "q_norm_g", "kv_norm_g", "w_uq", "w_ukv", "w_c_out", "w_o", "ln_g", "ln_b")
    L = w_ada.shape[0]
    S = x.shape[1]
    me = 4 * lax.axis_index("x") + 2 * lax.axis_index("y") + lax.axis_index("c")
    x2 = x[0]
    tgt = loss_target[0]

    small_in = _rows(jnp.concatenate([c.reshape(-1), conv_a_w.reshape(-1), conv_b_w.reshape(-1)]))
    w_in_b = jnp.pad(w_in.astype(BF), ((0, 0), (0, 0), (0, W_IN_PAD - IN_SHARD)))
    misc_b = jnp.concatenate([w_a_out, w_b_out, w_c_out, w_ukv, jnp.pad(w_uq, ((0, 0), (0, 0), (0, LANE - UQ_SHARD)))],
                             axis=1).astype(BF)
    w_o_b = w_o.astype(BF)
    gathered = [None] * L
    sg = exchange("gather_small", [small_in], [])[0]
    sgf = sg.reshape(NDEV, -1)
    c_all = sgf[:, :D_MODEL]
    o1 = D_MODEL + L * 31 * 64
    conv_a_full = sgf[:, D_MODEL:o1].reshape(NDEV, L, 31, 64).transpose(1, 2, 0, 3).reshape(L, 31, CW)
    conv_b_full = sgf[:, o1:o1 + L * 3 * 64].reshape(NDEV, L, 3, 64).transpose(1, 2, 0, 3).reshape(L, 3, CW)

    c_act = rowwise("silu_c", _silu, 16, 16, [(jnp.pad(c_all, ((0, 8), (0, 0))), D_MODEL, 0)], [], [(D_MODEL, BF)])[0]
    ncol = w_ada.shape[2]
    w_ada_b = w_ada.astype(BF).transpose(1, 0, 2).reshape(D_MODEL, L * ncol)
    b_mine = lax.dynamic_slice_in_dim(b_ada, me * ncol, ncol, axis=1).reshape(1, L * ncol)
    ada_part = mm(c_act, w_ada_b, name="mm_ada", bias=b_mine)
    ada_rows = -(-(L * ncol) // (LANE * 8)) * 8
    ada_send = jnp.pad(ada_part[:NDEV].reshape(NDEV, -1, LANE), ((0, 0), (0, ada_rows - L * ncol // LANE), (0, 0)))
    ada_recv = exchange("a2a_ada", [], [ada_send])[0]
    ada = ada_recv[:, :L * ncol // LANE].reshape(NDEV, L, ncol).transpose(1, 0, 2).reshape(L, 3, D_MODEL)
    gathered[0] = [gather_two_level("gather0_w_in", w_in_b[0], ada)]
    pending_rest, rest_token = exchange_begin("gather0_rest", [misc_b[0], w_o_b[0]], 2, gathered[0][0])

    inv_freq = ROPE_THETA ** (-jnp.arange(0, ROPE, 2, dtype=F32) / ROPE)
    ang = positions[0].astype(F32)[:, None] * inv_freq
    tabs = (jnp.tile(jnp.cos(ang), (1, 2 * LANE // ROPE)), jnp.tile(jnp.sin(ang), (1, 2 * LANE // ROPE)))

    straight = np.arange(D_MODEL)
    fwd_in = [(0, 0, 0, 0, D_MODEL, _to_gathered(IN_PERM, IN_SHARD, W_IN_PAD))]
    fwd_misc = [(0, ROW_A, 0, 0, CW, straight), (0, ROW_B, 1, 0, CW, straight), (0, ROW_C, 2, 0, CW, straight),
                (0, ROW_UKV, 3, 0, KVL, UKV_PERM), (0, ROW_UQ, 4, 0, QL, _to_gathered(UQ_PERM, UQ_SHARD, LANE))]
    rev_in = [(0, 0, 0, 0, D_MODEL, _from_full(_inverse(IN_PERM, D_IN), IN_SHARD, W_IN_PAD))]
    rev_misc = [(0, 0, 0, ROW_A, CW, straight), (1, 0, 0, ROW_B, CW, straight), (2, 0, 0, ROW_C, CW, straight),
                (3, 0, 0, ROW_UKV, KVL, _from_full(_inverse(UKV_PERM, HEADS * (NOPE + VH)), LANE, LANE)),
                (4, 0, 0, ROW_UQ, QL, _from_full(_inverse(UQ_PERM, HEADS * (NOPE + ROPE)), UQ_SHARD, LANE))]

    def layer_weights(l, deps):
        w_in_p = col_gather("relayout_w_in", [gathered[l][0]], [(D_MODEL, NP)], fwd_in, deps)[0]

        def late(after):
            if len(gathered[l]) == 1:
                gathered[l] += exchange_end(pending_rest, after)
            _, g_misc, g_o = gathered[l]
            a_out, b_out, c_out, ukv, uq = col_gather(
                "relayout_misc", [g_misc],
                [(CW, D_MODEL)] * 3 + [(KVL, HEADS * (NOPE + VH)), (QL, HEADS * (NOPE + ROPE))], fwd_misc, deps)
            return {"a_out": a_out, "b_out": b_out, "c_out": c_out, "uq": uq, "ukv": ukv,
                    "o": g_o.reshape(D_MODEL, D_MODEL)}

        return {
            "in": w_in_p, "late": late,
            "conv_a": jnp.pad(conv_a_full[l], ((0, 1), (0, 0))), "conv_b": jnp.pad(conv_b_full[l], ((0, 5), (0, 0))),
            "vec_a": jnp.stack([conv_a_b[l], ln_a_g[l], ln_a_b[l]]), "gq": q_norm_g[l][None], "gkv": kv_norm_g[l][None],
            "lnv": jnp.stack([ln_g[l], ln_b[l]]),
        }

    h = x2
    saved, weights = [], []
    handles, token, u_next = {}, rest_token, None
    for l in range(1, L):
        handles[l], token = exchange_begin("gather%d" % l, [w_in_b[l], misc_b[l], w_o_b[l]], 3, token)
    for l in range(L):
        ada_l, deps = (ada[l] + token[0, 0], (token,)) if l == 0 else (ada[l], ())
        h, sv, Wl, u_next = layer_fwd(h, ada_l, layer_weights(l, deps), tabs, S, u_next, ada[l + 1] if l + 1 < L else None)
        if l + 1 < L:
            gathered[l + 1] = exchange_end(handles[l + 1], h)
        saved.append(sv)
        weights.append(Wl)

    def loss_fn(y, t):
        e = y - t
        return e * (1.0 / D_MODEL), _colsum(e * e)

    dy, sq = rowwise("loss", loss_fn, S, 256, [(h, D_MODEL, 0), (tgt, D_MODEL, 0)], [], [(D_MODEL, F32)], [D_MODEL])
    loss = lax.psum(0.5 * jnp.sum(sq) / D_MODEL, ("x", "y", "c"))
    loss, dy = lax.optimization_barrier((loss, dy))

    grads, d_adas, recv = [None] * L, [None] * L, [None] * L
    pending, token = None, None

    def send_rest(g):
        send_misc = col_gather("unrelayout_misc", [g["w_a_out"], g["w_b_out"], g["w_c_out"], g["w_ukv"], g["w_uq"]],
                               [(NDEV, MISC_ROWS, LANE)], rev_misc)[0]
        return [send_misc, g["w_o"].reshape(NDEV, D_MODEL // NDEV, D_MODEL)]

    rest0 = []

    def early_rest(g):
        handle, tok = exchange_begin("scatter0_rest", send_rest(g), 0, g["w_o"])
        rest0.append(handle)
        return (tok,)

    for l in reversed(range(L)):
        ada_l = ada[l] if token is None else ada[l] + token[0, 0]
        dy, g, d_adas[l] = layer_bwd(dy, saved[l], ada_l, weights[l], tabs, S, early_rest if l == 0 else None)
        grads[l] = g
        if pending is not None:
            recv[l + 1] = exchange_end(pending, dy)
        send_in = col_gather("unrelayout_w_in", [g["w_in"]], [(NDEV, D_MODEL, W_IN_PAD)], rev_in)[0]
        if l == 0:
            def layer_vec(i):
                return jnp.concatenate([grads[i][n].reshape(-1) for n in SMALL] + [d_adas[i].reshape(-1)])

            def to_owners(name, taps):
                full = jnp.stack([grads[i][name][:taps] for i in range(L)])
                return full.reshape(L, taps, NDEV, CW // NDEV).transpose(2, 0, 1, 3).reshape(NDEV, -1)

            conv_send = jnp.concatenate([to_owners("conv_a_w", 31), to_owners("conv_b_w", 3)], axis=1)
            conv_rows = -(-conv_send.shape[1] // (LANE * 16)) * 16
            conv_send = jnp.pad(conv_send, ((0, 0), (0, conv_rows * LANE - conv_send.shape[1])))
            small_sizes = [int(grads[0][n].size) for n in SMALL] + [3 * D_MODEL]
            gsmall, conv_recv = exchange("gather_small_grads", [_rows(jnp.concatenate([layer_vec(i) for i in range(L)]))],
                                         [conv_send.reshape(NDEV, conv_rows, LANE)])
            pending, token = exchange_begin("scatter0", [send_in], 0, gsmall)
        else:
            pending, token = exchange_begin("scatter%d" % l, [send_in] + send_rest(g), 0,
                                            dy if l + 1 == L else recv[l + 1][0])
    grad_x = dy[None]

    gsmall = gsmall + token[0, 0]
    gsum = sum_slots("sum_small", gsmall).reshape(-1)
    recv[0] = [None] + exchange_end(rest0[0], gsum)
    Gr = {}
    offs = np.cumsum([0] + small_sizes)
    per_layer = int(offs[-1])
    gsum = gsum[:L * per_layer].reshape(L, per_layer)
    for i, n in enumerate(SMALL):
        Gr[n] = gsum[:, offs[i]:offs[i + 1]]
    csum = sum_slots("sum_conv", conv_recv + token[0, 0]).reshape(-1)
    n_a = L * 31 * (CW // NDEV)
    Gr["conv_a_w"] = csum[:n_a].reshape(L, 31, CW // NDEV)
    Gr["conv_b_w"] = csum[n_a:n_a + L * 3 * (CW // NDEV)].reshape(L, 3, CW // NDEV)
    Gr["b_ada"] = gsum[:, offs[7]:offs[8]]
    d_ada_all = gsmall.reshape(NDEV, -1)[:, :L * per_layer].reshape(NDEV, L, per_layer)[:, :, offs[7]:offs[8]]
    d_mine = lax.dynamic_slice_in_dim(d_ada_all, me * ncol, ncol, axis=2).reshape(NDEV, L * ncol)
    g_ada = mm(c_act, jnp.pad(d_mine, ((0, 8), (0, 0))).astype(BF), name="mm_gw_ada", trans_a=True)
    Gr["w_ada"] = g_ada.reshape(D_MODEL, L, ncol).transpose(1, 0, 2)

    D, NM, NV = {}, {}, {}
    D["w_ada"], NM["w_ada"], NV["w_ada"] = adamw("adamw_w_ada", P["w_ada"], Gr["w_ada"], Mo["w_ada"], Vo["w_ada"])
    Gr["w_o"], D["w_o"], NM["w_o"], NV["w_o"] = sum_adamw(
        "sum_adamw_w_o", [recv[l][2] for l in range(L)], P["w_o"], Mo["w_o"], Vo["w_o"])
    for n, row0 in (("w_a_out", ROW_A), ("w_b_out", ROW_B), ("w_c_out", ROW_C), ("w_ukv", ROW_UKV), ("w_uq", ROW_UQ)):
        Gr[n], D[n], NM[n], NV[n] = sum_adamw("sum_adamw_" + n, [recv[l][1] for l in range(L)], P[n], Mo[n], Vo[n],
                                              row0=row0)
    w_l, m_l, v_l, _ = lax.optimization_barrier((P["w_in"], Mo["w_in"], Vo["w_in"], token))
    upper = sum_adamw("sum_adamw_w_in_upper", [recv[l][0] for l in range(1, L)], w_l, m_l, v_l, lo=1)
    recv[0][0] = exchange_end(pending, upper[1])[0]
    Gr["w_in"], D["w_in"], NM["w_in"], NV["w_in"] = sum_adamw(
        "sum_adamw_w_in", [recv[0][0]], w_l, m_l, v_l, lo=0, prev=upper)
    for n in ("b_ada", "conv_a_w", "conv_b_w") + SMALL:
        shape = P[n].shape if P[n].ndim == 3 else (1,) + P[n].shape
        res = adamw("adamw_" + n, *[t.reshape(shape) for t in (P[n], Gr[n], Mo[n], Vo[n])])
        D[n], NM[n], NV[n] = [t.reshape(P[n].shape) for t in res]
    return (loss, grad_x, *[Gr[n] for n in ORDER], *[D[n] for n in ORDER], *[NM[n] for n in ORDER],
            *[NV[n] for n in ORDER])
```

```python
import functools
import math

import numpy as np
import jax
import jax.numpy as jnp
from jax import lax
from jax.experimental import pallas as pl
from jax.experimental.pallas import tpu as pltpu

BF = jnp.bfloat16
F32 = jnp.float32
MESH = pl.DeviceIdType.MESH
NDEV = 8

HEADS, NOPE, ROPE, VH = 8, 64, 32, 64
HP = 128
ROPE_THETA = 10000.0
LN_EPS = 1e-5
RMS_EPS = 1e-6
LR, B1, B2, EPS, WD, STEP = 0.001, 0.9, 0.999, 1e-08, 0.01, 10

LANE = 128
VMEM_LIMIT = 56 * 1024 * 1024

D_MODEL, CW, QL, KVL = 1024, 512, 384, 256
OFF_M, OFF_A, OFF_AG, OFF_B, OFF_CG, OFF_GB, OFF_BG = 0, 3072, 4096, 4608, 5632, 6144, 6656
OFF_KV, OFF_KR, OFF_Q, NP = 7168, 7424, 7680, 8192
D_IN = 7840


def _cparams(**kw):
    return pltpu.CompilerParams(vmem_limit_bytes=VMEM_LIMIT, **kw)


def _sigmoid(x):
    return jax.nn.sigmoid(x)


def _silu(x):
    return x * _sigmoid(x)


def _dsilu(x):
    s = _sigmoid(x)
    return s * (1.0 + x * (1.0 - s))


def _pick_tile(n, cap, mult):
    if n <= cap:
        return n
    for t in range(cap - cap % mult, 0, -mult):
        if n % t == 0:
            return t
    raise ValueError((n, cap, mult))


def mm(a, b, *, name, trans_a=False, trans_b=False, out_dtype=F32, bias=None, tm=1024, tn=1024, tk=2048, deps=()):
    if trans_a:
        K, M = a.shape
    else:
        M, K = a.shape
    if trans_b:
        N, K2 = b.shape
    else:
        K2, N = b.shape
    assert K == K2 and not (trans_a and trans_b), (a.shape, b.shape)
    tm, tn = _pick_tile(M, tm, 16), _pick_tile(N, tn, LANE)
    tk = _pick_tile(K, tk, LANE if trans_b else 16)
    assert M % tm == 0 and N % tn == 0 and K % tk == 0, (M, N, K, tm, tn, tk)
    nk = K // tk
    dims = (((0 if trans_a else 1,), (1 if trans_b else 0,)), ((), ()))
    has_bias = bias is not None

    def body(*refs):
        a_ref, b_ref = refs[0], refs[1]
        bias_ref = refs[2] if has_bias else None
        o_ref = refs[(3 if has_bias else 2) + len(deps)]
        p = lax.dot_general(a_ref[...], b_ref[...], dims, preferred_element_type=F32)

        def finish(v):
            if has_bias:
                v = v + bias_ref[...]
            o_ref[...] = v.astype(o_ref.dtype)

        if nk == 1:
            finish(p)
        else:
            acc = refs[-1]
            k = pl.program_id(2)

            @pl.when(k == 0)
            def _():
                acc[...] = p

            @pl.when(k > 0)
            def _():
                acc[...] += p

            @pl.when(k == nk - 1)
            def _():
                finish(acc[...])

    if trans_a:
        a_spec = pl.BlockSpec((tk, tm), lambda i, j, k: (k, i))
    else:
        a_spec = pl.BlockSpec((tm, tk), lambda i, j, k: (i, k))
    if trans_b:
        b_spec = pl.BlockSpec((tn, tk), lambda i, j, k: (j, k))
    else:
        b_spec = pl.BlockSpec((tk, tn), lambda i, j, k: (k, j))
    in_specs = [a_spec, b_spec]
    args = [a, b]
    if has_bias:
        in_specs.append(pl.BlockSpec((1, tn), lambda i, j, k: (0, j)))
        args.append(bias)
    in_specs += [ANY_SPEC] * len(deps)
    args += list(deps)
    return pl.pallas_call(
        body, name=name, grid=(M // tm, N // tn, nk),
        in_specs=in_specs, out_specs=pl.BlockSpec((tm, tn), lambda i, j, k: (i, j)),
        out_shape=jax.ShapeDtypeStruct((M, N), out_dtype),
        scratch_shapes=[pltpu.VMEM((tm, tn), F32)] if nk > 1 else [],
        compiler_params=_cparams(),
    )(*args)


def rowwise(name, fn, S, T, row_ins, full_ins, row_outs, acc_outs=(), into=None):
    n_in = len(row_ins) + len(full_ins)
    n_ro, n_ao = len(row_outs), len(acc_outs)
    alias = into is not None and into[0] is not None
    T = min(T, S)

    def body(*refs):
        vals = [r[...] for r in refs[:n_in]]
        vals = [v.astype(F32) if v.dtype == BF else v for v in vals]
        outs = fn(*vals)
        if not isinstance(outs, (tuple, list)):
            outs = (outs,)
        assert len(outs) == n_ro + n_ao, (name, len(outs))
        o0 = n_in + (1 if alias else 0)
        for r, v in zip(refs[o0:o0 + n_ro], outs[:n_ro]):
            r[...] = v.astype(r.dtype)
        first = pl.program_id(0) == 0
        for r, v in zip(refs[o0 + n_ro:], outs[n_ro:]):
            def init(r=r, v=v):
                r[...] = v

            def accum(r=r, v=v):
                r[...] += v

            pl.when(first)(init)
            pl.when(jnp.logical_not(first))(accum)

    in_specs, args = [], []
    for arr, W, off in row_ins:
        assert off % W == 0 and arr.shape[0] == S, (name, arr.shape, W, off)
        in_specs.append(pl.BlockSpec((T, W), functools.partial(lambda i, cb: (i, cb), cb=off // W)))
        args.append(arr)
    for arr in full_ins:
        in_specs.append(pl.BlockSpec(arr.shape, lambda i: (0, 0)))
        args.append(arr)
    out_specs = [pl.BlockSpec((T, W), lambda i: (i, 0)) for W, _ in row_outs]
    out_shape = [jax.ShapeDtypeStruct((S, W), dt) for W, dt in row_outs]
    aliases = {}
    if into is not None:
        buf, total, off = into
        W0, dt0 = row_outs[0]
        assert off % W0 == 0
        out_specs[0] = pl.BlockSpec((T, W0), functools.partial(lambda i, cb: (i, cb), cb=off // W0))
        out_shape[0] = jax.ShapeDtypeStruct((S, total), dt0)
        if alias:
            in_specs.append(ANY_SPEC)
            args.append(buf)
            aliases = {n_in: 0}
    out_specs += [pl.BlockSpec((1, W), lambda i: (0, 0)) for W in acc_outs]
    out_shape += [jax.ShapeDtypeStruct((1, W), F32) for W in acc_outs]
    return pl.pallas_call(
        body, name=name, grid=(S // T,), in_specs=in_specs, out_specs=out_specs, out_shape=out_shape,
        input_output_aliases=aliases, compiler_params=_cparams(),
    )(*args)


def _colsum(v):
    return jnp.sum(v, axis=0, keepdims=True)


def _ln_stats(r):
    mu = jnp.mean(r, axis=-1, keepdims=True)
    d = r - mu
    var = jnp.mean(d * d, axis=-1, keepdims=True)
    rstd = lax.rsqrt(var + LN_EPS)
    return d * rstd, rstd


def _ln_bwd(dn, n, rstd):
    return rstd * (dn - jnp.mean(dn, axis=-1, keepdims=True) - n * jnp.mean(dn * n, axis=-1, keepdims=True))


CPAD = 32
TC = 64


def _pre(mode, x1, x2):
    return x1 * _sigmoid(x2) if mode == "glu" else x1 * x2


def _sublane_shifts(ext):
    n = TC + CPAD
    return [ext] + [pltpu.roll(ext, n - r, 0) for r in range(1, 8)]


def _shifted(shifts, sft):
    q, r = divmod(sft, 8)
    return shifts[r][8 * q:8 * q + TC]


def _interleaved_specs(S, off):
    return [pl.BlockSpec((S, LANE), functools.partial(lambda j, o: (0, o + 2 * j), o=off // LANE)),
            pl.BlockSpec((S, LANE), functools.partial(lambda j, o: (0, o + 2 * j + 1), o=off // LANE))]


def conv_fwd(name, src, off, w_pad, taps, mode, S, C):
    nchunk = S // TC

    def body(x1_ref, x2_ref, w_ref, o_ref, a_pad):
        a_pad[0:CPAD, :] = jnp.zeros((CPAD, LANE), F32)

        def fill(i, _):
            r = pl.multiple_of(i * 256, 256)
            a_pad[pl.ds(CPAD + r, 256), :] = _pre(mode, x1_ref[pl.ds(r, 256), :].astype(F32),
                                                  x2_ref[pl.ds(r, 256), :].astype(F32))
            return 0

        lax.fori_loop(0, S // 256, fill, 0)

        def chunk(i, _):
            base = pl.multiple_of(i * TC, TC)
            shifts = _sublane_shifts(a_pad[pl.ds(base, TC + CPAD), :])
            acc = jnp.zeros((TC, LANE), F32)
            for k in range(taps):
                acc = acc + w_ref[pl.ds(k, 1), :] * _shifted(shifts, CPAD - (taps - 1) + k)
            o_ref[pl.ds(base, TC), :] = acc
            return 0

        lax.fori_loop(0, nchunk, chunk, 0)

    kp = w_pad.shape[0]
    return pl.pallas_call(
        body, name=name, grid=(C // LANE,),
        in_specs=_interleaved_specs(S, off) + [pl.BlockSpec((kp, LANE), lambda j: (0, j))],
        out_specs=pl.BlockSpec((S, LANE), lambda j: (0, j)),
        out_shape=jax.ShapeDtypeStruct((S, C), F32),
        scratch_shapes=[pltpu.VMEM((S + CPAD, LANE), F32)],
        compiler_params=_cparams(),
    )(src, src, w_pad)


def conv_bwd(name, src, off, dc, w_pad, taps, mode, S, C, buf):
    nchunk = S // TC
    kp = w_pad.shape[0]

    def body(x1_ref, x2_ref, dc_ref, w_ref, _, d_ref, dw_ref, a_pad, dc_pad, dw_acc):
        a_pad[0:CPAD, :] = jnp.zeros((CPAD, LANE), F32)
        dc_pad[S:S + CPAD, :] = jnp.zeros((CPAD, LANE), F32)
        dw_acc[...] = jnp.zeros(dw_acc.shape, F32)

        def fill(i, _):
            r = pl.multiple_of(i * 256, 256)
            a_pad[pl.ds(CPAD + r, 256), :] = _pre(mode, x1_ref[pl.ds(r, 256), :].astype(F32),
                                                  x2_ref[pl.ds(r, 256), :].astype(F32))
            dc_pad[pl.ds(r, 256), :] = dc_ref[pl.ds(r, 256), :]
            return 0

        lax.fori_loop(0, S // 256, fill, 0)

        def chunk(i, _):
            base = pl.multiple_of(i * TC, TC)
            shifts_d = _sublane_shifts(dc_pad[pl.ds(base, TC + CPAD), :])
            shifts_a = _sublane_shifts(a_pad[pl.ds(base, TC + CPAD), :])
            dcv = shifts_d[0][0:TC]
            da = jnp.zeros((TC, LANE), F32)
            for k in range(taps):
                da = da + w_ref[pl.ds(k, 1), :] * _shifted(shifts_d, taps - 1 - k)
                prod = dcv * _shifted(shifts_a, CPAD - (taps - 1) + k)
                fold = prod[0:8]
                for g in range(1, TC // 8):
                    fold = fold + prod[8 * g:8 * g + 8]
                dw_acc[pl.ds(8 * k, 8), :] += fold
            x1 = x1_ref[pl.ds(base, TC), :].astype(F32)
            x2 = x2_ref[pl.ds(base, TC), :].astype(F32)
            if mode == "glu":
                s = _sigmoid(x2)
                d1, d2 = da * s, da * x1 * s * (1.0 - s)
            else:
                d1, d2 = da * x2, da * x1
            d_ref[pl.ds(base, TC), 0:LANE] = d1.astype(BF)
            d_ref[pl.ds(base, TC), LANE:2 * LANE] = d2.astype(BF)
            return 0

        lax.fori_loop(0, nchunk, chunk, 0)
        dw_ref[...] = jnp.zeros(dw_ref.shape, F32)
        for k in range(taps):
            dw_ref[pl.ds(k, 1), :] = jnp.sum(dw_acc[pl.ds(8 * k, 8), :], axis=0, keepdims=True)

    blk = pl.BlockSpec((S, LANE), lambda j: (0, j))
    return pl.pallas_call(
        body, name=name, grid=(C // LANE,),
        in_specs=_interleaved_specs(S, off) + [blk, pl.BlockSpec((kp, LANE), lambda j: (0, j)), ANY_SPEC],
        out_specs=[pl.BlockSpec((S, 2 * LANE), functools.partial(lambda j, o: (0, o + j), o=off // (2 * LANE))),
                   pl.BlockSpec((kp, LANE), lambda j: (0, j))],
        out_shape=[jax.ShapeDtypeStruct(buf.shape, BF), jax.ShapeDtypeStruct((kp, C), F32)],
        input_output_aliases={4: 0},
        scratch_shapes=[pltpu.VMEM((S + CPAD, LANE), F32), pltpu.VMEM((S + CPAD, LANE), F32),
                        pltpu.VMEM((8 * kp, LANE), F32)],
        compiler_params=_cparams(),
    )(src, src, dc, w_pad, buf)


FWD_TILES = (512, 512)
BWD_TILES = (512, 512)
QUADS = HEADS // 4
QW, KVW = 4 * (NOPE + ROPE), 4 * (NOPE + VH)
SCALE = (NOPE + ROPE) ** -0.5
NT_DIMS = (((1,), (1,)), ((), ()))
TN_DIMS = (((0,), (0,)), ((), ()))


def _lane_mask(width, group, dtype):
    lane = lax.broadcasted_iota(jnp.int32, (1, LANE), 1)
    return jnp.where(lane // width == group, 1.0, 0.0).astype(dtype)


def _visible(tq, tk, off):
    row = lax.broadcasted_iota(jnp.int32, (tq, tk), 0)
    col = lax.broadcasted_iota(jnp.int32, (tq, tk), 1)
    return col <= row + off


def _attn_tiles(S, tq, tk):
    tk = tk if S % tk == 0 else 256
    return min(tq, tk), tk


def attn_fwd(q, kv, kpe, S):
    tq, tk = _attn_tiles(S, *FWD_TILES)
    nq = S // tq

    def body(q_ref, kv_ref, kp_ref, o_ref, lse_ref):
        for t in range(2):
            cols = slice(t * LANE, (t + 1) * LANE)
            for hh in range(2):
                def q_block(qi, _, t=t, hh=hh, cols=cols):
                    r0 = qi * tq
                    qcat = jnp.concatenate([q_ref[pl.ds(r0, tq), cols] * _lane_mask(NOPE, hh, BF),
                                            q_ref[pl.ds(r0, tq), 2 * LANE:3 * LANE] * _lane_mask(ROPE, 2 * t + hh, BF)],
                                           axis=1)
                    nfull = (qi * tq) // tk

                    def step(kj, carry, masked):
                        m, l, acc = carry
                        c0 = kj * tk
                        kc = jnp.concatenate([kv_ref[pl.ds(c0, tk), cols], kp_ref[pl.ds(c0, tk), :]], axis=1)
                        vt = kv_ref[pl.ds(c0, tk), (2 + t) * LANE:(3 + t) * LANE]
                        s = lax.dot_general(qcat, kc, NT_DIMS, preferred_element_type=F32) * SCALE
                        if masked:
                            s = jnp.where(_visible(tq, tk, qi * tq - nfull * tk), s, -jnp.inf)
                        m_new = jnp.maximum(m, jnp.max(s, axis=-1, keepdims=True))
                        p = jnp.exp(s - m_new)
                        alpha = jnp.exp(m - m_new)
                        l = alpha * l + jnp.sum(p, axis=-1, keepdims=True)
                        acc = alpha * acc + jnp.dot(p.astype(BF), vt, preferred_element_type=F32)
                        return m_new, l, acc

                    carry = (jnp.full((tq, 1), -jnp.inf, F32), jnp.zeros((tq, 1), F32), jnp.zeros((tq, LANE), F32))
                    for kj in range(nfull):
                        carry = step(kj, carry, False)
                    m, l, acc = step(nfull, carry, True)
                    mine = _lane_mask(NOPE, hh, F32)
                    if hh == 0:
                        o_ref[pl.ds(r0, tq), cols] = (acc / l) * mine
                        lse_ref[pl.ds(r0, tq), cols] = (m + jnp.log(l)) * mine
                    else:
                        o_ref[pl.ds(r0, tq), cols] += (acc / l) * mine
                        lse_ref[pl.ds(r0, tq), cols] += (m + jnp.log(l)) * mine
                    return 0

                for qi in range(nq):
                    q_block(qi, 0)

    return pl.pallas_call(
        body, name="attn_fwd", grid=(QUADS,),
        in_specs=[pl.BlockSpec((S, QW), lambda g: (0, g)), pl.BlockSpec((S, KVW), lambda g: (0, g)),
                  pl.BlockSpec((S, LANE), lambda g: (0, 0))],
        out_specs=[pl.BlockSpec((S, 2 * LANE), lambda g: (0, g))] * 2,
        out_shape=[jax.ShapeDtypeStruct((S, HEADS * VH), F32)] * 2,
        compiler_params=_cparams(),
    )(q, kv, kpe)


def attn_bwd(q, kv, kpe, o, lse, do, S):
    tq, tk = _attn_tiles(S, *BWD_TILES)
    nq = S // tq

    def body(q_ref, kv_ref, kp_ref, o_ref, lse_ref, do_ref, dq_ref, dkv_ref, dkp_ref, dq_acc, dk_acc, dv_acc):
        for t in range(2):
            cols = slice(t * LANE, (t + 1) * LANE)
            dk_acc[...] = jnp.zeros(dk_acc.shape, F32)
            dv_acc[...] = jnp.zeros(dv_acc.shape, F32)
            for hh in range(2):
                def q_block(qi, _, t=t, hh=hh, cols=cols):
                    r0 = qi * tq
                    mine = _lane_mask(NOPE, hh, F32)
                    qcat = jnp.concatenate([q_ref[pl.ds(r0, tq), cols] * _lane_mask(NOPE, hh, BF),
                                            q_ref[pl.ds(r0, tq), 2 * LANE:3 * LANE] * _lane_mask(ROPE, 2 * t + hh, BF)],
                                           axis=1)
                    dof = do_ref[pl.ds(r0, tq), cols] * mine
                    dob = dof.astype(BF)
                    delta = jnp.sum(dof * o_ref[pl.ds(r0, tq), cols], axis=-1, keepdims=True)
                    lse_h = lse_ref[pl.ds(r0, tq), cols][:, hh * NOPE:hh * NOPE + 1]
                    nfull = (qi * tq) // tk
                    dq_acc[...] = jnp.zeros(dq_acc.shape, F32)

                    def step(kj, _, masked):
                        c0 = kj * tk
                        kc = jnp.concatenate([kv_ref[pl.ds(c0, tk), cols], kp_ref[pl.ds(c0, tk), :]], axis=1)
                        vt = kv_ref[pl.ds(c0, tk), (2 + t) * LANE:(3 + t) * LANE]
                        s = lax.dot_general(qcat, kc, NT_DIMS, preferred_element_type=F32) * SCALE
                        if masked:
                            s = jnp.where(_visible(tq, tk, qi * tq - nfull * tk), s, -jnp.inf)
                        p = jnp.exp(s - lse_h)
                        dp = lax.dot_general(dob, vt, NT_DIMS, preferred_element_type=F32)
                        ds = (p * (dp - delta) * SCALE).astype(BF)
                        dv_acc[pl.ds(c0, tk), :] += lax.dot_general(p.astype(BF), dob, TN_DIMS,
                                                                    preferred_element_type=F32)
                        dk_acc[pl.ds(c0, tk), :] += lax.dot_general(ds, qcat, TN_DIMS, preferred_element_type=F32)
                        dq_acc[...] += jnp.dot(ds, kc, preferred_element_type=F32)
                        return 0

                    for kj in range(nfull):
                        step(kj, 0, False)
                    step(nfull, 0, True)
                    d = dq_acc[...]
                    pe = d[:, LANE:] * _lane_mask(ROPE, 2 * t + hh, F32)
                    if hh == 0:
                        dq_ref[pl.ds(r0, tq), cols] = d[:, :LANE] * mine
                    else:
                        dq_ref[pl.ds(r0, tq), cols] += d[:, :LANE] * mine
                    if t == 0 and hh == 0:
                        dq_ref[pl.ds(r0, tq), 2 * LANE:3 * LANE] = pe
                    else:
                        dq_ref[pl.ds(r0, tq), 2 * LANE:3 * LANE] += pe
                    return 0

                for qi in range(nq):
                    q_block(qi, 0)
            dkv_ref[:, t * LANE:(t + 1) * LANE] = dk_acc[:, :LANE].astype(BF)
            dkv_ref[:, (2 + t) * LANE:(3 + t) * LANE] = dv_acc[...].astype(BF)
            if t == 0:
                dkp_ref[...] = dk_acc[:, LANE:]
            else:
                dkp_ref[...] += dk_acc[:, LANE:]

    qspec = pl.BlockSpec((S, QW), lambda g: (0, g))
    kvspec = pl.BlockSpec((S, KVW), lambda g: (0, g))
    ospec = pl.BlockSpec((S, 2 * LANE), lambda g: (0, g))
    return pl.pallas_call(
        body, name="attn_bwd", grid=(QUADS,),
        in_specs=[qspec, kvspec, pl.BlockSpec((S, LANE), lambda g: (0, 0)), ospec, ospec, ospec],
        out_specs=[qspec, kvspec, pl.BlockSpec((S, LANE), lambda g: (0, g))],
        out_shape=[jax.ShapeDtypeStruct((S, HEADS * (NOPE + ROPE)), F32), jax.ShapeDtypeStruct((S, HEADS * (NOPE + VH)), BF),
                   jax.ShapeDtypeStruct((S, HEADS * ROPE), F32)],
        scratch_shapes=[pltpu.VMEM((tq, 2 * LANE), F32), pltpu.VMEM((S, 2 * LANE), F32), pltpu.VMEM((S, LANE), F32)],
        compiler_params=_cparams(),
    )(q, kv, kpe, o, lse, do)


def exchange(name, gathers, a2as):
    n_g, n = len(gathers), len(gathers) + len(a2as)

    def body(*refs):
        ins, outs = refs[:n], refs[n:2 * n]
        send_sems, recv_sems, loc_sems = refs[2 * n:]
        x, y, c = lax.axis_index("x"), lax.axis_index("y"), lax.axis_index("c")
        me = 4 * x + 2 * y + c

        def peer(k):
            px = 1 - x if k & 4 else x
            py = 1 - y if k & 2 else y
            pc = 1 - c if k & 1 else c
            return (px, py, pc), 4 * px + 2 * py + pc

        def remote(a, k):
            pid, pflat = peer(k)
            src = ins[a] if a < n_g else ins[a].at[pflat]
            return pltpu.make_async_remote_copy(
                src_ref=src, dst_ref=outs[a].at[me], send_sem=send_sems.at[a, k - 1], recv_sem=recv_sems.at[a, k - 1],
                device_id=pid, device_id_type=MESH)

        def arrival(a, k):
            pid, pflat = peer(k)
            src = ins[a] if a < n_g else ins[a].at[pflat]
            return pltpu.make_async_remote_copy(
                src_ref=src, dst_ref=outs[a].at[pflat], send_sem=send_sems.at[a, k - 1], recv_sem=recv_sems.at[a, k - 1],
                device_id=pid, device_id_type=MESH)

        local = []
        for a in range(n):
            own = ins[a] if a < n_g else ins[a].at[me]
            cp = pltpu.make_async_copy(own, outs[a].at[me], loc_sems.at[a])
            cp.start()
            local.append(cp)
        sent = []
        for k in (1, 2, 4, 3, 5, 6, 7):
            for a in range(n):
                cp = remote(a, k)
                cp.start()
                sent.append(cp)
        for k in range(1, 8):
            for a in range(n):
                arrival(a, k).wait_recv()
        for cp in sent:
            cp.wait_send()
        for cp in local:
            cp.wait()

    out_shape = [jax.ShapeDtypeStruct((NDEV,) + g.shape, g.dtype) for g in gathers]
    out_shape += [jax.ShapeDtypeStruct(a.shape, a.dtype) for a in a2as]
    any_spec = pl.BlockSpec(memory_space=pl.ANY)
    return pl.pallas_call(
        body, name=name, in_specs=[any_spec] * n, out_specs=[any_spec] * n, out_shape=out_shape,
        scratch_shapes=[pltpu.SemaphoreType.DMA((n, NDEV - 1)), pltpu.SemaphoreType.DMA((n, NDEV - 1)),
                        pltpu.SemaphoreType.DMA((n,))],
    )(*gathers, *a2as)


def gather_two_level(name, block, dep):
    def body(x_ref, _, out_ref, stage, send_sems, recv_sems, loc_sem):
        x, y, c = lax.axis_index("x"), lax.axis_index("y"), lax.axis_index("c")
        me, sibling = (x, y, c), (x, y, 1 - c)
        chips = [(1 - x, y), (x, 1 - y), (1 - x, 1 - y)]

        def slot(px, py, pc):
            return out_ref.at[4 * px + 2 * py + pc]

        def copy(k, owner, to, src=None):
            return pltpu.make_async_remote_copy(
                src_ref=slot(*owner) if src is None else src, dst_ref=slot(*owner), send_sem=send_sems.at[k],
                recv_sem=recv_sems.at[k], device_id=to, device_id_type=MESH)

        load = pltpu.make_async_copy(x_ref, stage, loc_sem)
        load.start()
        first = [copy(0, me, sibling, src=x_ref)] + [copy(1 + j, me, (*chip, c), src=x_ref) for j, chip in enumerate(chips)]
        for cp in first:
            cp.start()
        load.wait()
        store = pltpu.make_async_copy(stage, slot(*me), loc_sem)
        store.start()
        passed = [copy(4 + j, (*chip, c), sibling) for j, chip in enumerate(chips)]
        for j, chip in enumerate(chips):
            copy(1 + j, (*chip, c), me).wait_recv()
            passed[j].start()
        copy(0, sibling, me).wait_recv()
        for j, chip in enumerate(chips):
            copy(4 + j, (*chip, 1 - c), me).wait_recv()
        for cp in first + passed:
            cp.wait_send()
        store.wait()

    return pl.pallas_call(
        body, name=name, in_specs=[pl.BlockSpec(memory_space=pl.ANY)] * 2, out_specs=pl.BlockSpec(memory_space=pl.ANY),
        out_shape=jax.ShapeDtypeStruct((NDEV,) + block.shape, block.dtype),
        scratch_shapes=[pltpu.VMEM(block.shape, block.dtype), pltpu.SemaphoreType.DMA((NDEV - 1,)),
                        pltpu.SemaphoreType.DMA((NDEV - 1,)), pltpu.SemaphoreType.DMA],
        compiler_params=_cparams(),
    )(block, dep)


def _peer(k, x, y, c):
    px = 1 - x if k & 4 else x
    py = 1 - y if k & 2 else y
    pc = 1 - c if k & 1 else c
    return (px, py, pc), 4 * px + 2 * py + pc


PEER_ORDER = (1, 2, 4, 3, 5, 6, 7)
HBM_SPEC = pl.BlockSpec(memory_space=pltpu.HBM)
SEM_SPEC = pl.BlockSpec(memory_space=pltpu.SEMAPHORE)
ANY_SPEC = pl.BlockSpec(memory_space=pl.ANY)


def _split_copies(ins, lands, n_g, send_sems, recv_sems):
    x, y, c = lax.axis_index("x"), lax.axis_index("y"), lax.axis_index("c")
    me = 4 * x + 2 * y + c

    def outgoing(a, k):
        pid, pflat = _peer(k, x, y, c)
        src = ins[a] if a < n_g else ins[a].at[pflat]
        return pltpu.make_async_remote_copy(
            src_ref=src, dst_ref=lands[a].at[me], send_sem=send_sems.at[a * (NDEV - 1) + k - 1],
            recv_sem=recv_sems.at[a * (NDEV - 1) + k - 1],
            device_id=pid, device_id_type=MESH)

    def arrival(a, k):
        pid, pflat = _peer(k, x, y, c)
        src = ins[a] if a < n_g else ins[a].at[pflat]
        return pltpu.make_async_remote_copy(
            src_ref=src, dst_ref=lands[a].at[pflat], send_sem=send_sems.at[a * (NDEV - 1) + k - 1],
            recv_sem=recv_sems.at[a * (NDEV - 1) + k - 1],
            device_id=pid, device_id_type=MESH)

    return outgoing, arrival


def exchange_begin(name, srcs, n_g, dep):
    n = len(srcs)
    land_shapes = [((NDEV,) + s.shape) if a < n_g else s.shape for a, s in enumerate(srcs)]

    def own_body(*refs):
        ins, outs = refs[:n], refs[n + 1:2 * n + 1]
        stage, sems = refs[2 * n + 1:3 * n + 1], refs[-1]
        me = 4 * lax.axis_index("x") + 2 * lax.axis_index("y") + lax.axis_index("c")
        cps = [pltpu.make_async_copy(ins[a] if a < n_g else ins[a].at[me], stage[a], sems.at[a]) for a in range(n)]
        for cp in cps:
            cp.start()
        for cp in cps:
            cp.wait()
        cps = [pltpu.make_async_copy(stage[a], outs[a].at[me], sems.at[a]) for a in range(n)]
        for cp in cps:
            cp.start()
        for cp in cps:
            cp.wait()

    lands = pl.pallas_call(
        own_body, name=name + "_own", in_specs=[ANY_SPEC] * (n + 1), out_specs=[ANY_SPEC] * n,
        out_shape=[jax.ShapeDtypeStruct(sh, s.dtype) for sh, s in zip(land_shapes, srcs)],
        scratch_shapes=[pltpu.VMEM(sh[1:], s.dtype) for sh, s in zip(land_shapes, srcs)] + [pltpu.SemaphoreType.DMA((n,))],
        compiler_params=_cparams(),
    )(*srcs, dep)

    def start_body(*refs):
        ins, lz = refs[:n], refs[n:2 * n]
        send_sems, recv_sems, token = refs[2 * n], refs[2 * n + 1], refs[-1]
        outgoing, _ = _split_copies(ins, lz, n_g, send_sems, recv_sems)
        for k in PEER_ORDER:
            for a in range(n):
                outgoing(a, k).start()
        token[...] = jnp.zeros(token.shape, F32)

    hbm = lambda t: pltpu.HBM(t.shape, t.dtype)
    res = pl.pallas_call(
        start_body, name=name + "_start",
        out_shape=(pltpu.SemaphoreType.DMA((n * (NDEV - 1),)), pltpu.SemaphoreType.DMA((n * (NDEV - 1),)),
                   *[hbm(s) for s in srcs], *[hbm(t) for t in lands], jax.ShapeDtypeStruct((8, LANE), F32)),
        in_specs=[HBM_SPEC] * (2 * n),
        out_specs=(SEM_SPEC, SEM_SPEC, *[HBM_SPEC] * (2 * n), pl.BlockSpec(memory_space=pltpu.VMEM)),
        input_output_aliases={i: 2 + i for i in range(2 * n)},
        compiler_params=pltpu.CompilerParams(has_side_effects=pltpu.SideEffectType.DATAFLOW_SIDE_EFFECTING),
    )(*[pltpu.with_memory_space_constraint(t, pltpu.HBM) for t in list(srcs) + list(lands)])
    return (name, n, n_g, res[:-1]), res[-1]


def exchange_end(handle, after):
    name, n, n_g, (send_sems, recv_sems, *bufs) = handle

    def wait_body(*refs):
        ins, lz = refs[:n], refs[n:2 * n]
        ss, rs = refs[2 * n], refs[2 * n + 1]
        outgoing, arrival = _split_copies(ins, lz, n_g, ss, rs)
        for k in range(1, NDEV):
            for a in range(n):
                arrival(a, k).wait_recv()
        for k in range(1, NDEV):
            for a in range(n):
                outgoing(a, k).wait_send()

    res = pl.pallas_call(
        wait_body, name=name + "_wait", out_shape=tuple(pltpu.HBM(t.shape, t.dtype) for t in bufs),
        in_specs=[HBM_SPEC] * (2 * n) + [SEM_SPEC, SEM_SPEC, ANY_SPEC], out_specs=[HBM_SPEC] * (2 * n),
        input_output_aliases={i: i for i in range(2 * n)},
        compiler_params=pltpu.CompilerParams(has_side_effects=pltpu.SideEffectType.DATAFLOW_SIDE_EFFECTING),
    )(*bufs, send_sems, recv_sems, after)
    return list(res[n:])


def _pick_rows(R, mult, cap):
    best = None
    for n in range(1, R + 1):
        if R % n == 0 and (R // n) % mult == 0 and R // n <= cap:
            best = R // n
            break
    assert best is not None, (R, mult, cap)
    return best


def sum_slots(name, x):
    _, R, _ = x.shape
    tr = _pick_rows(R, 16, 2304)

    def body(x_ref, o_ref):
        acc = x_ref[0].astype(F32)
        for d in range(1, NDEV):
            acc = acc + x_ref[d].astype(F32)
        o_ref[...] = acc

    return pl.pallas_call(
        body, name=name, grid=(R // tr,),
        in_specs=[pl.BlockSpec((NDEV, tr, LANE), lambda i: (0, i, 0))],
        out_specs=pl.BlockSpec((tr, LANE), lambda i: (i, 0)),
        out_shape=jax.ShapeDtypeStruct((R, LANE), F32), compiler_params=_cparams(),
    )(x)


def adamw(name, w, g, m, v):
    L, R, C = w.shape
    tr = _pick_rows(R, 8, 256) if R % 8 == 0 else R

    def body(w_ref, g_ref, m_ref, v_ref, d_ref, nm_ref, nv_ref):
        gg = g_ref[...]
        nm = B1 * m_ref[...] + (1.0 - B1) * gg
        nv = B2 * v_ref[...] + (1.0 - B2) * jnp.square(gg)
        m_hat = nm / (1.0 - B1 ** STEP)
        v_hat = nv / (1.0 - B2 ** STEP)
        d_ref[...] = -LR * (m_hat / (jnp.sqrt(v_hat) + EPS) + WD * w_ref[...])
        nm_ref[...] = nm
        nv_ref[...] = nv

    blk = pl.BlockSpec((1, tr, C), lambda l, i: (l, i, 0))
    shp = jax.ShapeDtypeStruct(w.shape, F32)
    return pl.pallas_call(
        body, name=name, grid=(L, R // tr), in_specs=[blk] * 4, out_specs=[blk] * 3, out_shape=[shp] * 3,
        compiler_params=_cparams(),
    )(w, g, m, v)


IN_SHARD = D_IN // NDEV
UQ_SHARD = HEADS * (NOPE + ROPE) // NDEV
W_IN_PAD = 1024
ROW_A, ROW_B, ROW_C, ROW_UKV, ROW_UQ, MISC_ROWS = 0, 512, 1024, 1536, 1792, 2176


def _in_perm_index():
    ar = np.arange
    z = lambda n: np.full((n,), -1, np.int64)
    mix = lambda lo1, lo2: np.concatenate([ar(lo + LANE * j, lo + LANE * (j + 1)) for j in range(CW // LANE)
                                           for lo in (lo1, lo2)])
    return np.concatenate([ar(4768, 7840), mix(0, 512), ar(1024, 1536), mix(1536, 2560), ar(4256, 4768), ar(2048, 2560),
                           ar(3072, 3584), ar(3968, 4224), ar(4224, 4256), z(OFF_Q - OFF_KR - ROPE), ar(3584, 3968),
                           z(NP - OFF_Q - QL)])


def _head_perm_index(a, b):
    parts = []
    for g in range(QUADS):
        h = np.arange(4 * g, 4 * g + 4)[:, None] * (a + b)
        parts += [(h + np.arange(a)[None]).reshape(-1), (h + a + np.arange(b)[None]).reshape(-1)]
    return np.concatenate(parts)


def _inverse(perm, n):
    inv = np.full((n,), -1, np.int64)
    inv[perm[perm >= 0]] = np.nonzero(perm >= 0)[0]
    return inv


IN_PERM = _in_perm_index()
UQ_PERM = _head_perm_index(NOPE, ROPE)
UKV_PERM = _head_perm_index(NOPE, VH)


def _to_gathered(perm, shard, pad):
    return np.where(perm >= 0, (perm // shard) * pad + perm % shard, -1)


def _from_full(inv, shard, pad):
    j, i = np.divmod(np.arange(NDEV * pad), pad)
    return np.where(i < shard, inv[np.minimum(j * shard + i, inv.shape[0] - 1)], -1)


def col_gather(name, srcs, out_shapes, jobs, deps=()):
    ns, nj, nd, no = len(srcs), len(jobs), len(deps), len(out_shapes)
    tables = [jnp.asarray(np.asarray(job[5], np.int32)[None, :]) for job in jobs]

    def view(ref, col0, width, r0, rc):
        n = ref.shape[-1]
        if len(ref.shape) == 3:
            return ref.at[col0 // n, pl.ds(r0, rc), pl.ds(col0 % n, width)]
        return ref.at[pl.ds(r0, rc), pl.ds(col0, width)]

    def slabs(shape):
        if len(shape) == 3:
            return [((d,), d * shape[2], (d + 1) * shape[2]) for d in range(shape[0])]
        w = 1024 if shape[1] > 1024 and shape[1] % 1024 == 0 else shape[1]
        return [((slice(None), pl.ds(c, w)), c, c + w) for c in range(0, shape[1], w)]

    src_slabs = [slabs(s.shape) for s in srcs]
    out_slabs = [slabs(sh) for sh in out_shapes]
    work, first_use, last_touch = [], {}, {}
    for ji, (si, srow, oi, orow, nrows, tgt) in enumerate(jobs):
        tgt = np.asarray(tgt)
        tw = 256 if out_shapes[oi][-1] % 256 == 0 else LANE
        sw = 256 if srcs[si].shape[-1] % 256 == 0 else LANE
        for t in range(tgt.shape[0] // tw):
            tt = tgt[t * tw:(t + 1) * tw]
            tiles = sorted(set((tt[tt >= 0] // sw).tolist()))
            straight = bool(tiles) and tt[0] >= 0 and tt[0] % LANE == 0 and np.array_equal(tt, tt[0] + np.arange(tw))
            cols = [(int(tt[0]) + k * LANE, LANE) for k in range(tw // LANE)] if straight else [(s * sw, sw) for s in tiles]
            need = sorted({(si, k) for c0, _ in cols for k, (_, lo, hi) in enumerate(src_slabs[si]) if lo <= c0 < hi})
            touch = [(oi, k) for k, (_, lo, hi) in enumerate(out_slabs[oi]) if lo <= t * tw < hi][0]
            for key in need:
                first_use.setdefault(key, len(work))
            last_touch[touch] = len(work)
            work.append((ji, t, tw, sw, tiles, straight, need, touch))
    in_order = sorted(first_use, key=first_use.get)
    in_sem = {key: i for i, key in enumerate(in_order)}
    out_keys = sorted(last_touch)
    out_sem = {key: i for i, key in enumerate(out_keys)}

    def body(*refs):
        src_hbm, tab_refs = refs[:ns], refs[ns:ns + nj]
        out_hbm = refs[ns + nj + nd:ns + nj + nd + no]
        scratch = refs[ns + nj + nd + no:]
        src_refs, out_refs, in_sems, out_sems = scratch[:ns], scratch[ns:ns + no], scratch[-2], scratch[-1]
        loads = {}
        for key in in_order:
            si, k = key
            idx = src_slabs[si][k][0]
            loads[key] = pltpu.make_async_copy(src_hbm[si].at[idx], src_refs[si].at[idx], in_sems.at[in_sem[key]])
            loads[key].start()
        arrived, stores = set(), []
        for wi, (ji, t, tw, sw, tiles, straight, need, touch) in enumerate(work):
            si, srow, oi, orow, nrows, tgt = jobs[ji]
            sref, oref = src_refs[si], out_refs[oi]
            rc = nrows if nrows <= 1024 else 1024
            for key in need:
                if key not in arrived:
                    loads[key].wait()
                    arrived.add(key)
            onehots = []
            if tiles and not straight:
                want = tab_refs[ji][:, t * tw:(t + 1) * tw]
                row = lax.broadcasted_iota(jnp.int32, (sw, tw), 0)
                onehots = [jnp.where(want == row + s * sw, 1.0, 0.0).astype(BF) for s in tiles]
            first = int(np.asarray(tgt)[t * tw])

            def chunk(ci, _, t=t, tw=tw, sw=sw, tiles=tiles, straight=straight, onehots=onehots, first=first,
                      sref=sref, oref=oref, srow=srow, orow=orow, rc=rc):
                r0 = ci * rc
                ro = pl.multiple_of(orow + r0, LANE)
                rs = pl.multiple_of(srow + r0, LANE)
                if not tiles:
                    view(oref, t * tw, tw, ro, rc)[...] = jnp.zeros((rc, tw), BF)
                elif straight:
                    for k in range(tw // LANE):
                        view(oref, t * tw + k * LANE, LANE, ro, rc)[...] = view(sref, first + k * LANE, LANE, rs, rc)[...]
                else:
                    acc = None
                    for s, oh in zip(tiles, onehots):
                        p = jnp.dot(view(sref, s * sw, sw, rs, rc)[...], oh, preferred_element_type=F32)
                        acc = p if acc is None else acc + p
                    view(oref, t * tw, tw, ro, rc)[...] = acc.astype(BF)
                return 0

            lax.fori_loop(0, nrows // rc, chunk, 0)
            if last_touch[touch] == wi:
                idx = out_slabs[touch[0]][touch[1]][0]
                cp = pltpu.make_async_copy(out_refs[touch[0]].at[idx], out_hbm[touch[0]].at[idx], out_sems.at[out_sem[touch]])
                cp.start()
                stores.append(cp)
        for cp in stores:
            cp.wait()

    return pl.pallas_call(
        body, name=name, in_specs=[ANY_SPEC] * ns + [pl.BlockSpec(memory_space=pltpu.VMEM)] * nj + [ANY_SPEC] * nd,
        out_specs=[ANY_SPEC] * no, out_shape=[jax.ShapeDtypeStruct(s, BF) for s in out_shapes],
        scratch_shapes=[pltpu.VMEM(s.shape, BF) for s in srcs] + [pltpu.VMEM(s, BF) for s in out_shapes]
        + [pltpu.SemaphoreType.DMA((len(in_order),)), pltpu.SemaphoreType.DMA((len(out_keys),))],
        compiler_params=_cparams(),
    )(*srcs, *tables, *deps)


def sum_adamw(name, recvs, w, m, v, lo=0, prev=None, row0=0):
    _, R, C = w.shape
    L = len(recvs)
    CP = recvs[0].shape[-1]
    tr = _pick_rows(R, 16, 128)
    n_prev = 0 if prev is None else 4

    def body(*refs):
        r_refs = refs[:L]
        w_ref, m_ref, v_ref = refs[L:L + 3]
        g_ref, d_ref, nm_ref, nv_ref, gsum = refs[L + 3 + n_prev:]
        layer = pl.program_id(0)
        for k in range(L):
            def total(k=k):
                acc = r_refs[k][0].astype(F32)
                for d in range(1, NDEV):
                    acc = acc + r_refs[k][d].astype(F32)
                gsum[...] = acc
            pl.when(layer == k)(total)
        gg = gsum[:, 0:C]
        nm = B1 * m_ref[...] + (1.0 - B1) * gg
        nv = B2 * v_ref[...] + (1.0 - B2) * jnp.square(gg)
        m_hat = nm / (1.0 - B1 ** STEP)
        v_hat = nv / (1.0 - B2 ** STEP)
        g_ref[...] = gg
        d_ref[...] = -LR * (m_hat / (jnp.sqrt(v_hat) + EPS) + WD * w_ref[...])
        nm_ref[...] = nm
        nv_ref[...] = nv

    assert row0 % tr == 0
    r_specs = [pl.BlockSpec((NDEV, tr, CP),
                            functools.partial(lambda l, i, k: (0, row0 // tr + jnp.where(l == k, i, 0), 0), k=k))
               for k in range(L)]
    blk = pl.BlockSpec((None, tr, C), lambda l, i: (l + lo, i, 0))
    shp = jax.ShapeDtypeStruct(w.shape, F32)
    return pl.pallas_call(
        body, name=name, grid=(L, R // tr), in_specs=r_specs + [blk] * 3 + [ANY_SPEC] * n_prev, out_specs=[blk] * 4,
        out_shape=[shp] * 4, input_output_aliases={L + 3 + i: i for i in range(n_prev)},
        scratch_shapes=[pltpu.VMEM((tr, CP), F32)], compiler_params=_cparams(),
    )(*recvs, w, m, v, *(prev or ()))


ALPHA = 8.0 ** 0.25
T_WIDE, T_NARROW = 512, 1024


def _rope_fn(sign):
    def fn(x, cos, sin):
        W = x.shape[-1]
        lane = lax.broadcasted_iota(jnp.int32, x.shape, 1)
        first_half = (lane % ROPE) < (ROPE // 2)
        rot = jnp.where(first_half, -pltpu.roll(x, W - ROPE // 2, 1), pltpu.roll(x, ROPE // 2, 1))
        return x * cos + sign * rot * sin
    return fn


def _modulate(xv, a):
    return xv * (1.0 + a[1:2, :]) + a[0:1, :]


def layer_fwd(x, ada3, W, tabs, S, u=None, ada_next=None):
    cos, sin = tabs
    T = T_NARROW
    if u is None:
        u = rowwise("modulate", _modulate, S, T, [(x, D_MODEL, 0)], [ada3], [(D_MODEL, BF)])[0]
    proj = mm(u, W["in"], name="mm_proj", tm=1024, tn=1024, out_dtype=BF)
    W = {**W, **W["late"](proj)}

    ca = conv_fwd("conv_a_fwd", proj, OFF_A, W["conv_a"], 31, "glu", S, CW)

    def a_post(c, ag, vec):
        n, _ = _ln_stats(c + vec[0:1, :])
        return _silu(n * vec[1:2, :] + vec[2:3, :]) * _silu(ag)

    h_a = rowwise("mix_a_post", a_post, S, T, [(ca, CW, 0), (proj, CW, OFF_AG)], [W["vec_a"]], [(CW, BF)])[0]
    y_a = mm(h_a, W["a_out"], name="mm_branch_out", out_dtype=BF)

    cb = conv_fwd("conv_b_fwd", proj, OFF_B, W["conv_b"], 3, "mul", S, CW)
    h_b = rowwise("mix_b_post", lambda c, gb, bg: gb * c * _silu(bg), S, T,
                  [(cb, CW, 0), (proj, CW, OFF_GB), (proj, CW, OFF_BG)], [], [(CW, BF)])[0]
    y_b = mm(h_b, W["b_out"], name="mm_branch_out", out_dtype=BF)

    def rms2(ql, kvl, gq, gkv):
        rq = lax.rsqrt(jnp.mean(ql * ql, axis=-1, keepdims=True) + RMS_EPS)
        rk = lax.rsqrt(jnp.mean(kvl * kvl, axis=-1, keepdims=True) + RMS_EPS)
        return ql * rq * gq, kvl * rk * gkv

    qn, kvn = rowwise("rms_fwd", rms2, S, T, [(proj, QL, OFF_Q), (proj, KVL, OFF_KV)], [W["gq"], W["gkv"]],
                      [(QL, BF), (KVL, BF)])
    q = mm(qn, W["uq"], name="mm_q")
    kv = mm(kvn, W["ukv"], name="mm_kv", out_dtype=BF)
    rope = _rope_fn(1.0)

    def rope_fwd(qv, kr, c1, s1):
        parts = []
        for g in range(QUADS):
            parts.append(qv[:, g * QW:g * QW + 2 * LANE].astype(BF))
            parts.append(rope(qv[:, g * QW + 2 * LANE:(g + 1) * QW], c1, s1).astype(BF))
        kp = rope(kr, c1, s1)
        kp = kp + pltpu.roll(kp, ROPE, 1) + pltpu.roll(kp, 2 * ROPE, 1) + pltpu.roll(kp, 3 * ROPE, 1)
        return jnp.concatenate(parts, axis=1), kp

    q_b, kpe = rowwise("rope_fwd", rope_fwd, S, T,
                       [(q, HEADS * (NOPE + ROPE), 0), (proj, LANE, OFF_KR), (cos, LANE, 0), (sin, LANE, 0)], [],
                       [(HEADS * (NOPE + ROPE), BF), (LANE, BF)])
    o, lse = attn_fwd(q_b, kv, kpe, S)
    h_c = rowwise("mix_c_post", lambda ov, cg: ov * _silu(cg), S, T, [(o, CW, 0), (proj, CW, OFF_CG)], [],
                  [(CW, BF)])[0]
    y_c = mm(h_c, W["c_out"], name="mm_branch_out", out_dtype=BF)

    def merge(la, lb, lc, ya, yb, yc):
        return _sigmoid(la) * ya + _sigmoid(lb) * yb + _sigmoid(lc) * yc

    m = rowwise("merge_fwd", merge, S, T_WIDE,
                [(proj, D_MODEL, 0), (proj, D_MODEL, 1024), (proj, D_MODEL, 2048), (y_a, D_MODEL, 0),
                 (y_b, D_MODEL, 0), (y_c, D_MODEL, 0)], [], [(D_MODEL, BF)])[0]
    out = mm(m, W["o"], name="mm_out")

    def ln_fwd(xv, ov, a, lnv, *nxt):
        n, _ = _ln_stats(ALPHA * xv + a[2:3, :] * ov)
        y = n * lnv[0:1, :] + lnv[1:2, :]
        return (y, _modulate(y, nxt[0])) if nxt else y

    res = rowwise("ln_fwd", ln_fwd, S, T_WIDE, [(x, D_MODEL, 0), (out, D_MODEL, 0)],
                  [ada3, W["lnv"]] + ([] if ada_next is None else [ada_next]),
                  [(D_MODEL, F32)] + ([] if ada_next is None else [(D_MODEL, BF)]))
    saved = dict(x=x, u=u, proj=proj, ca=ca, cb=cb, h_a=h_a, h_b=h_b, h_c=h_c, y_a=y_a, y_b=y_b, y_c=y_c, qn=qn,
                 kvn=kvn, q_b=q_b, kv=kv, kpe=kpe, lse=lse, o=o, m=m, out=out)
    return res[0], saved, W, (res[1] if ada_next is not None else None)


def layer_bwd(dxn, sv, ada3, W, tabs, S, before_in=None):
    cos, sin = tabs
    T = T_NARROW
    x, proj = sv["x"], sv["proj"]
    G = {}

    def ln_bwd(xv, ov, dy, a, lnv):
        gate = a[2:3, :]
        n, rstd = _ln_stats(ALPHA * xv + gate * ov)
        dr = _ln_bwd(dy * lnv[0:1, :], n, rstd)
        return ALPHA * dr, gate * dr, _colsum(dy * n), _colsum(dy), _colsum(dr * ov)

    dres, d_out, G["ln_g"], G["ln_b"], d_gate = rowwise(
        "ln_bwd", ln_bwd, S, T_WIDE, [(x, D_MODEL, 0), (sv["out"], D_MODEL, 0), (dxn, D_MODEL, 0)], [ada3, W["lnv"]],
        [(D_MODEL, F32), (D_MODEL, BF)], [D_MODEL] * 3)
    dm = mm(d_out, W["o"], name="mm_dm", trans_b=True, out_dtype=BF)
    G["w_o"] = mm(sv["m"], d_out, name="mm_gw_o", trans_a=True, out_dtype=BF)

    def merge_bwd(dmv, la, lb, lc, ya, yb, yc):
        outs, dls = [], []
        for lg, yv in ((la, ya), (lb, yb), (lc, yc)):
            s = _sigmoid(lg)
            outs.append(dmv * s)
            dls.append((dmv * yv * s * (1.0 - s)).astype(BF))
        return (jnp.concatenate(dls, axis=1),) + tuple(outs)

    d_proj, dy_a, dy_b, dy_c = rowwise(
        "merge_bwd", merge_bwd, S, T_WIDE,
        [(dm, D_MODEL, 0), (proj, D_MODEL, 0), (proj, D_MODEL, 1024), (proj, D_MODEL, 2048), (sv["y_a"], D_MODEL, 0),
         (sv["y_b"], D_MODEL, 0), (sv["y_c"], D_MODEL, 0)], [], [(3 * D_MODEL, BF)] + [(D_MODEL, BF)] * 3,
        into=(None, NP, OFF_M))

    dh = {}
    for br, dy in (("a", dy_a), ("b", dy_b), ("c", dy_c)):
        dh[br] = mm(dy, W[br + "_out"], name="mm_dh", trans_b=True, out_dtype=BF)
        G["w_%s_out" % br] = mm(sv["h_" + br], dy, name="mm_gw_branch", trans_a=True, out_dtype=BF)

    def a_post_bwd(c, ag, dhv, vec):
        n, rstd = _ln_stats(c + vec[0:1, :])
        z = n * vec[1:2, :] + vec[2:3, :]
        d_ag = dhv * _silu(z) * _dsilu(ag)
        dz = dhv * _silu(ag) * _dsilu(z)
        dc = _ln_bwd(dz * vec[1:2, :], n, rstd)
        return d_ag, dc, _colsum(dc), _colsum(dz * n), _colsum(dz)

    d_proj, dca, G["conv_a_b"], G["ln_a_g"], G["ln_a_b"] = rowwise(
        "mix_a_post_bwd", a_post_bwd, S, T, [(sv["ca"], CW, 0), (proj, CW, OFF_AG), (dh["a"], CW, 0)], [W["vec_a"]],
        [(CW, BF), (CW, F32)], [CW] * 3, into=(d_proj, NP, OFF_AG))
    d_proj, G["conv_a_w"] = conv_bwd("conv_a_bwd", proj, OFF_A, dca, W["conv_a"], 31, "glu", S, CW, d_proj)

    def b_post_bwd(c, gb, bg, dhv):
        sg = _silu(bg)
        d_gb_bg = jnp.concatenate([(dhv * sg * c).astype(BF), (dhv * gb * c * _dsilu(bg)).astype(BF)], axis=1)
        return d_gb_bg, dhv * sg * gb

    d_proj, dcb = rowwise("mix_b_post_bwd", b_post_bwd, S, T,
                          [(sv["cb"], CW, 0), (proj, CW, OFF_GB), (proj, CW, OFF_BG), (dh["b"], CW, 0)], [],
                          [(2 * CW, BF), (CW, F32)], into=(d_proj, NP, OFF_GB))
    d_proj, G["conv_b_w"] = conv_bwd("conv_b_bwd", proj, OFF_B, dcb, W["conv_b"], 3, "mul", S, CW, d_proj)

    d_proj, d_o = rowwise("mix_c_post_bwd", lambda ov, cg, dhv: (dhv * ov * _dsilu(cg), dhv * _silu(cg)), S, T,
                          [(sv["o"], CW, 0), (proj, CW, OFF_CG), (dh["c"], CW, 0)], [], [(CW, BF), (CW, F32)],
                          into=(d_proj, NP, OFF_CG))
    dq, d_kv, dkp_heads = attn_bwd(sv["q_b"], sv["kv"], sv["kpe"], sv["o"], sv["lse"], d_o, S)
    ropeT = _rope_fn(-1.0)

    def rope_bwd(dqv, dkp, c1, s1):
        parts = []
        for g in range(QUADS):
            parts.append(dqv[:, g * QW:g * QW + 2 * LANE].astype(BF))
            parts.append(ropeT(dqv[:, g * QW + 2 * LANE:(g + 1) * QW], c1, s1).astype(BF))
        f = dkp[:, :LANE] + dkp[:, LANE:]
        f = f + pltpu.roll(f, 64, 1)
        f = f + pltpu.roll(f, 32, 1)
        lane = lax.broadcasted_iota(jnp.int32, f.shape, 1)
        return jnp.concatenate(parts, axis=1), jnp.where(lane < ROPE, ropeT(f, c1, s1), 0.0)

    d_q, dk_pe = rowwise("rope_bwd", rope_bwd, S, T,
                         [(dq, HEADS * (NOPE + ROPE), 0), (dkp_heads, HEADS * ROPE, 0), (cos, LANE, 0), (sin, LANE, 0)],
                         [], [(HEADS * (NOPE + ROPE), BF), (LANE, BF)])
    d_qn = mm(d_q, W["uq"], name="mm_dqn", trans_b=True, out_dtype=BF)
    d_kvn = mm(d_kv, W["ukv"], name="mm_dkvn", trans_b=True, out_dtype=BF)
    G["w_uq"] = mm(sv["qn"], d_q, name="mm_gw_uq", trans_a=True, out_dtype=BF)
    G["w_ukv"] = mm(sv["kvn"], d_kv, name="mm_gw_ukv", trans_a=True, out_dtype=BF)

    def rms_bwd(ql, kvl, dqn, dkn, dkp, gq, gkv):
        res = []
        for xv, dy, g in ((ql, dqn, gq), (kvl, dkn, gkv)):
            r = lax.rsqrt(jnp.mean(xv * xv, axis=-1, keepdims=True) + RMS_EPS)
            dxh = dy * g
            res.append(((r * (dxh - xv * (r * r) * jnp.mean(dxh * xv, axis=-1, keepdims=True))).astype(BF),
                        _colsum(dy * xv * r)))
        pad = jnp.zeros((ql.shape[0], LANE), BF)
        return jnp.concatenate([res[1][0], dkp, pad, res[0][0], pad], axis=1), res[0][1], res[1][1]

    d_proj, G["q_norm_g"], G["kv_norm_g"] = rowwise(
        "rms_bwd", rms_bwd, S, T,
        [(proj, QL, OFF_Q), (proj, KVL, OFF_KV), (d_qn, QL, 0), (d_kvn, KVL, 0), (dk_pe, LANE, 0)],
        [W["gq"], W["gkv"]], [(NP - OFF_KV, BF)], [QL, KVL], into=(d_proj, NP, OFF_KV))
    deps = before_in(G) if before_in is not None else ()
    du = mm(d_proj, W["in"], name="mm_du", trans_b=True, tm=1024, tk=2048, deps=deps)
    G["w_in"] = mm(sv["u"], d_proj, name="mm_gw_in", trans_a=True, out_dtype=BF, tm=1024, tk=2048, deps=deps)

    def mod_bwd(duv, xv, dr, a):
        return duv * (1.0 + a[1:2, :]) + dr, _colsum(duv), _colsum(duv * xv)

    dx, d_shift, d_scale = rowwise("mod_bwd", mod_bwd, S, T_WIDE, [(du, D_MODEL, 0), (x, D_MODEL, 0), (dres, D_MODEL, 0)],
                                   [ada3], [(D_MODEL, F32)], [D_MODEL] * 2)
    d_ada = jnp.concatenate([d_shift, d_scale, d_gate], axis=1)
    return dx, G, d_ada


SMALL = ("conv_a_b", "ln_a_g", "ln_a_b", "q_norm_g", "kv_norm_g", "ln_g", "ln_b")


def _rows(v):
    n = v.shape[0]
    r = -(-n // (LANE * 16)) * 16
    return jnp.pad(v, (0, r * LANE - n)).reshape(r, LANE)


def kernel(x, c, positions, w_ada, b_ada, w_in, conv_a_w, conv_a_b, ln_a_g, ln_a_b, w_a_out, conv_b_w, w_b_out, q_norm_g, kv_norm_g, w_uq, w_ukv, w_c_out, w_o, ln_g, ln_b, loss_target, m_w_ada, m_b_ada, m_w_in, m_conv_a_w, m_conv_a_b, m_ln_a_g, m_ln_a_b, m_w_a_out, m_conv_b_w, m_w_b_out, m_q_norm_g, m_kv_norm_g, m_w_uq, m_w_ukv, m_w_c_out, m_w_o, m_ln_g, m_ln_b, v_w_ada, v_b_ada, v_w_in, v_conv_a_w, v_conv_a_b, v_ln_a_g, v_ln_a_b, v_w_a_out, v_conv_b_w, v_w_b_out, v_q_norm_g, v_kv_norm_g, v_w_uq, v_w_ukv, v_w_c_out, v_w_o, v_ln_g, v_ln_b):
    P = dict(w_ada=w_ada, b_ada=b_ada, w_in=w_in, conv_a_w=conv_a_w, conv_a_b=conv_a_b, ln_a_g=ln_a_g, ln_a_b=ln_a_b,
             w_a_out=w_a_out, conv_b_w=conv_b_w, w_b_out=w_b_out, q_norm_g=q_norm_g, kv_norm_g=kv_norm_g, w_uq=w_uq,
             w_ukv=w_ukv, w_c_out=w_c_out, w_o=w_o, ln_g=ln_g, ln_b=ln_b)
    Mo = dict(w_ada=m_w_ada, b_ada=m_b_ada, w_in=m_w_in, conv_a_w=m_conv_a_w, conv_a_b=m_conv_a_b, ln_a_g=m_ln_a_g,
              ln_a_b=m_ln_a_b, w_a_out=m_w_a_out, conv_b_w=m_conv_b_w, w_b_out=m_w_b_out, q_norm_g=m_q_norm_g,
              kv_norm_g=m_kv_norm_g, w_uq=m_w_uq, w_ukv=m_w_ukv, w_c_out=m_w_c_out, w_o=m_w_o, ln_g=m_ln_g, ln_b=m_ln_b)
    Vo = dict(w_ada=v_w_ada, b_ada=v_b_ada, w_in=v_w_in, conv_a_w=v_conv_a_w, conv_a_b=v_conv_a_b, ln_a_g=v_ln_a_g,
              ln_a_b=v_ln_a_b, w_a_out=v_w_a_out, conv_b_w=v_conv_b_w, w_b_out=v_w_b_out, q_norm_g=v_q_norm_g,
              kv_norm_g=v_kv_norm_g, w_uq=v_w_uq, w_ukv=v_w_ukv, w_c_out=v_w_c_out, w_o=v_w_o, ln_g=v_ln_g, ln_b=v_ln_b)
    ORDER = ("w_ada", "b_ada", "w_in", "conv_a_w", "conv_a_b", "ln_a_g", "ln_a_b", "w_a_out", "conv_b_w", "w_b_out",
             "q_norm_g", "kv_norm_g", "w_uq", "w_ukv", "w_c_out", "w_o", "ln_g", "ln_b")
    L = w_ada.shape[0]
    S = x.shape[1]
    me = 4 * lax.axis_index("x") + 2 * lax.axis_index("y") + lax.axis_index("c")
    x2 = x[0]
    tgt = loss_target[0]

    small_in = _rows(jnp.concatenate([c.reshape(-1), conv_a_w.reshape(-1), conv_b_w.reshape(-1)]))
    w_in_b = jnp.pad(w_in.astype(BF), ((0, 0), (0, 0), (0, W_IN_PAD - IN_SHARD)))
    misc_b = jnp.concatenate([w_a_out, w_b_out, w_c_out, w_ukv, jnp.pad(w_uq, ((0, 0), (0, 0), (0, LANE - UQ_SHARD)))],
                             axis=1).astype(BF)
    w_o_b = w_o.astype(BF)
    gathered = [None] * L
    sg = exchange("gather_small", [small_in], [])[0]
    sgf = sg.reshape(NDEV, -1)
    c_all = sgf[:, :D_MODEL]
    o1 = D_MODEL + L * 31 * 64
    conv_a_full = sgf[:, D_MODEL:o1].reshape(NDEV, L, 31, 64).transpose(1, 2, 0, 3).reshape(L, 31, CW)
    conv_b_full = sgf[:, o1:o1 + L * 3 * 64].reshape(NDEV, L, 3, 64).transpose(1, 2, 0, 3).reshape(L, 3, CW)

    c_act = rowwise("silu_c", _silu, 16, 16, [(jnp.pad(c_all, ((0, 8), (0, 0))), D_MODEL, 0)], [], [(D_MODEL, BF)])[0]
    ncol = w_ada.shape[2]
    w_ada_b = w_ada.astype(BF).transpose(1, 0, 2).reshape(D_MODEL, L * ncol)
    b_mine = lax.dynamic_slice_in_dim(b_ada, me * ncol, ncol, axis=1).reshape(1, L * ncol)
    ada_part = mm(c_act, w_ada_b, name="mm_ada", bias=b_mine)
    ada_rows = -(-(L * ncol) // (LANE * 8)) * 8
    ada_send = jnp.pad(ada_part[:NDEV].reshape(NDEV, -1, LANE), ((0, 0), (0, ada_rows - L * ncol // LANE), (0, 0)))
    ada_recv = exchange("a2a_ada", [], [ada_send])[0]
    ada = ada_recv[:, :L * ncol // LANE].reshape(NDEV, L, ncol).transpose(1, 0, 2).reshape(L, 3, D_MODEL)
    gathered[0] = [gather_two_level("gather0_w_in", w_in_b[0], ada)]
    pending_rest, rest_token = exchange_begin("gather0_rest", [misc_b[0], w_o_b[0]], 2, gathered[0][0])

    inv_freq = ROPE_THETA ** (-jnp.arange(0, ROPE, 2, dtype=F32) / ROPE)
    ang = positions[0].astype(F32)[:, None] * inv_freq
    tabs = (jnp.tile(jnp.cos(ang), (1, 2 * LANE // ROPE)), jnp.tile(jnp.sin(ang), (1, 2 * LANE // ROPE)))

    straight = np.arange(D_MODEL)
    fwd_in = [(0, 0, 0, 0, D_MODEL, _to_gathered(IN_PERM, IN_SHARD, W_IN_PAD))]
    fwd_misc = [(0, ROW_A, 0, 0, CW, straight), (0, ROW_B, 1, 0, CW, straight), (0, ROW_C, 2, 0, CW, straight),
                (0, ROW_UKV, 3, 0, KVL, UKV_PERM), (0, ROW_UQ, 4, 0, QL, _to_gathered(UQ_PERM, UQ_SHARD, LANE))]
    rev_in = [(0, 0, 0, 0, D_MODEL, _from_full(_inverse(IN_PERM, D_IN), IN_SHARD, W_IN_PAD))]
    rev_misc = [(0, 0, 0, ROW_A, CW, straight), (1, 0, 0, ROW_B, CW, straight), (2, 0, 0, ROW_C, CW, straight),
                (3, 0, 0, ROW_UKV, KVL, _from_full(_inverse(UKV_PERM, HEADS * (NOPE + VH)), LANE, LANE)),
                (4, 0, 0, ROW_UQ, QL, _from_full(_inverse(UQ_PERM, HEADS * (NOPE + ROPE)), UQ_SHARD, LANE))]

    def layer_weights(l, deps):
        w_in_p = col_gather("relayout_w_in", [gathered[l][0]], [(D_MODEL, NP)], fwd_in, deps)[0]

        def late(after):
            if len(gathered[l]) == 1:
                gathered[l] += exchange_end(pending_rest, after)
            _, g_misc, g_o = gathered[l]
            a_out, b_out, c_out, ukv, uq = col_gather(
                "relayout_misc", [g_misc],
                [(CW, D_MODEL)] * 3 + [(KVL, HEADS * (NOPE + VH)), (QL, HEADS * (NOPE + ROPE))], fwd_misc, deps)
            return {"a_out": a_out, "b_out": b_out, "c_out": c_out, "uq": uq, "ukv": ukv,
                    "o": g_o.reshape(D_MODEL, D_MODEL)}

        return {
            "in": w_in_p, "late": late,
            "conv_a": jnp.pad(conv_a_full[l], ((0, 1), (0, 0))), "conv_b": jnp.pad(conv_b_full[l], ((0, 5), (0, 0))),
            "vec_a": jnp.stack([conv_a_b[l], ln_a_g[l], ln_a_b[l]]), "gq": q_norm_g[l][None], "gkv": kv_norm_g[l][None],
            "lnv": jnp.stack([ln_g[l], ln_b[l]]),
        }

    h = x2
    saved, weights = [], []
    handles, token, u_next = {}, rest_token, None
    for l in range(1, L):
        handles[l], token = exchange_begin("gather%d" % l, [w_in_b[l], misc_b[l], w_o_b[l]], 3, token)
    for l in range(L):
        ada_l, deps = (ada[l] + token[0, 0], (token,)) if l == 0 else (ada[l], ())
        h, sv, Wl, u_next = layer_fwd(h, ada_l, layer_weights(l, deps), tabs, S, u_next, ada[l + 1] if l + 1 < L else None)
        if l + 1 < L:
            gathered[l + 1] = exchange_end(handles[l + 1], h)
        saved.append(sv)
        weights.append(Wl)

    def loss_fn(y, t):
        e = y - t
        return e * (1.0 / D_MODEL), _colsum(e * e)

    dy, sq = rowwise("loss", loss_fn, S, 256, [(h, D_MODEL, 0), (tgt, D_MODEL, 0)], [], [(D_MODEL, F32)], [D_MODEL])
    loss = lax.psum(0.5 * jnp.sum(sq) / D_MODEL, ("x", "y", "c"))
    loss, dy = lax.optimization_barrier((loss, dy))

    grads, d_adas, recv = [None] * L, [None] * L, [None] * L
    pending, token = None, None

    def send_rest(g):
        send_misc = col_gather("unrelayout_misc", [g["w_a_out"], g["w_b_out"], g["w_c_out"], g["w_ukv"], g["w_uq"]],
                               [(NDEV, MISC_ROWS, LANE)], rev_misc)[0]
        return [send_misc, g["w_o"].reshape(NDEV, D_MODEL // NDEV, D_MODEL)]

    rest0 = []

    def early_rest(g):
        handle, tok = exchange_begin("scatter0_rest", send_rest(g), 0, g["w_o"])
        rest0.append(handle)
        return (tok,)

    for l in reversed(range(L)):
        ada_l = ada[l] if token is None else ada[l] + token[0, 0]
        dy, g, d_adas[l] = layer_bwd(dy, saved[l], ada_l, weights[l], tabs, S, early_rest if l == 0 else None)
        grads[l] = g
        if pending is not None:
            recv[l + 1] = exchange_end(pending, dy)
        send_in = col_gather("unrelayout_w_in", [g["w_in"]], [(NDEV, D_MODEL, W_IN_PAD)], rev_in)[0]
        if l == 0:
            def layer_vec(i):
                return jnp.concatenate([grads[i][n].reshape(-1) for n in SMALL] + [d_adas[i].reshape(-1)])

            def to_owners(name, taps):
                full = jnp.stack([grads[i][name][:taps] for i in range(L)])
                return full.reshape(L, taps, NDEV, CW // NDEV).transpose(2, 0, 1, 3).reshape(NDEV, -1)

            conv_send = jnp.concatenate([to_owners("conv_a_w", 31), to_owners("conv_b_w", 3)], axis=1)
            conv_rows = -(-conv_send.shape[1] // (LANE * 16)) * 16
            conv_send = jnp.pad(conv_send, ((0, 0), (0, conv_rows * LANE - conv_send.shape[1])))
            small_sizes = [int(grads[0][n].size) for n in SMALL] + [3 * D_MODEL]
            gsmall, conv_recv = exchange("gather_small_grads", [_rows(jnp.concatenate([layer_vec(i) for i in range(L)]))],
                                         [conv_send.reshape(NDEV, conv_rows, LANE)])
            pending, token = exchange_begin("scatter0", [send_in], 0, gsmall)
        else:
            pending, token = exchange_begin("scatter%d" % l, [send_in] + send_rest(g), 0,
                                            dy if l + 1 == L else recv[l + 1][0])
    grad_x = dy[None]

    gsmall = gsmall + token[0, 0]
    gsum = sum_slots("sum_small", gsmall).reshape(-1)
    recv[0] = [None] + exchange_end(rest0[0], gsum)
    Gr = {}
    offs = np.cumsum([0] + small_sizes)
    per_layer = int(offs[-1])
    gsum = gsum[:L * per_layer].reshape(L, per_layer)
    for i, n in enumerate(SMALL):
        Gr[n] = gsum[:, offs[i]:offs[i + 1]]
    csum = sum_slots("sum_conv", conv_recv + token[0, 0]).reshape(-1)
    n_a = L * 31 * (CW // NDEV)
    Gr["conv_a_w"] = csum[:n_a].reshape(L, 31, CW // NDEV)
    Gr["conv_b_w"] = csum[n_a:n_a + L * 3 * (CW // NDEV)].reshape(L, 3, CW // NDEV)
    Gr["b_ada"] = gsum[:, offs[7]:offs[8]]
    d_ada_all = gsmall.reshape(NDEV, -1)[:, :L * per_layer].reshape(NDEV, L, per_layer)[:, :, offs[7]:offs[8]]
    d_mine = lax.dynamic_slice_in_dim(d_ada_all, me * ncol, ncol, axis=2).reshape(NDEV, L * ncol)
    g_ada = mm(c_act, jnp.pad(d_mine, ((0, 8), (0, 0))).astype(BF), name="mm_gw_ada", trans_a=True)
    Gr["w_ada"] = g_ada.reshape(D_MODEL, L, ncol).transpose(1, 0, 2)

    D, NM, NV = {}, {}, {}
    D["w_ada"], NM["w_ada"], NV["w_ada"] = adamw("adamw_w_ada", P["w_ada"], Gr["w_ada"], Mo["w_ada"], Vo["w_ada"])
    Gr["w_o"], D["w_o"], NM["w_o"], NV["w_o"] = sum_adamw(
        "sum_adamw_w_o", [recv[l][2] for l in range(L)], P["w_o"], Mo["w_o"], Vo["w_o"])
    for n, row0 in (("w_a_out", ROW_A), ("w_b_out", ROW_B), ("w_c_out", ROW_C), ("w_ukv", ROW_UKV), ("w_uq", ROW_UQ)):
        Gr[n], D[n], NM[n], NV[n] = sum_adamw("sum_adamw_" + n, [recv[l][1] for l in range(L)], P[n], Mo[n], Vo[n],
                                              row0=row0)
    w_l, m_l, v_l, _ = lax.optimization_barrier((P["w_in"], Mo["w_in"], Vo["w_in"], token))
    upper = sum_adamw("sum_adamw_w_in_upper", [recv[l][0] for l in range(1, L)], w_l, m_l, v_l, lo=1)
    recv[0][0] = exchange_end(pending, upper[1])[0]
    Gr["w_in"], D["w_in"], NM["w_in"], NV["w_in"] = sum_adamw(
        "sum_adamw_w_in", [recv[0][0]], w_l, m_l, v_l, lo=0, prev=upper)
    for n in ("b_ada", "conv_a_w", "conv_b_w") + SMALL:
        shape = P[n].shape if P[n].ndim == 3 else (1,) + P[n].shape
        res = adamw("adamw_" + n, *[t.reshape(shape) for t in (P[n], Gr[n], Mo[n], Vo[n])])
        D[n], NM[n], NV[n] = [t.reshape(P[n].shape) for t in res]
    return (loss, grad_x, *[Gr[n] for n in ORDER], *[D[n] for n in ORDER], *[NM[n] for n in ORDER],
            *[NV[n] for n in ORDER])
```

```python
import functools
import math

import numpy as np
import jax
import jax.numpy as jnp
from jax import lax
from jax.experimental import pallas as pl
from jax.experimental.pallas import tpu as pltpu

BF = jnp.bfloat16
F32 = jnp.float32
MESH = pl.DeviceIdType.MESH
NDEV = 8

HEADS, NOPE, ROPE, VH = 8, 64, 32, 64
HP = 128
ROPE_THETA = 10000.0
LN_EPS = 1e-5
RMS_EPS = 1e-6
LR, B1, B2, EPS, WD, STEP = 0.001, 0.9, 0.999, 1e-08, 0.01, 10

LANE = 128
VMEM_LIMIT = 56 * 1024 * 1024

D_MODEL, CW, QL, KVL = 1024, 512, 384, 256
OFF_M, OFF_A, OFF_AG, OFF_B, OFF_CG, OFF_GB, OFF_BG = 0, 3072, 4096, 4608, 5632, 6144, 6656
OFF_KV, OFF_KR, OFF_Q, NP = 7168, 7424, 7680, 8192
D_IN = 7840


def _cparams(**kw):
    return pltpu.CompilerParams(vmem_limit_bytes=VMEM_LIMIT, **kw)


def _sigmoid(x):
    return jax.nn.sigmoid(x)


def _silu(x):
    return x * _sigmoid(x)


def _dsilu(x):
    s = _sigmoid(x)
    return s * (1.0 + x * (1.0 - s))


def _pick_tile(n, cap, mult):
    if n <= cap:
        return n
    for t in range(cap - cap % mult, 0, -mult):
        if n % t == 0:
            return t
    raise ValueError((n, cap, mult))


def mm(a, b, *, name, trans_a=False, trans_b=False, out_dtype=F32, bias=None, tm=1024, tn=1024, tk=2048, deps=()):
    if trans_a:
        K, M = a.shape
    else:
        M, K = a.shape
    if trans_b:
        N, K2 = b.shape
    else:
        K2, N = b.shape
    assert K == K2 and not (trans_a and trans_b), (a.shape, b.shape)
    tm, tn = _pick_tile(M, tm, 16), _pick_tile(N, tn, LANE)
    tk = _pick_tile(K, tk, LANE if trans_b else 16)
    assert M % tm == 0 and N % tn == 0 and K % tk == 0, (M, N, K, tm, tn, tk)
    nk = K // tk
    dims = (((0 if trans_a else 1,), (1 if trans_b else 0,)), ((), ()))
    has_bias = bias is not None

    def body(*refs):
        a_ref, b_ref = refs[0], refs[1]
        bias_ref = refs[2] if has_bias else None
        o_ref = refs[(3 if has_bias else 2) + len(deps)]
        p = lax.dot_general(a_ref[...], b_ref[...], dims, preferred_element_type=F32)

        def finish(v):
            if has_bias:
                v = v + bias_ref[...]
            o_ref[...] = v.astype(o_ref.dtype)

        if nk == 1:
            finish(p)
        else:
            acc = refs[-1]
            k = pl.program_id(2)

            @pl.when(k == 0)
            def _():
                acc[...] = p

            @pl.when(k > 0)
            def _():
                acc[...] += p

            @pl.when(k == nk - 1)
            def _():
                finish(acc[...])

    if trans_a:
        a_spec = pl.BlockSpec((tk, tm), lambda i, j, k: (k, i))
    else:
        a_spec = pl.BlockSpec((tm, tk), lambda i, j, k: (i, k))
    if trans_b:
        b_spec = pl.BlockSpec((tn, tk), lambda i, j, k: (j, k))
    else:
        b_spec = pl.BlockSpec((tk, tn), lambda i, j, k: (k, j))
    in_specs = [a_spec, b_spec]
    args = [a, b]
    if has_bias:
        in_specs.append(pl.BlockSpec((1, tn), lambda i, j, k: (0, j)))
        args.append(bias)
    in_specs += [ANY_SPEC] * len(deps)
    args += list(deps)
    return pl.pallas_call(
        body, name=name, grid=(M // tm, N // tn, nk),
        in_specs=in_specs, out_specs=pl.BlockSpec((tm, tn), lambda i, j, k: (i, j)),
        out_shape=jax.ShapeDtypeStruct((M, N), out_dtype),
        scratch_shapes=[pltpu.VMEM((tm, tn), F32)] if nk > 1 else [],
        compiler_params=_cparams(),
    )(*args)


def rowwise(name, fn, S, T, row_ins, full_ins, row_outs, acc_outs=(), into=None):
    n_in = len(row_ins) + len(full_ins)
    n_ro, n_ao = len(row_outs), len(acc_outs)
    alias = into is not None and into[0] is not None
    T = min(T, S)

    def body(*refs):
        vals = [r[...] for r in refs[:n_in]]
        vals = [v.astype(F32) if v.dtype == BF else v for v in vals]
        outs = fn(*vals)
        if not isinstance(outs, (tuple, list)):
            outs = (outs,)
        assert len(outs) == n_ro + n_ao, (name, len(outs))
        o0 = n_in + (1 if alias else 0)
        for r, v in zip(refs[o0:o0 + n_ro], outs[:n_ro]):
            r[...] = v.astype(r.dtype)
        first = pl.program_id(0) == 0
        for r, v in zip(refs[o0 + n_ro:], outs[n_ro:]):
            def init(r=r, v=v):
                r[...] = v

            def accum(r=r, v=v):
                r[...] += v

            pl.when(first)(init)
            pl.when(jnp.logical_not(first))(accum)

    in_specs, args = [], []
    for arr, W, off in row_ins:
        assert off % W == 0 and arr.shape[0] == S, (name, arr.shape, W, off)
        in_specs.append(pl.BlockSpec((T, W), functools.partial(lambda i, cb: (i, cb), cb=off // W)))
        args.append(arr)
    for arr in full_ins:
        in_specs.append(pl.BlockSpec(arr.shape, lambda i: (0, 0)))
        args.append(arr)
    out_specs = [pl.BlockSpec((T, W), lambda i: (i, 0)) for W, _ in row_outs]
    out_shape = [jax.ShapeDtypeStruct((S, W), dt) for W, dt in row_outs]
    aliases = {}
    if into is not None:
        buf, total, off = into
        W0, dt0 = row_outs[0]
        assert off % W0 == 0
        out_specs[0] = pl.BlockSpec((T, W0), functools.partial(lambda i, cb: (i, cb), cb=off // W0))
        out_shape[0] = jax.ShapeDtypeStruct((S, total), dt0)
        if alias:
            in_specs.append(ANY_SPEC)
            args.append(buf)
            aliases = {n_in: 0}
    out_specs += [pl.BlockSpec((1, W), lambda i: (0, 0)) for W in acc_outs]
    out_shape += [jax.ShapeDtypeStruct((1, W), F32) for W in acc_outs]
    return pl.pallas_call(
        body, name=name, grid=(S // T,), in_specs=in_specs, out_specs=out_specs, out_shape=out_shape,
        input_output_aliases=aliases, compiler_params=_cparams(),
    )(*args)


def _colsum(v):
    return jnp.sum(v, axis=0, keepdims=True)


def _ln_stats(r):
    mu = jnp.mean(r, axis=-1, keepdims=True)
    d = r - mu
    var = jnp.mean(d * d, axis=-1, keepdims=True)
    rstd = lax.rsqrt(var + LN_EPS)
    return d * rstd, rstd


def _ln_bwd(dn, n, rstd):
    return rstd * (dn - jnp.mean(dn, axis=-1, keepdims=True) - n * jnp.mean(dn * n, axis=-1, keepdims=True))


CPAD = 32
TC = 64


def _pre(mode, x1, x2):
    return x1 * _sigmoid(x2) if mode == "glu" else x1 * x2


def _sublane_shifts(ext):
    n = TC + CPAD
    return [ext] + [pltpu.roll(ext, n - r, 0) for r in range(1, 8)]


def _shifted(shifts, sft):
    q, r = divmod(sft, 8)
    return shifts[r][8 * q:8 * q + TC]


def _interleaved_specs(S, off):
    return [pl.BlockSpec((S, LANE), functools.partial(lambda j, o: (0, o + 2 * j), o=off // LANE)),
            pl.BlockSpec((S, LANE), functools.partial(lambda j, o: (0, o + 2 * j + 1), o=off // LANE))]


def conv_fwd(name, src, off, w_pad, taps, mode, S, C):
    nchunk = S // TC

    def body(x1_ref, x2_ref, w_ref, o_ref, a_pad):
        a_pad[0:CPAD, :] = jnp.zeros((CPAD, LANE), F32)

        def fill(i, _):
            r = pl.multiple_of(i * 256, 256)
            a_pad[pl.ds(CPAD + r, 256), :] = _pre(mode, x1_ref[pl.ds(r, 256), :].astype(F32),
                                                  x2_ref[pl.ds(r, 256), :].astype(F32))
            return 0

        lax.fori_loop(0, S // 256, fill, 0)

        def chunk(i, _):
            base = pl.multiple_of(i * TC, TC)
            shifts = _sublane_shifts(a_pad[pl.ds(base, TC + CPAD), :])
            acc = jnp.zeros((TC, LANE), F32)
            for k in range(taps):
                acc = acc + w_ref[pl.ds(k, 1), :] * _shifted(shifts, CPAD - (taps - 1) + k)
            o_ref[pl.ds(base, TC), :] = acc
            return 0

        lax.fori_loop(0, nchunk, chunk, 0)

    kp = w_pad.shape[0]
    return pl.pallas_call(
        body, name=name, grid=(C // LANE,),
        in_specs=_interleaved_specs(S, off) + [pl.BlockSpec((kp, LANE), lambda j: (0, j))],
        out_specs=pl.BlockSpec((S, LANE), lambda j: (0, j)),
        out_shape=jax.ShapeDtypeStruct((S, C), F32),
        scratch_shapes=[pltpu.VMEM((S + CPAD, LANE), F32)],
        compiler_params=_cparams(),
    )(src, src, w_pad)


def conv_bwd(name, src, off, dc, w_pad, taps, mode, S, C, buf):
    nchunk = S // TC
    kp = w_pad.shape[0]

    def body(x1_ref, x2_ref, dc_ref, w_ref, _, d_ref, dw_ref, a_pad, dc_pad, dw_acc):
        a_pad[0:CPAD, :] = jnp.zeros((CPAD, LANE), F32)
        dc_pad[S:S + CPAD, :] = jnp.zeros((CPAD, LANE), F32)
        dw_acc[...] = jnp.zeros(dw_acc.shape, F32)

        def fill(i, _):
            r = pl.multiple_of(i * 256, 256)
            a_pad[pl.ds(CPAD + r, 256), :] = _pre(mode, x1_ref[pl.ds(r, 256), :].astype(F32),
                                                  x2_ref[pl.ds(r, 256), :].astype(F32))
            dc_pad[pl.ds(r, 256), :] = dc_ref[pl.ds(r, 256), :]
            return 0

        lax.fori_loop(0, S // 256, fill, 0)

        def chunk(i, _):
            base = pl.multiple_of(i * TC, TC)
            shifts_d = _sublane_shifts(dc_pad[pl.ds(base, TC + CPAD), :])
            shifts_a = _sublane_shifts(a_pad[pl.ds(base, TC + CPAD), :])
            dcv = shifts_d[0][0:TC]
            da = jnp.zeros((TC, LANE), F32)
            for k in range(taps):
                da = da + w_ref[pl.ds(k, 1), :] * _shifted(shifts_d, taps - 1 - k)
                prod = dcv * _shifted(shifts_a, CPAD - (taps - 1) + k)
                fold = prod[0:8]
                for g in range(1, TC // 8):
                    fold = fold + prod[8 * g:8 * g + 8]
                dw_acc[pl.ds(8 * k, 8), :] += fold
            x1 = x1_ref[pl.ds(base, TC), :].astype(F32)
            x2 = x2_ref[pl.ds(base, TC), :].astype(F32)
            if mode == "glu":
                s = _sigmoid(x2)
                d1, d2 = da * s, da * x1 * s * (1.0 - s)
            else:
                d1, d2 = da * x2, da * x1
            d_ref[pl.ds(base, TC), 0:LANE] = d1.astype(BF)
            d_ref[pl.ds(base, TC), LANE:2 * LANE] = d2.astype(BF)
            return 0

        lax.fori_loop(0, nchunk, chunk, 0)
        dw_ref[...] = jnp.zeros(dw_ref.shape, F32)
        for k in range(taps):
            dw_ref[pl.ds(k, 1), :] = jnp.sum(dw_acc[pl.ds(8 * k, 8), :], axis=0, keepdims=True)

    blk = pl.BlockSpec((S, LANE), lambda j: (0, j))
    return pl.pallas_call(
        body, name=name, grid=(C // LANE,),
        in_specs=_interleaved_specs(S, off) + [blk, pl.BlockSpec((kp, LANE), lambda j: (0, j)), ANY_SPEC],
        out_specs=[pl.BlockSpec((S, 2 * LANE), functools.partial(lambda j, o: (0, o + j), o=off // (2 * LANE))),
                   pl.BlockSpec((kp, LANE), lambda j: (0, j))],
        out_shape=[jax.ShapeDtypeStruct(buf.shape, BF), jax.ShapeDtypeStruct((kp, C), F32)],
        input_output_aliases={4: 0},
        scratch_shapes=[pltpu.VMEM((S + CPAD, LANE), F32), pltpu.VMEM((S + CPAD, LANE), F32),
                        pltpu.VMEM((8 * kp, LANE), F32)],
        compiler_params=_cparams(),
    )(src, src, dc, w_pad, buf)


FWD_TILES = (512, 512)
BWD_TILES = (512, 512)
QUADS = HEADS // 4
QW, KVW = 4 * (NOPE + ROPE), 4 * (NOPE + VH)
SCALE = (NOPE + ROPE) ** -0.5
NT_DIMS = (((1,), (1,)), ((), ()))
TN_DIMS = (((0,), (0,)), ((), ()))


def _lane_mask(width, group, dtype):
    lane = lax.broadcasted_iota(jnp.int32, (1, LANE), 1)
    return jnp.where(lane // width == group, 1.0, 0.0).astype(dtype)


def _visible(tq, tk, off):
    row = lax.broadcasted_iota(jnp.int32, (tq, tk), 0)
    col = lax.broadcasted_iota(jnp.int32, (tq, tk), 1)
    return col <= row + off


def _attn_tiles(S, tq, tk):
    tk = tk if S % tk == 0 else 256
    return min(tq, tk), tk


def attn_fwd(q, kv, kpe, S):
    tq, tk = _attn_tiles(S, *FWD_TILES)
    nq = S // tq

    def body(q_ref, kv_ref, kp_ref, o_ref, lse_ref):
        for t in range(2):
            cols = slice(t * LANE, (t + 1) * LANE)
            for hh in range(2):
                def q_block(qi, _, t=t, hh=hh, cols=cols):
                    r0 = qi * tq
                    qcat = jnp.concatenate([q_ref[pl.ds(r0, tq), cols] * _lane_mask(NOPE, hh, BF),
                                            q_ref[pl.ds(r0, tq), 2 * LANE:3 * LANE] * _lane_mask(ROPE, 2 * t + hh, BF)],
                                           axis=1)
                    nfull = (qi * tq) // tk

                    def step(kj, carry, masked):
                        m, l, acc = carry
                        c0 = kj * tk
                        kc = jnp.concatenate([kv_ref[pl.ds(c0, tk), cols], kp_ref[pl.ds(c0, tk), :]], axis=1)
                        vt = kv_ref[pl.ds(c0, tk), (2 + t) * LANE:(3 + t) * LANE]
                        s = lax.dot_general(qcat, kc, NT_DIMS, preferred_element_type=F32) * SCALE
                        if masked:
                            s = jnp.where(_visible(tq, tk, qi * tq - nfull * tk), s, -jnp.inf)
                        m_new = jnp.maximum(m, jnp.max(s, axis=-1, keepdims=True))
                        p = jnp.exp(s - m_new)
                        alpha = jnp.exp(m - m_new)
                        l = alpha * l + jnp.sum(p, axis=-1, keepdims=True)
                        acc = alpha * acc + jnp.dot(p.astype(BF), vt, preferred_element_type=F32)
                        return m_new, l, acc

                    carry = (jnp.full((tq, 1), -jnp.inf, F32), jnp.zeros((tq, 1), F32), jnp.zeros((tq, LANE), F32))
                    for kj in range(nfull):
                        carry = step(kj, carry, False)
                    m, l, acc = step(nfull, carry, True)
                    mine = _lane_mask(NOPE, hh, F32)
                    if hh == 0:
                        o_ref[pl.ds(r0, tq), cols] = (acc / l) * mine
                        lse_ref[pl.ds(r0, tq), cols] = (m + jnp.log(l)) * mine
                    else:
                        o_ref[pl.ds(r0, tq), cols] += (acc / l) * mine
                        lse_ref[pl.ds(r0, tq), cols] += (m + jnp.log(l)) * mine
                    return 0

                for qi in range(nq):
                    q_block(qi, 0)

    return pl.pallas_call(
        body, name="attn_fwd", grid=(QUADS,),
        in_specs=[pl.BlockSpec((S, QW), lambda g: (0, g)), pl.BlockSpec((S, KVW), lambda g: (0, g)),
                  pl.BlockSpec((S, LANE), lambda g: (0, 0))],
        out_specs=[pl.BlockSpec((S, 2 * LANE), lambda g: (0, g))] * 2,
        out_shape=[jax.ShapeDtypeStruct((S, HEADS * VH), F32)] * 2,
        compiler_params=_cparams(),
    )(q, kv, kpe)


def attn_bwd(q, kv, kpe, o, lse, do, S):
    tq, tk = _attn_tiles(S, *BWD_TILES)
    nq = S // tq

    def body(q_ref, kv_ref, kp_ref, o_ref, lse_ref, do_ref, dq_ref, dkv_ref, dkp_ref, dq_acc, dk_acc, dv_acc):
        for t in range(2):
            cols = slice(t * LANE, (t + 1) * LANE)
            dk_acc[...] = jnp.zeros(dk_acc.shape, F32)
            dv_acc[...] = jnp.zeros(dv_acc.shape, F32)
            for hh in range(2):
                def q_block(qi, _, t=t, hh=hh, cols=cols):
                    r0 = qi * tq
                    mine = _lane_mask(NOPE, hh, F32)
                    qcat = jnp.concatenate([q_ref[pl.ds(r0, tq), cols] * _lane_mask(NOPE, hh, BF),
                                            q_ref[pl.ds(r0, tq), 2 * LANE:3 * LANE] * _lane_mask(ROPE, 2 * t + hh, BF)],
                                           axis=1)
                    dof = do_ref[pl.ds(r0, tq), cols] * mine
                    dob = dof.astype(BF)
                    delta = jnp.sum(dof * o_ref[pl.ds(r0, tq), cols], axis=-1, keepdims=True)
                    lse_h = lse_ref[pl.ds(r0, tq), cols][:, hh * NOPE:hh * NOPE + 1]
                    nfull = (qi * tq) // tk
                    dq_acc[...] = jnp.zeros(dq_acc.shape, F32)

                    def step(kj, _, masked):
                        c0 = kj * tk
                        kc = jnp.concatenate([kv_ref[pl.ds(c0, tk), cols], kp_ref[pl.ds(c0, tk), :]], axis=1)
                        vt = kv_ref[pl.ds(c0, tk), (2 + t) * LANE:(3 + t) * LANE]
                        s = lax.dot_general(qcat, kc, NT_DIMS, preferred_element_type=F32) * SCALE
                        if masked:
                            s = jnp.where(_visible(tq, tk, qi * tq - nfull * tk), s, -jnp.inf)
                        p = jnp.exp(s - lse_h)
                        dp = lax.dot_general(dob, vt, NT_DIMS, preferred_element_type=F32)
                        ds = (p * (dp - delta) * SCALE).astype(BF)
                        dv_acc[pl.ds(c0, tk), :] += lax.dot_general(p.astype(BF), dob, TN_DIMS,
                                                                    preferred_element_type=F32)
                        dk_acc[pl.ds(c0, tk), :] += lax.dot_general(ds, qcat, TN_DIMS, preferred_element_type=F32)
                        dq_acc[...] += jnp.dot(ds, kc, preferred_element_type=F32)
                        return 0

                    for kj in range(nfull):
                        step(kj, 0, False)
                    step(nfull, 0, True)
                    d = dq_acc[...]
                    pe = d[:, LANE:] * _lane_mask(ROPE, 2 * t + hh, F32)
                    if hh == 0:
                        dq_ref[pl.ds(r0, tq), cols] = d[:, :LANE] * mine
                    else:
                        dq_ref[pl.ds(r0, tq), cols] += d[:, :LANE] * mine
                    if t == 0 and hh == 0:
                        dq_ref[pl.ds(r0, tq), 2 * LANE:3 * LANE] = pe
                    else:
                        dq_ref[pl.ds(r0, tq), 2 * LANE:3 * LANE] += pe
                    return 0

                for qi in range(nq):
                    q_block(qi, 0)
            dkv_ref[:, t * LANE:(t + 1) * LANE] = dk_acc[:, :LANE].astype(BF)
            dkv_ref[:, (2 + t) * LANE:(3 + t) * LANE] = dv_acc[...].astype(BF)
            if t == 0:
                dkp_ref[...] = dk_acc[:, LANE:]
            else:
                dkp_ref[...] += dk_acc[:, LANE:]

    qspec = pl.BlockSpec((S, QW), lambda g: (0, g))
    kvspec = pl.BlockSpec((S, KVW), lambda g: (0, g))
    ospec = pl.BlockSpec((S, 2 * LANE), lambda g: (0, g))
    return pl.pallas_call(
        body, name="attn_bwd", grid=(QUADS,),
        in_specs=[qspec, kvspec, pl.BlockSpec((S, LANE), lambda g: (0, 0)), ospec, ospec, ospec],
        out_specs=[qspec, kvspec, pl.BlockSpec((S, LANE), lambda g: (0, g))],
        out_shape=[jax.ShapeDtypeStruct((S, HEADS * (NOPE + ROPE)), F32), jax.ShapeDtypeStruct((S, HEADS * (NOPE + VH)), BF),
                   jax.ShapeDtypeStruct((S, HEADS * ROPE), F32)],
        scratch_shapes=[pltpu.VMEM((tq, 2 * LANE), F32), pltpu.VMEM((S, 2 * LANE), F32), pltpu.VMEM((S, LANE), F32)],
        compiler_params=_cparams(),
    )(q, kv, kpe, o, lse, do)


def exchange(name, gathers, a2as):
    n_g, n = len(gathers), len(gathers) + len(a2as)

    def body(*refs):
        ins, outs = refs[:n], refs[n:2 * n]
        send_sems, recv_sems, loc_sems = refs[2 * n:]
        x, y, c = lax.axis_index("x"), lax.axis_index("y"), lax.axis_index("c")
        me = 4 * x + 2 * y + c

        def peer(k):
            px = 1 - x if k & 4 else x
            py = 1 - y if k & 2 else y
            pc = 1 - c if k & 1 else c
            return (px, py, pc), 4 * px + 2 * py + pc

        def remote(a, k):
            pid, pflat = peer(k)
            src = ins[a] if a < n_g else ins[a].at[pflat]
            return pltpu.make_async_remote_copy(
                src_ref=src, dst_ref=outs[a].at[me], send_sem=send_sems.at[a, k - 1], recv_sem=recv_sems.at[a, k - 1],
                device_id=pid, device_id_type=MESH)

        def arrival(a, k):
            pid, pflat = peer(k)
            src = ins[a] if a < n_g else ins[a].at[pflat]
            return pltpu.make_async_remote_copy(
                src_ref=src, dst_ref=outs[a].at[pflat], send_sem=send_sems.at[a, k - 1], recv_sem=recv_sems.at[a, k - 1],
                device_id=pid, device_id_type=MESH)

        local = []
        for a in range(n):
            own = ins[a] if a < n_g else ins[a].at[me]
            cp = pltpu.make_async_copy(own, outs[a].at[me], loc_sems.at[a])
            cp.start()
            local.append(cp)
        sent = []
        for k in (1, 2, 4, 3, 5, 6, 7):
            for a in range(n):
                cp = remote(a, k)
                cp.start()
                sent.append(cp)
        for k in range(1, 8):
            for a in range(n):
                arrival(a, k).wait_recv()
        for cp in sent:
            cp.wait_send()
        for cp in local:
            cp.wait()

    out_shape = [jax.ShapeDtypeStruct((NDEV,) + g.shape, g.dtype) for g in gathers]
    out_shape += [jax.ShapeDtypeStruct(a.shape, a.dtype) for a in a2as]
    any_spec = pl.BlockSpec(memory_space=pl.ANY)
    return pl.pallas_call(
        body, name=name, in_specs=[any_spec] * n, out_specs=[any_spec] * n, out_shape=out_shape,
        scratch_shapes=[pltpu.SemaphoreType.DMA((n, NDEV - 1)), pltpu.SemaphoreType.DMA((n, NDEV - 1)),
                        pltpu.SemaphoreType.DMA((n,))],
    )(*gathers, *a2as)


def gather_two_level(name, block, dep):
    def body(x_ref, _, out_ref, stage, send_sems, recv_sems, loc_sem):
        x, y, c = lax.axis_index("x"), lax.axis_index("y"), lax.axis_index("c")
        me, sibling = (x, y, c), (x, y, 1 - c)
        chips = [(1 - x, y), (x, 1 - y), (1 - x, 1 - y)]

        def slot(px, py, pc):
            return out_ref.at[4 * px + 2 * py + pc]

        def copy(k, owner, to, src=None):
            return pltpu.make_async_remote_copy(
                src_ref=slot(*owner) if src is None else src, dst_ref=slot(*owner), send_sem=send_sems.at[k],
                recv_sem=recv_sems.at[k], device_id=to, device_id_type=MESH)

        load = pltpu.make_async_copy(x_ref, stage, loc_sem)
        load.start()
        first = [copy(0, me, sibling, src=x_ref)] + [copy(1 + j, me, (*chip, c), src=x_ref) for j, chip in enumerate(chips)]
        for cp in first:
            cp.start()
        load.wait()
        store = pltpu.make_async_copy(stage, slot(*me), loc_sem)
        store.start()
        passed = [copy(4 + j, (*chip, c), sibling) for j, chip in enumerate(chips)]
        for j, chip in enumerate(chips):
            copy(1 + j, (*chip, c), me).wait_recv()
            passed[j].start()
        copy(0, sibling, me).wait_recv()
        for j, chip in enumerate(chips):
            copy(4 + j, (*chip, 1 - c), me).wait_recv()
        for cp in first + passed:
            cp.wait_send()
        store.wait()

    return pl.pallas_call(
        body, name=name, in_specs=[pl.BlockSpec(memory_space=pl.ANY)] * 2, out_specs=pl.BlockSpec(memory_space=pl.ANY),
        out_shape=jax.ShapeDtypeStruct((NDEV,) + block.shape, block.dtype),
        scratch_shapes=[pltpu.VMEM(block.shape, block.dtype), pltpu.SemaphoreType.DMA((NDEV - 1,)),
                        pltpu.SemaphoreType.DMA((NDEV - 1,)), pltpu.SemaphoreType.DMA],
        compiler_params=_cparams(),
    )(block, dep)


def _peer(k, x, y, c):
    px = 1 - x if k & 4 else x
    py = 1 - y if k & 2 else y
    pc = 1 - c if k & 1 else c
    return (px, py, pc), 4 * px + 2 * py + pc


PEER_ORDER = (1, 2, 4, 3, 5, 6, 7)
HBM_SPEC = pl.BlockSpec(memory_space=pltpu.HBM)
SEM_SPEC = pl.BlockSpec(memory_space=pltpu.SEMAPHORE)
ANY_SPEC = pl.BlockSpec(memory_space=pl.ANY)


def _split_copies(ins, lands, n_g, send_sems, recv_sems):
    x, y, c = lax.axis_index("x"), lax.axis_index("y"), lax.axis_index("c")
    me = 4 * x + 2 * y + c

    def outgoing(a, k):
        pid, pflat = _peer(k, x, y, c)
        src = ins[a] if a < n_g else ins[a].at[pflat]
        return pltpu.make_async_remote_copy(
            src_ref=src, dst_ref=lands[a].at[me], send_sem=send_sems.at[a * (NDEV - 1) + k - 1],
            recv_sem=recv_sems.at[a * (NDEV - 1) + k - 1],
            device_id=pid, device_id_type=MESH)

    def arrival(a, k):
        pid, pflat = _peer(k, x, y, c)
        src = ins[a] if a < n_g else ins[a].at[pflat]
        return pltpu.make_async_remote_copy(
            src_ref=src, dst_ref=lands[a].at[pflat], send_sem=send_sems.at[a * (NDEV - 1) + k - 1],
            recv_sem=recv_sems.at[a * (NDEV - 1) + k - 1],
            device_id=pid, device_id_type=MESH)

    return outgoing, arrival


def exchange_begin(name, srcs, n_g, dep):
    n = len(srcs)
    land_shapes = [((NDEV,) + s.shape) if a < n_g else s.shape for a, s in enumerate(srcs)]

    def own_body(*refs):
        ins, outs = refs[:n], refs[n + 1:2 * n + 1]
        stage, sems = refs[2 * n + 1:3 * n + 1], refs[-1]
        me = 4 * lax.axis_index("x") + 2 * lax.axis_index("y") + lax.axis_index("c")
        cps = [pltpu.make_async_copy(ins[a] if a < n_g else ins[a].at[me], stage[a], sems.at[a]) for a in range(n)]
        for cp in cps:
            cp.start()
        for cp in cps:
            cp.wait()
        cps = [pltpu.make_async_copy(stage[a], outs[a].at[me], sems.at[a]) for a in range(n)]
        for cp in cps:
            cp.start()
        for cp in cps:
            cp.wait()

    lands = pl.pallas_call(
        own_body, name=name + "_own", in_specs=[ANY_SPEC] * (n + 1), out_specs=[ANY_SPEC] * n,
        out_shape=[jax.ShapeDtypeStruct(sh, s.dtype) for sh, s in zip(land_shapes, srcs)],
        scratch_shapes=[pltpu.VMEM(sh[1:], s.dtype) for sh, s in zip(land_shapes, srcs)] + [pltpu.SemaphoreType.DMA((n,))],
        compiler_params=_cparams(),
    )(*srcs, dep)

    def start_body(*refs):
        ins, lz = refs[:n], refs[n:2 * n]
        send_sems, recv_sems, token = refs[2 * n], refs[2 * n + 1], refs[-1]
        outgoing, _ = _split_copies(ins, lz, n_g, send_sems, recv_sems)
        for k in PEER_ORDER:
            for a in range(n):
                outgoing(a, k).start()
        token[...] = jnp.zeros(token.shape, F32)

    hbm = lambda t: pltpu.HBM(t.shape, t.dtype)
    res = pl.pallas_call(
        start_body, name=name + "_start",
        out_shape=(pltpu.SemaphoreType.DMA((n * (NDEV - 1),)), pltpu.SemaphoreType.DMA((n * (NDEV - 1),)),
                   *[hbm(s) for s in srcs], *[hbm(t) for t in lands], jax.ShapeDtypeStruct((8, LANE), F32)),
        in_specs=[HBM_SPEC] * (2 * n),
        out_specs=(SEM_SPEC, SEM_SPEC, *[HBM_SPEC] * (2 * n), pl.BlockSpec(memory_space=pltpu.VMEM)),
        input_output_aliases={i: 2 + i for i in range(2 * n)},
        compiler_params=pltpu.CompilerParams(has_side_effects=pltpu.SideEffectType.DATAFLOW_SIDE_EFFECTING),
    )(*[pltpu.with_memory_space_constraint(t, pltpu.HBM) for t in list(srcs) + list(lands)])
    return (name, n, n_g, res[:-1]), res[-1]


def exchange_end(handle, after):
    name, n, n_g, (send_sems, recv_sems, *bufs) = handle

    def wait_body(*refs):
        ins, lz = refs[:n], refs[n:2 * n]
        ss, rs = refs[2 * n], refs[2 * n + 1]
        outgoing, arrival = _split_copies(ins, lz, n_g, ss, rs)
        for k in range(1, NDEV):
            for a in range(n):
                arrival(a, k).wait_recv()
        for k in range(1, NDEV):
            for a in range(n):
                outgoing(a, k).wait_send()

    res = pl.pallas_call(
        wait_body, name=name + "_wait", out_shape=tuple(pltpu.HBM(t.shape, t.dtype) for t in bufs),
        in_specs=[HBM_SPEC] * (2 * n) + [SEM_SPEC, SEM_SPEC, ANY_SPEC], out_specs=[HBM_SPEC] * (2 * n),
        input_output_aliases={i: i for i in range(2 * n)},
        compiler_params=pltpu.CompilerParams(has_side_effects=pltpu.SideEffectType.DATAFLOW_SIDE_EFFECTING),
    )(*bufs, send_sems, recv_sems, after)
    return list(res[n:])


def _pick_rows(R, mult, cap):
    best = None
    for n in range(1, R + 1):
        if R % n == 0 and (R // n) % mult == 0 and R // n <= cap:
            best = R // n
            break
    assert best is not None, (R, mult, cap)
    return best


def sum_slots(name, x):
    _, R, _ = x.shape
    tr = _pick_rows(R, 16, 2304)

    def body(x_ref, o_ref):
        acc = x_ref[0].astype(F32)
        for d in range(1, NDEV):
            acc = acc + x_ref[d].astype(F32)
        o_ref[...] = acc

    return pl.pallas_call(
        body, name=name, grid=(R // tr,),
        in_specs=[pl.BlockSpec((NDEV, tr, LANE), lambda i: (0, i, 0))],
        out_specs=pl.BlockSpec((tr, LANE), lambda i: (i, 0)),
        out_shape=jax.ShapeDtypeStruct((R, LANE), F32), compiler_params=_cparams(),
    )(x)


def adamw(name, w, g, m, v):
    L, R, C = w.shape
    tr = _pick_rows(R, 8, 256) if R % 8 == 0 else R

    def body(w_ref, g_ref, m_ref, v_ref, d_ref, nm_ref, nv_ref):
        gg = g_ref[...]
        nm = B1 * m_ref[...] + (1.0 - B1) * gg
        nv = B2 * v_ref[...] + (1.0 - B2) * jnp.square(gg)
        m_hat = nm / (1.0 - B1 ** STEP)
        v_hat = nv / (1.0 - B2 ** STEP)
        d_ref[...] = -LR * (m_hat / (jnp.sqrt(v_hat) + EPS) + WD * w_ref[...])
        nm_ref[...] = nm
        nv_ref[...] = nv

    blk = pl.BlockSpec((1, tr, C), lambda l, i: (l, i, 0))
    shp = jax.ShapeDtypeStruct(w.shape, F32)
    return pl.pallas_call(
        body, name=name, grid=(L, R // tr), in_specs=[blk] * 4, out_specs=[blk] * 3, out_shape=[shp] * 3,
        compiler_params=_cparams(),
    )(w, g, m, v)


IN_SHARD = D_IN // NDEV
UQ_SHARD = HEADS * (NOPE + ROPE) // NDEV
W_IN_PAD = 1024
ROW_A, ROW_B, ROW_C, ROW_UKV, ROW_UQ, MISC_ROWS = 0, 512, 1024, 1536, 1792, 2176


def _in_perm_index():
    ar = np.arange
    z = lambda n: np.full((n,), -1, np.int64)
    mix = lambda lo1, lo2: np.concatenate([ar(lo + LANE * j, lo + LANE * (j + 1)) for j in range(CW // LANE)
                                           for lo in (lo1, lo2)])
    return np.concatenate([ar(4768, 7840), mix(0, 512), ar(1024, 1536), mix(1536, 2560), ar(4256, 4768), ar(2048, 2560),
                           ar(3072, 3584), ar(3968, 4224), ar(4224, 4256), z(OFF_Q - OFF_KR - ROPE), ar(3584, 3968),
                           z(NP - OFF_Q - QL)])


def _head_perm_index(a, b):
    parts = []
    for g in range(QUADS):
        h = np.arange(4 * g, 4 * g + 4)[:, None] * (a + b)
        parts += [(h + np.arange(a)[None]).reshape(-1), (h + a + np.arange(b)[None]).reshape(-1)]
    return np.concatenate(parts)


def _inverse(perm, n):
    inv = np.full((n,), -1, np.int64)
    inv[perm[perm >= 0]] = np.nonzero(perm >= 0)[0]
    return inv


IN_PERM = _in_perm_index()
UQ_PERM = _head_perm_index(NOPE, ROPE)
UKV_PERM = _head_perm_index(NOPE, VH)


def _to_gathered(perm, shard, pad):
    return np.where(perm >= 0, (perm // shard) * pad + perm % shard, -1)


def _from_full(inv, shard, pad):
    j, i = np.divmod(np.arange(NDEV * pad), pad)
    return np.where(i < shard, inv[np.minimum(j * shard + i, inv.shape[0] - 1)], -1)


def col_gather(name, srcs, out_shapes, jobs, deps=()):
    ns, nj, nd, no = len(srcs), len(jobs), len(deps), len(out_shapes)
    tables = [jnp.asarray(np.asarray(job[5], np.int32)[None, :]) for job in jobs]

    def view(ref, col0, width, r0, rc):
        n = ref.shape[-1]
        if len(ref.shape) == 3:
            return ref.at[col0 // n, pl.ds(r0, rc), pl.ds(col0 % n, width)]
        return ref.at[pl.ds(r0, rc), pl.ds(col0, width)]

    def slabs(shape):
        if len(shape) == 3:
            return [((d,), d * shape[2], (d + 1) * shape[2]) for d in range(shape[0])]
        w = 1024 if shape[1] > 1024 and shape[1] % 1024 == 0 else shape[1]
        return [((slice(None), pl.ds(c, w)), c, c + w) for c in range(0, shape[1], w)]

    src_slabs = [slabs(s.shape) for s in srcs]
    out_slabs = [slabs(sh) for sh in out_shapes]
    work, first_use, last_touch = [], {}, {}
    for ji, (si, srow, oi, orow, nrows, tgt) in enumerate(jobs):
        tgt = np.asarray(tgt)
        tw = 256 if out_shapes[oi][-1] % 256 == 0 else LANE
        sw = 256 if srcs[si].shape[-1] % 256 == 0 else LANE
        for t in range(tgt.shape[0] // tw):
            tt = tgt[t * tw:(t + 1) * tw]
            tiles = sorted(set((tt[tt >= 0] // sw).tolist()))
            straight = bool(tiles) and tt[0] >= 0 and tt[0] % LANE == 0 and np.array_equal(tt, tt[0] + np.arange(tw))
            cols = [(int(tt[0]) + k * LANE, LANE) for k in range(tw // LANE)] if straight else [(s * sw, sw) for s in tiles]
            need = sorted({(si, k) for c0, _ in cols for k, (_, lo, hi) in enumerate(src_slabs[si]) if lo <= c0 < hi})
            touch = [(oi, k) for k, (_, lo, hi) in enumerate(out_slabs[oi]) if lo <= t * tw < hi][0]
            for key in need:
                first_use.setdefault(key, len(work))
            last_touch[touch] = len(work)
            work.append((ji, t, tw, sw, tiles, straight, need, touch))
    in_order = sorted(first_use, key=first_use.get)
    in_sem = {key: i for i, key in enumerate(in_order)}
    out_keys = sorted(last_touch)
    out_sem = {key: i for i, key in enumerate(out_keys)}

    def body(*refs):
        src_hbm, tab_refs = refs[:ns], refs[ns:ns + nj]
        out_hbm = refs[ns + nj + nd:ns + nj + nd + no]
        scratch = refs[ns + nj + nd + no:]
        src_refs, out_refs, in_sems, out_sems = scratch[:ns], scratch[ns:ns + no], scratch[-2], scratch[-1]
        loads = {}
        for key in in_order:
            si, k = key
            idx = src_slabs[si][k][0]
            loads[key] = pltpu.make_async_copy(src_hbm[si].at[idx], src_refs[si].at[idx], in_sems.at[in_sem[key]])
            loads[key].start()
        arrived, stores = set(), []
        for wi, (ji, t, tw, sw, tiles, straight, need, touch) in enumerate(work):
            si, srow, oi, orow, nrows, tgt = jobs[ji]
            sref, oref = src_refs[si], out_refs[oi]
            rc = nrows if nrows <= 1024 else 1024
            for key in need:
                if key not in arrived:
                    loads[key].wait()
                    arrived.add(key)
            onehots = []
            if tiles and not straight:
                want = tab_refs[ji][:, t * tw:(t + 1) * tw]
                row = lax.broadcasted_iota(jnp.int32, (sw, tw), 0)
                onehots = [jnp.where(want == row + s * sw, 1.0, 0.0).astype(BF) for s in tiles]
            first = int(np.asarray(tgt)[t * tw])

            def chunk(ci, _, t=t, tw=tw, sw=sw, tiles=tiles, straight=straight, onehots=onehots, first=first,
                      sref=sref, oref=oref, srow=srow, orow=orow, rc=rc):
                r0 = ci * rc
                ro = pl.multiple_of(orow + r0, LANE)
                rs = pl.multiple_of(srow + r0, LANE)
                if not tiles:
                    view(oref, t * tw, tw, ro, rc)[...] = jnp.zeros((rc, tw), BF)
                elif straight:
                    for k in range(tw // LANE):
                        view(oref, t * tw + k * LANE, LANE, ro, rc)[...] = view(sref, first + k * LANE, LANE, rs, rc)[...]
                else:
                    acc = None
                    for s, oh in zip(tiles, onehots):
                        p = jnp.dot(view(sref, s * sw, sw, rs, rc)[...], oh, preferred_element_type=F32)
                        acc = p if acc is None else acc + p
                    view(oref, t * tw, tw, ro, rc)[...] = acc.astype(BF)
                return 0

            lax.fori_loop(0, nrows // rc, chunk, 0)
            if last_touch[touch] == wi:
                idx = out_slabs[touch[0]][touch[1]][0]
                cp = pltpu.make_async_copy(out_refs[touch[0]].at[idx], out_hbm[touch[0]].at[idx], out_sems.at[out_sem[touch]])
                cp.start()
                stores.append(cp)
        for cp in stores:
            cp.wait()

    return pl.pallas_call(
        body, name=name, in_specs=[ANY_SPEC] * ns + [pl.BlockSpec(memory_space=pltpu.VMEM)] * nj + [ANY_SPEC] * nd,
        out_specs=[ANY_SPEC] * no, out_shape=[jax.ShapeDtypeStruct(s, BF) for s in out_shapes],
        scratch_shapes=[pltpu.VMEM(s.shape, BF) for s in srcs] + [pltpu.VMEM(s, BF) for s in out_shapes]
        + [pltpu.SemaphoreType.DMA((len(in_order),)), pltpu.SemaphoreType.DMA((len(out_keys),))],
        compiler_params=_cparams(),
    )(*srcs, *tables, *deps)


def sum_adamw(name, recvs, w, m, v, lo=0, prev=None, row0=0):
    _, R, C = w.shape
    L = len(recvs)
    CP = recvs[0].shape[-1]
    tr = _pick_rows(R, 16, 128)
    n_prev = 0 if prev is None else 4

    def body(*refs):
        r_refs = refs[:L]
        w_ref, m_ref, v_ref = refs[L:L + 3]
        g_ref, d_ref, nm_ref, nv_ref, gsum = refs[L + 3 + n_prev:]
        layer = pl.program_id(0)
        for k in range(L):
            def total(k=k):
                acc = r_refs[k][0].astype(F32)
                for d in range(1, NDEV):
                    acc = acc + r_refs[k][d].astype(F32)
                gsum[...] = acc
            pl.when(layer == k)(total)
        gg = gsum[:, 0:C]
        nm = B1 * m_ref[...] + (1.0 - B1) * gg
        nv = B2 * v_ref[...] + (1.0 - B2) * jnp.square(gg)
        m_hat = nm / (1.0 - B1 ** STEP)
        v_hat = nv / (1.0 - B2 ** STEP)
        g_ref[...] = gg
        d_ref[...] = -LR * (m_hat / (jnp.sqrt(v_hat) + EPS) + WD * w_ref[...])
        nm_ref[...] = nm
        nv_ref[...] = nv

    assert row0 % tr == 0
    r_specs = [pl.BlockSpec((NDEV, tr, CP),
                            functools.partial(lambda l, i, k: (0, row0 // tr + jnp.where(l == k, i, 0), 0), k=k))
               for k in range(L)]
    blk = pl.BlockSpec((None, tr, C), lambda l, i: (l + lo, i, 0))
    shp = jax.ShapeDtypeStruct(w.shape, F32)
    return pl.pallas_call(
        body, name=name, grid=(L, R // tr), in_specs=r_specs + [blk] * 3 + [ANY_SPEC] * n_prev, out_specs=[blk] * 4,
        out_shape=[shp] * 4, input_output_aliases={L + 3 + i: i for i in range(n_prev)},
        scratch_shapes=[pltpu.VMEM((tr, CP), F32)], compiler_params=_cparams(),
    )(*recvs, w, m, v, *(prev or ()))


ALPHA = 8.0 ** 0.25
T_WIDE, T_NARROW = 512, 1024


def _rope_fn(sign):
    def fn(x, cos, sin):
        W = x.shape[-1]
        lane = lax.broadcasted_iota(jnp.int32, x.shape, 1)
        first_half = (lane % ROPE) < (ROPE // 2)
        rot = jnp.where(first_half, -pltpu.roll(x, W - ROPE // 2, 1), pltpu.roll(x, ROPE // 2, 1))
        return x * cos + sign * rot * sin
    return fn


def _modulate(xv, a):
    return xv * (1.0 + a[1:2, :]) + a[0:1, :]


def layer_fwd(x, ada3, W, tabs, S, u=None, ada_next=None):
    cos, sin = tabs
    T = T_NARROW
    if u is None:
        u = rowwise("modulate", _modulate, S, T, [(x, D_MODEL, 0)], [ada3], [(D_MODEL, BF)])[0]
    proj = mm(u, W["in"], name="mm_proj", tm=1024, tn=1024, out_dtype=BF)
    W = {**W, **W["late"](proj)}

    ca = conv_fwd("conv_a_fwd", proj, OFF_A, W["conv_a"], 31, "glu", S, CW)

    def a_post(c, ag, vec):
        n, _ = _ln_stats(c + vec[0:1, :])
        return _silu(n * vec[1:2, :] + vec[2:3, :]) * _silu(ag)

    h_a = rowwise("mix_a_post", a_post, S, T, [(ca, CW, 0), (proj, CW, OFF_AG)], [W["vec_a"]], [(CW, BF)])[0]
    y_a = mm(h_a, W["a_out"], name="mm_branch_out", out_dtype=BF)

    cb = conv_fwd("conv_b_fwd", proj, OFF_B, W["conv_b"], 3, "mul", S, CW)
    h_b = rowwise("mix_b_post", lambda c, gb, bg: gb * c * _silu(bg), S, T,
                  [(cb, CW, 0), (proj, CW, OFF_GB), (proj, CW, OFF_BG)], [], [(CW, BF)])[0]
    y_b = mm(h_b, W["b_out"], name="mm_branch_out", out_dtype=BF)

    def rms2(ql, kvl, gq, gkv):
        rq = lax.rsqrt(jnp.mean(ql * ql, axis=-1, keepdims=True) + RMS_EPS)
        rk = lax.rsqrt(jnp.mean(kvl * kvl, axis=-1, keepdims=True) + RMS_EPS)
        return ql * rq * gq, kvl * rk * gkv

    qn, kvn = rowwise("rms_fwd", rms2, S, T, [(proj, QL, OFF_Q), (proj, KVL, OFF_KV)], [W["gq"], W["gkv"]],
                      [(QL, BF), (KVL, BF)])
    q = mm(qn, W["uq"], name="mm_q")
    kv = mm(kvn, W["ukv"], name="mm_kv", out_dtype=BF)
    rope = _rope_fn(1.0)

    def rope_fwd(qv, kr, c1, s1):
        parts = []
        for g in range(QUADS):
            parts.append(qv[:, g * QW:g * QW + 2 * LANE].astype(BF))
            parts.append(rope(qv[:, g * QW + 2 * LANE:(g + 1) * QW], c1, s1).astype(BF))
        kp = rope(kr, c1, s1)
        kp = kp + pltpu.roll(kp, ROPE, 1) + pltpu.roll(kp, 2 * ROPE, 1) + pltpu.roll(kp, 3 * ROPE, 1)
        return jnp.concatenate(parts, axis=1), kp

    q_b, kpe = rowwise("rope_fwd", rope_fwd, S, T,
                       [(q, HEADS * (NOPE + ROPE), 0), (proj, LANE, OFF_KR), (cos, LANE, 0), (sin, LANE, 0)], [],
                       [(HEADS * (NOPE + ROPE), BF), (LANE, BF)])
    o, lse = attn_fwd(q_b, kv, kpe, S)
    h_c = rowwise("mix_c_post", lambda ov, cg: ov * _silu(cg), S, T, [(o, CW, 0), (proj, CW, OFF_CG)], [],
                  [(CW, BF)])[0]
    y_c = mm(h_c, W["c_out"], name="mm_branch_out", out_dtype=BF)

    def merge(la, lb, lc, ya, yb, yc):
        return _sigmoid(la) * ya + _sigmoid(lb) * yb + _sigmoid(lc) * yc

    m = rowwise("merge_fwd", merge, S, T_WIDE,
                [(proj, D_MODEL, 0), (proj, D_MODEL, 1024), (proj, D_MODEL, 2048), (y_a, D_MODEL, 0),
                 (y_b, D_MODEL, 0), (y_c, D_MODEL, 0)], [], [(D_MODEL, BF)])[0]
    out = mm(m, W["o"], name="mm_out")

    def ln_fwd(xv, ov, a, lnv, *nxt):
        n, _ = _ln_stats(ALPHA * xv + a[2:3, :] * ov)
        y = n * lnv[0:1, :] + lnv[1:2, :]
        return (y, _modulate(y, nxt[0])) if nxt else y

    res = rowwise("ln_fwd", ln_fwd, S, T_WIDE, [(x, D_MODEL, 0), (out, D_MODEL, 0)],
                  [ada3, W["lnv"]] + ([] if ada_next is None else [ada_next]),
                  [(D_MODEL, F32)] + ([] if ada_next is None else [(D_MODEL, BF)]))
    saved = dict(x=x, u=u, proj=proj, ca=ca, cb=cb, h_a=h_a, h_b=h_b, h_c=h_c, y_a=y_a, y_b=y_b, y_c=y_c, qn=qn,
                 kvn=kvn, q_b=q_b, kv=kv, kpe=kpe, lse=lse, o=o, m=m, out=out)
    return res[0], saved, W, (res[1] if ada_next is not None else None)


def layer_bwd(dxn, sv, ada3, W, tabs, S, before_in=None):
    cos, sin = tabs
    T = T_NARROW
    x, proj = sv["x"], sv["proj"]
    G = {}

    def ln_bwd(xv, ov, dy, a, lnv):
        gate = a[2:3, :]
        n, rstd = _ln_stats(ALPHA * xv + gate * ov)
        dr = _ln_bwd(dy * lnv[0:1, :], n, rstd)
        return ALPHA * dr, gate * dr, _colsum(dy * n), _colsum(dy), _colsum(dr * ov)

    dres, d_out, G["ln_g"], G["ln_b"], d_gate = rowwise(
        "ln_bwd", ln_bwd, S, T_WIDE, [(x, D_MODEL, 0), (sv["out"], D_MODEL, 0), (dxn, D_MODEL, 0)], [ada3, W["lnv"]],
        [(D_MODEL, F32), (D_MODEL, BF)], [D_MODEL] * 3)
    dm = mm(d_out, W["o"], name="mm_dm", trans_b=True, out_dtype=BF)
    G["w_o"] = mm(sv["m"], d_out, name="mm_gw_o", trans_a=True, out_dtype=BF)

    def merge_bwd(dmv, la, lb, lc, ya, yb, yc):
        outs, dls = [], []
        for lg, yv in ((la, ya), (lb, yb), (lc, yc)):
            s = _sigmoid(lg)
            outs.append(dmv * s)
            dls.append((dmv * yv * s * (1.0 - s)).astype(BF))
        return (jnp.concatenate(dls, axis=1),) + tuple(outs)

    d_proj, dy_a, dy_b, dy_c = rowwise(
        "merge_bwd", merge_bwd, S, T_WIDE,
        [(dm, D_MODEL, 0), (proj, D_MODEL, 0), (proj, D_MODEL, 1024), (proj, D_MODEL, 2048), (sv["y_a"], D_MODEL, 0),
         (sv["y_b"], D_MODEL, 0), (sv["y_c"], D_MODEL, 0)], [], [(3 * D_MODEL, BF)] + [(D_MODEL, BF)] * 3,
        into=(None, NP, OFF_M))

    dh = {}
    for br, dy in (("a", dy_a), ("b", dy_b), ("c", dy_c)):
        dh[br] = mm(dy, W[br + "_out"], name="mm_dh", trans_b=True, out_dtype=BF)
        G["w_%s_out" % br] = mm(sv["h_" + br], dy, name="mm_gw_branch", trans_a=True, out_dtype=BF)

    def a_post_bwd(c, ag, dhv, vec):
        n, rstd = _ln_stats(c + vec[0:1, :])
        z = n * vec[1:2, :] + vec[2:3, :]
        d_ag = dhv * _silu(z) * _dsilu(ag)
        dz = dhv * _silu(ag) * _dsilu(z)
        dc = _ln_bwd(dz * vec[1:2, :], n, rstd)
        return d_ag, dc, _colsum(dc), _colsum(dz * n), _colsum(dz)

    d_proj, dca, G["conv_a_b"], G["ln_a_g"], G["ln_a_b"] = rowwise(
        "mix_a_post_bwd", a_post_bwd, S, T, [(sv["ca"], CW, 0), (proj, CW, OFF_AG), (dh["a"], CW, 0)], [W["vec_a"]],
        [(CW, BF), (CW, F32)], [CW] * 3, into=(d_proj, NP, OFF_AG))
    d_proj, G["conv_a_w"] = conv_bwd("conv_a_bwd", proj, OFF_A, dca, W["conv_a"], 31, "glu", S, CW, d_proj)

    def b_post_bwd(c, gb, bg, dhv):
        sg = _silu(bg)
        d_gb_bg = jnp.concatenate([(dhv * sg * c).astype(BF), (dhv * gb * c * _dsilu(bg)).astype(BF)], axis=1)
        return d_gb_bg, dhv * sg * gb

    d_proj, dcb = rowwise("mix_b_post_bwd", b_post_bwd, S, T,
                          [(sv["cb"], CW, 0), (proj, CW, OFF_GB), (proj, CW, OFF_BG), (dh["b"], CW, 0)], [],
                          [(2 * CW, BF), (CW, F32)], into=(d_proj, NP, OFF_GB))
    d_proj, G["conv_b_w"] = conv_bwd("conv_b_bwd", proj, OFF_B, dcb, W["conv_b"], 3, "mul", S, CW, d_proj)

    d_proj, d_o = rowwise("mix_c_post_bwd", lambda ov, cg, dhv: (dhv * ov * _dsilu(cg), dhv * _silu(cg)), S, T,
                          [(sv["o"], CW, 0), (proj, CW, OFF_CG), (dh["c"], CW, 0)], [], [(CW, BF), (CW, F32)],
                          into=(d_proj, NP, OFF_CG))
    dq, d_kv, dkp_heads = attn_bwd(sv["q_b"], sv["kv"], sv["kpe"], sv["o"], sv["lse"], d_o, S)
    ropeT = _rope_fn(-1.0)

    def rope_bwd(dqv, dkp, c1, s1):
        parts = []
        for g in range(QUADS):
            parts.append(dqv[:, g * QW:g * QW + 2 * LANE].astype(BF))
            parts.append(ropeT(dqv[:, g * QW + 2 * LANE:(g + 1) * QW], c1, s1).astype(BF))
        f = dkp[:, :LANE] + dkp[:, LANE:]
        f = f + pltpu.roll(f, 64, 1)
        f = f + pltpu.roll(f, 32, 1)
        lane = lax.broadcasted_iota(jnp.int32, f.shape, 1)
        return jnp.concatenate(parts, axis=1), jnp.where(lane < ROPE, ropeT(f, c1, s1), 0.0)

    d_q, dk_pe = rowwise("rope_bwd", rope_bwd, S, T,
                         [(dq, HEADS * (NOPE + ROPE), 0), (dkp_heads, HEADS * ROPE, 0), (cos, LANE, 0), (sin, LANE, 0)],
                         [], [(HEADS * (NOPE + ROPE), BF), (LANE, BF)])
    d_qn = mm(d_q, W["uq"], name="mm_dqn", trans_b=True, out_dtype=BF)
    d_kvn = mm(d_kv, W["ukv"], name="mm_dkvn", trans_b=True, out_dtype=BF)
    G["w_uq"] = mm(sv["qn"], d_q, name="mm_gw_uq", trans_a=True, out_dtype=BF)
    G["w_ukv"] = mm(sv["kvn"], d_kv, name="mm_gw_ukv", trans_a=True, out_dtype=BF)

    def rms_bwd(ql, kvl, dqn, dkn, dkp, gq, gkv):
        res = []
        for xv, dy, g in ((ql, dqn, gq), (kvl, dkn, gkv)):
            r = lax.rsqrt(jnp.mean(xv * xv, axis=-1, keepdims=True) + RMS_EPS)
            dxh = dy * g
            res.append(((r * (dxh - xv * (r * r) * jnp.mean(dxh * xv, axis=-1, keepdims=True))).astype(BF),
                        _colsum(dy * xv * r)))
        pad = jnp.zeros((ql.shape[0], LANE), BF)
        return jnp.concatenate([res[1][0], dkp, pad, res[0][0], pad], axis=1), res[0][1], res[1][1]

    d_proj, G["q_norm_g"], G["kv_norm_g"] = rowwise(
        "rms_bwd", rms_bwd, S, T,
        [(proj, QL, OFF_Q), (proj, KVL, OFF_KV), (d_qn, QL, 0), (d_kvn, KVL, 0), (dk_pe, LANE, 0)],
        [W["gq"], W["gkv"]], [(NP - OFF_KV, BF)], [QL, KVL], into=(d_proj, NP, OFF_KV))
    deps = before_in(G) if before_in is not None else ()
    du = mm(d_proj, W["in"], name="mm_du", trans_b=True, tm=1024, tk=2048, deps=deps)
    G["w_in"] = mm(sv["u"], d_proj, name="mm_gw_in", trans_a=True, out_dtype=BF, tm=1024, tk=2048, deps=deps)

    def mod_bwd(duv, xv, dr, a):
        return duv * (1.0 + a[1:2, :]) + dr, _colsum(duv), _colsum(duv * xv)

    dx, d_shift, d_scale = rowwise("mod_bwd", mod_bwd, S, T_WIDE, [(du, D_MODEL, 0), (x, D_MODEL, 0), (dres, D_MODEL, 0)],
                                   [ada3], [(D_MODEL, F32)], [D_MODEL] * 2)
    d_ada = jnp.concatenate([d_shift, d_scale, d_gate], axis=1)
    return dx, G, d_ada


SMALL = ("conv_a_b", "ln_a_g", "ln_a_b", "q_norm_g", "kv_norm_g", "ln_g", "ln_b")


def _rows(v):
    n = v.shape[0]
    r = -(-n // (LANE * 16)) * 16
    return jnp.pad(v, (0, r * LANE - n)).reshape(r, LANE)


def kernel(x, c, positions, w_ada, b_ada, w_in, conv_a_w, conv_a_b, ln_a_g, ln_a_b, w_a_out, conv_b_w, w_b_out, q_norm_g, kv_norm_g, w_uq, w_ukv, w_c_out, w_o, ln_g, ln_b, loss_target, m_w_ada, m_b_ada, m_w_in, m_conv_a_w, m_conv_a_b, m_ln_a_g, m_ln_a_b, m_w_a_out, m_conv_b_w, m_w_b_out, m_q_norm_g, m_kv_norm_g, m_w_uq, m_w_ukv, m_w_c_out, m_w_o, m_ln_g, m_ln_b, v_w_ada, v_b_ada, v_w_in, v_conv_a_w, v_conv_a_b, v_ln_a_g, v_ln_a_b, v_w_a_out, v_conv_b_w, v_w_b_out, v_q_norm_g, v_kv_norm_g, v_w_uq, v_w_ukv, v_w_c_out, v_w_o, v_ln_g, v_ln_b):
    P = dict(w_ada=w_ada, b_ada=b_ada, w_in=w_in, conv_a_w=conv_a_w, conv_a_b=conv_a_b, ln_a_g=ln_a_g, ln_a_b=ln_a_b,
             w_a_out=w_a_out, conv_b_w=conv_b_w, w_b_out=w_b_out, q_norm_g=q_norm_g, kv_norm_g=kv_norm_g, w_uq=w_uq,
             w_ukv=w_ukv, w_c_out=w_c_out, w_o=w_o, ln_g=ln_g, ln_b=ln_b)
    Mo = dict(w_ada=m_w_ada, b_ada=m_b_ada, w_in=m_w_in, conv_a_w=m_conv_a_w, conv_a_b=m_conv_a_b, ln_a_g=m_ln_a_g,
              ln_a_b=m_ln_a_b, w_a_out=m_w_a_out, conv_b_w=m_conv_b_w, w_b_out=m_w_b_out, q_norm_g=m_q_norm_g,
              kv_norm_g=m_kv_norm_g, w_uq=m_w_uq, w_ukv=m_w_ukv, w_c_out=m_w_c_out, w_o=m_w_o, ln_g=m_ln_g, ln_b=m_ln_b)
    Vo = dict(w_ada=v_w_ada, b_ada=v_b_ada, w_in=v_w_in, conv_a_w=v_conv_a_w, conv_a_b=v_conv_a_b, ln_a_g=v_ln_a_g,
              ln_a_b=v_ln_a_b, w_a_out=v_w_a_out, conv_b_w=v_conv_b_w, w_b_out=v_w_b_out, q_norm_g=v_q_norm_g,
              kv_norm_g=v_kv_norm_g, w_uq=v_w_uq, w_ukv=v_w_ukv, w_c_out=v_w_c_out, w_o=v_w_o, ln_g=v_ln_g, ln_b=v_ln_b)
    ORDER = ("w_ada", "b_ada", "w_in", "conv_a_w", "conv_a_b", "ln_a_g", "ln_a_b", "w_a_out", "conv_b_w", "w_b_out",
             "q_norm_g", "kv_norm_g", "w_uq", "w_ukv", "w_c_out", "w_o", "ln_g", "ln_b")
    L = w_ada.shape[0]
    S = x.shape[1]
    me = 4 * lax.axis_index("x") + 2 * lax.axis_index("y") + lax.axis_index("c")
    x2 = x[0]
    tgt = loss_target[0]

    small_in = _rows(jnp.concatenate([c.reshape(-1), conv_a_w.reshape(-1), conv_b_w.reshape(-1)]))
    w_in_b = jnp.pad(w_in.astype(BF), ((0, 0), (0, 0), (0, W_IN_PAD - IN_SHARD)))
    misc_b = jnp.concatenate([w_a_out, w_b_out, w_c_out, w_ukv, jnp.pad(w_uq, ((0, 0), (0, 0), (0, LANE - UQ_SHARD)))],
                             axis=1).astype(BF)
    w_o_b = w_o.astype(BF)
    gathered = [None] * L
    sg = exchange("gather_small", [small_in], [])[0]
    sgf = sg.reshape(NDEV, -1)
    c_all = sgf[:, :D_MODEL]
    o1 = D_MODEL + L * 31 * 64
    conv_a_full = sgf[:, D_MODEL:o1].reshape(NDEV, L, 31, 64).transpose(1, 2, 0, 3).reshape(L, 31, CW)
    conv_b_full = sgf[:, o1:o1 + L * 3 * 64].reshape(NDEV, L, 3, 64).transpose(1, 2, 0, 3).reshape(L, 3, CW)

    c_act = rowwise("silu_c", _silu, 16, 16, [(jnp.pad(c_all, ((0, 8), (0, 0))), D_MODEL, 0)], [], [(D_MODEL, BF)])[0]
    ncol = w_ada.shape[2]
    w_ada_b = w_ada.astype(BF).transpose(1, 0, 2).reshape(D_MODEL, L * ncol)
    b_mine = lax.dynamic_slice_in_dim(b_ada, me * ncol, ncol, axis=1).reshape(1, L * ncol)
    ada_part = mm(c_act, w_ada_b, name="mm_ada", bias=b_mine)
    ada_rows = -(-(L * ncol) // (LANE * 8)) * 8
    ada_send = jnp.pad(ada_part[:NDEV].reshape(NDEV, -1, LANE), ((0, 0), (0, ada_rows - L * ncol // LANE), (0, 0)))
    ada_recv = exchange("a2a_ada", [], [ada_send])[0]
    ada = ada_recv[:, :L * ncol // LANE].reshape(NDEV, L, ncol).transpose(1, 0, 2).reshape(L, 3, D_MODEL)
    gathered[0] = [gather_two_level("gather0_w_in", w_in_b[0], ada)]
    rests = {}
    rests[0], rest_token = exchange_begin("gather0_rest", [misc_b[0], w_o_b[0]], 2, gathered[0][0])

    inv_freq = ROPE_THETA ** (-jnp.arange(0, ROPE, 2, dtype=F32) / ROPE)
    ang = positions[0].astype(F32)[:, None] * inv_freq
    tabs = (jnp.tile(jnp.cos(ang), (1, 2 * LANE // ROPE)), jnp.tile(jnp.sin(ang), (1, 2 * LANE // ROPE)))

    straight = np.arange(D_MODEL)
    fwd_in = [(0, 0, 0, 0, D_MODEL, _to_gathered(IN_PERM, IN_SHARD, W_IN_PAD))]
    fwd_misc = [(0, ROW_A, 0, 0, CW, straight), (0, ROW_B, 1, 0, CW, straight), (0, ROW_C, 2, 0, CW, straight),
                (0, ROW_UKV, 3, 0, KVL, UKV_PERM), (0, ROW_UQ, 4, 0, QL, _to_gathered(UQ_PERM, UQ_SHARD, LANE))]
    rev_in = [(0, 0, 0, 0, D_MODEL, _from_full(_inverse(IN_PERM, D_IN), IN_SHARD, W_IN_PAD))]
    rev_misc = [(0, 0, 0, ROW_A, CW, straight), (1, 0, 0, ROW_B, CW, straight), (2, 0, 0, ROW_C, CW, straight),
                (3, 0, 0, ROW_UKV, KVL, _from_full(_inverse(UKV_PERM, HEADS * (NOPE + VH)), LANE, LANE)),
                (4, 0, 0, ROW_UQ, QL, _from_full(_inverse(UQ_PERM, HEADS * (NOPE + ROPE)), UQ_SHARD, LANE))]

    def layer_weights(l, deps):
        w_in_p = col_gather("relayout_w_in", [gathered[l][0]], [(D_MODEL, NP)], fwd_in, deps)[0]

        def late(after):
            if len(gathered[l]) == 1:
                gathered[l] += exchange_end(rests[l], after)
            _, g_misc, g_o = gathered[l]
            a_out, b_out, c_out, ukv, uq = col_gather(
                "relayout_misc", [g_misc],
                [(CW, D_MODEL)] * 3 + [(KVL, HEADS * (NOPE + VH)), (QL, HEADS * (NOPE + ROPE))], fwd_misc, deps)
            return {"a_out": a_out, "b_out": b_out, "c_out": c_out, "uq": uq, "ukv": ukv,
                    "o": g_o.reshape(D_MODEL, D_MODEL)}

        return {
            "in": w_in_p, "late": late,
            "conv_a": jnp.pad(conv_a_full[l], ((0, 1), (0, 0))), "conv_b": jnp.pad(conv_b_full[l], ((0, 5), (0, 0))),
            "vec_a": jnp.stack([conv_a_b[l], ln_a_g[l], ln_a_b[l]]), "gq": q_norm_g[l][None], "gkv": kv_norm_g[l][None],
            "lnv": jnp.stack([ln_g[l], ln_b[l]]),
        }

    h = x2
    saved, weights = [], []
    handles, token, u_next = {}, rest_token, None
    for l in range(1, L):
        handles[l], token = exchange_begin("gather%d" % l, [w_in_b[l]], 1, token)
        rests[l], token = exchange_begin("gather%d_rest" % l, [misc_b[l], w_o_b[l]], 2, token)
    for l in range(L):
        ada_l, deps = (ada[l] + token[0, 0], (token,)) if l == 0 else (ada[l], ())
        h, sv, Wl, u_next = layer_fwd(h, ada_l, layer_weights(l, deps), tabs, S, u_next, ada[l + 1] if l + 1 < L else None)
        if l + 1 < L:
            gathered[l + 1] = exchange_end(handles[l + 1], h)
        saved.append(sv)
        weights.append(Wl)

    def loss_fn(y, t):
        e = y - t
        return e * (1.0 / D_MODEL), _colsum(e * e)

    dy, sq = rowwise("loss", loss_fn, S, 256, [(h, D_MODEL, 0), (tgt, D_MODEL, 0)], [], [(D_MODEL, F32)], [D_MODEL])
    loss = lax.psum(0.5 * jnp.sum(sq) / D_MODEL, ("x", "y", "c"))
    loss, dy = lax.optimization_barrier((loss, dy))

    grads, d_adas, recv = [None] * L, [None] * L, [None] * L
    pending, token = None, None

    def send_rest(g):
        send_misc = col_gather("unrelayout_misc", [g["w_a_out"], g["w_b_out"], g["w_c_out"], g["w_ukv"], g["w_uq"]],
                               [(NDEV, MISC_ROWS, LANE)], rev_misc)[0]
        return [send_misc, g["w_o"].reshape(NDEV, D_MODEL // NDEV, D_MODEL)]

    rest0 = []

    def early_rest(g):
        handle, tok = exchange_begin("scatter0_rest", send_rest(g), 0, g["w_o"])
        rest0.append(handle)
        return (tok,)

    for l in reversed(range(L)):
        ada_l = ada[l] if token is None else ada[l] + token[0, 0]
        dy, g, d_adas[l] = layer_bwd(dy, saved[l], ada_l, weights[l], tabs, S, early_rest if l == 0 else None)
        grads[l] = g
        if pending is not None:
            recv[l + 1] = exchange_end(pending, dy)
        send_in = col_gather("unrelayout_w_in", [g["w_in"]], [(NDEV, D_MODEL, W_IN_PAD)], rev_in)[0]
        if l == 0:
            def layer_vec(i):
                return jnp.concatenate([grads[i][n].reshape(-1) for n in SMALL] + [d_adas[i].reshape(-1)])

            def to_owners(name, taps):
                full = jnp.stack([grads[i][name][:taps] for i in range(L)])
                return full.reshape(L, taps, NDEV, CW // NDEV).transpose(2, 0, 1, 3).reshape(NDEV, -1)

            conv_send = jnp.concatenate([to_owners("conv_a_w", 31), to_owners("conv_b_w", 3)], axis=1)
            conv_rows = -(-conv_send.shape[1] // (LANE * 16)) * 16
            conv_send = jnp.pad(conv_send, ((0, 0), (0, conv_rows * LANE - conv_send.shape[1])))
            small_sizes = [int(grads[0][n].size) for n in SMALL] + [3 * D_MODEL]
            gsmall, conv_recv = exchange("gather_small_grads", [_rows(jnp.concatenate([layer_vec(i) for i in range(L)]))],
                                         [conv_send.reshape(NDEV, conv_rows, LANE)])
            pending, token = exchange_begin("scatter0", [send_in], 0, gsmall)
        else:
            pending, token = exchange_begin("scatter%d" % l, [send_in] + send_rest(g), 0,
                                            dy if l + 1 == L else recv[l + 1][0])
    grad_x = dy[None]

    gsmall = gsmall + token[0, 0]
    gsum = sum_slots("sum_small", gsmall).reshape(-1)
    recv[0] = [None] + exchange_end(rest0[0], gsum)
    Gr = {}
    offs = np.cumsum([0] + small_sizes)
    per_layer = int(offs[-1])
    gsum = gsum[:L * per_layer].reshape(L, per_layer)
    for i, n in enumerate(SMALL):
        Gr[n] = gsum[:, offs[i]:offs[i + 1]]
    csum = sum_slots("sum_conv", conv_recv + token[0, 0]).reshape(-1)
    n_a = L * 31 * (CW // NDEV)
    Gr["conv_a_w"] = csum[:n_a].reshape(L, 31, CW // NDEV)
    Gr["conv_b_w"] = csum[n_a:n_a + L * 3 * (CW // NDEV)].reshape(L, 3, CW // NDEV)
    Gr["b_ada"] = gsum[:, offs[7]:offs[8]]
    d_ada_all = gsmall.reshape(NDEV, -1)[:, :L * per_layer].reshape(NDEV, L, per_layer)[:, :, offs[7]:offs[8]]
    d_mine = lax.dynamic_slice_in_dim(d_ada_all, me * ncol, ncol, axis=2).reshape(NDEV, L * ncol)
    g_ada = mm(c_act, jnp.pad(d_mine, ((0, 8), (0, 0))).astype(BF), name="mm_gw_ada", trans_a=True)
    Gr["w_ada"] = g_ada.reshape(D_MODEL, L, ncol).transpose(1, 0, 2)

    D, NM, NV = {}, {}, {}
    D["w_ada"], NM["w_ada"], NV["w_ada"] = adamw("adamw_w_ada", P["w_ada"], Gr["w_ada"], Mo["w_ada"], Vo["w_ada"])
    Gr["w_o"], D["w_o"], NM["w_o"], NV["w_o"] = sum_adamw(
        "sum_adamw_w_o", [recv[l][2] for l in range(L)], P["w_o"], Mo["w_o"], Vo["w_o"])
    for n, row0 in (("w_a_out", ROW_A), ("w_b_out", ROW_B), ("w_c_out", ROW_C), ("w_ukv", ROW_UKV), ("w_uq", ROW_UQ)):
        Gr[n], D[n], NM[n], NV[n] = sum_adamw("sum_adamw_" + n, [recv[l][1] for l in range(L)], P[n], Mo[n], Vo[n],
                                              row0=row0)
    w_l, m_l, v_l, _ = lax.optimization_barrier((P["w_in"], Mo["w_in"], Vo["w_in"], token))
    upper = sum_adamw("sum_adamw_w_in_upper", [recv[l][0] for l in range(1, L)], w_l, m_l, v_l, lo=1)
    recv[0][0] = exchange_end(pending, upper[1])[0]
    Gr["w_in"], D["w_in"], NM["w_in"], NV["w_in"] = sum_adamw(
        "sum_adamw_w_in", [recv[0][0]], w_l, m_l, v_l, lo=0, prev=upper)
    for n in ("b_ada", "conv_a_w", "conv_b_w") + SMALL:
        shape = P[n].shape if P[n].ndim == 3 else (1,) + P[n].shape
        res = adamw("adamw_" + n, *[t.reshape(shape) for t in (P[n], Gr[n], Mo[n], Vo[n])])
        D[n], NM[n], NV[n] = [t.reshape(P[n].shape) for t in res]
    return (loss, grad_x, *[Gr[n] for n in ORDER], *[D[n] for n in ORDER], *[NM[n] for n in ORDER],
            *[NV[n] for n in ORDER])
```

```python
import functools
import math

import numpy as np
import jax
import jax.numpy as jnp
from jax import lax
from jax.experimental import pallas as pl
from jax.experimental.pallas import tpu as pltpu

BF = jnp.bfloat16
F32 = jnp.float32
MESH = pl.DeviceIdType.MESH
NDEV = 8

HEADS, NOPE, ROPE, VH = 8, 64, 32, 64
HP = 128
ROPE_THETA = 10000.0
LN_EPS = 1e-5
RMS_EPS = 1e-6
LR, B1, B2, EPS, WD, STEP = 0.001, 0.9, 0.999, 1e-08, 0.01, 10

LANE = 128
VMEM_LIMIT = 56 * 1024 * 1024

D_MODEL, CW, QL, KVL = 1024, 512, 384, 256
OFF_M, OFF_A, OFF_AG, OFF_B, OFF_CG, OFF_GB, OFF_BG = 0, 3072, 4096, 4608, 5632, 6144, 6656
OFF_KV, OFF_KR, OFF_Q, NP = 7168, 7424, 7680, 8192
D_IN = 7840


def _cparams(**kw):
    return pltpu.CompilerParams(vmem_limit_bytes=VMEM_LIMIT, **kw)


def _sigmoid(x):
    return jax.nn.sigmoid(x)


def _silu(x):
    return x * _sigmoid(x)


def _dsilu(x):
    s = _sigmoid(x)
    return s * (1.0 + x * (1.0 - s))


def _pick_tile(n, cap, mult):
    if n <= cap:
        return n
    for t in range(cap - cap % mult, 0, -mult):
        if n % t == 0:
            return t
    raise ValueError((n, cap, mult))


def mm(a, b, *, name, trans_a=False, trans_b=False, out_dtype=F32, bias=None, tm=1024, tn=1024, tk=2048, deps=()):
    if trans_a:
        K, M = a.shape
    else:
        M, K = a.shape
    if trans_b:
        N, K2 = b.shape
    else:
        K2, N = b.shape
    assert K == K2 and not (trans_a and trans_b), (a.shape, b.shape)
    tm, tn = _pick_tile(M, tm, 16), _pick_tile(N, tn, LANE)
    tk = _pick_tile(K, tk, LANE if trans_b else 16)
    assert M % tm == 0 and N % tn == 0 and K % tk == 0, (M, N, K, tm, tn, tk)
    nk = K // tk
    dims = (((0 if trans_a else 1,), (1 if trans_b else 0,)), ((), ()))
    has_bias = bias is not None

    def body(*refs):
        a_ref, b_ref = refs[0], refs[1]
        bias_ref = refs[2] if has_bias else None
        o_ref = refs[(3 if has_bias else 2) + len(deps)]
        p = lax.dot_general(a_ref[...], b_ref[...], dims, preferred_element_type=F32)

        def finish(v):
            if has_bias:
                v = v + bias_ref[...]
            o_ref[...] = v.astype(o_ref.dtype)

        if nk == 1:
            finish(p)
        else:
            acc = refs[-1]
            k = pl.program_id(2)

            @pl.when(k == 0)
            def _():
                acc[...] = p

            @pl.when(k > 0)
            def _():
                acc[...] += p

            @pl.when(k == nk - 1)
            def _():
                finish(acc[...])

    if trans_a:
        a_spec = pl.BlockSpec((tk, tm), lambda i, j, k: (k, i))
    else:
        a_spec = pl.BlockSpec((tm, tk), lambda i, j, k: (i, k))
    if trans_b:
        b_spec = pl.BlockSpec((tn, tk), lambda i, j, k: (j, k))
    else:
        b_spec = pl.BlockSpec((tk, tn), lambda i, j, k: (k, j))
    in_specs = [a_spec, b_spec]
    args = [a, b]
    if has_bias:
        in_specs.append(pl.BlockSpec((1, tn), lambda i, j, k: (0, j)))
        args.append(bias)
    in_specs += [ANY_SPEC] * len(deps)
    args += list(deps)
    return pl.pallas_call(
        body, name=name, grid=(M // tm, N // tn, nk),
        in_specs=in_specs, out_specs=pl.BlockSpec((tm, tn), lambda i, j, k: (i, j)),
        out_shape=jax.ShapeDtypeStruct((M, N), out_dtype),
        scratch_shapes=[pltpu.VMEM((tm, tn), F32)] if nk > 1 else [],
        compiler_params=_cparams(),
    )(*args)


def mm_bwd_pair(name, dy, w, h, tm=1024):
    M, N = dy.shape
    K = w.shape[0]
    tm = min(tm, M)
    steps = M // tm

    def body(dy_ref, w_ref, h_ref, dh_ref, gw_ref, acc):
        dyv = dy_ref[...]
        dh_ref[...] = lax.dot_general(dyv, w_ref[...], (((1,), (1,)), ((), ())),
                                      preferred_element_type=F32).astype(dh_ref.dtype)
        part = lax.dot_general(h_ref[...], dyv, (((0,), (0,)), ((), ())), preferred_element_type=F32)
        i = pl.program_id(0)

        @pl.when(i == 0)
        def _():
            acc[...] = part

        @pl.when(i > 0)
        def _():
            acc[...] += part

        @pl.when(i == steps - 1)
        def _():
            gw_ref[...] = acc[...].astype(gw_ref.dtype)

    return pl.pallas_call(
        body, name=name, grid=(steps,),
        in_specs=[pl.BlockSpec((tm, N), lambda i: (i, 0)), pl.BlockSpec((K, N), lambda i: (0, 0)),
                  pl.BlockSpec((tm, K), lambda i: (i, 0))],
        out_specs=[pl.BlockSpec((tm, K), lambda i: (i, 0)), pl.BlockSpec((K, N), lambda i: (0, 0))],
        out_shape=[jax.ShapeDtypeStruct((M, K), BF), jax.ShapeDtypeStruct((K, N), BF)],
        scratch_shapes=[pltpu.VMEM((K, N), F32)], compiler_params=_cparams(),
    )(dy, w, h)


def rowwise(name, fn, S, T, row_ins, full_ins, row_outs, acc_outs=(), into=None):
    n_in = len(row_ins) + len(full_ins)
    n_ro, n_ao = len(row_outs), len(acc_outs)
    alias = into is not None and into[0] is not None
    T = min(T, S)

    def body(*refs):
        vals = [r[...] for r in refs[:n_in]]
        vals = [v.astype(F32) if v.dtype == BF else v for v in vals]
        outs = fn(*vals)
        if not isinstance(outs, (tuple, list)):
            outs = (outs,)
        assert len(outs) == n_ro + n_ao, (name, len(outs))
        o0 = n_in + (1 if alias else 0)
        for r, v in zip(refs[o0:o0 + n_ro], outs[:n_ro]):
            r[...] = v.astype(r.dtype)
        first = pl.program_id(0) == 0
        for r, v in zip(refs[o0 + n_ro:], outs[n_ro:]):
            def init(r=r, v=v):
                r[...] = v

            def accum(r=r, v=v):
                r[...] += v

            pl.when(first)(init)
            pl.when(jnp.logical_not(first))(accum)

    in_specs, args = [], []
    for arr, W, off in row_ins:
        assert off % W == 0 and arr.shape[0] == S, (name, arr.shape, W, off)
        in_specs.append(pl.BlockSpec((T, W), functools.partial(lambda i, cb: (i, cb), cb=off // W)))
        args.append(arr)
    for arr in full_ins:
        in_specs.append(pl.BlockSpec(arr.shape, lambda i: (0, 0)))
        args.append(arr)
    out_specs = [pl.BlockSpec((T, W), lambda i: (i, 0)) for W, _ in row_outs]
    out_shape = [jax.ShapeDtypeStruct((S, W), dt) for W, dt in row_outs]
    aliases = {}
    if into is not None:
        buf, total, off = into
        W0, dt0 = row_outs[0]
        assert off % W0 == 0
        out_specs[0] = pl.BlockSpec((T, W0), functools.partial(lambda i, cb: (i, cb), cb=off // W0))
        out_shape[0] = jax.ShapeDtypeStruct((S, total), dt0)
        if alias:
            in_specs.append(ANY_SPEC)
            args.append(buf)
            aliases = {n_in: 0}
    out_specs += [pl.BlockSpec((1, W), lambda i: (0, 0)) for W in acc_outs]
    out_shape += [jax.ShapeDtypeStruct((1, W), F32) for W in acc_outs]
    return pl.pallas_call(
        body, name=name, grid=(S // T,), in_specs=in_specs, out_specs=out_specs, out_shape=out_shape,
        input_output_aliases=aliases, compiler_params=_cparams(),
    )(*args)


def _colsum(v):
    return jnp.sum(v, axis=0, keepdims=True)


def _ln_stats(r):
    mu = jnp.mean(r, axis=-1, keepdims=True)
    d = r - mu
    var = jnp.mean(d * d, axis=-1, keepdims=True)
    rstd = lax.rsqrt(var + LN_EPS)
    return d * rstd, rstd


def _ln_bwd(dn, n, rstd):
    return rstd * (dn - jnp.mean(dn, axis=-1, keepdims=True) - n * jnp.mean(dn * n, axis=-1, keepdims=True))


CPAD = 32
TC = 64


def _pre(mode, x1, x2):
    return x1 * _sigmoid(x2) if mode == "glu" else x1 * x2


def _sublane_shifts(ext):
    n = TC + CPAD
    return [ext] + [pltpu.roll(ext, n - r, 0) for r in range(1, 8)]


def _shifted(shifts, sft):
    q, r = divmod(sft, 8)
    return shifts[r][8 * q:8 * q + TC]


def _interleaved_specs(S, off):
    return [pl.BlockSpec((S, LANE), functools.partial(lambda j, o: (0, o + 2 * j), o=off // LANE)),
            pl.BlockSpec((S, LANE), functools.partial(lambda j, o: (0, o + 2 * j + 1), o=off // LANE))]


def conv_fwd(name, src, off, w_pad, taps, mode, S, C):
    nchunk = S // TC

    def body(x1_ref, x2_ref, w_ref, o_ref, a_pad):
        a_pad[0:CPAD, :] = jnp.zeros((CPAD, LANE), F32)

        def fill(i, _):
            r = pl.multiple_of(i * 256, 256)
            a_pad[pl.ds(CPAD + r, 256), :] = _pre(mode, x1_ref[pl.ds(r, 256), :].astype(F32),
                                                  x2_ref[pl.ds(r, 256), :].astype(F32))
            return 0

        lax.fori_loop(0, S // 256, fill, 0)

        def chunk(i, _):
            base = pl.multiple_of(i * TC, TC)
            shifts = _sublane_shifts(a_pad[pl.ds(base, TC + CPAD), :])
            acc = jnp.zeros((TC, LANE), F32)
            for k in range(taps):
                acc = acc + w_ref[pl.ds(k, 1), :] * _shifted(shifts, CPAD - (taps - 1) + k)
            o_ref[pl.ds(base, TC), :] = acc
            return 0

        lax.fori_loop(0, nchunk, chunk, 0)

    kp = w_pad.shape[0]
    return pl.pallas_call(
        body, name=name, grid=(C // LANE,),
        in_specs=_interleaved_specs(S, off) + [pl.BlockSpec((kp, LANE), lambda j: (0, j))],
        out_specs=pl.BlockSpec((S, LANE), lambda j: (0, j)),
        out_shape=jax.ShapeDtypeStruct((S, C), F32),
        scratch_shapes=[pltpu.VMEM((S + CPAD, LANE), F32)],
        compiler_params=_cparams(),
    )(src, src, w_pad)


def conv_bwd(name, src, off, dc, w_pad, taps, mode, S, C, buf):
    nchunk = S // TC
    kp = w_pad.shape[0]

    def body(x1_ref, x2_ref, dc_ref, w_ref, _, d_ref, dw_ref, a_pad, dc_pad, dw_acc):
        a_pad[0:CPAD, :] = jnp.zeros((CPAD, LANE), F32)
        dc_pad[S:S + CPAD, :] = jnp.zeros((CPAD, LANE), F32)
        dw_acc[...] = jnp.zeros(dw_acc.shape, F32)

        def fill(i, _):
            r = pl.multiple_of(i * 256, 256)
            a_pad[pl.ds(CPAD + r, 256), :] = _pre(mode, x1_ref[pl.ds(r, 256), :].astype(F32),
                                                  x2_ref[pl.ds(r, 256), :].astype(F32))
            dc_pad[pl.ds(r, 256), :] = dc_ref[pl.ds(r, 256), :]
            return 0

        lax.fori_loop(0, S // 256, fill, 0)

        def chunk(i, _):
            base = pl.multiple_of(i * TC, TC)
            shifts_d = _sublane_shifts(dc_pad[pl.ds(base, TC + CPAD), :])
            shifts_a = _sublane_shifts(a_pad[pl.ds(base, TC + CPAD), :])
            dcv = shifts_d[0][0:TC]
            da = jnp.zeros((TC, LANE), F32)
            for k in range(taps):
                da = da + w_ref[pl.ds(k, 1), :] * _shifted(shifts_d, taps - 1 - k)
                prod = dcv * _shifted(shifts_a, CPAD - (taps - 1) + k)
                fold = prod[0:8]
                for g in range(1, TC // 8):
                    fold = fold + prod[8 * g:8 * g + 8]
                dw_acc[pl.ds(8 * k, 8), :] += fold
            x1 = x1_ref[pl.ds(base, TC), :].astype(F32)
            x2 = x2_ref[pl.ds(base, TC), :].astype(F32)
            if mode == "glu":
                s = _sigmoid(x2)
                d1, d2 = da * s, da * x1 * s * (1.0 - s)
            else:
                d1, d2 = da * x2, da * x1
            d_ref[pl.ds(base, TC), 0:LANE] = d1.astype(BF)
            d_ref[pl.ds(base, TC), LANE:2 * LANE] = d2.astype(BF)
            return 0

        lax.fori_loop(0, nchunk, chunk, 0)
        dw_ref[...] = jnp.zeros(dw_ref.shape, F32)
        for k in range(taps):
            dw_ref[pl.ds(k, 1), :] = jnp.sum(dw_acc[pl.ds(8 * k, 8), :], axis=0, keepdims=True)

    blk = pl.BlockSpec((S, LANE), lambda j: (0, j))
    return pl.pallas_call(
        body, name=name, grid=(C // LANE,),
        in_specs=_interleaved_specs(S, off) + [blk, pl.BlockSpec((kp, LANE), lambda j: (0, j)), ANY_SPEC],
        out_specs=[pl.BlockSpec((S, 2 * LANE), functools.partial(lambda j, o: (0, o + j), o=off // (2 * LANE))),
                   pl.BlockSpec((kp, LANE), lambda j: (0, j))],
        out_shape=[jax.ShapeDtypeStruct(buf.shape, BF), jax.ShapeDtypeStruct((kp, C), F32)],
        input_output_aliases={4: 0},
        scratch_shapes=[pltpu.VMEM((S + CPAD, LANE), F32), pltpu.VMEM((S + CPAD, LANE), F32),
                        pltpu.VMEM((8 * kp, LANE), F32)],
        compiler_params=_cparams(),
    )(src, src, dc, w_pad, buf)


FWD_TILES = (512, 512)
BWD_TILES = (512, 512)
QUADS = HEADS // 4
QW, KVW = 4 * (NOPE + ROPE), 4 * (NOPE + VH)
SCALE = (NOPE + ROPE) ** -0.5
NT_DIMS = (((1,), (1,)), ((), ()))
TN_DIMS = (((0,), (0,)), ((), ()))


def _lane_mask(width, group, dtype):
    lane = lax.broadcasted_iota(jnp.int32, (1, LANE), 1)
    return jnp.where(lane // width == group, 1.0, 0.0).astype(dtype)


def _visible(tq, tk, off):
    row = lax.broadcasted_iota(jnp.int32, (tq, tk), 0)
    col = lax.broadcasted_iota(jnp.int32, (tq, tk), 1)
    return col <= row + off


def _attn_tiles(S, tq, tk):
    tk = tk if S % tk == 0 else 256
    return min(tq, tk), tk


def attn_fwd(q, kv, kpe, S):
    tq, tk = _attn_tiles(S, *FWD_TILES)
    nq = S // tq

    def body(q_ref, kv_ref, kp_ref, o_ref, lse_ref):
        for t in range(2):
            cols = slice(t * LANE, (t + 1) * LANE)
            for hh in range(2):
                def q_block(qi, _, t=t, hh=hh, cols=cols):
                    r0 = qi * tq
                    qcat = jnp.concatenate([q_ref[pl.ds(r0, tq), cols] * _lane_mask(NOPE, hh, BF),
                                            q_ref[pl.ds(r0, tq), 2 * LANE:3 * LANE] * _lane_mask(ROPE, 2 * t + hh, BF)],
                                           axis=1)
                    nfull = (qi * tq) // tk

                    def step(kj, carry, masked):
                        m, l, acc = carry
                        c0 = kj * tk
                        kc = jnp.concatenate([kv_ref[pl.ds(c0, tk), cols], kp_ref[pl.ds(c0, tk), :]], axis=1)
                        vt = kv_ref[pl.ds(c0, tk), (2 + t) * LANE:(3 + t) * LANE]
                        s = lax.dot_general(qcat, kc, NT_DIMS, preferred_element_type=F32) * SCALE
                        if masked:
                            s = jnp.where(_visible(tq, tk, qi * tq - nfull * tk), s, -jnp.inf)
                        m_new = jnp.maximum(m, jnp.max(s, axis=-1, keepdims=True))
                        p = jnp.exp(s - m_new)
                        alpha = jnp.exp(m - m_new)
                        l = alpha * l + jnp.sum(p, axis=-1, keepdims=True)
                        acc = alpha * acc + jnp.dot(p.astype(BF), vt, preferred_element_type=F32)
                        return m_new, l, acc

                    carry = (jnp.full((tq, 1), -jnp.inf, F32), jnp.zeros((tq, 1), F32), jnp.zeros((tq, LANE), F32))
                    for kj in range(nfull):
                        carry = step(kj, carry, False)
                    m, l, acc = step(nfull, carry, True)
                    mine = _lane_mask(NOPE, hh, F32)
                    if hh == 0:
                        o_ref[pl.ds(r0, tq), cols] = (acc / l) * mine
                        lse_ref[pl.ds(r0, tq), cols] = (m + jnp.log(l)) * mine
                    else:
                        o_ref[pl.ds(r0, tq), cols] += (acc / l) * mine
                        lse_ref[pl.ds(r0, tq), cols] += (m + jnp.log(l)) * mine
                    return 0

                for qi in range(nq):
                    q_block(qi, 0)

    return pl.pallas_call(
        body, name="attn_fwd", grid=(QUADS,),
        in_specs=[pl.BlockSpec((S, QW), lambda g: (0, g)), pl.BlockSpec((S, KVW), lambda g: (0, g)),
                  pl.BlockSpec((S, LANE), lambda g: (0, 0))],
        out_specs=[pl.BlockSpec((S, 2 * LANE), lambda g: (0, g))] * 2,
        out_shape=[jax.ShapeDtypeStruct((S, HEADS * VH), F32)] * 2,
        compiler_params=_cparams(),
    )(q, kv, kpe)


def attn_bwd(q, kv, kpe, o, lse, do, S):
    tq, tk = _attn_tiles(S, *BWD_TILES)
    nq = S // tq

    def body(q_ref, kv_ref, kp_ref, o_ref, lse_ref, do_ref, dq_ref, dkv_ref, dkp_ref, dq_acc, dk_acc, dv_acc):
        for t in range(2):
            cols = slice(t * LANE, (t + 1) * LANE)
            dk_acc[...] = jnp.zeros(dk_acc.shape, F32)
            dv_acc[...] = jnp.zeros(dv_acc.shape, F32)
            for hh in range(2):
                def q_block(qi, _, t=t, hh=hh, cols=cols):
                    r0 = qi * tq
                    mine = _lane_mask(NOPE, hh, F32)
                    qcat = jnp.concatenate([q_ref[pl.ds(r0, tq), cols] * _lane_mask(NOPE, hh, BF),
                                            q_ref[pl.ds(r0, tq), 2 * LANE:3 * LANE] * _lane_mask(ROPE, 2 * t + hh, BF)],
                                           axis=1)
                    dof = do_ref[pl.ds(r0, tq), cols] * mine
                    dob = dof.astype(BF)
                    delta = jnp.sum(dof * o_ref[pl.ds(r0, tq), cols], axis=-1, keepdims=True)
                    lse_h = lse_ref[pl.ds(r0, tq), cols][:, hh * NOPE:hh * NOPE + 1]
                    nfull = (qi * tq) // tk
                    dq_acc[...] = jnp.zeros(dq_acc.shape, F32)

                    def step(kj, _, masked):
                        c0 = kj * tk
                        kc = jnp.concatenate([kv_ref[pl.ds(c0, tk), cols], kp_ref[pl.ds(c0, tk), :]], axis=1)
                        vt = kv_ref[pl.ds(c0, tk), (2 + t) * LANE:(3 + t) * LANE]
                        s = lax.dot_general(qcat, kc, NT_DIMS, preferred_element_type=F32) * SCALE
                        if masked:
                            s = jnp.where(_visible(tq, tk, qi * tq - nfull * tk), s, -jnp.inf)
                        p = jnp.exp(s - lse_h)
                        dp = lax.dot_general(dob, vt, NT_DIMS, preferred_element_type=F32)
                        ds = (p * (dp - delta) * SCALE).astype(BF)
                        dv_acc[pl.ds(c0, tk), :] += lax.dot_general(p.astype(BF), dob, TN_DIMS,
                                                                    preferred_element_type=F32)
                        dk_acc[pl.ds(c0, tk), :] += lax.dot_general(ds, qcat, TN_DIMS, preferred_element_type=F32)
                        dq_acc[...] += jnp.dot(ds, kc, preferred_element_type=F32)
                        return 0

                    for kj in range(nfull):
                        step(kj, 0, False)
                    step(nfull, 0, True)
                    d = dq_acc[...]
                    pe = d[:, LANE:] * _lane_mask(ROPE, 2 * t + hh, F32)
                    if hh == 0:
                        dq_ref[pl.ds(r0, tq), cols] = d[:, :LANE] * mine
                    else:
                        dq_ref[pl.ds(r0, tq), cols] += d[:, :LANE] * mine
                    if t == 0 and hh == 0:
                        dq_ref[pl.ds(r0, tq), 2 * LANE:3 * LANE] = pe
                    else:
                        dq_ref[pl.ds(r0, tq), 2 * LANE:3 * LANE] += pe
                    return 0

                for qi in range(nq):
                    q_block(qi, 0)
            dkv_ref[:, t * LANE:(t + 1) * LANE] = dk_acc[:, :LANE].astype(BF)
            dkv_ref[:, (2 + t) * LANE:(3 + t) * LANE] = dv_acc[...].astype(BF)
            if t == 0:
                dkp_ref[...] = dk_acc[:, LANE:]
            else:
                dkp_ref[...] += dk_acc[:, LANE:]

    qspec = pl.BlockSpec((S, QW), lambda g: (0, g))
    kvspec = pl.BlockSpec((S, KVW), lambda g: (0, g))
    ospec = pl.BlockSpec((S, 2 * LANE), lambda g: (0, g))
    return pl.pallas_call(
        body, name="attn_bwd", grid=(QUADS,),
        in_specs=[qspec, kvspec, pl.BlockSpec((S, LANE), lambda g: (0, 0)), ospec, ospec, ospec],
        out_specs=[qspec, kvspec, pl.BlockSpec((S, LANE), lambda g: (0, g))],
        out_shape=[jax.ShapeDtypeStruct((S, HEADS * (NOPE + ROPE)), F32), jax.ShapeDtypeStruct((S, HEADS * (NOPE + VH)), BF),
                   jax.ShapeDtypeStruct((S, HEADS * ROPE), F32)],
        scratch_shapes=[pltpu.VMEM((tq, 2 * LANE), F32), pltpu.VMEM((S, 2 * LANE), F32), pltpu.VMEM((S, LANE), F32)],
        compiler_params=_cparams(),
    )(q, kv, kpe, o, lse, do)


def exchange(name, gathers, a2as):
    n_g, n = len(gathers), len(gathers) + len(a2as)

    def body(*refs):
        ins, outs = refs[:n], refs[n:2 * n]
        send_sems, recv_sems, loc_sems = refs[2 * n:]
        x, y, c = lax.axis_index("x"), lax.axis_index("y"), lax.axis_index("c")
        me = 4 * x + 2 * y + c

        def peer(k):
            px = 1 - x if k & 4 else x
            py = 1 - y if k & 2 else y
            pc = 1 - c if k & 1 else c
            return (px, py, pc), 4 * px + 2 * py + pc

        def remote(a, k):
            pid, pflat = peer(k)
            src = ins[a] if a < n_g else ins[a].at[pflat]
            return pltpu.make_async_remote_copy(
                src_ref=src, dst_ref=outs[a].at[me], send_sem=send_sems.at[a, k - 1], recv_sem=recv_sems.at[a, k - 1],
                device_id=pid, device_id_type=MESH)

        def arrival(a, k):
            pid, pflat = peer(k)
            src = ins[a] if a < n_g else ins[a].at[pflat]
            return pltpu.make_async_remote_copy(
                src_ref=src, dst_ref=outs[a].at[pflat], send_sem=send_sems.at[a, k - 1], recv_sem=recv_sems.at[a, k - 1],
                device_id=pid, device_id_type=MESH)

        local = []
        for a in range(n):
            own = ins[a] if a < n_g else ins[a].at[me]
            cp = pltpu.make_async_copy(own, outs[a].at[me], loc_sems.at[a])
            cp.start()
            local.append(cp)
        sent = []
        for k in (1, 2, 4, 3, 5, 6, 7):
            for a in range(n):
                cp = remote(a, k)
                cp.start()
                sent.append(cp)
        for k in range(1, 8):
            for a in range(n):
                arrival(a, k).wait_recv()
        for cp in sent:
            cp.wait_send()
        for cp in local:
            cp.wait()

    out_shape = [jax.ShapeDtypeStruct((NDEV,) + g.shape, g.dtype) for g in gathers]
    out_shape += [jax.ShapeDtypeStruct(a.shape, a.dtype) for a in a2as]
    any_spec = pl.BlockSpec(memory_space=pl.ANY)
    return pl.pallas_call(
        body, name=name, in_specs=[any_spec] * n, out_specs=[any_spec] * n, out_shape=out_shape,
        scratch_shapes=[pltpu.SemaphoreType.DMA((n, NDEV - 1)), pltpu.SemaphoreType.DMA((n, NDEV - 1)),
                        pltpu.SemaphoreType.DMA((n,))],
    )(*gathers, *a2as)


def gather_two_level(name, block, dep):
    def body(x_ref, _, out_ref, stage, send_sems, recv_sems, loc_sem):
        x, y, c = lax.axis_index("x"), lax.axis_index("y"), lax.axis_index("c")
        me, sibling = (x, y, c), (x, y, 1 - c)
        chips = [(1 - x, y), (x, 1 - y), (1 - x, 1 - y)]

        def slot(px, py, pc):
            return out_ref.at[4 * px + 2 * py + pc]

        def copy(k, owner, to, src=None):
            return pltpu.make_async_remote_copy(
                src_ref=slot(*owner) if src is None else src, dst_ref=slot(*owner), send_sem=send_sems.at[k],
                recv_sem=recv_sems.at[k], device_id=to, device_id_type=MESH)

        load = pltpu.make_async_copy(x_ref, stage, loc_sem)
        load.start()
        first = [copy(0, me, sibling, src=x_ref)] + [copy(1 + j, me, (*chip, c), src=x_ref) for j, chip in enumerate(chips)]
        for cp in first:
            cp.start()
        load.wait()
        store = pltpu.make_async_copy(stage, slot(*me), loc_sem)
        store.start()
        passed = [copy(4 + j, (*chip, c), sibling) for j, chip in enumerate(chips)]
        for j, chip in enumerate(chips):
            copy(1 + j, (*chip, c), me).wait_recv()
            passed[j].start()
        copy(0, sibling, me).wait_recv()
        for j, chip in enumerate(chips):
            copy(4 + j, (*chip, 1 - c), me).wait_recv()
        for cp in first + passed:
            cp.wait_send()
        store.wait()

    return pl.pallas_call(
        body, name=name, in_specs=[pl.BlockSpec(memory_space=pl.ANY)] * 2, out_specs=pl.BlockSpec(memory_space=pl.ANY),
        out_shape=jax.ShapeDtypeStruct((NDEV,) + block.shape, block.dtype),
        scratch_shapes=[pltpu.VMEM(block.shape, block.dtype), pltpu.SemaphoreType.DMA((NDEV - 1,)),
                        pltpu.SemaphoreType.DMA((NDEV - 1,)), pltpu.SemaphoreType.DMA],
        compiler_params=_cparams(),
    )(block, dep)


def _peer(k, x, y, c):
    px = 1 - x if k & 4 else x
    py = 1 - y if k & 2 else y
    pc = 1 - c if k & 1 else c
    return (px, py, pc), 4 * px + 2 * py + pc


PEER_ORDER = (1, 2, 4, 3, 5, 6, 7)
HBM_SPEC = pl.BlockSpec(memory_space=pltpu.HBM)
SEM_SPEC = pl.BlockSpec(memory_space=pltpu.SEMAPHORE)
ANY_SPEC = pl.BlockSpec(memory_space=pl.ANY)


def _split_copies(ins, lands, n_g, send_sems, recv_sems):
    x, y, c = lax.axis_index("x"), lax.axis_index("y"), lax.axis_index("c")
    me = 4 * x + 2 * y + c

    def outgoing(a, k):
        pid, pflat = _peer(k, x, y, c)
        src = ins[a] if a < n_g else ins[a].at[pflat]
        return pltpu.make_async_remote_copy(
            src_ref=src, dst_ref=lands[a].at[me], send_sem=send_sems.at[a * (NDEV - 1) + k - 1],
            recv_sem=recv_sems.at[a * (NDEV - 1) + k - 1],
            device_id=pid, device_id_type=MESH)

    def arrival(a, k):
        pid, pflat = _peer(k, x, y, c)
        src = ins[a] if a < n_g else ins[a].at[pflat]
        return pltpu.make_async_remote_copy(
            src_ref=src, dst_ref=lands[a].at[pflat], send_sem=send_sems.at[a * (NDEV - 1) + k - 1],
            recv_sem=recv_sems.at[a * (NDEV - 1) + k - 1],
            device_id=pid, device_id_type=MESH)

    return outgoing, arrival


def exchange_begin(name, srcs, n_g, dep):
    n = len(srcs)
    land_shapes = [((NDEV,) + s.shape) if a < n_g else s.shape for a, s in enumerate(srcs)]

    def own_body(*refs):
        ins, outs = refs[:n], refs[n + 1:2 * n + 1]
        stage, sems = refs[2 * n + 1:3 * n + 1], refs[-1]
        me = 4 * lax.axis_index("x") + 2 * lax.axis_index("y") + lax.axis_index("c")
        cps = [pltpu.make_async_copy(ins[a] if a < n_g else ins[a].at[me], stage[a], sems.at[a]) for a in range(n)]
        for cp in cps:
            cp.start()
        for cp in cps:
            cp.wait()
        cps = [pltpu.make_async_copy(stage[a], outs[a].at[me], sems.at[a]) for a in range(n)]
        for cp in cps:
            cp.start()
        for cp in cps:
            cp.wait()

    lands = pl.pallas_call(
        own_body, name=name + "_own", in_specs=[ANY_SPEC] * (n + 1), out_specs=[ANY_SPEC] * n,
        out_shape=[jax.ShapeDtypeStruct(sh, s.dtype) for sh, s in zip(land_shapes, srcs)],
        scratch_shapes=[pltpu.VMEM(sh[1:], s.dtype) for sh, s in zip(land_shapes, srcs)] + [pltpu.SemaphoreType.DMA((n,))],
        compiler_params=_cparams(),
    )(*srcs, dep)

    def start_body(*refs):
        ins, lz = refs[:n], refs[n:2 * n]
        send_sems, recv_sems, token = refs[2 * n], refs[2 * n + 1], refs[-1]
        outgoing, _ = _split_copies(ins, lz, n_g, send_sems, recv_sems)
        for k in PEER_ORDER:
            for a in range(n):
                outgoing(a, k).start()
        token[...] = jnp.zeros(token.shape, F32)

    hbm = lambda t: pltpu.HBM(t.shape, t.dtype)
    res = pl.pallas_call(
        start_body, name=name + "_start",
        out_shape=(pltpu.SemaphoreType.DMA((n * (NDEV - 1),)), pltpu.SemaphoreType.DMA((n * (NDEV - 1),)),
                   *[hbm(s) for s in srcs], *[hbm(t) for t in lands], jax.ShapeDtypeStruct((8, LANE), F32)),
        in_specs=[HBM_SPEC] * (2 * n),
        out_specs=(SEM_SPEC, SEM_SPEC, *[HBM_SPEC] * (2 * n), pl.BlockSpec(memory_space=pltpu.VMEM)),
        input_output_aliases={i: 2 + i for i in range(2 * n)},
        compiler_params=pltpu.CompilerParams(has_side_effects=pltpu.SideEffectType.DATAFLOW_SIDE_EFFECTING),
    )(*[pltpu.with_memory_space_constraint(t, pltpu.HBM) for t in list(srcs) + list(lands)])
    return (name, n, n_g, res[:-1]), res[-1]


def exchange_end(handle, after):
    name, n, n_g, (send_sems, recv_sems, *bufs) = handle

    def wait_body(*refs):
        ins, lz = refs[:n], refs[n:2 * n]
        ss, rs = refs[2 * n], refs[2 * n + 1]
        outgoing, arrival = _split_copies(ins, lz, n_g, ss, rs)
        for k in range(1, NDEV):
            for a in range(n):
                arrival(a, k).wait_recv()
        for k in range(1, NDEV):
            for a in range(n):
                outgoing(a, k).wait_send()

    res = pl.pallas_call(
        wait_body, name=name + "_wait", out_shape=tuple(pltpu.HBM(t.shape, t.dtype) for t in bufs),
        in_specs=[HBM_SPEC] * (2 * n) + [SEM_SPEC, SEM_SPEC, ANY_SPEC], out_specs=[HBM_SPEC] * (2 * n),
        input_output_aliases={i: i for i in range(2 * n)},
        compiler_params=pltpu.CompilerParams(has_side_effects=pltpu.SideEffectType.DATAFLOW_SIDE_EFFECTING),
    )(*bufs, send_sems, recv_sems, after)
    return list(res[n:])


def _pick_rows(R, mult, cap):
    best = None
    for n in range(1, R + 1):
        if R % n == 0 and (R // n) % mult == 0 and R // n <= cap:
            best = R // n
            break
    assert best is not None, (R, mult, cap)
    return best


def sum_slots(name, x):
    _, R, _ = x.shape
    tr = _pick_rows(R, 16, 2304)

    def body(x_ref, o_ref):
        acc = x_ref[0].astype(F32)
        for d in range(1, NDEV):
            acc = acc + x_ref[d].astype(F32)
        o_ref[...] = acc

    return pl.pallas_call(
        body, name=name, grid=(R // tr,),
        in_specs=[pl.BlockSpec((NDEV, tr, LANE), lambda i: (0, i, 0))],
        out_specs=pl.BlockSpec((tr, LANE), lambda i: (i, 0)),
        out_shape=jax.ShapeDtypeStruct((R, LANE), F32), compiler_params=_cparams(),
    )(x)


def adamw(name, w, g, m, v):
    L, R, C = w.shape
    tr = _pick_rows(R, 8, 256) if R % 8 == 0 else R

    def body(w_ref, g_ref, m_ref, v_ref, d_ref, nm_ref, nv_ref):
        gg = g_ref[...]
        nm = B1 * m_ref[...] + (1.0 - B1) * gg
        nv = B2 * v_ref[...] + (1.0 - B2) * jnp.square(gg)
        m_hat = nm / (1.0 - B1 ** STEP)
        v_hat = nv / (1.0 - B2 ** STEP)
        d_ref[...] = -LR * (m_hat / (jnp.sqrt(v_hat) + EPS) + WD * w_ref[...])
        nm_ref[...] = nm
        nv_ref[...] = nv

    blk = pl.BlockSpec((1, tr, C), lambda l, i: (l, i, 0))
    shp = jax.ShapeDtypeStruct(w.shape, F32)
    return pl.pallas_call(
        body, name=name, grid=(L, R // tr), in_specs=[blk] * 4, out_specs=[blk] * 3, out_shape=[shp] * 3,
        compiler_params=_cparams(),
    )(w, g, m, v)


IN_SHARD = D_IN // NDEV
UQ_SHARD = HEADS * (NOPE + ROPE) // NDEV
W_IN_PAD = 1024
ROW_A, ROW_B, ROW_C, ROW_UKV, ROW_UQ, MISC_ROWS = 0, 512, 1024, 1536, 1792, 2176


def _in_perm_index():
    ar = np.arange
    z = lambda n: np.full((n,), -1, np.int64)
    mix = lambda lo1, lo2: np.concatenate([ar(lo + LANE * j, lo + LANE * (j + 1)) for j in range(CW // LANE)
                                           for lo in (lo1, lo2)])
    return np.concatenate([ar(4768, 7840), mix(0, 512), ar(1024, 1536), mix(1536, 2560), ar(4256, 4768), ar(2048, 2560),
                           ar(3072, 3584), ar(3968, 4224), ar(4224, 4256), z(OFF_Q - OFF_KR - ROPE), ar(3584, 3968),
                           z(NP - OFF_Q - QL)])


def _head_perm_index(a, b):
    parts = []
    for g in range(QUADS):
        h = np.arange(4 * g, 4 * g + 4)[:, None] * (a + b)
        parts += [(h + np.arange(a)[None]).reshape(-1), (h + a + np.arange(b)[None]).reshape(-1)]
    return np.concatenate(parts)


def _inverse(perm, n):
    inv = np.full((n,), -1, np.int64)
    inv[perm[perm >= 0]] = np.nonzero(perm >= 0)[0]
    return inv


IN_PERM = _in_perm_index()
UQ_PERM = _head_perm_index(NOPE, ROPE)
UKV_PERM = _head_perm_index(NOPE, VH)


def _to_gathered(perm, shard, pad):
    return np.where(perm >= 0, (perm // shard) * pad + perm % shard, -1)


def _from_full(inv, shard, pad):
    j, i = np.divmod(np.arange(NDEV * pad), pad)
    return np.where(i < shard, inv[np.minimum(j * shard + i, inv.shape[0] - 1)], -1)


def col_gather(name, srcs, out_shapes, jobs, deps=()):
    ns, nj, nd, no = len(srcs), len(jobs), len(deps), len(out_shapes)
    tables = [jnp.asarray(np.asarray(job[5], np.int32)[None, :]) for job in jobs]

    def view(ref, col0, width, r0, rc):
        n = ref.shape[-1]
        if len(ref.shape) == 3:
            return ref.at[col0 // n, pl.ds(r0, rc), pl.ds(col0 % n, width)]
        return ref.at[pl.ds(r0, rc), pl.ds(col0, width)]

    def slabs(shape):
        if len(shape) == 3:
            return [((d,), d * shape[2], (d + 1) * shape[2]) for d in range(shape[0])]
        w = 1024 if shape[1] > 1024 and shape[1] % 1024 == 0 else shape[1]
        return [((slice(None), pl.ds(c, w)), c, c + w) for c in range(0, shape[1], w)]

    src_slabs = [slabs(s.shape) for s in srcs]
    out_slabs = [slabs(sh) for sh in out_shapes]
    work, first_use, last_touch = [], {}, {}
    for ji, (si, srow, oi, orow, nrows, tgt) in enumerate(jobs):
        tgt = np.asarray(tgt)
        tw = 256 if out_shapes[oi][-1] % 256 == 0 else LANE
        sw = 256 if srcs[si].shape[-1] % 256 == 0 else LANE
        for t in range(tgt.shape[0] // tw):
            tt = tgt[t * tw:(t + 1) * tw]
            tiles = sorted(set((tt[tt >= 0] // sw).tolist()))
            straight = bool(tiles) and tt[0] >= 0 and tt[0] % LANE == 0 and np.array_equal(tt, tt[0] + np.arange(tw))
            cols = [(int(tt[0]) + k * LANE, LANE) for k in range(tw // LANE)] if straight else [(s * sw, sw) for s in tiles]
            need = sorted({(si, k) for c0, _ in cols for k, (_, lo, hi) in enumerate(src_slabs[si]) if lo <= c0 < hi})
            touch = [(oi, k) for k, (_, lo, hi) in enumerate(out_slabs[oi]) if lo <= t * tw < hi][0]
            for key in need:
                first_use.setdefault(key, len(work))
            last_touch[touch] = len(work)
            work.append((ji, t, tw, sw, tiles, straight, need, touch))
    in_order = sorted(first_use, key=first_use.get)
    in_sem = {key: i for i, key in enumerate(in_order)}
    out_keys = sorted(last_touch)
    out_sem = {key: i for i, key in enumerate(out_keys)}

    def body(*refs):
        src_hbm, tab_refs = refs[:ns], refs[ns:ns + nj]
        out_hbm = refs[ns + nj + nd:ns + nj + nd + no]
        scratch = refs[ns + nj + nd + no:]
        src_refs, out_refs, in_sems, out_sems = scratch[:ns], scratch[ns:ns + no], scratch[-2], scratch[-1]
        loads = {}
        for key in in_order:
            si, k = key
            idx = src_slabs[si][k][0]
            loads[key] = pltpu.make_async_copy(src_hbm[si].at[idx], src_refs[si].at[idx], in_sems.at[in_sem[key]])
            loads[key].start()
        arrived, stores = set(), []
        for wi, (ji, t, tw, sw, tiles, straight, need, touch) in enumerate(work):
            si, srow, oi, orow, nrows, tgt = jobs[ji]
            sref, oref = src_refs[si], out_refs[oi]
            rc = nrows if nrows <= 1024 else 1024
            for key in need:
                if key not in arrived:
                    loads[key].wait()
                    arrived.add(key)
            onehots = []
            if tiles and not straight:
                want = tab_refs[ji][:, t * tw:(t + 1) * tw]
                row = lax.broadcasted_iota(jnp.int32, (sw, tw), 0)
                onehots = [jnp.where(want == row + s * sw, 1.0, 0.0).astype(BF) for s in tiles]
            first = int(np.asarray(tgt)[t * tw])

            def chunk(ci, _, t=t, tw=tw, sw=sw, tiles=tiles, straight=straight, onehots=onehots, first=first,
                      sref=sref, oref=oref, srow=srow, orow=orow, rc=rc):
                r0 = ci * rc
                ro = pl.multiple_of(orow + r0, LANE)
                rs = pl.multiple_of(srow + r0, LANE)
                if not tiles:
                    view(oref, t * tw, tw, ro, rc)[...] = jnp.zeros((rc, tw), BF)
                elif straight:
                    for k in range(tw // LANE):
                        view(oref, t * tw + k * LANE, LANE, ro, rc)[...] = view(sref, first + k * LANE, LANE, rs, rc)[...]
                else:
                    acc = None
                    for s, oh in zip(tiles, onehots):
                        p = jnp.dot(view(sref, s * sw, sw, rs, rc)[...], oh, preferred_element_type=F32)
                        acc = p if acc is None else acc + p
                    view(oref, t * tw, tw, ro, rc)[...] = acc.astype(BF)
                return 0

            lax.fori_loop(0, nrows // rc, chunk, 0)
            if last_touch[touch] == wi:
                idx = out_slabs[touch[0]][touch[1]][0]
                cp = pltpu.make_async_copy(out_refs[touch[0]].at[idx], out_hbm[touch[0]].at[idx], out_sems.at[out_sem[touch]])
                cp.start()
                stores.append(cp)
        for cp in stores:
            cp.wait()

    return pl.pallas_call(
        body, name=name, in_specs=[ANY_SPEC] * ns + [pl.BlockSpec(memory_space=pltpu.VMEM)] * nj + [ANY_SPEC] * nd,
        out_specs=[ANY_SPEC] * no, out_shape=[jax.ShapeDtypeStruct(s, BF) for s in out_shapes],
        scratch_shapes=[pltpu.VMEM(s.shape, BF) for s in srcs] + [pltpu.VMEM(s, BF) for s in out_shapes]
        + [pltpu.SemaphoreType.DMA((len(in_order),)), pltpu.SemaphoreType.DMA((len(out_keys),))],
        compiler_params=_cparams(),
    )(*srcs, *tables, *deps)


def sum_adamw(name, recvs, w, m, v, lo=0, prev=None, row0=0):
    _, R, C = w.shape
    L = len(recvs)
    CP = recvs[0].shape[-1]
    tr = _pick_rows(R, 16, 128)
    n_prev = 0 if prev is None else 4

    def body(*refs):
        r_refs = refs[:L]
        w_ref, m_ref, v_ref = refs[L:L + 3]
        g_ref, d_ref, nm_ref, nv_ref, gsum = refs[L + 3 + n_prev:]
        layer = pl.program_id(0)
        for k in range(L):
            def total(k=k):
                acc = r_refs[k][0].astype(F32)
                for d in range(1, NDEV):
                    acc = acc + r_refs[k][d].astype(F32)
                gsum[...] = acc
            pl.when(layer == k)(total)
        gg = gsum[:, 0:C]
        nm = B1 * m_ref[...] + (1.0 - B1) * gg
        nv = B2 * v_ref[...] + (1.0 - B2) * jnp.square(gg)
        m_hat = nm / (1.0 - B1 ** STEP)
        v_hat = nv / (1.0 - B2 ** STEP)
        g_ref[...] = gg
        d_ref[...] = -LR * (m_hat / (jnp.sqrt(v_hat) + EPS) + WD * w_ref[...])
        nm_ref[...] = nm
        nv_ref[...] = nv

    assert row0 % tr == 0
    r_specs = [pl.BlockSpec((NDEV, tr, CP),
                            functools.partial(lambda l, i, k: (0, row0 // tr + jnp.where(l == k, i, 0), 0), k=k))
               for k in range(L)]
    blk = pl.BlockSpec((None, tr, C), lambda l, i: (l + lo, i, 0))
    shp = jax.ShapeDtypeStruct(w.shape, F32)
    return pl.pallas_call(
        body, name=name, grid=(L, R // tr), in_specs=r_specs + [blk] * 3 + [ANY_SPEC] * n_prev, out_specs=[blk] * 4,
        out_shape=[shp] * 4, input_output_aliases={L + 3 + i: i for i in range(n_prev)},
        scratch_shapes=[pltpu.VMEM((tr, CP), F32)], compiler_params=_cparams(),
    )(*recvs, w, m, v, *(prev or ()))


ALPHA = 8.0 ** 0.25
T_WIDE, T_NARROW = 512, 1024


def _rope_fn(sign):
    def fn(x, cos, sin):
        W = x.shape[-1]
        lane = lax.broadcasted_iota(jnp.int32, x.shape, 1)
        first_half = (lane % ROPE) < (ROPE // 2)
        rot = jnp.where(first_half, -pltpu.roll(x, W - ROPE // 2, 1), pltpu.roll(x, ROPE // 2, 1))
        return x * cos + sign * rot * sin
    return fn


def _modulate(xv, a):
    return xv * (1.0 + a[1:2, :]) + a[0:1, :]


def layer_fwd(x, ada3, W, tabs, S, u=None, ada_next=None):
    cos, sin = tabs
    T = T_NARROW
    if u is None:
        u = rowwise("modulate", _modulate, S, T, [(x, D_MODEL, 0)], [ada3], [(D_MODEL, BF)])[0]
    proj = mm(u, W["in"], name="mm_proj", tm=1024, tn=1024, out_dtype=BF)
    W = {**W, **W["late"](proj)}

    ca = conv_fwd("conv_a_fwd", proj, OFF_A, W["conv_a"], 31, "glu", S, CW)

    def a_post(c, ag, vec):
        n, _ = _ln_stats(c + vec[0:1, :])
        return _silu(n * vec[1:2, :] + vec[2:3, :]) * _silu(ag)

    h_a = rowwise("mix_a_post", a_post, S, T, [(ca, CW, 0), (proj, CW, OFF_AG)], [W["vec_a"]], [(CW, BF)])[0]
    y_a = mm(h_a, W["a_out"], name="mm_branch_out", out_dtype=BF)

    cb = conv_fwd("conv_b_fwd", proj, OFF_B, W["conv_b"], 3, "mul", S, CW)
    h_b = rowwise("mix_b_post", lambda c, gb, bg: gb * c * _silu(bg), S, T,
                  [(cb, CW, 0), (proj, CW, OFF_GB), (proj, CW, OFF_BG)], [], [(CW, BF)])[0]
    y_b = mm(h_b, W["b_out"], name="mm_branch_out", out_dtype=BF)

    def rms2(ql, kvl, gq, gkv):
        rq = lax.rsqrt(jnp.mean(ql * ql, axis=-1, keepdims=True) + RMS_EPS)
        rk = lax.rsqrt(jnp.mean(kvl * kvl, axis=-1, keepdims=True) + RMS_EPS)
        return ql * rq * gq, kvl * rk * gkv

    qn, kvn = rowwise("rms_fwd", rms2, S, T, [(proj, QL, OFF_Q), (proj, KVL, OFF_KV)], [W["gq"], W["gkv"]],
                      [(QL, BF), (KVL, BF)])
    q = mm(qn, W["uq"], name="mm_q")
    kv = mm(kvn, W["ukv"], name="mm_kv", out_dtype=BF)
    rope = _rope_fn(1.0)

    def rope_fwd(qv, kr, c1, s1):
        parts = []
        for g in range(QUADS):
            parts.append(qv[:, g * QW:g * QW + 2 * LANE].astype(BF))
            parts.append(rope(qv[:, g * QW + 2 * LANE:(g + 1) * QW], c1, s1).astype(BF))
        kp = rope(kr, c1, s1)
        kp = kp + pltpu.roll(kp, ROPE, 1) + pltpu.roll(kp, 2 * ROPE, 1) + pltpu.roll(kp, 3 * ROPE, 1)
        return jnp.concatenate(parts, axis=1), kp

    q_b, kpe = rowwise("rope_fwd", rope_fwd, S, T,
                       [(q, HEADS * (NOPE + ROPE), 0), (proj, LANE, OFF_KR), (cos, LANE, 0), (sin, LANE, 0)], [],
                       [(HEADS * (NOPE + ROPE), BF), (LANE, BF)])
    o, lse = attn_fwd(q_b, kv, kpe, S)
    h_c = rowwise("mix_c_post", lambda ov, cg: ov * _silu(cg), S, T, [(o, CW, 0), (proj, CW, OFF_CG)], [],
                  [(CW, BF)])[0]
    y_c = mm(h_c, W["c_out"], name="mm_branch_out", out_dtype=BF)

    def merge(la, lb, lc, ya, yb, yc):
        return _sigmoid(la) * ya + _sigmoid(lb) * yb + _sigmoid(lc) * yc

    m = rowwise("merge_fwd", merge, S, T_WIDE,
                [(proj, D_MODEL, 0), (proj, D_MODEL, 1024), (proj, D_MODEL, 2048), (y_a, D_MODEL, 0),
                 (y_b, D_MODEL, 0), (y_c, D_MODEL, 0)], [], [(D_MODEL, BF)])[0]
    out = mm(m, W["o"], name="mm_out")

    def ln_fwd(xv, ov, a, lnv, *nxt):
        n, _ = _ln_stats(ALPHA * xv + a[2:3, :] * ov)
        y = n * lnv[0:1, :] + lnv[1:2, :]
        return (y, _modulate(y, nxt[0])) if nxt else y

    res = rowwise("ln_fwd", ln_fwd, S, T_WIDE, [(x, D_MODEL, 0), (out, D_MODEL, 0)],
                  [ada3, W["lnv"]] + ([] if ada_next is None else [ada_next]),
                  [(D_MODEL, F32)] + ([] if ada_next is None else [(D_MODEL, BF)]))
    saved = dict(x=x, u=u, proj=proj, ca=ca, cb=cb, h_a=h_a, h_b=h_b, h_c=h_c, y_a=y_a, y_b=y_b, y_c=y_c, qn=qn,
                 kvn=kvn, q_b=q_b, kv=kv, kpe=kpe, lse=lse, o=o, m=m, out=out)
    return res[0], saved, W, (res[1] if ada_next is not None else None)


def layer_bwd(dxn, sv, ada3, W, tabs, S, before_in=None):
    cos, sin = tabs
    T = T_NARROW
    x, proj = sv["x"], sv["proj"]
    G = {}

    def ln_bwd(xv, ov, dy, a, lnv):
        gate = a[2:3, :]
        n, rstd = _ln_stats(ALPHA * xv + gate * ov)
        dr = _ln_bwd(dy * lnv[0:1, :], n, rstd)
        return ALPHA * dr, gate * dr, _colsum(dy * n), _colsum(dy), _colsum(dr * ov)

    dres, d_out, G["ln_g"], G["ln_b"], d_gate = rowwise(
        "ln_bwd", ln_bwd, S, T_WIDE, [(x, D_MODEL, 0), (sv["out"], D_MODEL, 0), (dxn, D_MODEL, 0)], [ada3, W["lnv"]],
        [(D_MODEL, F32), (D_MODEL, BF)], [D_MODEL] * 3)
    dm, G["w_o"] = mm_bwd_pair("mm_bwd_out", d_out, W["o"], sv["m"])

    def merge_bwd(dmv, la, lb, lc, ya, yb, yc):
        outs, dls = [], []
        for lg, yv in ((la, ya), (lb, yb), (lc, yc)):
            s = _sigmoid(lg)
            outs.append(dmv * s)
            dls.append((dmv * yv * s * (1.0 - s)).astype(BF))
        return (jnp.concatenate(dls, axis=1),) + tuple(outs)

    d_proj, dy_a, dy_b, dy_c = rowwise(
        "merge_bwd", merge_bwd, S, T_WIDE,
        [(dm, D_MODEL, 0), (proj, D_MODEL, 0), (proj, D_MODEL, 1024), (proj, D_MODEL, 2048), (sv["y_a"], D_MODEL, 0),
         (sv["y_b"], D_MODEL, 0), (sv["y_c"], D_MODEL, 0)], [], [(3 * D_MODEL, BF)] + [(D_MODEL, BF)] * 3,
        into=(None, NP, OFF_M))

    dh = {}
    for br, dy in (("a", dy_a), ("b", dy_b), ("c", dy_c)):
        dh[br], G["w_%s_out" % br] = mm_bwd_pair("mm_bwd_branch", dy, W[br + "_out"], sv["h_" + br])

    def a_post_bwd(c, ag, dhv, vec):
        n, rstd = _ln_stats(c + vec[0:1, :])
        z = n * vec[1:2, :] + vec[2:3, :]
        d_ag = dhv * _silu(z) * _dsilu(ag)
        dz = dhv * _silu(ag) * _dsilu(z)
        dc = _ln_bwd(dz * vec[1:2, :], n, rstd)
        return d_ag, dc, _colsum(dc), _colsum(dz * n), _colsum(dz)

    d_proj, dca, G["conv_a_b"], G["ln_a_g"], G["ln_a_b"] = rowwise(
        "mix_a_post_bwd", a_post_bwd, S, T, [(sv["ca"], CW, 0), (proj, CW, OFF_AG), (dh["a"], CW, 0)], [W["vec_a"]],
        [(CW, BF), (CW, F32)], [CW] * 3, into=(d_proj, NP, OFF_AG))
    d_proj, G["conv_a_w"] = conv_bwd("conv_a_bwd", proj, OFF_A, dca, W["conv_a"], 31, "glu", S, CW, d_proj)

    def b_post_bwd(c, gb, bg, dhv):
        sg = _silu(bg)
        d_gb_bg = jnp.concatenate([(dhv * sg * c).astype(BF), (dhv * gb * c * _dsilu(bg)).astype(BF)], axis=1)
        return d_gb_bg, dhv * sg * gb

    d_proj, dcb = rowwise("mix_b_post_bwd", b_post_bwd, S, T,
                          [(sv["cb"], CW, 0), (proj, CW, OFF_GB), (proj, CW, OFF_BG), (dh["b"], CW, 0)], [],
                          [(2 * CW, BF), (CW, F32)], into=(d_proj, NP, OFF_GB))
    d_proj, G["conv_b_w"] = conv_bwd("conv_b_bwd", proj, OFF_B, dcb, W["conv_b"], 3, "mul", S, CW, d_proj)

    d_proj, d_o = rowwise("mix_c_post_bwd", lambda ov, cg, dhv: (dhv * ov * _dsilu(cg), dhv * _silu(cg)), S, T,
                          [(sv["o"], CW, 0), (proj, CW, OFF_CG), (dh["c"], CW, 0)], [], [(CW, BF), (CW, F32)],
                          into=(d_proj, NP, OFF_CG))
    dq, d_kv, dkp_heads = attn_bwd(sv["q_b"], sv["kv"], sv["kpe"], sv["o"], sv["lse"], d_o, S)
    ropeT = _rope_fn(-1.0)

    def rope_bwd(dqv, dkp, c1, s1):
        parts = []
        for g in range(QUADS):
            parts.append(dqv[:, g * QW:g * QW + 2 * LANE].astype(BF))
            parts.append(ropeT(dqv[:, g * QW + 2 * LANE:(g + 1) * QW], c1, s1).astype(BF))
        f = dkp[:, :LANE] + dkp[:, LANE:]
        f = f + pltpu.roll(f, 64, 1)
        f = f + pltpu.roll(f, 32, 1)
        lane = lax.broadcasted_iota(jnp.int32, f.shape, 1)
        return jnp.concatenate(parts, axis=1), jnp.where(lane < ROPE, ropeT(f, c1, s1), 0.0)

    d_q, dk_pe = rowwise("rope_bwd", rope_bwd, S, T,
                         [(dq, HEADS * (NOPE + ROPE), 0), (dkp_heads, HEADS * ROPE, 0), (cos, LANE, 0), (sin, LANE, 0)],
                         [], [(HEADS * (NOPE + ROPE), BF), (LANE, BF)])
    d_qn, G["w_uq"] = mm_bwd_pair("mm_bwd_q", d_q, W["uq"], sv["qn"])
    d_kvn, G["w_ukv"] = mm_bwd_pair("mm_bwd_kv", d_kv, W["ukv"], sv["kvn"])

    def rms_bwd(ql, kvl, dqn, dkn, dkp, gq, gkv):
        res = []
        for xv, dy, g in ((ql, dqn, gq), (kvl, dkn, gkv)):
            r = lax.rsqrt(jnp.mean(xv * xv, axis=-1, keepdims=True) + RMS_EPS)
            dxh = dy * g
            res.append(((r * (dxh - xv * (r * r) * jnp.mean(dxh * xv, axis=-1, keepdims=True))).astype(BF),
                        _colsum(dy * xv * r)))
        pad = jnp.zeros((ql.shape[0], LANE), BF)
        return jnp.concatenate([res[1][0], dkp, pad, res[0][0], pad], axis=1), res[0][1], res[1][1]

    d_proj, G["q_norm_g"], G["kv_norm_g"] = rowwise(
        "rms_bwd", rms_bwd, S, T,
        [(proj, QL, OFF_Q), (proj, KVL, OFF_KV), (d_qn, QL, 0), (d_kvn, KVL, 0), (dk_pe, LANE, 0)],
        [W["gq"], W["gkv"]], [(NP - OFF_KV, BF)], [QL, KVL], into=(d_proj, NP, OFF_KV))
    deps = before_in(G) if before_in is not None else ()
    du = mm(d_proj, W["in"], name="mm_du", trans_b=True, tm=1024, tk=2048, deps=deps)
    G["w_in"] = mm(sv["u"], d_proj, name="mm_gw_in", trans_a=True, out_dtype=BF, tm=1024, tk=2048, deps=deps)

    def mod_bwd(duv, xv, dr, a):
        return duv * (1.0 + a[1:2, :]) + dr, _colsum(duv), _colsum(duv * xv)

    dx, d_shift, d_scale = rowwise("mod_bwd", mod_bwd, S, T_WIDE, [(du, D_MODEL, 0), (x, D_MODEL, 0), (dres, D_MODEL, 0)],
                                   [ada3], [(D_MODEL, F32)], [D_MODEL] * 2)
    d_ada = jnp.concatenate([d_shift, d_scale, d_gate], axis=1)
    return dx, G, d_ada


SMALL = ("conv_a_b", "ln_a_g", "ln_a_b", "q_norm_g", "kv_norm_g", "ln_g", "ln_b")


def _rows(v):
    n = v.shape[0]
    r = -(-n // (LANE * 16)) * 16
    return jnp.pad(v, (0, r * LANE - n)).reshape(r, LANE)


def kernel(x, c, positions, w_ada, b_ada, w_in, conv_a_w, conv_a_b, ln_a_g, ln_a_b, w_a_out, conv_b_w, w_b_out, q_norm_g, kv_norm_g, w_uq, w_ukv, w_c_out, w_o, ln_g, ln_b, loss_target, m_w_ada, m_b_ada, m_w_in, m_conv_a_w, m_conv_a_b, m_ln_a_g, m_ln_a_b, m_w_a_out, m_conv_b_w, m_w_b_out, m_q_norm_g, m_kv_norm_g, m_w_uq, m_w_ukv, m_w_c_out, m_w_o, m_ln_g, m_ln_b, v_w_ada, v_b_ada, v_w_in, v_conv_a_w, v_conv_a_b, v_ln_a_g, v_ln_a_b, v_w_a_out, v_conv_b_w, v_w_b_out, v_q_norm_g, v_kv_norm_g, v_w_uq, v_w_ukv, v_w_c_out, v_w_o, v_ln_g, v_ln_b):
    P = dict(w_ada=w_ada, b_ada=b_ada, w_in=w_in, conv_a_w=conv_a_w, conv_a_b=conv_a_b, ln_a_g=ln_a_g, ln_a_b=ln_a_b,
             w_a_out=w_a_out, conv_b_w=conv_b_w, w_b_out=w_b_out, q_norm_g=q_norm_g, kv_norm_g=kv_norm_g, w_uq=w_uq,
             w_ukv=w_ukv, w_c_out=w_c_out, w_o=w_o, ln_g=ln_g, ln_b=ln_b)
    Mo = dict(w_ada=m_w_ada, b_ada=m_b_ada, w_in=m_w_in, conv_a_w=m_conv_a_w, conv_a_b=m_conv_a_b, ln_a_g=m_ln_a_g,
              ln_a_b=m_ln_a_b, w_a_out=m_w_a_out, conv_b_w=m_conv_b_w, w_b_out=m_w_b_out, q_norm_g=m_q_norm_g,
              kv_norm_g=m_kv_norm_g, w_uq=m_w_uq, w_ukv=m_w_ukv, w_c_out=m_w_c_out, w_o=m_w_o, ln_g=m_ln_g, ln_b=m_ln_b)
    Vo = dict(w_ada=v_w_ada, b_ada=v_b_ada, w_in=v_w_in, conv_a_w=v_conv_a_w, conv_a_b=v_conv_a_b, ln_a_g=v_ln_a_g,
              ln_a_b=v_ln_a_b, w_a_out=v_w_a_out, conv_b_w=v_conv_b_w, w_b_out=v_w_b_out, q_norm_g=v_q_norm_g,
              kv_norm_g=v_kv_norm_g, w_uq=v_w_uq, w_ukv=v_w_ukv, w_c_out=v_w_c_out, w_o=v_w_o, ln_g=v_ln_g, ln_b=v_ln_b)
    ORDER = ("w_ada", "b_ada", "w_in", "conv_a_w", "conv_a_b", "ln_a_g", "ln_a_b", "w_a_out", "conv_b_w", "w_b_out",
             "q_norm_g", "kv_norm_g", "w_uq", "w_ukv", "w_c_out", "w_o", "ln_g", "ln_b")
    L = w_ada.shape[0]
    S = x.shape[1]
    me = 4 * lax.axis_index("x") + 2 * lax.axis_index("y") + lax.axis_index("c")
    x2 = x[0]
    tgt = loss_target[0]

    small_in = _rows(jnp.concatenate([c.reshape(-1), conv_a_w.reshape(-1), conv_b_w.reshape(-1)]))
    w_in_b = jnp.pad(w_in.astype(BF), ((0, 0), (0, 0), (0, W_IN_PAD - IN_SHARD)))
    misc_b = jnp.concatenate([w_a_out, w_b_out, w_c_out, w_ukv, jnp.pad(w_uq, ((0, 0), (0, 0), (0, LANE - UQ_SHARD)))],
                             axis=1).astype(BF)
    w_o_b = w_o.astype(BF)
    gathered = [None] * L
    sg = exchange("gather_small", [small_in], [])[0]
    sgf = sg.reshape(NDEV, -1)
    c_all = sgf[:, :D_MODEL]
    o1 = D_MODEL + L * 31 * 64
    conv_a_full = sgf[:, D_MODEL:o1].reshape(NDEV, L, 31, 64).transpose(1, 2, 0, 3).reshape(L, 31, CW)
    conv_b_full = sgf[:, o1:o1 + L * 3 * 64].reshape(NDEV, L, 3, 64).transpose(1, 2, 0, 3).reshape(L, 3, CW)

    c_act = rowwise("silu_c", _silu, 16, 16, [(jnp.pad(c_all, ((0, 8), (0, 0))), D_MODEL, 0)], [], [(D_MODEL, BF)])[0]
    ncol = w_ada.shape[2]
    w_ada_b = w_ada.astype(BF).transpose(1, 0, 2).reshape(D_MODEL, L * ncol)
    b_mine = lax.dynamic_slice_in_dim(b_ada, me * ncol, ncol, axis=1).reshape(1, L * ncol)
    ada_part = mm(c_act, w_ada_b, name="mm_ada", bias=b_mine)
    ada_rows = -(-(L * ncol) // (LANE * 8)) * 8
    ada_send = jnp.pad(ada_part[:NDEV].reshape(NDEV, -1, LANE), ((0, 0), (0, ada_rows - L * ncol // LANE), (0, 0)))
    ada_recv = exchange("a2a_ada", [], [ada_send])[0]
    ada = ada_recv[:, :L * ncol // LANE].reshape(NDEV, L, ncol).transpose(1, 0, 2).reshape(L, 3, D_MODEL)
    gathered[0] = [gather_two_level("gather0_w_in", w_in_b[0], ada)]
    rests = {}
    rests[0], rest_token = exchange_begin("gather0_rest", [misc_b[0], w_o_b[0]], 2, gathered[0][0])

    inv_freq = ROPE_THETA ** (-jnp.arange(0, ROPE, 2, dtype=F32) / ROPE)
    ang = positions[0].astype(F32)[:, None] * inv_freq
    tabs = (jnp.tile(jnp.cos(ang), (1, 2 * LANE // ROPE)), jnp.tile(jnp.sin(ang), (1, 2 * LANE // ROPE)))

    straight = np.arange(D_MODEL)
    fwd_in = [(0, 0, 0, 0, D_MODEL, _to_gathered(IN_PERM, IN_SHARD, W_IN_PAD))]
    fwd_misc = [(0, ROW_A, 0, 0, CW, straight), (0, ROW_B, 1, 0, CW, straight), (0, ROW_C, 2, 0, CW, straight),
                (0, ROW_UKV, 3, 0, KVL, UKV_PERM), (0, ROW_UQ, 4, 0, QL, _to_gathered(UQ_PERM, UQ_SHARD, LANE))]
    rev_in = [(0, 0, 0, 0, D_MODEL, _from_full(_inverse(IN_PERM, D_IN), IN_SHARD, W_IN_PAD))]
    rev_misc = [(0, 0, 0, ROW_A, CW, straight), (1, 0, 0, ROW_B, CW, straight), (2, 0, 0, ROW_C, CW, straight),
                (3, 0, 0, ROW_UKV, KVL, _from_full(_inverse(UKV_PERM, HEADS * (NOPE + VH)), LANE, LANE)),
                (4, 0, 0, ROW_UQ, QL, _from_full(_inverse(UQ_PERM, HEADS * (NOPE + ROPE)), UQ_SHARD, LANE))]

    def layer_weights(l, deps):
        w_in_p = col_gather("relayout_w_in", [gathered[l][0]], [(D_MODEL, NP)], fwd_in, deps)[0]

        def late(after):
            if len(gathered[l]) == 1:
                gathered[l] += exchange_end(rests[l], after)
            _, g_misc, g_o = gathered[l]
            a_out, b_out, c_out, ukv, uq = col_gather(
                "relayout_misc", [g_misc],
                [(CW, D_MODEL)] * 3 + [(KVL, HEADS * (NOPE + VH)), (QL, HEADS * (NOPE + ROPE))], fwd_misc, deps)
            return {"a_out": a_out, "b_out": b_out, "c_out": c_out, "uq": uq, "ukv": ukv,
                    "o": g_o.reshape(D_MODEL, D_MODEL)}

        return {
            "in": w_in_p, "late": late,
            "conv_a": jnp.pad(conv_a_full[l], ((0, 1), (0, 0))), "conv_b": jnp.pad(conv_b_full[l], ((0, 5), (0, 0))),
            "vec_a": jnp.stack([conv_a_b[l], ln_a_g[l], ln_a_b[l]]), "gq": q_norm_g[l][None], "gkv": kv_norm_g[l][None],
            "lnv": jnp.stack([ln_g[l], ln_b[l]]),
        }

    h = x2
    saved, weights = [], []
    handles, token, u_next = {}, rest_token, None
    for l in range(1, L):
        handles[l], token = exchange_begin("gather%d" % l, [w_in_b[l]], 1, token)
        rests[l], token = exchange_begin("gather%d_rest" % l, [misc_b[l], w_o_b[l]], 2, token)
    for l in range(L):
        ada_l, deps = (ada[l] + token[0, 0], (token,)) if l == 0 else (ada[l], ())
        h, sv, Wl, u_next = layer_fwd(h, ada_l, layer_weights(l, deps), tabs, S, u_next, ada[l + 1] if l + 1 < L else None)
        if l + 1 < L:
            gathered[l + 1] = exchange_end(handles[l + 1], h)
        saved.append(sv)
        weights.append(Wl)

    def loss_fn(y, t):
        e = y - t
        return e * (1.0 / D_MODEL), _colsum(e * e)

    dy, sq = rowwise("loss", loss_fn, S, 256, [(h, D_MODEL, 0), (tgt, D_MODEL, 0)], [], [(D_MODEL, F32)], [D_MODEL])
    loss = lax.psum(0.5 * jnp.sum(sq) / D_MODEL, ("x", "y", "c"))
    loss, dy = lax.optimization_barrier((loss, dy))

    grads, d_adas, recv = [None] * L, [None] * L, [None] * L
    pending, token = None, None

    def send_rest(g):
        send_misc = col_gather("unrelayout_misc", [g["w_a_out"], g["w_b_out"], g["w_c_out"], g["w_ukv"], g["w_uq"]],
                               [(NDEV, MISC_ROWS, LANE)], rev_misc)[0]
        return [send_misc, g["w_o"].reshape(NDEV, D_MODEL // NDEV, D_MODEL)]

    rest0 = []

    def early_rest(g):
        handle, tok = exchange_begin("scatter0_rest", send_rest(g), 0, g["w_o"])
        rest0.append(handle)
        return (tok,)

    for l in reversed(range(L)):
        ada_l = ada[l] if token is None else ada[l] + token[0, 0]
        dy, g, d_adas[l] = layer_bwd(dy, saved[l], ada_l, weights[l], tabs, S, early_rest if l == 0 else None)
        grads[l] = g
        if pending is not None:
            recv[l + 1] = exchange_end(pending, dy)
        send_in = col_gather("unrelayout_w_in", [g["w_in"]], [(NDEV, D_MODEL, W_IN_PAD)], rev_in)[0]
        if l == 0:
            def layer_vec(i):
                return jnp.concatenate([grads[i][n].reshape(-1) for n in SMALL] + [d_adas[i].reshape(-1)])

            def to_owners(name, taps):
                full = jnp.stack([grads[i][name][:taps] for i in range(L)])
                return full.reshape(L, taps, NDEV, CW // NDEV).transpose(2, 0, 1, 3).reshape(NDEV, -1)

            conv_send = jnp.concatenate([to_owners("conv_a_w", 31), to_owners("conv_b_w", 3)], axis=1)
            conv_rows = -(-conv_send.shape[1] // (LANE * 16)) * 16
            conv_send = jnp.pad(conv_send, ((0, 0), (0, conv_rows * LANE - conv_send.shape[1])))
            small_sizes = [int(grads[0][n].size) for n in SMALL] + [3 * D_MODEL]
            gsmall, conv_recv = exchange("gather_small_grads", [_rows(jnp.concatenate([layer_vec(i) for i in range(L)]))],
                                         [conv_send.reshape(NDEV, conv_rows, LANE)])
            pending, token = exchange_begin("scatter0", [send_in], 0, gsmall)
        else:
            pending, token = exchange_begin("scatter%d" % l, [send_in] + send_rest(g), 0,
                                            dy if l + 1 == L else recv[l + 1][0])
    grad_x = dy[None]

    gsmall = gsmall + token[0, 0]
    gsum = sum_slots("sum_small", gsmall).reshape(-1)
    recv[0] = [None] + exchange_end(rest0[0], gsum)
    Gr = {}
    offs = np.cumsum([0] + small_sizes)
    per_layer = int(offs[-1])
    gsum = gsum[:L * per_layer].reshape(L, per_layer)
    for i, n in enumerate(SMALL):
        Gr[n] = gsum[:, offs[i]:offs[i + 1]]
    csum = sum_slots("sum_conv", conv_recv + token[0, 0]).reshape(-1)
    n_a = L * 31 * (CW // NDEV)
    Gr["conv_a_w"] = csum[:n_a].reshape(L, 31, CW // NDEV)
    Gr["conv_b_w"] = csum[n_a:n_a + L * 3 * (CW // NDEV)].reshape(L, 3, CW // NDEV)
    Gr["b_ada"] = gsum[:, offs[7]:offs[8]]
    d_ada_all = gsmall.reshape(NDEV, -1)[:, :L * per_layer].reshape(NDEV, L, per_layer)[:, :, offs[7]:offs[8]]
    d_mine = lax.dynamic_slice_in_dim(d_ada_all, me * ncol, ncol, axis=2).reshape(NDEV, L * ncol)
    g_ada = mm(c_act, jnp.pad(d_mine, ((0, 8), (0, 0))).astype(BF), name="mm_gw_ada", trans_a=True)
    Gr["w_ada"] = g_ada.reshape(D_MODEL, L, ncol).transpose(1, 0, 2)

    D, NM, NV = {}, {}, {}
    D["w_ada"], NM["w_ada"], NV["w_ada"] = adamw("adamw_w_ada", P["w_ada"], Gr["w_ada"], Mo["w_ada"], Vo["w_ada"])
    Gr["w_o"], D["w_o"], NM["w_o"], NV["w_o"] = sum_adamw(
        "sum_adamw_w_o", [recv[l][2] for l in range(L)], P["w_o"], Mo["w_o"], Vo["w_o"])
    for n, row0 in (("w_a_out", ROW_A), ("w_b_out", ROW_B), ("w_c_out", ROW_C), ("w_ukv", ROW_UKV), ("w_uq", ROW_UQ)):
        Gr[n], D[n], NM[n], NV[n] = sum_adamw("sum_adamw_" + n, [recv[l][1] for l in range(L)], P[n], Mo[n], Vo[n],
                                              row0=row0)
    w_l, m_l, v_l, _ = lax.optimization_barrier((P["w_in"], Mo["w_in"], Vo["w_in"], token))
    upper = sum_adamw("sum_adamw_w_in_upper", [recv[l][0] for l in range(1, L)], w_l, m_l, v_l, lo=1)
    recv[0][0] = exchange_end(pending, upper[1])[0]
    Gr["w_in"], D["w_in"], NM["w_in"], NV["w_in"] = sum_adamw(
        "sum_adamw_w_in", [recv[0][0]], w_l, m_l, v_l, lo=0, prev=upper)
    for n in ("b_ada", "conv_a_w", "conv_b_w") + SMALL:
        shape = P[n].shape if P[n].ndim == 3 else (1,) + P[n].shape
        res = adamw("adamw_" + n, *[t.reshape(shape) for t in (P[n], Gr[n], Mo[n], Vo[n])])
        D[n], NM[n], NV[n] = [t.reshape(P[n].shape) for t in res]
    return (loss, grad_x, *[Gr[n] for n in ORDER], *[D[n] for n in ORDER], *[NM[n] for n in ORDER],
            *[NV[n] for n in ORDER])
```

```python
import functools
import math

import numpy as np
import jax
import jax.numpy as jnp
from jax import lax
from jax.experimental import pallas as pl
from jax.experimental.pallas import tpu as pltpu

BF = jnp.bfloat16
F32 = jnp.float32
MESH = pl.DeviceIdType.MESH
NDEV = 8

HEADS, NOPE, ROPE, VH = 8, 64, 32, 64
HP = 128
ROPE_THETA = 10000.0
LN_EPS = 1e-5
RMS_EPS = 1e-6
LR, B1, B2, EPS, WD, STEP = 0.001, 0.9, 0.999, 1e-08, 0.01, 10

LANE = 128
VMEM_LIMIT = 56 * 1024 * 1024

D_MODEL, CW, QL, KVL = 1024, 512, 384, 256
OFF_M, OFF_A, OFF_AG, OFF_B, OFF_CG, OFF_GB, OFF_BG = 0, 3072, 4096, 4608, 5632, 6144, 6656
OFF_KV, OFF_KR, OFF_Q, NP = 7168, 7424, 7680, 8192
D_IN = 7840


def _cparams(**kw):
    return pltpu.CompilerParams(vmem_limit_bytes=VMEM_LIMIT, **kw)


def _sigmoid(x):
    return jax.nn.sigmoid(x)


def _silu(x):
    return x * _sigmoid(x)


def _dsilu(x):
    s = _sigmoid(x)
    return s * (1.0 + x * (1.0 - s))


def _pick_tile(n, cap, mult):
    if n <= cap:
        return n
    for t in range(cap - cap % mult, 0, -mult):
        if n % t == 0:
            return t
    raise ValueError((n, cap, mult))


def mm(a, b, *, name, trans_a=False, trans_b=False, out_dtype=F32, bias=None, tm=1024, tn=1024, tk=2048, deps=()):
    if trans_a:
        K, M = a.shape
    else:
        M, K = a.shape
    if trans_b:
        N, K2 = b.shape
    else:
        K2, N = b.shape
    assert K == K2 and not (trans_a and trans_b), (a.shape, b.shape)
    tm, tn = _pick_tile(M, tm, 16), _pick_tile(N, tn, LANE)
    tk = _pick_tile(K, tk, LANE if trans_b else 16)
    assert M % tm == 0 and N % tn == 0 and K % tk == 0, (M, N, K, tm, tn, tk)
    nk = K // tk
    dims = (((0 if trans_a else 1,), (1 if trans_b else 0,)), ((), ()))
    has_bias = bias is not None

    def body(*refs):
        a_ref, b_ref = refs[0], refs[1]
        bias_ref = refs[2] if has_bias else None
        o_ref = refs[(3 if has_bias else 2) + len(deps)]
        p = lax.dot_general(a_ref[...], b_ref[...], dims, preferred_element_type=F32)

        def finish(v):
            if has_bias:
                v = v + bias_ref[...]
            o_ref[...] = v.astype(o_ref.dtype)

        if nk == 1:
            finish(p)
        else:
            acc = refs[-1]
            k = pl.program_id(2)

            @pl.when(k == 0)
            def _():
                acc[...] = p

            @pl.when(k > 0)
            def _():
                acc[...] += p

            @pl.when(k == nk - 1)
            def _():
                finish(acc[...])

    if trans_a:
        a_spec = pl.BlockSpec((tk, tm), lambda i, j, k: (k, i))
    else:
        a_spec = pl.BlockSpec((tm, tk), lambda i, j, k: (i, k))
    if trans_b:
        b_spec = pl.BlockSpec((tn, tk), lambda i, j, k: (j, k))
    else:
        b_spec = pl.BlockSpec((tk, tn), lambda i, j, k: (k, j))
    in_specs = [a_spec, b_spec]
    args = [a, b]
    if has_bias:
        in_specs.append(pl.BlockSpec((1, tn), lambda i, j, k: (0, j)))
        args.append(bias)
    in_specs += [ANY_SPEC] * len(deps)
    args += list(deps)
    return pl.pallas_call(
        body, name=name, grid=(M // tm, N // tn, nk),
        in_specs=in_specs, out_specs=pl.BlockSpec((tm, tn), lambda i, j, k: (i, j)),
        out_shape=jax.ShapeDtypeStruct((M, N), out_dtype),
        scratch_shapes=[pltpu.VMEM((tm, tn), F32)] if nk > 1 else [],
        compiler_params=_cparams(),
    )(*args)


def mm_bwd_pair(name, dy, w, h, tm=1024):
    M, N = dy.shape
    K = w.shape[0]
    tm = min(tm, M)
    steps = M // tm

    def body(dy_ref, w_ref, h_ref, dh_ref, gw_ref, acc):
        dyv = dy_ref[...]
        dh_ref[...] = lax.dot_general(dyv, w_ref[...], (((1,), (1,)), ((), ())),
                                      preferred_element_type=F32).astype(dh_ref.dtype)
        part = lax.dot_general(h_ref[...], dyv, (((0,), (0,)), ((), ())), preferred_element_type=F32)
        i = pl.program_id(0)

        @pl.when(i == 0)
        def _():
            acc[...] = part

        @pl.when(i > 0)
        def _():
            acc[...] += part

        @pl.when(i == steps - 1)
        def _():
            gw_ref[...] = acc[...].astype(gw_ref.dtype)

    return pl.pallas_call(
        body, name=name, grid=(steps,),
        in_specs=[pl.BlockSpec((tm, N), lambda i: (i, 0)), pl.BlockSpec((K, N), lambda i: (0, 0)),
                  pl.BlockSpec((tm, K), lambda i: (i, 0))],
        out_specs=[pl.BlockSpec((tm, K), lambda i: (i, 0)), pl.BlockSpec((K, N), lambda i: (0, 0))],
        out_shape=[jax.ShapeDtypeStruct((M, K), BF), jax.ShapeDtypeStruct((K, N), BF)],
        scratch_shapes=[pltpu.VMEM((K, N), F32)], compiler_params=_cparams(),
    )(dy, w, h)


def rowwise(name, fn, S, T, row_ins, full_ins, row_outs, acc_outs=(), into=None):
    n_in = len(row_ins) + len(full_ins)
    n_ro, n_ao = len(row_outs), len(acc_outs)
    alias = into is not None and into[0] is not None
    T = min(T, S)

    def body(*refs):
        vals = [r[...] for r in refs[:n_in]]
        vals = [v.astype(F32) if v.dtype == BF else v for v in vals]
        outs = fn(*vals)
        if not isinstance(outs, (tuple, list)):
            outs = (outs,)
        assert len(outs) == n_ro + n_ao, (name, len(outs))
        o0 = n_in + (1 if alias else 0)
        for r, v in zip(refs[o0:o0 + n_ro], outs[:n_ro]):
            r[...] = v.astype(r.dtype)
        first = pl.program_id(0) == 0
        for r, v in zip(refs[o0 + n_ro:], outs[n_ro:]):
            def init(r=r, v=v):
                r[...] = v

            def accum(r=r, v=v):
                r[...] += v

            pl.when(first)(init)
            pl.when(jnp.logical_not(first))(accum)

    in_specs, args = [], []
    for arr, W, off in row_ins:
        assert off % W == 0 and arr.shape[0] == S, (name, arr.shape, W, off)
        in_specs.append(pl.BlockSpec((T, W), functools.partial(lambda i, cb: (i, cb), cb=off // W)))
        args.append(arr)
    for arr in full_ins:
        in_specs.append(pl.BlockSpec(arr.shape, lambda i: (0, 0)))
        args.append(arr)
    out_specs = [pl.BlockSpec((T, W), lambda i: (i, 0)) for W, _ in row_outs]
    out_shape = [jax.ShapeDtypeStruct((S, W), dt) for W, dt in row_outs]
    aliases = {}
    if into is not None:
        buf, total, off = into
        W0, dt0 = row_outs[0]
        assert off % W0 == 0
        out_specs[0] = pl.BlockSpec((T, W0), functools.partial(lambda i, cb: (i, cb), cb=off // W0))
        out_shape[0] = jax.ShapeDtypeStruct((S, total), dt0)
        if alias:
            in_specs.append(ANY_SPEC)
            args.append(buf)
            aliases = {n_in: 0}
    out_specs += [pl.BlockSpec((1, W), lambda i: (0, 0)) for W in acc_outs]
    out_shape += [jax.ShapeDtypeStruct((1, W), F32) for W in acc_outs]
    return pl.pallas_call(
        body, name=name, grid=(S // T,), in_specs=in_specs, out_specs=out_specs, out_shape=out_shape,
        input_output_aliases=aliases, compiler_params=_cparams(),
    )(*args)


def _colsum(v):
    return jnp.sum(v, axis=0, keepdims=True)


def _ln_stats(r):
    mu = jnp.mean(r, axis=-1, keepdims=True)
    d = r - mu
    var = jnp.mean(d * d, axis=-1, keepdims=True)
    rstd = lax.rsqrt(var + LN_EPS)
    return d * rstd, rstd


def _ln_bwd(dn, n, rstd):
    return rstd * (dn - jnp.mean(dn, axis=-1, keepdims=True) - n * jnp.mean(dn * n, axis=-1, keepdims=True))


CPAD = 32
TC = 64


def _pre(mode, x1, x2):
    return x1 * _sigmoid(x2) if mode == "glu" else x1 * x2


def _sublane_shifts(ext):
    n = TC + CPAD
    return [ext] + [pltpu.roll(ext, n - r, 0) for r in range(1, 8)]


def _shifted(shifts, sft):
    q, r = divmod(sft, 8)
    return shifts[r][8 * q:8 * q + TC]


def _interleaved_specs(S, off):
    return [pl.BlockSpec((S, LANE), functools.partial(lambda j, o: (0, o + 2 * j), o=off // LANE)),
            pl.BlockSpec((S, LANE), functools.partial(lambda j, o: (0, o + 2 * j + 1), o=off // LANE))]


def conv_fwd(name, src, off, w_pad, taps, mode, S, C):
    nchunk = S // TC

    def body(x1_ref, x2_ref, w_ref, o_ref, a_pad):
        a_pad[0:CPAD, :] = jnp.zeros((CPAD, LANE), F32)

        def fill(i, _):
            r = pl.multiple_of(i * 256, 256)
            a_pad[pl.ds(CPAD + r, 256), :] = _pre(mode, x1_ref[pl.ds(r, 256), :].astype(F32),
                                                  x2_ref[pl.ds(r, 256), :].astype(F32))
            return 0

        lax.fori_loop(0, S // 256, fill, 0)

        def chunk(i, _):
            base = pl.multiple_of(i * TC, TC)
            shifts = _sublane_shifts(a_pad[pl.ds(base, TC + CPAD), :])
            acc = jnp.zeros((TC, LANE), F32)
            for k in range(taps):
                acc = acc + w_ref[pl.ds(k, 1), :] * _shifted(shifts, CPAD - (taps - 1) + k)
            o_ref[pl.ds(base, TC), :] = acc
            return 0

        lax.fori_loop(0, nchunk, chunk, 0)

    kp = w_pad.shape[0]
    return pl.pallas_call(
        body, name=name, grid=(C // LANE,),
        in_specs=_interleaved_specs(S, off) + [pl.BlockSpec((kp, LANE), lambda j: (0, j))],
        out_specs=pl.BlockSpec((S, LANE), lambda j: (0, j)),
        out_shape=jax.ShapeDtypeStruct((S, C), F32),
        scratch_shapes=[pltpu.VMEM((S + CPAD, LANE), F32)],
        compiler_params=_cparams(),
    )(src, src, w_pad)


def conv_bwd(name, src, off, dc, w_pad, taps, mode, S, C, buf):
    nchunk = S // TC
    kp = w_pad.shape[0]

    def body(x1_ref, x2_ref, dc_ref, w_ref, _, d_ref, dw_ref, a_pad, dc_pad, dw_acc):
        a_pad[0:CPAD, :] = jnp.zeros((CPAD, LANE), F32)
        dc_pad[S:S + CPAD, :] = jnp.zeros((CPAD, LANE), F32)
        dw_acc[...] = jnp.zeros(dw_acc.shape, F32)

        def fill(i, _):
            r = pl.multiple_of(i * 256, 256)
            a_pad[pl.ds(CPAD + r, 256), :] = _pre(mode, x1_ref[pl.ds(r, 256), :].astype(F32),
                                                  x2_ref[pl.ds(r, 256), :].astype(F32))
            dc_pad[pl.ds(r, 256), :] = dc_ref[pl.ds(r, 256), :]
            return 0

        lax.fori_loop(0, S // 256, fill, 0)

        def chunk(i, _):
            base = pl.multiple_of(i * TC, TC)
            shifts_d = _sublane_shifts(dc_pad[pl.ds(base, TC + CPAD), :])
            shifts_a = _sublane_shifts(a_pad[pl.ds(base, TC + CPAD), :])
            dcv = shifts_d[0][0:TC]
            da = jnp.zeros((TC, LANE), F32)
            for k in range(taps):
                da = da + w_ref[pl.ds(k, 1), :] * _shifted(shifts_d, taps - 1 - k)
                prod = dcv * _shifted(shifts_a, CPAD - (taps - 1) + k)
                fold = prod[0:8]
                for g in range(1, TC // 8):
                    fold = fold + prod[8 * g:8 * g + 8]
                dw_acc[pl.ds(8 * k, 8), :] += fold
            x1 = x1_ref[pl.ds(base, TC), :].astype(F32)
            x2 = x2_ref[pl.ds(base, TC), :].astype(F32)
            if mode == "glu":
                s = _sigmoid(x2)
                d1, d2 = da * s, da * x1 * s * (1.0 - s)
            else:
                d1, d2 = da * x2, da * x1
            d_ref[pl.ds(base, TC), 0:LANE] = d1.astype(BF)
            d_ref[pl.ds(base, TC), LANE:2 * LANE] = d2.astype(BF)
            return 0

        lax.fori_loop(0, nchunk, chunk, 0)
        dw_ref[...] = jnp.zeros(dw_ref.shape, F32)
        for k in range(taps):
            dw_ref[pl.ds(k, 1), :] = jnp.sum(dw_acc[pl.ds(8 * k, 8), :], axis=0, keepdims=True)

    blk = pl.BlockSpec((S, LANE), lambda j: (0, j))
    return pl.pallas_call(
        body, name=name, grid=(C // LANE,),
        in_specs=_interleaved_specs(S, off) + [blk, pl.BlockSpec((kp, LANE), lambda j: (0, j)), ANY_SPEC],
        out_specs=[pl.BlockSpec((S, 2 * LANE), functools.partial(lambda j, o: (0, o + j), o=off // (2 * LANE))),
                   pl.BlockSpec((kp, LANE), lambda j: (0, j))],
        out_shape=[jax.ShapeDtypeStruct(buf.shape, BF), jax.ShapeDtypeStruct((kp, C), F32)],
        input_output_aliases={4: 0},
        scratch_shapes=[pltpu.VMEM((S + CPAD, LANE), F32), pltpu.VMEM((S + CPAD, LANE), F32),
                        pltpu.VMEM((8 * kp, LANE), F32)],
        compiler_params=_cparams(),
    )(src, src, dc, w_pad, buf)


FWD_TILES = (512, 512)
BWD_TILES = (512, 512)
QUADS = HEADS // 4
QW, KVW = 4 * (NOPE + ROPE), 4 * (NOPE + VH)
SCALE = (NOPE + ROPE) ** -0.5
NT_DIMS = (((1,), (1,)), ((), ()))
TN_DIMS = (((0,), (0,)), ((), ()))


def _lane_mask(width, group, dtype):
    lane = lax.broadcasted_iota(jnp.int32, (1, LANE), 1)
    return jnp.where(lane // width == group, 1.0, 0.0).astype(dtype)


def _visible(tq, tk, off):
    row = lax.broadcasted_iota(jnp.int32, (tq, tk), 0)
    col = lax.broadcasted_iota(jnp.int32, (tq, tk), 1)
    return col <= row + off


def _attn_tiles(S, tq, tk):
    tk = tk if S % tk == 0 else 256
    return min(tq, tk), tk


def attn_fwd(q, kv, kpe, S):
    tq, tk = _attn_tiles(S, *FWD_TILES)
    nq = S // tq

    def body(q_ref, kv_ref, kp_ref, o_ref, lse_ref):
        for t in range(2):
            cols = slice(t * LANE, (t + 1) * LANE)
            for hh in range(2):
                def q_block(qi, _, t=t, hh=hh, cols=cols):
                    r0 = qi * tq
                    qcat = jnp.concatenate([q_ref[pl.ds(r0, tq), cols] * _lane_mask(NOPE, hh, BF),
                                            q_ref[pl.ds(r0, tq), 2 * LANE:3 * LANE] * _lane_mask(ROPE, 2 * t + hh, BF)],
                                           axis=1)
                    nfull = (qi * tq) // tk

                    def step(kj, carry, masked):
                        m, l, acc = carry
                        c0 = kj * tk
                        kc = jnp.concatenate([kv_ref[pl.ds(c0, tk), cols], kp_ref[pl.ds(c0, tk), :]], axis=1)
                        vt = kv_ref[pl.ds(c0, tk), (2 + t) * LANE:(3 + t) * LANE]
                        s = lax.dot_general(qcat, kc, NT_DIMS, preferred_element_type=F32) * SCALE
                        if masked:
                            s = jnp.where(_visible(tq, tk, qi * tq - nfull * tk), s, -jnp.inf)
                        m_new = jnp.maximum(m, jnp.max(s, axis=-1, keepdims=True))
                        p = jnp.exp(s - m_new)
                        alpha = jnp.exp(m - m_new)
                        l = alpha * l + jnp.sum(p, axis=-1, keepdims=True)
                        acc = alpha * acc + jnp.dot(p.astype(BF), vt, preferred_element_type=F32)
                        return m_new, l, acc

                    carry = (jnp.full((tq, 1), -jnp.inf, F32), jnp.zeros((tq, 1), F32), jnp.zeros((tq, LANE), F32))
                    for kj in range(nfull):
                        carry = step(kj, carry, False)
                    m, l, acc = step(nfull, carry, True)
                    mine = _lane_mask(NOPE, hh, F32)
                    if hh == 0:
                        o_ref[pl.ds(r0, tq), cols] = (acc / l) * mine
                        lse_ref[pl.ds(r0, tq), cols] = (m + jnp.log(l)) * mine
                    else:
                        o_ref[pl.ds(r0, tq), cols] += (acc / l) * mine
                        lse_ref[pl.ds(r0, tq), cols] += (m + jnp.log(l)) * mine
                    return 0

                for qi in range(nq):
                    q_block(qi, 0)

    return pl.pallas_call(
        body, name="attn_fwd", grid=(QUADS,),
        in_specs=[pl.BlockSpec((S, QW), lambda g: (0, g)), pl.BlockSpec((S, KVW), lambda g: (0, g)),
                  pl.BlockSpec((S, LANE), lambda g: (0, 0))],
        out_specs=[pl.BlockSpec((S, 2 * LANE), lambda g: (0, g))] * 2,
        out_shape=[jax.ShapeDtypeStruct((S, HEADS * VH), F32)] * 2,
        compiler_params=_cparams(),
    )(q, kv, kpe)


def attn_bwd(q, kv, kpe, o, lse, do, S):
    tq, tk = _attn_tiles(S, *BWD_TILES)
    nq = S // tq

    def body(q_ref, kv_ref, kp_ref, o_ref, lse_ref, do_ref, dq_ref, dkv_ref, dkp_ref, dq_acc, dk_acc, dv_acc):
        for t in range(2):
            cols = slice(t * LANE, (t + 1) * LANE)
            dk_acc[...] = jnp.zeros(dk_acc.shape, F32)
            dv_acc[...] = jnp.zeros(dv_acc.shape, F32)
            for hh in range(2):
                def q_block(qi, _, t=t, hh=hh, cols=cols):
                    r0 = qi * tq
                    mine = _lane_mask(NOPE, hh, F32)
                    qcat = jnp.concatenate([q_ref[pl.ds(r0, tq), cols] * _lane_mask(NOPE, hh, BF),
                                            q_ref[pl.ds(r0, tq), 2 * LANE:3 * LANE] * _lane_mask(ROPE, 2 * t + hh, BF)],
                                           axis=1)
                    dof = do_ref[pl.ds(r0, tq), cols] * mine
                    dob = dof.astype(BF)
                    delta = jnp.sum(dof * o_ref[pl.ds(r0, tq), cols], axis=-1, keepdims=True)
                    lse_h = lse_ref[pl.ds(r0, tq), cols][:, hh * NOPE:hh * NOPE + 1]
                    nfull = (qi * tq) // tk
                    dq_acc[...] = jnp.zeros(dq_acc.shape, F32)

                    def step(kj, _, masked):
                        c0 = kj * tk
                        kc = jnp.concatenate([kv_ref[pl.ds(c0, tk), cols], kp_ref[pl.ds(c0, tk), :]], axis=1)
                        vt = kv_ref[pl.ds(c0, tk), (2 + t) * LANE:(3 + t) * LANE]
                        s = lax.dot_general(qcat, kc, NT_DIMS, preferred_element_type=F32) * SCALE
                        if masked:
                            s = jnp.where(_visible(tq, tk, qi * tq - nfull * tk), s, -jnp.inf)
                        p = jnp.exp(s - lse_h)
                        dp = lax.dot_general(dob, vt, NT_DIMS, preferred_element_type=F32)
                        ds = (p * (dp - delta) * SCALE).astype(BF)
                        dv_acc[pl.ds(c0, tk), :] += lax.dot_general(p.astype(BF), dob, TN_DIMS,
                                                                    preferred_element_type=F32)
                        dk_acc[pl.ds(c0, tk), :] += lax.dot_general(ds, qcat, TN_DIMS, preferred_element_type=F32)
                        dq_acc[...] += jnp.dot(ds, kc, preferred_element_type=F32)
                        return 0

                    for kj in range(nfull):
                        step(kj, 0, False)
                    step(nfull, 0, True)
                    d = dq_acc[...]
                    pe = d[:, LANE:] * _lane_mask(ROPE, 2 * t + hh, F32)
                    if hh == 0:
                        dq_ref[pl.ds(r0, tq), cols] = d[:, :LANE] * mine
                    else:
                        dq_ref[pl.ds(r0, tq), cols] += d[:, :LANE] * mine
                    if t == 0 and hh == 0:
                        dq_ref[pl.ds(r0, tq), 2 * LANE:3 * LANE] = pe
                    else:
                        dq_ref[pl.ds(r0, tq), 2 * LANE:3 * LANE] += pe
                    return 0

                for qi in range(nq):
                    q_block(qi, 0)
            dkv_ref[:, t * LANE:(t + 1) * LANE] = dk_acc[:, :LANE].astype(BF)
            dkv_ref[:, (2 + t) * LANE:(3 + t) * LANE] = dv_acc[...].astype(BF)
            if t == 0:
                dkp_ref[...] = dk_acc[:, LANE:]
            else:
                dkp_ref[...] += dk_acc[:, LANE:]

    qspec = pl.BlockSpec((S, QW), lambda g: (0, g))
    kvspec = pl.BlockSpec((S, KVW), lambda g: (0, g))
    ospec = pl.BlockSpec((S, 2 * LANE), lambda g: (0, g))
    return pl.pallas_call(
        body, name="attn_bwd", grid=(QUADS,),
        in_specs=[qspec, kvspec, pl.BlockSpec((S, LANE), lambda g: (0, 0)), ospec, ospec, ospec],
        out_specs=[qspec, kvspec, pl.BlockSpec((S, LANE), lambda g: (0, g))],
        out_shape=[jax.ShapeDtypeStruct((S, HEADS * (NOPE + ROPE)), F32), jax.ShapeDtypeStruct((S, HEADS * (NOPE + VH)), BF),
                   jax.ShapeDtypeStruct((S, HEADS * ROPE), F32)],
        scratch_shapes=[pltpu.VMEM((tq, 2 * LANE), F32), pltpu.VMEM((S, 2 * LANE), F32), pltpu.VMEM((S, LANE), F32)],
        compiler_params=_cparams(),
    )(q, kv, kpe, o, lse, do)


def exchange(name, gathers, a2as):
    n_g, n = len(gathers), len(gathers) + len(a2as)

    def body(*refs):
        ins, outs = refs[:n], refs[n:2 * n]
        send_sems, recv_sems, loc_sems = refs[2 * n:]
        x, y, c = lax.axis_index("x"), lax.axis_index("y"), lax.axis_index("c")
        me = 4 * x + 2 * y + c

        def peer(k):
            px = 1 - x if k & 4 else x
            py = 1 - y if k & 2 else y
            pc = 1 - c if k & 1 else c
            return (px, py, pc), 4 * px + 2 * py + pc

        def remote(a, k):
            pid, pflat = peer(k)
            src = ins[a] if a < n_g else ins[a].at[pflat]
            return pltpu.make_async_remote_copy(
                src_ref=src, dst_ref=outs[a].at[me], send_sem=send_sems.at[a, k - 1], recv_sem=recv_sems.at[a, k - 1],
                device_id=pid, device_id_type=MESH)

        def arrival(a, k):
            pid, pflat = peer(k)
            src = ins[a] if a < n_g else ins[a].at[pflat]
            return pltpu.make_async_remote_copy(
                src_ref=src, dst_ref=outs[a].at[pflat], send_sem=send_sems.at[a, k - 1], recv_sem=recv_sems.at[a, k - 1],
                device_id=pid, device_id_type=MESH)

        local = []
        for a in range(n):
            own = ins[a] if a < n_g else ins[a].at[me]
            cp = pltpu.make_async_copy(own, outs[a].at[me], loc_sems.at[a])
            cp.start()
            local.append(cp)
        sent = []
        for k in (1, 2, 4, 3, 5, 6, 7):
            for a in range(n):
                cp = remote(a, k)
                cp.start()
                sent.append(cp)
        for k in range(1, 8):
            for a in range(n):
                arrival(a, k).wait_recv()
        for cp in sent:
            cp.wait_send()
        for cp in local:
            cp.wait()

    out_shape = [jax.ShapeDtypeStruct((NDEV,) + g.shape, g.dtype) for g in gathers]
    out_shape += [jax.ShapeDtypeStruct(a.shape, a.dtype) for a in a2as]
    any_spec = pl.BlockSpec(memory_space=pl.ANY)
    return pl.pallas_call(
        body, name=name, in_specs=[any_spec] * n, out_specs=[any_spec] * n, out_shape=out_shape,
        scratch_shapes=[pltpu.SemaphoreType.DMA((n, NDEV - 1)), pltpu.SemaphoreType.DMA((n, NDEV - 1)),
                        pltpu.SemaphoreType.DMA((n,))],
    )(*gathers, *a2as)


def gather_two_level(name, block, dep):
    def body(x_ref, _, out_ref, stage, send_sems, recv_sems, loc_sem):
        x, y, c = lax.axis_index("x"), lax.axis_index("y"), lax.axis_index("c")
        me, sibling = (x, y, c), (x, y, 1 - c)
        chips = [(1 - x, y), (x, 1 - y), (1 - x, 1 - y)]

        def slot(px, py, pc):
            return out_ref.at[4 * px + 2 * py + pc]

        def copy(k, owner, to, src=None):
            return pltpu.make_async_remote_copy(
                src_ref=slot(*owner) if src is None else src, dst_ref=slot(*owner), send_sem=send_sems.at[k],
                recv_sem=recv_sems.at[k], device_id=to, device_id_type=MESH)

        load = pltpu.make_async_copy(x_ref, stage, loc_sem)
        load.start()
        first = [copy(0, me, sibling, src=x_ref)] + [copy(1 + j, me, (*chip, c), src=x_ref) for j, chip in enumerate(chips)]
        for cp in first:
            cp.start()
        load.wait()
        store = pltpu.make_async_copy(stage, slot(*me), loc_sem)
        store.start()
        passed = [copy(4 + j, (*chip, c), sibling) for j, chip in enumerate(chips)]
        for j, chip in enumerate(chips):
            copy(1 + j, (*chip, c), me).wait_recv()
            passed[j].start()
        copy(0, sibling, me).wait_recv()
        for j, chip in enumerate(chips):
            copy(4 + j, (*chip, 1 - c), me).wait_recv()
        for cp in first + passed:
            cp.wait_send()
        store.wait()

    return pl.pallas_call(
        body, name=name, in_specs=[pl.BlockSpec(memory_space=pl.ANY)] * 2, out_specs=pl.BlockSpec(memory_space=pl.ANY),
        out_shape=jax.ShapeDtypeStruct((NDEV,) + block.shape, block.dtype),
        scratch_shapes=[pltpu.VMEM(block.shape, block.dtype), pltpu.SemaphoreType.DMA((NDEV - 1,)),
                        pltpu.SemaphoreType.DMA((NDEV - 1,)), pltpu.SemaphoreType.DMA],
        compiler_params=_cparams(),
    )(block, dep)


def _peer(k, x, y, c):
    px = 1 - x if k & 4 else x
    py = 1 - y if k & 2 else y
    pc = 1 - c if k & 1 else c
    return (px, py, pc), 4 * px + 2 * py + pc


PEER_ORDER = (1, 2, 4, 3, 5, 6, 7)
HBM_SPEC = pl.BlockSpec(memory_space=pltpu.HBM)
SEM_SPEC = pl.BlockSpec(memory_space=pltpu.SEMAPHORE)
ANY_SPEC = pl.BlockSpec(memory_space=pl.ANY)


def _split_copies(ins, lands, n_g, send_sems, recv_sems):
    x, y, c = lax.axis_index("x"), lax.axis_index("y"), lax.axis_index("c")
    me = 4 * x + 2 * y + c

    def outgoing(a, k):
        pid, pflat = _peer(k, x, y, c)
        src = ins[a] if a < n_g else ins[a].at[pflat]
        return pltpu.make_async_remote_copy(
            src_ref=src, dst_ref=lands[a].at[me], send_sem=send_sems.at[a * (NDEV - 1) + k - 1],
            recv_sem=recv_sems.at[a * (NDEV - 1) + k - 1],
            device_id=pid, device_id_type=MESH)

    def arrival(a, k):
        pid, pflat = _peer(k, x, y, c)
        src = ins[a] if a < n_g else ins[a].at[pflat]
        return pltpu.make_async_remote_copy(
            src_ref=src, dst_ref=lands[a].at[pflat], send_sem=send_sems.at[a * (NDEV - 1) + k - 1],
            recv_sem=recv_sems.at[a * (NDEV - 1) + k - 1],
            device_id=pid, device_id_type=MESH)

    return outgoing, arrival


def exchange_begin(name, srcs, n_g, dep):
    n = len(srcs)
    land_shapes = [((NDEV,) + s.shape) if a < n_g else s.shape for a, s in enumerate(srcs)]

    def own_body(*refs):
        ins, outs = refs[:n], refs[n + 1:2 * n + 1]
        stage, sems = refs[2 * n + 1:3 * n + 1], refs[-1]
        me = 4 * lax.axis_index("x") + 2 * lax.axis_index("y") + lax.axis_index("c")
        cps = [pltpu.make_async_copy(ins[a] if a < n_g else ins[a].at[me], stage[a], sems.at[a]) for a in range(n)]
        for cp in cps:
            cp.start()
        for cp in cps:
            cp.wait()
        cps = [pltpu.make_async_copy(stage[a], outs[a].at[me], sems.at[a]) for a in range(n)]
        for cp in cps:
            cp.start()
        for cp in cps:
            cp.wait()

    lands = pl.pallas_call(
        own_body, name=name + "_own", in_specs=[ANY_SPEC] * (n + 1), out_specs=[ANY_SPEC] * n,
        out_shape=[jax.ShapeDtypeStruct(sh, s.dtype) for sh, s in zip(land_shapes, srcs)],
        scratch_shapes=[pltpu.VMEM(sh[1:], s.dtype) for sh, s in zip(land_shapes, srcs)] + [pltpu.SemaphoreType.DMA((n,))],
        compiler_params=_cparams(),
    )(*srcs, dep)

    def start_body(*refs):
        ins, lz = refs[:n], refs[n:2 * n]
        send_sems, recv_sems, token = refs[2 * n], refs[2 * n + 1], refs[-1]
        outgoing, _ = _split_copies(ins, lz, n_g, send_sems, recv_sems)
        for k in PEER_ORDER:
            for a in range(n):
                outgoing(a, k).start()
        token[...] = jnp.zeros(token.shape, F32)

    hbm = lambda t: pltpu.HBM(t.shape, t.dtype)
    res = pl.pallas_call(
        start_body, name=name + "_start",
        out_shape=(pltpu.SemaphoreType.DMA((n * (NDEV - 1),)), pltpu.SemaphoreType.DMA((n * (NDEV - 1),)),
                   *[hbm(s) for s in srcs], *[hbm(t) for t in lands], jax.ShapeDtypeStruct((8, LANE), F32)),
        in_specs=[HBM_SPEC] * (2 * n),
        out_specs=(SEM_SPEC, SEM_SPEC, *[HBM_SPEC] * (2 * n), pl.BlockSpec(memory_space=pltpu.VMEM)),
        input_output_aliases={i: 2 + i for i in range(2 * n)},
        compiler_params=pltpu.CompilerParams(has_side_effects=pltpu.SideEffectType.DATAFLOW_SIDE_EFFECTING),
    )(*[pltpu.with_memory_space_constraint(t, pltpu.HBM) for t in list(srcs) + list(lands)])
    return (name, n, n_g, res[:-1]), res[-1]


def exchange_end(handle, after):
    name, n, n_g, (send_sems, recv_sems, *bufs) = handle

    def wait_body(*refs):
        ins, lz = refs[:n], refs[n:2 * n]
        ss, rs = refs[2 * n], refs[2 * n + 1]
        outgoing, arrival = _split_copies(ins, lz, n_g, ss, rs)
        for k in range(1, NDEV):
            for a in range(n):
                arrival(a, k).wait_recv()
        for k in range(1, NDEV):
            for a in range(n):
                outgoing(a, k).wait_send()

    res = pl.pallas_call(
        wait_body, name=name + "_wait", out_shape=tuple(pltpu.HBM(t.shape, t.dtype) for t in bufs),
        in_specs=[HBM_SPEC] * (2 * n) + [SEM_SPEC, SEM_SPEC, ANY_SPEC], out_specs=[HBM_SPEC] * (2 * n),
        input_output_aliases={i: i for i in range(2 * n)},
        compiler_params=pltpu.CompilerParams(has_side_effects=pltpu.SideEffectType.DATAFLOW_SIDE_EFFECTING),
    )(*bufs, send_sems, recv_sems, after)
    return list(res[n:])


def _pick_rows(R, mult, cap):
    best = None
    for n in range(1, R + 1):
        if R % n == 0 and (R // n) % mult == 0 and R // n <= cap:
            best = R // n
            break
    assert best is not None, (R, mult, cap)
    return best


def sum_slots(name, x):
    _, R, _ = x.shape
    tr = _pick_rows(R, 16, 2304)

    def body(x_ref, o_ref):
        acc = x_ref[0].astype(F32)
        for d in range(1, NDEV):
            acc = acc + x_ref[d].astype(F32)
        o_ref[...] = acc

    return pl.pallas_call(
        body, name=name, grid=(R // tr,),
        in_specs=[pl.BlockSpec((NDEV, tr, LANE), lambda i: (0, i, 0))],
        out_specs=pl.BlockSpec((tr, LANE), lambda i: (i, 0)),
        out_shape=jax.ShapeDtypeStruct((R, LANE), F32), compiler_params=_cparams(),
    )(x)


def adamw(name, w, g, m, v):
    L, R, C = w.shape
    tr = _pick_rows(R, 8, 256) if R % 8 == 0 else R

    def body(w_ref, g_ref, m_ref, v_ref, d_ref, nm_ref, nv_ref):
        gg = g_ref[...]
        nm = B1 * m_ref[...] + (1.0 - B1) * gg
        nv = B2 * v_ref[...] + (1.0 - B2) * jnp.square(gg)
        m_hat = nm / (1.0 - B1 ** STEP)
        v_hat = nv / (1.0 - B2 ** STEP)
        d_ref[...] = -LR * (m_hat / (jnp.sqrt(v_hat) + EPS) + WD * w_ref[...])
        nm_ref[...] = nm
        nv_ref[...] = nv

    blk = pl.BlockSpec((1, tr, C), lambda l, i: (l, i, 0))
    shp = jax.ShapeDtypeStruct(w.shape, F32)
    return pl.pallas_call(
        body, name=name, grid=(L, R // tr), in_specs=[blk] * 4, out_specs=[blk] * 3, out_shape=[shp] * 3,
        compiler_params=_cparams(),
    )(w, g, m, v)


IN_SHARD = D_IN // NDEV
UQ_SHARD = HEADS * (NOPE + ROPE) // NDEV
W_IN_PAD = 1024
ROW_A, ROW_B, ROW_C, ROW_UKV, ROW_UQ, MISC_ROWS = 0, 512, 1024, 1536, 1792, 2176


def _in_perm_index():
    ar = np.arange
    z = lambda n: np.full((n,), -1, np.int64)
    mix = lambda lo1, lo2: np.concatenate([ar(lo + LANE * j, lo + LANE * (j + 1)) for j in range(CW // LANE)
                                           for lo in (lo1, lo2)])
    return np.concatenate([ar(4768, 7840), mix(0, 512), ar(1024, 1536), mix(1536, 2560), ar(4256, 4768), ar(2048, 2560),
                           ar(3072, 3584), ar(3968, 4224), ar(4224, 4256), z(OFF_Q - OFF_KR - ROPE), ar(3584, 3968),
                           z(NP - OFF_Q - QL)])


def _head_perm_index(a, b):
    parts = []
    for g in range(QUADS):
        h = np.arange(4 * g, 4 * g + 4)[:, None] * (a + b)
        parts += [(h + np.arange(a)[None]).reshape(-1), (h + a + np.arange(b)[None]).reshape(-1)]
    return np.concatenate(parts)


def _inverse(perm, n):
    inv = np.full((n,), -1, np.int64)
    inv[perm[perm >= 0]] = np.nonzero(perm >= 0)[0]
    return inv


IN_PERM = _in_perm_index()
UQ_PERM = _head_perm_index(NOPE, ROPE)
UKV_PERM = _head_perm_index(NOPE, VH)


def _to_gathered(perm, shard, pad):
    return np.where(perm >= 0, (perm // shard) * pad + perm % shard, -1)


def _from_full(inv, shard, pad):
    j, i = np.divmod(np.arange(NDEV * pad), pad)
    return np.where(i < shard, inv[np.minimum(j * shard + i, inv.shape[0] - 1)], -1)


def col_gather(name, srcs, out_shapes, jobs, deps=()):
    ns, nj, nd, no = len(srcs), len(jobs), len(deps), len(out_shapes)
    tables = [jnp.asarray(np.asarray(job[5], np.int32)[None, :]) for job in jobs]

    def view(ref, col0, width, r0, rc):
        n = ref.shape[-1]
        if len(ref.shape) == 3:
            return ref.at[col0 // n, pl.ds(r0, rc), pl.ds(col0 % n, width)]
        return ref.at[pl.ds(r0, rc), pl.ds(col0, width)]

    def slabs(shape):
        if len(shape) == 3:
            return [((d,), d * shape[2], (d + 1) * shape[2]) for d in range(shape[0])]
        w = 1024 if shape[1] > 1024 and shape[1] % 1024 == 0 else shape[1]
        return [((slice(None), pl.ds(c, w)), c, c + w) for c in range(0, shape[1], w)]

    src_slabs = [slabs(s.shape) for s in srcs]
    out_slabs = [slabs(sh) for sh in out_shapes]
    work, first_use, last_touch = [], {}, {}
    for ji, (si, srow, oi, orow, nrows, tgt) in enumerate(jobs):
        tgt = np.asarray(tgt)
        tw = 256 if out_shapes[oi][-1] % 256 == 0 else LANE
        sw = 256 if srcs[si].shape[-1] % 256 == 0 else LANE
        for t in range(tgt.shape[0] // tw):
            tt = tgt[t * tw:(t + 1) * tw]
            tiles = sorted(set((tt[tt >= 0] // sw).tolist()))
            straight = bool(tiles) and tt[0] >= 0 and tt[0] % LANE == 0 and np.array_equal(tt, tt[0] + np.arange(tw))
            cols = [(int(tt[0]) + k * LANE, LANE) for k in range(tw // LANE)] if straight else [(s * sw, sw) for s in tiles]
            need = sorted({(si, k) for c0, _ in cols for k, (_, lo, hi) in enumerate(src_slabs[si]) if lo <= c0 < hi})
            touch = [(oi, k) for k, (_, lo, hi) in enumerate(out_slabs[oi]) if lo <= t * tw < hi][0]
            for key in need:
                first_use.setdefault(key, len(work))
            last_touch[touch] = len(work)
            work.append((ji, t, tw, sw, tiles, straight, need, touch))
    in_order = sorted(first_use, key=first_use.get)
    in_sem = {key: i for i, key in enumerate(in_order)}
    out_keys = sorted(last_touch)
    out_sem = {key: i for i, key in enumerate(out_keys)}

    def body(*refs):
        src_hbm, tab_refs = refs[:ns], refs[ns:ns + nj]
        out_hbm = refs[ns + nj + nd:ns + nj + nd + no]
        scratch = refs[ns + nj + nd + no:]
        src_refs, out_refs, in_sems, out_sems = scratch[:ns], scratch[ns:ns + no], scratch[-2], scratch[-1]
        loads = {}
        for key in in_order:
            si, k = key
            idx = src_slabs[si][k][0]
            loads[key] = pltpu.make_async_copy(src_hbm[si].at[idx], src_refs[si].at[idx], in_sems.at[in_sem[key]])
            loads[key].start()
        arrived, stores = set(), []
        for wi, (ji, t, tw, sw, tiles, straight, need, touch) in enumerate(work):
            si, srow, oi, orow, nrows, tgt = jobs[ji]
            sref, oref = src_refs[si], out_refs[oi]
            rc = nrows if nrows <= 1024 else 1024
            for key in need:
                if key not in arrived:
                    loads[key].wait()
                    arrived.add(key)
            onehots = []
            if tiles and not straight:
                want = tab_refs[ji][:, t * tw:(t + 1) * tw]
                row = lax.broadcasted_iota(jnp.int32, (sw, tw), 0)
                onehots = [jnp.where(want == row + s * sw, 1.0, 0.0).astype(BF) for s in tiles]
            first = int(np.asarray(tgt)[t * tw])

            def chunk(ci, _, t=t, tw=tw, sw=sw, tiles=tiles, straight=straight, onehots=onehots, first=first,
                      sref=sref, oref=oref, srow=srow, orow=orow, rc=rc):
                r0 = ci * rc
                ro = pl.multiple_of(orow + r0, LANE)
                rs = pl.multiple_of(srow + r0, LANE)
                if not tiles:
                    view(oref, t * tw, tw, ro, rc)[...] = jnp.zeros((rc, tw), BF)
                elif straight:
                    for k in range(tw // LANE):
                        view(oref, t * tw + k * LANE, LANE, ro, rc)[...] = view(sref, first + k * LANE, LANE, rs, rc)[...]
                else:
                    acc = None
                    for s, oh in zip(tiles, onehots):
                        p = jnp.dot(view(sref, s * sw, sw, rs, rc)[...], oh, preferred_element_type=F32)
                        acc = p if acc is None else acc + p
                    view(oref, t * tw, tw, ro, rc)[...] = acc.astype(BF)
                return 0

            lax.fori_loop(0, nrows // rc, chunk, 0)
            if last_touch[touch] == wi:
                idx = out_slabs[touch[0]][touch[1]][0]
                cp = pltpu.make_async_copy(out_refs[touch[0]].at[idx], out_hbm[touch[0]].at[idx], out_sems.at[out_sem[touch]])
                cp.start()
                stores.append(cp)
        for cp in stores:
            cp.wait()

    return pl.pallas_call(
        body, name=name, in_specs=[ANY_SPEC] * ns + [pl.BlockSpec(memory_space=pltpu.VMEM)] * nj + [ANY_SPEC] * nd,
        out_specs=[ANY_SPEC] * no, out_shape=[jax.ShapeDtypeStruct(s, BF) for s in out_shapes],
        scratch_shapes=[pltpu.VMEM(s.shape, BF) for s in srcs] + [pltpu.VMEM(s, BF) for s in out_shapes]
        + [pltpu.SemaphoreType.DMA((len(in_order),)), pltpu.SemaphoreType.DMA((len(out_keys),))],
        compiler_params=_cparams(),
    )(*srcs, *tables, *deps)


def sum_adamw(name, recvs, w, m, v, lo=0, prev=None, row0=0):
    _, R, C = w.shape
    L = len(recvs)
    CP = recvs[0].shape[-1]
    tr = _pick_rows(R, 16, 128)
    n_prev = 0 if prev is None else 4

    def body(*refs):
        r_refs = refs[:L]
        w_ref, m_ref, v_ref = refs[L:L + 3]
        g_ref, d_ref, nm_ref, nv_ref, gsum = refs[L + 3 + n_prev:]
        layer = pl.program_id(0)
        for k in range(L):
            def total(k=k):
                acc = r_refs[k][0].astype(F32)
                for d in range(1, NDEV):
                    acc = acc + r_refs[k][d].astype(F32)
                gsum[...] = acc
            pl.when(layer == k)(total)
        gg = gsum[:, 0:C]
        nm = B1 * m_ref[...] + (1.0 - B1) * gg
        nv = B2 * v_ref[...] + (1.0 - B2) * jnp.square(gg)
        m_hat = nm / (1.0 - B1 ** STEP)
        v_hat = nv / (1.0 - B2 ** STEP)
        g_ref[...] = gg
        d_ref[...] = -LR * (m_hat / (jnp.sqrt(v_hat) + EPS) + WD * w_ref[...])
        nm_ref[...] = nm
        nv_ref[...] = nv

    assert row0 % tr == 0
    r_specs = [pl.BlockSpec((NDEV, tr, CP),
                            functools.partial(lambda l, i, k: (0, row0 // tr + jnp.where(l == k, i, 0), 0), k=k))
               for k in range(L)]
    blk = pl.BlockSpec((None, tr, C), lambda l, i: (l + lo, i, 0))
    shp = jax.ShapeDtypeStruct(w.shape, F32)
    return pl.pallas_call(
        body, name=name, grid=(L, R // tr), in_specs=r_specs + [blk] * 3 + [ANY_SPEC] * n_prev, out_specs=[blk] * 4,
        out_shape=[shp] * 4, input_output_aliases={L + 3 + i: i for i in range(n_prev)},
        scratch_shapes=[pltpu.VMEM((tr, CP), F32)], compiler_params=_cparams(),
    )(*recvs, w, m, v, *(prev or ()))


ALPHA = 8.0 ** 0.25
T_WIDE, T_NARROW = 512, 1024


def _rope_fn(sign):
    def fn(x, cos, sin):
        W = x.shape[-1]
        lane = lax.broadcasted_iota(jnp.int32, x.shape, 1)
        first_half = (lane % ROPE) < (ROPE // 2)
        rot = jnp.where(first_half, -pltpu.roll(x, W - ROPE // 2, 1), pltpu.roll(x, ROPE // 2, 1))
        return x * cos + sign * rot * sin
    return fn


def _modulate(xv, a):
    return xv * (1.0 + a[1:2, :]) + a[0:1, :]


def layer_fwd(x, ada3, W, tabs, S, u=None, ada_next=None, target=None):
    cos, sin = tabs
    T = T_NARROW
    if u is None:
        u = rowwise("modulate", _modulate, S, T, [(x, D_MODEL, 0)], [ada3], [(D_MODEL, BF)])[0]
    proj = mm(u, W["in"], name="mm_proj", tm=1024, tn=1024, out_dtype=BF)
    W = {**W, **W["late"](proj)}

    ca = conv_fwd("conv_a_fwd", proj, OFF_A, W["conv_a"], 31, "glu", S, CW)

    def a_post(c, ag, vec):
        n, _ = _ln_stats(c + vec[0:1, :])
        return _silu(n * vec[1:2, :] + vec[2:3, :]) * _silu(ag)

    h_a = rowwise("mix_a_post", a_post, S, T, [(ca, CW, 0), (proj, CW, OFF_AG)], [W["vec_a"]], [(CW, BF)])[0]
    y_a = mm(h_a, W["a_out"], name="mm_branch_out", out_dtype=BF)

    cb = conv_fwd("conv_b_fwd", proj, OFF_B, W["conv_b"], 3, "mul", S, CW)
    h_b = rowwise("mix_b_post", lambda c, gb, bg: gb * c * _silu(bg), S, T,
                  [(cb, CW, 0), (proj, CW, OFF_GB), (proj, CW, OFF_BG)], [], [(CW, BF)])[0]
    y_b = mm(h_b, W["b_out"], name="mm_branch_out", out_dtype=BF)

    def rms2(ql, kvl, gq, gkv):
        rq = lax.rsqrt(jnp.mean(ql * ql, axis=-1, keepdims=True) + RMS_EPS)
        rk = lax.rsqrt(jnp.mean(kvl * kvl, axis=-1, keepdims=True) + RMS_EPS)
        return ql * rq * gq, kvl * rk * gkv

    qn, kvn = rowwise("rms_fwd", rms2, S, T, [(proj, QL, OFF_Q), (proj, KVL, OFF_KV)], [W["gq"], W["gkv"]],
                      [(QL, BF), (KVL, BF)])
    q = mm(qn, W["uq"], name="mm_q")
    kv = mm(kvn, W["ukv"], name="mm_kv", out_dtype=BF)
    rope = _rope_fn(1.0)

    def rope_fwd(qv, kr, c1, s1):
        parts = []
        for g in range(QUADS):
            parts.append(qv[:, g * QW:g * QW + 2 * LANE].astype(BF))
            parts.append(rope(qv[:, g * QW + 2 * LANE:(g + 1) * QW], c1, s1).astype(BF))
        kp = rope(kr, c1, s1)
        kp = kp + pltpu.roll(kp, ROPE, 1) + pltpu.roll(kp, 2 * ROPE, 1) + pltpu.roll(kp, 3 * ROPE, 1)
        return jnp.concatenate(parts, axis=1), kp

    q_b, kpe = rowwise("rope_fwd", rope_fwd, S, T,
                       [(q, HEADS * (NOPE + ROPE), 0), (proj, LANE, OFF_KR), (cos, LANE, 0), (sin, LANE, 0)], [],
                       [(HEADS * (NOPE + ROPE), BF), (LANE, BF)])
    o, lse = attn_fwd(q_b, kv, kpe, S)
    h_c = rowwise("mix_c_post", lambda ov, cg: ov * _silu(cg), S, T, [(o, CW, 0), (proj, CW, OFF_CG)], [],
                  [(CW, BF)])[0]
    y_c = mm(h_c, W["c_out"], name="mm_branch_out", out_dtype=BF)

    def merge(la, lb, lc, ya, yb, yc):
        return _sigmoid(la) * ya + _sigmoid(lb) * yb + _sigmoid(lc) * yc

    m = rowwise("merge_fwd", merge, S, T_WIDE,
                [(proj, D_MODEL, 0), (proj, D_MODEL, 1024), (proj, D_MODEL, 2048), (y_a, D_MODEL, 0),
                 (y_b, D_MODEL, 0), (y_c, D_MODEL, 0)], [], [(D_MODEL, BF)])[0]
    out = mm(m, W["o"], name="mm_out")

    def ln_fwd(xv, ov, a, lnv, *nxt):
        n, _ = _ln_stats(ALPHA * xv + a[2:3, :] * ov)
        y = n * lnv[0:1, :] + lnv[1:2, :]
        return (y, _modulate(y, nxt[0])) if nxt else y

    saved = dict(x=x, u=u, proj=proj, ca=ca, cb=cb, h_a=h_a, h_b=h_b, h_c=h_c, y_a=y_a, y_b=y_b, y_c=y_c, qn=qn,
                 kvn=kvn, q_b=q_b, kv=kv, kpe=kpe, lse=lse, o=o, m=m, out=out)
    if target is not None:
        def ln_loss(xv, ov, tv, a, lnv):
            e = ln_fwd(xv, ov, a, lnv) - tv
            return e * (1.0 / D_MODEL), _colsum(e * e)

        dy, sq = rowwise("ln_fwd_loss", ln_loss, S, T_WIDE, [(x, D_MODEL, 0), (out, D_MODEL, 0), (target, D_MODEL, 0)],
                         [ada3, W["lnv"]], [(D_MODEL, F32)], [D_MODEL])
        return None, saved, W, (dy, sq)
    res = rowwise("ln_fwd", ln_fwd, S, T_WIDE, [(x, D_MODEL, 0), (out, D_MODEL, 0)],
                  [ada3, W["lnv"]] + ([] if ada_next is None else [ada_next]),
                  [(D_MODEL, F32)] + ([] if ada_next is None else [(D_MODEL, BF)]))
    return res[0], saved, W, (res[1] if ada_next is not None else None)


def layer_bwd(dxn, sv, ada3, W, tabs, S, before_in=None):
    cos, sin = tabs
    T = T_NARROW
    x, proj = sv["x"], sv["proj"]
    G = {}

    def ln_bwd(xv, ov, dy, a, lnv):
        gate = a[2:3, :]
        n, rstd = _ln_stats(ALPHA * xv + gate * ov)
        dr = _ln_bwd(dy * lnv[0:1, :], n, rstd)
        return ALPHA * dr, gate * dr, _colsum(dy * n), _colsum(dy), _colsum(dr * ov)

    dres, d_out, G["ln_g"], G["ln_b"], d_gate = rowwise(
        "ln_bwd", ln_bwd, S, T_WIDE, [(x, D_MODEL, 0), (sv["out"], D_MODEL, 0), (dxn, D_MODEL, 0)], [ada3, W["lnv"]],
        [(D_MODEL, F32), (D_MODEL, BF)], [D_MODEL] * 3)
    dm, G["w_o"] = mm_bwd_pair("mm_bwd_out", d_out, W["o"], sv["m"])

    def merge_bwd(dmv, la, lb, lc, ya, yb, yc):
        outs, dls = [], []
        for lg, yv in ((la, ya), (lb, yb), (lc, yc)):
            s = _sigmoid(lg)
            outs.append(dmv * s)
            dls.append((dmv * yv * s * (1.0 - s)).astype(BF))
        return (jnp.concatenate(dls, axis=1),) + tuple(outs)

    d_proj, dy_a, dy_b, dy_c = rowwise(
        "merge_bwd", merge_bwd, S, T_WIDE,
        [(dm, D_MODEL, 0), (proj, D_MODEL, 0), (proj, D_MODEL, 1024), (proj, D_MODEL, 2048), (sv["y_a"], D_MODEL, 0),
         (sv["y_b"], D_MODEL, 0), (sv["y_c"], D_MODEL, 0)], [], [(3 * D_MODEL, BF)] + [(D_MODEL, BF)] * 3,
        into=(None, NP, OFF_M))

    dh = {}
    for br, dy in (("a", dy_a), ("b", dy_b), ("c", dy_c)):
        dh[br], G["w_%s_out" % br] = mm_bwd_pair("mm_bwd_branch", dy, W[br + "_out"], sv["h_" + br])

    def a_post_bwd(c, ag, dhv, vec):
        n, rstd = _ln_stats(c + vec[0:1, :])
        z = n * vec[1:2, :] + vec[2:3, :]
        d_ag = dhv * _silu(z) * _dsilu(ag)
        dz = dhv * _silu(ag) * _dsilu(z)
        dc = _ln_bwd(dz * vec[1:2, :], n, rstd)
        return d_ag, dc, _colsum(dc), _colsum(dz * n), _colsum(dz)

    d_proj, dca, G["conv_a_b"], G["ln_a_g"], G["ln_a_b"] = rowwise(
        "mix_a_post_bwd", a_post_bwd, S, T, [(sv["ca"], CW, 0), (proj, CW, OFF_AG), (dh["a"], CW, 0)], [W["vec_a"]],
        [(CW, BF), (CW, F32)], [CW] * 3, into=(d_proj, NP, OFF_AG))
    d_proj, G["conv_a_w"] = conv_bwd("conv_a_bwd", proj, OFF_A, dca, W["conv_a"], 31, "glu", S, CW, d_proj)

    def b_post_bwd(c, gb, bg, dhv):
        sg = _silu(bg)
        d_gb_bg = jnp.concatenate([(dhv * sg * c).astype(BF), (dhv * gb * c * _dsilu(bg)).astype(BF)], axis=1)
        return d_gb_bg, dhv * sg * gb

    d_proj, dcb = rowwise("mix_b_post_bwd", b_post_bwd, S, T,
                          [(sv["cb"], CW, 0), (proj, CW, OFF_GB), (proj, CW, OFF_BG), (dh["b"], CW, 0)], [],
                          [(2 * CW, BF), (CW, F32)], into=(d_proj, NP, OFF_GB))
    d_proj, G["conv_b_w"] = conv_bwd("conv_b_bwd", proj, OFF_B, dcb, W["conv_b"], 3, "mul", S, CW, d_proj)

    d_proj, d_o = rowwise("mix_c_post_bwd", lambda ov, cg, dhv: (dhv * ov * _dsilu(cg), dhv * _silu(cg)), S, T,
                          [(sv["o"], CW, 0), (proj, CW, OFF_CG), (dh["c"], CW, 0)], [], [(CW, BF), (CW, F32)],
                          into=(d_proj, NP, OFF_CG))
    dq, d_kv, dkp_heads = attn_bwd(sv["q_b"], sv["kv"], sv["kpe"], sv["o"], sv["lse"], d_o, S)
    ropeT = _rope_fn(-1.0)

    def rope_bwd(dqv, dkp, c1, s1):
        parts = []
        for g in range(QUADS):
            parts.append(dqv[:, g * QW:g * QW + 2 * LANE].astype(BF))
            parts.append(ropeT(dqv[:, g * QW + 2 * LANE:(g + 1) * QW], c1, s1).astype(BF))
        f = dkp[:, :LANE] + dkp[:, LANE:]
        f = f + pltpu.roll(f, 64, 1)
        f = f + pltpu.roll(f, 32, 1)
        lane = lax.broadcasted_iota(jnp.int32, f.shape, 1)
        return jnp.concatenate(parts, axis=1), jnp.where(lane < ROPE, ropeT(f, c1, s1), 0.0)

    d_q, dk_pe = rowwise("rope_bwd", rope_bwd, S, T,
                         [(dq, HEADS * (NOPE + ROPE), 0), (dkp_heads, HEADS * ROPE, 0), (cos, LANE, 0), (sin, LANE, 0)],
                         [], [(HEADS * (NOPE + ROPE), BF), (LANE, BF)])
    d_qn, G["w_uq"] = mm_bwd_pair("mm_bwd_q", d_q, W["uq"], sv["qn"])
    d_kvn, G["w_ukv"] = mm_bwd_pair("mm_bwd_kv", d_kv, W["ukv"], sv["kvn"])

    def rms_bwd(ql, kvl, dqn, dkn, dkp, gq, gkv):
        res = []
        for xv, dy, g in ((ql, dqn, gq), (kvl, dkn, gkv)):
            r = lax.rsqrt(jnp.mean(xv * xv, axis=-1, keepdims=True) + RMS_EPS)
            dxh = dy * g
            res.append(((r * (dxh - xv * (r * r) * jnp.mean(dxh * xv, axis=-1, keepdims=True))).astype(BF),
                        _colsum(dy * xv * r)))
        pad = jnp.zeros((ql.shape[0], LANE), BF)
        return jnp.concatenate([res[1][0], dkp, pad, res[0][0], pad], axis=1), res[0][1], res[1][1]

    d_proj, G["q_norm_g"], G["kv_norm_g"] = rowwise(
        "rms_bwd", rms_bwd, S, T,
        [(proj, QL, OFF_Q), (proj, KVL, OFF_KV), (d_qn, QL, 0), (d_kvn, KVL, 0), (dk_pe, LANE, 0)],
        [W["gq"], W["gkv"]], [(NP - OFF_KV, BF)], [QL, KVL], into=(d_proj, NP, OFF_KV))
    deps = before_in(G) if before_in is not None else ()
    du = mm(d_proj, W["in"], name="mm_du", trans_b=True, tm=1024, tk=2048, deps=deps)
    G["w_in"] = mm(sv["u"], d_proj, name="mm_gw_in", trans_a=True, out_dtype=BF, tm=1024, tk=2048, deps=deps)

    def mod_bwd(duv, xv, dr, a):
        return duv * (1.0 + a[1:2, :]) + dr, _colsum(duv), _colsum(duv * xv)

    dx, d_shift, d_scale = rowwise("mod_bwd", mod_bwd, S, T_WIDE, [(du, D_MODEL, 0), (x, D_MODEL, 0), (dres, D_MODEL, 0)],
                                   [ada3], [(D_MODEL, F32)], [D_MODEL] * 2)
    d_ada = jnp.concatenate([d_shift, d_scale, d_gate], axis=1)
    return dx, G, d_ada


SMALL = ("conv_a_b", "ln_a_g", "ln_a_b", "q_norm_g", "kv_norm_g", "ln_g", "ln_b")


def _rows(v):
    n = v.shape[0]
    r = -(-n // (LANE * 16)) * 16
    return jnp.pad(v, (0, r * LANE - n)).reshape(r, LANE)


def kernel(x, c, positions, w_ada, b_ada, w_in, conv_a_w, conv_a_b, ln_a_g, ln_a_b, w_a_out, conv_b_w, w_b_out, q_norm_g, kv_norm_g, w_uq, w_ukv, w_c_out, w_o, ln_g, ln_b, loss_target, m_w_ada, m_b_ada, m_w_in, m_conv_a_w, m_conv_a_b, m_ln_a_g, m_ln_a_b, m_w_a_out, m_conv_b_w, m_w_b_out, m_q_norm_g, m_kv_norm_g, m_w_uq, m_w_ukv, m_w_c_out, m_w_o, m_ln_g, m_ln_b, v_w_ada, v_b_ada, v_w_in, v_conv_a_w, v_conv_a_b, v_ln_a_g, v_ln_a_b, v_w_a_out, v_conv_b_w, v_w_b_out, v_q_norm_g, v_kv_norm_g, v_w_uq, v_w_ukv, v_w_c_out, v_w_o, v_ln_g, v_ln_b):
    P = dict(w_ada=w_ada, b_ada=b_ada, w_in=w_in, conv_a_w=conv_a_w, conv_a_b=conv_a_b, ln_a_g=ln_a_g, ln_a_b=ln_a_b,
             w_a_out=w_a_out, conv_b_w=conv_b_w, w_b_out=w_b_out, q_norm_g=q_norm_g, kv_norm_g=kv_norm_g, w_uq=w_uq,
             w_ukv=w_ukv, w_c_out=w_c_out, w_o=w_o, ln_g=ln_g, ln_b=ln_b)
    Mo = dict(w_ada=m_w_ada, b_ada=m_b_ada, w_in=m_w_in, conv_a_w=m_conv_a_w, conv_a_b=m_conv_a_b, ln_a_g=m_ln_a_g,
              ln_a_b=m_ln_a_b, w_a_out=m_w_a_out, conv_b_w=m_conv_b_w, w_b_out=m_w_b_out, q_norm_g=m_q_norm_g,
              kv_norm_g=m_kv_norm_g, w_uq=m_w_uq, w_ukv=m_w_ukv, w_c_out=m_w_c_out, w_o=m_w_o, ln_g=m_ln_g, ln_b=m_ln_b)
    Vo = dict(w_ada=v_w_ada, b_ada=v_b_ada, w_in=v_w_in, conv_a_w=v_conv_a_w, conv_a_b=v_conv_a_b, ln_a_g=v_ln_a_g,
              ln_a_b=v_ln_a_b, w_a_out=v_w_a_out, conv_b_w=v_conv_b_w, w_b_out=v_w_b_out, q_norm_g=v_q_norm_g,
              kv_norm_g=v_kv_norm_g, w_uq=v_w_uq, w_ukv=v_w_ukv, w_c_out=v_w_c_out, w_o=v_w_o, ln_g=v_ln_g, ln_b=v_ln_b)
    ORDER = ("w_ada", "b_ada", "w_in", "conv_a_w", "conv_a_b", "ln_a_g", "ln_a_b", "w_a_out", "conv_b_w", "w_b_out",
             "q_norm_g", "kv_norm_g", "w_uq", "w_ukv", "w_c_out", "w_o", "ln_g", "ln_b")
    L = w_ada.shape[0]
    S = x.shape[1]
    me = 4 * lax.axis_index("x") + 2 * lax.axis_index("y") + lax.axis_index("c")
    x2 = x[0]
    tgt = loss_target[0]

    small_in = _rows(jnp.concatenate([c.reshape(-1), conv_a_w.reshape(-1), conv_b_w.reshape(-1)]))
    w_in_b = jnp.pad(w_in.astype(BF), ((0, 0), (0, 0), (0, W_IN_PAD - IN_SHARD)))
    misc_b = jnp.concatenate([w_a_out, w_b_out, w_c_out, w_ukv, jnp.pad(w_uq, ((0, 0), (0, 0), (0, LANE - UQ_SHARD)))],
                             axis=1).astype(BF)
    w_o_b = w_o.astype(BF)
    gathered = [None] * L
    sg = exchange("gather_small", [small_in], [])[0]
    sgf = sg.reshape(NDEV, -1)
    c_all = sgf[:, :D_MODEL]
    o1 = D_MODEL + L * 31 * 64
    conv_a_full = sgf[:, D_MODEL:o1].reshape(NDEV, L, 31, 64).transpose(1, 2, 0, 3).reshape(L, 31, CW)
    conv_b_full = sgf[:, o1:o1 + L * 3 * 64].reshape(NDEV, L, 3, 64).transpose(1, 2, 0, 3).reshape(L, 3, CW)

    c_act = rowwise("silu_c", _silu, 16, 16, [(jnp.pad(c_all, ((0, 8), (0, 0))), D_MODEL, 0)], [], [(D_MODEL, BF)])[0]
    ncol = w_ada.shape[2]
    w_ada_b = w_ada.astype(BF).transpose(1, 0, 2).reshape(D_MODEL, L * ncol)
    b_mine = lax.dynamic_slice_in_dim(b_ada, me * ncol, ncol, axis=1).reshape(1, L * ncol)
    ada_part = mm(c_act, w_ada_b, name="mm_ada", bias=b_mine)
    ada_rows = -(-(L * ncol) // (LANE * 8)) * 8
    ada_send = jnp.pad(ada_part[:NDEV].reshape(NDEV, -1, LANE), ((0, 0), (0, ada_rows - L * ncol // LANE), (0, 0)))
    ada_recv = exchange("a2a_ada", [], [ada_send])[0]
    ada = ada_recv[:, :L * ncol // LANE].reshape(NDEV, L, ncol).transpose(1, 0, 2).reshape(L, 3, D_MODEL)
    gathered[0] = [gather_two_level("gather0_w_in", w_in_b[0], ada)]
    rests = {}
    rests[0], rest_token = exchange_begin("gather0_rest", [misc_b[0], w_o_b[0]], 2, gathered[0][0])

    inv_freq = ROPE_THETA ** (-jnp.arange(0, ROPE, 2, dtype=F32) / ROPE)
    ang = positions[0].astype(F32)[:, None] * inv_freq
    tabs = (jnp.tile(jnp.cos(ang), (1, 2 * LANE // ROPE)), jnp.tile(jnp.sin(ang), (1, 2 * LANE // ROPE)))

    straight = np.arange(D_MODEL)
    fwd_in = [(0, 0, 0, 0, D_MODEL, _to_gathered(IN_PERM, IN_SHARD, W_IN_PAD))]
    fwd_misc = [(0, ROW_A, 0, 0, CW, straight), (0, ROW_B, 1, 0, CW, straight), (0, ROW_C, 2, 0, CW, straight),
                (0, ROW_UKV, 3, 0, KVL, UKV_PERM), (0, ROW_UQ, 4, 0, QL, _to_gathered(UQ_PERM, UQ_SHARD, LANE))]
    rev_in = [(0, 0, 0, 0, D_MODEL, _from_full(_inverse(IN_PERM, D_IN), IN_SHARD, W_IN_PAD))]
    rev_misc = [(0, 0, 0, ROW_A, CW, straight), (1, 0, 0, ROW_B, CW, straight), (2, 0, 0, ROW_C, CW, straight),
                (3, 0, 0, ROW_UKV, KVL, _from_full(_inverse(UKV_PERM, HEADS * (NOPE + VH)), LANE, LANE)),
                (4, 0, 0, ROW_UQ, QL, _from_full(_inverse(UQ_PERM, HEADS * (NOPE + ROPE)), UQ_SHARD, LANE))]

    def layer_weights(l, deps):
        w_in_p = col_gather("relayout_w_in", [gathered[l][0]], [(D_MODEL, NP)], fwd_in, deps)[0]

        def late(after):
            if len(gathered[l]) == 1:
                gathered[l] += exchange_end(rests[l], after)
            _, g_misc, g_o = gathered[l]
            a_out, b_out, c_out, ukv, uq = col_gather(
                "relayout_misc", [g_misc],
                [(CW, D_MODEL)] * 3 + [(KVL, HEADS * (NOPE + VH)), (QL, HEADS * (NOPE + ROPE))], fwd_misc, deps)
            return {"a_out": a_out, "b_out": b_out, "c_out": c_out, "uq": uq, "ukv": ukv,
                    "o": g_o.reshape(D_MODEL, D_MODEL)}

        return {
            "in": w_in_p, "late": late,
            "conv_a": jnp.pad(conv_a_full[l], ((0, 1), (0, 0))), "conv_b": jnp.pad(conv_b_full[l], ((0, 5), (0, 0))),
            "vec_a": jnp.stack([conv_a_b[l], ln_a_g[l], ln_a_b[l]]), "gq": q_norm_g[l][None], "gkv": kv_norm_g[l][None],
            "lnv": jnp.stack([ln_g[l], ln_b[l]]),
        }

    h = x2
    saved, weights = [], []
    handles, token, u_next = {}, rest_token, None
    for l in range(1, L):
        handles[l], token = exchange_begin("gather%d" % l, [w_in_b[l]], 1, token)
        rests[l], token = exchange_begin("gather%d_rest" % l, [misc_b[l], w_o_b[l]], 2, token)
    for l in range(L):
        ada_l, deps = (ada[l] + token[0, 0], (token,)) if l == 0 else (ada[l], ())
        h, sv, Wl, u_next = layer_fwd(h, ada_l, layer_weights(l, deps), tabs, S, u_next, ada[l + 1] if l + 1 < L else None,
                                      tgt if l + 1 == L else None)
        if l + 1 < L:
            gathered[l + 1] = exchange_end(handles[l + 1], h)
        saved.append(sv)
        weights.append(Wl)

    dy, sq = u_next
    loss = lax.psum(0.5 * jnp.sum(sq) / D_MODEL, ("x", "y", "c"))
    loss, dy = lax.optimization_barrier((loss, dy))

    grads, d_adas, recv = [None] * L, [None] * L, [None] * L
    pending, token = None, None

    def send_rest(g):
        send_misc = col_gather("unrelayout_misc", [g["w_a_out"], g["w_b_out"], g["w_c_out"], g["w_ukv"], g["w_uq"]],
                               [(NDEV, MISC_ROWS, LANE)], rev_misc)[0]
        return [send_misc, g["w_o"].reshape(NDEV, D_MODEL // NDEV, D_MODEL)]

    rest0 = []

    def early_rest(g):
        handle, tok = exchange_begin("scatter0_rest", send_rest(g), 0, g["w_o"])
        rest0.append(handle)
        return (tok,)

    for l in reversed(range(L)):
        ada_l = ada[l] if token is None else ada[l] + token[0, 0]
        dy, g, d_adas[l] = layer_bwd(dy, saved[l], ada_l, weights[l], tabs, S, early_rest if l == 0 else None)
        grads[l] = g
        if pending is not None:
            recv[l + 1] = exchange_end(pending, dy)
        send_in = col_gather("unrelayout_w_in", [g["w_in"]], [(NDEV, D_MODEL, W_IN_PAD)], rev_in)[0]
        if l == 0:
            def layer_vec(i):
                return jnp.concatenate([grads[i][n].reshape(-1) for n in SMALL] + [d_adas[i].reshape(-1)])

            def to_owners(name, taps):
                full = jnp.stack([grads[i][name][:taps] for i in range(L)])
                return full.reshape(L, taps, NDEV, CW // NDEV).transpose(2, 0, 1, 3).reshape(NDEV, -1)

            conv_send = jnp.concatenate([to_owners("conv_a_w", 31), to_owners("conv_b_w", 3)], axis=1)
            conv_rows = -(-conv_send.shape[1] // (LANE * 16)) * 16
            conv_send = jnp.pad(conv_send, ((0, 0), (0, conv_rows * LANE - conv_send.shape[1])))
            small_sizes = [int(grads[0][n].size) for n in SMALL] + [3 * D_MODEL]
            gsmall, conv_recv = exchange("gather_small_grads", [_rows(jnp.concatenate([layer_vec(i) for i in range(L)]))],
                                         [conv_send.reshape(NDEV, conv_rows, LANE)])
            pending, token = exchange_begin("scatter0", [send_in], 0, gsmall)
        else:
            pending, token = exchange_begin("scatter%d" % l, [send_in] + send_rest(g), 0,
                                            dy if l + 1 == L else recv[l + 1][0])
    grad_x = dy[None]

    gsmall = gsmall + token[0, 0]
    gsum = sum_slots("sum_small", gsmall).reshape(-1)
    recv[0] = [None] + exchange_end(rest0[0], gsum)
    Gr = {}
    offs = np.cumsum([0] + small_sizes)
    per_layer = int(offs[-1])
    gsum = gsum[:L * per_layer].reshape(L, per_layer)
    for i, n in enumerate(SMALL):
        Gr[n] = gsum[:, offs[i]:offs[i + 1]]
    csum = sum_slots("sum_conv", conv_recv + token[0, 0]).reshape(-1)
    n_a = L * 31 * (CW // NDEV)
    Gr["conv_a_w"] = csum[:n_a].reshape(L, 31, CW // NDEV)
    Gr["conv_b_w"] = csum[n_a:n_a + L * 3 * (CW // NDEV)].reshape(L, 3, CW // NDEV)
    Gr["b_ada"] = gsum[:, offs[7]:offs[8]]
    d_ada_all = gsmall.reshape(NDEV, -1)[:, :L * per_layer].reshape(NDEV, L, per_layer)[:, :, offs[7]:offs[8]]
    d_mine = lax.dynamic_slice_in_dim(d_ada_all, me * ncol, ncol, axis=2).reshape(NDEV, L * ncol)
    g_ada = mm(c_act, jnp.pad(d_mine, ((0, 8), (0, 0))).astype(BF), name="mm_gw_ada", trans_a=True)
    Gr["w_ada"] = g_ada.reshape(D_MODEL, L, ncol).transpose(1, 0, 2)

    D, NM, NV = {}, {}, {}
    D["w_ada"], NM["w_ada"], NV["w_ada"] = adamw("adamw_w_ada", P["w_ada"], Gr["w_ada"], Mo["w_ada"], Vo["w_ada"])
    Gr["w_o"], D["w_o"], NM["w_o"], NV["w_o"] = sum_adamw(
        "sum_adamw_w_o", [recv[l][2] for l in range(L)], P["w_o"], Mo["w_o"], Vo["w_o"])
    for n, row0 in (("w_a_out", ROW_A), ("w_b_out", ROW_B), ("w_c_out", ROW_C), ("w_ukv", ROW_UKV), ("w_uq", ROW_UQ)):
        Gr[n], D[n], NM[n], NV[n] = sum_adamw("sum_adamw_" + n, [recv[l][1] for l in range(L)], P[n], Mo[n], Vo[n],
                                              row0=row0)
    w_l, m_l, v_l, _ = lax.optimization_barrier((P["w_in"], Mo["w_in"], Vo["w_in"], token))
    upper = sum_adamw("sum_adamw_w_in_upper", [recv[l][0] for l in range(1, L)], w_l, m_l, v_l, lo=1)
    recv[0][0] = exchange_end(pending, upper[1])[0]
    Gr["w_in"], D["w_in"], NM["w_in"], NV["w_in"] = sum_adamw(
        "sum_adamw_w_in", [recv[0][0]], w_l, m_l, v_l, lo=0, prev=upper)
    for n in ("b_ada", "conv_a_w", "conv_b_w") + SMALL:
        shape = P[n].shape if P[n].ndim == 3 else (1,) + P[n].shape
        res = adamw("adamw_" + n, *[t.reshape(shape) for t in (P[n], Gr[n], Mo[n], Vo[n])])
        D[n], NM[n], NV[n] = [t.reshape(P[n].shape) for t in res]
    return (loss, grad_x, *[Gr[n] for n in ORDER], *[D[n] for n in ORDER], *[NM[n] for n in ORDER],
            *[NV[n] for n in ORDER])
```
